```python
import jax, jax.numpy as jnp
from jax import lax
import numpy as np

D_MODEL = 1024
BATCH = 8
SEQ = 2048
DEPTH = 2

CHUNK = 64
PLE_DIM = 256
N_A = DEPTH // 2
N_B = DEPTH - N_A
POOL_WINDOWS = (2, 4, 8, 16)
N_POOL_GROUPS = len(POOL_WINDOWS)
POOL_GROUP_DIM = D_MODEL // N_POOL_GROUPS
POOL_WMAX = max(POOL_WINDOWS)
SB_HEADS = 16
SB_HEAD_DIM = D_MODEL // SB_HEADS
SB_SCALE = SB_HEAD_DIM ** -0.5
Q_BLOCK = 128
EPS = 1e-6

kernel_name = "yoco_pool_stickbreaking_hybrid"


def rms_norm(x, g):
    xf = x.astype(jnp.float32)
    y = xf * lax.rsqrt(jnp.mean(xf * xf, axis=-1, keepdims=True) + EPS)
    return (y * g.astype(jnp.float32)).astype(x.dtype)


def pool_mixer(h, w_in, w_group, scale, w_out):
    B, S, _ = h.shape
    u, z = jnp.split(h @ w_in, 2, axis=-1)
    u = u.reshape(B, S, N_POOL_GROUPS, POOL_GROUP_DIM)
    uf = u.astype(jnp.float32)
    cs = jnp.cumsum(uf, axis=1)
    cs_ext = jnp.pad(cs, ((0, 0), (POOL_WMAX, 0), (0, 0), (0, 0)))
    pos1 = jnp.arange(1, S + 1)
    means = []
    for g, w in enumerate(POOL_WINDOWS):
        lo = cs_ext[:, POOL_WMAX - w: POOL_WMAX - w + S, g]
        cnt = jnp.minimum(pos1, w).astype(jnp.float32)[None, :, None]
        means.append((cs[:, :, g] - lo) / cnt)
    pooled = (jnp.stack(means, axis=2) - uf).astype(h.dtype)
    mixed = jnp.einsum('bsgc,gcd->bsgd', pooled, w_group).reshape(B, S, D_MODEL) * scale
    return (mixed * jax.nn.silu(z)) @ w_out


def split_heads(t):
    B, S, _ = t.shape
    return t.reshape(B, S, SB_HEADS, SB_HEAD_DIM).transpose(0, 2, 1, 3)


def stick_breaking_attention(q, k, v):
    S = q.shape[2]
    outs = []
    for blk in range(S // Q_BLOCK):
        t0 = blk * Q_BLOCK
        L = t0 + Q_BLOCK
        logits = jnp.einsum('bhtd,bhsd->bhts', q[:, :, t0:L], k[:, :, :L]).astype(jnp.float32) * SB_SCALE
        t_idx = t0 + jnp.arange(Q_BLOCK)[:, None]
        s_idx = jnp.arange(L)[None, :]
        mask = s_idx < t_idx
        log_keep = jnp.where(mask, jax.nn.log_sigmoid(-logits), 0.0)
        later = lax.cumsum(log_keep, axis=log_keep.ndim - 1, reverse=True) - log_keep
        log_a = jax.nn.log_sigmoid(logits) + later
        a = jnp.where(mask, jnp.exp(log_a), 0.0)
        outs.append(jnp.einsum('bhts,bhsd->bhtd', a.astype(v.dtype), v[:, :, :L]))
    return jnp.concatenate(outs, axis=2)


def _fwd_setup_inputs(seed: int = 0) -> dict:
    key = jax.random.key(seed)
    ks = jax.random.split(key, 20)
    D, C = D_MODEL, POOL_GROUP_DIM
    f32 = jnp.float32

    def nrm(k, shape, fan_in, gain=1.0):
        return jax.random.normal(k, shape, f32) * (gain * fan_in ** -0.5)

    def gain(k, shape):
        return 1.0 + 0.05 * jax.random.normal(k, shape, f32)

    return {
        "x": jax.random.normal(ks[0], (BATCH, SEQ, D), f32),
        "p": jax.random.normal(ks[1], (DEPTH, BATCH, SEQ, PLE_DIM), f32),
        "a_norm": gain(ks[2], (N_A, D)),
        "a_w_in": nrm(ks[3], (N_A, D, 2 * D), D),
        "a_w_group": nrm(ks[4], (N_A, N_POOL_GROUPS, C, C), C),
        "a_scale": gain(ks[5], (N_A, D)),
        "a_w_out": nrm(ks[6], (N_A, D, D), D, 0.5),
        "kv_norm": gain(ks[7], (D,)),
        "w_kv": nrm(ks[8], (D, 2 * D), D),
        "k_norm": gain(ks[9], (SB_HEAD_DIM,)),
        "b_norm": gain(ks[10], (N_B, D)),
        "b_w_in": nrm(ks[11], (N_B, D, 2 * D), D),
        "b_q_norm": gain(ks[12], (N_B, SB_HEAD_DIM)),
        "b_w_out": nrm(ks[13], (N_B, D, D), D, 0.5),
        "ple_w": nrm(ks[14], (DEPTH, PLE_DIM, D), PLE_DIM, 0.5),
        "ple_gate_w": nrm(ks[15], (DEPTH, D, D), D),
    }


def _fwd_reference(x, p, a_norm, a_w_in, a_w_group, a_scale, a_w_out, kv_norm, w_kv, k_norm,
              b_norm, b_w_in, b_q_norm, b_w_out, ple_w, ple_gate_w):
    B, S, _ = x.shape
    k_sh = v_sh = None
    for i in range(DEPTH):
        if i < N_A:
            h = rms_norm(x, a_norm[i])
            x = x + pool_mixer(h, a_w_in[i], a_w_group[i], a_scale[i], a_w_out[i])
        else:
            j = i - N_A
            if j == 0:
                kv_in = rms_norm(x, kv_norm)
                k_all, v_all = jnp.split(kv_in @ w_kv, 2, axis=-1)
                k_sh = rms_norm(split_heads(k_all), k_norm)
                v_sh = split_heads(v_all)
            h = rms_norm(x, b_norm[j])
            q, z = jnp.split(h @ b_w_in[j], 2, axis=-1)
            q = rms_norm(split_heads(q), b_q_norm[j])
            o = stick_breaking_attention(q, k_sh, v_sh)
            o = o.transpose(0, 2, 1, 3).reshape(B, S, D_MODEL)
            x = x + (o * jax.nn.silu(z)) @ b_w_out[j]
        x = x + (p[i] @ ple_w[i]) * jax.nn.sigmoid(x @ ple_gate_w[i])
    return x


import jax as _jax
import jax.numpy as _jnp

TWIN_FORMAT = 'train_step'
FWD_PARAMS = ['x', 'p', 'a_norm', 'a_w_in', 'a_w_group', 'a_scale', 'a_w_out', 'kv_norm', 'w_kv', 'k_norm', 'b_norm', 'b_w_in', 'b_q_norm', 'b_w_out', 'ple_w', 'ple_gate_w']
TWIN_WEIGHTS = ['a_norm', 'a_w_in', 'a_w_group', 'a_scale', 'a_w_out', 'kv_norm', 'w_kv', 'k_norm', 'b_norm', 'b_w_in', 'b_q_norm', 'b_w_out', 'ple_w', 'ple_gate_w']
TWIN_DIFF_INPUT = 'x'
TWIN_INPUTS = ['x', 'p', 'a_norm', 'a_w_in', 'a_w_group', 'a_scale', 'a_w_out', 'kv_norm', 'w_kv', 'k_norm', 'b_norm', 'b_w_in', 'b_q_norm', 'b_w_out', 'ple_w', 'ple_gate_w', 'loss_target', 'm_a_norm', 'm_a_w_in', 'm_a_w_group', 'm_a_scale', 'm_a_w_out', 'm_kv_norm', 'm_w_kv', 'm_k_norm', 'm_b_norm', 'm_b_w_in', 'm_b_q_norm', 'm_b_w_out', 'm_ple_w', 'm_ple_gate_w', 'v_a_norm', 'v_a_w_in', 'v_a_w_group', 'v_a_scale', 'v_a_w_out', 'v_kv_norm', 'v_w_kv', 'v_k_norm', 'v_b_norm', 'v_b_w_in', 'v_b_q_norm', 'v_b_w_out', 'v_ple_w', 'v_ple_gate_w']
TWIN_OUTPUTS = ['loss', 'grad_x', 'grad_a_norm', 'grad_a_w_in', 'grad_a_w_group', 'grad_a_scale', 'grad_a_w_out', 'grad_kv_norm', 'grad_w_kv', 'grad_k_norm', 'grad_b_norm', 'grad_b_w_in', 'grad_b_q_norm', 'grad_b_w_out', 'grad_ple_w', 'grad_ple_gate_w', 'delta_a_norm', 'delta_a_w_in', 'delta_a_w_group', 'delta_a_scale', 'delta_a_w_out', 'delta_kv_norm', 'delta_w_kv', 'delta_k_norm', 'delta_b_norm', 'delta_b_w_in', 'delta_b_q_norm', 'delta_b_w_out', 'delta_ple_w', 'delta_ple_gate_w', 'new_m_a_norm', 'new_m_a_w_in', 'new_m_a_w_group', 'new_m_a_scale', 'new_m_a_w_out', 'new_m_kv_norm', 'new_m_w_kv', 'new_m_k_norm', 'new_m_b_norm', 'new_m_b_w_in', 'new_m_b_q_norm', 'new_m_b_w_out', 'new_m_ple_w', 'new_m_ple_gate_w', 'new_v_a_norm', 'new_v_a_w_in', 'new_v_a_w_group', 'new_v_a_scale', 'new_v_a_w_out', 'new_v_kv_norm', 'new_v_w_kv', 'new_v_k_norm', 'new_v_b_norm', 'new_v_b_w_in', 'new_v_b_q_norm', 'new_v_b_w_out', 'new_v_ple_w', 'new_v_ple_gate_w']
TWIN_LEAF_KINDS = {'loss': 'loss', 'grad_x': 'grad_x', 'grad_a_norm': 'grad_w', 'grad_a_w_in': 'grad_w', 'grad_a_w_group': 'grad_w', 'grad_a_scale': 'grad_w', 'grad_a_w_out': 'grad_w', 'grad_kv_norm': 'grad_w', 'grad_w_kv': 'grad_w', 'grad_k_norm': 'grad_w', 'grad_b_norm': 'grad_w', 'grad_b_w_in': 'grad_w', 'grad_b_q_norm': 'grad_w', 'grad_b_w_out': 'grad_w', 'grad_ple_w': 'grad_w', 'grad_ple_gate_w': 'grad_w', 'delta_a_norm': 'delta_w', 'delta_a_w_in': 'delta_w', 'delta_a_w_group': 'delta_w', 'delta_a_scale': 'delta_w', 'delta_a_w_out': 'delta_w', 'delta_kv_norm': 'delta_w', 'delta_w_kv': 'delta_w', 'delta_k_norm': 'delta_w', 'delta_b_norm': 'delta_w', 'delta_b_w_in': 'delta_w', 'delta_b_q_norm': 'delta_w', 'delta_b_w_out': 'delta_w', 'delta_ple_w': 'delta_w', 'delta_ple_gate_w': 'delta_w', 'new_m_a_norm': 'new_m', 'new_m_a_w_in': 'new_m', 'new_m_a_w_group': 'new_m', 'new_m_a_scale': 'new_m', 'new_m_a_w_out': 'new_m', 'new_m_kv_norm': 'new_m', 'new_m_w_kv': 'new_m', 'new_m_k_norm': 'new_m', 'new_m_b_norm': 'new_m', 'new_m_b_w_in': 'new_m', 'new_m_b_q_norm': 'new_m', 'new_m_b_w_out': 'new_m', 'new_m_ple_w': 'new_m', 'new_m_ple_gate_w': 'new_m', 'new_v_a_norm': 'new_v', 'new_v_a_w_in': 'new_v', 'new_v_a_w_group': 'new_v', 'new_v_a_scale': 'new_v', 'new_v_a_w_out': 'new_v', 'new_v_kv_norm': 'new_v', 'new_v_w_kv': 'new_v', 'new_v_k_norm': 'new_v', 'new_v_b_norm': 'new_v', 'new_v_b_w_in': 'new_v', 'new_v_b_q_norm': 'new_v', 'new_v_b_w_out': 'new_v', 'new_v_ple_w': 'new_v', 'new_v_ple_gate_w': 'new_v'}


def _forward(args):
    return _fwd_reference(*[args[k] for k in FWD_PARAMS])


def _output_shape():
    out = _jax.eval_shape(lambda: _forward(_fwd_setup_inputs(0)))
    return out.shape, out.dtype

N_MICROBATCH = 1
ADAM_LR = 0.001
ADAM_B1 = 0.9
ADAM_B2 = 0.999
ADAM_EPS = 1e-08
ADAM_WD = 0.01
ADAM_STEP = 10
PER_EXAMPLE_BATCH_AXIS = {'x': 0, 'p': 1, 'loss_target': 0}
SHARED_INPUTS = []
_WEIGHT_DTYPES = {'a_norm': _jnp.float32, 'a_w_in': _jnp.float32, 'a_w_group': _jnp.float32, 'a_scale': _jnp.float32, 'a_w_out': _jnp.float32, 'kv_norm': _jnp.float32, 'w_kv': _jnp.float32, 'k_norm': _jnp.float32, 'b_norm': _jnp.float32, 'b_w_in': _jnp.float32, 'b_q_norm': _jnp.float32, 'b_w_out': _jnp.float32, 'ple_w': _jnp.float32, 'ple_gate_w': _jnp.float32}
MOMENT_SCALE = {'a_norm': 2.408962e+00, 'a_w_in': 7.055383e-02, 'a_w_group': 8.737089e-02, 'a_scale': 1.135547e+00, 'a_w_out': 1.298346e-01, 'kv_norm': 5.932345e-01, 'w_kv': 3.279335e-02, 'k_norm': 1.472591e+00, 'b_norm': 6.967796e-01, 'b_w_in': 3.400398e-02, 'b_q_norm': 1.462253e+00, 'b_w_out': 7.750991e-02, 'ple_w': 1.501454e-01, 'ple_gate_w': 2.224338e-02}


def _to_microbatches(a, axis):
    t = _jnp.moveaxis(a, axis, 0)
    t = t.reshape((N_MICROBATCH, t.shape[0] // N_MICROBATCH) + t.shape[1:])
    return _jnp.moveaxis(t, 1, axis + 1)


def setup_inputs(seed: int = 0) -> dict:
    inp = _fwd_setup_inputs(seed)
    key = _jax.random.fold_in(_jax.random.key(seed), 7919)
    shape, _ = _output_shape()
    out = dict(inp)
    out["loss_target"] = _jax.random.normal(_jax.random.fold_in(key, 0), shape, _jnp.float32)
    for i, name in enumerate(TWIN_WEIGHTS):
        w = inp[name].astype(_jnp.float32)
        if MOMENT_SCALE is None:
            s = _jnp.sqrt(_jnp.mean(_jnp.square(w)) + 1e-30)
        else:
            s = MOMENT_SCALE[name]
        km, kv = _jax.random.split(_jax.random.fold_in(key, i + 1))
        out[name] = w
        out["m_" + name] = s * _jax.random.normal(km, w.shape, _jnp.float32)
        out["v_" + name] = (s * s) * _jax.random.uniform(kv, w.shape, _jnp.float32, 0.5, 1.5)
    if N_MICROBATCH > 1:
        for name, axis in PER_EXAMPLE_BATCH_AXIS.items():
            out[name] = _to_microbatches(out[name], axis)
    return {'x': out['x'], 'p': out['p'], 'a_norm': out['a_norm'], 'a_w_in': out['a_w_in'], 'a_w_group': out['a_w_group'], 'a_scale': out['a_scale'], 'a_w_out': out['a_w_out'], 'kv_norm': out['kv_norm'], 'w_kv': out['w_kv'], 'k_norm': out['k_norm'], 'b_norm': out['b_norm'], 'b_w_in': out['b_w_in'], 'b_q_norm': out['b_q_norm'], 'b_w_out': out['b_w_out'], 'ple_w': out['ple_w'], 'ple_gate_w': out['ple_gate_w'], 'loss_target': out['loss_target'], 'm_a_norm': out['m_a_norm'], 'm_a_w_in': out['m_a_w_in'], 'm_a_w_group': out['m_a_w_group'], 'm_a_scale': out['m_a_scale'], 'm_a_w_out': out['m_a_w_out'], 'm_kv_norm': out['m_kv_norm'], 'm_w_kv': out['m_w_kv'], 'm_k_norm': out['m_k_norm'], 'm_b_norm': out['m_b_norm'], 'm_b_w_in': out['m_b_w_in'], 'm_b_q_norm': out['m_b_q_norm'], 'm_b_w_out': out['m_b_w_out'], 'm_ple_w': out['m_ple_w'], 'm_ple_gate_w': out['m_ple_gate_w'], 'v_a_norm': out['v_a_norm'], 'v_a_w_in': out['v_a_w_in'], 'v_a_w_group': out['v_a_w_group'], 'v_a_scale': out['v_a_scale'], 'v_a_w_out': out['v_a_w_out'], 'v_kv_norm': out['v_kv_norm'], 'v_w_kv': out['v_w_kv'], 'v_k_norm': out['v_k_norm'], 'v_b_norm': out['v_b_norm'], 'v_b_w_in': out['v_b_w_in'], 'v_b_q_norm': out['v_b_q_norm'], 'v_b_w_out': out['v_b_w_out'], 'v_ple_w': out['v_ple_w'], 'v_ple_gate_w': out['v_ple_gate_w']}


def _loss(weights, diff, rest, loss_target):
    with _jax.named_scope("forward"):
        args = {**rest, TWIN_DIFF_INPUT: diff, **{k: w.astype(_WEIGHT_DTYPES[k]) for k, w in weights.items()}}
        y = _forward(args)
    with _jax.named_scope("loss_head"):
        err = _jnp.square(y.astype(_jnp.float32) - loss_target)
        return 0.5 * _jnp.sum(_jnp.mean(err, axis=-1)) if err.ndim else 0.5 * err


def _adamw(w, g, m, v):
    m = ADAM_B1 * m + (1.0 - ADAM_B1) * g
    v = ADAM_B2 * v + (1.0 - ADAM_B2) * _jnp.square(g)
    m_hat = m / (1.0 - ADAM_B1 ** ADAM_STEP)
    v_hat = v / (1.0 - ADAM_B2 ** ADAM_STEP)
    delta = -ADAM_LR * (m_hat / (_jnp.sqrt(v_hat) + ADAM_EPS) + ADAM_WD * w)
    return delta, m, v


def reference(x, p, a_norm, a_w_in, a_w_group, a_scale, a_w_out, kv_norm, w_kv, k_norm, b_norm, b_w_in, b_q_norm, b_w_out, ple_w, ple_gate_w, loss_target, m_a_norm, m_a_w_in, m_a_w_group, m_a_scale, m_a_w_out, m_kv_norm, m_w_kv, m_k_norm, m_b_norm, m_b_w_in, m_b_q_norm, m_b_w_out, m_ple_w, m_ple_gate_w, v_a_norm, v_a_w_in, v_a_w_group, v_a_scale, v_a_w_out, v_kv_norm, v_w_kv, v_k_norm, v_b_norm, v_b_w_in, v_b_q_norm, v_b_w_out, v_ple_w, v_ple_gate_w):
    given = dict(x=x, p=p, a_norm=a_norm, a_w_in=a_w_in, a_w_group=a_w_group, a_scale=a_scale, a_w_out=a_w_out, kv_norm=kv_norm, w_kv=w_kv, k_norm=k_norm, b_norm=b_norm, b_w_in=b_w_in, b_q_norm=b_q_norm, b_w_out=b_w_out, ple_w=ple_w, ple_gate_w=ple_gate_w, loss_target=loss_target, m_a_norm=m_a_norm, m_a_w_in=m_a_w_in, m_a_w_group=m_a_w_group, m_a_scale=m_a_scale, m_a_w_out=m_a_w_out, m_kv_norm=m_kv_norm, m_w_kv=m_w_kv, m_k_norm=m_k_norm, m_b_norm=m_b_norm, m_b_w_in=m_b_w_in, m_b_q_norm=m_b_q_norm, m_b_w_out=m_b_w_out, m_ple_w=m_ple_w, m_ple_gate_w=m_ple_gate_w, v_a_norm=v_a_norm, v_a_w_in=v_a_w_in, v_a_w_group=v_a_w_group, v_a_scale=v_a_scale, v_a_w_out=v_a_w_out, v_kv_norm=v_kv_norm, v_w_kv=v_w_kv, v_k_norm=v_k_norm, v_b_norm=v_b_norm, v_b_w_in=v_b_w_in, v_b_q_norm=v_b_q_norm, v_b_w_out=v_b_w_out, v_ple_w=v_ple_w, v_ple_gate_w=v_ple_gate_w)
    weights = {n: given[n] for n in TWIN_WEIGHTS}
    shared = {n: given[n] for n in SHARED_INPUTS}
    per_example = {n: given[n] for n in ['x', 'p']}
    grad_fn = _jax.value_and_grad(_loss, argnums=(0, 1))

    def one_microbatch(ex, loss_target):
        ex = dict(ex)
        diff = ex.pop(TWIN_DIFF_INPUT)
        return grad_fn(weights, diff, {**shared, **ex}, loss_target)

    if N_MICROBATCH == 1:
        loss, (grad_w, grad_x) = one_microbatch(per_example, given["loss_target"])
    else:
        def body(carry, xs):
            loss_sum, grad_sum = carry
            l_k, (gw_k, gx_k) = one_microbatch(xs[0], xs[1])
            with _jax.named_scope("update"):
                return (loss_sum + l_k, _jax.tree.map(_jnp.add, grad_sum, gw_k)), gx_k

        init = (_jnp.zeros((), _jnp.float32), _jax.tree.map(_jnp.zeros_like, weights))
        (loss, grad_w), grad_x = _jax.lax.scan(body, init, (per_example, given["loss_target"]))
    with _jax.named_scope("update"):
        delta_w, new_m, new_v = {}, {}, {}
        for n in TWIN_WEIGHTS:
            delta_w[n], new_m[n], new_v[n] = _adamw(weights[n], grad_w[n], given["m_" + n], given["v_" + n])
    return (loss, grad_x, *[grad_w[n] for n in TWIN_WEIGHTS], *[delta_w[n] for n in TWIN_WEIGHTS],
            *[new_m[n] for n in TWIN_WEIGHTS], *[new_v[n] for n in TWIN_WEIGHTS])
```

```python
import functools

import jax
import jax.numpy as jnp
from jax import lax
from jax.experimental import pallas as pl
from jax.experimental.pallas import tpu as pltpu

F32 = jnp.float32
BF16 = jnp.bfloat16
MESH = pl.DeviceIdType.MESH

D_MODEL = 1024
N_HEADS = 16
HEAD_DIM = 64
PLE_DIM = 256
N_GROUPS = 4
GROUP_DIM = 256
POOL_WINDOWS = (2, 4, 8, 16)
N_CHIPS = 4
EPS = 1e-6
SB_SCALE = HEAD_DIM ** -0.5

ADAM_LR = 0.001
ADAM_B1 = 0.9
ADAM_B2 = 0.999
ADAM_EPS = 1e-08
ADAM_WD = 0.01
ADAM_STEP = 10

ROW_TILE = 256
ATT_TILE = 256
WGRAD_SEQ_TILE = 512
MIB = 1024 * 1024


def _params(semantics=None, vmem_mib=48):
    return pltpu.CompilerParams(dimension_semantics=semantics, vmem_limit_bytes=vmem_mib * MIB)


def _dot(a, b):
    return jnp.dot(a, b, preferred_element_type=F32)


def _dot_nt(a, b):
    return lax.dot_general(a, b, (((1,), (1,)), ((), ())), preferred_element_type=F32)


def _dot_tn(a, b):
    return lax.dot_general(a, b, (((0,), (0,)), ((), ())), preferred_element_type=F32)


def _hilo(x):
    hi = x.astype(BF16)
    lo = (x - hi.astype(F32)).astype(BF16)
    return hi, lo


def _dot_hilo(x, w):
    hi, lo = _hilo(x)
    return _dot(hi, w) + _dot(lo, w)


def _sigmoid(z):
    return jax.nn.sigmoid(z)


def _dsilu(z, sg):
    return sg * (1.0 + z * (1.0 - sg))


def _mask_bf16(cond):
    return jnp.where(cond, 1.0, 0.0).astype(BF16)


def _head_mean_matrix():
    r = lax.broadcasted_iota(jnp.int32, (256, 256), 0) // HEAD_DIM
    c = lax.broadcasted_iota(jnp.int32, (256, 256), 1) // HEAD_DIM
    return _mask_bf16(r == c)


def _head_mean(x, bd):
    parts = []
    for s in range(x.shape[1] // 256):
        parts.append(_dot_hilo(x[:, s * 256:(s + 1) * 256], bd))
    out = parts[0] if len(parts) == 1 else jnp.concatenate(parts, axis=1)
    return out * (1.0 / HEAD_DIM)


def _a_in(x, gain, w_sh):
    S = x.shape[0]
    tm = 512
    nsh, _, wn = w_sh.shape

    def body(x_ref, g_ref, w_ref, uz_ref, h_ref):
        @pl.when(pl.program_id(1) == 0)
        def _():
            xv = x_ref[...]
            r = lax.rsqrt(jnp.mean(xv * xv, axis=-1, keepdims=True) + EPS)
            h_ref[...] = (xv * r * g_ref[...]).astype(BF16)

        uz_ref[...] = _dot(h_ref[...], w_ref[0])

    return pl.pallas_call(
        body, name="a_in", grid=(S // tm, nsh),
        in_specs=[pl.BlockSpec((tm, D_MODEL), lambda i, j: (i, 0)),
                  pl.BlockSpec((1, D_MODEL), lambda i, j: (0, 0)),
                  pl.BlockSpec((1, D_MODEL, wn), lambda i, j: (j, 0, 0))],
        out_specs=[pl.BlockSpec((tm, wn), lambda i, j: (i, j)),
                   pl.BlockSpec((tm, D_MODEL), lambda i, j: (i, 0))],
        out_shape=[jax.ShapeDtypeStruct((S, nsh * wn), F32),
                   jax.ShapeDtypeStruct((S, D_MODEL), BF16)],
        compiler_params=_params(("parallel", "arbitrary")),
    )(x, gain, w_sh)


def _inv_count(first_row, rows, w):
    t1 = first_row + 1 + lax.broadcasted_iota(jnp.int32, (rows, 1), 0)
    return 1.0 / jnp.minimum(t1, w).astype(F32)


def _group_weight(wg_ref, g):
    return jnp.concatenate([wg_ref[sh, g] for sh in range(N_CHIPS)], axis=0)


def _a_mix(uz, wg, scale):
    S = uz.shape[0]
    tm = ROW_TILE

    def body(u_ref, up_ref, z_ref, wg_ref, sc_ref, ga_ref, p_ref):
        i = pl.program_id(0)
        row = lax.broadcasted_iota(jnp.int32, (tm, tm), 0)
        col = lax.broadcasted_iota(jnp.int32, (tm, tm), 1)
        d = row - col
        for g, w in enumerate(POOL_WINDOWS):
            cols = slice(g * GROUP_DIM, (g + 1) * GROUP_DIM)
            t_main = _mask_bf16((d >= 0) & (d < w))
            t_halo = _mask_bf16(d + tm < w)
            u = u_ref[:, cols]
            up = jnp.where(i > 0, up_ref[:, cols], 0.0)
            hi, lo = _hilo(u)
            hip, lop = _hilo(up)
            wsum = _dot(t_main, hi) + _dot(t_main, lo) + _dot(t_halo, hip) + _dot(t_halo, lop)
            pooled = (wsum * _inv_count(i * tm, tm, w) - u).astype(BF16)
            p_ref[:, cols] = pooled
            mraw = _dot(pooled, _group_weight(wg_ref, g))
            z = z_ref[:, cols]
            ga_ref[:, cols] = (mraw * sc_ref[:, cols] * (z * _sigmoid(z))).astype(BF16)

    return pl.pallas_call(
        body, name="a_mix", grid=(S // tm,),
        in_specs=[pl.BlockSpec((tm, D_MODEL), lambda i: (i, 0)),
                  pl.BlockSpec((tm, D_MODEL), lambda i: (jnp.maximum(i - 1, 0), 0)),
                  pl.BlockSpec((tm, D_MODEL), lambda i: (i, 1)),
                  pl.BlockSpec((N_CHIPS, N_GROUPS, 64, GROUP_DIM), lambda i: (0, 0, 0, 0)),
                  pl.BlockSpec((1, D_MODEL), lambda i: (0, 0))],
        out_specs=[pl.BlockSpec((tm, D_MODEL), lambda i: (i, 0)),
                   pl.BlockSpec((tm, D_MODEL), lambda i: (i, 0))],
        out_shape=[jax.ShapeDtypeStruct((S, D_MODEL), BF16),
                   jax.ShapeDtypeStruct((S, D_MODEL), BF16)],
        compiler_params=_params(("arbitrary",)),
    )(uz, uz, uz, wg, scale)


def _out_ple(name, gated, x_in, w_out, p, layer, ple_w, ple_g, target=None):
    S = x_in.shape[0]
    tm = ROW_TILE
    with_loss = target is not None

    def body(*refs):
        if with_loss:
            g_ref, x_ref, wo_ref, p_ref, pw_ref, pg_ref, t_ref, xm_ref, dx_ref, e_ref, gt_ref, loss_ref = refs
        else:
            g_ref, x_ref, wo_ref, p_ref, pw_ref, pg_ref, xm_ref, xo_ref, e_ref, gt_ref = refs
        xm = x_ref[...] + _dot(g_ref[...], wo_ref[...])
        xm_ref[...] = xm
        pb = p_ref[...].astype(BF16)
        e = jnp.concatenate([_dot(pb, pw_ref[sh]) for sh in range(N_CHIPS)], axis=1)
        pg = jnp.concatenate([pg_ref[sh] for sh in range(N_CHIPS)], axis=0)
        gate = _sigmoid(_dot(xm.astype(BF16), pg))
        e_ref[...] = e.astype(BF16)
        gt_ref[...] = gate.astype(BF16)
        xo = xm + e * gate
        if with_loss:
            diff = xo - t_ref[...]
            dx_ref[...] = diff * (1.0 / D_MODEL)

            @pl.when(pl.program_id(0) == 0)
            def _():
                loss_ref[...] = jnp.zeros_like(loss_ref)

            loss_ref[...] += jnp.sum(diff * diff) * (0.5 / D_MODEL)
        else:
            xo_ref[...] = xo

    row = pl.BlockSpec((tm, D_MODEL), lambda i: (i, 0))
    in_specs = [row, row,
                pl.BlockSpec((D_MODEL, D_MODEL), lambda i: (0, 0)),
                pl.BlockSpec((None, None, tm, PLE_DIM), lambda i: (layer, 0, i, 0)),
                pl.BlockSpec((N_CHIPS, None, PLE_DIM, 256), lambda i: (0, layer, 0, 0)),
                pl.BlockSpec((N_CHIPS, None, 256, D_MODEL), lambda i: (0, layer, 0, 0))]
    args = [gated, x_in, w_out, p, ple_w, ple_g]
    out_specs = [row, row, row, row]
    out_shape = [jax.ShapeDtypeStruct((S, D_MODEL), F32), jax.ShapeDtypeStruct((S, D_MODEL), F32),
                 jax.ShapeDtypeStruct((S, D_MODEL), BF16), jax.ShapeDtypeStruct((S, D_MODEL), BF16)]
    if with_loss:
        in_specs.append(row)
        args.append(target)
        out_specs.append(pl.BlockSpec((8, 128), lambda i: (0, 0)))
        out_shape.append(jax.ShapeDtypeStruct((8, 128), F32))
    return pl.pallas_call(
        body, name=name, grid=(S // tm,), in_specs=in_specs, out_specs=out_specs, out_shape=out_shape,
        compiler_params=_params(("arbitrary",)),
    )(*args)


def _b_in(x, kv_gain, b_gain, k_gain_t, q_gain_t, w_kv, w_in):
    S = x.shape[0]
    tm = ROW_TILE

    def body(x_ref, kvg_ref, bg_ref, kg_ref, qg_ref, wkv_ref, win_ref,
             hkv_ref, hb_ref, kraw_ref, qraw_ref, k_ref, q_ref, v_ref, z_ref):
        xv = x_ref[...]
        y = xv * lax.rsqrt(jnp.mean(xv * xv, axis=-1, keepdims=True) + EPS)
        hkv = (y * kvg_ref[...]).astype(BF16)
        hb = (y * bg_ref[...]).astype(BF16)
        hkv_ref[...] = hkv
        hb_ref[...] = hb
        bd = _head_mean_matrix()

        def head_norm(raw, gain):
            rr = lax.rsqrt(_head_mean(raw * raw, bd) + EPS)
            return raw * rr * gain

        for sh in range(N_CHIPS):
            kvc = _dot(hkv, wkv_ref[sh])
            qzc = _dot(hb, win_ref[sh])
            cols = slice((sh % 2) * 512, (sh % 2) * 512 + 512)
            if sh < 2:
                kraw_ref[:, cols] = kvc.astype(BF16)
                qraw_ref[:, cols] = qzc.astype(BF16)
                k_ref[:, cols] = head_norm(kvc, kg_ref[:, cols]).astype(BF16)
                q_ref[:, cols] = (head_norm(qzc, qg_ref[:, cols]) * SB_SCALE).astype(BF16)
            else:
                v_ref[:, cols] = kvc.astype(BF16)
                z_ref[:, cols] = qzc.astype(BF16)

    row = pl.BlockSpec((tm, D_MODEL), lambda i: (i, 0))
    vec = pl.BlockSpec((1, D_MODEL), lambda i: (0, 0))
    wsp = pl.BlockSpec((N_CHIPS, D_MODEL, 512), lambda i: (0, 0, 0))
    return pl.pallas_call(
        body, name="b_in", grid=(S // tm,),
        in_specs=[row, vec, vec, vec, vec, wsp, wsp],
        out_specs=[row] * 8,
        out_shape=[jax.ShapeDtypeStruct((S, D_MODEL), BF16)] * 8,
        compiler_params=_params(("arbitrary",), 56),
    )(x, kv_gain, b_gain, k_gain_t, q_gain_t, w_kv, w_in)


def _softplus_parts(z):
    e = jnp.exp(-jnp.abs(z))
    return -(jnp.maximum(z, 0.0) + jnp.log(1.0 + e)), e


def _attn_fwd(q, k, v, zgate):
    S = q.shape[0]
    t = ATT_TILE

    def body(q_ref, k_ref, v_ref, z_ref, o_ref, g_ref, lt_ref):
        qi = pl.program_id(1)
        lane = lax.broadcasted_iota(jnp.int32, (1, 128), 1)
        ri = lax.broadcasted_iota(jnp.int32, (t, t), 0)
        ci = lax.broadcasted_iota(jnp.int32, (t, t), 1)
        later_mat = _mask_bf16(ri > ci)
        causal = ci < ri
        qv = q_ref[...]
        o_tot = jnp.zeros((t, 128), F32)
        l_tot = jnp.zeros((t, 128), F32)
        for h in range(2):
            hm = (lane >= HEAD_DIM) if h else (lane < HEAD_DIM)
            qh = jnp.where(hm, qv, jnp.zeros_like(qv))

            def block(s0, run, acc, diag):
                kb = k_ref[pl.ds(s0, t), :]
                vb = v_ref[pl.ds(s0, t), :]
                z = _dot_nt(qh, kb)
                lk, _ = _softplus_parts(z)
                if diag:
                    lk = jnp.where(causal, lk, 0.0)
                later = _dot_hilo(lk, later_mat) + run
                a = jnp.exp(z + lk + later)
                if diag:
                    a = jnp.where(causal, a, 0.0)
                acc = acc + _dot(a.astype(BF16), vb)
                run = run + jnp.sum(lk, axis=-1, keepdims=True)
                return run, acc

            run, acc = block(pl.multiple_of(qi * t, t), jnp.zeros((t, 1), F32), jnp.zeros((t, 128), F32), True)

            def step(n, carry):
                return block(pl.multiple_of((qi - 1 - n) * t, t), carry[0], carry[1], False)

            run, acc = lax.fori_loop(0, qi, step, (run, acc))
            o_tot = jnp.where(hm, acc, o_tot)
            l_tot = jnp.where(hm, run, l_tot)
        o_ref[...] = o_tot.astype(BF16)
        lt_ref[...] = l_tot
        zz = z_ref[...].astype(F32)
        g_ref[...] = (o_tot * (zz * _sigmoid(zz))).astype(BF16)

    blk = pl.BlockSpec((t, 128), lambda hp, qi: (qi, hp))
    seq = pl.BlockSpec((S, 128), lambda hp, qi: (0, hp))
    return pl.pallas_call(
        body, name="attn_fwd", grid=(D_MODEL // 128, S // t),
        in_specs=[blk, seq, seq, blk], out_specs=[blk, blk, blk],
        out_shape=[jax.ShapeDtypeStruct((S, D_MODEL), BF16)] * 2 + [jax.ShapeDtypeStruct((S, D_MODEL), F32)],
        compiler_params=_params(("parallel", "arbitrary")),
    )(q, k, v, zgate)


def _ple_out_bwd(name, dx_out, e, gate, layer, ple_g, w_out):
    S = dx_out.shape[0]
    tm = ROW_TILE

    def body(dx_ref, e_ref, gt_ref, pg_ref, wo_ref, de_ref, dgp_ref, dxm_ref, dg_ref):
        dxo = dx_ref[...]
        ev = e_ref[...].astype(F32)
        gv = gt_ref[...].astype(F32)
        de_ref[...] = (dxo * gv).astype(BF16)
        dgp = (dxo * ev * gv * (1.0 - gv)).astype(BF16)
        dgp_ref[...] = dgp
        pg = jnp.concatenate([pg_ref[sh] for sh in range(N_CHIPS)], axis=0)
        dxm = dxo + _dot_nt(dgp, pg)
        dxm_ref[...] = dxm
        dg_ref[...] = _dot_nt(dxm.astype(BF16), wo_ref[...]).astype(BF16)

    row = pl.BlockSpec((tm, D_MODEL), lambda i: (i, 0))
    return pl.pallas_call(
        body, name=name, grid=(S // tm,),
        in_specs=[row, row, row,
                  pl.BlockSpec((N_CHIPS, None, 256, D_MODEL), lambda i: (0, layer, 0, 0)),
                  pl.BlockSpec((D_MODEL, D_MODEL), lambda i: (0, 0))],
        out_specs=[row, row, row, row],
        out_shape=[jax.ShapeDtypeStruct((S, D_MODEL), BF16), jax.ShapeDtypeStruct((S, D_MODEL), BF16),
                   jax.ShapeDtypeStruct((S, D_MODEL), F32), jax.ShapeDtypeStruct((S, D_MODEL), BF16)],
        compiler_params=_params(("arbitrary",)),
    )(dx_out, e, gate, ple_g, w_out)


def _attn_bwd(q, k, v, ltot, dgated, o, zgate):
    S = q.shape[0]
    t = ATT_TILE
    nq = S // t

    def body(q_ref, k_ref, v_ref, lt_ref, dg_ref, o_ref, z_ref, dq_ref, dk_ref, dv_ref, dz_ref, dk_acc, dv_acc):
        qi = pl.program_id(1)

        @pl.when(qi == 0)
        def _():
            dk_acc[...] = jnp.zeros_like(dk_acc)
            dv_acc[...] = jnp.zeros_like(dv_acc)

        lane = lax.broadcasted_iota(jnp.int32, (1, 128), 1)
        ri = lax.broadcasted_iota(jnp.int32, (t, t), 0)
        ci = lax.broadcasted_iota(jnp.int32, (t, t), 1)
        upto_mat = _mask_bf16(ri <= ci)
        before_mat = _mask_bf16(ri < ci)
        causal = ci < ri
        zz = z_ref[...].astype(F32)
        sg = _sigmoid(zz)
        dgv = dg_ref[...].astype(F32)
        dz_ref[...] = (dgv * o_ref[...].astype(F32) * _dsilu(zz, sg)).astype(BF16)
        dob = (dgv * (zz * sg)).astype(BF16)
        ltv = lt_ref[...]
        qv = q_ref[...]
        dq_tot = jnp.zeros((t, 128), F32)
        for h in range(2):
            hm = (lane >= HEAD_DIM) if h else (lane < HEAD_DIM)
            qh = jnp.where(hm, qv, jnp.zeros_like(qv))
            doh = jnp.where(hm, dob, jnp.zeros_like(dob))
            total = jnp.max(jnp.where(hm, ltv, -jnp.inf), axis=-1, keepdims=True)

            def block(s0, run, grun, dqa, diag):
                kb = k_ref[pl.ds(s0, t), :]
                vb = v_ref[pl.ds(s0, t), :]
                z = _dot_nt(qh, kb)
                lk, e = _softplus_parts(z)
                if diag:
                    lk = jnp.where(causal, lk, 0.0)
                later = total - run - _dot_hilo(lk, upto_mat)
                a = jnp.exp(z + lk + later)
                if diag:
                    a = jnp.where(causal, a, 0.0)
                beta = jnp.where(z >= 0.0, 1.0, e) / (1.0 + e)
                g = _dot_nt(doh, vb) * a
                gbefore = _dot_hilo(g, before_mat) + grun
                dz = g - beta * (g + gbefore)
                if diag:
                    dz = jnp.where(causal, dz, 0.0)
                dzb = dz.astype(BF16)
                dqa = dqa + _dot(dzb, kb)
                dk_acc[pl.ds(s0, t), :] += _dot_tn(dzb, qh)
                dv_acc[pl.ds(s0, t), :] += _dot_tn(a.astype(BF16), doh)
                run = run + jnp.sum(lk, axis=-1, keepdims=True)
                grun = grun + jnp.sum(g, axis=-1, keepdims=True)
                return run, grun, dqa

            def step(n, c):
                return block(pl.multiple_of(n * t, t), c[0], c[1], c[2], False)

            zero1 = jnp.zeros((t, 1), F32)
            carry = lax.fori_loop(0, qi, step, (zero1, zero1, jnp.zeros((t, 128), F32)))
            carry = block(pl.multiple_of(qi * t, t), carry[0], carry[1], carry[2], True)
            dq_tot = jnp.where(hm, carry[2], dq_tot)
        dq_ref[...] = dq_tot.astype(BF16)

        @pl.when(qi == nq - 1)
        def _():
            dk_ref[...] = dk_acc[...].astype(BF16)
            dv_ref[...] = dv_acc[...].astype(BF16)

    blk = pl.BlockSpec((t, 128), lambda hp, qi: (qi, hp))
    seq = pl.BlockSpec((S, 128), lambda hp, qi: (0, hp))
    return pl.pallas_call(
        body, name="attn_bwd", grid=(D_MODEL // 128, nq),
        in_specs=[blk, seq, seq, blk, blk, blk, blk], out_specs=[blk, seq, seq, blk],
        out_shape=[jax.ShapeDtypeStruct((S, D_MODEL), BF16)] * 4,
        scratch_shapes=[pltpu.VMEM((S, 128), F32), pltpu.VMEM((S, 128), F32)],
        compiler_params=_params(("parallel", "arbitrary")),
    )(q, k, v, ltot, dgated, o, zgate)


def _rms_bwd(xv, dh_gain_sum):
    r = lax.rsqrt(jnp.mean(xv * xv, axis=-1, keepdims=True) + EPS)
    xhat = xv * r
    dx = r * (dh_gain_sum - xhat * jnp.mean(dh_gain_sum * xhat, axis=-1, keepdims=True))
    return dx, xhat


def _b_in_bwd(dq, dk, dv, dz, q_raw, k_raw, x, dx_mid, q_gain_t, k_gain_t, b_gain, kv_gain, w_in, w_kv):
    S = x.shape[0]
    tm = ROW_TILE

    def body(dq_ref, dk_ref, dv_ref, dz_ref, qr_ref, kr_ref, x_ref, dxm_ref, qg_ref, kg_ref, bg_ref, kvg_ref,
             win_ref, wkv_ref, dqz_ref, dkv_ref, dx_ref, small_ref):
        @pl.when(pl.program_id(0) == 0)
        def _():
            small_ref[...] = jnp.zeros_like(small_ref)

        bd = _head_mean_matrix()

        def head_norm_bwd(dy_ref, raw_ref, gain, scale):
            raw = raw_ref[...].astype(F32)
            rr = lax.rsqrt(_head_mean(raw * raw, bd) + EPS)
            xhat = raw * rr
            dy = dy_ref[...].astype(F32) * scale
            gdy = dy * gain
            draw = rr * (gdy - xhat * _head_mean(gdy * xhat, bd))
            return draw.astype(BF16), jnp.sum(dy * xhat, axis=0, keepdims=True)

        dqr, dqg = head_norm_bwd(dq_ref, qr_ref, qg_ref[...], SB_SCALE)
        dkr, dkg = head_norm_bwd(dk_ref, kr_ref, kg_ref[...], 1.0)
        dqz_ref[:, :D_MODEL] = dqr
        dqz_ref[:, D_MODEL:] = dz_ref[...]
        dkv_ref[:, :D_MODEL] = dkr
        dkv_ref[:, D_MODEL:] = dv_ref[...]
        dhb = jnp.zeros((tm, D_MODEL), F32)
        dhkv = jnp.zeros((tm, D_MODEL), F32)
        for sh in range(N_CHIPS):
            cols = slice(sh * 512, (sh + 1) * 512)
            dhb = dhb + _dot_nt(dqz_ref[:, cols], win_ref[sh])
            dhkv = dhkv + _dot_nt(dkv_ref[:, cols], wkv_ref[sh])
        dx, xhat = _rms_bwd(x_ref[...], dhb * bg_ref[...] + dhkv * kvg_ref[...])
        dx_ref[...] = dxm_ref[...] + dx
        small_ref[0:1, :] += dqg
        small_ref[1:2, :] += dkg
        small_ref[2:3, :] += jnp.sum(dhb * xhat, axis=0, keepdims=True)
        small_ref[3:4, :] += jnp.sum(dhkv * xhat, axis=0, keepdims=True)

    row = pl.BlockSpec((tm, D_MODEL), lambda i: (i, 0))
    wide = pl.BlockSpec((tm, 2 * D_MODEL), lambda i: (i, 0))
    vec = pl.BlockSpec((1, D_MODEL), lambda i: (0, 0))
    wsp = pl.BlockSpec((N_CHIPS, D_MODEL, 512), lambda i: (0, 0, 0))
    return pl.pallas_call(
        body, name="b_in_bwd", grid=(S // tm,),
        in_specs=[row] * 8 + [vec] * 4 + [wsp, wsp],
        out_specs=[wide, wide, row, pl.BlockSpec((8, D_MODEL), lambda i: (0, 0))],
        out_shape=[jax.ShapeDtypeStruct((S, 2 * D_MODEL), BF16), jax.ShapeDtypeStruct((S, 2 * D_MODEL), BF16),
                   jax.ShapeDtypeStruct((S, D_MODEL), F32), jax.ShapeDtypeStruct((8, D_MODEL), F32)],
        compiler_params=_params(("arbitrary",), 56),
    )(dq, dk, dv, dz, q_raw, k_raw, x, dx_mid, q_gain_t, k_gain_t, b_gain, kv_gain, w_in, w_kv)


def _a_mix_bwd(dgated, uz, pooled, wg, scale, w_in, x, dx_mid, gain):
    S = x.shape[0]
    tm = ROW_TILE
    n = S // tm

    def body(dg_ref, z_ref, p_ref, wg_ref, sc_ref, win_ref, x_ref, dxm_ref, gn_ref,
             duz_ref, dmr_ref, dx_ref, small_ref, halo_hi, halo_lo):
        i = pl.program_id(0)

        @pl.when(i == 0)
        def _():
            small_ref[...] = jnp.zeros_like(small_ref)
            halo_hi[...] = jnp.zeros_like(halo_hi)
            halo_lo[...] = jnp.zeros_like(halo_lo)

        first_row = (n - 1 - i) * tm
        row = lax.broadcasted_iota(jnp.int32, (tm, tm), 0)
        col = lax.broadcasted_iota(jnp.int32, (tm, tm), 1)
        d = col - row
        for g, w in enumerate(POOL_WINDOWS):
            cols = slice(g * GROUP_DIM, (g + 1) * GROUP_DIM)
            wgg = _group_weight(wg_ref, g)
            sc = sc_ref[:, cols]
            mraw = _dot(p_ref[:, cols], wgg)
            z = z_ref[:, cols]
            sg = _sigmoid(z)
            dga = dg_ref[:, cols].astype(F32)
            dm = dga * (z * sg)
            duz_ref[:, D_MODEL + g * GROUP_DIM:D_MODEL + (g + 1) * GROUP_DIM] = (
                dga * (mraw * sc) * _dsilu(z, sg)).astype(BF16)
            small_ref[0:1, cols] += jnp.sum(dm * mraw, axis=0, keepdims=True)
            dmr = (dm * sc).astype(BF16)
            dmr_ref[:, cols] = dmr
            dp = _dot_nt(dmr, wgg)
            hi, lo = _hilo(dp * _inv_count(first_row, tm, w))
            t_main = _mask_bf16((d >= 0) & (d < w))
            t_halo = _mask_bf16(d + tm < w)
            du = (_dot(t_main, hi) + _dot(t_main, lo) + _dot(t_halo, halo_hi[:, cols]) + _dot(t_halo, halo_lo[:, cols])
                  - dp)
            halo_hi[:, cols] = hi
            halo_lo[:, cols] = lo
            duz_ref[:, cols] = du.astype(BF16)
        dh = jnp.zeros((tm, D_MODEL), F32)
        for sh in range(N_CHIPS):
            dh = dh + _dot_nt(duz_ref[:, sh * 512:(sh + 1) * 512], win_ref[sh])
        dx, xhat = _rms_bwd(x_ref[...], dh * gn_ref[...])
        dx_ref[...] = dxm_ref[...] + dx
        small_ref[1:2, :] += jnp.sum(dh * xhat, axis=0, keepdims=True)

    rev = lambda i: (n - 1 - i, 0)
    row = pl.BlockSpec((tm, D_MODEL), rev)
    vec = pl.BlockSpec((1, D_MODEL), lambda i: (0, 0))
    return pl.pallas_call(
        body, name="a_mix_bwd", grid=(n,),
        in_specs=[row,
                  pl.BlockSpec((tm, D_MODEL), lambda i: (n - 1 - i, 1)),
                  row,
                  pl.BlockSpec((N_CHIPS, N_GROUPS, 64, GROUP_DIM), lambda i: (0, 0, 0, 0)),
                  vec,
                  pl.BlockSpec((N_CHIPS, D_MODEL, 512), lambda i: (0, 0, 0)),
                  row, row, vec],
        out_specs=[pl.BlockSpec((tm, 2 * D_MODEL), rev), row, row,
                   pl.BlockSpec((8, D_MODEL), lambda i: (0, 0))],
        out_shape=[jax.ShapeDtypeStruct((S, 2 * D_MODEL), BF16), jax.ShapeDtypeStruct((S, D_MODEL), BF16),
                   jax.ShapeDtypeStruct((S, D_MODEL), F32), jax.ShapeDtypeStruct((8, D_MODEL), F32)],
        scratch_shapes=[pltpu.VMEM((tm, D_MODEL), BF16), pltpu.VMEM((tm, D_MODEL), BF16)],
        compiler_params=_params(("arbitrary",)),
    )(dgated, uz, pooled, wg, scale, w_in, x, dx_mid, gain)


def _wgrad(name, a, dy, n_shards, a_spec=None, k_dim=None):
    S = dy.shape[0]
    ts = WGRAD_SEQ_TILE
    k_dim = a.shape[-1] if k_dim is None else k_dim
    wn = dy.shape[1] // n_shards
    tk = k_dim if k_dim * wn * 4 <= 2 * MIB else 512
    nst = S // ts

    def body(a_ref, dy_ref, out_ref):
        @pl.when(pl.program_id(2) == 0)
        def _():
            out_ref[...] = jnp.zeros_like(out_ref)

        out_ref[...] += _dot_tn(a_ref[...].astype(BF16), dy_ref[...].astype(BF16))

    if a_spec is None:
        a_spec = pl.BlockSpec((ts, tk), lambda sh, kt, st: (st, kt))
    return pl.pallas_call(
        body, name=name, grid=(n_shards, k_dim // tk, nst),
        in_specs=[a_spec, pl.BlockSpec((ts, wn), lambda sh, kt, st: (st, sh))],
        out_specs=pl.BlockSpec((None, tk, wn), lambda sh, kt, st: (sh, kt, 0)),
        out_shape=jax.ShapeDtypeStruct((n_shards, k_dim, wn), F32),
        compiler_params=_params(("parallel", "parallel", "arbitrary")),
    )(a, dy)


def _wgrad_ple(name, p, layer, de):
    ts = WGRAD_SEQ_TILE
    spec = pl.BlockSpec((None, None, ts, PLE_DIM), lambda sh, kt, st: (layer, 0, st, 0))
    return _wgrad(name, p, de, N_CHIPS, a_spec=spec, k_dim=PLE_DIM)


def _wgrad_group(pooled, dmr):
    S = pooled.shape[0]
    ts = WGRAD_SEQ_TILE
    nst = S // ts

    def body(p_ref, d_ref, out_ref, acc):
        st = pl.program_id(1)

        @pl.when(st == 0)
        def _():
            acc[...] = jnp.zeros_like(acc)

        acc[...] += _dot_tn(p_ref[...], d_ref[...])

        @pl.when(st == nst - 1)
        def _():
            for sh in range(N_CHIPS):
                out_ref[sh] = acc[sh * 64:(sh + 1) * 64, :]

    blk = pl.BlockSpec((ts, GROUP_DIM), lambda g, st: (st, g))
    return pl.pallas_call(
        body, name="wgrad_group", grid=(N_GROUPS, nst),
        in_specs=[blk, blk],
        out_specs=pl.BlockSpec((N_CHIPS, None, 64, GROUP_DIM), lambda g, st: (0, g, 0, 0)),
        out_shape=jax.ShapeDtypeStruct((N_CHIPS, N_GROUPS, 64, GROUP_DIM), F32),
        scratch_shapes=[pltpu.VMEM((GROUP_DIM, GROUP_DIM), F32)],
        compiler_params=_params(("parallel", "arbitrary")),
    )(pooled, dmr)


def _local_step(x, p, target, w):
    wg4 = w["a_w_group"].reshape(N_CHIPS, N_GROUPS, 64, GROUP_DIM)
    wa_out = w["a_w_out"].reshape(D_MODEL, D_MODEL)
    wb_out = w["b_w_out"].reshape(D_MODEL, D_MODEL)
    ple_w = w["ple_w"].reshape(N_CHIPS, 2, PLE_DIM, 256)
    ple_g = w["ple_gate_w"].reshape(N_CHIPS, 2, 256, D_MODEL)
    k_gain_t = jnp.tile(w["k_norm"].reshape(1, HEAD_DIM), (1, N_HEADS))
    q_gain_t = jnp.tile(w["b_q_norm"].reshape(1, HEAD_DIM), (1, N_HEADS))

    uz, h_a = _a_in(x, w["a_norm"], w["a_w_in"])
    gated_a, pooled = _a_mix(uz, wg4, w["a_scale"])
    x1, x2, e_a, gate_a = _out_ple("a_out_ple", gated_a, x, wa_out, p, 0, ple_w, ple_g)
    h_kv, h_b, k_raw, q_raw, k, q, v, z_b = _b_in(
        x2, w["kv_norm"], w["b_norm"], k_gain_t, q_gain_t, w["w_kv"], w["b_w_in"])
    o, gated_b, ltot = _attn_fwd(q, k, v, z_b)
    x3, dx4, e_b, gate_b, loss_blk = _out_ple("b_out_ple", gated_b, x2, wb_out, p, 1, ple_w, ple_g, target=target)

    de_b, dgp_b, dx3, dgated_b = _ple_out_bwd("b_ple_out_bwd", dx4, e_b, gate_b, 1, ple_g, wb_out)
    dq, dk, dv, dz_b = _attn_bwd(q, k, v, ltot, dgated_b, o, z_b)
    dqz, dkv, dx2, small_b = _b_in_bwd(dq, dk, dv, dz_b, q_raw, k_raw, x2, dx3, q_gain_t, k_gain_t,
                                       w["b_norm"], w["kv_norm"], w["b_w_in"], w["w_kv"])
    de_a, dgp_a, dx1, dgated_a = _ple_out_bwd("a_ple_out_bwd", dx2, e_a, gate_a, 0, ple_g, wa_out)
    duz, dmr, grad_x, small_a = _a_mix_bwd(dgated_a, uz, pooled, wg4, w["a_scale"], w["a_w_in"], x, dx1, w["a_norm"])

    grads = {
        "a_w_in": _wgrad("wgrad_a_in", h_a, duz, N_CHIPS),
        "a_w_group": _wgrad_group(pooled, dmr).reshape(N_CHIPS, N_GROUPS * 64, GROUP_DIM),
        "a_w_out": _wgrad("wgrad_a_out", gated_a, dx1, 1).reshape(N_CHIPS, 256, D_MODEL),
        "w_kv": _wgrad("wgrad_kv", h_kv, dkv, N_CHIPS),
        "b_w_in": _wgrad("wgrad_b_in", h_b, dqz, N_CHIPS),
        "b_w_out": _wgrad("wgrad_b_out", gated_b, dx3, 1).reshape(N_CHIPS, 256, D_MODEL),
        "ple_w": jnp.concatenate([_wgrad_ple("wgrad_ple0", p, 0, de_a), _wgrad_ple("wgrad_ple1", p, 1, de_b)], axis=1),
        "ple_gate_w": jnp.concatenate(
            [_wgrad("wgrad_gate0", x1, dgp_a, 1).reshape(N_CHIPS, 256, D_MODEL),
             _wgrad("wgrad_gate1", x3, dgp_b, 1).reshape(N_CHIPS, 256, D_MODEL)], axis=1),
    }
    fold = lambda row: jnp.pad(row.reshape(N_HEADS, HEAD_DIM).sum(axis=0), (0, D_MODEL - HEAD_DIM))
    small = jnp.stack([small_a[1], small_a[0], small_b[3], small_b[2], fold(small_b[1]), fold(small_b[0]),
                       jnp.zeros((D_MODEL,), F32), jnp.zeros((D_MODEL,), F32)])
    return loss_blk[0, 0], grad_x, grads, small


def _mesh_place():
    x, y, c = lax.axis_index("x"), lax.axis_index("y"), lax.axis_index("c")
    other_chips = [(1 - x, y), (x, 1 - y), (1 - x, 1 - y)]
    return x, y, c, other_chips


def _allgather_weights(shards, small):
    n = len(shards)

    def body(*refs):
        ins, small_in = refs[:n], refs[n]
        outs, small_out = refs[n + 1:2 * n + 1], refs[2 * n + 1]
        cast = refs[2 * n + 2:3 * n + 2]
        send_far, recv_far, send_sib, recv_sib, send_small, recv_small, local_sem = refs[3 * n + 2:]
        x, y, c, chips = _mesh_place()
        me = 2 * x + y
        sibling = (x, y, 1 - c)

        def half(k, which):
            rows = ins[k].shape[0] // 2
            return pl.ds(pl.multiple_of(which * rows, 16), rows)

        local = []
        for k in range(n):
            cast[k][...] = ins[k][...].astype(BF16)
            local.append(pltpu.make_async_copy(cast[k], outs[k].at[me], local_sem.at[k]))
            local[-1].start()
        local.append(pltpu.make_async_copy(small_in, small_out.at[me], local_sem.at[n]))
        local[-1].start()

        sends = []
        for j, (px, py) in enumerate(chips):
            for k in range(n):
                cp = pltpu.make_async_remote_copy(
                    src_ref=cast[k].at[half(k, c)], dst_ref=outs[k].at[me, half(k, c)],
                    send_sem=send_far.at[j * n + k], recv_sem=recv_far.at[j * n + k],
                    device_id=(px, py, c), device_id_type=MESH)
                cp.start()
                sends.append(cp)
            cp = pltpu.make_async_remote_copy(
                src_ref=small_in, dst_ref=small_out.at[me], send_sem=send_small.at[j], recv_sem=recv_small.at[j],
                device_id=(px, py, c), device_id_type=MESH)
            cp.start()
            sends.append(cp)

        def landed(j, k, which, sems_s, sems_r, device):
            px, py = chips[j]
            piece = outs[k].at[2 * px + py, half(k, which)]
            return pltpu.make_async_remote_copy(
                src_ref=piece, dst_ref=piece, send_sem=sems_s.at[j * n + k], recv_sem=sems_r.at[j * n + k],
                device_id=device, device_id_type=MESH)

        for j in range(len(chips)):
            for k in range(n):
                landed(j, k, c, send_far, recv_far, sibling).wait_recv()
                cp = landed(j, k, c, send_sib, recv_sib, sibling)
                cp.start()
                sends.append(cp)
        for j, (px, py) in enumerate(chips):
            for k in range(n):
                landed(j, k, 1 - c, send_sib, recv_sib, sibling).wait_recv()
            pltpu.make_async_remote_copy(
                src_ref=small_in, dst_ref=small_out.at[2 * px + py], send_sem=send_small.at[j],
                recv_sem=recv_small.at[j], device_id=(px, py, c), device_id_type=MESH).wait_recv()
        for cp in sends:
            cp.wait_send()
        for cp in local:
            cp.wait()

    vmem = pl.BlockSpec(memory_space=pltpu.VMEM)
    hbm = pl.BlockSpec(memory_space=pltpu.HBM)
    return pl.pallas_call(
        body, name="allgather_weights",
        in_specs=[vmem] * (n + 1), out_specs=[hbm] * (n + 1),
        out_shape=[jax.ShapeDtypeStruct((N_CHIPS,) + s.shape, BF16) for s in shards]
        + [jax.ShapeDtypeStruct((N_CHIPS,) + small.shape, F32)],
        scratch_shapes=[pltpu.VMEM(s.shape, BF16) for s in shards]
        + [pltpu.SemaphoreType.DMA((3 * n,)), pltpu.SemaphoreType.DMA((3 * n,)),
           pltpu.SemaphoreType.DMA((3 * n,)), pltpu.SemaphoreType.DMA((3 * n,)),
           pltpu.SemaphoreType.DMA((3,)), pltpu.SemaphoreType.DMA((3,)),
           pltpu.SemaphoreType.DMA((n + 1,))],
        compiler_params=_params(None, 40),
    )(*shards, small)


def _adamw(w, g, m, v):
    m = ADAM_B1 * m + (1.0 - ADAM_B1) * g
    v = ADAM_B2 * v + (1.0 - ADAM_B2) * (g * g)
    m_hat = m / (1.0 - ADAM_B1 ** ADAM_STEP)
    v_hat = v / (1.0 - ADAM_B2 ** ADAM_STEP)
    delta = -ADAM_LR * (m_hat / (jnp.sqrt(v_hat) + ADAM_EPS) + ADAM_WD * w)
    return delta, m, v


def _reduce_adam(name, grad, w, m, v):
    _, R, C = grad.shape
    hr = R // 2

    def body(g_ref, w_ref, m_ref, v_ref, go_ref, d_ref, mo_ref, vo_ref,
             to_sib, from_sib, part, to_far, from_far, send_sem, recv_sem):
        x, y, c, chips = _mesh_place()
        me = 2 * x + y
        sibling = (x, y, 1 - c)
        mine = pl.ds(pl.multiple_of(c * hr, 16), hr)
        other = pl.ds(pl.multiple_of((1 - c) * hr, 16), hr)

        to_sib[...] = g_ref[:, other, :].astype(BF16)
        swap = pltpu.make_async_remote_copy(src_ref=to_sib, dst_ref=from_sib, send_sem=send_sem.at[0],
                                            recv_sem=recv_sem.at[0], device_id=sibling, device_id_type=MESH)
        swap.start()
        swap.wait_recv()
        part[...] = g_ref[:, mine, :] + from_sib[...].astype(F32)

        far = []
        for j, (px, py) in enumerate(chips):
            to_far[j] = part[2 * px + py].astype(BF16)
            cp = pltpu.make_async_remote_copy(src_ref=to_far.at[j], dst_ref=from_far.at[j], send_sem=send_sem.at[1 + j],
                                              recv_sem=recv_sem.at[1 + j], device_id=(px, py, c), device_id_type=MESH)
            cp.start()
            far.append(cp)
        total = part[me]
        for j, cp in enumerate(far):
            cp.wait_recv()
            total = total + from_far[j].astype(F32)

        go_ref[mine, :] = total
        back = pltpu.make_async_remote_copy(src_ref=go_ref.at[mine], dst_ref=go_ref.at[mine], send_sem=send_sem.at[4],
                                            recv_sem=recv_sem.at[4], device_id=sibling, device_id_type=MESH)
        back.start()
        back.wait_recv()
        delta, m_new, v_new = _adamw(w_ref[...], go_ref[...], m_ref[...], v_ref[...])
        d_ref[...] = delta
        mo_ref[...] = m_new
        vo_ref[...] = v_new
        swap.wait_send()
        for cp in far:
            cp.wait_send()
        back.wait_send()

    vmem = pl.BlockSpec(memory_space=pltpu.VMEM)
    return pl.pallas_call(
        body, name=name,
        in_specs=[vmem] * 4, out_specs=[vmem] * 4,
        out_shape=[jax.ShapeDtypeStruct((R, C), F32)] * 4,
        scratch_shapes=[pltpu.VMEM((N_CHIPS, hr, C), BF16), pltpu.VMEM((N_CHIPS, hr, C), BF16),
                        pltpu.VMEM((N_CHIPS, hr, C), F32),
                        pltpu.VMEM((3, hr, C), BF16), pltpu.VMEM((3, hr, C), BF16),
                        pltpu.SemaphoreType.DMA((5,)), pltpu.SemaphoreType.DMA((5,))],
        compiler_params=_params(None, 56),
    )(grad, w, m, v)


def _allreduce_small(part):
    n_dev = 8

    def body(part_ref, out_ref, buf, send_sem, recv_sem):
        x, y, c, _ = _mesh_place()
        me = 4 * x + 2 * y + c
        buf[me] = part_ref[...]
        sends = []
        for k in range(1, n_dev):
            peer = ((1 - x) if k & 4 else x, (1 - y) if k & 2 else y, (1 - c) if k & 1 else c)
            cp = pltpu.make_async_remote_copy(src_ref=part_ref, dst_ref=buf.at[me], send_sem=send_sem.at[k - 1],
                                              recv_sem=recv_sem.at[k - 1], device_id=peer, device_id_type=MESH)
            cp.start()
            sends.append(cp)
        for cp in sends:
            cp.wait_recv()
        total = buf[0]
        for s in range(1, n_dev):
            total = total + buf[s]
        out_ref[...] = total
        for cp in sends:
            cp.wait_send()

    vmem = pl.BlockSpec(memory_space=pltpu.VMEM)
    return pl.pallas_call(
        body, name="allreduce_small", in_specs=[vmem], out_specs=vmem,
        out_shape=jax.ShapeDtypeStruct(part.shape, F32),
        scratch_shapes=[pltpu.VMEM((n_dev,) + part.shape, F32),
                        pltpu.SemaphoreType.DMA((n_dev - 1,)), pltpu.SemaphoreType.DMA((n_dev - 1,))],
    )(part)


def _adam_small(w, g, m, v):
    def body(w_ref, g_ref, m_ref, v_ref, d_ref, mo_ref, vo_ref):
        delta, m_new, v_new = _adamw(w_ref[...], g_ref[...], m_ref[...], v_ref[...])
        d_ref[...] = delta
        mo_ref[...] = m_new
        vo_ref[...] = v_new

    vmem = pl.BlockSpec(memory_space=pltpu.VMEM)
    return pl.pallas_call(
        body, name="adam_small", in_specs=[vmem] * 4, out_specs=[vmem] * 3,
        out_shape=[jax.ShapeDtypeStruct(w.shape, F32)] * 3,
    )(w, g, m, v)


BIG = ("a_w_in", "a_w_group", "a_w_out", "w_kv", "b_w_in", "b_w_out", "ple_w", "ple_gate_w")
SMALL = ("a_norm", "a_scale", "kv_norm", "b_norm", "k_norm", "b_q_norm")
SMALL_SHARDED = ("a_norm", "a_scale")
WEIGHTS = ("a_norm", "a_w_in", "a_w_group", "a_scale", "a_w_out", "kv_norm", "w_kv", "k_norm", "b_norm", "b_w_in",
           "b_q_norm", "b_w_out", "ple_w", "ple_gate_w")


def _as_matrix(a):
    return a.reshape(-1, a.shape[-1])


def _pack_small(arrs):
    rows = [jnp.pad(a.reshape(-1), (0, D_MODEL - a.size)) for a in arrs]
    rows += [jnp.zeros((D_MODEL,), F32)] * (8 - len(rows))
    return jnp.stack(rows)


def kernel(x, p, a_norm, a_w_in, a_w_group, a_scale, a_w_out, kv_norm, w_kv, k_norm, b_norm, b_w_in, b_q_norm, b_w_out, ple_w, ple_gate_w, loss_target, m_a_norm, m_a_w_in, m_a_w_group, m_a_scale, m_a_w_out, m_kv_norm, m_w_kv, m_k_norm, m_b_norm, m_b_w_in, m_b_q_norm, m_b_w_out, m_ple_w, m_ple_gate_w, v_a_norm, v_a_w_in, v_a_w_group, v_a_scale, v_a_w_out, v_kv_norm, v_w_kv, v_k_norm, v_b_norm, v_b_w_in, v_b_q_norm, v_b_w_out, v_ple_w, v_ple_gate_w):
    wts = dict(a_norm=a_norm, a_w_in=a_w_in, a_w_group=a_w_group, a_scale=a_scale, a_w_out=a_w_out, kv_norm=kv_norm,
               w_kv=w_kv, k_norm=k_norm, b_norm=b_norm, b_w_in=b_w_in, b_q_norm=b_q_norm, b_w_out=b_w_out,
               ple_w=ple_w, ple_gate_w=ple_gate_w)
    mom = dict(a_norm=m_a_norm, a_w_in=m_a_w_in, a_w_group=m_a_w_group, a_scale=m_a_scale, a_w_out=m_a_w_out,
               kv_norm=m_kv_norm, w_kv=m_w_kv, k_norm=m_k_norm, b_norm=m_b_norm, b_w_in=m_b_w_in,
               b_q_norm=m_b_q_norm, b_w_out=m_b_w_out, ple_w=m_ple_w, ple_gate_w=m_ple_gate_w)
    var = dict(a_norm=v_a_norm, a_w_in=v_a_w_in, a_w_group=v_a_w_group, a_scale=v_a_scale, a_w_out=v_a_w_out,
               kv_norm=v_kv_norm, w_kv=v_w_kv, k_norm=v_k_norm, b_norm=v_b_norm, b_w_in=v_b_w_in,
               b_q_norm=v_b_q_norm, b_w_out=v_b_w_out, ple_w=v_ple_w, ple_gate_w=v_ple_gate_w)
    S = x.shape[1]
    chip = 2 * lax.axis_index("x") + lax.axis_index("y")

    sharded_small = jnp.concatenate([a_norm.reshape(1, 256), a_scale.reshape(1, 256), jnp.zeros((6, 256), F32)], axis=0)
    gathered = _allgather_weights([_as_matrix(wts[n]) for n in BIG], sharded_small)
    full = dict(zip(BIG, gathered[:-1]))
    full["a_norm"] = gathered[-1][:, 0, :].reshape(1, D_MODEL)
    full["a_scale"] = gathered[-1][:, 1, :].reshape(1, D_MODEL)
    full["kv_norm"] = kv_norm.reshape(1, D_MODEL)
    full["b_norm"] = b_norm.reshape(1, D_MODEL)
    full["k_norm"] = k_norm
    full["b_q_norm"] = b_q_norm

    loss_local, grad_x, grads, small_part = _local_step(x.reshape(S, D_MODEL), p, loss_target.reshape(S, D_MODEL), full)
    loss = lax.psum(loss_local, ("x", "y", "c"))

    out_g, out_d, out_m, out_v = {}, {}, {}, {}
    for n in BIG:
        shape = wts[n].shape
        g, d, m_new, v_new = _reduce_adam("reduce_adam_" + n, grads[n], _as_matrix(wts[n]), _as_matrix(mom[n]),
                                          _as_matrix(var[n]))
        out_g[n], out_d[n], out_m[n], out_v[n] = (t.reshape(shape) for t in (g, d, m_new, v_new))

    small_sum = _allreduce_small(small_part)
    small_rows = []
    for i, n in enumerate(SMALL):
        row = small_sum[i]
        if n in SMALL_SHARDED:
            row = lax.dynamic_slice(row, (chip * 256,), (256,))
        else:
            row = row[:wts[n].size]
        small_rows.append(row)
    g_small = _pack_small(small_rows)
    d_small, m_small, v_small = _adam_small(_pack_small([wts[n] for n in SMALL]), g_small,
                                            _pack_small([mom[n] for n in SMALL]), _pack_small([var[n] for n in SMALL]))
    for i, n in enumerate(SMALL):
        shape, size = wts[n].shape, wts[n].size
        out_g[n], out_d[n], out_m[n], out_v[n] = (t[i, :size].reshape(shape) for t in (g_small, d_small, m_small, v_small))

    return (loss, grad_x.reshape(1, S, D_MODEL), *[out_g[n] for n in WEIGHTS], *[out_d[n] for n in WEIGHTS],
            *[out_m[n] for n in WEIGHTS], *[out_v[n] for n in WEIGHTS])
```

```python
import functools

import jax
import jax.numpy as jnp
from jax import lax
from jax.experimental import pallas as pl
from jax.experimental.pallas import tpu as pltpu

F32 = jnp.float32
BF16 = jnp.bfloat16
MESH = pl.DeviceIdType.MESH

D_MODEL = 1024
N_HEADS = 16
HEAD_DIM = 64
PLE_DIM = 256
N_GROUPS = 4
GROUP_DIM = 256
POOL_WINDOWS = (2, 4, 8, 16)
N_CHIPS = 4
EPS = 1e-6
SB_SCALE = HEAD_DIM ** -0.5

ADAM_LR = 0.001
ADAM_B1 = 0.9
ADAM_B2 = 0.999
ADAM_EPS = 1e-08
ADAM_WD = 0.01
ADAM_STEP = 10

ROW_TILE = 256
ATT_Q_TILE = 512
ATT_K_TILE = 256
WGRAD_SEQ_TILE = 512
MIB = 1024 * 1024


def _params(semantics=None, vmem_mib=48):
    return pltpu.CompilerParams(dimension_semantics=semantics, vmem_limit_bytes=vmem_mib * MIB)


def _dot(a, b):
    return jnp.dot(a, b, preferred_element_type=F32)


def _dot_nt(a, b):
    return lax.dot_general(a, b, (((1,), (1,)), ((), ())), preferred_element_type=F32)


def _dot_tn(a, b):
    return lax.dot_general(a, b, (((0,), (0,)), ((), ())), preferred_element_type=F32)


def _hilo(x):
    hi = x.astype(BF16)
    lo = (x - hi.astype(F32)).astype(BF16)
    return hi, lo


def _dot_hilo(x, w):
    hi, lo = _hilo(x)
    return _dot(hi, w) + _dot(lo, w)


def _sigmoid(z):
    return jax.nn.sigmoid(z)


def _dsilu(z, sg):
    return sg * (1.0 + z * (1.0 - sg))


def _mask_bf16(cond):
    return jnp.where(cond, 1.0, 0.0).astype(BF16)


def _head_mean_matrix():
    r = lax.broadcasted_iota(jnp.int32, (256, 256), 0) // HEAD_DIM
    c = lax.broadcasted_iota(jnp.int32, (256, 256), 1) // HEAD_DIM
    return _mask_bf16(r == c)


def _head_mean(x, bd):
    parts = []
    for s in range(x.shape[1] // 256):
        parts.append(_dot_hilo(x[:, s * 256:(s + 1) * 256], bd))
    out = parts[0] if len(parts) == 1 else jnp.concatenate(parts, axis=1)
    return out * (1.0 / HEAD_DIM)


def _a_in(x, gain, w_sh):
    S = x.shape[0]
    tm = 512
    nsh, _, wn = w_sh.shape

    def body(x_ref, g_ref, w_ref, uz_ref, h_ref):
        @pl.when(pl.program_id(1) == 0)
        def _():
            xv = x_ref[...]
            r = lax.rsqrt(jnp.mean(xv * xv, axis=-1, keepdims=True) + EPS)
            h_ref[...] = (xv * r * g_ref[...]).astype(BF16)

        uz_ref[...] = _dot(h_ref[...], w_ref[0])

    return pl.pallas_call(
        body, name="a_in", grid=(S // tm, nsh),
        in_specs=[pl.BlockSpec((tm, D_MODEL), lambda i, j: (i, 0)),
                  pl.BlockSpec((1, D_MODEL), lambda i, j: (0, 0)),
                  pl.BlockSpec((1, D_MODEL, wn), lambda i, j: (j, 0, 0))],
        out_specs=[pl.BlockSpec((tm, wn), lambda i, j: (i, j)),
                   pl.BlockSpec((tm, D_MODEL), lambda i, j: (i, 0))],
        out_shape=[jax.ShapeDtypeStruct((S, nsh * wn), F32),
                   jax.ShapeDtypeStruct((S, D_MODEL), BF16)],
        compiler_params=_params(("parallel", "arbitrary")),
    )(x, gain, w_sh)


def _inv_count(first_row, rows, w):
    t1 = first_row + 1 + lax.broadcasted_iota(jnp.int32, (rows, 1), 0)
    return 1.0 / jnp.minimum(t1, w).astype(F32)


def _group_weight(wg_ref, g):
    return jnp.concatenate([wg_ref[sh, g] for sh in range(N_CHIPS)], axis=0)


def _a_mix(uz, wg, scale):
    S = uz.shape[0]
    tm = ROW_TILE

    def body(u_ref, up_ref, z_ref, wg_ref, sc_ref, ga_ref, p_ref):
        i = pl.program_id(0)
        row = lax.broadcasted_iota(jnp.int32, (tm, tm), 0)
        col = lax.broadcasted_iota(jnp.int32, (tm, tm), 1)
        d = row - col
        for g, w in enumerate(POOL_WINDOWS):
            cols = slice(g * GROUP_DIM, (g + 1) * GROUP_DIM)
            t_main = _mask_bf16((d >= 0) & (d < w))
            t_halo = _mask_bf16(d + tm < w)
            u = u_ref[:, cols]
            up = jnp.where(i > 0, up_ref[:, cols], 0.0)
            hi, lo = _hilo(u)
            hip, lop = _hilo(up)
            wsum = _dot(t_main, hi) + _dot(t_main, lo) + _dot(t_halo, hip) + _dot(t_halo, lop)
            pooled = (wsum * _inv_count(i * tm, tm, w) - u).astype(BF16)
            p_ref[:, cols] = pooled
            mraw = _dot(pooled, _group_weight(wg_ref, g))
            z = z_ref[:, cols]
            ga_ref[:, cols] = (mraw * sc_ref[:, cols] * (z * _sigmoid(z))).astype(BF16)

    return pl.pallas_call(
        body, name="a_mix", grid=(S // tm,),
        in_specs=[pl.BlockSpec((tm, D_MODEL), lambda i: (i, 0)),
                  pl.BlockSpec((tm, D_MODEL), lambda i: (jnp.maximum(i - 1, 0), 0)),
                  pl.BlockSpec((tm, D_MODEL), lambda i: (i, 1)),
                  pl.BlockSpec((N_CHIPS, N_GROUPS, 64, GROUP_DIM), lambda i: (0, 0, 0, 0)),
                  pl.BlockSpec((1, D_MODEL), lambda i: (0, 0))],
        out_specs=[pl.BlockSpec((tm, D_MODEL), lambda i: (i, 0)),
                   pl.BlockSpec((tm, D_MODEL), lambda i: (i, 0))],
        out_shape=[jax.ShapeDtypeStruct((S, D_MODEL), BF16),
                   jax.ShapeDtypeStruct((S, D_MODEL), BF16)],
        compiler_params=_params(("arbitrary",)),
    )(uz, uz, uz, wg, scale)


def _out_ple(name, gated, x_in, w_out, p, layer, ple_w, ple_g, target=None):
    S = x_in.shape[0]
    tm = ROW_TILE
    with_loss = target is not None

    def body(*refs):
        if with_loss:
            g_ref, x_ref, wo_ref, p_ref, pw_ref, pg_ref, t_ref, xm_ref, dx_ref, e_ref, gt_ref, loss_ref = refs
        else:
            g_ref, x_ref, wo_ref, p_ref, pw_ref, pg_ref, xm_ref, xo_ref, e_ref, gt_ref = refs
        xm = x_ref[...] + _dot(g_ref[...], wo_ref[...])
        xm_ref[...] = xm
        pb = p_ref[...].astype(BF16)
        e = jnp.concatenate([_dot(pb, pw_ref[sh]) for sh in range(N_CHIPS)], axis=1)
        pg = jnp.concatenate([pg_ref[sh] for sh in range(N_CHIPS)], axis=0)
        gate = _sigmoid(_dot(xm.astype(BF16), pg))
        e_ref[...] = e.astype(BF16)
        gt_ref[...] = gate.astype(BF16)
        xo = xm + e * gate
        if with_loss:
            diff = xo - t_ref[...]
            dx_ref[...] = diff * (1.0 / D_MODEL)

            @pl.when(pl.program_id(0) == 0)
            def _():
                loss_ref[...] = jnp.zeros_like(loss_ref)

            loss_ref[...] += jnp.sum(diff * diff) * (0.5 / D_MODEL)
        else:
            xo_ref[...] = xo

    row = pl.BlockSpec((tm, D_MODEL), lambda i: (i, 0))
    in_specs = [row, row,
                pl.BlockSpec((D_MODEL, D_MODEL), lambda i: (0, 0)),
                pl.BlockSpec((None, None, tm, PLE_DIM), lambda i: (layer, 0, i, 0)),
                pl.BlockSpec((N_CHIPS, None, PLE_DIM, 256), lambda i: (0, layer, 0, 0)),
                pl.BlockSpec((N_CHIPS, None, 256, D_MODEL), lambda i: (0, layer, 0, 0))]
    args = [gated, x_in, w_out, p, ple_w, ple_g]
    out_specs = [row, row, row, row]
    out_shape = [jax.ShapeDtypeStruct((S, D_MODEL), F32), jax.ShapeDtypeStruct((S, D_MODEL), F32),
                 jax.ShapeDtypeStruct((S, D_MODEL), BF16), jax.ShapeDtypeStruct((S, D_MODEL), BF16)]
    if with_loss:
        in_specs.append(row)
        args.append(target)
        out_specs.append(pl.BlockSpec((8, 128), lambda i: (0, 0)))
        out_shape.append(jax.ShapeDtypeStruct((8, 128), F32))
    return pl.pallas_call(
        body, name=name, grid=(S // tm,), in_specs=in_specs, out_specs=out_specs, out_shape=out_shape,
        compiler_params=_params(("arbitrary",)),
    )(*args)


def _b_in(x, kv_gain, b_gain, k_gain_t, q_gain_t, w_kv, w_in):
    S = x.shape[0]
    tm = ROW_TILE

    def body(x_ref, kvg_ref, bg_ref, kg_ref, qg_ref, wkv_ref, win_ref,
             hkv_ref, hb_ref, kraw_ref, qraw_ref, k_ref, q_ref, v_ref, z_ref):
        xv = x_ref[...]
        y = xv * lax.rsqrt(jnp.mean(xv * xv, axis=-1, keepdims=True) + EPS)
        hkv = (y * kvg_ref[...]).astype(BF16)
        hb = (y * bg_ref[...]).astype(BF16)
        hkv_ref[...] = hkv
        hb_ref[...] = hb
        bd = _head_mean_matrix()

        def head_norm(raw, gain):
            rr = lax.rsqrt(_head_mean(raw * raw, bd) + EPS)
            return raw * rr * gain

        for sh in range(N_CHIPS):
            kvc = _dot(hkv, wkv_ref[sh])
            qzc = _dot(hb, win_ref[sh])
            cols = slice((sh % 2) * 512, (sh % 2) * 512 + 512)
            if sh < 2:
                kraw_ref[:, cols] = kvc.astype(BF16)
                qraw_ref[:, cols] = qzc.astype(BF16)
                k_ref[:, cols] = head_norm(kvc, kg_ref[:, cols]).astype(BF16)
                q_ref[:, cols] = (head_norm(qzc, qg_ref[:, cols]) * SB_SCALE).astype(BF16)
            else:
                v_ref[:, cols] = kvc.astype(BF16)
                z_ref[:, cols] = qzc.astype(BF16)

    row = pl.BlockSpec((tm, D_MODEL), lambda i: (i, 0))
    vec = pl.BlockSpec((1, D_MODEL), lambda i: (0, 0))
    wsp = pl.BlockSpec((N_CHIPS, D_MODEL, 512), lambda i: (0, 0, 0))
    return pl.pallas_call(
        body, name="b_in", grid=(S // tm,),
        in_specs=[row, vec, vec, vec, vec, wsp, wsp],
        out_specs=[row] * 8,
        out_shape=[jax.ShapeDtypeStruct((S, D_MODEL), BF16)] * 8,
        compiler_params=_params(("arbitrary",), 56),
    )(x, kv_gain, b_gain, k_gain_t, q_gain_t, w_kv, w_in)


def _softplus_parts(z):
    e = jnp.exp(-jnp.abs(z))
    return -(jnp.maximum(z, 0.0) + jnp.log(1.0 + e)), e


def _attn_fwd(q, k, v, zgate):
    S = q.shape[0]
    tq, tk = ATT_Q_TILE, ATT_K_TILE
    kpq = tq // tk

    def body(q_ref, k_ref, v_ref, z_ref, o_ref, g_ref, lt_ref):
        qi = pl.program_id(1)
        lane = lax.broadcasted_iota(jnp.int32, (1, 128), 1)
        ri = lax.broadcasted_iota(jnp.int32, (tk, tk), 0)
        ci = lax.broadcasted_iota(jnp.int32, (tk, tk), 1)
        later_mat = _mask_bf16(ri > ci)
        t_idx = qi * tq + lax.broadcasted_iota(jnp.int32, (tq, tk), 0)
        s_off = lax.broadcasted_iota(jnp.int32, (tq, tk), 1)
        qv = q_ref[...]
        first = lane < HEAD_DIM
        q_heads = (jnp.where(first, qv, jnp.zeros_like(qv)), jnp.where(first, jnp.zeros_like(qv), qv))

        def block(kj, carry, masked):
            s0 = pl.multiple_of(kj * tk, tk)
            kb = k_ref[pl.ds(s0, tk), :]
            vb = v_ref[pl.ds(s0, tk), :]
            heads = range(2)
            z = [_dot_nt(q_heads[h], kb) for h in heads]
            lk = [_softplus_parts(z[h])[0] for h in heads]
            if masked:
                visible = s0 + s_off < t_idx
                lk = [jnp.where(visible, lk[h], 0.0) for h in heads]
            split = [_hilo(lk[h]) for h in heads]
            later = [_dot(split[h][0], later_mat) + _dot(split[h][1], later_mat) + carry[2 * h] for h in heads]
            a = [jnp.exp(z[h] + lk[h] + later[h]) for h in heads]
            if masked:
                a = [jnp.where(visible, a[h], 0.0) for h in heads]
            out = []
            for h in heads:
                out += [carry[2 * h] + jnp.sum(lk[h], axis=-1, keepdims=True),
                        carry[2 * h + 1] + _dot(a[h].astype(BF16), vb)]
            return tuple(out)

        zero1, zero128 = jnp.zeros((tq, 1), F32), jnp.zeros((tq, 128), F32)
        carry = (zero1, zero128, zero1, zero128)
        for d in reversed(range(kpq)):
            carry = block(qi * kpq + d, carry, True)
        carry = lax.fori_loop(0, qi * kpq, lambda n, c: block(qi * kpq - 1 - n, c, False), carry)
        o_tot = jnp.where(first, carry[1], carry[3])
        l_tot = jnp.where(first, carry[0], carry[2])
        o_ref[...] = o_tot.astype(BF16)
        lt_ref[...] = l_tot
        zz = z_ref[...].astype(F32)
        g_ref[...] = (o_tot * (zz * _sigmoid(zz))).astype(BF16)

    blk = pl.BlockSpec((tq, 128), lambda hp, qi: (qi, hp))
    seq = pl.BlockSpec((S, 128), lambda hp, qi: (0, hp))
    return pl.pallas_call(
        body, name="attn_fwd", grid=(D_MODEL // 128, S // tq),
        in_specs=[blk, seq, seq, blk], out_specs=[blk, blk, blk],
        out_shape=[jax.ShapeDtypeStruct((S, D_MODEL), BF16)] * 2 + [jax.ShapeDtypeStruct((S, D_MODEL), F32)],
        compiler_params=_params(("parallel", "arbitrary")),
    )(q, k, v, zgate)


def _ple_out_bwd(name, dx_out, e, gate, layer, ple_g, w_out):
    S = dx_out.shape[0]
    tm = ROW_TILE

    def body(dx_ref, e_ref, gt_ref, pg_ref, wo_ref, de_ref, dgp_ref, dxm_ref, dg_ref):
        dxo = dx_ref[...]
        ev = e_ref[...].astype(F32)
        gv = gt_ref[...].astype(F32)
        de_ref[...] = (dxo * gv).astype(BF16)
        dgp = (dxo * ev * gv * (1.0 - gv)).astype(BF16)
        dgp_ref[...] = dgp
        pg = jnp.concatenate([pg_ref[sh] for sh in range(N_CHIPS)], axis=0)
        dxm = dxo + _dot_nt(dgp, pg)
        dxm_ref[...] = dxm
        dg_ref[...] = _dot_nt(dxm.astype(BF16), wo_ref[...]).astype(BF16)

    row = pl.BlockSpec((tm, D_MODEL), lambda i: (i, 0))
    return pl.pallas_call(
        body, name=name, grid=(S // tm,),
        in_specs=[row, row, row,
                  pl.BlockSpec((N_CHIPS, None, 256, D_MODEL), lambda i: (0, layer, 0, 0)),
                  pl.BlockSpec((D_MODEL, D_MODEL), lambda i: (0, 0))],
        out_specs=[row, row, row, row],
        out_shape=[jax.ShapeDtypeStruct((S, D_MODEL), BF16), jax.ShapeDtypeStruct((S, D_MODEL), BF16),
                   jax.ShapeDtypeStruct((S, D_MODEL), F32), jax.ShapeDtypeStruct((S, D_MODEL), BF16)],
        compiler_params=_params(("arbitrary",)),
    )(dx_out, e, gate, ple_g, w_out)


def _attn_bwd(q, k, v, ltot, dgated, o, zgate):
    S = q.shape[0]
    tq, tk = ATT_Q_TILE, ATT_K_TILE
    kpq = tq // tk
    nq = S // tq

    def body(q_ref, k_ref, v_ref, lt_ref, dg_ref, o_ref, z_ref, dq_ref, dk_ref, dv_ref, dz_ref, dk_acc, dv_acc):
        qi = pl.program_id(1)

        @pl.when(qi == 0)
        def _():
            dk_acc[...] = jnp.zeros_like(dk_acc)
            dv_acc[...] = jnp.zeros_like(dv_acc)

        lane = lax.broadcasted_iota(jnp.int32, (1, 128), 1)
        ri = lax.broadcasted_iota(jnp.int32, (tk, tk), 0)
        ci = lax.broadcasted_iota(jnp.int32, (tk, tk), 1)
        upto_mat = _mask_bf16(ri <= ci)
        before_mat = _mask_bf16(ri < ci)
        t_idx = qi * tq + lax.broadcasted_iota(jnp.int32, (tq, tk), 0)
        s_off = lax.broadcasted_iota(jnp.int32, (tq, tk), 1)
        zz = z_ref[...].astype(F32)
        sg = _sigmoid(zz)
        dgv = dg_ref[...].astype(F32)
        dz_ref[...] = (dgv * o_ref[...].astype(F32) * _dsilu(zz, sg)).astype(BF16)
        dob = (dgv * (zz * sg)).astype(BF16)
        ltv = lt_ref[...]
        qv = q_ref[...]
        first = lane < HEAD_DIM
        masks = (first, jnp.logical_not(first))
        q_heads = [jnp.where(hm, qv, jnp.zeros_like(qv)) for hm in masks]
        do_heads = [jnp.where(hm, dob, jnp.zeros_like(dob)) for hm in masks]
        totals = [jnp.max(jnp.where(hm, ltv, -jnp.inf), axis=-1, keepdims=True) for hm in masks]

        def block(kj, carry, masked):
            s0 = pl.multiple_of(kj * tk, tk)
            kb = k_ref[pl.ds(s0, tk), :]
            vb = v_ref[pl.ds(s0, tk), :]
            heads = range(2)
            z = [_dot_nt(q_heads[h], kb) for h in heads]
            da = [_dot_nt(do_heads[h], vb) for h in heads]
            sp = [_softplus_parts(z[h]) for h in heads]
            lk = [sp[h][0] for h in heads]
            if masked:
                visible = s0 + s_off < t_idx
                lk = [jnp.where(visible, lk[h], 0.0) for h in heads]
            split = [_hilo(lk[h]) for h in heads]
            upto = [_dot(split[h][0], upto_mat) + _dot(split[h][1], upto_mat) for h in heads]
            a = [jnp.exp(z[h] + lk[h] + (totals[h] - carry[3 * h] - upto[h])) for h in heads]
            if masked:
                a = [jnp.where(visible, a[h], 0.0) for h in heads]
            g = [da[h] * a[h] for h in heads]
            gsplit = [_hilo(g[h]) for h in heads]
            gbefore = [_dot(gsplit[h][0], before_mat) + _dot(gsplit[h][1], before_mat) + carry[3 * h + 1]
                       for h in heads]
            dzb = []
            for h in heads:
                e = sp[h][1]
                beta = jnp.where(z[h] >= 0.0, 1.0, e) / (1.0 + e)
                dz = g[h] - beta * (g[h] + gbefore[h])
                if masked:
                    dz = jnp.where(visible, dz, 0.0)
                dzb.append(dz.astype(BF16))
            out = []
            for h in heads:
                out += [carry[3 * h] + jnp.sum(lk[h], axis=-1, keepdims=True),
                        carry[3 * h + 1] + jnp.sum(g[h], axis=-1, keepdims=True),
                        carry[3 * h + 2] + _dot(dzb[h], kb)]
            dk_acc[pl.ds(s0, tk), :] += _dot_tn(dzb[0], q_heads[0]) + _dot_tn(dzb[1], q_heads[1])
            dv_acc[pl.ds(s0, tk), :] += (_dot_tn(a[0].astype(BF16), do_heads[0])
                                         + _dot_tn(a[1].astype(BF16), do_heads[1]))
            return tuple(out)

        zero1, zero128 = jnp.zeros((tq, 1), F32), jnp.zeros((tq, 128), F32)
        carry = lax.fori_loop(0, qi * kpq, lambda n, c: block(n, c, False),
                              (zero1, zero1, zero128, zero1, zero1, zero128))
        for d in range(kpq):
            carry = block(qi * kpq + d, carry, True)
        dq_ref[...] = jnp.where(first, carry[2], carry[5]).astype(BF16)

        @pl.when(qi == nq - 1)
        def _():
            dk_ref[...] = dk_acc[...].astype(BF16)
            dv_ref[...] = dv_acc[...].astype(BF16)

    blk = pl.BlockSpec((tq, 128), lambda hp, qi: (qi, hp))
    seq = pl.BlockSpec((S, 128), lambda hp, qi: (0, hp))
    return pl.pallas_call(
        body, name="attn_bwd", grid=(D_MODEL // 128, nq),
        in_specs=[blk, seq, seq, blk, blk, blk, blk], out_specs=[blk, seq, seq, blk],
        out_shape=[jax.ShapeDtypeStruct((S, D_MODEL), BF16)] * 4,
        scratch_shapes=[pltpu.VMEM((S, 128), F32), pltpu.VMEM((S, 128), F32)],
        compiler_params=_params(("parallel", "arbitrary")),
    )(q, k, v, ltot, dgated, o, zgate)


def _rms_bwd(xv, dh_gain_sum):
    r = lax.rsqrt(jnp.mean(xv * xv, axis=-1, keepdims=True) + EPS)
    xhat = xv * r
    dx = r * (dh_gain_sum - xhat * jnp.mean(dh_gain_sum * xhat, axis=-1, keepdims=True))
    return dx, xhat


def _b_in_bwd(dq, dk, dv, dz, q_raw, k_raw, x, dx_mid, q_gain_t, k_gain_t, b_gain, kv_gain, w_in, w_kv):
    S = x.shape[0]
    tm = ROW_TILE

    def body(dq_ref, dk_ref, dv_ref, dz_ref, qr_ref, kr_ref, x_ref, dxm_ref, qg_ref, kg_ref, bg_ref, kvg_ref,
             win_ref, wkv_ref, dqz_ref, dkv_ref, dx_ref, small_ref):
        @pl.when(pl.program_id(0) == 0)
        def _():
            small_ref[...] = jnp.zeros_like(small_ref)

        bd = _head_mean_matrix()

        def head_norm_bwd(dy_ref, raw_ref, gain, scale):
            raw = raw_ref[...].astype(F32)
            rr = lax.rsqrt(_head_mean(raw * raw, bd) + EPS)
            xhat = raw * rr
            dy = dy_ref[...].astype(F32) * scale
            gdy = dy * gain
            draw = rr * (gdy - xhat * _head_mean(gdy * xhat, bd))
            return draw.astype(BF16), jnp.sum(dy * xhat, axis=0, keepdims=True)

        dqr, dqg = head_norm_bwd(dq_ref, qr_ref, qg_ref[...], SB_SCALE)
        dkr, dkg = head_norm_bwd(dk_ref, kr_ref, kg_ref[...], 1.0)
        dqz_ref[:, :D_MODEL] = dqr
        dqz_ref[:, D_MODEL:] = dz_ref[...]
        dkv_ref[:, :D_MODEL] = dkr
        dkv_ref[:, D_MODEL:] = dv_ref[...]
        dhb = jnp.zeros((tm, D_MODEL), F32)
        dhkv = jnp.zeros((tm, D_MODEL), F32)
        for sh in range(N_CHIPS):
            cols = slice(sh * 512, (sh + 1) * 512)
            dhb = dhb + _dot_nt(dqz_ref[:, cols], win_ref[sh])
            dhkv = dhkv + _dot_nt(dkv_ref[:, cols], wkv_ref[sh])
        dx, xhat = _rms_bwd(x_ref[...], dhb * bg_ref[...] + dhkv * kvg_ref[...])
        dx_ref[...] = dxm_ref[...] + dx
        small_ref[0:1, :] += dqg
        small_ref[1:2, :] += dkg
        small_ref[2:3, :] += jnp.sum(dhb * xhat, axis=0, keepdims=True)
        small_ref[3:4, :] += jnp.sum(dhkv * xhat, axis=0, keepdims=True)

    row = pl.BlockSpec((tm, D_MODEL), lambda i: (i, 0))
    wide = pl.BlockSpec((tm, 2 * D_MODEL), lambda i: (i, 0))
    vec = pl.BlockSpec((1, D_MODEL), lambda i: (0, 0))
    wsp = pl.BlockSpec((N_CHIPS, D_MODEL, 512), lambda i: (0, 0, 0))
    return pl.pallas_call(
        body, name="b_in_bwd", grid=(S // tm,),
        in_specs=[row] * 8 + [vec] * 4 + [wsp, wsp],
        out_specs=[wide, wide, row, pl.BlockSpec((8, D_MODEL), lambda i: (0, 0))],
        out_shape=[jax.ShapeDtypeStruct((S, 2 * D_MODEL), BF16), jax.ShapeDtypeStruct((S, 2 * D_MODEL), BF16),
                   jax.ShapeDtypeStruct((S, D_MODEL), F32), jax.ShapeDtypeStruct((8, D_MODEL), F32)],
        compiler_params=_params(("arbitrary",), 56),
    )(dq, dk, dv, dz, q_raw, k_raw, x, dx_mid, q_gain_t, k_gain_t, b_gain, kv_gain, w_in, w_kv)


def _a_mix_bwd(dgated, uz, pooled, wg, scale, w_in, x, dx_mid, gain):
    S = x.shape[0]
    tm = ROW_TILE
    n = S // tm

    def body(dg_ref, z_ref, p_ref, wg_ref, sc_ref, win_ref, x_ref, dxm_ref, gn_ref,
             duz_ref, dmr_ref, dx_ref, small_ref, halo_hi, halo_lo):
        i = pl.program_id(0)

        @pl.when(i == 0)
        def _():
            small_ref[...] = jnp.zeros_like(small_ref)
            halo_hi[...] = jnp.zeros_like(halo_hi)
            halo_lo[...] = jnp.zeros_like(halo_lo)

        first_row = (n - 1 - i) * tm
        row = lax.broadcasted_iota(jnp.int32, (tm, tm), 0)
        col = lax.broadcasted_iota(jnp.int32, (tm, tm), 1)
        d = col - row
        for g, w in enumerate(POOL_WINDOWS):
            cols = slice(g * GROUP_DIM, (g + 1) * GROUP_DIM)
            wgg = _group_weight(wg_ref, g)
            sc = sc_ref[:, cols]
            mraw = _dot(p_ref[:, cols], wgg)
            z = z_ref[:, cols]
            sg = _sigmoid(z)
            dga = dg_ref[:, cols].astype(F32)
            dm = dga * (z * sg)
            duz_ref[:, D_MODEL + g * GROUP_DIM:D_MODEL + (g + 1) * GROUP_DIM] = (
                dga * (mraw * sc) * _dsilu(z, sg)).astype(BF16)
            small_ref[0:1, cols] += jnp.sum(dm * mraw, axis=0, keepdims=True)
            dmr = (dm * sc).astype(BF16)
            dmr_ref[:, cols] = dmr
            dp = _dot_nt(dmr, wgg)
            hi, lo = _hilo(dp * _inv_count(first_row, tm, w))
            t_main = _mask_bf16((d >= 0) & (d < w))
            t_halo = _mask_bf16(d + tm < w)
            du = (_dot(t_main, hi) + _dot(t_main, lo) + _dot(t_halo, halo_hi[:, cols]) + _dot(t_halo, halo_lo[:, cols])
                  - dp)
            halo_hi[:, cols] = hi
            halo_lo[:, cols] = lo
            duz_ref[:, cols] = du.astype(BF16)
        dh = jnp.zeros((tm, D_MODEL), F32)
        for sh in range(N_CHIPS):
            dh = dh + _dot_nt(duz_ref[:, sh * 512:(sh + 1) * 512], win_ref[sh])
        dx, xhat = _rms_bwd(x_ref[...], dh * gn_ref[...])
        dx_ref[...] = dxm_ref[...] + dx
        small_ref[1:2, :] += jnp.sum(dh * xhat, axis=0, keepdims=True)

    rev = lambda i: (n - 1 - i, 0)
    row = pl.BlockSpec((tm, D_MODEL), rev)
    vec = pl.BlockSpec((1, D_MODEL), lambda i: (0, 0))
    return pl.pallas_call(
        body, name="a_mix_bwd", grid=(n,),
        in_specs=[row,
                  pl.BlockSpec((tm, D_MODEL), lambda i: (n - 1 - i, 1)),
                  row,
                  pl.BlockSpec((N_CHIPS, N_GROUPS, 64, GROUP_DIM), lambda i: (0, 0, 0, 0)),
                  vec,
                  pl.BlockSpec((N_CHIPS, D_MODEL, 512), lambda i: (0, 0, 0)),
                  row, row, vec],
        out_specs=[pl.BlockSpec((tm, 2 * D_MODEL), rev), row, row,
                   pl.BlockSpec((8, D_MODEL), lambda i: (0, 0))],
        out_shape=[jax.ShapeDtypeStruct((S, 2 * D_MODEL), BF16), jax.ShapeDtypeStruct((S, D_MODEL), BF16),
                   jax.ShapeDtypeStruct((S, D_MODEL), F32), jax.ShapeDtypeStruct((8, D_MODEL), F32)],
        scratch_shapes=[pltpu.VMEM((tm, D_MODEL), BF16), pltpu.VMEM((tm, D_MODEL), BF16)],
        compiler_params=_params(("arbitrary",)),
    )(dgated, uz, pooled, wg, scale, w_in, x, dx_mid, gain)


def _wgrad(name, a, dy, n_shards, a_spec=None, k_dim=None):
    S = dy.shape[0]
    ts = WGRAD_SEQ_TILE
    k_dim = a.shape[-1] if k_dim is None else k_dim
    wn = dy.shape[1] // n_shards
    tk = k_dim if k_dim * wn * 4 <= 2 * MIB else 512
    nst = S // ts

    def body(a_ref, dy_ref, out_ref):
        @pl.when(pl.program_id(2) == 0)
        def _():
            out_ref[...] = jnp.zeros_like(out_ref)

        out_ref[...] += _dot_tn(a_ref[...].astype(BF16), dy_ref[...].astype(BF16))

    if a_spec is None:
        a_spec = pl.BlockSpec((ts, tk), lambda sh, kt, st: (st, kt))
    return pl.pallas_call(
        body, name=name, grid=(n_shards, k_dim // tk, nst),
        in_specs=[a_spec, pl.BlockSpec((ts, wn), lambda sh, kt, st: (st, sh))],
        out_specs=pl.BlockSpec((None, tk, wn), lambda sh, kt, st: (sh, kt, 0)),
        out_shape=jax.ShapeDtypeStruct((n_shards, k_dim, wn), F32),
        compiler_params=_params(("parallel", "parallel", "arbitrary")),
    )(a, dy)


def _wgrad_ple(name, p, layer, de):
    ts = WGRAD_SEQ_TILE
    spec = pl.BlockSpec((None, None, ts, PLE_DIM), lambda sh, kt, st: (layer, 0, st, 0))
    return _wgrad(name, p, de, N_CHIPS, a_spec=spec, k_dim=PLE_DIM)


def _wgrad_group(pooled, dmr):
    S = pooled.shape[0]
    ts = WGRAD_SEQ_TILE
    nst = S // ts

    def body(p_ref, d_ref, out_ref, acc):
        st = pl.program_id(1)

        @pl.when(st == 0)
        def _():
            acc[...] = jnp.zeros_like(acc)

        acc[...] += _dot_tn(p_ref[...], d_ref[...])

        @pl.when(st == nst - 1)
        def _():
            for sh in range(N_CHIPS):
                out_ref[sh] = acc[sh * 64:(sh + 1) * 64, :]

    blk = pl.BlockSpec((ts, GROUP_DIM), lambda g, st: (st, g))
    return pl.pallas_call(
        body, name="wgrad_group", grid=(N_GROUPS, nst),
        in_specs=[blk, blk],
        out_specs=pl.BlockSpec((N_CHIPS, None, 64, GROUP_DIM), lambda g, st: (0, g, 0, 0)),
        out_shape=jax.ShapeDtypeStruct((N_CHIPS, N_GROUPS, 64, GROUP_DIM), F32),
        scratch_shapes=[pltpu.VMEM((GROUP_DIM, GROUP_DIM), F32)],
        compiler_params=_params(("parallel", "arbitrary")),
    )(pooled, dmr)


def _local_step(x, p, target, w):
    wg4 = w["a_w_group"].reshape(N_CHIPS, N_GROUPS, 64, GROUP_DIM)
    wa_out = w["a_w_out"].reshape(D_MODEL, D_MODEL)
    wb_out = w["b_w_out"].reshape(D_MODEL, D_MODEL)
    ple_w = w["ple_w"].reshape(N_CHIPS, 2, PLE_DIM, 256)
    ple_g = w["ple_gate_w"].reshape(N_CHIPS, 2, 256, D_MODEL)
    k_gain_t = jnp.tile(w["k_norm"].reshape(1, HEAD_DIM), (1, N_HEADS))
    q_gain_t = jnp.tile(w["b_q_norm"].reshape(1, HEAD_DIM), (1, N_HEADS))

    uz, h_a = _a_in(x, w["a_norm"], w["a_w_in"])
    gated_a, pooled = _a_mix(uz, wg4, w["a_scale"])
    x1, x2, e_a, gate_a = _out_ple("a_out_ple", gated_a, x, wa_out, p, 0, ple_w, ple_g)
    h_kv, h_b, k_raw, q_raw, k, q, v, z_b = _b_in(
        x2, w["kv_norm"], w["b_norm"], k_gain_t, q_gain_t, w["w_kv"], w["b_w_in"])
    o, gated_b, ltot = _attn_fwd(q, k, v, z_b)
    x3, dx4, e_b, gate_b, loss_blk = _out_ple("b_out_ple", gated_b, x2, wb_out, p, 1, ple_w, ple_g, target=target)

    de_b, dgp_b, dx3, dgated_b = _ple_out_bwd("b_ple_out_bwd", dx4, e_b, gate_b, 1, ple_g, wb_out)
    dq, dk, dv, dz_b = _attn_bwd(q, k, v, ltot, dgated_b, o, z_b)
    dqz, dkv, dx2, small_b = _b_in_bwd(dq, dk, dv, dz_b, q_raw, k_raw, x2, dx3, q_gain_t, k_gain_t,
                                       w["b_norm"], w["kv_norm"], w["b_w_in"], w["w_kv"])
    de_a, dgp_a, dx1, dgated_a = _ple_out_bwd("a_ple_out_bwd", dx2, e_a, gate_a, 0, ple_g, wa_out)
    duz, dmr, grad_x, small_a = _a_mix_bwd(dgated_a, uz, pooled, wg4, w["a_scale"], w["a_w_in"], x, dx1, w["a_norm"])

    grads = {
        "a_w_in": _wgrad("wgrad_a_in", h_a, duz, N_CHIPS),
        "a_w_group": _wgrad_group(pooled, dmr).reshape(N_CHIPS, N_GROUPS * 64, GROUP_DIM),
        "a_w_out": _wgrad("wgrad_a_out", gated_a, dx1, 1).reshape(N_CHIPS, 256, D_MODEL),
        "w_kv": _wgrad("wgrad_kv", h_kv, dkv, N_CHIPS),
        "b_w_in": _wgrad("wgrad_b_in", h_b, dqz, N_CHIPS),
        "b_w_out": _wgrad("wgrad_b_out", gated_b, dx3, 1).reshape(N_CHIPS, 256, D_MODEL),
        "ple_w": jnp.concatenate([_wgrad_ple("wgrad_ple0", p, 0, de_a), _wgrad_ple("wgrad_ple1", p, 1, de_b)], axis=1),
        "ple_gate_w": jnp.concatenate(
            [_wgrad("wgrad_gate0", x1, dgp_a, 1).reshape(N_CHIPS, 256, D_MODEL),
             _wgrad("wgrad_gate1", x3, dgp_b, 1).reshape(N_CHIPS, 256, D_MODEL)], axis=1),
    }
    fold = lambda row: jnp.pad(row.reshape(N_HEADS, HEAD_DIM).sum(axis=0), (0, D_MODEL - HEAD_DIM))
    small = jnp.stack([small_a[1], small_a[0], small_b[3], small_b[2], fold(small_b[1]), fold(small_b[0]),
                       jnp.zeros((D_MODEL,), F32), jnp.zeros((D_MODEL,), F32)])
    return loss_blk[0, 0], grad_x, grads, small


def _mesh_place():
    x, y, c = lax.axis_index("x"), lax.axis_index("y"), lax.axis_index("c")
    other_chips = [(1 - x, y), (x, 1 - y), (1 - x, 1 - y)]
    return x, y, c, other_chips


def _allgather_weights(shards, small):
    n = len(shards)

    def body(*refs):
        ins, small_in = refs[:n], refs[n]
        outs, small_out = refs[n + 1:2 * n + 1], refs[2 * n + 1]
        cast = refs[2 * n + 2:3 * n + 2]
        send_far, recv_far, send_sib, recv_sib, send_small, recv_small, local_sem = refs[3 * n + 2:]
        x, y, c, chips = _mesh_place()
        me = 2 * x + y
        sibling = (x, y, 1 - c)

        def half(k, which):
            rows = ins[k].shape[0] // 2
            return pl.ds(pl.multiple_of(which * rows, 16), rows)

        local = []
        for k in range(n):
            cast[k][...] = ins[k][...].astype(BF16)
            local.append(pltpu.make_async_copy(cast[k], outs[k].at[me], local_sem.at[k]))
            local[-1].start()
        local.append(pltpu.make_async_copy(small_in, small_out.at[me], local_sem.at[n]))
        local[-1].start()

        sends = []
        for j, (px, py) in enumerate(chips):
            for k in range(n):
                cp = pltpu.make_async_remote_copy(
                    src_ref=cast[k].at[half(k, c)], dst_ref=outs[k].at[me, half(k, c)],
                    send_sem=send_far.at[j * n + k], recv_sem=recv_far.at[j * n + k],
                    device_id=(px, py, c), device_id_type=MESH)
                cp.start()
                sends.append(cp)
            cp = pltpu.make_async_remote_copy(
                src_ref=small_in, dst_ref=small_out.at[me], send_sem=send_small.at[j], recv_sem=recv_small.at[j],
                device_id=(px, py, c), device_id_type=MESH)
            cp.start()
            sends.append(cp)

        def landed(j, k, which, sems_s, sems_r, device):
            px, py = chips[j]
            piece = outs[k].at[2 * px + py, half(k, which)]
            return pltpu.make_async_remote_copy(
                src_ref=piece, dst_ref=piece, send_sem=sems_s.at[j * n + k], recv_sem=sems_r.at[j * n + k],
                device_id=device, device_id_type=MESH)

        for j in range(len(chips)):
            for k in range(n):
                landed(j, k, c, send_far, recv_far, sibling).wait_recv()
                cp = landed(j, k, c, send_sib, recv_sib, sibling)
                cp.start()
                sends.append(cp)
        for j, (px, py) in enumerate(chips):
            for k in range(n):
                landed(j, k, 1 - c, send_sib, recv_sib, sibling).wait_recv()
            pltpu.make_async_remote_copy(
                src_ref=small_in, dst_ref=small_out.at[2 * px + py], send_sem=send_small.at[j],
                recv_sem=recv_small.at[j], device_id=(px, py, c), device_id_type=MESH).wait_recv()
        for cp in sends:
            cp.wait_send()
        for cp in local:
            cp.wait()

    vmem = pl.BlockSpec(memory_space=pltpu.VMEM)
    hbm = pl.BlockSpec(memory_space=pltpu.HBM)
    return pl.pallas_call(
        body, name="allgather_weights",
        in_specs=[vmem] * (n + 1), out_specs=[hbm] * (n + 1),
        out_shape=[jax.ShapeDtypeStruct((N_CHIPS,) + s.shape, BF16) for s in shards]
        + [jax.ShapeDtypeStruct((N_CHIPS,) + small.shape, F32)],
        scratch_shapes=[pltpu.VMEM(s.shape, BF16) for s in shards]
        + [pltpu.SemaphoreType.DMA((3 * n,)), pltpu.SemaphoreType.DMA((3 * n,)),
           pltpu.SemaphoreType.DMA((3 * n,)), pltpu.SemaphoreType.DMA((3 * n,)),
           pltpu.SemaphoreType.DMA((3,)), pltpu.SemaphoreType.DMA((3,)),
           pltpu.SemaphoreType.DMA((n + 1,))],
        compiler_params=_params(None, 40),
    )(*shards, small)


def _adamw(w, g, m, v):
    m = ADAM_B1 * m + (1.0 - ADAM_B1) * g
    v = ADAM_B2 * v + (1.0 - ADAM_B2) * (g * g)
    m_hat = m / (1.0 - ADAM_B1 ** ADAM_STEP)
    v_hat = v / (1.0 - ADAM_B2 ** ADAM_STEP)
    delta = -ADAM_LR * (m_hat / (jnp.sqrt(v_hat) + ADAM_EPS) + ADAM_WD * w)
    return delta, m, v


def _reduce_adam(name, grad, w, m, v):
    _, R, C = grad.shape
    hr = R // 2

    def body(g_ref, w_ref, m_ref, v_ref, go_ref, d_ref, mo_ref, vo_ref,
             to_sib, from_sib, part, to_far, from_far, send_sem, recv_sem):
        x, y, c, chips = _mesh_place()
        me = 2 * x + y
        sibling = (x, y, 1 - c)
        mine = pl.ds(pl.multiple_of(c * hr, 16), hr)
        other = pl.ds(pl.multiple_of((1 - c) * hr, 16), hr)

        to_sib[...] = g_ref[:, other, :].astype(BF16)
        swap = pltpu.make_async_remote_copy(src_ref=to_sib, dst_ref=from_sib, send_sem=send_sem.at[0],
                                            recv_sem=recv_sem.at[0], device_id=sibling, device_id_type=MESH)
        swap.start()
        swap.wait_recv()
        part[...] = g_ref[:, mine, :] + from_sib[...].astype(F32)

        far = []
        for j, (px, py) in enumerate(chips):
            to_far[j] = part[2 * px + py].astype(BF16)
            cp = pltpu.make_async_remote_copy(src_ref=to_far.at[j], dst_ref=from_far.at[j], send_sem=send_sem.at[1 + j],
                                              recv_sem=recv_sem.at[1 + j], device_id=(px, py, c), device_id_type=MESH)
            cp.start()
            far.append(cp)
        total = part[me]
        for j, cp in enumerate(far):
            cp.wait_recv()
            total = total + from_far[j].astype(F32)

        go_ref[mine, :] = total
        back = pltpu.make_async_remote_copy(src_ref=go_ref.at[mine], dst_ref=go_ref.at[mine], send_sem=send_sem.at[4],
                                            recv_sem=recv_sem.at[4], device_id=sibling, device_id_type=MESH)
        back.start()
        back.wait_recv()
        delta, m_new, v_new = _adamw(w_ref[...], go_ref[...], m_ref[...], v_ref[...])
        d_ref[...] = delta
        mo_ref[...] = m_new
        vo_ref[...] = v_new
        swap.wait_send()
        for cp in far:
            cp.wait_send()
        back.wait_send()

    vmem = pl.BlockSpec(memory_space=pltpu.VMEM)
    return pl.pallas_call(
        body, name=name,
        in_specs=[vmem] * 4, out_specs=[vmem] * 4,
        out_shape=[jax.ShapeDtypeStruct((R, C), F32)] * 4,
        scratch_shapes=[pltpu.VMEM((N_CHIPS, hr, C), BF16), pltpu.VMEM((N_CHIPS, hr, C), BF16),
                        pltpu.VMEM((N_CHIPS, hr, C), F32),
                        pltpu.VMEM((3, hr, C), BF16), pltpu.VMEM((3, hr, C), BF16),
                        pltpu.SemaphoreType.DMA((5,)), pltpu.SemaphoreType.DMA((5,))],
        compiler_params=_params(None, 56),
    )(grad, w, m, v)


def _allreduce_small(part):
    n_dev = 8

    def body(part_ref, out_ref, buf, send_sem, recv_sem):
        x, y, c, _ = _mesh_place()
        me = 4 * x + 2 * y + c
        buf[me] = part_ref[...]
        sends = []
        for k in range(1, n_dev):
            peer = ((1 - x) if k & 4 else x, (1 - y) if k & 2 else y, (1 - c) if k & 1 else c)
            cp = pltpu.make_async_remote_copy(src_ref=part_ref, dst_ref=buf.at[me], send_sem=send_sem.at[k - 1],
                                              recv_sem=recv_sem.at[k - 1], device_id=peer, device_id_type=MESH)
            cp.start()
            sends.append(cp)
        for cp in sends:
            cp.wait_recv()
        total = buf[0]
        for s in range(1, n_dev):
            total = total + buf[s]
        out_ref[...] = total
        for cp in sends:
            cp.wait_send()

    vmem = pl.BlockSpec(memory_space=pltpu.VMEM)
    return pl.pallas_call(
        body, name="allreduce_small", in_specs=[vmem], out_specs=vmem,
        out_shape=jax.ShapeDtypeStruct(part.shape, F32),
        scratch_shapes=[pltpu.VMEM((n_dev,) + part.shape, F32),
                        pltpu.SemaphoreType.DMA((n_dev - 1,)), pltpu.SemaphoreType.DMA((n_dev - 1,))],
    )(part)


def _adam_small(w, g, m, v):
    def body(w_ref, g_ref, m_ref, v_ref, d_ref, mo_ref, vo_ref):
        delta, m_new, v_new = _adamw(w_ref[...], g_ref[...], m_ref[...], v_ref[...])
        d_ref[...] = delta
        mo_ref[...] = m_new
        vo_ref[...] = v_new

    vmem = pl.BlockSpec(memory_space=pltpu.VMEM)
    return pl.pallas_call(
        body, name="adam_small", in_specs=[vmem] * 4, out_specs=[vmem] * 3,
        out_shape=[jax.ShapeDtypeStruct(w.shape, F32)] * 3,
    )(w, g, m, v)


BIG = ("a_w_in", "a_w_group", "a_w_out", "w_kv", "b_w_in", "b_w_out", "ple_w", "ple_gate_w")
SMALL = ("a_norm", "a_scale", "kv_norm", "b_norm", "k_norm", "b_q_norm")
SMALL_SHARDED = ("a_norm", "a_scale")
WEIGHTS = ("a_norm", "a_w_in", "a_w_group", "a_scale", "a_w_out", "kv_norm", "w_kv", "k_norm", "b_norm", "b_w_in",
           "b_q_norm", "b_w_out", "ple_w", "ple_gate_w")


def _as_matrix(a):
    return a.reshape(-1, a.shape[-1])


def _pack_small(arrs):
    rows = [jnp.pad(a.reshape(-1), (0, D_MODEL - a.size)) for a in arrs]
    rows += [jnp.zeros((D_MODEL,), F32)] * (8 - len(rows))
    return jnp.stack(rows)


def kernel(x, p, a_norm, a_w_in, a_w_group, a_scale, a_w_out, kv_norm, w_kv, k_norm, b_norm, b_w_in, b_q_norm, b_w_out, ple_w, ple_gate_w, loss_target, m_a_norm, m_a_w_in, m_a_w_group, m_a_scale, m_a_w_out, m_kv_norm, m_w_kv, m_k_norm, m_b_norm, m_b_w_in, m_b_q_norm, m_b_w_out, m_ple_w, m_ple_gate_w, v_a_norm, v_a_w_in, v_a_w_group, v_a_scale, v_a_w_out, v_kv_norm, v_w_kv, v_k_norm, v_b_norm, v_b_w_in, v_b_q_norm, v_b_w_out, v_ple_w, v_ple_gate_w):
    wts = dict(a_norm=a_norm, a_w_in=a_w_in, a_w_group=a_w_group, a_scale=a_scale, a_w_out=a_w_out, kv_norm=kv_norm,
               w_kv=w_kv, k_norm=k_norm, b_norm=b_norm, b_w_in=b_w_in, b_q_norm=b_q_norm, b_w_out=b_w_out,
               ple_w=ple_w, ple_gate_w=ple_gate_w)
    mom = dict(a_norm=m_a_norm, a_w_in=m_a_w_in, a_w_group=m_a_w_group, a_scale=m_a_scale, a_w_out=m_a_w_out,
               kv_norm=m_kv_norm, w_kv=m_w_kv, k_norm=m_k_norm, b_norm=m_b_norm, b_w_in=m_b_w_in,
               b_q_norm=m_b_q_norm, b_w_out=m_b_w_out, ple_w=m_ple_w, ple_gate_w=m_ple_gate_w)
    var = dict(a_norm=v_a_norm, a_w_in=v_a_w_in, a_w_group=v_a_w_group, a_scale=v_a_scale, a_w_out=v_a_w_out,
               kv_norm=v_kv_norm, w_kv=v_w_kv, k_norm=v_k_norm, b_norm=v_b_norm, b_w_in=v_b_w_in,
               b_q_norm=v_b_q_norm, b_w_out=v_b_w_out, ple_w=v_ple_w, ple_gate_w=v_ple_gate_w)
    S = x.shape[1]
    chip = 2 * lax.axis_index("x") + lax.axis_index("y")

    sharded_small = jnp.concatenate([a_norm.reshape(1, 256), a_scale.reshape(1, 256), jnp.zeros((6, 256), F32)], axis=0)
    gathered = _allgather_weights([_as_matrix(wts[n]) for n in BIG], sharded_small)
    full = dict(zip(BIG, gathered[:-1]))
    full["a_norm"] = gathered[-1][:, 0, :].reshape(1, D_MODEL)
    full["a_scale"] = gathered[-1][:, 1, :].reshape(1, D_MODEL)
    full["kv_norm"] = kv_norm.reshape(1, D_MODEL)
    full["b_norm"] = b_norm.reshape(1, D_MODEL)
    full["k_norm"] = k_norm
    full["b_q_norm"] = b_q_norm

    loss_local, grad_x, grads, small_part = _local_step(x.reshape(S, D_MODEL), p, loss_target.reshape(S, D_MODEL), full)
    loss = lax.psum(loss_local, ("x", "y", "c"))

    out_g, out_d, out_m, out_v = {}, {}, {}, {}
    for n in BIG:
        shape = wts[n].shape
        g, d, m_new, v_new = _reduce_adam("reduce_adam_" + n, grads[n], _as_matrix(wts[n]), _as_matrix(mom[n]),
                                          _as_matrix(var[n]))
        out_g[n], out_d[n], out_m[n], out_v[n] = (t.reshape(shape) for t in (g, d, m_new, v_new))

    small_sum = _allreduce_small(small_part)
    small_rows = []
    for i, n in enumerate(SMALL):
        row = small_sum[i]
        if n in SMALL_SHARDED:
            row = lax.dynamic_slice(row, (chip * 256,), (256,))
        else:
            row = row[:wts[n].size]
        small_rows.append(row)
    g_small = _pack_small(small_rows)
    d_small, m_small, v_small = _adam_small(_pack_small([wts[n] for n in SMALL]), g_small,
                                            _pack_small([mom[n] for n in SMALL]), _pack_small([var[n] for n in SMALL]))
    for i, n in enumerate(SMALL):
        shape, size = wts[n].shape, wts[n].size
        out_g[n], out_d[n], out_m[n], out_v[n] = (t[i, :size].reshape(shape) for t in (g_small, d_small, m_small, v_small))

    return (loss, grad_x.reshape(1, S, D_MODEL), *[out_g[n] for n in WEIGHTS], *[out_d[n] for n in WEIGHTS],
            *[out_m[n] for n in WEIGHTS], *[out_v[n] for n in WEIGHTS])
```

```python
import functools

import jax
import jax.numpy as jnp
from jax import lax
from jax.experimental import pallas as pl
from jax.experimental.pallas import tpu as pltpu

F32 = jnp.float32
BF16 = jnp.bfloat16
MESH = pl.DeviceIdType.MESH

D_MODEL = 1024
N_HEADS = 16
HEAD_DIM = 64
PLE_DIM = 256
N_GROUPS = 4
GROUP_DIM = 256
POOL_WINDOWS = (2, 4, 8, 16)
N_CHIPS = 4
EPS = 1e-6
SB_SCALE = HEAD_DIM ** -0.5

ADAM_LR = 0.001
ADAM_B1 = 0.9
ADAM_B2 = 0.999
ADAM_EPS = 1e-08
ADAM_WD = 0.01
ADAM_STEP = 10

ROW_TILE = 256
ATT_Q_TILE = 512
ATT_K_TILE = 256
WGRAD_SEQ_TILE = 512
MIB = 1024 * 1024


def _params(semantics=None, vmem_mib=48):
    return pltpu.CompilerParams(dimension_semantics=semantics, vmem_limit_bytes=vmem_mib * MIB)


def _dot(a, b):
    return jnp.dot(a, b, preferred_element_type=F32)


def _dot_nt(a, b):
    return lax.dot_general(a, b, (((1,), (1,)), ((), ())), preferred_element_type=F32)


def _dot_tn(a, b):
    return lax.dot_general(a, b, (((0,), (0,)), ((), ())), preferred_element_type=F32)


def _hilo(x):
    hi = x.astype(BF16)
    lo = (x - hi.astype(F32)).astype(BF16)
    return hi, lo


def _dot_hilo(x, w):
    hi, lo = _hilo(x)
    return _dot(hi, w) + _dot(lo, w)


def _sigmoid(z):
    return jax.nn.sigmoid(z)


def _dsilu(z, sg):
    return sg * (1.0 + z * (1.0 - sg))


def _mask_bf16(cond):
    return jnp.where(cond, 1.0, 0.0).astype(BF16)


def _head_mean_matrix():
    r = lax.broadcasted_iota(jnp.int32, (256, 256), 0) // HEAD_DIM
    c = lax.broadcasted_iota(jnp.int32, (256, 256), 1) // HEAD_DIM
    return _mask_bf16(r == c)


def _head_mean(x, bd):
    parts = []
    for s in range(x.shape[1] // 256):
        parts.append(_dot_hilo(x[:, s * 256:(s + 1) * 256], bd))
    out = parts[0] if len(parts) == 1 else jnp.concatenate(parts, axis=1)
    return out * (1.0 / HEAD_DIM)


def _a_in(x, gain, w_sh):
    S = x.shape[0]
    tm = 512
    nsh, _, wn = w_sh.shape

    def body(x_ref, g_ref, w_ref, uz_ref, h_ref):
        @pl.when(pl.program_id(1) == 0)
        def _():
            xv = x_ref[...]
            r = lax.rsqrt(jnp.mean(xv * xv, axis=-1, keepdims=True) + EPS)
            h_ref[...] = (xv * r * g_ref[...]).astype(BF16)

        uz_ref[...] = _dot(h_ref[...], w_ref[0])

    return pl.pallas_call(
        body, name="a_in", grid=(S // tm, nsh),
        in_specs=[pl.BlockSpec((tm, D_MODEL), lambda i, j: (i, 0)),
                  pl.BlockSpec((1, D_MODEL), lambda i, j: (0, 0)),
                  pl.BlockSpec((1, D_MODEL, wn), lambda i, j: (j, 0, 0))],
        out_specs=[pl.BlockSpec((tm, wn), lambda i, j: (i, j)),
                   pl.BlockSpec((tm, D_MODEL), lambda i, j: (i, 0))],
        out_shape=[jax.ShapeDtypeStruct((S, nsh * wn), F32),
                   jax.ShapeDtypeStruct((S, D_MODEL), BF16)],
        compiler_params=_params(("parallel", "arbitrary")),
    )(x, gain, w_sh)


def _inv_count(first_row, rows, w):
    t1 = first_row + 1 + lax.broadcasted_iota(jnp.int32, (rows, 1), 0)
    return 1.0 / jnp.minimum(t1, w).astype(F32)


def _group_weight(wg_ref, g):
    return jnp.concatenate([wg_ref[sh, g] for sh in range(N_CHIPS)], axis=0)


def _a_mix(uz, wg, scale):
    S = uz.shape[0]
    tm = ROW_TILE

    def body(u_ref, up_ref, z_ref, wg_ref, sc_ref, ga_ref, p_ref):
        i = pl.program_id(0)
        row = lax.broadcasted_iota(jnp.int32, (tm, tm), 0)
        col = lax.broadcasted_iota(jnp.int32, (tm, tm), 1)
        d = row - col
        for g, w in enumerate(POOL_WINDOWS):
            cols = slice(g * GROUP_DIM, (g + 1) * GROUP_DIM)
            t_main = _mask_bf16((d >= 0) & (d < w))
            t_halo = _mask_bf16(d + tm < w)
            u = u_ref[:, cols]
            up = jnp.where(i > 0, up_ref[:, cols], 0.0)
            hi, lo = _hilo(u)
            hip, lop = _hilo(up)
            wsum = _dot(t_main, hi) + _dot(t_main, lo) + _dot(t_halo, hip) + _dot(t_halo, lop)
            pooled = (wsum * _inv_count(i * tm, tm, w) - u).astype(BF16)
            p_ref[:, cols] = pooled
            mraw = _dot(pooled, _group_weight(wg_ref, g))
            z = z_ref[:, cols]
            ga_ref[:, cols] = (mraw * sc_ref[:, cols] * (z * _sigmoid(z))).astype(BF16)

    return pl.pallas_call(
        body, name="a_mix", grid=(S // tm,),
        in_specs=[pl.BlockSpec((tm, D_MODEL), lambda i: (i, 0)),
                  pl.BlockSpec((tm, D_MODEL), lambda i: (jnp.maximum(i - 1, 0), 0)),
                  pl.BlockSpec((tm, D_MODEL), lambda i: (i, 1)),
                  pl.BlockSpec((N_CHIPS, N_GROUPS, 64, GROUP_DIM), lambda i: (0, 0, 0, 0)),
                  pl.BlockSpec((1, D_MODEL), lambda i: (0, 0))],
        out_specs=[pl.BlockSpec((tm, D_MODEL), lambda i: (i, 0)),
                   pl.BlockSpec((tm, D_MODEL), lambda i: (i, 0))],
        out_shape=[jax.ShapeDtypeStruct((S, D_MODEL), BF16),
                   jax.ShapeDtypeStruct((S, D_MODEL), BF16)],
        compiler_params=_params(("arbitrary",)),
    )(uz, uz, uz, wg, scale)


def _out_ple(name, gated, x_in, w_out, p, layer, ple_w, ple_g, target=None):
    S = x_in.shape[0]
    tm = ROW_TILE
    with_loss = target is not None

    def body(*refs):
        if with_loss:
            g_ref, x_ref, wo_ref, p_ref, pw_ref, pg_ref, t_ref, xm_ref, dx_ref, e_ref, gt_ref, loss_ref = refs
        else:
            g_ref, x_ref, wo_ref, p_ref, pw_ref, pg_ref, xm_ref, xo_ref, e_ref, gt_ref = refs
        xm = x_ref[...] + _dot(g_ref[...], wo_ref[...])
        xm_ref[...] = xm
        pb = p_ref[...].astype(BF16)
        e = jnp.concatenate([_dot(pb, pw_ref[sh]) for sh in range(N_CHIPS)], axis=1)
        pg = jnp.concatenate([pg_ref[sh] for sh in range(N_CHIPS)], axis=0)
        gate = _sigmoid(_dot(xm.astype(BF16), pg))
        e_ref[...] = e.astype(BF16)
        gt_ref[...] = gate.astype(BF16)
        xo = xm + e * gate
        if with_loss:
            diff = xo - t_ref[...]
            dx_ref[...] = diff * (1.0 / D_MODEL)

            @pl.when(pl.program_id(0) == 0)
            def _():
                loss_ref[...] = jnp.zeros_like(loss_ref)

            loss_ref[...] += jnp.sum(diff * diff) * (0.5 / D_MODEL)
        else:
            xo_ref[...] = xo

    row = pl.BlockSpec((tm, D_MODEL), lambda i: (i, 0))
    in_specs = [row, row,
                pl.BlockSpec((D_MODEL, D_MODEL), lambda i: (0, 0)),
                pl.BlockSpec((None, None, tm, PLE_DIM), lambda i: (layer, 0, i, 0)),
                pl.BlockSpec((N_CHIPS, None, PLE_DIM, 256), lambda i: (0, layer, 0, 0)),
                pl.BlockSpec((N_CHIPS, None, 256, D_MODEL), lambda i: (0, layer, 0, 0))]
    args = [gated, x_in, w_out, p, ple_w, ple_g]
    out_specs = [row, row, row, row]
    out_shape = [jax.ShapeDtypeStruct((S, D_MODEL), F32), jax.ShapeDtypeStruct((S, D_MODEL), F32),
                 jax.ShapeDtypeStruct((S, D_MODEL), BF16), jax.ShapeDtypeStruct((S, D_MODEL), BF16)]
    if with_loss:
        in_specs.append(row)
        args.append(target)
        out_specs.append(pl.BlockSpec((8, 128), lambda i: (0, 0)))
        out_shape.append(jax.ShapeDtypeStruct((8, 128), F32))
    return pl.pallas_call(
        body, name=name, grid=(S // tm,), in_specs=in_specs, out_specs=out_specs, out_shape=out_shape,
        compiler_params=_params(("arbitrary",)),
    )(*args)


def _b_in(x, kv_gain, b_gain, k_gain_t, q_gain_t, w_kv, w_in):
    S = x.shape[0]
    tm = ROW_TILE

    def body(x_ref, kvg_ref, bg_ref, kg_ref, qg_ref, wkv_ref, win_ref,
             hkv_ref, hb_ref, kraw_ref, qraw_ref, k_ref, q_ref, v_ref, z_ref):
        xv = x_ref[...]
        y = xv * lax.rsqrt(jnp.mean(xv * xv, axis=-1, keepdims=True) + EPS)
        hkv = (y * kvg_ref[...]).astype(BF16)
        hb = (y * bg_ref[...]).astype(BF16)
        hkv_ref[...] = hkv
        hb_ref[...] = hb
        bd = _head_mean_matrix()

        def head_norm(raw, gain):
            rr = lax.rsqrt(_head_mean(raw * raw, bd) + EPS)
            return raw * rr * gain

        for sh in range(N_CHIPS):
            kvc = _dot(hkv, wkv_ref[sh])
            qzc = _dot(hb, win_ref[sh])
            cols = slice((sh % 2) * 512, (sh % 2) * 512 + 512)
            if sh < 2:
                kraw_ref[:, cols] = kvc.astype(BF16)
                qraw_ref[:, cols] = qzc.astype(BF16)
                k_ref[:, cols] = head_norm(kvc, kg_ref[:, cols]).astype(BF16)
                q_ref[:, cols] = (head_norm(qzc, qg_ref[:, cols]) * SB_SCALE).astype(BF16)
            else:
                v_ref[:, cols] = kvc.astype(BF16)
                z_ref[:, cols] = qzc.astype(BF16)

    row = pl.BlockSpec((tm, D_MODEL), lambda i: (i, 0))
    vec = pl.BlockSpec((1, D_MODEL), lambda i: (0, 0))
    wsp = pl.BlockSpec((N_CHIPS, D_MODEL, 512), lambda i: (0, 0, 0))
    return pl.pallas_call(
        body, name="b_in", grid=(S // tm,),
        in_specs=[row, vec, vec, vec, vec, wsp, wsp],
        out_specs=[row] * 8,
        out_shape=[jax.ShapeDtypeStruct((S, D_MODEL), BF16)] * 8,
        compiler_params=_params(("arbitrary",), 56),
    )(x, kv_gain, b_gain, k_gain_t, q_gain_t, w_kv, w_in)


def _softplus_parts(z):
    e = jnp.exp(-jnp.abs(z))
    return -(jnp.maximum(z, 0.0) + jnp.log(1.0 + e)), e


def _attn_fwd(q, k, v, zgate):
    S = q.shape[0]
    tq, tk = ATT_Q_TILE, ATT_K_TILE
    kpq = tq // tk

    def body(q_ref, k_ref, v_ref, z_ref, o_ref, g_ref, lt_ref):
        qi = pl.program_id(1)
        lane = lax.broadcasted_iota(jnp.int32, (1, 128), 1)
        ri = lax.broadcasted_iota(jnp.int32, (tk, tk), 0)
        ci = lax.broadcasted_iota(jnp.int32, (tk, tk), 1)
        later_mat = _mask_bf16(ri > ci)
        t_idx = qi * tq + lax.broadcasted_iota(jnp.int32, (tq, tk), 0)
        s_off = lax.broadcasted_iota(jnp.int32, (tq, tk), 1)
        qv = q_ref[...]
        first = lane < HEAD_DIM
        q_heads = (jnp.where(first, qv, jnp.zeros_like(qv)), jnp.where(first, jnp.zeros_like(qv), qv))

        def step(kj_last, carry, masked):
            chains = [(d, h) for d in range(kpq) for h in range(2)]
            s0 = [pl.multiple_of((kj_last - d) * tk, tk) for d in range(kpq)]
            kb = [k_ref[pl.ds(s, tk), :] for s in s0]
            vb = [v_ref[pl.ds(s, tk), :] for s in s0]
            visible = [s + s_off < t_idx for s in s0] if masked else None
            z = {c: _dot_nt(q_heads[c[1]], kb[c[0]]) for c in chains}
            run = [carry[0], carry[2]]
            log_own, later, run_at = {}, {}, {}
            for c in chains:
                kj, h = c
                lk = _softplus_parts(z[c])[0]
                if masked:
                    lk = jnp.where(visible[kj], lk, 0.0)
                log_own[c] = z[c] + lk
                later[c] = _dot(lk.astype(BF16), later_mat)
                run_at[c] = run[h]
                run[h] = run[h] + jnp.sum(lk, axis=-1, keepdims=True)
            acc = [carry[1], carry[3]]
            for c in chains:
                kj, h = c
                a = jnp.exp(log_own[c] + later[c] + run_at[c])
                if masked:
                    a = jnp.where(visible[kj], a, 0.0)
                acc[h] = acc[h] + _dot(a.astype(BF16), vb[kj])
            return run[0], acc[0], run[1], acc[1]

        zero1, zero128 = jnp.zeros((tq, 1), F32), jnp.zeros((tq, 128), F32)
        carry = step(qi * kpq + kpq - 1, (zero1, zero128, zero1, zero128), True)
        carry = lax.fori_loop(0, qi, lambda n, c: step((qi - n) * kpq - 1, c, False), carry)
        o_tot = jnp.where(first, carry[1], carry[3])
        l_tot = jnp.where(first, carry[0], carry[2])
        o_ref[...] = o_tot.astype(BF16)
        lt_ref[...] = l_tot
        zz = z_ref[...].astype(F32)
        g_ref[...] = (o_tot * (zz * _sigmoid(zz))).astype(BF16)

    blk = pl.BlockSpec((tq, 128), lambda hp, qi: (qi, hp))
    seq = pl.BlockSpec((S, 128), lambda hp, qi: (0, hp))
    return pl.pallas_call(
        body, name="attn_fwd", grid=(D_MODEL // 128, S // tq),
        in_specs=[blk, seq, seq, blk], out_specs=[blk, blk, blk],
        out_shape=[jax.ShapeDtypeStruct((S, D_MODEL), BF16)] * 2 + [jax.ShapeDtypeStruct((S, D_MODEL), F32)],
        compiler_params=_params(("parallel", "arbitrary")),
    )(q, k, v, zgate)


def _ple_out_bwd(name, dx_out, e, gate, layer, ple_g, w_out):
    S = dx_out.shape[0]
    tm = ROW_TILE

    def body(dx_ref, e_ref, gt_ref, pg_ref, wo_ref, de_ref, dgp_ref, dxm_ref, dg_ref):
        dxo = dx_ref[...]
        ev = e_ref[...].astype(F32)
        gv = gt_ref[...].astype(F32)
        de_ref[...] = (dxo * gv).astype(BF16)
        dgp = (dxo * ev * gv * (1.0 - gv)).astype(BF16)
        dgp_ref[...] = dgp
        pg = jnp.concatenate([pg_ref[sh] for sh in range(N_CHIPS)], axis=0)
        dxm = dxo + _dot_nt(dgp, pg)
        dxm_ref[...] = dxm
        dg_ref[...] = _dot_nt(dxm.astype(BF16), wo_ref[...]).astype(BF16)

    row = pl.BlockSpec((tm, D_MODEL), lambda i: (i, 0))
    return pl.pallas_call(
        body, name=name, grid=(S // tm,),
        in_specs=[row, row, row,
                  pl.BlockSpec((N_CHIPS, None, 256, D_MODEL), lambda i: (0, layer, 0, 0)),
                  pl.BlockSpec((D_MODEL, D_MODEL), lambda i: (0, 0))],
        out_specs=[row, row, row, row],
        out_shape=[jax.ShapeDtypeStruct((S, D_MODEL), BF16), jax.ShapeDtypeStruct((S, D_MODEL), BF16),
                   jax.ShapeDtypeStruct((S, D_MODEL), F32), jax.ShapeDtypeStruct((S, D_MODEL), BF16)],
        compiler_params=_params(("arbitrary",)),
    )(dx_out, e, gate, ple_g, w_out)


def _attn_bwd(q, k, v, ltot, dgated, o, zgate):
    S = q.shape[0]
    tq, tk = ATT_Q_TILE, ATT_K_TILE
    kpq = tq // tk
    nq = S // tq

    def body(q_ref, k_ref, v_ref, lt_ref, dg_ref, o_ref, z_ref, dq_ref, dk_ref, dv_ref, dz_ref, dk_acc, dv_acc):
        qi = pl.program_id(1)

        @pl.when(qi == 0)
        def _():
            dk_acc[...] = jnp.zeros_like(dk_acc)
            dv_acc[...] = jnp.zeros_like(dv_acc)

        lane = lax.broadcasted_iota(jnp.int32, (1, 128), 1)
        ri = lax.broadcasted_iota(jnp.int32, (tk, tk), 0)
        ci = lax.broadcasted_iota(jnp.int32, (tk, tk), 1)
        later_mat = _mask_bf16(ri > ci)
        before_mat = _mask_bf16(ri < ci)
        t_idx = qi * tq + lax.broadcasted_iota(jnp.int32, (tq, tk), 0)
        s_off = lax.broadcasted_iota(jnp.int32, (tq, tk), 1)
        zz = z_ref[...].astype(F32)
        sg = _sigmoid(zz)
        dgv = dg_ref[...].astype(F32)
        dz_ref[...] = (dgv * o_ref[...].astype(F32) * _dsilu(zz, sg)).astype(BF16)
        dob = (dgv * (zz * sg)).astype(BF16)
        ltv = lt_ref[...]
        qv = q_ref[...]
        first = lane < HEAD_DIM
        masks = (first, jnp.logical_not(first))
        q_heads = [jnp.where(hm, qv, jnp.zeros_like(qv)) for hm in masks]
        do_heads = [jnp.where(hm, dob, jnp.zeros_like(dob)) for hm in masks]
        totals = [jnp.max(jnp.where(hm, ltv, -jnp.inf), axis=-1, keepdims=True) for hm in masks]

        def step(kj_first, carry, masked):
            chains = [(d, h) for d in range(kpq) for h in range(2)]
            s0 = [pl.multiple_of((kj_first + d) * tk, tk) for d in range(kpq)]
            kb = [k_ref[pl.ds(s, tk), :] for s in s0]
            vb = [v_ref[pl.ds(s, tk), :] for s in s0]
            visible = [s + s_off < t_idx for s in s0] if masked else None
            z = {c: _dot_nt(q_heads[c[1]], kb[c[0]]) for c in chains}
            da = {c: _dot_nt(do_heads[c[1]], vb[c[0]]) for c in chains}
            run = [carry[0], carry[3]]
            log_own, beta, later, base = {}, {}, {}, {}
            for c in chains:
                kj, h = c
                lk = _softplus_parts(z[c])[0]
                if masked:
                    lk = jnp.where(visible[kj], lk, 0.0)
                log_own[c] = z[c] + lk
                beta[c] = jnp.exp(log_own[c]).astype(BF16)
                later[c] = _dot(lk.astype(BF16), later_mat)
                run[h] = run[h] + jnp.sum(lk, axis=-1, keepdims=True)
                base[c] = totals[h] - run[h]
            grun = [carry[1], carry[4]]
            a_bf, g_bf, gbefore, grun_at = {}, {}, {}, {}
            for c in chains:
                kj, h = c
                a = jnp.exp(log_own[c] + later[c] + base[c])
                if masked:
                    a = jnp.where(visible[kj], a, 0.0)
                a_bf[c] = a.astype(BF16)
                g = da[c] * a
                g_bf[c] = g.astype(BF16)
                gbefore[c] = _dot(g_bf[c], before_mat)
                grun_at[c] = grun[h]
                grun[h] = grun[h] + jnp.sum(g, axis=-1, keepdims=True)
            dq = [carry[2], carry[5]]
            dk_blk = [jnp.zeros((tk, 128), F32) for _ in range(kpq)]
            dv_blk = [jnp.zeros((tk, 128), F32) for _ in range(kpq)]
            for c in chains:
                kj, h = c
                g = g_bf[c].astype(F32)
                dz = g - beta[c].astype(F32) * (g + gbefore[c] + grun_at[c])
                if masked:
                    dz = jnp.where(visible[kj], dz, 0.0)
                dzb = dz.astype(BF16)
                dq[h] = dq[h] + _dot(dzb, kb[kj])
                dk_blk[kj] = dk_blk[kj] + _dot_tn(dzb, q_heads[h])
                dv_blk[kj] = dv_blk[kj] + _dot_tn(a_bf[c], do_heads[h])
            for d in range(kpq):
                dk_acc[pl.ds(s0[d], tk), :] += dk_blk[d]
                dv_acc[pl.ds(s0[d], tk), :] += dv_blk[d]
            return run[0], grun[0], dq[0], run[1], grun[1], dq[1]

        zero1, zero128 = jnp.zeros((tq, 1), F32), jnp.zeros((tq, 128), F32)
        carry = lax.fori_loop(0, qi, lambda n, c: step(n * kpq, c, False),
                              (zero1, zero1, zero128, zero1, zero1, zero128))
        carry = step(qi * kpq, carry, True)
        dq_ref[...] = jnp.where(first, carry[2], carry[5]).astype(BF16)

        @pl.when(qi == nq - 1)
        def _():
            dk_ref[...] = dk_acc[...].astype(BF16)
            dv_ref[...] = dv_acc[...].astype(BF16)

    blk = pl.BlockSpec((tq, 128), lambda hp, qi: (qi, hp))
    seq = pl.BlockSpec((S, 128), lambda hp, qi: (0, hp))
    return pl.pallas_call(
        body, name="attn_bwd", grid=(D_MODEL // 128, nq),
        in_specs=[blk, seq, seq, blk, blk, blk, blk], out_specs=[blk, seq, seq, blk],
        out_shape=[jax.ShapeDtypeStruct((S, D_MODEL), BF16)] * 4,
        scratch_shapes=[pltpu.VMEM((S, 128), F32), pltpu.VMEM((S, 128), F32)],
        compiler_params=_params(("parallel", "arbitrary")),
    )(q, k, v, ltot, dgated, o, zgate)


def _rms_bwd(xv, dh_gain_sum):
    r = lax.rsqrt(jnp.mean(xv * xv, axis=-1, keepdims=True) + EPS)
    xhat = xv * r
    dx = r * (dh_gain_sum - xhat * jnp.mean(dh_gain_sum * xhat, axis=-1, keepdims=True))
    return dx, xhat


def _b_in_bwd(dq, dk, dv, dz, q_raw, k_raw, x, dx_mid, q_gain_t, k_gain_t, b_gain, kv_gain, w_in, w_kv):
    S = x.shape[0]
    tm = ROW_TILE

    def body(dq_ref, dk_ref, dv_ref, dz_ref, qr_ref, kr_ref, x_ref, dxm_ref, qg_ref, kg_ref, bg_ref, kvg_ref,
             win_ref, wkv_ref, dqz_ref, dkv_ref, dx_ref, small_ref):
        @pl.when(pl.program_id(0) == 0)
        def _():
            small_ref[...] = jnp.zeros_like(small_ref)

        bd = _head_mean_matrix()

        def head_norm_bwd(dy_ref, raw_ref, gain, scale):
            raw = raw_ref[...].astype(F32)
            rr = lax.rsqrt(_head_mean(raw * raw, bd) + EPS)
            xhat = raw * rr
            dy = dy_ref[...].astype(F32) * scale
            gdy = dy * gain
            draw = rr * (gdy - xhat * _head_mean(gdy * xhat, bd))
            return draw.astype(BF16), jnp.sum(dy * xhat, axis=0, keepdims=True)

        dqr, dqg = head_norm_bwd(dq_ref, qr_ref, qg_ref[...], SB_SCALE)
        dkr, dkg = head_norm_bwd(dk_ref, kr_ref, kg_ref[...], 1.0)
        dqz_ref[:, :D_MODEL] = dqr
        dqz_ref[:, D_MODEL:] = dz_ref[...]
        dkv_ref[:, :D_MODEL] = dkr
        dkv_ref[:, D_MODEL:] = dv_ref[...]
        dhb = jnp.zeros((tm, D_MODEL), F32)
        dhkv = jnp.zeros((tm, D_MODEL), F32)
        for sh in range(N_CHIPS):
            cols = slice(sh * 512, (sh + 1) * 512)
            dhb = dhb + _dot_nt(dqz_ref[:, cols], win_ref[sh])
            dhkv = dhkv + _dot_nt(dkv_ref[:, cols], wkv_ref[sh])
        dx, xhat = _rms_bwd(x_ref[...], dhb * bg_ref[...] + dhkv * kvg_ref[...])
        dx_ref[...] = dxm_ref[...] + dx
        small_ref[0:1, :] += dqg
        small_ref[1:2, :] += dkg
        small_ref[2:3, :] += jnp.sum(dhb * xhat, axis=0, keepdims=True)
        small_ref[3:4, :] += jnp.sum(dhkv * xhat, axis=0, keepdims=True)

    row = pl.BlockSpec((tm, D_MODEL), lambda i: (i, 0))
    wide = pl.BlockSpec((tm, 2 * D_MODEL), lambda i: (i, 0))
    vec = pl.BlockSpec((1, D_MODEL), lambda i: (0, 0))
    wsp = pl.BlockSpec((N_CHIPS, D_MODEL, 512), lambda i: (0, 0, 0))
    return pl.pallas_call(
        body, name="b_in_bwd", grid=(S // tm,),
        in_specs=[row] * 8 + [vec] * 4 + [wsp, wsp],
        out_specs=[wide, wide, row, pl.BlockSpec((8, D_MODEL), lambda i: (0, 0))],
        out_shape=[jax.ShapeDtypeStruct((S, 2 * D_MODEL), BF16), jax.ShapeDtypeStruct((S, 2 * D_MODEL), BF16),
                   jax.ShapeDtypeStruct((S, D_MODEL), F32), jax.ShapeDtypeStruct((8, D_MODEL), F32)],
        compiler_params=_params(("arbitrary",), 56),
    )(dq, dk, dv, dz, q_raw, k_raw, x, dx_mid, q_gain_t, k_gain_t, b_gain, kv_gain, w_in, w_kv)


def _a_mix_bwd(dgated, uz, pooled, wg, scale, w_in, x, dx_mid, gain):
    S = x.shape[0]
    tm = ROW_TILE
    n = S // tm

    def body(dg_ref, z_ref, p_ref, wg_ref, sc_ref, win_ref, x_ref, dxm_ref, gn_ref,
             duz_ref, dmr_ref, dx_ref, small_ref, halo_hi, halo_lo):
        i = pl.program_id(0)

        @pl.when(i == 0)
        def _():
            small_ref[...] = jnp.zeros_like(small_ref)
            halo_hi[...] = jnp.zeros_like(halo_hi)
            halo_lo[...] = jnp.zeros_like(halo_lo)

        first_row = (n - 1 - i) * tm
        row = lax.broadcasted_iota(jnp.int32, (tm, tm), 0)
        col = lax.broadcasted_iota(jnp.int32, (tm, tm), 1)
        d = col - row
        for g, w in enumerate(POOL_WINDOWS):
            cols = slice(g * GROUP_DIM, (g + 1) * GROUP_DIM)
            wgg = _group_weight(wg_ref, g)
            sc = sc_ref[:, cols]
            mraw = _dot(p_ref[:, cols], wgg)
            z = z_ref[:, cols]
            sg = _sigmoid(z)
            dga = dg_ref[:, cols].astype(F32)
            dm = dga * (z * sg)
            duz_ref[:, D_MODEL + g * GROUP_DIM:D_MODEL + (g + 1) * GROUP_DIM] = (
                dga * (mraw * sc) * _dsilu(z, sg)).astype(BF16)
            small_ref[0:1, cols] += jnp.sum(dm * mraw, axis=0, keepdims=True)
            dmr = (dm * sc).astype(BF16)
            dmr_ref[:, cols] = dmr
            dp = _dot_nt(dmr, wgg)
            hi, lo = _hilo(dp * _inv_count(first_row, tm, w))
            t_main = _mask_bf16((d >= 0) & (d < w))
            t_halo = _mask_bf16(d + tm < w)
            du = (_dot(t_main, hi) + _dot(t_main, lo) + _dot(t_halo, halo_hi[:, cols]) + _dot(t_halo, halo_lo[:, cols])
                  - dp)
            halo_hi[:, cols] = hi
            halo_lo[:, cols] = lo
            duz_ref[:, cols] = du.astype(BF16)
        dh = jnp.zeros((tm, D_MODEL), F32)
        for sh in range(N_CHIPS):
            dh = dh + _dot_nt(duz_ref[:, sh * 512:(sh + 1) * 512], win_ref[sh])
        dx, xhat = _rms_bwd(x_ref[...], dh * gn_ref[...])
        dx_ref[...] = dxm_ref[...] + dx
        small_ref[1:2, :] += jnp.sum(dh * xhat, axis=0, keepdims=True)

    rev = lambda i: (n - 1 - i, 0)
    row = pl.BlockSpec((tm, D_MODEL), rev)
    vec = pl.BlockSpec((1, D_MODEL), lambda i: (0, 0))
    return pl.pallas_call(
        body, name="a_mix_bwd", grid=(n,),
        in_specs=[row,
                  pl.BlockSpec((tm, D_MODEL), lambda i: (n - 1 - i, 1)),
                  row,
                  pl.BlockSpec((N_CHIPS, N_GROUPS, 64, GROUP_DIM), lambda i: (0, 0, 0, 0)),
                  vec,
                  pl.BlockSpec((N_CHIPS, D_MODEL, 512), lambda i: (0, 0, 0)),
                  row, row, vec],
        out_specs=[pl.BlockSpec((tm, 2 * D_MODEL), rev), row, row,
                   pl.BlockSpec((8, D_MODEL), lambda i: (0, 0))],
        out_shape=[jax.ShapeDtypeStruct((S, 2 * D_MODEL), BF16), jax.ShapeDtypeStruct((S, D_MODEL), BF16),
                   jax.ShapeDtypeStruct((S, D_MODEL), F32), jax.ShapeDtypeStruct((8, D_MODEL), F32)],
        scratch_shapes=[pltpu.VMEM((tm, D_MODEL), BF16), pltpu.VMEM((tm, D_MODEL), BF16)],
        compiler_params=_params(("arbitrary",)),
    )(dgated, uz, pooled, wg, scale, w_in, x, dx_mid, gain)


def _wgrad(name, a, dy, n_shards, a_spec=None, k_dim=None):
    S = dy.shape[0]
    ts = WGRAD_SEQ_TILE
    k_dim = a.shape[-1] if k_dim is None else k_dim
    wn = dy.shape[1] // n_shards
    tk = k_dim if k_dim * wn * 4 <= 2 * MIB else 512
    nst = S // ts

    def body(a_ref, dy_ref, out_ref):
        @pl.when(pl.program_id(2) == 0)
        def _():
            out_ref[...] = jnp.zeros_like(out_ref)

        out_ref[...] += _dot_tn(a_ref[...].astype(BF16), dy_ref[...].astype(BF16))

    if a_spec is None:
        a_spec = pl.BlockSpec((ts, tk), lambda sh, kt, st: (st, kt))
    return pl.pallas_call(
        body, name=name, grid=(n_shards, k_dim // tk, nst),
        in_specs=[a_spec, pl.BlockSpec((ts, wn), lambda sh, kt, st: (st, sh))],
        out_specs=pl.BlockSpec((None, tk, wn), lambda sh, kt, st: (sh, kt, 0)),
        out_shape=jax.ShapeDtypeStruct((n_shards, k_dim, wn), F32),
        compiler_params=_params(("parallel", "parallel", "arbitrary")),
    )(a, dy)


def _wgrad_ple(name, p, layer, de):
    ts = WGRAD_SEQ_TILE
    spec = pl.BlockSpec((None, None, ts, PLE_DIM), lambda sh, kt, st: (layer, 0, st, 0))
    return _wgrad(name, p, de, N_CHIPS, a_spec=spec, k_dim=PLE_DIM)


def _wgrad_group(pooled, dmr):
    S = pooled.shape[0]
    ts = WGRAD_SEQ_TILE
    nst = S // ts

    def body(p_ref, d_ref, out_ref, acc):
        st = pl.program_id(1)

        @pl.when(st == 0)
        def _():
            acc[...] = jnp.zeros_like(acc)

        acc[...] += _dot_tn(p_ref[...], d_ref[...])

        @pl.when(st == nst - 1)
        def _():
            for sh in range(N_CHIPS):
                out_ref[sh] = acc[sh * 64:(sh + 1) * 64, :]

    blk = pl.BlockSpec((ts, GROUP_DIM), lambda g, st: (st, g))
    return pl.pallas_call(
        body, name="wgrad_group", grid=(N_GROUPS, nst),
        in_specs=[blk, blk],
        out_specs=pl.BlockSpec((N_CHIPS, None, 64, GROUP_DIM), lambda g, st: (0, g, 0, 0)),
        out_shape=jax.ShapeDtypeStruct((N_CHIPS, N_GROUPS, 64, GROUP_DIM), F32),
        scratch_shapes=[pltpu.VMEM((GROUP_DIM, GROUP_DIM), F32)],
        compiler_params=_params(("parallel", "arbitrary")),
    )(pooled, dmr)


def _local_step(x, p, target, w):
    wg4 = w["a_w_group"].reshape(N_CHIPS, N_GROUPS, 64, GROUP_DIM)
    wa_out = w["a_w_out"].reshape(D_MODEL, D_MODEL)
    wb_out = w["b_w_out"].reshape(D_MODEL, D_MODEL)
    ple_w = w["ple_w"].reshape(N_CHIPS, 2, PLE_DIM, 256)
    ple_g = w["ple_gate_w"].reshape(N_CHIPS, 2, 256, D_MODEL)
    k_gain_t = jnp.tile(w["k_norm"].reshape(1, HEAD_DIM), (1, N_HEADS))
    q_gain_t = jnp.tile(w["b_q_norm"].reshape(1, HEAD_DIM), (1, N_HEADS))

    uz, h_a = _a_in(x, w["a_norm"], w["a_w_in"])
    gated_a, pooled = _a_mix(uz, wg4, w["a_scale"])
    x1, x2, e_a, gate_a = _out_ple("a_out_ple", gated_a, x, wa_out, p, 0, ple_w, ple_g)
    h_kv, h_b, k_raw, q_raw, k, q, v, z_b = _b_in(
        x2, w["kv_norm"], w["b_norm"], k_gain_t, q_gain_t, w["w_kv"], w["b_w_in"])
    o, gated_b, ltot = _attn_fwd(q, k, v, z_b)
    x3, dx4, e_b, gate_b, loss_blk = _out_ple("b_out_ple", gated_b, x2, wb_out, p, 1, ple_w, ple_g, target=target)

    de_b, dgp_b, dx3, dgated_b = _ple_out_bwd("b_ple_out_bwd", dx4, e_b, gate_b, 1, ple_g, wb_out)
    dq, dk, dv, dz_b = _attn_bwd(q, k, v, ltot, dgated_b, o, z_b)
    dqz, dkv, dx2, small_b = _b_in_bwd(dq, dk, dv, dz_b, q_raw, k_raw, x2, dx3, q_gain_t, k_gain_t,
                                       w["b_norm"], w["kv_norm"], w["b_w_in"], w["w_kv"])
    de_a, dgp_a, dx1, dgated_a = _ple_out_bwd("a_ple_out_bwd", dx2, e_a, gate_a, 0, ple_g, wa_out)
    duz, dmr, grad_x, small_a = _a_mix_bwd(dgated_a, uz, pooled, wg4, w["a_scale"], w["a_w_in"], x, dx1, w["a_norm"])

    grads = {
        "a_w_in": _wgrad("wgrad_a_in", h_a, duz, N_CHIPS),
        "a_w_group": _wgrad_group(pooled, dmr).reshape(N_CHIPS, N_GROUPS * 64, GROUP_DIM),
        "a_w_out": _wgrad("wgrad_a_out", gated_a, dx1, 1).reshape(N_CHIPS, 256, D_MODEL),
        "w_kv": _wgrad("wgrad_kv", h_kv, dkv, N_CHIPS),
        "b_w_in": _wgrad("wgrad_b_in", h_b, dqz, N_CHIPS),
        "b_w_out": _wgrad("wgrad_b_out", gated_b, dx3, 1).reshape(N_CHIPS, 256, D_MODEL),
        "ple_w": jnp.concatenate([_wgrad_ple("wgrad_ple0", p, 0, de_a), _wgrad_ple("wgrad_ple1", p, 1, de_b)], axis=1),
        "ple_gate_w": jnp.concatenate(
            [_wgrad("wgrad_gate0", x1, dgp_a, 1).reshape(N_CHIPS, 256, D_MODEL),
             _wgrad("wgrad_gate1", x3, dgp_b, 1).reshape(N_CHIPS, 256, D_MODEL)], axis=1),
    }
    fold = lambda row: jnp.pad(row.reshape(N_HEADS, HEAD_DIM).sum(axis=0), (0, D_MODEL - HEAD_DIM))
    small = jnp.stack([small_a[1], small_a[0], small_b[3], small_b[2], fold(small_b[1]), fold(small_b[0]),
                       jnp.zeros((D_MODEL,), F32), jnp.zeros((D_MODEL,), F32)])
    return loss_blk[0, 0], grad_x, grads, small


def _mesh_place():
    x, y, c = lax.axis_index("x"), lax.axis_index("y"), lax.axis_index("c")
    other_chips = [(1 - x, y), (x, 1 - y), (1 - x, 1 - y)]
    return x, y, c, other_chips


def _allgather_weights(shards, small):
    n = len(shards)

    def body(*refs):
        ins, small_in = refs[:n], refs[n]
        outs, small_out = refs[n + 1:2 * n + 1], refs[2 * n + 1]
        cast = refs[2 * n + 2:3 * n + 2]
        send_far, recv_far, send_sib, recv_sib, send_small, recv_small, local_sem = refs[3 * n + 2:]
        x, y, c, chips = _mesh_place()
        me = 2 * x + y
        sibling = (x, y, 1 - c)

        def half(k, which):
            rows = ins[k].shape[0] // 2
            return pl.ds(pl.multiple_of(which * rows, 16), rows)

        local = []
        for k in range(n):
            cast[k][...] = ins[k][...].astype(BF16)
            local.append(pltpu.make_async_copy(cast[k], outs[k].at[me], local_sem.at[k]))
            local[-1].start()
        local.append(pltpu.make_async_copy(small_in, small_out.at[me], local_sem.at[n]))
        local[-1].start()

        sends = []
        for j, (px, py) in enumerate(chips):
            for k in range(n):
                cp = pltpu.make_async_remote_copy(
                    src_ref=cast[k].at[half(k, c)], dst_ref=outs[k].at[me, half(k, c)],
                    send_sem=send_far.at[j * n + k], recv_sem=recv_far.at[j * n + k],
                    device_id=(px, py, c), device_id_type=MESH)
                cp.start()
                sends.append(cp)
            cp = pltpu.make_async_remote_copy(
                src_ref=small_in, dst_ref=small_out.at[me], send_sem=send_small.at[j], recv_sem=recv_small.at[j],
                device_id=(px, py, c), device_id_type=MESH)
            cp.start()
            sends.append(cp)

        def landed(j, k, which, sems_s, sems_r, device):
            px, py = chips[j]
            piece = outs[k].at[2 * px + py, half(k, which)]
            return pltpu.make_async_remote_copy(
                src_ref=piece, dst_ref=piece, send_sem=sems_s.at[j * n + k], recv_sem=sems_r.at[j * n + k],
                device_id=device, device_id_type=MESH)

        for j in range(len(chips)):
            for k in range(n):
                landed(j, k, c, send_far, recv_far, sibling).wait_recv()
                cp = landed(j, k, c, send_sib, recv_sib, sibling)
                cp.start()
                sends.append(cp)
        for j, (px, py) in enumerate(chips):
            for k in range(n):
                landed(j, k, 1 - c, send_sib, recv_sib, sibling).wait_recv()
            pltpu.make_async_remote_copy(
                src_ref=small_in, dst_ref=small_out.at[2 * px + py], send_sem=send_small.at[j],
                recv_sem=recv_small.at[j], device_id=(px, py, c), device_id_type=MESH).wait_recv()
        for cp in sends:
            cp.wait_send()
        for cp in local:
            cp.wait()

    vmem = pl.BlockSpec(memory_space=pltpu.VMEM)
    hbm = pl.BlockSpec(memory_space=pltpu.HBM)
    return pl.pallas_call(
        body, name="allgather_weights",
        in_specs=[vmem] * (n + 1), out_specs=[hbm] * (n + 1),
        out_shape=[jax.ShapeDtypeStruct((N_CHIPS,) + s.shape, BF16) for s in shards]
        + [jax.ShapeDtypeStruct((N_CHIPS,) + small.shape, F32)],
        scratch_shapes=[pltpu.VMEM(s.shape, BF16) for s in shards]
        + [pltpu.SemaphoreType.DMA((3 * n,)), pltpu.SemaphoreType.DMA((3 * n,)),
           pltpu.SemaphoreType.DMA((3 * n,)), pltpu.SemaphoreType.DMA((3 * n,)),
           pltpu.SemaphoreType.DMA((3,)), pltpu.SemaphoreType.DMA((3,)),
           pltpu.SemaphoreType.DMA((n + 1,))],
        compiler_params=_params(None, 40),
    )(*shards, small)


def _adamw(w, g, m, v):
    m = ADAM_B1 * m + (1.0 - ADAM_B1) * g
    v = ADAM_B2 * v + (1.0 - ADAM_B2) * (g * g)
    m_hat = m / (1.0 - ADAM_B1 ** ADAM_STEP)
    v_hat = v / (1.0 - ADAM_B2 ** ADAM_STEP)
    delta = -ADAM_LR * (m_hat / (jnp.sqrt(v_hat) + ADAM_EPS) + ADAM_WD * w)
    return delta, m, v


def _reduce_adam(name, grad, w, m, v):
    _, R, C = grad.shape
    hr = R // 2

    def body(g_ref, w_ref, m_ref, v_ref, go_ref, d_ref, mo_ref, vo_ref,
             to_sib, from_sib, part, to_far, from_far, send_sem, recv_sem):
        x, y, c, chips = _mesh_place()
        me = 2 * x + y
        sibling = (x, y, 1 - c)
        mine = pl.ds(pl.multiple_of(c * hr, 16), hr)
        other = pl.ds(pl.multiple_of((1 - c) * hr, 16), hr)

        to_sib[...] = g_ref[:, other, :].astype(BF16)
        swap = pltpu.make_async_remote_copy(src_ref=to_sib, dst_ref=from_sib, send_sem=send_sem.at[0],
                                            recv_sem=recv_sem.at[0], device_id=sibling, device_id_type=MESH)
        swap.start()
        swap.wait_recv()
        part[...] = g_ref[:, mine, :] + from_sib[...].astype(F32)

        far = []
        for j, (px, py) in enumerate(chips):
            to_far[j] = part[2 * px + py].astype(BF16)
            cp = pltpu.make_async_remote_copy(src_ref=to_far.at[j], dst_ref=from_far.at[j], send_sem=send_sem.at[1 + j],
                                              recv_sem=recv_sem.at[1 + j], device_id=(px, py, c), device_id_type=MESH)
            cp.start()
            far.append(cp)
        total = part[me]
        for j, cp in enumerate(far):
            cp.wait_recv()
            total = total + from_far[j].astype(F32)

        go_ref[mine, :] = total
        back = pltpu.make_async_remote_copy(src_ref=go_ref.at[mine], dst_ref=go_ref.at[mine], send_sem=send_sem.at[4],
                                            recv_sem=recv_sem.at[4], device_id=sibling, device_id_type=MESH)
        back.start()
        back.wait_recv()
        delta, m_new, v_new = _adamw(w_ref[...], go_ref[...], m_ref[...], v_ref[...])
        d_ref[...] = delta
        mo_ref[...] = m_new
        vo_ref[...] = v_new
        swap.wait_send()
        for cp in far:
            cp.wait_send()
        back.wait_send()

    vmem = pl.BlockSpec(memory_space=pltpu.VMEM)
    return pl.pallas_call(
        body, name=name,
        in_specs=[vmem] * 4, out_specs=[vmem] * 4,
        out_shape=[jax.ShapeDtypeStruct((R, C), F32)] * 4,
        scratch_shapes=[pltpu.VMEM((N_CHIPS, hr, C), BF16), pltpu.VMEM((N_CHIPS, hr, C), BF16),
                        pltpu.VMEM((N_CHIPS, hr, C), F32),
                        pltpu.VMEM((3, hr, C), BF16), pltpu.VMEM((3, hr, C), BF16),
                        pltpu.SemaphoreType.DMA((5,)), pltpu.SemaphoreType.DMA((5,))],
        compiler_params=_params(None, 56),
    )(grad, w, m, v)


def _allreduce_small(part):
    n_dev = 8

    def body(part_ref, out_ref, buf, send_sem, recv_sem):
        x, y, c, _ = _mesh_place()
        me = 4 * x + 2 * y + c
        buf[me] = part_ref[...]
        sends = []
        for k in range(1, n_dev):
            peer = ((1 - x) if k & 4 else x, (1 - y) if k & 2 else y, (1 - c) if k & 1 else c)
            cp = pltpu.make_async_remote_copy(src_ref=part_ref, dst_ref=buf.at[me], send_sem=send_sem.at[k - 1],
                                              recv_sem=recv_sem.at[k - 1], device_id=peer, device_id_type=MESH)
            cp.start()
            sends.append(cp)
        for cp in sends:
            cp.wait_recv()
        total = buf[0]
        for s in range(1, n_dev):
            total = total + buf[s]
        out_ref[...] = total
        for cp in sends:
            cp.wait_send()

    vmem = pl.BlockSpec(memory_space=pltpu.VMEM)
    return pl.pallas_call(
        body, name="allreduce_small", in_specs=[vmem], out_specs=vmem,
        out_shape=jax.ShapeDtypeStruct(part.shape, F32),
        scratch_shapes=[pltpu.VMEM((n_dev,) + part.shape, F32),
                        pltpu.SemaphoreType.DMA((n_dev - 1,)), pltpu.SemaphoreType.DMA((n_dev - 1,))],
    )(part)


def _adam_small(w, g, m, v):
    def body(w_ref, g_ref, m_ref, v_ref, d_ref, mo_ref, vo_ref):
        delta, m_new, v_new = _adamw(w_ref[...], g_ref[...], m_ref[...], v_ref[...])
        d_ref[...] = delta
        mo_ref[...] = m_new
        vo_ref[...] = v_new

    vmem = pl.BlockSpec(memory_space=pltpu.VMEM)
    return pl.pallas_call(
        body, name="adam_small", in_specs=[vmem] * 4, out_specs=[vmem] * 3,
        out_shape=[jax.ShapeDtypeStruct(w.shape, F32)] * 3,
    )(w, g, m, v)


BIG = ("a_w_in", "a_w_group", "a_w_out", "w_kv", "b_w_in", "b_w_out", "ple_w", "ple_gate_w")
SMALL = ("a_norm", "a_scale", "kv_norm", "b_norm", "k_norm", "b_q_norm")
SMALL_SHARDED = ("a_norm", "a_scale")
WEIGHTS = ("a_norm", "a_w_in", "a_w_group", "a_scale", "a_w_out", "kv_norm", "w_kv", "k_norm", "b_norm", "b_w_in",
           "b_q_norm", "b_w_out", "ple_w", "ple_gate_w")


def _as_matrix(a):
    return a.reshape(-1, a.shape[-1])


def _pack_small(arrs):
    rows = [jnp.pad(a.reshape(-1), (0, D_MODEL - a.size)) for a in arrs]
    rows += [jnp.zeros((D_MODEL,), F32)] * (8 - len(rows))
    return jnp.stack(rows)


def kernel(x, p, a_norm, a_w_in, a_w_group, a_scale, a_w_out, kv_norm, w_kv, k_norm, b_norm, b_w_in, b_q_norm, b_w_out, ple_w, ple_gate_w, loss_target, m_a_norm, m_a_w_in, m_a_w_group, m_a_scale, m_a_w_out, m_kv_norm, m_w_kv, m_k_norm, m_b_norm, m_b_w_in, m_b_q_norm, m_b_w_out, m_ple_w, m_ple_gate_w, v_a_norm, v_a_w_in, v_a_w_group, v_a_scale, v_a_w_out, v_kv_norm, v_w_kv, v_k_norm, v_b_norm, v_b_w_in, v_b_q_norm, v_b_w_out, v_ple_w, v_ple_gate_w):
    wts = dict(a_norm=a_norm, a_w_in=a_w_in, a_w_group=a_w_group, a_scale=a_scale, a_w_out=a_w_out, kv_norm=kv_norm,
               w_kv=w_kv, k_norm=k_norm, b_norm=b_norm, b_w_in=b_w_in, b_q_norm=b_q_norm, b_w_out=b_w_out,
               ple_w=ple_w, ple_gate_w=ple_gate_w)
    mom = dict(a_norm=m_a_norm, a_w_in=m_a_w_in, a_w_group=m_a_w_group, a_scale=m_a_scale, a_w_out=m_a_w_out,
               kv_norm=m_kv_norm, w_kv=m_w_kv, k_norm=m_k_norm, b_norm=m_b_norm, b_w_in=m_b_w_in,
               b_q_norm=m_b_q_norm, b_w_out=m_b_w_out, ple_w=m_ple_w, ple_gate_w=m_ple_gate_w)
    var = dict(a_norm=v_a_norm, a_w_in=v_a_w_in, a_w_group=v_a_w_group, a_scale=v_a_scale, a_w_out=v_a_w_out,
               kv_norm=v_kv_norm, w_kv=v_w_kv, k_norm=v_k_norm, b_norm=v_b_norm, b_w_in=v_b_w_in,
               b_q_norm=v_b_q_norm, b_w_out=v_b_w_out, ple_w=v_ple_w, ple_gate_w=v_ple_gate_w)
    S = x.shape[1]
    chip = 2 * lax.axis_index("x") + lax.axis_index("y")

    sharded_small = jnp.concatenate([a_norm.reshape(1, 256), a_scale.reshape(1, 256), jnp.zeros((6, 256), F32)], axis=0)
    gathered = _allgather_weights([_as_matrix(wts[n]) for n in BIG], sharded_small)
    full = dict(zip(BIG, gathered[:-1]))
    full["a_norm"] = gathered[-1][:, 0, :].reshape(1, D_MODEL)
    full["a_scale"] = gathered[-1][:, 1, :].reshape(1, D_MODEL)
    full["kv_norm"] = kv_norm.reshape(1, D_MODEL)
    full["b_norm"] = b_norm.reshape(1, D_MODEL)
    full["k_norm"] = k_norm
    full["b_q_norm"] = b_q_norm

    loss_local, grad_x, grads, small_part = _local_step(x.reshape(S, D_MODEL), p, loss_target.reshape(S, D_MODEL), full)
    loss = lax.psum(loss_local, ("x", "y", "c"))

    out_g, out_d, out_m, out_v = {}, {}, {}, {}
    for n in BIG:
        shape = wts[n].shape
        g, d, m_new, v_new = _reduce_adam("reduce_adam_" + n, grads[n], _as_matrix(wts[n]), _as_matrix(mom[n]),
                                          _as_matrix(var[n]))
        out_g[n], out_d[n], out_m[n], out_v[n] = (t.reshape(shape) for t in (g, d, m_new, v_new))

    small_sum = _allreduce_small(small_part)
    small_rows = []
    for i, n in enumerate(SMALL):
        row = small_sum[i]
        if n in SMALL_SHARDED:
            row = lax.dynamic_slice(row, (chip * 256,), (256,))
        else:
            row = row[:wts[n].size]
        small_rows.append(row)
    g_small = _pack_small(small_rows)
    d_small, m_small, v_small = _adam_small(_pack_small([wts[n] for n in SMALL]), g_small,
                                            _pack_small([mom[n] for n in SMALL]), _pack_small([var[n] for n in SMALL]))
    for i, n in enumerate(SMALL):
        shape, size = wts[n].shape, wts[n].size
        out_g[n], out_d[n], out_m[n], out_v[n] = (t[i, :size].reshape(shape) for t in (g_small, d_small, m_small, v_small))

    return (loss, grad_x.reshape(1, S, D_MODEL), *[out_g[n] for n in WEIGHTS], *[out_d[n] for n in WEIGHTS],
            *[out_m[n] for n in WEIGHTS], *[out_v[n] for n in WEIGHTS])
```

```python
import functools

import jax
import jax.numpy as jnp
from jax import lax
from jax.experimental import pallas as pl
from jax.experimental.pallas import tpu as pltpu

F32 = jnp.float32
BF16 = jnp.bfloat16
MESH = pl.DeviceIdType.MESH

D_MODEL = 1024
N_HEADS = 16
HEAD_DIM = 64
PLE_DIM = 256
N_GROUPS = 4
GROUP_DIM = 256
POOL_WINDOWS = (2, 4, 8, 16)
N_CHIPS = 4
EPS = 1e-6
SB_SCALE = HEAD_DIM ** -0.5

ADAM_LR = 0.001
ADAM_B1 = 0.9
ADAM_B2 = 0.999
ADAM_EPS = 1e-08
ADAM_WD = 0.01
ADAM_STEP = 10

ROW_TILE = 256
ATT_Q_TILE = 512
ATT_K_TILE = 256
WGRAD_SEQ_TILE = 512
MIB = 1024 * 1024


def _params(semantics=None, vmem_mib=48):
    return pltpu.CompilerParams(dimension_semantics=semantics, vmem_limit_bytes=vmem_mib * MIB)


def _dot(a, b):
    return jnp.dot(a, b, preferred_element_type=F32)


def _dot_nt(a, b):
    return lax.dot_general(a, b, (((1,), (1,)), ((), ())), preferred_element_type=F32)


def _dot_tn(a, b):
    return lax.dot_general(a, b, (((0,), (0,)), ((), ())), preferred_element_type=F32)


def _hilo(x):
    hi = x.astype(BF16)
    lo = (x - hi.astype(F32)).astype(BF16)
    return hi, lo


def _dot_hilo(x, w):
    hi, lo = _hilo(x)
    return _dot(hi, w) + _dot(lo, w)


def _sigmoid(z):
    return jax.nn.sigmoid(z)


def _dsilu(z, sg):
    return sg * (1.0 + z * (1.0 - sg))


def _mask_bf16(cond):
    return jnp.where(cond, 1.0, 0.0).astype(BF16)


def _head_mean_matrix():
    r = lax.broadcasted_iota(jnp.int32, (256, 256), 0) // HEAD_DIM
    c = lax.broadcasted_iota(jnp.int32, (256, 256), 1) // HEAD_DIM
    return _mask_bf16(r == c)


def _head_mean(x, bd):
    parts = []
    for s in range(x.shape[1] // 256):
        parts.append(_dot_hilo(x[:, s * 256:(s + 1) * 256], bd))
    out = parts[0] if len(parts) == 1 else jnp.concatenate(parts, axis=1)
    return out * (1.0 / HEAD_DIM)


def _a_in(x, gain, w_sh):
    S = x.shape[0]
    tm = 512
    nsh, _, wn = w_sh.shape

    def body(x_ref, g_ref, w_ref, uz_ref, h_ref):
        @pl.when(pl.program_id(1) == 0)
        def _():
            xv = x_ref[...]
            r = lax.rsqrt(jnp.mean(xv * xv, axis=-1, keepdims=True) + EPS)
            h_ref[...] = (xv * r * g_ref[...]).astype(BF16)

        uz_ref[...] = _dot(h_ref[...], w_ref[0])

    return pl.pallas_call(
        body, name="a_in", grid=(S // tm, nsh),
        in_specs=[pl.BlockSpec((tm, D_MODEL), lambda i, j: (i, 0)),
                  pl.BlockSpec((1, D_MODEL), lambda i, j: (0, 0)),
                  pl.BlockSpec((1, D_MODEL, wn), lambda i, j: (j, 0, 0))],
        out_specs=[pl.BlockSpec((tm, wn), lambda i, j: (i, j)),
                   pl.BlockSpec((tm, D_MODEL), lambda i, j: (i, 0))],
        out_shape=[jax.ShapeDtypeStruct((S, nsh * wn), F32),
                   jax.ShapeDtypeStruct((S, D_MODEL), BF16)],
        compiler_params=_params(("parallel", "arbitrary")),
    )(x, gain, w_sh)


def _inv_count(first_row, rows, w):
    t1 = first_row + 1 + lax.broadcasted_iota(jnp.int32, (rows, 1), 0)
    return 1.0 / jnp.minimum(t1, w).astype(F32)


def _group_weight(wg_ref, g):
    return jnp.concatenate([wg_ref[sh, g] for sh in range(N_CHIPS)], axis=0)


def _a_mix(uz, wg, scale):
    S = uz.shape[0]
    tm = ROW_TILE

    def body(u_ref, up_ref, z_ref, wg_ref, sc_ref, ga_ref, p_ref):
        i = pl.program_id(0)
        row = lax.broadcasted_iota(jnp.int32, (tm, tm), 0)
        col = lax.broadcasted_iota(jnp.int32, (tm, tm), 1)
        d = row - col
        for g, w in enumerate(POOL_WINDOWS):
            cols = slice(g * GROUP_DIM, (g + 1) * GROUP_DIM)
            t_main = _mask_bf16((d >= 0) & (d < w))
            t_halo = _mask_bf16(d + tm < w)
            u = u_ref[:, cols]
            up = jnp.where(i > 0, up_ref[:, cols], 0.0)
            hi, lo = _hilo(u)
            hip, lop = _hilo(up)
            wsum = _dot(t_main, hi) + _dot(t_main, lo) + _dot(t_halo, hip) + _dot(t_halo, lop)
            pooled = (wsum * _inv_count(i * tm, tm, w) - u).astype(BF16)
            p_ref[:, cols] = pooled
            mraw = _dot(pooled, _group_weight(wg_ref, g))
            z = z_ref[:, cols]
            ga_ref[:, cols] = (mraw * sc_ref[:, cols] * (z * _sigmoid(z))).astype(BF16)

    return pl.pallas_call(
        body, name="a_mix", grid=(S // tm,),
        in_specs=[pl.BlockSpec((tm, D_MODEL), lambda i: (i, 0)),
                  pl.BlockSpec((tm, D_MODEL), lambda i: (jnp.maximum(i - 1, 0), 0)),
                  pl.BlockSpec((tm, D_MODEL), lambda i: (i, 1)),
                  pl.BlockSpec((N_CHIPS, N_GROUPS, 64, GROUP_DIM), lambda i: (0, 0, 0, 0)),
                  pl.BlockSpec((1, D_MODEL), lambda i: (0, 0))],
        out_specs=[pl.BlockSpec((tm, D_MODEL), lambda i: (i, 0)),
                   pl.BlockSpec((tm, D_MODEL), lambda i: (i, 0))],
        out_shape=[jax.ShapeDtypeStruct((S, D_MODEL), BF16),
                   jax.ShapeDtypeStruct((S, D_MODEL), BF16)],
        compiler_params=_params(("arbitrary",)),
    )(uz, uz, uz, wg, scale)


def _out_ple(name, gated, x_in, w_out, p, layer, ple_w, ple_g, target=None):
    S = x_in.shape[0]
    tm = ROW_TILE
    with_loss = target is not None

    def body(*refs):
        if with_loss:
            g_ref, x_ref, wo_ref, p_ref, pw_ref, pg_ref, t_ref, xm_ref, dx_ref, e_ref, gt_ref, loss_ref = refs
        else:
            g_ref, x_ref, wo_ref, p_ref, pw_ref, pg_ref, xm_ref, xo_ref, e_ref, gt_ref = refs
        xm = x_ref[...] + _dot(g_ref[...], wo_ref[...])
        xm_ref[...] = xm
        pb = p_ref[...].astype(BF16)
        e = jnp.concatenate([_dot(pb, pw_ref[sh]) for sh in range(N_CHIPS)], axis=1)
        pg = jnp.concatenate([pg_ref[sh] for sh in range(N_CHIPS)], axis=0)
        gate = _sigmoid(_dot(xm.astype(BF16), pg))
        e_ref[...] = e.astype(BF16)
        gt_ref[...] = gate.astype(BF16)
        xo = xm + e * gate
        if with_loss:
            diff = xo - t_ref[...]
            dx_ref[...] = diff * (1.0 / D_MODEL)

            @pl.when(pl.program_id(0) == 0)
            def _():
                loss_ref[...] = jnp.zeros_like(loss_ref)

            loss_ref[...] += jnp.sum(diff * diff) * (0.5 / D_MODEL)
        else:
            xo_ref[...] = xo

    row = pl.BlockSpec((tm, D_MODEL), lambda i: (i, 0))
    in_specs = [row, row,
                pl.BlockSpec((D_MODEL, D_MODEL), lambda i: (0, 0)),
                pl.BlockSpec((None, None, tm, PLE_DIM), lambda i: (layer, 0, i, 0)),
                pl.BlockSpec((N_CHIPS, None, PLE_DIM, 256), lambda i: (0, layer, 0, 0)),
                pl.BlockSpec((N_CHIPS, None, 256, D_MODEL), lambda i: (0, layer, 0, 0))]
    args = [gated, x_in, w_out, p, ple_w, ple_g]
    out_specs = [row, row, row, row]
    out_shape = [jax.ShapeDtypeStruct((S, D_MODEL), F32), jax.ShapeDtypeStruct((S, D_MODEL), F32),
                 jax.ShapeDtypeStruct((S, D_MODEL), BF16), jax.ShapeDtypeStruct((S, D_MODEL), BF16)]
    if with_loss:
        in_specs.append(row)
        args.append(target)
        out_specs.append(pl.BlockSpec((8, 128), lambda i: (0, 0)))
        out_shape.append(jax.ShapeDtypeStruct((8, 128), F32))
    return pl.pallas_call(
        body, name=name, grid=(S // tm,), in_specs=in_specs, out_specs=out_specs, out_shape=out_shape,
        compiler_params=_params(("arbitrary",)),
    )(*args)


def _b_in(x, kv_gain, b_gain, k_gain_t, q_gain_t, w_kv, w_in):
    S = x.shape[0]
    tm = ROW_TILE

    def body(x_ref, kvg_ref, bg_ref, kg_ref, qg_ref, wkv_ref, win_ref,
             hkv_ref, hb_ref, kraw_ref, qraw_ref, k_ref, q_ref, v_ref, z_ref):
        xv = x_ref[...]
        y = xv * lax.rsqrt(jnp.mean(xv * xv, axis=-1, keepdims=True) + EPS)
        hkv = (y * kvg_ref[...]).astype(BF16)
        hb = (y * bg_ref[...]).astype(BF16)
        hkv_ref[...] = hkv
        hb_ref[...] = hb
        bd = _head_mean_matrix()

        def head_norm(raw, gain):
            rr = lax.rsqrt(_head_mean(raw * raw, bd) + EPS)
            return raw * rr * gain

        for sh in range(N_CHIPS):
            kvc = _dot(hkv, wkv_ref[sh])
            qzc = _dot(hb, win_ref[sh])
            cols = slice((sh % 2) * 512, (sh % 2) * 512 + 512)
            if sh < 2:
                kraw_ref[:, cols] = kvc.astype(BF16)
                qraw_ref[:, cols] = qzc.astype(BF16)
                k_ref[:, cols] = head_norm(kvc, kg_ref[:, cols]).astype(BF16)
                q_ref[:, cols] = (head_norm(qzc, qg_ref[:, cols]) * SB_SCALE).astype(BF16)
            else:
                v_ref[:, cols] = kvc.astype(BF16)
                z_ref[:, cols] = qzc.astype(BF16)

    row = pl.BlockSpec((tm, D_MODEL), lambda i: (i, 0))
    vec = pl.BlockSpec((1, D_MODEL), lambda i: (0, 0))
    wsp = pl.BlockSpec((N_CHIPS, D_MODEL, 512), lambda i: (0, 0, 0))
    return pl.pallas_call(
        body, name="b_in", grid=(S // tm,),
        in_specs=[row, vec, vec, vec, vec, wsp, wsp],
        out_specs=[row] * 8,
        out_shape=[jax.ShapeDtypeStruct((S, D_MODEL), BF16)] * 8,
        compiler_params=_params(("arbitrary",), 56),
    )(x, kv_gain, b_gain, k_gain_t, q_gain_t, w_kv, w_in)


def _softplus_parts(z):
    e = jnp.exp(-jnp.abs(z))
    return -(jnp.maximum(z, 0.0) + jnp.log(1.0 + e)), e


def _attn_fwd(q, k, v, zgate):
    S = q.shape[0]
    tq, tk = ATT_Q_TILE, ATT_K_TILE
    kpq = tq // tk

    def body(q_ref, k_ref, v_ref, z_ref, o_ref, g_ref, lt_ref):
        qi = pl.program_id(1)
        lane = lax.broadcasted_iota(jnp.int32, (1, 128), 1)
        ri = lax.broadcasted_iota(jnp.int32, (tk, tk), 0)
        ci = lax.broadcasted_iota(jnp.int32, (tk, tk), 1)
        later_mat = _mask_bf16(ri > ci)
        t_idx = qi * tq + lax.broadcasted_iota(jnp.int32, (tq, tk), 0)
        s_off = lax.broadcasted_iota(jnp.int32, (tq, tk), 1)
        qv = q_ref[...]
        first = lane < HEAD_DIM
        q_heads = (jnp.where(first, qv, jnp.zeros_like(qv)), jnp.where(first, jnp.zeros_like(qv), qv))

        def step(kj_last, carry, masked):
            chains = [(d, h) for d in range(kpq) for h in range(2)]
            s0 = [pl.multiple_of((kj_last - d) * tk, tk) for d in range(kpq)]
            kb = [k_ref[pl.ds(s, tk), :] for s in s0]
            vb = [v_ref[pl.ds(s, tk), :] for s in s0]
            visible = [s + s_off < t_idx for s in s0] if masked else None
            z = {c: _dot_nt(q_heads[c[1]], kb[c[0]]) for c in chains}
            run = [carry[0], carry[2]]
            log_own, later, run_at = {}, {}, {}
            for c in chains:
                kj, h = c
                lk = _softplus_parts(z[c])[0]
                if masked:
                    lk = jnp.where(visible[kj], lk, 0.0)
                log_own[c] = z[c] + lk
                later[c] = _dot(lk.astype(BF16), later_mat)
                run_at[c] = run[h]
                run[h] = run[h] + jnp.sum(lk, axis=-1, keepdims=True)
            acc = [carry[1], carry[3]]
            for c in chains:
                kj, h = c
                a = jnp.exp(log_own[c] + later[c] + run_at[c])
                if masked:
                    a = jnp.where(visible[kj], a, 0.0)
                acc[h] = acc[h] + _dot(a.astype(BF16), vb[kj])
            return run[0], acc[0], run[1], acc[1]

        zero1, zero128 = jnp.zeros((tq, 1), F32), jnp.zeros((tq, 128), F32)
        carry = step(qi * kpq + kpq - 1, (zero1, zero128, zero1, zero128), True)
        carry = lax.fori_loop(0, qi, lambda n, c: step((qi - n) * kpq - 1, c, False), carry)
        o_tot = jnp.where(first, carry[1], carry[3])
        l_tot = jnp.where(first, carry[0], carry[2])
        o_ref[...] = o_tot.astype(BF16)
        lt_ref[...] = l_tot
        zz = z_ref[...].astype(F32)
        g_ref[...] = (o_tot * (zz * _sigmoid(zz))).astype(BF16)

    blk = pl.BlockSpec((tq, 128), lambda hp, qi: (qi, hp))
    seq = pl.BlockSpec((S, 128), lambda hp, qi: (0, hp))
    return pl.pallas_call(
        body, name="attn_fwd", grid=(D_MODEL // 128, S // tq),
        in_specs=[blk, seq, seq, blk], out_specs=[blk, blk, blk],
        out_shape=[jax.ShapeDtypeStruct((S, D_MODEL), BF16)] * 2 + [jax.ShapeDtypeStruct((S, D_MODEL), F32)],
        compiler_params=_params(("parallel", "arbitrary")),
    )(q, k, v, zgate)


def _ple_out_bwd(name, dx_out, e, gate, layer, ple_g, w_out):
    S = dx_out.shape[0]
    tm = ROW_TILE

    def body(dx_ref, e_ref, gt_ref, pg_ref, wo_ref, de_ref, dgp_ref, dxm_ref, dg_ref):
        dxo = dx_ref[...]
        ev = e_ref[...].astype(F32)
        gv = gt_ref[...].astype(F32)
        de_ref[...] = (dxo * gv).astype(BF16)
        dgp = (dxo * ev * gv * (1.0 - gv)).astype(BF16)
        dgp_ref[...] = dgp
        pg = jnp.concatenate([pg_ref[sh] for sh in range(N_CHIPS)], axis=0)
        dxm = dxo + _dot_nt(dgp, pg)
        dxm_ref[...] = dxm
        dg_ref[...] = _dot_nt(dxm.astype(BF16), wo_ref[...]).astype(BF16)

    row = pl.BlockSpec((tm, D_MODEL), lambda i: (i, 0))
    return pl.pallas_call(
        body, name=name, grid=(S // tm,),
        in_specs=[row, row, row,
                  pl.BlockSpec((N_CHIPS, None, 256, D_MODEL), lambda i: (0, layer, 0, 0)),
                  pl.BlockSpec((D_MODEL, D_MODEL), lambda i: (0, 0))],
        out_specs=[row, row, row, row],
        out_shape=[jax.ShapeDtypeStruct((S, D_MODEL), BF16), jax.ShapeDtypeStruct((S, D_MODEL), BF16),
                   jax.ShapeDtypeStruct((S, D_MODEL), F32), jax.ShapeDtypeStruct((S, D_MODEL), BF16)],
        compiler_params=_params(("arbitrary",)),
    )(dx_out, e, gate, ple_g, w_out)


def _attn_bwd(q, k, v, ltot, dgated, o, zgate):
    S = q.shape[0]
    tq, tk = ATT_Q_TILE, ATT_K_TILE
    kpq = tq // tk
    nq = S // tq

    def body(q_ref, k_ref, v_ref, lt_ref, dg_ref, o_ref, z_ref, dq_ref, dk_ref, dv_ref, dz_ref, dk_acc, dv_acc):
        qi = pl.program_id(1)

        @pl.when(qi == 0)
        def _():
            dk_acc[...] = jnp.zeros_like(dk_acc)
            dv_acc[...] = jnp.zeros_like(dv_acc)

        lane = lax.broadcasted_iota(jnp.int32, (1, 128), 1)
        ri = lax.broadcasted_iota(jnp.int32, (tk, tk), 0)
        ci = lax.broadcasted_iota(jnp.int32, (tk, tk), 1)
        later_mat = _mask_bf16(ri > ci)
        before_mat = _mask_bf16(ri < ci)
        t_idx = qi * tq + lax.broadcasted_iota(jnp.int32, (tq, tk), 0)
        s_off = lax.broadcasted_iota(jnp.int32, (tq, tk), 1)
        zz = z_ref[...].astype(F32)
        sg = _sigmoid(zz)
        dgv = dg_ref[...].astype(F32)
        dz_ref[...] = (dgv * o_ref[...].astype(F32) * _dsilu(zz, sg)).astype(BF16)
        dob = (dgv * (zz * sg)).astype(BF16)
        ltv = lt_ref[...]
        qv = q_ref[...]
        first = lane < HEAD_DIM
        masks = (first, jnp.logical_not(first))
        q_heads = [jnp.where(hm, qv, jnp.zeros_like(qv)) for hm in masks]
        do_heads = [jnp.where(hm, dob, jnp.zeros_like(dob)) for hm in masks]
        totals = [jnp.max(jnp.where(hm, ltv, -jnp.inf), axis=-1, keepdims=True) for hm in masks]

        def step(kj_first, carry, masked):
            chains = [(d, h) for d in range(kpq) for h in range(2)]
            s0 = [pl.multiple_of((kj_first + d) * tk, tk) for d in range(kpq)]
            kb = [k_ref[pl.ds(s, tk), :] for s in s0]
            vb = [v_ref[pl.ds(s, tk), :] for s in s0]
            visible = [s + s_off < t_idx for s in s0] if masked else None
            z = {c: _dot_nt(q_heads[c[1]], kb[c[0]]) for c in chains}
            da = {c: _dot_nt(do_heads[c[1]], vb[c[0]]) for c in chains}
            run = [carry[0], carry[3]]
            log_own, beta, later, base = {}, {}, {}, {}
            for c in chains:
                kj, h = c
                lk = _softplus_parts(z[c])[0]
                if masked:
                    lk = jnp.where(visible[kj], lk, 0.0)
                log_own[c] = z[c] + lk
                beta[c] = jnp.exp(log_own[c]).astype(BF16)
                later[c] = _dot(lk.astype(BF16), later_mat)
                run[h] = run[h] + jnp.sum(lk, axis=-1, keepdims=True)
                base[c] = totals[h] - run[h]
            grun = [carry[1], carry[4]]
            a_bf, g_bf, gbefore, grun_at = {}, {}, {}, {}
            for c in chains:
                kj, h = c
                a = jnp.exp(log_own[c] + later[c] + base[c])
                if masked:
                    a = jnp.where(visible[kj], a, 0.0)
                a_bf[c] = a.astype(BF16)
                g = da[c] * a
                g_bf[c] = g.astype(BF16)
                gbefore[c] = _dot(g_bf[c], before_mat)
                grun_at[c] = grun[h]
                grun[h] = grun[h] + jnp.sum(g, axis=-1, keepdims=True)
            dq = [carry[2], carry[5]]
            dk_blk = [jnp.zeros((tk, 128), F32) for _ in range(kpq)]
            dv_blk = [jnp.zeros((tk, 128), F32) for _ in range(kpq)]
            for c in chains:
                kj, h = c
                g = g_bf[c].astype(F32)
                dz = g - beta[c].astype(F32) * (g + gbefore[c] + grun_at[c])
                if masked:
                    dz = jnp.where(visible[kj], dz, 0.0)
                dzb = dz.astype(BF16)
                dq[h] = dq[h] + _dot(dzb, kb[kj])
                dk_blk[kj] = dk_blk[kj] + _dot_tn(dzb, q_heads[h])
                dv_blk[kj] = dv_blk[kj] + _dot_tn(a_bf[c], do_heads[h])
            for d in range(kpq):
                dk_acc[pl.ds(s0[d], tk), :] += dk_blk[d]
                dv_acc[pl.ds(s0[d], tk), :] += dv_blk[d]
            return run[0], grun[0], dq[0], run[1], grun[1], dq[1]

        zero1, zero128 = jnp.zeros((tq, 1), F32), jnp.zeros((tq, 128), F32)
        carry = lax.fori_loop(0, qi, lambda n, c: step(n * kpq, c, False),
                              (zero1, zero1, zero128, zero1, zero1, zero128))
        carry = step(qi * kpq, carry, True)
        dq_ref[...] = jnp.where(first, carry[2], carry[5]).astype(BF16)

        @pl.when(qi == nq - 1)
        def _():
            dk_ref[...] = dk_acc[...].astype(BF16)
            dv_ref[...] = dv_acc[...].astype(BF16)

    blk = pl.BlockSpec((tq, 128), lambda hp, qi: (qi, hp))
    seq = pl.BlockSpec((S, 128), lambda hp, qi: (0, hp))
    return pl.pallas_call(
        body, name="attn_bwd", grid=(D_MODEL // 128, nq),
        in_specs=[blk, seq, seq, blk, blk, blk, blk], out_specs=[blk, seq, seq, blk],
        out_shape=[jax.ShapeDtypeStruct((S, D_MODEL), BF16)] * 4,
        scratch_shapes=[pltpu.VMEM((S, 128), F32), pltpu.VMEM((S, 128), F32)],
        compiler_params=_params(("parallel", "arbitrary")),
    )(q, k, v, ltot, dgated, o, zgate)


def _rms_bwd(xv, dh_gain_sum):
    r = lax.rsqrt(jnp.mean(xv * xv, axis=-1, keepdims=True) + EPS)
    xhat = xv * r
    dx = r * (dh_gain_sum - xhat * jnp.mean(dh_gain_sum * xhat, axis=-1, keepdims=True))
    return dx, xhat


def _b_in_bwd(dq, dk, dv, dz, q_raw, k_raw, x, dx_mid, q_gain_t, k_gain_t, b_gain, kv_gain, w_in, w_kv):
    S = x.shape[0]
    tm = ROW_TILE

    def body(dq_ref, dk_ref, dv_ref, dz_ref, qr_ref, kr_ref, x_ref, dxm_ref, qg_ref, kg_ref, bg_ref, kvg_ref,
             win_ref, wkv_ref, dqz_ref, dkv_ref, dx_ref, small_ref):
        @pl.when(pl.program_id(0) == 0)
        def _():
            small_ref[...] = jnp.zeros_like(small_ref)

        bd = _head_mean_matrix()

        def head_norm_bwd(dy_ref, raw_ref, gain, scale):
            raw = raw_ref[...].astype(F32)
            rr = lax.rsqrt(_head_mean(raw * raw, bd) + EPS)
            xhat = raw * rr
            dy = dy_ref[...].astype(F32) * scale
            gdy = dy * gain
            draw = rr * (gdy - xhat * _head_mean(gdy * xhat, bd))
            return draw.astype(BF16), jnp.sum(dy * xhat, axis=0, keepdims=True)

        dqr, dqg = head_norm_bwd(dq_ref, qr_ref, qg_ref[...], SB_SCALE)
        dkr, dkg = head_norm_bwd(dk_ref, kr_ref, kg_ref[...], 1.0)
        dqz_ref[:, :D_MODEL] = dqr
        dqz_ref[:, D_MODEL:] = dz_ref[...]
        dkv_ref[:, :D_MODEL] = dkr
        dkv_ref[:, D_MODEL:] = dv_ref[...]
        dhb = jnp.zeros((tm, D_MODEL), F32)
        dhkv = jnp.zeros((tm, D_MODEL), F32)
        for sh in range(N_CHIPS):
            cols = slice(sh * 512, (sh + 1) * 512)
            dhb = dhb + _dot_nt(dqz_ref[:, cols], win_ref[sh])
            dhkv = dhkv + _dot_nt(dkv_ref[:, cols], wkv_ref[sh])
        dx, xhat = _rms_bwd(x_ref[...], dhb * bg_ref[...] + dhkv * kvg_ref[...])
        dx_ref[...] = dxm_ref[...] + dx
        small_ref[0:1, :] += dqg
        small_ref[1:2, :] += dkg
        small_ref[2:3, :] += jnp.sum(dhb * xhat, axis=0, keepdims=True)
        small_ref[3:4, :] += jnp.sum(dhkv * xhat, axis=0, keepdims=True)

    row = pl.BlockSpec((tm, D_MODEL), lambda i: (i, 0))
    wide = pl.BlockSpec((tm, 2 * D_MODEL), lambda i: (i, 0))
    vec = pl.BlockSpec((1, D_MODEL), lambda i: (0, 0))
    wsp = pl.BlockSpec((N_CHIPS, D_MODEL, 512), lambda i: (0, 0, 0))
    return pl.pallas_call(
        body, name="b_in_bwd", grid=(S // tm,),
        in_specs=[row] * 8 + [vec] * 4 + [wsp, wsp],
        out_specs=[wide, wide, row, pl.BlockSpec((8, D_MODEL), lambda i: (0, 0))],
        out_shape=[jax.ShapeDtypeStruct((S, 2 * D_MODEL), BF16), jax.ShapeDtypeStruct((S, 2 * D_MODEL), BF16),
                   jax.ShapeDtypeStruct((S, D_MODEL), F32), jax.ShapeDtypeStruct((8, D_MODEL), F32)],
        compiler_params=_params(("arbitrary",), 56),
    )(dq, dk, dv, dz, q_raw, k_raw, x, dx_mid, q_gain_t, k_gain_t, b_gain, kv_gain, w_in, w_kv)


def _a_mix_bwd(dgated, uz, pooled, wg, scale, w_in, x, dx_mid, gain):
    S = x.shape[0]
    tm = ROW_TILE
    n = S // tm

    def body(dg_ref, z_ref, p_ref, wg_ref, sc_ref, win_ref, x_ref, dxm_ref, gn_ref,
             duz_ref, dmr_ref, dx_ref, small_ref, halo_hi, halo_lo):
        i = pl.program_id(0)

        @pl.when(i == 0)
        def _():
            small_ref[...] = jnp.zeros_like(small_ref)
            halo_hi[...] = jnp.zeros_like(halo_hi)
            halo_lo[...] = jnp.zeros_like(halo_lo)

        first_row = (n - 1 - i) * tm
        row = lax.broadcasted_iota(jnp.int32, (tm, tm), 0)
        col = lax.broadcasted_iota(jnp.int32, (tm, tm), 1)
        d = col - row
        for g, w in enumerate(POOL_WINDOWS):
            cols = slice(g * GROUP_DIM, (g + 1) * GROUP_DIM)
            wgg = _group_weight(wg_ref, g)
            sc = sc_ref[:, cols]
            mraw = _dot(p_ref[:, cols], wgg)
            z = z_ref[:, cols]
            sg = _sigmoid(z)
            dga = dg_ref[:, cols].astype(F32)
            dm = dga * (z * sg)
            duz_ref[:, D_MODEL + g * GROUP_DIM:D_MODEL + (g + 1) * GROUP_DIM] = (
                dga * (mraw * sc) * _dsilu(z, sg)).astype(BF16)
            small_ref[0:1, cols] += jnp.sum(dm * mraw, axis=0, keepdims=True)
            dmr = (dm * sc).astype(BF16)
            dmr_ref[:, cols] = dmr
            dp = _dot_nt(dmr, wgg)
            hi, lo = _hilo(dp * _inv_count(first_row, tm, w))
            t_main = _mask_bf16((d >= 0) & (d < w))
            t_halo = _mask_bf16(d + tm < w)
            du = (_dot(t_main, hi) + _dot(t_main, lo) + _dot(t_halo, halo_hi[:, cols]) + _dot(t_halo, halo_lo[:, cols])
                  - dp)
            halo_hi[:, cols] = hi
            halo_lo[:, cols] = lo
            duz_ref[:, cols] = du.astype(BF16)
        dh = jnp.zeros((tm, D_MODEL), F32)
        for sh in range(N_CHIPS):
            dh = dh + _dot_nt(duz_ref[:, sh * 512:(sh + 1) * 512], win_ref[sh])
        dx, xhat = _rms_bwd(x_ref[...], dh * gn_ref[...])
        dx_ref[...] = dxm_ref[...] + dx
        small_ref[1:2, :] += jnp.sum(dh * xhat, axis=0, keepdims=True)

    rev = lambda i: (n - 1 - i, 0)
    row = pl.BlockSpec((tm, D_MODEL), rev)
    vec = pl.BlockSpec((1, D_MODEL), lambda i: (0, 0))
    return pl.pallas_call(
        body, name="a_mix_bwd", grid=(n,),
        in_specs=[row,
                  pl.BlockSpec((tm, D_MODEL), lambda i: (n - 1 - i, 1)),
                  row,
                  pl.BlockSpec((N_CHIPS, N_GROUPS, 64, GROUP_DIM), lambda i: (0, 0, 0, 0)),
                  vec,
                  pl.BlockSpec((N_CHIPS, D_MODEL, 512), lambda i: (0, 0, 0)),
                  row, row, vec],
        out_specs=[pl.BlockSpec((tm, 2 * D_MODEL), rev), row, row,
                   pl.BlockSpec((8, D_MODEL), lambda i: (0, 0))],
        out_shape=[jax.ShapeDtypeStruct((S, 2 * D_MODEL), BF16), jax.ShapeDtypeStruct((S, D_MODEL), BF16),
                   jax.ShapeDtypeStruct((S, D_MODEL), F32), jax.ShapeDtypeStruct((8, D_MODEL), F32)],
        scratch_shapes=[pltpu.VMEM((tm, D_MODEL), BF16), pltpu.VMEM((tm, D_MODEL), BF16)],
        compiler_params=_params(("arbitrary",)),
    )(dgated, uz, pooled, wg, scale, w_in, x, dx_mid, gain)


def _wgrad(name, a, dy, n_shards, a_spec=None, k_dim=None):
    S = dy.shape[0]
    ts = WGRAD_SEQ_TILE
    k_dim = a.shape[-1] if k_dim is None else k_dim
    wn = dy.shape[1] // n_shards
    tk = k_dim if k_dim * wn * 4 <= 2 * MIB else 512
    nst = S // ts

    def body(a_ref, dy_ref, out_ref):
        @pl.when(pl.program_id(2) == 0)
        def _():
            out_ref[...] = jnp.zeros_like(out_ref)

        out_ref[...] += _dot_tn(a_ref[...].astype(BF16), dy_ref[...].astype(BF16))

    if a_spec is None:
        a_spec = pl.BlockSpec((ts, tk), lambda sh, kt, st: (st, kt))
    return pl.pallas_call(
        body, name=name, grid=(n_shards, k_dim // tk, nst),
        in_specs=[a_spec, pl.BlockSpec((ts, wn), lambda sh, kt, st: (st, sh))],
        out_specs=pl.BlockSpec((None, tk, wn), lambda sh, kt, st: (sh, kt, 0)),
        out_shape=jax.ShapeDtypeStruct((n_shards, k_dim, wn), F32),
        compiler_params=_params(("parallel", "parallel", "arbitrary")),
    )(a, dy)


def _wgrad_ple(name, p, layer, de):
    ts = WGRAD_SEQ_TILE
    spec = pl.BlockSpec((None, None, ts, PLE_DIM), lambda sh, kt, st: (layer, 0, st, 0))
    return _wgrad(name, p, de, N_CHIPS, a_spec=spec, k_dim=PLE_DIM)


def _wgrad_group(pooled, dmr):
    S = pooled.shape[0]
    ts = WGRAD_SEQ_TILE
    nst = S // ts

    def body(p_ref, d_ref, out_ref, acc):
        st = pl.program_id(1)

        @pl.when(st == 0)
        def _():
            acc[...] = jnp.zeros_like(acc)

        acc[...] += _dot_tn(p_ref[...], d_ref[...])

        @pl.when(st == nst - 1)
        def _():
            for sh in range(N_CHIPS):
                out_ref[sh] = acc[sh * 64:(sh + 1) * 64, :]

    blk = pl.BlockSpec((ts, GROUP_DIM), lambda g, st: (st, g))
    return pl.pallas_call(
        body, name="wgrad_group", grid=(N_GROUPS, nst),
        in_specs=[blk, blk],
        out_specs=pl.BlockSpec((N_CHIPS, None, 64, GROUP_DIM), lambda g, st: (0, g, 0, 0)),
        out_shape=jax.ShapeDtypeStruct((N_CHIPS, N_GROUPS, 64, GROUP_DIM), F32),
        scratch_shapes=[pltpu.VMEM((GROUP_DIM, GROUP_DIM), F32)],
        compiler_params=_params(("parallel", "arbitrary")),
    )(pooled, dmr)


def _local_step(x, p, target, w):
    wg4 = w["a_w_group"].reshape(N_CHIPS, N_GROUPS, 64, GROUP_DIM)
    wa_out = w["a_w_out"].reshape(D_MODEL, D_MODEL)
    wb_out = w["b_w_out"].reshape(D_MODEL, D_MODEL)
    ple_w = w["ple_w"].reshape(N_CHIPS, 2, PLE_DIM, 256)
    ple_g = w["ple_gate_w"].reshape(N_CHIPS, 2, 256, D_MODEL)
    k_gain_t = jnp.tile(w["k_norm"].reshape(1, HEAD_DIM), (1, N_HEADS))
    q_gain_t = jnp.tile(w["b_q_norm"].reshape(1, HEAD_DIM), (1, N_HEADS))

    uz, h_a = _a_in(x, w["a_norm"], w["a_w_in"])
    gated_a, pooled = _a_mix(uz, wg4, w["a_scale"])
    x1, x2, e_a, gate_a = _out_ple("a_out_ple", gated_a, x, wa_out, p, 0, ple_w, ple_g)
    h_kv, h_b, k_raw, q_raw, k, q, v, z_b = _b_in(
        x2, w["kv_norm"], w["b_norm"], k_gain_t, q_gain_t, w["w_kv"], w["b_w_in"])
    o, gated_b, ltot = _attn_fwd(q, k, v, z_b)
    x3, dx4, e_b, gate_b, loss_blk = _out_ple("b_out_ple", gated_b, x2, wb_out, p, 1, ple_w, ple_g, target=target)

    de_b, dgp_b, dx3, dgated_b = _ple_out_bwd("b_ple_out_bwd", dx4, e_b, gate_b, 1, ple_g, wb_out)
    dq, dk, dv, dz_b = _attn_bwd(q, k, v, ltot, dgated_b, o, z_b)
    dqz, dkv, dx2, small_b = _b_in_bwd(dq, dk, dv, dz_b, q_raw, k_raw, x2, dx3, q_gain_t, k_gain_t,
                                       w["b_norm"], w["kv_norm"], w["b_w_in"], w["w_kv"])
    de_a, dgp_a, dx1, dgated_a = _ple_out_bwd("a_ple_out_bwd", dx2, e_a, gate_a, 0, ple_g, wa_out)
    duz, dmr, grad_x, small_a = _a_mix_bwd(dgated_a, uz, pooled, wg4, w["a_scale"], w["a_w_in"], x, dx1, w["a_norm"])

    grads = {
        "a_w_in": _wgrad("wgrad_a_in", h_a, duz, N_CHIPS),
        "a_w_group": _wgrad_group(pooled, dmr).reshape(N_CHIPS, N_GROUPS * 64, GROUP_DIM),
        "a_w_out": _wgrad("wgrad_a_out", gated_a, dx1, 1).reshape(N_CHIPS, 256, D_MODEL),
        "w_kv": _wgrad("wgrad_kv", h_kv, dkv, N_CHIPS),
        "b_w_in": _wgrad("wgrad_b_in", h_b, dqz, N_CHIPS),
        "b_w_out": _wgrad("wgrad_b_out", gated_b, dx3, 1).reshape(N_CHIPS, 256, D_MODEL),
        "ple_w": jnp.concatenate([_wgrad_ple("wgrad_ple0", p, 0, de_a), _wgrad_ple("wgrad_ple1", p, 1, de_b)], axis=1),
        "ple_gate_w": jnp.concatenate(
            [_wgrad("wgrad_gate0", x1, dgp_a, 1).reshape(N_CHIPS, 256, D_MODEL),
             _wgrad("wgrad_gate1", x3, dgp_b, 1).reshape(N_CHIPS, 256, D_MODEL)], axis=1),
    }
    fold = lambda row: jnp.pad(row.reshape(N_HEADS, HEAD_DIM).sum(axis=0), (0, D_MODEL - HEAD_DIM))
    small = jnp.stack([small_a[1], small_a[0], small_b[3], small_b[2], fold(small_b[1]), fold(small_b[0]),
                       jnp.zeros((D_MODEL,), F32), jnp.zeros((D_MODEL,), F32)])
    return loss_blk[0, 0], grad_x, grads, small


def _mesh_place():
    x, y, c = lax.axis_index("x"), lax.axis_index("y"), lax.axis_index("c")
    other_chips = [(1 - x, y), (x, 1 - y), (1 - x, 1 - y)]
    return x, y, c, other_chips


def _allgather_weights(shards, small):
    n = len(shards)

    def body(*refs):
        ins, small_in = refs[:n], refs[n]
        outs, small_out = refs[n + 1:2 * n + 1], refs[2 * n + 1]
        cast = refs[2 * n + 2:3 * n + 2]
        send_far, recv_far, send_sib, recv_sib, send_small, recv_small, local_sem = refs[3 * n + 2:]
        x, y, c, chips = _mesh_place()
        me = 2 * x + y
        sibling = (x, y, 1 - c)

        def half(k, which):
            rows = ins[k].shape[0] // 2
            return pl.ds(pl.multiple_of(which * rows, 16), rows)

        local = []
        for k in range(n):
            cast[k][...] = ins[k][...].astype(BF16)
            local.append(pltpu.make_async_copy(cast[k], outs[k].at[me], local_sem.at[k]))
            local[-1].start()
        local.append(pltpu.make_async_copy(small_in, small_out.at[me], local_sem.at[n]))
        local[-1].start()

        sends = []
        for j, (px, py) in enumerate(chips):
            for k in range(n):
                cp = pltpu.make_async_remote_copy(
                    src_ref=cast[k].at[half(k, c)], dst_ref=outs[k].at[me, half(k, c)],
                    send_sem=send_far.at[j * n + k], recv_sem=recv_far.at[j * n + k],
                    device_id=(px, py, c), device_id_type=MESH)
                cp.start()
                sends.append(cp)
            cp = pltpu.make_async_remote_copy(
                src_ref=small_in, dst_ref=small_out.at[me], send_sem=send_small.at[j], recv_sem=recv_small.at[j],
                device_id=(px, py, c), device_id_type=MESH)
            cp.start()
            sends.append(cp)

        def landed(j, k, which, sems_s, sems_r, device):
            px, py = chips[j]
            piece = outs[k].at[2 * px + py, half(k, which)]
            return pltpu.make_async_remote_copy(
                src_ref=piece, dst_ref=piece, send_sem=sems_s.at[j * n + k], recv_sem=sems_r.at[j * n + k],
                device_id=device, device_id_type=MESH)

        for j in range(len(chips)):
            for k in range(n):
                landed(j, k, c, send_far, recv_far, sibling).wait_recv()
                cp = landed(j, k, c, send_sib, recv_sib, sibling)
                cp.start()
                sends.append(cp)
        for j, (px, py) in enumerate(chips):
            for k in range(n):
                landed(j, k, 1 - c, send_sib, recv_sib, sibling).wait_recv()
            pltpu.make_async_remote_copy(
                src_ref=small_in, dst_ref=small_out.at[2 * px + py], send_sem=send_small.at[j],
                recv_sem=recv_small.at[j], device_id=(px, py, c), device_id_type=MESH).wait_recv()
        for cp in sends:
            cp.wait_send()
        for cp in local:
            cp.wait()

    vmem = pl.BlockSpec(memory_space=pltpu.VMEM)
    hbm = pl.BlockSpec(memory_space=pltpu.HBM)
    return pl.pallas_call(
        body, name="allgather_weights",
        in_specs=[vmem] * (n + 1), out_specs=[hbm] * (n + 1),
        out_shape=[jax.ShapeDtypeStruct((N_CHIPS,) + s.shape, BF16) for s in shards]
        + [jax.ShapeDtypeStruct((N_CHIPS,) + small.shape, F32)],
        scratch_shapes=[pltpu.VMEM(s.shape, BF16) for s in shards]
        + [pltpu.SemaphoreType.DMA((3 * n,)), pltpu.SemaphoreType.DMA((3 * n,)),
           pltpu.SemaphoreType.DMA((3 * n,)), pltpu.SemaphoreType.DMA((3 * n,)),
           pltpu.SemaphoreType.DMA((3,)), pltpu.SemaphoreType.DMA((3,)),
           pltpu.SemaphoreType.DMA((n + 1,))],
        compiler_params=_params(None, 40),
    )(*shards, small)


def _adamw(w, g, m, v):
    m = ADAM_B1 * m + (1.0 - ADAM_B1) * g
    v = ADAM_B2 * v + (1.0 - ADAM_B2) * (g * g)
    m_hat = m / (1.0 - ADAM_B1 ** ADAM_STEP)
    v_hat = v / (1.0 - ADAM_B2 ** ADAM_STEP)
    delta = -ADAM_LR * (m_hat / (jnp.sqrt(v_hat) + ADAM_EPS) + ADAM_WD * w)
    return delta, m, v


RS_COLS = 512
RS_PIECE_ROWS = 128


def _reduce_adam_all(grads, ws, ms, vs):
    n_w = len(grads)
    pieces = []
    for k, g in enumerate(grads):
        hr = g.shape[1] // 2
        pr = min(hr, RS_PIECE_ROWS)
        pieces += [(k, p * pr, hr, pr) for p in range(hr // pr)]
    n = len(pieces)
    P, C = RS_PIECE_ROWS, RS_COLS

    def body(*refs):
        g_in, w_in, m_in, v_in = (refs[i * n_w:(i + 1) * n_w] for i in range(4))
        g_out, d_out, m_out, v_out = (refs[(4 + i) * n_w:(5 + i) * n_w] for i in range(4))
        (gm, go, sb1, rb1, part, sb2, rb2, fin, wmv, outs,
         ld_sem, wmv_sem, s1_send, s1_recv, s2_send, s2_recv, s3_send, s3_recv, out_sem) = refs[8 * n_w:]
        x, y, c, chips = _mesh_place()
        me = 2 * x + y
        sibling = (x, y, 1 - c)

        def rows(i, which):
            _, off, hr, pr = pieces[i]
            half = c if which == 0 else 1 - c
            return pl.ds(pl.multiple_of(half * hr + off, 64), pr)

        def loads(i):
            k, _, _, pr = pieces[i]
            s = i % 3
            return [pltpu.make_async_copy(g_in[k].at[:, rows(i, 0), :], gm.at[s, :, pl.ds(0, pr), :], ld_sem.at[s, 0]),
                    pltpu.make_async_copy(g_in[k].at[:, rows(i, 1), :], go.at[s, :, pl.ds(0, pr), :], ld_sem.at[s, 1])]

        def wmv_loads(i):
            k, _, _, pr = pieces[i]
            s = i % 2
            return [pltpu.make_async_copy(src[k].at[rows(i, h), :], wmv.at[s, a, h, pl.ds(0, pr), :], wmv_sem.at[s, a, h])
                    for a, src in enumerate((w_in, m_in, v_in)) for h in range(2)]

        def stores(i):
            k, _, _, pr = pieces[i]
            s = i % 2
            return [pltpu.make_async_copy(outs.at[s, a, h, pl.ds(0, pr), :], dst[k].at[rows(i, h), :], out_sem.at[s, a, h])
                    for a, dst in enumerate((g_out, d_out, m_out, v_out)) for h in range(2)]

        def swap1(i):
            pr, s = pieces[i][3], i % 2
            return pltpu.make_async_remote_copy(
                src_ref=sb1.at[s, :, pl.ds(0, pr), :], dst_ref=rb1.at[s, :, pl.ds(0, pr), :],
                send_sem=s1_send.at[s], recv_sem=s1_recv.at[s], device_id=sibling, device_id_type=MESH)

        def far2(i, j):
            pr, s = pieces[i][3], i % 2
            px, py = chips[j]
            return pltpu.make_async_remote_copy(
                src_ref=sb2.at[s, j, pl.ds(0, pr), :], dst_ref=rb2.at[s, j, pl.ds(0, pr), :],
                send_sem=s2_send.at[s, j], recv_sem=s2_recv.at[s, j], device_id=(px, py, c), device_id_type=MESH)

        def swap3(i):
            pr, s = pieces[i][3], i % 2
            return pltpu.make_async_remote_copy(
                src_ref=fin.at[s, 0, pl.ds(0, pr), :], dst_ref=fin.at[s, 1, pl.ds(0, pr), :],
                send_sem=s3_send.at[s], recv_sem=s3_recv.at[s], device_id=sibling, device_id_type=MESH)

        def stage0(i):
            for cp in loads(i):
                cp.start()

        def stage1(i):
            pr, s, s3 = pieces[i][3], i % 2, i % 3
            for cp in loads(i):
                cp.wait()
            sb1[s, :, pl.ds(0, pr), :] = go[s3, :, pl.ds(0, pr), :].astype(BF16)
            swap1(i).start()

        def stage2(i):
            pr, s, s3 = pieces[i][3], i % 2, i % 3
            swap1(i).wait()
            part[s, :, pl.ds(0, pr), :] = gm[s3, :, pl.ds(0, pr), :] + rb1[s, :, pl.ds(0, pr), :].astype(F32)
            for j, (px, py) in enumerate(chips):
                sb2[s, j, pl.ds(0, pr), :] = part[s, 2 * px + py, pl.ds(0, pr), :].astype(BF16)
                far2(i, j).start()

        def stage3(i):
            pr, s = pieces[i][3], i % 2
            total = part[s, me, pl.ds(0, pr), :]
            for j in range(3):
                far2(i, j).wait()
                total = total + rb2[s, j, pl.ds(0, pr), :].astype(F32)
            fin[s, 0, pl.ds(0, pr), :] = total
            swap3(i).start()
            for cp in wmv_loads(i):
                cp.start()

        def stage4(i):
            pr, s = pieces[i][3], i % 2
            if i >= 2:
                for cp in stores(i - 2):
                    cp.wait()
            swap3(i).wait()
            for cp in wmv_loads(i):
                cp.wait()
            sl = (s, slice(None), pl.ds(0, pr), slice(None))
            g = fin[sl]
            delta, m_new, v_new = _adamw(wmv[(s, 0) + sl[1:]], g, wmv[(s, 1) + sl[1:]], wmv[(s, 2) + sl[1:]])
            outs[(s, 0) + sl[1:]] = g
            outs[(s, 1) + sl[1:]] = delta
            outs[(s, 2) + sl[1:]] = m_new
            outs[(s, 3) + sl[1:]] = v_new
            for cp in stores(i):
                cp.start()

        stages = (stage0, stage1, stage2, stage3, stage4)
        for t in range(n + len(stages) - 1):
            for age in reversed(range(len(stages))):
                if 0 <= t - age < n:
                    stages[age](t - age)
        for i in range(max(0, n - 2), n):
            for cp in stores(i):
                cp.wait()

    hbm = pl.BlockSpec(memory_space=pltpu.HBM)
    outs = pl.pallas_call(
        body, name="reduce_adam_all",
        in_specs=[hbm] * (4 * n_w), out_specs=[hbm] * (4 * n_w),
        out_shape=[jax.ShapeDtypeStruct(w.shape, F32) for _ in range(4) for w in ws],
        scratch_shapes=[
            pltpu.VMEM((3, N_CHIPS, P, C), F32), pltpu.VMEM((3, N_CHIPS, P, C), F32),
            pltpu.VMEM((2, N_CHIPS, P, C), BF16), pltpu.VMEM((2, N_CHIPS, P, C), BF16),
            pltpu.VMEM((2, N_CHIPS, P, C), F32),
            pltpu.VMEM((2, 3, P, C), BF16), pltpu.VMEM((2, 3, P, C), BF16),
            pltpu.VMEM((2, 2, P, C), F32),
            pltpu.VMEM((2, 3, 2, P, C), F32), pltpu.VMEM((2, 4, 2, P, C), F32),
            pltpu.SemaphoreType.DMA((3, 2)), pltpu.SemaphoreType.DMA((2, 3, 2)),
            pltpu.SemaphoreType.DMA((2,)), pltpu.SemaphoreType.DMA((2,)),
            pltpu.SemaphoreType.DMA((2, 3)), pltpu.SemaphoreType.DMA((2, 3)),
            pltpu.SemaphoreType.DMA((2,)), pltpu.SemaphoreType.DMA((2,)),
            pltpu.SemaphoreType.DMA((2, 4, 2))],
        compiler_params=_params(None, 48),
    )(*grads, *ws, *ms, *vs)
    return [outs[i * n_w:(i + 1) * n_w] for i in range(4)]


def _allreduce_small(part):
    n_dev = 8

    def body(part_ref, out_ref, buf, send_sem, recv_sem):
        x, y, c, _ = _mesh_place()
        me = 4 * x + 2 * y + c
        buf[me] = part_ref[...]
        sends = []
        for k in range(1, n_dev):
            peer = ((1 - x) if k & 4 else x, (1 - y) if k & 2 else y, (1 - c) if k & 1 else c)
            cp = pltpu.make_async_remote_copy(src_ref=part_ref, dst_ref=buf.at[me], send_sem=send_sem.at[k - 1],
                                              recv_sem=recv_sem.at[k - 1], device_id=peer, device_id_type=MESH)
            cp.start()
            sends.append(cp)
        for cp in sends:
            cp.wait_recv()
        total = buf[0]
        for s in range(1, n_dev):
            total = total + buf[s]
        out_ref[...] = total
        for cp in sends:
            cp.wait_send()

    vmem = pl.BlockSpec(memory_space=pltpu.VMEM)
    return pl.pallas_call(
        body, name="allreduce_small", in_specs=[vmem], out_specs=vmem,
        out_shape=jax.ShapeDtypeStruct(part.shape, F32),
        scratch_shapes=[pltpu.VMEM((n_dev,) + part.shape, F32),
                        pltpu.SemaphoreType.DMA((n_dev - 1,)), pltpu.SemaphoreType.DMA((n_dev - 1,))],
    )(part)


def _adam_small(w, g, m, v):
    def body(w_ref, g_ref, m_ref, v_ref, d_ref, mo_ref, vo_ref):
        delta, m_new, v_new = _adamw(w_ref[...], g_ref[...], m_ref[...], v_ref[...])
        d_ref[...] = delta
        mo_ref[...] = m_new
        vo_ref[...] = v_new

    vmem = pl.BlockSpec(memory_space=pltpu.VMEM)
    return pl.pallas_call(
        body, name="adam_small", in_specs=[vmem] * 4, out_specs=[vmem] * 3,
        out_shape=[jax.ShapeDtypeStruct(w.shape, F32)] * 3,
    )(w, g, m, v)


BIG = ("a_w_in", "a_w_group", "a_w_out", "w_kv", "b_w_in", "b_w_out", "ple_w", "ple_gate_w")
SMALL = ("a_norm", "a_scale", "kv_norm", "b_norm", "k_norm", "b_q_norm")
SMALL_SHARDED = ("a_norm", "a_scale")
WEIGHTS = ("a_norm", "a_w_in", "a_w_group", "a_scale", "a_w_out", "kv_norm", "w_kv", "k_norm", "b_norm", "b_w_in",
           "b_q_norm", "b_w_out", "ple_w", "ple_gate_w")


def _as_matrix(a):
    return a.reshape(-1, a.shape[-1])


def _pack_small(arrs):
    rows = [jnp.pad(a.reshape(-1), (0, D_MODEL - a.size)) for a in arrs]
    rows += [jnp.zeros((D_MODEL,), F32)] * (8 - len(rows))
    return jnp.stack(rows)


def kernel(x, p, a_norm, a_w_in, a_w_group, a_scale, a_w_out, kv_norm, w_kv, k_norm, b_norm, b_w_in, b_q_norm, b_w_out, ple_w, ple_gate_w, loss_target, m_a_norm, m_a_w_in, m_a_w_group, m_a_scale, m_a_w_out, m_kv_norm, m_w_kv, m_k_norm, m_b_norm, m_b_w_in, m_b_q_norm, m_b_w_out, m_ple_w, m_ple_gate_w, v_a_norm, v_a_w_in, v_a_w_group, v_a_scale, v_a_w_out, v_kv_norm, v_w_kv, v_k_norm, v_b_norm, v_b_w_in, v_b_q_norm, v_b_w_out, v_ple_w, v_ple_gate_w):
    wts = dict(a_norm=a_norm, a_w_in=a_w_in, a_w_group=a_w_group, a_scale=a_scale, a_w_out=a_w_out, kv_norm=kv_norm,
               w_kv=w_kv, k_norm=k_norm, b_norm=b_norm, b_w_in=b_w_in, b_q_norm=b_q_norm, b_w_out=b_w_out,
               ple_w=ple_w, ple_gate_w=ple_gate_w)
    mom = dict(a_norm=m_a_norm, a_w_in=m_a_w_in, a_w_group=m_a_w_group, a_scale=m_a_scale, a_w_out=m_a_w_out,
               kv_norm=m_kv_norm, w_kv=m_w_kv, k_norm=m_k_norm, b_norm=m_b_norm, b_w_in=m_b_w_in,
               b_q_norm=m_b_q_norm, b_w_out=m_b_w_out, ple_w=m_ple_w, ple_gate_w=m_ple_gate_w)
    var = dict(a_norm=v_a_norm, a_w_in=v_a_w_in, a_w_group=v_a_w_group, a_scale=v_a_scale, a_w_out=v_a_w_out,
               kv_norm=v_kv_norm, w_kv=v_w_kv, k_norm=v_k_norm, b_norm=v_b_norm, b_w_in=v_b_w_in,
               b_q_norm=v_b_q_norm, b_w_out=v_b_w_out, ple_w=v_ple_w, ple_gate_w=v_ple_gate_w)
    S = x.shape[1]
    chip = 2 * lax.axis_index("x") + lax.axis_index("y")

    sharded_small = jnp.concatenate([a_norm.reshape(1, 256), a_scale.reshape(1, 256), jnp.zeros((6, 256), F32)], axis=0)
    gathered = _allgather_weights([_as_matrix(wts[n]) for n in BIG], sharded_small)
    full = dict(zip(BIG, gathered[:-1]))
    full["a_norm"] = gathered[-1][:, 0, :].reshape(1, D_MODEL)
    full["a_scale"] = gathered[-1][:, 1, :].reshape(1, D_MODEL)
    full["kv_norm"] = kv_norm.reshape(1, D_MODEL)
    full["b_norm"] = b_norm.reshape(1, D_MODEL)
    full["k_norm"] = k_norm
    full["b_q_norm"] = b_q_norm

    loss_local, grad_x, grads, small_part = _local_step(x.reshape(S, D_MODEL), p, loss_target.reshape(S, D_MODEL), full)
    loss = lax.psum(loss_local, ("x", "y", "c"))

    out_g, out_d, out_m, out_v = {}, {}, {}, {}
    reduced = _reduce_adam_all([grads[n].reshape(N_CHIPS, -1, RS_COLS) for n in BIG],
                               *[[t[n].reshape(-1, RS_COLS) for n in BIG] for t in (wts, mom, var)])
    for out, res in zip((out_g, out_d, out_m, out_v), reduced):
        for n, t in zip(BIG, res):
            out[n] = t.reshape(wts[n].shape)

    small_sum = _allreduce_small(small_part)
    small_rows = []
    for i, n in enumerate(SMALL):
        row = small_sum[i]
        if n in SMALL_SHARDED:
            row = lax.dynamic_slice(row, (chip * 256,), (256,))
        else:
            row = row[:wts[n].size]
        small_rows.append(row)
    g_small = _pack_small(small_rows)
    d_small, m_small, v_small = _adam_small(_pack_small([wts[n] for n in SMALL]), g_small,
                                            _pack_small([mom[n] for n in SMALL]), _pack_small([var[n] for n in SMALL]))
    for i, n in enumerate(SMALL):
        shape, size = wts[n].shape, wts[n].size
        out_g[n], out_d[n], out_m[n], out_v[n] = (t[i, :size].reshape(shape) for t in (g_small, d_small, m_small, v_small))

    return (loss, grad_x.reshape(1, S, D_MODEL), *[out_g[n] for n in WEIGHTS], *[out_d[n] for n in WEIGHTS],
            *[out_m[n] for n in WEIGHTS], *[out_v[n] for n in WEIGHTS])
```

```python
import functools

import jax
import jax.numpy as jnp
from jax import lax
from jax.experimental import pallas as pl
from jax.experimental.pallas import tpu as pltpu

F32 = jnp.float32
BF16 = jnp.bfloat16
MESH = pl.DeviceIdType.MESH

D_MODEL = 1024
N_HEADS = 16
HEAD_DIM = 64
PLE_DIM = 256
N_GROUPS = 4
GROUP_DIM = 256
POOL_WINDOWS = (2, 4, 8, 16)
N_CHIPS = 4
EPS = 1e-6
SB_SCALE = HEAD_DIM ** -0.5

ADAM_LR = 0.001
ADAM_B1 = 0.9
ADAM_B2 = 0.999
ADAM_EPS = 1e-08
ADAM_WD = 0.01
ADAM_STEP = 10

ROW_TILE = 256
ATT_Q_TILE = 512
ATT_K_TILE = 256
WGRAD_SEQ_TILE = 512
MIB = 1024 * 1024


def _params(semantics=None, vmem_mib=48):
    return pltpu.CompilerParams(dimension_semantics=semantics, vmem_limit_bytes=vmem_mib * MIB)


def _dot(a, b):
    return jnp.dot(a, b, preferred_element_type=F32)


def _dot_nt(a, b):
    return lax.dot_general(a, b, (((1,), (1,)), ((), ())), preferred_element_type=F32)


def _dot_tn(a, b):
    return lax.dot_general(a, b, (((0,), (0,)), ((), ())), preferred_element_type=F32)


def _hilo(x):
    hi = x.astype(BF16)
    lo = (x - hi.astype(F32)).astype(BF16)
    return hi, lo


def _dot_hilo(x, w):
    hi, lo = _hilo(x)
    return _dot(hi, w) + _dot(lo, w)


def _sigmoid(z):
    return jax.nn.sigmoid(z)


def _dsilu(z, sg):
    return sg * (1.0 + z * (1.0 - sg))


def _mask_bf16(cond):
    return jnp.where(cond, 1.0, 0.0).astype(BF16)


def _head_mean_matrix():
    r = lax.broadcasted_iota(jnp.int32, (256, 256), 0) // HEAD_DIM
    c = lax.broadcasted_iota(jnp.int32, (256, 256), 1) // HEAD_DIM
    return _mask_bf16(r == c)


def _head_mean(x, bd):
    parts = []
    for s in range(x.shape[1] // 256):
        parts.append(_dot_hilo(x[:, s * 256:(s + 1) * 256], bd))
    out = parts[0] if len(parts) == 1 else jnp.concatenate(parts, axis=1)
    return out * (1.0 / HEAD_DIM)


def _a_in(x, gain, w_sh):
    S = x.shape[0]
    tm = 512
    nsh, _, wn = w_sh.shape

    def body(x_ref, g_ref, w_ref, uz_ref, h_ref):
        @pl.when(pl.program_id(1) == 0)
        def _():
            xv = x_ref[...]
            r = lax.rsqrt(jnp.mean(xv * xv, axis=-1, keepdims=True) + EPS)
            h_ref[...] = (xv * r * g_ref[...]).astype(BF16)

        uz_ref[...] = _dot(h_ref[...], w_ref[0])

    return pl.pallas_call(
        body, name="a_in", grid=(S // tm, nsh),
        in_specs=[pl.BlockSpec((tm, D_MODEL), lambda i, j: (i, 0)),
                  pl.BlockSpec((1, D_MODEL), lambda i, j: (0, 0)),
                  pl.BlockSpec((1, D_MODEL, wn), lambda i, j: (j, 0, 0))],
        out_specs=[pl.BlockSpec((tm, wn), lambda i, j: (i, j)),
                   pl.BlockSpec((tm, D_MODEL), lambda i, j: (i, 0))],
        out_shape=[jax.ShapeDtypeStruct((S, nsh * wn), F32),
                   jax.ShapeDtypeStruct((S, D_MODEL), BF16)],
        compiler_params=_params(("parallel", "arbitrary")),
    )(x, gain, w_sh)


def _inv_count(first_row, rows, w):
    t1 = first_row + 1 + lax.broadcasted_iota(jnp.int32, (rows, 1), 0)
    return 1.0 / jnp.minimum(t1, w).astype(F32)


def _group_weight(wg_ref, g):
    return jnp.concatenate([wg_ref[sh, g] for sh in range(N_CHIPS)], axis=0)


def _a_mix(uz, wg, scale):
    S = uz.shape[0]
    tm = ROW_TILE

    def body(u_ref, up_ref, z_ref, wg_ref, sc_ref, ga_ref, p_ref):
        i = pl.program_id(0)
        row = lax.broadcasted_iota(jnp.int32, (tm, tm), 0)
        col = lax.broadcasted_iota(jnp.int32, (tm, tm), 1)
        d = row - col
        for g, w in enumerate(POOL_WINDOWS):
            cols = slice(g * GROUP_DIM, (g + 1) * GROUP_DIM)
            t_main = _mask_bf16((d >= 0) & (d < w))
            t_halo = _mask_bf16(d + tm < w)
            u = u_ref[:, cols]
            up = jnp.where(i > 0, up_ref[:, cols], 0.0)
            hi, lo = _hilo(u)
            hip, lop = _hilo(up)
            wsum = _dot(t_main, hi) + _dot(t_main, lo) + _dot(t_halo, hip) + _dot(t_halo, lop)
            pooled = (wsum * _inv_count(i * tm, tm, w) - u).astype(BF16)
            p_ref[:, cols] = pooled
            mraw = _dot(pooled, _group_weight(wg_ref, g))
            z = z_ref[:, cols]
            ga_ref[:, cols] = (mraw * sc_ref[:, cols] * (z * _sigmoid(z))).astype(BF16)

    return pl.pallas_call(
        body, name="a_mix", grid=(S // tm,),
        in_specs=[pl.BlockSpec((tm, D_MODEL), lambda i: (i, 0)),
                  pl.BlockSpec((tm, D_MODEL), lambda i: (jnp.maximum(i - 1, 0), 0)),
                  pl.BlockSpec((tm, D_MODEL), lambda i: (i, 1)),
                  pl.BlockSpec((N_CHIPS, N_GROUPS, 64, GROUP_DIM), lambda i: (0, 0, 0, 0)),
                  pl.BlockSpec((1, D_MODEL), lambda i: (0, 0))],
        out_specs=[pl.BlockSpec((tm, D_MODEL), lambda i: (i, 0)),
                   pl.BlockSpec((tm, D_MODEL), lambda i: (i, 0))],
        out_shape=[jax.ShapeDtypeStruct((S, D_MODEL), BF16),
                   jax.ShapeDtypeStruct((S, D_MODEL), BF16)],
        compiler_params=_params(("arbitrary",)),
    )(uz, uz, uz, wg, scale)


def _out_ple(name, gated, x_in, w_out, p, layer, ple_w, ple_g, target=None):
    S = x_in.shape[0]
    tm = ROW_TILE
    with_loss = target is not None

    def body(*refs):
        if with_loss:
            g_ref, x_ref, wo_ref, p_ref, pw_ref, pg_ref, t_ref, xm_ref, dx_ref, e_ref, gt_ref, loss_ref = refs
        else:
            g_ref, x_ref, wo_ref, p_ref, pw_ref, pg_ref, xm_ref, xo_ref, e_ref, gt_ref = refs
        xm = x_ref[...] + _dot(g_ref[...], wo_ref[...])
        xm_ref[...] = xm
        pb = p_ref[...].astype(BF16)
        e = jnp.concatenate([_dot(pb, pw_ref[sh]) for sh in range(N_CHIPS)], axis=1)
        pg = jnp.concatenate([pg_ref[sh] for sh in range(N_CHIPS)], axis=0)
        gate = _sigmoid(_dot(xm.astype(BF16), pg))
        e_ref[...] = e.astype(BF16)
        gt_ref[...] = gate.astype(BF16)
        xo = xm + e * gate
        if with_loss:
            diff = xo - t_ref[...]
            dx_ref[...] = diff * (1.0 / D_MODEL)

            @pl.when(pl.program_id(0) == 0)
            def _():
                loss_ref[...] = jnp.zeros_like(loss_ref)

            loss_ref[...] += jnp.sum(diff * diff) * (0.5 / D_MODEL)
        else:
            xo_ref[...] = xo

    row = pl.BlockSpec((tm, D_MODEL), lambda i: (i, 0))
    in_specs = [row, row,
                pl.BlockSpec((D_MODEL, D_MODEL), lambda i: (0, 0)),
                pl.BlockSpec((None, None, tm, PLE_DIM), lambda i: (layer, 0, i, 0)),
                pl.BlockSpec((N_CHIPS, None, PLE_DIM, 256), lambda i: (0, layer, 0, 0)),
                pl.BlockSpec((N_CHIPS, None, 256, D_MODEL), lambda i: (0, layer, 0, 0))]
    args = [gated, x_in, w_out, p, ple_w, ple_g]
    out_specs = [row, row, row, row]
    out_shape = [jax.ShapeDtypeStruct((S, D_MODEL), F32), jax.ShapeDtypeStruct((S, D_MODEL), F32),
                 jax.ShapeDtypeStruct((S, D_MODEL), BF16), jax.ShapeDtypeStruct((S, D_MODEL), BF16)]
    if with_loss:
        in_specs.append(row)
        args.append(target)
        out_specs.append(pl.BlockSpec((8, 128), lambda i: (0, 0)))
        out_shape.append(jax.ShapeDtypeStruct((8, 128), F32))
    return pl.pallas_call(
        body, name=name, grid=(S // tm,), in_specs=in_specs, out_specs=out_specs, out_shape=out_shape,
        compiler_params=_params(("arbitrary",)),
    )(*args)


def _b_in(x, kv_gain, b_gain, k_gain_t, q_gain_t, w_kv, w_in):
    S = x.shape[0]
    tm = ROW_TILE

    def body(x_ref, kvg_ref, bg_ref, kg_ref, qg_ref, wkv_ref, win_ref,
             hkv_ref, hb_ref, kraw_ref, qraw_ref, k_ref, q_ref, v_ref, z_ref):
        xv = x_ref[...]
        y = xv * lax.rsqrt(jnp.mean(xv * xv, axis=-1, keepdims=True) + EPS)
        hkv = (y * kvg_ref[...]).astype(BF16)
        hb = (y * bg_ref[...]).astype(BF16)
        hkv_ref[...] = hkv
        hb_ref[...] = hb
        bd = _head_mean_matrix()

        def head_norm(raw, gain):
            rr = lax.rsqrt(_head_mean(raw * raw, bd) + EPS)
            return raw * rr * gain

        for sh in range(N_CHIPS):
            kvc = _dot(hkv, wkv_ref[sh])
            qzc = _dot(hb, win_ref[sh])
            cols = slice((sh % 2) * 512, (sh % 2) * 512 + 512)
            if sh < 2:
                kraw_ref[:, cols] = kvc.astype(BF16)
                qraw_ref[:, cols] = qzc.astype(BF16)
                k_ref[:, cols] = head_norm(kvc, kg_ref[:, cols]).astype(BF16)
                q_ref[:, cols] = (head_norm(qzc, qg_ref[:, cols]) * SB_SCALE).astype(BF16)
            else:
                v_ref[:, cols] = kvc.astype(BF16)
                z_ref[:, cols] = qzc.astype(BF16)

    row = pl.BlockSpec((tm, D_MODEL), lambda i: (i, 0))
    vec = pl.BlockSpec((1, D_MODEL), lambda i: (0, 0))
    wsp = pl.BlockSpec((N_CHIPS, D_MODEL, 512), lambda i: (0, 0, 0))
    return pl.pallas_call(
        body, name="b_in", grid=(S // tm,),
        in_specs=[row, vec, vec, vec, vec, wsp, wsp],
        out_specs=[row] * 8,
        out_shape=[jax.ShapeDtypeStruct((S, D_MODEL), BF16)] * 8,
        compiler_params=_params(("arbitrary",), 56),
    )(x, kv_gain, b_gain, k_gain_t, q_gain_t, w_kv, w_in)


def _softplus_parts(z):
    e = jnp.exp(-jnp.abs(z))
    return -(jnp.maximum(z, 0.0) + jnp.log(1.0 + e)), e


def _attn_fwd(q, k, v, zgate):
    S = q.shape[0]
    tq, tk = ATT_Q_TILE, ATT_K_TILE
    kpq = tq // tk

    def body(q_ref, k_ref, v_ref, z_ref, o_ref, g_ref, lt_ref):
        qi = pl.program_id(1)
        lane = lax.broadcasted_iota(jnp.int32, (1, 128), 1)
        ri = lax.broadcasted_iota(jnp.int32, (tk, tk), 0)
        ci = lax.broadcasted_iota(jnp.int32, (tk, tk), 1)
        later_mat = _mask_bf16(ri > ci)
        t_idx = qi * tq + lax.broadcasted_iota(jnp.int32, (tq, tk), 0)
        s_off = lax.broadcasted_iota(jnp.int32, (tq, tk), 1)
        qv = q_ref[...]
        first = lane < HEAD_DIM
        q_heads = (jnp.where(first, qv, jnp.zeros_like(qv)), jnp.where(first, jnp.zeros_like(qv), qv))

        def step(kj_last, carry, masked):
            chains = [(d, h) for d in range(kpq) for h in range(2)]
            s0 = [pl.multiple_of((kj_last - d) * tk, tk) for d in range(kpq)]
            kb = [k_ref[pl.ds(s, tk), :] for s in s0]
            vb = [v_ref[pl.ds(s, tk), :] for s in s0]
            visible = [s + s_off < t_idx for s in s0] if masked else None
            z = {c: _dot_nt(q_heads[c[1]], kb[c[0]]) for c in chains}
            run = [carry[0], carry[2]]
            log_own, later, run_at = {}, {}, {}
            for c in chains:
                kj, h = c
                lk = _softplus_parts(z[c])[0]
                if masked:
                    lk = jnp.where(visible[kj], lk, 0.0)
                log_own[c] = z[c] + lk
                later[c] = _dot(lk.astype(BF16), later_mat)
                run_at[c] = run[h]
                run[h] = run[h] + jnp.sum(lk, axis=-1, keepdims=True)
            acc = [carry[1], carry[3]]
            for c in chains:
                kj, h = c
                a = jnp.exp(log_own[c] + later[c] + run_at[c])
                if masked:
                    a = jnp.where(visible[kj], a, 0.0)
                acc[h] = acc[h] + _dot(a.astype(BF16), vb[kj])
            return run[0], acc[0], run[1], acc[1]

        zero1, zero128 = jnp.zeros((tq, 1), F32), jnp.zeros((tq, 128), F32)
        carry = step(qi * kpq + kpq - 1, (zero1, zero128, zero1, zero128), True)
        carry = lax.fori_loop(0, qi, lambda n, c: step((qi - n) * kpq - 1, c, False), carry)
        o_tot = jnp.where(first, carry[1], carry[3])
        l_tot = jnp.where(first, carry[0], carry[2])
        o_ref[...] = o_tot.astype(BF16)
        lt_ref[...] = l_tot
        zz = z_ref[...].astype(F32)
        g_ref[...] = (o_tot * (zz * _sigmoid(zz))).astype(BF16)

    blk = pl.BlockSpec((tq, 128), lambda hp, qi: (qi, hp))
    seq = pl.BlockSpec((S, 128), lambda hp, qi: (0, hp))
    return pl.pallas_call(
        body, name="attn_fwd", grid=(D_MODEL // 128, S // tq),
        in_specs=[blk, seq, seq, blk], out_specs=[blk, blk, blk],
        out_shape=[jax.ShapeDtypeStruct((S, D_MODEL), BF16)] * 2 + [jax.ShapeDtypeStruct((S, D_MODEL), F32)],
        compiler_params=_params(("parallel", "arbitrary")),
    )(q, k, v, zgate)


def _ple_out_bwd(name, dx_out, e, gate, layer, ple_g, w_out):
    S = dx_out.shape[0]
    tm = ROW_TILE

    def body(dx_ref, e_ref, gt_ref, pg_ref, wo_ref, de_ref, dgp_ref, dxm_ref, dg_ref):
        dxo = dx_ref[...]
        ev = e_ref[...].astype(F32)
        gv = gt_ref[...].astype(F32)
        de_ref[...] = (dxo * gv).astype(BF16)
        dgp = (dxo * ev * gv * (1.0 - gv)).astype(BF16)
        dgp_ref[...] = dgp
        pg = jnp.concatenate([pg_ref[sh] for sh in range(N_CHIPS)], axis=0)
        dxm = dxo + _dot_nt(dgp, pg)
        dxm_ref[...] = dxm
        dg_ref[...] = _dot_nt(dxm.astype(BF16), wo_ref[...]).astype(BF16)

    row = pl.BlockSpec((tm, D_MODEL), lambda i: (i, 0))
    return pl.pallas_call(
        body, name=name, grid=(S // tm,),
        in_specs=[row, row, row,
                  pl.BlockSpec((N_CHIPS, None, 256, D_MODEL), lambda i: (0, layer, 0, 0)),
                  pl.BlockSpec((D_MODEL, D_MODEL), lambda i: (0, 0))],
        out_specs=[row, row, row, row],
        out_shape=[jax.ShapeDtypeStruct((S, D_MODEL), BF16), jax.ShapeDtypeStruct((S, D_MODEL), BF16),
                   jax.ShapeDtypeStruct((S, D_MODEL), F32), jax.ShapeDtypeStruct((S, D_MODEL), BF16)],
        compiler_params=_params(("arbitrary",)),
    )(dx_out, e, gate, ple_g, w_out)


def _attn_bwd(q, k, v, ltot, dgated, o, zgate):
    S = q.shape[0]
    tq, tk = ATT_Q_TILE, ATT_K_TILE
    kpq = tq // tk
    nq = S // tq

    def body(q_ref, k_ref, v_ref, lt_ref, dg_ref, o_ref, z_ref, dq_ref, dk_ref, dv_ref, dz_ref, dk_acc, dv_acc):
        qi = pl.program_id(1)

        @pl.when(qi == 0)
        def _():
            dk_acc[...] = jnp.zeros_like(dk_acc)
            dv_acc[...] = jnp.zeros_like(dv_acc)

        lane = lax.broadcasted_iota(jnp.int32, (1, 128), 1)
        ri = lax.broadcasted_iota(jnp.int32, (tk, tk), 0)
        ci = lax.broadcasted_iota(jnp.int32, (tk, tk), 1)
        later_mat = _mask_bf16(ri > ci)
        before_mat = _mask_bf16(ri < ci)
        t_idx = qi * tq + lax.broadcasted_iota(jnp.int32, (tq, tk), 0)
        s_off = lax.broadcasted_iota(jnp.int32, (tq, tk), 1)
        zz = z_ref[...].astype(F32)
        sg = _sigmoid(zz)
        dgv = dg_ref[...].astype(F32)
        dz_ref[...] = (dgv * o_ref[...].astype(F32) * _dsilu(zz, sg)).astype(BF16)
        dob = (dgv * (zz * sg)).astype(BF16)
        ltv = lt_ref[...]
        qv = q_ref[...]
        first = lane < HEAD_DIM
        masks = (first, jnp.logical_not(first))
        q_heads = [jnp.where(hm, qv, jnp.zeros_like(qv)) for hm in masks]
        do_heads = [jnp.where(hm, dob, jnp.zeros_like(dob)) for hm in masks]
        totals = [jnp.max(jnp.where(hm, ltv, -jnp.inf), axis=-1, keepdims=True) for hm in masks]

        def step(kj_first, carry, masked):
            chains = [(d, h) for d in range(kpq) for h in range(2)]
            s0 = [pl.multiple_of((kj_first + d) * tk, tk) for d in range(kpq)]
            kb = [k_ref[pl.ds(s, tk), :] for s in s0]
            vb = [v_ref[pl.ds(s, tk), :] for s in s0]
            visible = [s + s_off < t_idx for s in s0] if masked else None
            z = {c: _dot_nt(q_heads[c[1]], kb[c[0]]) for c in chains}
            da = {c: _dot_nt(do_heads[c[1]], vb[c[0]]) for c in chains}
            run = [carry[0], carry[3]]
            log_own, beta, later, base = {}, {}, {}, {}
            for c in chains:
                kj, h = c
                lk = _softplus_parts(z[c])[0]
                if masked:
                    lk = jnp.where(visible[kj], lk, 0.0)
                log_own[c] = z[c] + lk
                beta[c] = jnp.exp(log_own[c]).astype(BF16)
                later[c] = _dot(lk.astype(BF16), later_mat)
                run[h] = run[h] + jnp.sum(lk, axis=-1, keepdims=True)
                base[c] = totals[h] - run[h]
            grun = [carry[1], carry[4]]
            a_bf, g_bf, gbefore, grun_at = {}, {}, {}, {}
            for c in chains:
                kj, h = c
                a = jnp.exp(log_own[c] + later[c] + base[c])
                if masked:
                    a = jnp.where(visible[kj], a, 0.0)
                a_bf[c] = a.astype(BF16)
                g = da[c] * a
                g_bf[c] = g.astype(BF16)
                gbefore[c] = _dot(g_bf[c], before_mat)
                grun_at[c] = grun[h]
                grun[h] = grun[h] + jnp.sum(g, axis=-1, keepdims=True)
            dq = [carry[2], carry[5]]
            dk_blk = [jnp.zeros((tk, 128), F32) for _ in range(kpq)]
            dv_blk = [jnp.zeros((tk, 128), F32) for _ in range(kpq)]
            for c in chains:
                kj, h = c
                g = g_bf[c].astype(F32)
                dz = g - beta[c].astype(F32) * (g + gbefore[c] + grun_at[c])
                if masked:
                    dz = jnp.where(visible[kj], dz, 0.0)
                dzb = dz.astype(BF16)
                dq[h] = dq[h] + _dot(dzb, kb[kj])
                dk_blk[kj] = dk_blk[kj] + _dot_tn(dzb, q_heads[h])
                dv_blk[kj] = dv_blk[kj] + _dot_tn(a_bf[c], do_heads[h])
            for d in range(kpq):
                dk_acc[pl.ds(s0[d], tk), :] += dk_blk[d]
                dv_acc[pl.ds(s0[d], tk), :] += dv_blk[d]
            return run[0], grun[0], dq[0], run[1], grun[1], dq[1]

        zero1, zero128 = jnp.zeros((tq, 1), F32), jnp.zeros((tq, 128), F32)
        carry = lax.fori_loop(0, qi, lambda n, c: step(n * kpq, c, False),
                              (zero1, zero1, zero128, zero1, zero1, zero128))
        carry = step(qi * kpq, carry, True)
        dq_ref[...] = jnp.where(first, carry[2], carry[5]).astype(BF16)

        @pl.when(qi == nq - 1)
        def _():
            dk_ref[...] = dk_acc[...].astype(BF16)
            dv_ref[...] = dv_acc[...].astype(BF16)

    blk = pl.BlockSpec((tq, 128), lambda hp, qi: (qi, hp))
    seq = pl.BlockSpec((S, 128), lambda hp, qi: (0, hp))
    return pl.pallas_call(
        body, name="attn_bwd", grid=(D_MODEL // 128, nq),
        in_specs=[blk, seq, seq, blk, blk, blk, blk], out_specs=[blk, seq, seq, blk],
        out_shape=[jax.ShapeDtypeStruct((S, D_MODEL), BF16)] * 4,
        scratch_shapes=[pltpu.VMEM((S, 128), F32), pltpu.VMEM((S, 128), F32)],
        compiler_params=_params(("parallel", "arbitrary")),
    )(q, k, v, ltot, dgated, o, zgate)


def _rms_bwd(xv, dh_gain_sum):
    r = lax.rsqrt(jnp.mean(xv * xv, axis=-1, keepdims=True) + EPS)
    xhat = xv * r
    dx = r * (dh_gain_sum - xhat * jnp.mean(dh_gain_sum * xhat, axis=-1, keepdims=True))
    return dx, xhat


def _b_in_bwd(dq, dk, dv, dz, q_raw, k_raw, x, dx_mid, q_gain_t, k_gain_t, b_gain, kv_gain, w_in, w_kv):
    S = x.shape[0]
    tm = ROW_TILE

    def body(dq_ref, dk_ref, dv_ref, dz_ref, qr_ref, kr_ref, x_ref, dxm_ref, qg_ref, kg_ref, bg_ref, kvg_ref,
             win_ref, wkv_ref, dqz_ref, dkv_ref, dx_ref, small_ref):
        @pl.when(pl.program_id(0) == 0)
        def _():
            small_ref[...] = jnp.zeros_like(small_ref)

        bd = _head_mean_matrix()

        def head_norm_bwd(dy_ref, raw_ref, gain, scale):
            raw = raw_ref[...].astype(F32)
            rr = lax.rsqrt(_head_mean(raw * raw, bd) + EPS)
            xhat = raw * rr
            dy = dy_ref[...].astype(F32) * scale
            gdy = dy * gain
            draw = rr * (gdy - xhat * _head_mean(gdy * xhat, bd))
            return draw.astype(BF16), jnp.sum(dy * xhat, axis=0, keepdims=True)

        dqr, dqg = head_norm_bwd(dq_ref, qr_ref, qg_ref[...], SB_SCALE)
        dkr, dkg = head_norm_bwd(dk_ref, kr_ref, kg_ref[...], 1.0)
        dqz_ref[:, :D_MODEL] = dqr
        dqz_ref[:, D_MODEL:] = dz_ref[...]
        dkv_ref[:, :D_MODEL] = dkr
        dkv_ref[:, D_MODEL:] = dv_ref[...]
        dhb = jnp.zeros((tm, D_MODEL), F32)
        dhkv = jnp.zeros((tm, D_MODEL), F32)
        for sh in range(N_CHIPS):
            cols = slice(sh * 512, (sh + 1) * 512)
            dhb = dhb + _dot_nt(dqz_ref[:, cols], win_ref[sh])
            dhkv = dhkv + _dot_nt(dkv_ref[:, cols], wkv_ref[sh])
        dx, xhat = _rms_bwd(x_ref[...], dhb * bg_ref[...] + dhkv * kvg_ref[...])
        dx_ref[...] = dxm_ref[...] + dx
        small_ref[0:1, :] += dqg
        small_ref[1:2, :] += dkg
        small_ref[2:3, :] += jnp.sum(dhb * xhat, axis=0, keepdims=True)
        small_ref[3:4, :] += jnp.sum(dhkv * xhat, axis=0, keepdims=True)

    row = pl.BlockSpec((tm, D_MODEL), lambda i: (i, 0))
    wide = pl.BlockSpec((tm, 2 * D_MODEL), lambda i: (i, 0))
    vec = pl.BlockSpec((1, D_MODEL), lambda i: (0, 0))
    wsp = pl.BlockSpec((N_CHIPS, D_MODEL, 512), lambda i: (0, 0, 0))
    return pl.pallas_call(
        body, name="b_in_bwd", grid=(S // tm,),
        in_specs=[row] * 8 + [vec] * 4 + [wsp, wsp],
        out_specs=[wide, wide, row, pl.BlockSpec((8, D_MODEL), lambda i: (0, 0))],
        out_shape=[jax.ShapeDtypeStruct((S, 2 * D_MODEL), BF16), jax.ShapeDtypeStruct((S, 2 * D_MODEL), BF16),
                   jax.ShapeDtypeStruct((S, D_MODEL), F32), jax.ShapeDtypeStruct((8, D_MODEL), F32)],
        compiler_params=_params(("arbitrary",), 56),
    )(dq, dk, dv, dz, q_raw, k_raw, x, dx_mid, q_gain_t, k_gain_t, b_gain, kv_gain, w_in, w_kv)


def _a_mix_bwd(dgated, uz, pooled, wg, scale, w_in, x, dx_mid, gain):
    S = x.shape[0]
    tm = ROW_TILE
    n = S // tm

    def body(dg_ref, z_ref, p_ref, wg_ref, sc_ref, win_ref, x_ref, dxm_ref, gn_ref,
             duz_ref, dmr_ref, dx_ref, small_ref, halo_hi, halo_lo):
        i = pl.program_id(0)

        @pl.when(i == 0)
        def _():
            small_ref[...] = jnp.zeros_like(small_ref)
            halo_hi[...] = jnp.zeros_like(halo_hi)
            halo_lo[...] = jnp.zeros_like(halo_lo)

        first_row = (n - 1 - i) * tm
        row = lax.broadcasted_iota(jnp.int32, (tm, tm), 0)
        col = lax.broadcasted_iota(jnp.int32, (tm, tm), 1)
        d = col - row
        for g, w in enumerate(POOL_WINDOWS):
            cols = slice(g * GROUP_DIM, (g + 1) * GROUP_DIM)
            wgg = _group_weight(wg_ref, g)
            sc = sc_ref[:, cols]
            mraw = _dot(p_ref[:, cols], wgg)
            z = z_ref[:, cols]
            sg = _sigmoid(z)
            dga = dg_ref[:, cols].astype(F32)
            dm = dga * (z * sg)
            duz_ref[:, D_MODEL + g * GROUP_DIM:D_MODEL + (g + 1) * GROUP_DIM] = (
                dga * (mraw * sc) * _dsilu(z, sg)).astype(BF16)
            small_ref[0:1, cols] += jnp.sum(dm * mraw, axis=0, keepdims=True)
            dmr = (dm * sc).astype(BF16)
            dmr_ref[:, cols] = dmr
            dp = _dot_nt(dmr, wgg)
            hi, lo = _hilo(dp * _inv_count(first_row, tm, w))
            t_main = _mask_bf16((d >= 0) & (d < w))
            t_halo = _mask_bf16(d + tm < w)
            du = (_dot(t_main, hi) + _dot(t_main, lo) + _dot(t_halo, halo_hi[:, cols]) + _dot(t_halo, halo_lo[:, cols])
                  - dp)
            halo_hi[:, cols] = hi
            halo_lo[:, cols] = lo
            duz_ref[:, cols] = du.astype(BF16)
        dh = jnp.zeros((tm, D_MODEL), F32)
        for sh in range(N_CHIPS):
            dh = dh + _dot_nt(duz_ref[:, sh * 512:(sh + 1) * 512], win_ref[sh])
        dx, xhat = _rms_bwd(x_ref[...], dh * gn_ref[...])
        dx_ref[...] = dxm_ref[...] + dx
        small_ref[1:2, :] += jnp.sum(dh * xhat, axis=0, keepdims=True)

    rev = lambda i: (n - 1 - i, 0)
    row = pl.BlockSpec((tm, D_MODEL), rev)
    vec = pl.BlockSpec((1, D_MODEL), lambda i: (0, 0))
    return pl.pallas_call(
        body, name="a_mix_bwd", grid=(n,),
        in_specs=[row,
                  pl.BlockSpec((tm, D_MODEL), lambda i: (n - 1 - i, 1)),
                  row,
                  pl.BlockSpec((N_CHIPS, N_GROUPS, 64, GROUP_DIM), lambda i: (0, 0, 0, 0)),
                  vec,
                  pl.BlockSpec((N_CHIPS, D_MODEL, 512), lambda i: (0, 0, 0)),
                  row, row, vec],
        out_specs=[pl.BlockSpec((tm, 2 * D_MODEL), rev), row, row,
                   pl.BlockSpec((8, D_MODEL), lambda i: (0, 0))],
        out_shape=[jax.ShapeDtypeStruct((S, 2 * D_MODEL), BF16), jax.ShapeDtypeStruct((S, D_MODEL), BF16),
                   jax.ShapeDtypeStruct((S, D_MODEL), F32), jax.ShapeDtypeStruct((8, D_MODEL), F32)],
        scratch_shapes=[pltpu.VMEM((tm, D_MODEL), BF16), pltpu.VMEM((tm, D_MODEL), BF16)],
        compiler_params=_params(("arbitrary",)),
    )(dgated, uz, pooled, wg, scale, w_in, x, dx_mid, gain)


def _wgrad(name, a, dy, n_shards, a_spec=None, k_dim=None):
    S = dy.shape[0]
    ts = WGRAD_SEQ_TILE
    k_dim = a.shape[-1] if k_dim is None else k_dim
    wn = dy.shape[1] // n_shards
    tk = k_dim if k_dim * wn * 4 <= 2 * MIB else 512
    nst = S // ts

    def body(a_ref, dy_ref, out_ref):
        @pl.when(pl.program_id(2) == 0)
        def _():
            out_ref[...] = jnp.zeros_like(out_ref)

        out_ref[...] += _dot_tn(a_ref[...].astype(BF16), dy_ref[...].astype(BF16))

    if a_spec is None:
        a_spec = pl.BlockSpec((ts, tk), lambda sh, kt, st: (st, kt))
    return pl.pallas_call(
        body, name=name, grid=(n_shards, k_dim // tk, nst),
        in_specs=[a_spec, pl.BlockSpec((ts, wn), lambda sh, kt, st: (st, sh))],
        out_specs=pl.BlockSpec((None, tk, wn), lambda sh, kt, st: (sh, kt, 0)),
        out_shape=jax.ShapeDtypeStruct((n_shards, k_dim, wn), F32),
        compiler_params=_params(("parallel", "parallel", "arbitrary")),
    )(a, dy)


def _wgrad_ple(name, p, layer, de):
    ts = WGRAD_SEQ_TILE
    spec = pl.BlockSpec((None, None, ts, PLE_DIM), lambda sh, kt, st: (layer, 0, st, 0))
    return _wgrad(name, p, de, N_CHIPS, a_spec=spec, k_dim=PLE_DIM)


def _wgrad_group(pooled, dmr):
    S = pooled.shape[0]
    ts = WGRAD_SEQ_TILE
    nst = S // ts

    def body(p_ref, d_ref, out_ref, acc):
        st = pl.program_id(1)

        @pl.when(st == 0)
        def _():
            acc[...] = jnp.zeros_like(acc)

        acc[...] += _dot_tn(p_ref[...], d_ref[...])

        @pl.when(st == nst - 1)
        def _():
            for sh in range(N_CHIPS):
                out_ref[sh] = acc[sh * 64:(sh + 1) * 64, :]

    blk = pl.BlockSpec((ts, GROUP_DIM), lambda g, st: (st, g))
    return pl.pallas_call(
        body, name="wgrad_group", grid=(N_GROUPS, nst),
        in_specs=[blk, blk],
        out_specs=pl.BlockSpec((N_CHIPS, None, 64, GROUP_DIM), lambda g, st: (0, g, 0, 0)),
        out_shape=jax.ShapeDtypeStruct((N_CHIPS, N_GROUPS, 64, GROUP_DIM), F32),
        scratch_shapes=[pltpu.VMEM((GROUP_DIM, GROUP_DIM), F32)],
        compiler_params=_params(("parallel", "arbitrary")),
    )(pooled, dmr)


def _local_step(x, p, target, w):
    wg4 = w["a_w_group"].reshape(N_CHIPS, N_GROUPS, 64, GROUP_DIM)
    wa_out = w["a_w_out"].reshape(D_MODEL, D_MODEL)
    wb_out = w["b_w_out"].reshape(D_MODEL, D_MODEL)
    ple_w = w["ple_w"].reshape(N_CHIPS, 2, PLE_DIM, 256)
    ple_g = w["ple_gate_w"].reshape(N_CHIPS, 2, 256, D_MODEL)
    k_gain_t = jnp.tile(w["k_norm"].reshape(1, HEAD_DIM), (1, N_HEADS))
    q_gain_t = jnp.tile(w["b_q_norm"].reshape(1, HEAD_DIM), (1, N_HEADS))

    uz, h_a = _a_in(x, w["a_norm"], w["a_w_in"])
    gated_a, pooled = _a_mix(uz, wg4, w["a_scale"])
    x1, x2, e_a, gate_a = _out_ple("a_out_ple", gated_a, x, wa_out, p, 0, ple_w, ple_g)
    h_kv, h_b, k_raw, q_raw, k, q, v, z_b = _b_in(
        x2, w["kv_norm"], w["b_norm"], k_gain_t, q_gain_t, w["w_kv"], w["b_w_in"])
    o, gated_b, ltot = _attn_fwd(q, k, v, z_b)
    x3, dx4, e_b, gate_b, loss_blk = _out_ple("b_out_ple", gated_b, x2, wb_out, p, 1, ple_w, ple_g, target=target)

    de_b, dgp_b, dx3, dgated_b = _ple_out_bwd("b_ple_out_bwd", dx4, e_b, gate_b, 1, ple_g, wb_out)
    dq, dk, dv, dz_b = _attn_bwd(q, k, v, ltot, dgated_b, o, z_b)
    dqz, dkv, dx2, small_b = _b_in_bwd(dq, dk, dv, dz_b, q_raw, k_raw, x2, dx3, q_gain_t, k_gain_t,
                                       w["b_norm"], w["kv_norm"], w["b_w_in"], w["w_kv"])
    de_a, dgp_a, dx1, dgated_a = _ple_out_bwd("a_ple_out_bwd", dx2, e_a, gate_a, 0, ple_g, wa_out)
    duz, dmr, grad_x, small_a = _a_mix_bwd(dgated_a, uz, pooled, wg4, w["a_scale"], w["a_w_in"], x, dx1, w["a_norm"])

    grads = {
        "a_w_in": _wgrad("wgrad_a_in", h_a, duz, N_CHIPS),
        "a_w_group": _wgrad_group(pooled, dmr).reshape(N_CHIPS, N_GROUPS * 64, GROUP_DIM),
        "a_w_out": _wgrad("wgrad_a_out", gated_a, dx1, 1).reshape(N_CHIPS, 256, D_MODEL),
        "w_kv": _wgrad("wgrad_kv", h_kv, dkv, N_CHIPS),
        "b_w_in": _wgrad("wgrad_b_in", h_b, dqz, N_CHIPS),
        "b_w_out": _wgrad("wgrad_b_out", gated_b, dx3, 1).reshape(N_CHIPS, 256, D_MODEL),
        "ple_w": jnp.concatenate([_wgrad_ple("wgrad_ple0", p, 0, de_a), _wgrad_ple("wgrad_ple1", p, 1, de_b)], axis=1),
        "ple_gate_w": jnp.concatenate(
            [_wgrad("wgrad_gate0", x1, dgp_a, 1).reshape(N_CHIPS, 256, D_MODEL),
             _wgrad("wgrad_gate1", x3, dgp_b, 1).reshape(N_CHIPS, 256, D_MODEL)], axis=1),
    }
    fold = lambda row: jnp.pad(row.reshape(N_HEADS, HEAD_DIM).sum(axis=0), (0, D_MODEL - HEAD_DIM))
    small = jnp.stack([small_a[1], small_a[0], small_b[3], small_b[2], fold(small_b[1]), fold(small_b[0]),
                       jnp.pad(loss_blk[0], (0, D_MODEL - loss_blk.shape[1])), jnp.zeros((D_MODEL,), F32)])
    return grad_x, grads, small


def _mesh_place():
    x, y, c = lax.axis_index("x"), lax.axis_index("y"), lax.axis_index("c")
    other_chips = [(1 - x, y), (x, 1 - y), (1 - x, 1 - y)]
    return x, y, c, other_chips


def _allgather_weights(shards, small):
    n = len(shards)

    def body(*refs):
        ins, small_in = refs[:n], refs[n]
        outs, small_out = refs[n + 1:2 * n + 1], refs[2 * n + 1]
        cast = refs[2 * n + 2:3 * n + 2]
        send_far, recv_far, send_sib, recv_sib, send_small, recv_small, local_sem = refs[3 * n + 2:]
        x, y, c, chips = _mesh_place()
        me = 2 * x + y
        sibling = (x, y, 1 - c)

        def half(k, which):
            rows = ins[k].shape[0] // 2
            return pl.ds(pl.multiple_of(which * rows, 16), rows)

        local = []
        for k in range(n):
            cast[k][...] = ins[k][...].astype(BF16)
            local.append(pltpu.make_async_copy(cast[k], outs[k].at[me], local_sem.at[k]))
            local[-1].start()
        local.append(pltpu.make_async_copy(small_in, small_out.at[me], local_sem.at[n]))
        local[-1].start()

        sends = []
        for j, (px, py) in enumerate(chips):
            for k in range(n):
                cp = pltpu.make_async_remote_copy(
                    src_ref=cast[k].at[half(k, c)], dst_ref=outs[k].at[me, half(k, c)],
                    send_sem=send_far.at[j * n + k], recv_sem=recv_far.at[j * n + k],
                    device_id=(px, py, c), device_id_type=MESH)
                cp.start()
                sends.append(cp)
            cp = pltpu.make_async_remote_copy(
                src_ref=small_in, dst_ref=small_out.at[me], send_sem=send_small.at[j], recv_sem=recv_small.at[j],
                device_id=(px, py, c), device_id_type=MESH)
            cp.start()
            sends.append(cp)

        def landed(j, k, which, sems_s, sems_r, device):
            px, py = chips[j]
            piece = outs[k].at[2 * px + py, half(k, which)]
            return pltpu.make_async_remote_copy(
                src_ref=piece, dst_ref=piece, send_sem=sems_s.at[j * n + k], recv_sem=sems_r.at[j * n + k],
                device_id=device, device_id_type=MESH)

        for j in range(len(chips)):
            for k in range(n):
                landed(j, k, c, send_far, recv_far, sibling).wait_recv()
                cp = landed(j, k, c, send_sib, recv_sib, sibling)
                cp.start()
                sends.append(cp)
        for j, (px, py) in enumerate(chips):
            for k in range(n):
                landed(j, k, 1 - c, send_sib, recv_sib, sibling).wait_recv()
            pltpu.make_async_remote_copy(
                src_ref=small_in, dst_ref=small_out.at[2 * px + py], send_sem=send_small.at[j],
                recv_sem=recv_small.at[j], device_id=(px, py, c), device_id_type=MESH).wait_recv()
        for cp in sends:
            cp.wait_send()
        for cp in local:
            cp.wait()

    vmem = pl.BlockSpec(memory_space=pltpu.VMEM)
    hbm = pl.BlockSpec(memory_space=pltpu.HBM)
    return pl.pallas_call(
        body, name="allgather_weights",
        in_specs=[vmem] * (n + 1), out_specs=[hbm] * (n + 1),
        out_shape=[jax.ShapeDtypeStruct((N_CHIPS,) + s.shape, BF16) for s in shards]
        + [jax.ShapeDtypeStruct((N_CHIPS,) + small.shape, F32)],
        scratch_shapes=[pltpu.VMEM(s.shape, BF16) for s in shards]
        + [pltpu.SemaphoreType.DMA((3 * n,)), pltpu.SemaphoreType.DMA((3 * n,)),
           pltpu.SemaphoreType.DMA((3 * n,)), pltpu.SemaphoreType.DMA((3 * n,)),
           pltpu.SemaphoreType.DMA((3,)), pltpu.SemaphoreType.DMA((3,)),
           pltpu.SemaphoreType.DMA((n + 1,))],
        compiler_params=_params(None, 40),
    )(*shards, small)


def _adamw(w, g, m, v):
    m = ADAM_B1 * m + (1.0 - ADAM_B1) * g
    v = ADAM_B2 * v + (1.0 - ADAM_B2) * (g * g)
    m_hat = m / (1.0 - ADAM_B1 ** ADAM_STEP)
    v_hat = v / (1.0 - ADAM_B2 ** ADAM_STEP)
    delta = -ADAM_LR * (m_hat / (jnp.sqrt(v_hat) + ADAM_EPS) + ADAM_WD * w)
    return delta, m, v


RS_PIECE_ROWS = 128
RS_PIECE_COLS = 512


def _reduce_adam_all(grads, ws, ms, vs):
    n_w = len(grads)
    P, C = RS_PIECE_ROWS, RS_PIECE_COLS
    pieces = []
    for k, g in enumerate(grads):
        hr, cols = g.shape[1] // 2, g.shape[2]
        pr, pc = min(hr, P), min(cols, C)
        pieces += [(k, ro, hr, co, pr, pc) for ro in range(0, hr, pr) for co in range(0, cols, pc)]
    n = len(pieces)

    def body(*refs):
        g_in, w_in, m_in, v_in = (refs[i * n_w:(i + 1) * n_w] for i in range(4))
        g_out, d_out, m_out, v_out = (refs[(4 + i) * n_w:(5 + i) * n_w] for i in range(4))
        (gm, go, sb1, rb1, part, sb2, rb2, fin, wmv, outs,
         ld_sem, wmv_sem, s1_send, s1_recv, s2_send, s2_recv, s3_send, s3_recv, out_sem) = refs[8 * n_w:]
        x, y, c, chips = _mesh_place()
        me = 2 * x + y
        sibling = (x, y, 1 - c)

        def at_hbm(i, which):
            _, ro, hr, co, pr, pc = pieces[i]
            half = c if which == 0 else 1 - c
            return pl.ds(pl.multiple_of(half * hr + ro, 64), pr), pl.ds(co, pc)

        def win(i):
            return pl.ds(0, pieces[i][4]), pl.ds(0, pieces[i][5])

        every = slice(None)

        def loads(i):
            k, s = pieces[i][0], i % 3
            return [pltpu.make_async_copy(g_in[k].at[(every,) + at_hbm(i, h)], buf.at[(s, every) + win(i)], ld_sem.at[s, h])
                    for h, buf in enumerate((gm, go))]

        def wmv_loads(i):
            k, s = pieces[i][0], i % 2
            return [pltpu.make_async_copy(src[k].at[at_hbm(i, h)], wmv.at[(s, a, h) + win(i)], wmv_sem.at[s, a, h])
                    for a, src in enumerate((w_in, m_in, v_in)) for h in range(2)]

        def stores(i):
            k, s = pieces[i][0], i % 2
            return [pltpu.make_async_copy(outs.at[(s, a, h) + win(i)], dst[k].at[at_hbm(i, h)], out_sem.at[s, a, h])
                    for a, dst in enumerate((g_out, d_out, m_out, v_out)) for h in range(2)]

        def swap1(i):
            s = i % 2
            return pltpu.make_async_remote_copy(
                src_ref=sb1.at[(s, every) + win(i)], dst_ref=rb1.at[(s, every) + win(i)],
                send_sem=s1_send.at[s], recv_sem=s1_recv.at[s], device_id=sibling, device_id_type=MESH)

        def far2(i, j):
            s = i % 2
            px, py = chips[j]
            return pltpu.make_async_remote_copy(
                src_ref=sb2.at[(s, j) + win(i)], dst_ref=rb2.at[(s, j) + win(i)],
                send_sem=s2_send.at[s, j], recv_sem=s2_recv.at[s, j], device_id=(px, py, c), device_id_type=MESH)

        def swap3(i):
            s = i % 2
            return pltpu.make_async_remote_copy(
                src_ref=fin.at[(s, 0) + win(i)], dst_ref=fin.at[(s, 1) + win(i)],
                send_sem=s3_send.at[s], recv_sem=s3_recv.at[s], device_id=sibling, device_id_type=MESH)

        def stage0(i):
            for cp in loads(i):
                cp.start()

        def stage1(i):
            s, s3 = i % 2, i % 3
            for cp in loads(i):
                cp.wait()
            sb1[(s, every) + win(i)] = go[(s3, every) + win(i)].astype(BF16)
            swap1(i).start()

        def stage2(i):
            s, s3 = i % 2, i % 3
            swap1(i).wait()
            part[(s, every) + win(i)] = gm[(s3, every) + win(i)] + rb1[(s, every) + win(i)].astype(F32)
            for j, (px, py) in enumerate(chips):
                sb2[(s, j) + win(i)] = part[(s, 2 * px + py) + win(i)].astype(BF16)
                far2(i, j).start()

        def stage3(i):
            s = i % 2
            total = part[(s, me) + win(i)]
            for j in range(3):
                far2(i, j).wait()
                total = total + rb2[(s, j) + win(i)].astype(F32)
            fin[(s, 0) + win(i)] = total
            swap3(i).start()
            for cp in wmv_loads(i):
                cp.start()

        def stage4(i):
            s = i % 2
            if i >= 2:
                for cp in stores(i - 2):
                    cp.wait()
            swap3(i).wait()
            for cp in wmv_loads(i):
                cp.wait()
            both = (every,) + win(i)
            g = fin[(s,) + both]
            delta, m_new, v_new = _adamw(wmv[(s, 0) + both], g, wmv[(s, 1) + both], wmv[(s, 2) + both])
            outs[(s, 0) + both] = g
            outs[(s, 1) + both] = delta
            outs[(s, 2) + both] = m_new
            outs[(s, 3) + both] = v_new
            for cp in stores(i):
                cp.start()

        stages = (stage0, stage1, stage2, stage3, stage4)
        for t in range(n + len(stages) - 1):
            for age in reversed(range(len(stages))):
                if 0 <= t - age < n:
                    stages[age](t - age)
        for i in range(max(0, n - 2), n):
            for cp in stores(i):
                cp.wait()

    hbm = pl.BlockSpec(memory_space=pltpu.HBM)
    outs = pl.pallas_call(
        body, name="reduce_adam_all",
        in_specs=[hbm] * (4 * n_w), out_specs=[hbm] * (4 * n_w),
        out_shape=[jax.ShapeDtypeStruct(w.shape, F32) for _ in range(4) for w in ws],
        scratch_shapes=[
            pltpu.VMEM((3, N_CHIPS, P, C), F32), pltpu.VMEM((3, N_CHIPS, P, C), F32),
            pltpu.VMEM((2, N_CHIPS, P, C), BF16), pltpu.VMEM((2, N_CHIPS, P, C), BF16),
            pltpu.VMEM((2, N_CHIPS, P, C), F32),
            pltpu.VMEM((2, 3, P, C), BF16), pltpu.VMEM((2, 3, P, C), BF16),
            pltpu.VMEM((2, 2, P, C), F32),
            pltpu.VMEM((2, 3, 2, P, C), F32), pltpu.VMEM((2, 4, 2, P, C), F32),
            pltpu.SemaphoreType.DMA((3, 2)), pltpu.SemaphoreType.DMA((2, 3, 2)),
            pltpu.SemaphoreType.DMA((2,)), pltpu.SemaphoreType.DMA((2,)),
            pltpu.SemaphoreType.DMA((2, 3)), pltpu.SemaphoreType.DMA((2, 3)),
            pltpu.SemaphoreType.DMA((2,)), pltpu.SemaphoreType.DMA((2,)),
            pltpu.SemaphoreType.DMA((2, 4, 2))],
        compiler_params=_params(None, 48),
    )(*grads, *ws, *ms, *vs)
    return [outs[i * n_w:(i + 1) * n_w] for i in range(4)]


def _allreduce_small(part):
    n_dev = 8

    def body(part_ref, out_ref, buf, send_sem, recv_sem):
        x, y, c, _ = _mesh_place()
        me = 4 * x + 2 * y + c
        buf[me] = part_ref[...]
        sends = []
        for k in range(1, n_dev):
            peer = ((1 - x) if k & 4 else x, (1 - y) if k & 2 else y, (1 - c) if k & 1 else c)
            cp = pltpu.make_async_remote_copy(src_ref=part_ref, dst_ref=buf.at[me], send_sem=send_sem.at[k - 1],
                                              recv_sem=recv_sem.at[k - 1], device_id=peer, device_id_type=MESH)
            cp.start()
            sends.append(cp)
        for cp in sends:
            cp.wait_recv()
        total = buf[0]
        for s in range(1, n_dev):
            total = total + buf[s]
        out_ref[...] = total
        for cp in sends:
            cp.wait_send()

    vmem = pl.BlockSpec(memory_space=pltpu.VMEM)
    return pl.pallas_call(
        body, name="allreduce_small", in_specs=[vmem], out_specs=vmem,
        out_shape=jax.ShapeDtypeStruct(part.shape, F32),
        scratch_shapes=[pltpu.VMEM((n_dev,) + part.shape, F32),
                        pltpu.SemaphoreType.DMA((n_dev - 1,)), pltpu.SemaphoreType.DMA((n_dev - 1,))],
    )(part)


def _adam_small(w, g, m, v):
    def body(w_ref, g_ref, m_ref, v_ref, d_ref, mo_ref, vo_ref):
        delta, m_new, v_new = _adamw(w_ref[...], g_ref[...], m_ref[...], v_ref[...])
        d_ref[...] = delta
        mo_ref[...] = m_new
        vo_ref[...] = v_new

    vmem = pl.BlockSpec(memory_space=pltpu.VMEM)
    return pl.pallas_call(
        body, name="adam_small", in_specs=[vmem] * 4, out_specs=[vmem] * 3,
        out_shape=[jax.ShapeDtypeStruct(w.shape, F32)] * 3,
    )(w, g, m, v)


BIG = ("a_w_in", "a_w_group", "a_w_out", "w_kv", "b_w_in", "b_w_out", "ple_w", "ple_gate_w")
SMALL = ("a_norm", "a_scale", "kv_norm", "b_norm", "k_norm", "b_q_norm")
SMALL_SHARDED = ("a_norm", "a_scale")
WEIGHTS = ("a_norm", "a_w_in", "a_w_group", "a_scale", "a_w_out", "kv_norm", "w_kv", "k_norm", "b_norm", "b_w_in",
           "b_q_norm", "b_w_out", "ple_w", "ple_gate_w")


def _as_matrix(a):
    return a.reshape(-1, a.shape[-1])


def _pack_small(arrs):
    rows = [jnp.pad(a.reshape(-1), (0, D_MODEL - a.size)) for a in arrs]
    rows += [jnp.zeros((D_MODEL,), F32)] * (8 - len(rows))
    return jnp.stack(rows)


def kernel(x, p, a_norm, a_w_in, a_w_group, a_scale, a_w_out, kv_norm, w_kv, k_norm, b_norm, b_w_in, b_q_norm, b_w_out, ple_w, ple_gate_w, loss_target, m_a_norm, m_a_w_in, m_a_w_group, m_a_scale, m_a_w_out, m_kv_norm, m_w_kv, m_k_norm, m_b_norm, m_b_w_in, m_b_q_norm, m_b_w_out, m_ple_w, m_ple_gate_w, v_a_norm, v_a_w_in, v_a_w_group, v_a_scale, v_a_w_out, v_kv_norm, v_w_kv, v_k_norm, v_b_norm, v_b_w_in, v_b_q_norm, v_b_w_out, v_ple_w, v_ple_gate_w):
    wts = dict(a_norm=a_norm, a_w_in=a_w_in, a_w_group=a_w_group, a_scale=a_scale, a_w_out=a_w_out, kv_norm=kv_norm,
               w_kv=w_kv, k_norm=k_norm, b_norm=b_norm, b_w_in=b_w_in, b_q_norm=b_q_norm, b_w_out=b_w_out,
               ple_w=ple_w, ple_gate_w=ple_gate_w)
    mom = dict(a_norm=m_a_norm, a_w_in=m_a_w_in, a_w_group=m_a_w_group, a_scale=m_a_scale, a_w_out=m_a_w_out,
               kv_norm=m_kv_norm, w_kv=m_w_kv, k_norm=m_k_norm, b_norm=m_b_norm, b_w_in=m_b_w_in,
               b_q_norm=m_b_q_norm, b_w_out=m_b_w_out, ple_w=m_ple_w, ple_gate_w=m_ple_gate_w)
    var = dict(a_norm=v_a_norm, a_w_in=v_a_w_in, a_w_group=v_a_w_group, a_scale=v_a_scale, a_w_out=v_a_w_out,
               kv_norm=v_kv_norm, w_kv=v_w_kv, k_norm=v_k_norm, b_norm=v_b_norm, b_w_in=v_b_w_in,
               b_q_norm=v_b_q_norm, b_w_out=v_b_w_out, ple_w=v_ple_w, ple_gate_w=v_ple_gate_w)
    S = x.shape[1]
    chip = 2 * lax.axis_index("x") + lax.axis_index("y")

    sharded_small = jnp.concatenate([a_norm.reshape(1, 256), a_scale.reshape(1, 256), jnp.zeros((6, 256), F32)], axis=0)
    gathered = _allgather_weights([_as_matrix(wts[n]) for n in BIG], sharded_small)
    full = dict(zip(BIG, gathered[:-1]))
    full["a_norm"] = gathered[-1][:, 0, :].reshape(1, D_MODEL)
    full["a_scale"] = gathered[-1][:, 1, :].reshape(1, D_MODEL)
    full["kv_norm"] = kv_norm.reshape(1, D_MODEL)
    full["b_norm"] = b_norm.reshape(1, D_MODEL)
    full["k_norm"] = k_norm
    full["b_q_norm"] = b_q_norm

    grad_x, grads, small_part = _local_step(x.reshape(S, D_MODEL), p, loss_target.reshape(S, D_MODEL), full)

    out_g, out_d, out_m, out_v = {}, {}, {}, {}
    reduced = _reduce_adam_all([grads[n] for n in BIG], *[[_as_matrix(t[n]) for n in BIG] for t in (wts, mom, var)])
    for out, res in zip((out_g, out_d, out_m, out_v), reduced):
        for n, t in zip(BIG, res):
            out[n] = t.reshape(wts[n].shape)

    small_sum = _allreduce_small(small_part)
    loss = small_sum[len(SMALL), 0]
    small_rows = []
    for i, n in enumerate(SMALL):
        row = small_sum[i]
        if n in SMALL_SHARDED:
            row = lax.dynamic_slice(row, (chip * 256,), (256,))
        else:
            row = row[:wts[n].size]
        small_rows.append(row)
    g_small = _pack_small(small_rows)
    d_small, m_small, v_small = _adam_small(_pack_small([wts[n] for n in SMALL]), g_small,
                                            _pack_small([mom[n] for n in SMALL]), _pack_small([var[n] for n in SMALL]))
    for i, n in enumerate(SMALL):
        shape, size = wts[n].shape, wts[n].size
        out_g[n], out_d[n], out_m[n], out_v[n] = (t[i, :size].reshape(shape) for t in (g_small, d_small, m_small, v_small))

    return (loss, grad_x.reshape(1, S, D_MODEL), *[out_g[n] for n in WEIGHTS], *[out_d[n] for n in WEIGHTS],
            *[out_m[n] for n in WEIGHTS], *[out_v[n] for n in WEIGHTS])
```

```python
import functools

import jax
import jax.numpy as jnp
from jax import lax
from jax.experimental import pallas as pl
from jax.experimental.pallas import tpu as pltpu

F32 = jnp.float32
BF16 = jnp.bfloat16
MESH = pl.DeviceIdType.MESH

D_MODEL = 1024
N_HEADS = 16
HEAD_DIM = 64
PLE_DIM = 256
N_GROUPS = 4
GROUP_DIM = 256
POOL_WINDOWS = (2, 4, 8, 16)
N_CHIPS = 4
EPS = 1e-6
SB_SCALE = HEAD_DIM ** -0.5

ADAM_LR = 0.001
ADAM_B1 = 0.9
ADAM_B2 = 0.999
ADAM_EPS = 1e-08
ADAM_WD = 0.01
ADAM_STEP = 10

ROW_TILE = 256
ATT_Q_TILE = 512
ATT_K_TILE = 256
WGRAD_SEQ_TILE = 1024
WGRAD_ACC_BYTES = 4 * 1024 * 1024
MIB = 1024 * 1024


def _params(semantics=None, vmem_mib=48):
    return pltpu.CompilerParams(dimension_semantics=semantics, vmem_limit_bytes=vmem_mib * MIB)


def _dot(a, b):
    return jnp.dot(a, b, preferred_element_type=F32)


def _dot_nt(a, b):
    return lax.dot_general(a, b, (((1,), (1,)), ((), ())), preferred_element_type=F32)


def _dot_tn(a, b):
    return lax.dot_general(a, b, (((0,), (0,)), ((), ())), preferred_element_type=F32)


def _hilo(x):
    hi = x.astype(BF16)
    lo = (x - hi.astype(F32)).astype(BF16)
    return hi, lo


def _dot_hilo(x, w):
    hi, lo = _hilo(x)
    return _dot(hi, w) + _dot(lo, w)


def _sigmoid(z):
    return jax.nn.sigmoid(z)


def _dsilu(z, sg):
    return sg * (1.0 + z * (1.0 - sg))


def _mask_bf16(cond):
    return jnp.where(cond, 1.0, 0.0).astype(BF16)


def _head_mean_matrix():
    r = lax.broadcasted_iota(jnp.int32, (256, 256), 0) // HEAD_DIM
    c = lax.broadcasted_iota(jnp.int32, (256, 256), 1) // HEAD_DIM
    return _mask_bf16(r == c)


def _head_mean(x, bd):
    parts = []
    for s in range(x.shape[1] // 256):
        parts.append(_dot_hilo(x[:, s * 256:(s + 1) * 256], bd))
    out = parts[0] if len(parts) == 1 else jnp.concatenate(parts, axis=1)
    return out * (1.0 / HEAD_DIM)


def _a_in(x, gain, w_sh):
    S = x.shape[0]
    tm = 512
    nsh, _, wn = w_sh.shape

    def body(x_ref, g_ref, w_ref, uz_ref, h_ref):
        @pl.when(pl.program_id(1) == 0)
        def _():
            xv = x_ref[...]
            r = lax.rsqrt(jnp.mean(xv * xv, axis=-1, keepdims=True) + EPS)
            h_ref[...] = (xv * r * g_ref[...]).astype(BF16)

        uz_ref[...] = _dot(h_ref[...], w_ref[0])

    return pl.pallas_call(
        body, name="a_in", grid=(S // tm, nsh),
        in_specs=[pl.BlockSpec((tm, D_MODEL), lambda i, j: (i, 0)),
                  pl.BlockSpec((1, D_MODEL), lambda i, j: (0, 0)),
                  pl.BlockSpec((1, D_MODEL, wn), lambda i, j: (j, 0, 0))],
        out_specs=[pl.BlockSpec((tm, wn), lambda i, j: (i, j)),
                   pl.BlockSpec((tm, D_MODEL), lambda i, j: (i, 0))],
        out_shape=[jax.ShapeDtypeStruct((S, nsh * wn), F32),
                   jax.ShapeDtypeStruct((S, D_MODEL), BF16)],
        compiler_params=_params(("parallel", "arbitrary")),
    )(x, gain, w_sh)


def _inv_count(first_row, rows, w):
    t1 = first_row + 1 + lax.broadcasted_iota(jnp.int32, (rows, 1), 0)
    return 1.0 / jnp.minimum(t1, w).astype(F32)


def _group_weight(wg_ref, g):
    return jnp.concatenate([wg_ref[sh, g] for sh in range(N_CHIPS)], axis=0)


def _a_mix(uz, wg, scale):
    S = uz.shape[0]
    tm = ROW_TILE

    def body(u_ref, up_ref, z_ref, wg_ref, sc_ref, ga_ref, p_ref):
        i = pl.program_id(0)
        row = lax.broadcasted_iota(jnp.int32, (tm, tm), 0)
        col = lax.broadcasted_iota(jnp.int32, (tm, tm), 1)
        d = row - col
        for g, w in enumerate(POOL_WINDOWS):
            cols = slice(g * GROUP_DIM, (g + 1) * GROUP_DIM)
            t_main = _mask_bf16((d >= 0) & (d < w))
            t_halo = _mask_bf16(d + tm < w)
            u = u_ref[:, cols]
            up = jnp.where(i > 0, up_ref[:, cols], 0.0)
            hi, lo = _hilo(u)
            hip, lop = _hilo(up)
            wsum = _dot(t_main, hi) + _dot(t_main, lo) + _dot(t_halo, hip) + _dot(t_halo, lop)
            pooled = (wsum * _inv_count(i * tm, tm, w) - u).astype(BF16)
            p_ref[:, cols] = pooled
            mraw = _dot(pooled, _group_weight(wg_ref, g))
            z = z_ref[:, cols]
            ga_ref[:, cols] = (mraw * sc_ref[:, cols] * (z * _sigmoid(z))).astype(BF16)

    return pl.pallas_call(
        body, name="a_mix", grid=(S // tm,),
        in_specs=[pl.BlockSpec((tm, D_MODEL), lambda i: (i, 0)),
                  pl.BlockSpec((tm, D_MODEL), lambda i: (jnp.maximum(i - 1, 0), 0)),
                  pl.BlockSpec((tm, D_MODEL), lambda i: (i, 1)),
                  pl.BlockSpec((N_CHIPS, N_GROUPS, 64, GROUP_DIM), lambda i: (0, 0, 0, 0)),
                  pl.BlockSpec((1, D_MODEL), lambda i: (0, 0))],
        out_specs=[pl.BlockSpec((tm, D_MODEL), lambda i: (i, 0)),
                   pl.BlockSpec((tm, D_MODEL), lambda i: (i, 0))],
        out_shape=[jax.ShapeDtypeStruct((S, D_MODEL), BF16),
                   jax.ShapeDtypeStruct((S, D_MODEL), BF16)],
        compiler_params=_params(("arbitrary",)),
    )(uz, uz, uz, wg, scale)


def _out_ple(name, gated, x_in, w_out, p, layer, ple_w, ple_g, target=None):
    S = x_in.shape[0]
    tm = ROW_TILE
    with_loss = target is not None

    def body(*refs):
        if with_loss:
            g_ref, x_ref, wo_ref, p_ref, pw_ref, pg_ref, t_ref, xm_ref, dx_ref, e_ref, gt_ref, loss_ref = refs
        else:
            g_ref, x_ref, wo_ref, p_ref, pw_ref, pg_ref, xm_ref, xo_ref, e_ref, gt_ref = refs
        xm = x_ref[...] + _dot(g_ref[...], wo_ref[...])
        xm_ref[...] = xm
        pb = p_ref[...].astype(BF16)
        e = jnp.concatenate([_dot(pb, pw_ref[sh]) for sh in range(N_CHIPS)], axis=1)
        pg = jnp.concatenate([pg_ref[sh] for sh in range(N_CHIPS)], axis=0)
        gate = _sigmoid(_dot(xm.astype(BF16), pg))
        e_ref[...] = e.astype(BF16)
        gt_ref[...] = gate.astype(BF16)
        xo = xm + e * gate
        if with_loss:
            diff = xo - t_ref[...]
            dx_ref[...] = diff * (1.0 / D_MODEL)

            @pl.when(pl.program_id(0) == 0)
            def _():
                loss_ref[...] = jnp.zeros_like(loss_ref)

            loss_ref[...] += jnp.sum(diff * diff) * (0.5 / D_MODEL)
        else:
            xo_ref[...] = xo

    row = pl.BlockSpec((tm, D_MODEL), lambda i: (i, 0))
    in_specs = [row, row,
                pl.BlockSpec((D_MODEL, D_MODEL), lambda i: (0, 0)),
                pl.BlockSpec((None, None, tm, PLE_DIM), lambda i: (layer, 0, i, 0)),
                pl.BlockSpec((N_CHIPS, None, PLE_DIM, 256), lambda i: (0, layer, 0, 0)),
                pl.BlockSpec((N_CHIPS, None, 256, D_MODEL), lambda i: (0, layer, 0, 0))]
    args = [gated, x_in, w_out, p, ple_w, ple_g]
    out_specs = [row, row, row, row]
    out_shape = [jax.ShapeDtypeStruct((S, D_MODEL), F32), jax.ShapeDtypeStruct((S, D_MODEL), F32),
                 jax.ShapeDtypeStruct((S, D_MODEL), BF16), jax.ShapeDtypeStruct((S, D_MODEL), BF16)]
    if with_loss:
        in_specs.append(row)
        args.append(target)
        out_specs.append(pl.BlockSpec((8, 128), lambda i: (0, 0)))
        out_shape.append(jax.ShapeDtypeStruct((8, 128), F32))
    return pl.pallas_call(
        body, name=name, grid=(S // tm,), in_specs=in_specs, out_specs=out_specs, out_shape=out_shape,
        compiler_params=_params(("arbitrary",)),
    )(*args)


def _b_in(x, kv_gain, b_gain, k_gain_t, q_gain_t, w_kv, w_in):
    S = x.shape[0]
    tm = ROW_TILE

    def body(x_ref, kvg_ref, bg_ref, kg_ref, qg_ref, wkv_ref, win_ref,
             hkv_ref, hb_ref, kraw_ref, qraw_ref, k_ref, q_ref, v_ref, z_ref):
        xv = x_ref[...]
        y = xv * lax.rsqrt(jnp.mean(xv * xv, axis=-1, keepdims=True) + EPS)
        hkv = (y * kvg_ref[...]).astype(BF16)
        hb = (y * bg_ref[...]).astype(BF16)
        hkv_ref[...] = hkv
        hb_ref[...] = hb
        bd = _head_mean_matrix()

        def head_norm(raw, gain):
            rr = lax.rsqrt(_head_mean(raw * raw, bd) + EPS)
            return raw * rr * gain

        for sh in range(N_CHIPS):
            kvc = _dot(hkv, wkv_ref[sh])
            qzc = _dot(hb, win_ref[sh])
            cols = slice((sh % 2) * 512, (sh % 2) * 512 + 512)
            if sh < 2:
                kraw_ref[:, cols] = kvc.astype(BF16)
                qraw_ref[:, cols] = qzc.astype(BF16)
                k_ref[:, cols] = head_norm(kvc, kg_ref[:, cols]).astype(BF16)
                q_ref[:, cols] = (head_norm(qzc, qg_ref[:, cols]) * SB_SCALE).astype(BF16)
            else:
                v_ref[:, cols] = kvc.astype(BF16)
                z_ref[:, cols] = qzc.astype(BF16)

    row = pl.BlockSpec((tm, D_MODEL), lambda i: (i, 0))
    vec = pl.BlockSpec((1, D_MODEL), lambda i: (0, 0))
    wsp = pl.BlockSpec((N_CHIPS, D_MODEL, 512), lambda i: (0, 0, 0))
    return pl.pallas_call(
        body, name="b_in", grid=(S // tm,),
        in_specs=[row, vec, vec, vec, vec, wsp, wsp],
        out_specs=[row] * 8,
        out_shape=[jax.ShapeDtypeStruct((S, D_MODEL), BF16)] * 8,
        compiler_params=_params(("arbitrary",), 56),
    )(x, kv_gain, b_gain, k_gain_t, q_gain_t, w_kv, w_in)


def _softplus_parts(z):
    e = jnp.exp(-jnp.abs(z))
    return -(jnp.maximum(z, 0.0) + jnp.log(1.0 + e)), e


def _add_rows(total, first_row, update):
    if first_row == 0:
        return total + update
    return jnp.concatenate([total[:first_row], total[first_row:] + update], axis=0)


def _attn_fwd(q, k, v, zgate):
    S = q.shape[0]
    tq, tk = ATT_Q_TILE, ATT_K_TILE
    kpq = tq // tk

    def body(q_ref, k_ref, v_ref, z_ref, o_ref, g_ref, lt_ref):
        qi = pl.program_id(1)
        lane = lax.broadcasted_iota(jnp.int32, (1, 128), 1)
        ri = lax.broadcasted_iota(jnp.int32, (tk, tk), 0)
        ci = lax.broadcasted_iota(jnp.int32, (tk, tk), 1)
        later_mat = _mask_bf16(ri > ci)
        t_idx = qi * tq + lax.broadcasted_iota(jnp.int32, (tq, tk), 0)
        s_off = lax.broadcasted_iota(jnp.int32, (tq, tk), 1)
        qv = q_ref[...]
        first = lane < HEAD_DIM
        q_heads = (jnp.where(first, qv, jnp.zeros_like(qv)), jnp.where(first, jnp.zeros_like(qv), qv))

        def step(kj_last, carry, masked):
            chains = [(d, h) for d in range(kpq) for h in range(2)]
            r0 = [(kpq - 1 - d) * tk if masked else 0 for d in range(kpq)]
            s0 = [pl.multiple_of((kj_last - d) * tk, tk) for d in range(kpq)]
            kb = [k_ref[pl.ds(s, tk), :] for s in s0]
            vb = [v_ref[pl.ds(s, tk), :] for s in s0]
            visible = [(s + s_off < t_idx)[r:] for s, r in zip(s0, r0)] if masked else None
            z = {c: _dot_nt(q_heads[c[1]][r0[c[0]]:], kb[c[0]]) for c in chains}
            run = [carry[0], carry[2]]
            log_own, later, run_at = {}, {}, {}
            for c in chains:
                kj, h = c
                lk = _softplus_parts(z[c])[0]
                if masked:
                    lk = jnp.where(visible[kj], lk, 0.0)
                log_own[c] = z[c] + lk
                later[c] = _dot(lk.astype(BF16), later_mat)
                run_at[c] = run[h][r0[kj]:]
                run[h] = _add_rows(run[h], r0[kj], jnp.sum(lk, axis=-1, keepdims=True))
            acc = [carry[1], carry[3]]
            for c in chains:
                kj, h = c
                a = jnp.exp(log_own[c] + later[c] + run_at[c])
                if masked:
                    a = jnp.where(visible[kj], a, 0.0)
                acc[h] = _add_rows(acc[h], r0[kj], _dot(a.astype(BF16), vb[kj]))
            return run[0], acc[0], run[1], acc[1]

        zero1, zero128 = jnp.zeros((tq, 1), F32), jnp.zeros((tq, 128), F32)
        carry = step(qi * kpq + kpq - 1, (zero1, zero128, zero1, zero128), True)
        carry = lax.fori_loop(0, qi, lambda n, c: step((qi - n) * kpq - 1, c, False), carry)
        o_tot = jnp.where(first, carry[1], carry[3])
        l_tot = jnp.where(first, carry[0], carry[2])
        o_ref[...] = o_tot.astype(BF16)
        lt_ref[...] = l_tot
        zz = z_ref[...].astype(F32)
        g_ref[...] = (o_tot * (zz * _sigmoid(zz))).astype(BF16)

    blk = pl.BlockSpec((tq, 128), lambda hp, qi: (qi, hp))
    seq = pl.BlockSpec((S, 128), lambda hp, qi: (0, hp))
    return pl.pallas_call(
        body, name="attn_fwd", grid=(D_MODEL // 128, S // tq),
        in_specs=[blk, seq, seq, blk], out_specs=[blk, blk, blk],
        out_shape=[jax.ShapeDtypeStruct((S, D_MODEL), BF16)] * 2 + [jax.ShapeDtypeStruct((S, D_MODEL), F32)],
        compiler_params=_params(("parallel", "arbitrary")),
    )(q, k, v, zgate)


def _ple_out_bwd(name, dx_out, e, gate, layer, ple_g, w_out):
    S = dx_out.shape[0]
    tm = ROW_TILE

    def body(dx_ref, e_ref, gt_ref, pg_ref, wo_ref, de_ref, dgp_ref, dxm_ref, dg_ref):
        dxo = dx_ref[...]
        ev = e_ref[...].astype(F32)
        gv = gt_ref[...].astype(F32)
        de_ref[...] = (dxo * gv).astype(BF16)
        dgp = (dxo * ev * gv * (1.0 - gv)).astype(BF16)
        dgp_ref[...] = dgp
        pg = jnp.concatenate([pg_ref[sh] for sh in range(N_CHIPS)], axis=0)
        dxm = dxo + _dot_nt(dgp, pg)
        dxm_ref[...] = dxm
        dg_ref[...] = _dot_nt(dxm.astype(BF16), wo_ref[...]).astype(BF16)

    row = pl.BlockSpec((tm, D_MODEL), lambda i: (i, 0))
    return pl.pallas_call(
        body, name=name, grid=(S // tm,),
        in_specs=[row, row, row,
                  pl.BlockSpec((N_CHIPS, None, 256, D_MODEL), lambda i: (0, layer, 0, 0)),
                  pl.BlockSpec((D_MODEL, D_MODEL), lambda i: (0, 0))],
        out_specs=[row, row, row, row],
        out_shape=[jax.ShapeDtypeStruct((S, D_MODEL), BF16), jax.ShapeDtypeStruct((S, D_MODEL), BF16),
                   jax.ShapeDtypeStruct((S, D_MODEL), F32), jax.ShapeDtypeStruct((S, D_MODEL), BF16)],
        compiler_params=_params(("arbitrary",)),
    )(dx_out, e, gate, ple_g, w_out)


def _attn_bwd(q, k, v, ltot, dgated, o, zgate):
    S = q.shape[0]
    tq, tk = ATT_Q_TILE, ATT_K_TILE
    kpq = tq // tk
    nq = S // tq

    def body(q_ref, k_ref, v_ref, lt_ref, dg_ref, o_ref, z_ref, dq_ref, dk_ref, dv_ref, dz_ref, dk_acc, dv_acc):
        qi = pl.program_id(1)

        @pl.when(qi == 0)
        def _():
            dk_acc[...] = jnp.zeros_like(dk_acc)
            dv_acc[...] = jnp.zeros_like(dv_acc)

        lane = lax.broadcasted_iota(jnp.int32, (1, 128), 1)
        ri = lax.broadcasted_iota(jnp.int32, (tk, tk), 0)
        ci = lax.broadcasted_iota(jnp.int32, (tk, tk), 1)
        later_mat = _mask_bf16(ri > ci)
        before_mat = _mask_bf16(ri < ci)
        t_idx = qi * tq + lax.broadcasted_iota(jnp.int32, (tq, tk), 0)
        s_off = lax.broadcasted_iota(jnp.int32, (tq, tk), 1)
        zz = z_ref[...].astype(F32)
        sg = _sigmoid(zz)
        dgv = dg_ref[...].astype(F32)
        dz_ref[...] = (dgv * o_ref[...].astype(F32) * _dsilu(zz, sg)).astype(BF16)
        dob = (dgv * (zz * sg)).astype(BF16)
        ltv = lt_ref[...]
        qv = q_ref[...]
        first = lane < HEAD_DIM
        masks = (first, jnp.logical_not(first))
        q_heads = [jnp.where(hm, qv, jnp.zeros_like(qv)) for hm in masks]
        do_heads = [jnp.where(hm, dob, jnp.zeros_like(dob)) for hm in masks]
        totals = [jnp.max(jnp.where(hm, ltv, -jnp.inf), axis=-1, keepdims=True) for hm in masks]

        def step(kj_first, carry, masked):
            chains = [(d, h) for d in range(kpq) for h in range(2)]
            r0 = [d * tk if masked else 0 for d in range(kpq)]
            s0 = [pl.multiple_of((kj_first + d) * tk, tk) for d in range(kpq)]
            kb = [k_ref[pl.ds(s, tk), :] for s in s0]
            vb = [v_ref[pl.ds(s, tk), :] for s in s0]
            visible = [(s + s_off < t_idx)[r:] for s, r in zip(s0, r0)] if masked else None
            z = {c: _dot_nt(q_heads[c[1]][r0[c[0]]:], kb[c[0]]) for c in chains}
            da = {c: _dot_nt(do_heads[c[1]][r0[c[0]]:], vb[c[0]]) for c in chains}
            run = [carry[0], carry[3]]
            log_own, beta, later, base = {}, {}, {}, {}
            for c in chains:
                kj, h = c
                lk = _softplus_parts(z[c])[0]
                if masked:
                    lk = jnp.where(visible[kj], lk, 0.0)
                log_own[c] = z[c] + lk
                beta[c] = jnp.exp(log_own[c]).astype(BF16)
                later[c] = _dot(lk.astype(BF16), later_mat)
                run[h] = _add_rows(run[h], r0[kj], jnp.sum(lk, axis=-1, keepdims=True))
                base[c] = (totals[h] - run[h])[r0[kj]:]
            grun = [carry[1], carry[4]]
            a_bf, g_bf, gbefore, grun_at = {}, {}, {}, {}
            for c in chains:
                kj, h = c
                a = jnp.exp(log_own[c] + later[c] + base[c])
                if masked:
                    a = jnp.where(visible[kj], a, 0.0)
                a_bf[c] = a.astype(BF16)
                g = da[c] * a
                g_bf[c] = g.astype(BF16)
                gbefore[c] = _dot(g_bf[c], before_mat)
                grun_at[c] = grun[h][r0[kj]:]
                grun[h] = _add_rows(grun[h], r0[kj], jnp.sum(g, axis=-1, keepdims=True))
            dq = [carry[2], carry[5]]
            dk_blk = [jnp.zeros((tk, 128), F32) for _ in range(kpq)]
            dv_blk = [jnp.zeros((tk, 128), F32) for _ in range(kpq)]
            for c in chains:
                kj, h = c
                g = g_bf[c].astype(F32)
                dz = g - beta[c].astype(F32) * (g + gbefore[c] + grun_at[c])
                if masked:
                    dz = jnp.where(visible[kj], dz, 0.0)
                dzb = dz.astype(BF16)
                dq[h] = _add_rows(dq[h], r0[kj], _dot(dzb, kb[kj]))
                dk_blk[kj] = dk_blk[kj] + _dot_tn(dzb, q_heads[h][r0[kj]:])
                dv_blk[kj] = dv_blk[kj] + _dot_tn(a_bf[c], do_heads[h][r0[kj]:])
            for d in range(kpq):
                dk_acc[pl.ds(s0[d], tk), :] += dk_blk[d]
                dv_acc[pl.ds(s0[d], tk), :] += dv_blk[d]
            return run[0], grun[0], dq[0], run[1], grun[1], dq[1]

        zero1, zero128 = jnp.zeros((tq, 1), F32), jnp.zeros((tq, 128), F32)
        carry = lax.fori_loop(0, qi, lambda n, c: step(n * kpq, c, False),
                              (zero1, zero1, zero128, zero1, zero1, zero128))
        carry = step(qi * kpq, carry, True)
        dq_ref[...] = jnp.where(first, carry[2], carry[5]).astype(BF16)

        @pl.when(qi == nq - 1)
        def _():
            dk_ref[...] = dk_acc[...].astype(BF16)
            dv_ref[...] = dv_acc[...].astype(BF16)

    blk = pl.BlockSpec((tq, 128), lambda hp, qi: (qi, hp))
    seq = pl.BlockSpec((S, 128), lambda hp, qi: (0, hp))
    return pl.pallas_call(
        body, name="attn_bwd", grid=(D_MODEL // 128, nq),
        in_specs=[blk, seq, seq, blk, blk, blk, blk], out_specs=[blk, seq, seq, blk],
        out_shape=[jax.ShapeDtypeStruct((S, D_MODEL), BF16)] * 4,
        scratch_shapes=[pltpu.VMEM((S, 128), F32), pltpu.VMEM((S, 128), F32)],
        compiler_params=_params(("parallel", "arbitrary")),
    )(q, k, v, ltot, dgated, o, zgate)


def _rms_bwd(xv, dh_gain_sum):
    r = lax.rsqrt(jnp.mean(xv * xv, axis=-1, keepdims=True) + EPS)
    xhat = xv * r
    dx = r * (dh_gain_sum - xhat * jnp.mean(dh_gain_sum * xhat, axis=-1, keepdims=True))
    return dx, xhat


def _b_in_bwd(dq, dk, dv, dz, q_raw, k_raw, x, dx_mid, q_gain_t, k_gain_t, b_gain, kv_gain, w_in, w_kv):
    S = x.shape[0]
    tm = ROW_TILE

    def body(dq_ref, dk_ref, dv_ref, dz_ref, qr_ref, kr_ref, x_ref, dxm_ref, qg_ref, kg_ref, bg_ref, kvg_ref,
             win_ref, wkv_ref, dqz_ref, dkv_ref, dx_ref, small_ref):
        @pl.when(pl.program_id(0) == 0)
        def _():
            small_ref[...] = jnp.zeros_like(small_ref)

        bd = _head_mean_matrix()

        def head_norm_bwd(dy_ref, raw_ref, gain, scale):
            raw = raw_ref[...].astype(F32)
            rr = lax.rsqrt(_head_mean(raw * raw, bd) + EPS)
            xhat = raw * rr
            dy = dy_ref[...].astype(F32) * scale
            gdy = dy * gain
            draw = rr * (gdy - xhat * _head_mean(gdy * xhat, bd))
            return draw.astype(BF16), jnp.sum(dy * xhat, axis=0, keepdims=True)

        dqr, dqg = head_norm_bwd(dq_ref, qr_ref, qg_ref[...], SB_SCALE)
        dkr, dkg = head_norm_bwd(dk_ref, kr_ref, kg_ref[...], 1.0)
        dqz_ref[:, :D_MODEL] = dqr
        dqz_ref[:, D_MODEL:] = dz_ref[...]
        dkv_ref[:, :D_MODEL] = dkr
        dkv_ref[:, D_MODEL:] = dv_ref[...]
        dhb = jnp.zeros((tm, D_MODEL), F32)
        dhkv = jnp.zeros((tm, D_MODEL), F32)
        for sh in range(N_CHIPS):
            cols = slice(sh * 512, (sh + 1) * 512)
            dhb = dhb + _dot_nt(dqz_ref[:, cols], win_ref[sh])
            dhkv = dhkv + _dot_nt(dkv_ref[:, cols], wkv_ref[sh])
        dx, xhat = _rms_bwd(x_ref[...], dhb * bg_ref[...] + dhkv * kvg_ref[...])
        dx_ref[...] = dxm_ref[...] + dx
        small_ref[0:1, :] += dqg
        small_ref[1:2, :] += dkg
        small_ref[2:3, :] += jnp.sum(dhb * xhat, axis=0, keepdims=True)
        small_ref[3:4, :] += jnp.sum(dhkv * xhat, axis=0, keepdims=True)

    row = pl.BlockSpec((tm, D_MODEL), lambda i: (i, 0))
    wide = pl.BlockSpec((tm, 2 * D_MODEL), lambda i: (i, 0))
    vec = pl.BlockSpec((1, D_MODEL), lambda i: (0, 0))
    wsp = pl.BlockSpec((N_CHIPS, D_MODEL, 512), lambda i: (0, 0, 0))
    return pl.pallas_call(
        body, name="b_in_bwd", grid=(S // tm,),
        in_specs=[row] * 8 + [vec] * 4 + [wsp, wsp],
        out_specs=[wide, wide, row, pl.BlockSpec((8, D_MODEL), lambda i: (0, 0))],
        out_shape=[jax.ShapeDtypeStruct((S, 2 * D_MODEL), BF16), jax.ShapeDtypeStruct((S, 2 * D_MODEL), BF16),
                   jax.ShapeDtypeStruct((S, D_MODEL), F32), jax.ShapeDtypeStruct((8, D_MODEL), F32)],
        compiler_params=_params(("arbitrary",), 56),
    )(dq, dk, dv, dz, q_raw, k_raw, x, dx_mid, q_gain_t, k_gain_t, b_gain, kv_gain, w_in, w_kv)


def _a_mix_bwd(dgated, uz, pooled, wg, scale, w_in, x, dx_mid, gain):
    S = x.shape[0]
    tm = ROW_TILE
    n = S // tm

    def body(dg_ref, z_ref, p_ref, wg_ref, sc_ref, win_ref, x_ref, dxm_ref, gn_ref,
             duz_ref, dmr_ref, dx_ref, small_ref, halo_hi, halo_lo):
        i = pl.program_id(0)

        @pl.when(i == 0)
        def _():
            small_ref[...] = jnp.zeros_like(small_ref)
            halo_hi[...] = jnp.zeros_like(halo_hi)
            halo_lo[...] = jnp.zeros_like(halo_lo)

        first_row = (n - 1 - i) * tm
        row = lax.broadcasted_iota(jnp.int32, (tm, tm), 0)
        col = lax.broadcasted_iota(jnp.int32, (tm, tm), 1)
        d = col - row
        for g, w in enumerate(POOL_WINDOWS):
            cols = slice(g * GROUP_DIM, (g + 1) * GROUP_DIM)
            wgg = _group_weight(wg_ref, g)
            sc = sc_ref[:, cols]
            mraw = _dot(p_ref[:, cols], wgg)
            z = z_ref[:, cols]
            sg = _sigmoid(z)
            dga = dg_ref[:, cols].astype(F32)
            dm = dga * (z * sg)
            duz_ref[:, D_MODEL + g * GROUP_DIM:D_MODEL + (g + 1) * GROUP_DIM] = (
                dga * (mraw * sc) * _dsilu(z, sg)).astype(BF16)
            small_ref[0:1, cols] += jnp.sum(dm * mraw, axis=0, keepdims=True)
            dmr = (dm * sc).astype(BF16)
            dmr_ref[:, cols] = dmr
            dp = _dot_nt(dmr, wgg)
            hi, lo = _hilo(dp * _inv_count(first_row, tm, w))
            t_main = _mask_bf16((d >= 0) & (d < w))
            t_halo = _mask_bf16(d + tm < w)
            du = (_dot(t_main, hi) + _dot(t_main, lo) + _dot(t_halo, halo_hi[:, cols]) + _dot(t_halo, halo_lo[:, cols])
                  - dp)
            halo_hi[:, cols] = hi
            halo_lo[:, cols] = lo
            duz_ref[:, cols] = du.astype(BF16)
        dh = jnp.zeros((tm, D_MODEL), F32)
        for sh in range(N_CHIPS):
            dh = dh + _dot_nt(duz_ref[:, sh * 512:(sh + 1) * 512], win_ref[sh])
        dx, xhat = _rms_bwd(x_ref[...], dh * gn_ref[...])
        dx_ref[...] = dxm_ref[...] + dx
        small_ref[1:2, :] += jnp.sum(dh * xhat, axis=0, keepdims=True)

    rev = lambda i: (n - 1 - i, 0)
    row = pl.BlockSpec((tm, D_MODEL), rev)
    vec = pl.BlockSpec((1, D_MODEL), lambda i: (0, 0))
    return pl.pallas_call(
        body, name="a_mix_bwd", grid=(n,),
        in_specs=[row,
                  pl.BlockSpec((tm, D_MODEL), lambda i: (n - 1 - i, 1)),
                  row,
                  pl.BlockSpec((N_CHIPS, N_GROUPS, 64, GROUP_DIM), lambda i: (0, 0, 0, 0)),
                  vec,
                  pl.BlockSpec((N_CHIPS, D_MODEL, 512), lambda i: (0, 0, 0)),
                  row, row, vec],
        out_specs=[pl.BlockSpec((tm, 2 * D_MODEL), rev), row, row,
                   pl.BlockSpec((8, D_MODEL), lambda i: (0, 0))],
        out_shape=[jax.ShapeDtypeStruct((S, 2 * D_MODEL), BF16), jax.ShapeDtypeStruct((S, D_MODEL), BF16),
                   jax.ShapeDtypeStruct((S, D_MODEL), F32), jax.ShapeDtypeStruct((8, D_MODEL), F32)],
        scratch_shapes=[pltpu.VMEM((tm, D_MODEL), BF16), pltpu.VMEM((tm, D_MODEL), BF16)],
        compiler_params=_params(("arbitrary",)),
    )(dgated, uz, pooled, wg, scale, w_in, x, dx_mid, gain)


def _wgrad(name, a, dy, n_shards, a_spec=None, k_dim=None):
    S, n_cols = dy.shape
    ts = WGRAD_SEQ_TILE
    k_dim = a.shape[-1] if k_dim is None else k_dim
    wn = n_cols // n_shards
    tk = min(k_dim, WGRAD_ACC_BYTES // (4 * n_cols))
    nst = S // ts

    def body(a_ref, dy_ref, out_ref, acc):
        st = pl.program_id(1)

        @pl.when(st == 0)
        def _():
            acc[...] = jnp.zeros_like(acc)

        acc[...] += _dot_tn(a_ref[...].astype(BF16), dy_ref[...].astype(BF16))

        @pl.when(st == nst - 1)
        def _():
            for sh in range(n_shards):
                out_ref[sh] = acc[:, sh * wn:(sh + 1) * wn]

    if a_spec is None:
        a_spec = pl.BlockSpec((ts, tk), lambda kt, st: (st, kt))
    return pl.pallas_call(
        body, name=name, grid=(k_dim // tk, nst),
        in_specs=[a_spec, pl.BlockSpec((ts, n_cols), lambda kt, st: (st, 0))],
        out_specs=pl.BlockSpec((n_shards, tk, wn), lambda kt, st: (0, kt, 0)),
        out_shape=jax.ShapeDtypeStruct((n_shards, k_dim, wn), F32),
        scratch_shapes=[pltpu.VMEM((tk, n_cols), F32)],
        compiler_params=_params(("parallel", "arbitrary")),
    )(a, dy)


def _wgrad_ple(name, p, layer, de):
    ts = WGRAD_SEQ_TILE
    spec = pl.BlockSpec((None, None, ts, PLE_DIM), lambda kt, st: (layer, 0, st, 0))
    return _wgrad(name, p, de, N_CHIPS, a_spec=spec, k_dim=PLE_DIM)


def _wgrad_group(pooled, dmr):
    S = pooled.shape[0]
    ts = WGRAD_SEQ_TILE
    nst = S // ts

    def body(p_ref, d_ref, out_ref, acc):
        st = pl.program_id(1)

        @pl.when(st == 0)
        def _():
            acc[...] = jnp.zeros_like(acc)

        acc[...] += _dot_tn(p_ref[...], d_ref[...])

        @pl.when(st == nst - 1)
        def _():
            for sh in range(N_CHIPS):
                out_ref[sh] = acc[sh * 64:(sh + 1) * 64, :]

    blk = pl.BlockSpec((ts, GROUP_DIM), lambda g, st: (st, g))
    return pl.pallas_call(
        body, name="wgrad_group", grid=(N_GROUPS, nst),
        in_specs=[blk, blk],
        out_specs=pl.BlockSpec((N_CHIPS, None, 64, GROUP_DIM), lambda g, st: (0, g, 0, 0)),
        out_shape=jax.ShapeDtypeStruct((N_CHIPS, N_GROUPS, 64, GROUP_DIM), F32),
        scratch_shapes=[pltpu.VMEM((GROUP_DIM, GROUP_DIM), F32)],
        compiler_params=_params(("parallel", "arbitrary")),
    )(pooled, dmr)


def _local_step(x, p, target, w):
    wg4 = w["a_w_group"].reshape(N_CHIPS, N_GROUPS, 64, GROUP_DIM)
    wa_out = w["a_w_out"].reshape(D_MODEL, D_MODEL)
    wb_out = w["b_w_out"].reshape(D_MODEL, D_MODEL)
    ple_w = w["ple_w"].reshape(N_CHIPS, 2, PLE_DIM, 256)
    ple_g = w["ple_gate_w"].reshape(N_CHIPS, 2, 256, D_MODEL)
    k_gain_t = jnp.tile(w["k_norm"].reshape(1, HEAD_DIM), (1, N_HEADS))
    q_gain_t = jnp.tile(w["b_q_norm"].reshape(1, HEAD_DIM), (1, N_HEADS))

    uz, h_a = _a_in(x, w["a_norm"], w["a_w_in"])
    gated_a, pooled = _a_mix(uz, wg4, w["a_scale"])
    x1, x2, e_a, gate_a = _out_ple("a_out_ple", gated_a, x, wa_out, p, 0, ple_w, ple_g)
    h_kv, h_b, k_raw, q_raw, k, q, v, z_b = _b_in(
        x2, w["kv_norm"], w["b_norm"], k_gain_t, q_gain_t, w["w_kv"], w["b_w_in"])
    o, gated_b, ltot = _attn_fwd(q, k, v, z_b)
    x3, dx4, e_b, gate_b, loss_blk = _out_ple("b_out_ple", gated_b, x2, wb_out, p, 1, ple_w, ple_g, target=target)

    de_b, dgp_b, dx3, dgated_b = _ple_out_bwd("b_ple_out_bwd", dx4, e_b, gate_b, 1, ple_g, wb_out)
    dq, dk, dv, dz_b = _attn_bwd(q, k, v, ltot, dgated_b, o, z_b)
    dqz, dkv, dx2, small_b = _b_in_bwd(dq, dk, dv, dz_b, q_raw, k_raw, x2, dx3, q_gain_t, k_gain_t,
                                       w["b_norm"], w["kv_norm"], w["b_w_in"], w["w_kv"])
    de_a, dgp_a, dx1, dgated_a = _ple_out_bwd("a_ple_out_bwd", dx2, e_a, gate_a, 0, ple_g, wa_out)
    duz, dmr, grad_x, small_a = _a_mix_bwd(dgated_a, uz, pooled, wg4, w["a_scale"], w["a_w_in"], x, dx1, w["a_norm"])

    grads = {
        "a_w_in": _wgrad("wgrad_a_in", h_a, duz, N_CHIPS),
        "a_w_group": _wgrad_group(pooled, dmr).reshape(N_CHIPS, N_GROUPS * 64, GROUP_DIM),
        "a_w_out": _wgrad("wgrad_a_out", gated_a, dx1, 1).reshape(N_CHIPS, 256, D_MODEL),
        "w_kv": _wgrad("wgrad_kv", h_kv, dkv, N_CHIPS),
        "b_w_in": _wgrad("wgrad_b_in", h_b, dqz, N_CHIPS),
        "b_w_out": _wgrad("wgrad_b_out", gated_b, dx3, 1).reshape(N_CHIPS, 256, D_MODEL),
        "ple_w": jnp.concatenate([_wgrad_ple("wgrad_ple0", p, 0, de_a), _wgrad_ple("wgrad_ple1", p, 1, de_b)], axis=1),
        "ple_gate_w": jnp.concatenate(
            [_wgrad("wgrad_gate0", x1, dgp_a, 1).reshape(N_CHIPS, 256, D_MODEL),
             _wgrad("wgrad_gate1", x3, dgp_b, 1).reshape(N_CHIPS, 256, D_MODEL)], axis=1),
    }
    fold = lambda row: jnp.pad(row.reshape(N_HEADS, HEAD_DIM).sum(axis=0), (0, D_MODEL - HEAD_DIM))
    small = jnp.stack([small_a[1], small_a[0], small_b[3], small_b[2], fold(small_b[1]), fold(small_b[0]),
                       jnp.pad(loss_blk[0], (0, D_MODEL - loss_blk.shape[1])), jnp.zeros((D_MODEL,), F32)])
    return grad_x, grads, small


def _mesh_place():
    x, y, c = lax.axis_index("x"), lax.axis_index("y"), lax.axis_index("c")
    other_chips = [(1 - x, y), (x, 1 - y), (1 - x, 1 - y)]
    return x, y, c, other_chips


def _allgather_weights(shards, small):
    n = len(shards)

    def body(*refs):
        ins, small_in = refs[:n], refs[n]
        outs, small_out = refs[n + 1:2 * n + 1], refs[2 * n + 1]
        cast = refs[2 * n + 2:3 * n + 2]
        send_far, recv_far, send_sib, recv_sib, send_small, recv_small, local_sem = refs[3 * n + 2:]
        x, y, c, chips = _mesh_place()
        me = 2 * x + y
        sibling = (x, y, 1 - c)

        def half(k, which):
            rows = ins[k].shape[0] // 2
            return pl.ds(pl.multiple_of(which * rows, 16), rows)

        local = []
        for k in range(n):
            cast[k][...] = ins[k][...].astype(BF16)
            local.append(pltpu.make_async_copy(cast[k], outs[k].at[me], local_sem.at[k]))
            local[-1].start()
        local.append(pltpu.make_async_copy(small_in, small_out.at[me], local_sem.at[n]))
        local[-1].start()

        sends = []
        for j, (px, py) in enumerate(chips):
            for k in range(n):
                cp = pltpu.make_async_remote_copy(
                    src_ref=cast[k].at[half(k, c)], dst_ref=outs[k].at[me, half(k, c)],
                    send_sem=send_far.at[j * n + k], recv_sem=recv_far.at[j * n + k],
                    device_id=(px, py, c), device_id_type=MESH)
                cp.start()
                sends.append(cp)
            cp = pltpu.make_async_remote_copy(
                src_ref=small_in, dst_ref=small_out.at[me], send_sem=send_small.at[j], recv_sem=recv_small.at[j],
                device_id=(px, py, c), device_id_type=MESH)
            cp.start()
            sends.append(cp)

        def landed(j, k, which, sems_s, sems_r, device):
            px, py = chips[j]
            piece = outs[k].at[2 * px + py, half(k, which)]
            return pltpu.make_async_remote_copy(
                src_ref=piece, dst_ref=piece, send_sem=sems_s.at[j * n + k], recv_sem=sems_r.at[j * n + k],
                device_id=device, device_id_type=MESH)

        for j in range(len(chips)):
            for k in range(n):
                landed(j, k, c, send_far, recv_far, sibling).wait_recv()
                cp = landed(j, k, c, send_sib, recv_sib, sibling)
                cp.start()
                sends.append(cp)
        for j, (px, py) in enumerate(chips):
            for k in range(n):
                landed(j, k, 1 - c, send_sib, recv_sib, sibling).wait_recv()
            pltpu.make_async_remote_copy(
                src_ref=small_in, dst_ref=small_out.at[2 * px + py], send_sem=send_small.at[j],
                recv_sem=recv_small.at[j], device_id=(px, py, c), device_id_type=MESH).wait_recv()
        for cp in sends:
            cp.wait_send()
        for cp in local:
            cp.wait()

    vmem = pl.BlockSpec(memory_space=pltpu.VMEM)
    hbm = pl.BlockSpec(memory_space=pltpu.HBM)
    return pl.pallas_call(
        body, name="allgather_weights",
        in_specs=[vmem] * (n + 1), out_specs=[hbm] * (n + 1),
        out_shape=[jax.ShapeDtypeStruct((N_CHIPS,) + s.shape, BF16) for s in shards]
        + [jax.ShapeDtypeStruct((N_CHIPS,) + small.shape, F32)],
        scratch_shapes=[pltpu.VMEM(s.shape, BF16) for s in shards]
        + [pltpu.SemaphoreType.DMA((3 * n,)), pltpu.SemaphoreType.DMA((3 * n,)),
           pltpu.SemaphoreType.DMA((3 * n,)), pltpu.SemaphoreType.DMA((3 * n,)),
           pltpu.SemaphoreType.DMA((3,)), pltpu.SemaphoreType.DMA((3,)),
           pltpu.SemaphoreType.DMA((n + 1,))],
        compiler_params=_params(None, 40),
    )(*shards, small)


def _adamw(w, g, m, v):
    m = ADAM_B1 * m + (1.0 - ADAM_B1) * g
    v = ADAM_B2 * v + (1.0 - ADAM_B2) * (g * g)
    m_hat = m / (1.0 - ADAM_B1 ** ADAM_STEP)
    v_hat = v / (1.0 - ADAM_B2 ** ADAM_STEP)
    delta = -ADAM_LR * (m_hat / (jnp.sqrt(v_hat) + ADAM_EPS) + ADAM_WD * w)
    return delta, m, v


RS_PIECE_ROWS = 128
RS_PIECE_COLS = 512


def _reduce_adam_all(grads, ws, ms, vs):
    n_w = len(grads)
    P, C = RS_PIECE_ROWS, RS_PIECE_COLS
    pieces = []
    for k, g in enumerate(grads):
        hr, cols = g.shape[1] // 2, g.shape[2]
        pr, pc = min(hr, P), min(cols, C)
        pieces += [(k, ro, hr, co, pr, pc) for ro in range(0, hr, pr) for co in range(0, cols, pc)]
    n = len(pieces)

    def body(*refs):
        g_in, w_in, m_in, v_in = (refs[i * n_w:(i + 1) * n_w] for i in range(4))
        g_out, d_out, m_out, v_out = (refs[(4 + i) * n_w:(5 + i) * n_w] for i in range(4))
        (gm, go, sb1, rb1, part, sb2, rb2, fin, wmv, outs,
         ld_sem, wmv_sem, s1_send, s1_recv, s2_send, s2_recv, s3_send, s3_recv, out_sem) = refs[8 * n_w:]
        x, y, c, chips = _mesh_place()
        me = 2 * x + y
        sibling = (x, y, 1 - c)

        def at_hbm(i, which):
            _, ro, hr, co, pr, pc = pieces[i]
            half = c if which == 0 else 1 - c
            return pl.ds(pl.multiple_of(half * hr + ro, 64), pr), pl.ds(co, pc)

        def win(i):
            return pl.ds(0, pieces[i][4]), pl.ds(0, pieces[i][5])

        every = slice(None)

        def loads(i):
            k, s = pieces[i][0], i % 3
            return [pltpu.make_async_copy(g_in[k].at[(every,) + at_hbm(i, h)], buf.at[(s, every) + win(i)], ld_sem.at[s, h])
                    for h, buf in enumerate((gm, go))]

        def wmv_loads(i):
            k, s = pieces[i][0], i % 2
            return [pltpu.make_async_copy(src[k].at[at_hbm(i, h)], wmv.at[(s, a, h) + win(i)], wmv_sem.at[s, a, h])
                    for a, src in enumerate((w_in, m_in, v_in)) for h in range(2)]

        def stores(i):
            k, s = pieces[i][0], i % 2
            return [pltpu.make_async_copy(outs.at[(s, a, h) + win(i)], dst[k].at[at_hbm(i, h)], out_sem.at[s, a, h])
                    for a, dst in enumerate((g_out, d_out, m_out, v_out)) for h in range(2)]

        def swap1(i):
            s = i % 2
            return pltpu.make_async_remote_copy(
                src_ref=sb1.at[(s, every) + win(i)], dst_ref=rb1.at[(s, every) + win(i)],
                send_sem=s1_send.at[s], recv_sem=s1_recv.at[s], device_id=sibling, device_id_type=MESH)

        def far2(i, j):
            s = i % 2
            px, py = chips[j]
            return pltpu.make_async_remote_copy(
                src_ref=sb2.at[(s, j) + win(i)], dst_ref=rb2.at[(s, j) + win(i)],
                send_sem=s2_send.at[s, j], recv_sem=s2_recv.at[s, j], device_id=(px, py, c), device_id_type=MESH)

        def swap3(i):
            s = i % 2
            return pltpu.make_async_remote_copy(
                src_ref=fin.at[(s, 0) + win(i)], dst_ref=fin.at[(s, 1) + win(i)],
                send_sem=s3_send.at[s], recv_sem=s3_recv.at[s], device_id=sibling, device_id_type=MESH)

        def stage0(i):
            for cp in loads(i):
                cp.start()

        def stage1(i):
            s, s3 = i % 2, i % 3
            for cp in loads(i):
                cp.wait()
            sb1[(s, every) + win(i)] = go[(s3, every) + win(i)].astype(BF16)
            swap1(i).start()

        def stage2(i):
            s, s3 = i % 2, i % 3
            swap1(i).wait()
            part[(s, every) + win(i)] = gm[(s3, every) + win(i)] + rb1[(s, every) + win(i)].astype(F32)
            for j, (px, py) in enumerate(chips):
                sb2[(s, j) + win(i)] = part[(s, 2 * px + py) + win(i)].astype(BF16)
                far2(i, j).start()

        def stage3(i):
            s = i % 2
            total = part[(s, me) + win(i)]
            for j in range(3):
                far2(i, j).wait()
                total = total + rb2[(s, j) + win(i)].astype(F32)
            fin[(s, 0) + win(i)] = total
            swap3(i).start()
            for cp in wmv_loads(i):
                cp.start()

        def stage4(i):
            s = i % 2
            if i >= 2:
                for cp in stores(i - 2):
                    cp.wait()
            swap3(i).wait()
            for cp in wmv_loads(i):
                cp.wait()
            both = (every,) + win(i)
            g = fin[(s,) + both]
            delta, m_new, v_new = _adamw(wmv[(s, 0) + both], g, wmv[(s, 1) + both], wmv[(s, 2) + both])
            outs[(s, 0) + both] = g
            outs[(s, 1) + both] = delta
            outs[(s, 2) + both] = m_new
            outs[(s, 3) + both] = v_new
            for cp in stores(i):
                cp.start()

        stages = (stage0, stage1, stage2, stage3, stage4)
        for t in range(n + len(stages) - 1):
            for age in reversed(range(len(stages))):
                if 0 <= t - age < n:
                    stages[age](t - age)
        for i in range(max(0, n - 2), n):
            for cp in stores(i):
                cp.wait()

    hbm = pl.BlockSpec(memory_space=pltpu.HBM)
    outs = pl.pallas_call(
        body, name="reduce_adam_all",
        in_specs=[hbm] * (4 * n_w), out_specs=[hbm] * (4 * n_w),
        out_shape=[jax.ShapeDtypeStruct(w.shape, F32) for _ in range(4) for w in ws],
        scratch_shapes=[
            pltpu.VMEM((3, N_CHIPS, P, C), F32), pltpu.VMEM((3, N_CHIPS, P, C), F32),
            pltpu.VMEM((2, N_CHIPS, P, C), BF16), pltpu.VMEM((2, N_CHIPS, P, C), BF16),
            pltpu.VMEM((2, N_CHIPS, P, C), F32),
            pltpu.VMEM((2, 3, P, C), BF16), pltpu.VMEM((2, 3, P, C), BF16),
            pltpu.VMEM((2, 2, P, C), F32),
            pltpu.VMEM((2, 3, 2, P, C), F32), pltpu.VMEM((2, 4, 2, P, C), F32),
            pltpu.SemaphoreType.DMA((3, 2)), pltpu.SemaphoreType.DMA((2, 3, 2)),
            pltpu.SemaphoreType.DMA((2,)), pltpu.SemaphoreType.DMA((2,)),
            pltpu.SemaphoreType.DMA((2, 3)), pltpu.SemaphoreType.DMA((2, 3)),
            pltpu.SemaphoreType.DMA((2,)), pltpu.SemaphoreType.DMA((2,)),
            pltpu.SemaphoreType.DMA((2, 4, 2))],
        compiler_params=_params(None, 48),
    )(*grads, *ws, *ms, *vs)
    return [outs[i * n_w:(i + 1) * n_w] for i in range(4)]


def _allreduce_small(part):
    n_dev = 8

    def body(part_ref, out_ref, buf, send_sem, recv_sem):
        x, y, c, _ = _mesh_place()
        me = 4 * x + 2 * y + c
        buf[me] = part_ref[...]
        sends = []
        for k in range(1, n_dev):
            peer = ((1 - x) if k & 4 else x, (1 - y) if k & 2 else y, (1 - c) if k & 1 else c)
            cp = pltpu.make_async_remote_copy(src_ref=part_ref, dst_ref=buf.at[me], send_sem=send_sem.at[k - 1],
                                              recv_sem=recv_sem.at[k - 1], device_id=peer, device_id_type=MESH)
            cp.start()
            sends.append(cp)
        for cp in sends:
            cp.wait_recv()
        total = buf[0]
        for s in range(1, n_dev):
            total = total + buf[s]
        out_ref[...] = total
        for cp in sends:
            cp.wait_send()

    vmem = pl.BlockSpec(memory_space=pltpu.VMEM)
    return pl.pallas_call(
        body, name="allreduce_small", in_specs=[vmem], out_specs=vmem,
        out_shape=jax.ShapeDtypeStruct(part.shape, F32),
        scratch_shapes=[pltpu.VMEM((n_dev,) + part.shape, F32),
                        pltpu.SemaphoreType.DMA((n_dev - 1,)), pltpu.SemaphoreType.DMA((n_dev - 1,))],
    )(part)


def _adam_small(w, g, m, v):
    def body(w_ref, g_ref, m_ref, v_ref, d_ref, mo_ref, vo_ref):
        delta, m_new, v_new = _adamw(w_ref[...], g_ref[...], m_ref[...], v_ref[...])
        d_ref[...] = delta
        mo_ref[...] = m_new
        vo_ref[...] = v_new

    vmem = pl.BlockSpec(memory_space=pltpu.VMEM)
    return pl.pallas_call(
        body, name="adam_small", in_specs=[vmem] * 4, out_specs=[vmem] * 3,
        out_shape=[jax.ShapeDtypeStruct(w.shape, F32)] * 3,
    )(w, g, m, v)


BIG = ("a_w_in", "a_w_group", "a_w_out", "w_kv", "b_w_in", "b_w_out", "ple_w", "ple_gate_w")
SMALL = ("a_norm", "a_scale", "kv_norm", "b_norm", "k_norm", "b_q_norm")
SMALL_SHARDED = ("a_norm", "a_scale")
WEIGHTS = ("a_norm", "a_w_in", "a_w_group", "a_scale", "a_w_out", "kv_norm", "w_kv", "k_norm", "b_norm", "b_w_in",
           "b_q_norm", "b_w_out", "ple_w", "ple_gate_w")


def _as_matrix(a):
    return a.reshape(-1, a.shape[-1])


def _pack_small(arrs):
    rows = [jnp.pad(a.reshape(-1), (0, D_MODEL - a.size)) for a in arrs]
    rows += [jnp.zeros((D_MODEL,), F32)] * (8 - len(rows))
    return jnp.stack(rows)


def kernel(x, p, a_norm, a_w_in, a_w_group, a_scale, a_w_out, kv_norm, w_kv, k_norm, b_norm, b_w_in, b_q_norm, b_w_out, ple_w, ple_gate_w, loss_target, m_a_norm, m_a_w_in, m_a_w_group, m_a_scale, m_a_w_out, m_kv_norm, m_w_kv, m_k_norm, m_b_norm, m_b_w_in, m_b_q_norm, m_b_w_out, m_ple_w, m_ple_gate_w, v_a_norm, v_a_w_in, v_a_w_group, v_a_scale, v_a_w_out, v_kv_norm, v_w_kv, v_k_norm, v_b_norm, v_b_w_in, v_b_q_norm, v_b_w_out, v_ple_w, v_ple_gate_w):
    wts = dict(a_norm=a_norm, a_w_in=a_w_in, a_w_group=a_w_group, a_scale=a_scale, a_w_out=a_w_out, kv_norm=kv_norm,
               w_kv=w_kv, k_norm=k_norm, b_norm=b_norm, b_w_in=b_w_in, b_q_norm=b_q_norm, b_w_out=b_w_out,
               ple_w=ple_w, ple_gate_w=ple_gate_w)
    mom = dict(a_norm=m_a_norm, a_w_in=m_a_w_in, a_w_group=m_a_w_group, a_scale=m_a_scale, a_w_out=m_a_w_out,
               kv_norm=m_kv_norm, w_kv=m_w_kv, k_norm=m_k_norm, b_norm=m_b_norm, b_w_in=m_b_w_in,
               b_q_norm=m_b_q_norm, b_w_out=m_b_w_out, ple_w=m_ple_w, ple_gate_w=m_ple_gate_w)
    var = dict(a_norm=v_a_norm, a_w_in=v_a_w_in, a_w_group=v_a_w_group, a_scale=v_a_scale, a_w_out=v_a_w_out,
               kv_norm=v_kv_norm, w_kv=v_w_kv, k_norm=v_k_norm, b_norm=v_b_norm, b_w_in=v_b_w_in,
               b_q_norm=v_b_q_norm, b_w_out=v_b_w_out, ple_w=v_ple_w, ple_gate_w=v_ple_gate_w)
    S = x.shape[1]
    chip = 2 * lax.axis_index("x") + lax.axis_index("y")

    sharded_small = jnp.concatenate([a_norm.reshape(1, 256), a_scale.reshape(1, 256), jnp.zeros((6, 256), F32)], axis=0)
    gathered = _allgather_weights([_as_matrix(wts[n]) for n in BIG], sharded_small)
    full = dict(zip(BIG, gathered[:-1]))
    full["a_norm"] = gathered[-1][:, 0, :].reshape(1, D_MODEL)
    full["a_scale"] = gathered[-1][:, 1, :].reshape(1, D_MODEL)
    full["kv_norm"] = kv_norm.reshape(1, D_MODEL)
    full["b_norm"] = b_norm.reshape(1, D_MODEL)
    full["k_norm"] = k_norm
    full["b_q_norm"] = b_q_norm

    grad_x, grads, small_part = _local_step(x.reshape(S, D_MODEL), p, loss_target.reshape(S, D_MODEL), full)

    out_g, out_d, out_m, out_v = {}, {}, {}, {}
    reduced = _reduce_adam_all([grads[n] for n in BIG], *[[_as_matrix(t[n]) for n in BIG] for t in (wts, mom, var)])
    for out, res in zip((out_g, out_d, out_m, out_v), reduced):
        for n, t in zip(BIG, res):
            out[n] = t.reshape(wts[n].shape)

    small_sum = _allreduce_small(small_part)
    loss = small_sum[len(SMALL), 0]
    small_rows = []
    for i, n in enumerate(SMALL):
        row = small_sum[i]
        if n in SMALL_SHARDED:
            row = lax.dynamic_slice(row, (chip * 256,), (256,))
        else:
            row = row[:wts[n].size]
        small_rows.append(row)
    g_small = _pack_small(small_rows)
    d_small, m_small, v_small = _adam_small(_pack_small([wts[n] for n in SMALL]), g_small,
                                            _pack_small([mom[n] for n in SMALL]), _pack_small([var[n] for n in SMALL]))
    for i, n in enumerate(SMALL):
        shape, size = wts[n].shape, wts[n].size
        out_g[n], out_d[n], out_m[n], out_v[n] = (t[i, :size].reshape(shape) for t in (g_small, d_small, m_small, v_small))

    return (loss, grad_x.reshape(1, S, D_MODEL), *[out_g[n] for n in WEIGHTS], *[out_d[n] for n in WEIGHTS],
            *[out_m[n] for n in WEIGHTS], *[out_v[n] for n in WEIGHTS])
```

```python
import functools

import jax
import jax.numpy as jnp
from jax import lax
from jax.experimental import pallas as pl
from jax.experimental.pallas import tpu as pltpu

F32 = jnp.float32
BF16 = jnp.bfloat16
MESH = pl.DeviceIdType.MESH

D_MODEL = 1024
N_HEADS = 16
HEAD_DIM = 64
PLE_DIM = 256
N_GROUPS = 4
GROUP_DIM = 256
POOL_WINDOWS = (2, 4, 8, 16)
N_CHIPS = 4
EPS = 1e-6
SB_SCALE = HEAD_DIM ** -0.5

ADAM_LR = 0.001
ADAM_B1 = 0.9
ADAM_B2 = 0.999
ADAM_EPS = 1e-08
ADAM_WD = 0.01
ADAM_STEP = 10

ROW_TILE = 256
ATT_Q_TILE = 512
ATT_K_TILE = 256
WGRAD_SEQ_TILE = 1024
WGRAD_ACC_BYTES = 4 * 1024 * 1024
MIB = 1024 * 1024


def _params(semantics=None, vmem_mib=48):
    return pltpu.CompilerParams(dimension_semantics=semantics, vmem_limit_bytes=vmem_mib * MIB)


def _dot(a, b):
    return jnp.dot(a, b, preferred_element_type=F32)


def _dot_nt(a, b):
    return lax.dot_general(a, b, (((1,), (1,)), ((), ())), preferred_element_type=F32)


def _dot_tn(a, b):
    return lax.dot_general(a, b, (((0,), (0,)), ((), ())), preferred_element_type=F32)


def _hilo(x):
    hi = x.astype(BF16)
    lo = (x - hi.astype(F32)).astype(BF16)
    return hi, lo


def _dot_hilo(x, w):
    hi, lo = _hilo(x)
    return _dot(hi, w) + _dot(lo, w)


def _sigmoid(z):
    return jax.nn.sigmoid(z)


def _dsilu(z, sg):
    return sg * (1.0 + z * (1.0 - sg))


def _mask_bf16(cond):
    return jnp.where(cond, 1.0, 0.0).astype(BF16)


def _head_mean_matrix():
    r = lax.broadcasted_iota(jnp.int32, (256, 256), 0) // HEAD_DIM
    c = lax.broadcasted_iota(jnp.int32, (256, 256), 1) // HEAD_DIM
    return _mask_bf16(r == c)


def _head_mean(x, bd):
    parts = []
    for s in range(x.shape[1] // 256):
        parts.append(_dot_hilo(x[:, s * 256:(s + 1) * 256], bd))
    out = parts[0] if len(parts) == 1 else jnp.concatenate(parts, axis=1)
    return out * (1.0 / HEAD_DIM)


def _a_in(x, gain, w_sh, gather=()):
    S = x.shape[0]
    tm = 512
    nsh, _, wn = w_sh.shape

    def body(x_ref, g_ref, w_ref, uz_ref, h_ref):
        @pl.when(pl.program_id(1) == 0)
        def _():
            xv = x_ref[...]
            r = lax.rsqrt(jnp.mean(xv * xv, axis=-1, keepdims=True) + EPS)
            h_ref[...] = (xv * r * g_ref[...]).astype(BF16)

        uz_ref[...] = _dot(h_ref[...], w_ref[0])

    return _call_with_gather(
        body, name="a_in", grid=(S // tm, nsh),
        in_specs=[pl.BlockSpec((tm, D_MODEL), lambda i, j: (i, 0)),
                  pl.BlockSpec((1, D_MODEL), lambda i, j: (0, 0)),
                  pl.BlockSpec((1, D_MODEL, wn), lambda i, j: (j, 0, 0))],
        out_specs=[pl.BlockSpec((tm, wn), lambda i, j: (i, j)),
                   pl.BlockSpec((tm, D_MODEL), lambda i, j: (i, 0))],
        out_shape=[jax.ShapeDtypeStruct((S, nsh * wn), F32),
                   jax.ShapeDtypeStruct((S, D_MODEL), BF16)],
        args=(x, gain, w_sh), gather=gather)


def _inv_count(first_row, rows, w):
    t1 = first_row + 1 + lax.broadcasted_iota(jnp.int32, (rows, 1), 0)
    return 1.0 / jnp.minimum(t1, w).astype(F32)


def _group_weight(wg_ref, g):
    return jnp.concatenate([wg_ref[sh, g] for sh in range(N_CHIPS)], axis=0)


def _a_mix(uz, wg, scale, gather=()):
    S = uz.shape[0]
    tm = ROW_TILE

    def body(u_ref, up_ref, z_ref, wg_ref, sc_ref, ga_ref, p_ref):
        i = pl.program_id(0)
        row = lax.broadcasted_iota(jnp.int32, (tm, tm), 0)
        col = lax.broadcasted_iota(jnp.int32, (tm, tm), 1)
        d = row - col
        for g, w in enumerate(POOL_WINDOWS):
            cols = slice(g * GROUP_DIM, (g + 1) * GROUP_DIM)
            t_main = _mask_bf16((d >= 0) & (d < w))
            t_halo = _mask_bf16(d + tm < w)
            u = u_ref[:, cols]
            up = jnp.where(i > 0, up_ref[:, cols], 0.0)
            hi, lo = _hilo(u)
            hip, lop = _hilo(up)
            wsum = _dot(t_main, hi) + _dot(t_main, lo) + _dot(t_halo, hip) + _dot(t_halo, lop)
            pooled = (wsum * _inv_count(i * tm, tm, w) - u).astype(BF16)
            p_ref[:, cols] = pooled
            mraw = _dot(pooled, _group_weight(wg_ref, g))
            z = z_ref[:, cols]
            ga_ref[:, cols] = (mraw * sc_ref[:, cols] * (z * _sigmoid(z))).astype(BF16)

    return _call_with_gather(
        body, name="a_mix", grid=(S // tm,),
        in_specs=[pl.BlockSpec((tm, D_MODEL), lambda i: (i, 0)),
                  pl.BlockSpec((tm, D_MODEL), lambda i: (jnp.maximum(i - 1, 0), 0)),
                  pl.BlockSpec((tm, D_MODEL), lambda i: (i, 1)),
                  pl.BlockSpec((N_CHIPS, N_GROUPS, 64, GROUP_DIM), lambda i: (0, 0, 0, 0)),
                  pl.BlockSpec((1, D_MODEL), lambda i: (0, 0))],
        out_specs=[pl.BlockSpec((tm, D_MODEL), lambda i: (i, 0)),
                   pl.BlockSpec((tm, D_MODEL), lambda i: (i, 0))],
        out_shape=[jax.ShapeDtypeStruct((S, D_MODEL), BF16),
                   jax.ShapeDtypeStruct((S, D_MODEL), BF16)],
        args=(uz, uz, uz, wg, scale), gather=gather)


def _out_ple(name, gated, x_in, w_out, p, layer, ple_w, ple_g, target=None, gather=()):
    S = x_in.shape[0]
    tm = ROW_TILE
    with_loss = target is not None

    def body(*refs):
        if with_loss:
            g_ref, x_ref, wo_ref, p_ref, pw_ref, pg_ref, t_ref, xm_ref, dx_ref, e_ref, gt_ref, loss_ref = refs
        else:
            g_ref, x_ref, wo_ref, p_ref, pw_ref, pg_ref, xm_ref, xo_ref, e_ref, gt_ref = refs
        xm = x_ref[...] + _dot(g_ref[...], wo_ref[...])
        xm_ref[...] = xm
        pb = p_ref[...].astype(BF16)
        e = jnp.concatenate([_dot(pb, pw_ref[sh]) for sh in range(N_CHIPS)], axis=1)
        pg = jnp.concatenate([pg_ref[sh] for sh in range(N_CHIPS)], axis=0)
        gate = _sigmoid(_dot(xm.astype(BF16), pg))
        e_ref[...] = e.astype(BF16)
        gt_ref[...] = gate.astype(BF16)
        xo = xm + e * gate
        if with_loss:
            diff = xo - t_ref[...]
            dx_ref[...] = diff * (1.0 / D_MODEL)

            @pl.when(pl.program_id(0) == 0)
            def _():
                loss_ref[...] = jnp.zeros_like(loss_ref)

            loss_ref[...] += jnp.sum(diff * diff) * (0.5 / D_MODEL)
        else:
            xo_ref[...] = xo

    row = pl.BlockSpec((tm, D_MODEL), lambda i: (i, 0))
    in_specs = [row, row,
                pl.BlockSpec((D_MODEL, D_MODEL), lambda i: (0, 0)),
                pl.BlockSpec((None, None, tm, PLE_DIM), lambda i: (layer, 0, i, 0)),
                pl.BlockSpec((N_CHIPS, PLE_DIM, 256), lambda i: (0, 0, 0)),
                pl.BlockSpec((N_CHIPS, 256, D_MODEL), lambda i: (0, 0, 0))]
    args = [gated, x_in, w_out, p, ple_w, ple_g]
    out_specs = [row, row, row, row]
    out_shape = [jax.ShapeDtypeStruct((S, D_MODEL), F32), jax.ShapeDtypeStruct((S, D_MODEL), F32),
                 jax.ShapeDtypeStruct((S, D_MODEL), BF16), jax.ShapeDtypeStruct((S, D_MODEL), BF16)]
    if with_loss:
        in_specs.append(row)
        args.append(target)
        out_specs.append(pl.BlockSpec((8, 128), lambda i: (0, 0)))
        out_shape.append(jax.ShapeDtypeStruct((8, 128), F32))
    return _call_with_gather(body, name=name, grid=(S // tm,), in_specs=in_specs, out_specs=out_specs,
                             out_shape=out_shape, args=args, gather=gather)


def _b_in(x, kv_gain, b_gain, k_gain_t, q_gain_t, w_kv, w_in, gather=()):
    S = x.shape[0]
    tm = ROW_TILE

    def body(x_ref, kvg_ref, bg_ref, kg_ref, qg_ref, wkv_ref, win_ref,
             hkv_ref, hb_ref, kraw_ref, qraw_ref, k_ref, q_ref, v_ref, z_ref):
        xv = x_ref[...]
        y = xv * lax.rsqrt(jnp.mean(xv * xv, axis=-1, keepdims=True) + EPS)
        hkv = (y * kvg_ref[...]).astype(BF16)
        hb = (y * bg_ref[...]).astype(BF16)
        hkv_ref[...] = hkv
        hb_ref[...] = hb
        bd = _head_mean_matrix()

        def head_norm(raw, gain):
            rr = lax.rsqrt(_head_mean(raw * raw, bd) + EPS)
            return raw * rr * gain

        for sh in range(N_CHIPS):
            kvc = _dot(hkv, wkv_ref[sh])
            qzc = _dot(hb, win_ref[sh])
            cols = slice((sh % 2) * 512, (sh % 2) * 512 + 512)
            if sh < 2:
                kraw_ref[:, cols] = kvc.astype(BF16)
                qraw_ref[:, cols] = qzc.astype(BF16)
                k_ref[:, cols] = head_norm(kvc, kg_ref[:, cols]).astype(BF16)
                q_ref[:, cols] = (head_norm(qzc, qg_ref[:, cols]) * SB_SCALE).astype(BF16)
            else:
                v_ref[:, cols] = kvc.astype(BF16)
                z_ref[:, cols] = qzc.astype(BF16)

    row = pl.BlockSpec((tm, D_MODEL), lambda i: (i, 0))
    vec = pl.BlockSpec((1, D_MODEL), lambda i: (0, 0))
    wsp = pl.BlockSpec((N_CHIPS, D_MODEL, 512), lambda i: (0, 0, 0))
    return _call_with_gather(
        body, name="b_in", grid=(S // tm,),
        in_specs=[row, vec, vec, vec, vec, wsp, wsp],
        out_specs=[row] * 8,
        out_shape=[jax.ShapeDtypeStruct((S, D_MODEL), BF16)] * 8,
        args=(x, kv_gain, b_gain, k_gain_t, q_gain_t, w_kv, w_in), gather=gather, vmem_mib=56)


def _softplus_parts(z):
    e = jnp.exp(-jnp.abs(z))
    return -(jnp.maximum(z, 0.0) + jnp.log(1.0 + e)), e


def _add_rows(total, first_row, update):
    if first_row == 0:
        return total + update
    return jnp.concatenate([total[:first_row], total[first_row:] + update], axis=0)


def _attn_fwd(q, k, v, zgate):
    S = q.shape[0]
    tq, tk = ATT_Q_TILE, ATT_K_TILE
    kpq = tq // tk

    def body(q_ref, k_ref, v_ref, z_ref, o_ref, g_ref, lt_ref):
        qi = pl.program_id(1)
        lane = lax.broadcasted_iota(jnp.int32, (1, 128), 1)
        ri = lax.broadcasted_iota(jnp.int32, (tk, tk), 0)
        ci = lax.broadcasted_iota(jnp.int32, (tk, tk), 1)
        later_mat = _mask_bf16(ri > ci)
        t_idx = qi * tq + lax.broadcasted_iota(jnp.int32, (tq, tk), 0)
        s_off = lax.broadcasted_iota(jnp.int32, (tq, tk), 1)
        qv = q_ref[...]
        first = lane < HEAD_DIM
        q_heads = (jnp.where(first, qv, jnp.zeros_like(qv)), jnp.where(first, jnp.zeros_like(qv), qv))

        def step(kj_last, carry, masked):
            chains = [(d, h) for d in range(kpq) for h in range(2)]
            r0 = [(kpq - 1 - d) * tk if masked else 0 for d in range(kpq)]
            s0 = [pl.multiple_of((kj_last - d) * tk, tk) for d in range(kpq)]
            kb = [k_ref[pl.ds(s, tk), :] for s in s0]
            vb = [v_ref[pl.ds(s, tk), :] for s in s0]
            visible = [(s + s_off < t_idx)[r:] for s, r in zip(s0, r0)] if masked else None
            z = {c: _dot_nt(q_heads[c[1]][r0[c[0]]:], kb[c[0]]) for c in chains}
            run = [carry[0], carry[2]]
            log_own, later, run_at = {}, {}, {}
            for c in chains:
                kj, h = c
                lk = _softplus_parts(z[c])[0]
                if masked:
                    lk = jnp.where(visible[kj], lk, 0.0)
                log_own[c] = z[c] + lk
                later[c] = _dot(lk.astype(BF16), later_mat)
                run_at[c] = run[h][r0[kj]:]
                run[h] = _add_rows(run[h], r0[kj], jnp.sum(lk, axis=-1, keepdims=True))
            acc = [carry[1], carry[3]]
            for c in chains:
                kj, h = c
                a = jnp.exp(log_own[c] + later[c] + run_at[c])
                if masked:
                    a = jnp.where(visible[kj], a, 0.0)
                acc[h] = _add_rows(acc[h], r0[kj], _dot(a.astype(BF16), vb[kj]))
            return run[0], acc[0], run[1], acc[1]

        zero1, zero128 = jnp.zeros((tq, 1), F32), jnp.zeros((tq, 128), F32)
        carry = step(qi * kpq + kpq - 1, (zero1, zero128, zero1, zero128), True)
        carry = lax.fori_loop(0, qi, lambda n, c: step((qi - n) * kpq - 1, c, False), carry)
        o_tot = jnp.where(first, carry[1], carry[3])
        l_tot = jnp.where(first, carry[0], carry[2])
        o_ref[...] = o_tot.astype(BF16)
        lt_ref[...] = l_tot
        zz = z_ref[...].astype(F32)
        g_ref[...] = (o_tot * (zz * _sigmoid(zz))).astype(BF16)

    blk = pl.BlockSpec((tq, 128), lambda hp, qi: (qi, hp))
    seq = pl.BlockSpec((S, 128), lambda hp, qi: (0, hp))
    return pl.pallas_call(
        body, name="attn_fwd", grid=(D_MODEL // 128, S // tq),
        in_specs=[blk, seq, seq, blk], out_specs=[blk, blk, blk],
        out_shape=[jax.ShapeDtypeStruct((S, D_MODEL), BF16)] * 2 + [jax.ShapeDtypeStruct((S, D_MODEL), F32)],
        compiler_params=_params(("parallel", "arbitrary")),
    )(q, k, v, zgate)


def _ple_out_bwd(name, dx_out, e, gate, ple_g, w_out):
    S = dx_out.shape[0]
    tm = ROW_TILE

    def body(dx_ref, e_ref, gt_ref, pg_ref, wo_ref, de_ref, dgp_ref, dxm_ref, dg_ref):
        dxo = dx_ref[...]
        ev = e_ref[...].astype(F32)
        gv = gt_ref[...].astype(F32)
        de_ref[...] = (dxo * gv).astype(BF16)
        dgp = (dxo * ev * gv * (1.0 - gv)).astype(BF16)
        dgp_ref[...] = dgp
        pg = jnp.concatenate([pg_ref[sh] for sh in range(N_CHIPS)], axis=0)
        dxm = dxo + _dot_nt(dgp, pg)
        dxm_ref[...] = dxm
        dg_ref[...] = _dot_nt(dxm.astype(BF16), wo_ref[...]).astype(BF16)

    row = pl.BlockSpec((tm, D_MODEL), lambda i: (i, 0))
    return pl.pallas_call(
        body, name=name, grid=(S // tm,),
        in_specs=[row, row, row,
                  pl.BlockSpec((N_CHIPS, 256, D_MODEL), lambda i: (0, 0, 0)),
                  pl.BlockSpec((D_MODEL, D_MODEL), lambda i: (0, 0))],
        out_specs=[row, row, row, row],
        out_shape=[jax.ShapeDtypeStruct((S, D_MODEL), BF16), jax.ShapeDtypeStruct((S, D_MODEL), BF16),
                   jax.ShapeDtypeStruct((S, D_MODEL), F32), jax.ShapeDtypeStruct((S, D_MODEL), BF16)],
        compiler_params=_params(("arbitrary",)),
    )(dx_out, e, gate, ple_g, w_out)


def _attn_bwd(q, k, v, ltot, dgated, o, zgate):
    S = q.shape[0]
    tq, tk = ATT_Q_TILE, ATT_K_TILE
    kpq = tq // tk
    nq = S // tq

    def body(q_ref, k_ref, v_ref, lt_ref, dg_ref, o_ref, z_ref, dq_ref, dk_ref, dv_ref, dz_ref, dk_acc, dv_acc):
        qi = pl.program_id(1)

        @pl.when(qi == 0)
        def _():
            dk_acc[...] = jnp.zeros_like(dk_acc)
            dv_acc[...] = jnp.zeros_like(dv_acc)

        lane = lax.broadcasted_iota(jnp.int32, (1, 128), 1)
        ri = lax.broadcasted_iota(jnp.int32, (tk, tk), 0)
        ci = lax.broadcasted_iota(jnp.int32, (tk, tk), 1)
        later_mat = _mask_bf16(ri > ci)
        before_mat = _mask_bf16(ri < ci)
        t_idx = qi * tq + lax.broadcasted_iota(jnp.int32, (tq, tk), 0)
        s_off = lax.broadcasted_iota(jnp.int32, (tq, tk), 1)
        zz = z_ref[...].astype(F32)
        sg = _sigmoid(zz)
        dgv = dg_ref[...].astype(F32)
        dz_ref[...] = (dgv * o_ref[...].astype(F32) * _dsilu(zz, sg)).astype(BF16)
        dob = (dgv * (zz * sg)).astype(BF16)
        ltv = lt_ref[...]
        qv = q_ref[...]
        first = lane < HEAD_DIM
        masks = (first, jnp.logical_not(first))
        q_heads = [jnp.where(hm, qv, jnp.zeros_like(qv)) for hm in masks]
        do_heads = [jnp.where(hm, dob, jnp.zeros_like(dob)) for hm in masks]
        totals = [jnp.max(jnp.where(hm, ltv, -jnp.inf), axis=-1, keepdims=True) for hm in masks]

        def step(kj_first, carry, masked):
            chains = [(d, h) for d in range(kpq) for h in range(2)]
            r0 = [d * tk if masked else 0 for d in range(kpq)]
            s0 = [pl.multiple_of((kj_first + d) * tk, tk) for d in range(kpq)]
            kb = [k_ref[pl.ds(s, tk), :] for s in s0]
            vb = [v_ref[pl.ds(s, tk), :] for s in s0]
            visible = [(s + s_off < t_idx)[r:] for s, r in zip(s0, r0)] if masked else None
            z = {c: _dot_nt(q_heads[c[1]][r0[c[0]]:], kb[c[0]]) for c in chains}
            da = {c: _dot_nt(do_heads[c[1]][r0[c[0]]:], vb[c[0]]) for c in chains}
            run = [carry[0], carry[3]]
            log_own, beta, later, base = {}, {}, {}, {}
            for c in chains:
                kj, h = c
                lk = _softplus_parts(z[c])[0]
                if masked:
                    lk = jnp.where(visible[kj], lk, 0.0)
                log_own[c] = z[c] + lk
                beta[c] = jnp.exp(log_own[c]).astype(BF16)
                later[c] = _dot(lk.astype(BF16), later_mat)
                run[h] = _add_rows(run[h], r0[kj], jnp.sum(lk, axis=-1, keepdims=True))
                base[c] = (totals[h] - run[h])[r0[kj]:]
            grun = [carry[1], carry[4]]
            a_bf, g_bf, gbefore, grun_at = {}, {}, {}, {}
            for c in chains:
                kj, h = c
                a = jnp.exp(log_own[c] + later[c] + base[c])
                if masked:
                    a = jnp.where(visible[kj], a, 0.0)
                a_bf[c] = a.astype(BF16)
                g = da[c] * a
                g_bf[c] = g.astype(BF16)
                gbefore[c] = _dot(g_bf[c], before_mat)
                grun_at[c] = grun[h][r0[kj]:]
                grun[h] = _add_rows(grun[h], r0[kj], jnp.sum(g, axis=-1, keepdims=True))
            dq = [carry[2], carry[5]]
            dk_blk = [jnp.zeros((tk, 128), F32) for _ in range(kpq)]
            dv_blk = [jnp.zeros((tk, 128), F32) for _ in range(kpq)]
            for c in chains:
                kj, h = c
                g = g_bf[c].astype(F32)
                dz = g - beta[c].astype(F32) * (g + gbefore[c] + grun_at[c])
                if masked:
                    dz = jnp.where(visible[kj], dz, 0.0)
                dzb = dz.astype(BF16)
                dq[h] = _add_rows(dq[h], r0[kj], _dot(dzb, kb[kj]))
                dk_blk[kj] = dk_blk[kj] + _dot_tn(dzb, q_heads[h][r0[kj]:])
                dv_blk[kj] = dv_blk[kj] + _dot_tn(a_bf[c], do_heads[h][r0[kj]:])
            for d in range(kpq):
                dk_acc[pl.ds(s0[d], tk), :] += dk_blk[d]
                dv_acc[pl.ds(s0[d], tk), :] += dv_blk[d]
            return run[0], grun[0], dq[0], run[1], grun[1], dq[1]

        zero1, zero128 = jnp.zeros((tq, 1), F32), jnp.zeros((tq, 128), F32)
        carry = lax.fori_loop(0, qi, lambda n, c: step(n * kpq, c, False),
                              (zero1, zero1, zero128, zero1, zero1, zero128))
        carry = step(qi * kpq, carry, True)
        dq_ref[...] = jnp.where(first, carry[2], carry[5]).astype(BF16)

        @pl.when(qi == nq - 1)
        def _():
            dk_ref[...] = dk_acc[...].astype(BF16)
            dv_ref[...] = dv_acc[...].astype(BF16)

    blk = pl.BlockSpec((tq, 128), lambda hp, qi: (qi, hp))
    seq = pl.BlockSpec((S, 128), lambda hp, qi: (0, hp))
    return pl.pallas_call(
        body, name="attn_bwd", grid=(D_MODEL // 128, nq),
        in_specs=[blk, seq, seq, blk, blk, blk, blk], out_specs=[blk, seq, seq, blk],
        out_shape=[jax.ShapeDtypeStruct((S, D_MODEL), BF16)] * 4,
        scratch_shapes=[pltpu.VMEM((S, 128), F32), pltpu.VMEM((S, 128), F32)],
        compiler_params=_params(("parallel", "arbitrary")),
    )(q, k, v, ltot, dgated, o, zgate)


def _rms_bwd(xv, dh_gain_sum):
    r = lax.rsqrt(jnp.mean(xv * xv, axis=-1, keepdims=True) + EPS)
    xhat = xv * r
    dx = r * (dh_gain_sum - xhat * jnp.mean(dh_gain_sum * xhat, axis=-1, keepdims=True))
    return dx, xhat


def _b_in_bwd(dq, dk, dv, dz, q_raw, k_raw, x, dx_mid, q_gain_t, k_gain_t, b_gain, kv_gain, w_in, w_kv):
    S = x.shape[0]
    tm = ROW_TILE

    def body(dq_ref, dk_ref, dv_ref, dz_ref, qr_ref, kr_ref, x_ref, dxm_ref, qg_ref, kg_ref, bg_ref, kvg_ref,
             win_ref, wkv_ref, dqz_ref, dkv_ref, dx_ref, small_ref):
        @pl.when(pl.program_id(0) == 0)
        def _():
            small_ref[...] = jnp.zeros_like(small_ref)

        bd = _head_mean_matrix()

        def head_norm_bwd(dy_ref, raw_ref, gain, scale):
            raw = raw_ref[...].astype(F32)
            rr = lax.rsqrt(_head_mean(raw * raw, bd) + EPS)
            xhat = raw * rr
            dy = dy_ref[...].astype(F32) * scale
            gdy = dy * gain
            draw = rr * (gdy - xhat * _head_mean(gdy * xhat, bd))
            return draw.astype(BF16), jnp.sum(dy * xhat, axis=0, keepdims=True)

        dqr, dqg = head_norm_bwd(dq_ref, qr_ref, qg_ref[...], SB_SCALE)
        dkr, dkg = head_norm_bwd(dk_ref, kr_ref, kg_ref[...], 1.0)
        dqz_ref[:, :D_MODEL] = dqr
        dqz_ref[:, D_MODEL:] = dz_ref[...]
        dkv_ref[:, :D_MODEL] = dkr
        dkv_ref[:, D_MODEL:] = dv_ref[...]
        dhb = jnp.zeros((tm, D_MODEL), F32)
        dhkv = jnp.zeros((tm, D_MODEL), F32)
        for sh in range(N_CHIPS):
            cols = slice(sh * 512, (sh + 1) * 512)
            dhb = dhb + _dot_nt(dqz_ref[:, cols], win_ref[sh])
            dhkv = dhkv + _dot_nt(dkv_ref[:, cols], wkv_ref[sh])
        dx, xhat = _rms_bwd(x_ref[...], dhb * bg_ref[...] + dhkv * kvg_ref[...])
        dx_ref[...] = dxm_ref[...] + dx
        small_ref[0:1, :] += dqg
        small_ref[1:2, :] += dkg
        small_ref[2:3, :] += jnp.sum(dhb * xhat, axis=0, keepdims=True)
        small_ref[3:4, :] += jnp.sum(dhkv * xhat, axis=0, keepdims=True)

    row = pl.BlockSpec((tm, D_MODEL), lambda i: (i, 0))
    wide = pl.BlockSpec((tm, 2 * D_MODEL), lambda i: (i, 0))
    vec = pl.BlockSpec((1, D_MODEL), lambda i: (0, 0))
    wsp = pl.BlockSpec((N_CHIPS, D_MODEL, 512), lambda i: (0, 0, 0))
    return pl.pallas_call(
        body, name="b_in_bwd", grid=(S // tm,),
        in_specs=[row] * 8 + [vec] * 4 + [wsp, wsp],
        out_specs=[wide, wide, row, pl.BlockSpec((8, D_MODEL), lambda i: (0, 0))],
        out_shape=[jax.ShapeDtypeStruct((S, 2 * D_MODEL), BF16), jax.ShapeDtypeStruct((S, 2 * D_MODEL), BF16),
                   jax.ShapeDtypeStruct((S, D_MODEL), F32), jax.ShapeDtypeStruct((8, D_MODEL), F32)],
        compiler_params=_params(("arbitrary",), 56),
    )(dq, dk, dv, dz, q_raw, k_raw, x, dx_mid, q_gain_t, k_gain_t, b_gain, kv_gain, w_in, w_kv)


def _a_mix_bwd(dgated, uz, pooled, wg, scale, w_in, x, dx_mid, gain):
    S = x.shape[0]
    tm = ROW_TILE
    n = S // tm

    def body(dg_ref, z_ref, p_ref, wg_ref, sc_ref, win_ref, x_ref, dxm_ref, gn_ref,
             duz_ref, dmr_ref, dx_ref, small_ref, halo_hi, halo_lo):
        i = pl.program_id(0)

        @pl.when(i == 0)
        def _():
            small_ref[...] = jnp.zeros_like(small_ref)
            halo_hi[...] = jnp.zeros_like(halo_hi)
            halo_lo[...] = jnp.zeros_like(halo_lo)

        first_row = (n - 1 - i) * tm
        row = lax.broadcasted_iota(jnp.int32, (tm, tm), 0)
        col = lax.broadcasted_iota(jnp.int32, (tm, tm), 1)
        d = col - row
        for g, w in enumerate(POOL_WINDOWS):
            cols = slice(g * GROUP_DIM, (g + 1) * GROUP_DIM)
            wgg = _group_weight(wg_ref, g)
            sc = sc_ref[:, cols]
            mraw = _dot(p_ref[:, cols], wgg)
            z = z_ref[:, cols]
            sg = _sigmoid(z)
            dga = dg_ref[:, cols].astype(F32)
            dm = dga * (z * sg)
            duz_ref[:, D_MODEL + g * GROUP_DIM:D_MODEL + (g + 1) * GROUP_DIM] = (
                dga * (mraw * sc) * _dsilu(z, sg)).astype(BF16)
            small_ref[0:1, cols] += jnp.sum(dm * mraw, axis=0, keepdims=True)
            dmr = (dm * sc).astype(BF16)
            dmr_ref[:, cols] = dmr
            dp = _dot_nt(dmr, wgg)
            hi, lo = _hilo(dp * _inv_count(first_row, tm, w))
            t_main = _mask_bf16((d >= 0) & (d < w))
            t_halo = _mask_bf16(d + tm < w)
            du = (_dot(t_main, hi) + _dot(t_main, lo) + _dot(t_halo, halo_hi[:, cols]) + _dot(t_halo, halo_lo[:, cols])
                  - dp)
            halo_hi[:, cols] = hi
            halo_lo[:, cols] = lo
            duz_ref[:, cols] = du.astype(BF16)
        dh = jnp.zeros((tm, D_MODEL), F32)
        for sh in range(N_CHIPS):
            dh = dh + _dot_nt(duz_ref[:, sh * 512:(sh + 1) * 512], win_ref[sh])
        dx, xhat = _rms_bwd(x_ref[...], dh * gn_ref[...])
        dx_ref[...] = dxm_ref[...] + dx
        small_ref[1:2, :] += jnp.sum(dh * xhat, axis=0, keepdims=True)

    rev = lambda i: (n - 1 - i, 0)
    row = pl.BlockSpec((tm, D_MODEL), rev)
    vec = pl.BlockSpec((1, D_MODEL), lambda i: (0, 0))
    return pl.pallas_call(
        body, name="a_mix_bwd", grid=(n,),
        in_specs=[row,
                  pl.BlockSpec((tm, D_MODEL), lambda i: (n - 1 - i, 1)),
                  row,
                  pl.BlockSpec((N_CHIPS, N_GROUPS, 64, GROUP_DIM), lambda i: (0, 0, 0, 0)),
                  vec,
                  pl.BlockSpec((N_CHIPS, D_MODEL, 512), lambda i: (0, 0, 0)),
                  row, row, vec],
        out_specs=[pl.BlockSpec((tm, 2 * D_MODEL), rev), row, row,
                   pl.BlockSpec((8, D_MODEL), lambda i: (0, 0))],
        out_shape=[jax.ShapeDtypeStruct((S, 2 * D_MODEL), BF16), jax.ShapeDtypeStruct((S, D_MODEL), BF16),
                   jax.ShapeDtypeStruct((S, D_MODEL), F32), jax.ShapeDtypeStruct((8, D_MODEL), F32)],
        scratch_shapes=[pltpu.VMEM((tm, D_MODEL), BF16), pltpu.VMEM((tm, D_MODEL), BF16)],
        compiler_params=_params(("arbitrary",)),
    )(dgated, uz, pooled, wg, scale, w_in, x, dx_mid, gain)


def _wgrad(name, a, dy, n_shards, a_spec=None, k_dim=None):
    S, n_cols = dy.shape
    ts = WGRAD_SEQ_TILE
    k_dim = a.shape[-1] if k_dim is None else k_dim
    wn = n_cols // n_shards
    tk = min(k_dim, WGRAD_ACC_BYTES // (4 * n_cols))
    nst = S // ts

    def body(a_ref, dy_ref, out_ref, acc):
        st = pl.program_id(1)

        @pl.when(st == 0)
        def _():
            acc[...] = jnp.zeros_like(acc)

        acc[...] += _dot_tn(a_ref[...].astype(BF16), dy_ref[...].astype(BF16))

        @pl.when(st == nst - 1)
        def _():
            for sh in range(n_shards):
                out_ref[sh] = acc[:, sh * wn:(sh + 1) * wn]

    if a_spec is None:
        a_spec = pl.BlockSpec((ts, tk), lambda kt, st: (st, kt))
    return pl.pallas_call(
        body, name=name, grid=(k_dim // tk, nst),
        in_specs=[a_spec, pl.BlockSpec((ts, n_cols), lambda kt, st: (st, 0))],
        out_specs=pl.BlockSpec((n_shards, tk, wn), lambda kt, st: (0, kt, 0)),
        out_shape=jax.ShapeDtypeStruct((n_shards, k_dim, wn), F32),
        scratch_shapes=[pltpu.VMEM((tk, n_cols), F32)],
        compiler_params=_params(("parallel", "arbitrary")),
    )(a, dy)


def _wgrad_ple(name, p, layer, de):
    ts = WGRAD_SEQ_TILE
    spec = pl.BlockSpec((None, None, ts, PLE_DIM), lambda kt, st: (layer, 0, st, 0))
    return _wgrad(name, p, de, N_CHIPS, a_spec=spec, k_dim=PLE_DIM)


def _wgrad_group(pooled, dmr):
    S = pooled.shape[0]
    ts = WGRAD_SEQ_TILE
    nst = S // ts

    def body(p_ref, d_ref, out_ref, acc):
        st = pl.program_id(1)

        @pl.when(st == 0)
        def _():
            acc[...] = jnp.zeros_like(acc)

        acc[...] += _dot_tn(p_ref[...], d_ref[...])

        @pl.when(st == nst - 1)
        def _():
            for sh in range(N_CHIPS):
                out_ref[sh] = acc[sh * 64:(sh + 1) * 64, :]

    blk = pl.BlockSpec((ts, GROUP_DIM), lambda g, st: (st, g))
    return pl.pallas_call(
        body, name="wgrad_group", grid=(N_GROUPS, nst),
        in_specs=[blk, blk],
        out_specs=pl.BlockSpec((N_CHIPS, None, 64, GROUP_DIM), lambda g, st: (0, g, 0, 0)),
        out_shape=jax.ShapeDtypeStruct((N_CHIPS, N_GROUPS, 64, GROUP_DIM), F32),
        scratch_shapes=[pltpu.VMEM((GROUP_DIM, GROUP_DIM), F32)],
        compiler_params=_params(("parallel", "arbitrary")),
    )(pooled, dmr)


GATHER_AT = {
    "a_in": ("a_w_group", "a_w_out", "ple_w0", "ple_gate_w0"),
    "a_mix": ("w_kv",),
    "a_out_ple": ("b_w_in",),
    "b_in": ("b_w_out", "ple_w1", "ple_gate_w1"),
}


def _local_step(x, p, target, w, local=None):
    w = dict(w)

    def run(fn, host, n_out, *args, **kwargs):
        names = GATHER_AT[host] if local is not None else ()
        res = fn(*args, gather=[local[n] for n in names], **kwargs)
        w.update(zip(names, res[n_out:]))
        return res[:n_out]

    k_gain_t = jnp.tile(w["k_norm"].reshape(1, HEAD_DIM), (1, N_HEADS))
    q_gain_t = jnp.tile(w["b_q_norm"].reshape(1, HEAD_DIM), (1, N_HEADS))

    uz, h_a = run(_a_in, "a_in", 2, x, w["a_norm"], w["a_w_in"])
    wg4 = w["a_w_group"].reshape(N_CHIPS, N_GROUPS, 64, GROUP_DIM)
    wa_out = w["a_w_out"].reshape(D_MODEL, D_MODEL)
    gated_a, pooled = run(_a_mix, "a_mix", 2, uz, wg4, w["a_scale"])
    x1, x2, e_a, gate_a = run(_out_ple, "a_out_ple", 4, "a_out_ple", gated_a, x, wa_out, p, 0,
                              w["ple_w0"], w["ple_gate_w0"])
    h_kv, h_b, k_raw, q_raw, k, q, v, z_b = run(
        _b_in, "b_in", 8, x2, w["kv_norm"], w["b_norm"], k_gain_t, q_gain_t, w["w_kv"], w["b_w_in"])
    wb_out = w["b_w_out"].reshape(D_MODEL, D_MODEL)
    o, gated_b, ltot = _attn_fwd(q, k, v, z_b)
    x3, dx4, e_b, gate_b, loss_blk = _out_ple("b_out_ple", gated_b, x2, wb_out, p, 1, w["ple_w1"], w["ple_gate_w1"],
                                              target=target)

    de_b, dgp_b, dx3, dgated_b = _ple_out_bwd("b_ple_out_bwd", dx4, e_b, gate_b, w["ple_gate_w1"], wb_out)
    dq, dk, dv, dz_b = _attn_bwd(q, k, v, ltot, dgated_b, o, z_b)
    dqz, dkv, dx2, small_b = _b_in_bwd(dq, dk, dv, dz_b, q_raw, k_raw, x2, dx3, q_gain_t, k_gain_t,
                                       w["b_norm"], w["kv_norm"], w["b_w_in"], w["w_kv"])
    de_a, dgp_a, dx1, dgated_a = _ple_out_bwd("a_ple_out_bwd", dx2, e_a, gate_a, w["ple_gate_w0"], wa_out)
    duz, dmr, grad_x, small_a = _a_mix_bwd(dgated_a, uz, pooled, wg4, w["a_scale"], w["a_w_in"], x, dx1, w["a_norm"])

    grads = {
        "a_w_in": _wgrad("wgrad_a_in", h_a, duz, N_CHIPS),
        "a_w_group": _wgrad_group(pooled, dmr).reshape(N_CHIPS, N_GROUPS * 64, GROUP_DIM),
        "a_w_out": _wgrad("wgrad_a_out", gated_a, dx1, 1).reshape(N_CHIPS, 256, D_MODEL),
        "w_kv": _wgrad("wgrad_kv", h_kv, dkv, N_CHIPS),
        "b_w_in": _wgrad("wgrad_b_in", h_b, dqz, N_CHIPS),
        "b_w_out": _wgrad("wgrad_b_out", gated_b, dx3, 1).reshape(N_CHIPS, 256, D_MODEL),
        "ple_w": jnp.concatenate([_wgrad_ple("wgrad_ple0", p, 0, de_a), _wgrad_ple("wgrad_ple1", p, 1, de_b)], axis=1),
        "ple_gate_w": jnp.concatenate(
            [_wgrad("wgrad_gate0", x1, dgp_a, 1).reshape(N_CHIPS, 256, D_MODEL),
             _wgrad("wgrad_gate1", x3, dgp_b, 1).reshape(N_CHIPS, 256, D_MODEL)], axis=1),
    }
    fold = lambda row: jnp.pad(row.reshape(N_HEADS, HEAD_DIM).sum(axis=0), (0, D_MODEL - HEAD_DIM))
    small = jnp.stack([small_a[1], small_a[0], small_b[3], small_b[2], fold(small_b[1]), fold(small_b[0]),
                       jnp.pad(loss_blk[0], (0, D_MODEL - loss_blk.shape[1])), jnp.zeros((D_MODEL,), F32)])
    return grad_x, grads, small


def _mesh_place():
    x, y, c = lax.axis_index("x"), lax.axis_index("y"), lax.axis_index("c")
    other_chips = [(1 - x, y), (x, 1 - y), (1 - x, 1 - y)]
    return x, y, c, other_chips


def _gather_sems(n):
    return [pltpu.SemaphoreType.DMA((3 * n,)), pltpu.SemaphoreType.DMA((3 * n,)),
            pltpu.SemaphoreType.DMA((3 * n,)), pltpu.SemaphoreType.DMA((3 * n,)), pltpu.SemaphoreType.DMA((n,))]


def _gather_copies(srcs, outs, sems):
    send_far, recv_far, send_sib, recv_sib, local_sem = sems
    n = len(srcs)
    x, y, c, chips = _mesh_place()
    me = 2 * x + y
    sibling = (x, y, 1 - c)

    def half(k, which):
        rows = srcs[k].shape[0] // 2
        return pl.ds(pl.multiple_of(which * rows, 16), rows)

    local = [pltpu.make_async_copy(srcs[k], outs[k].at[me], local_sem.at[k]) for k in range(n)]
    far = [pltpu.make_async_remote_copy(
        src_ref=srcs[k].at[half(k, c)], dst_ref=outs[k].at[me, half(k, c)],
        send_sem=send_far.at[j * n + k], recv_sem=recv_far.at[j * n + k], device_id=(px, py, c), device_id_type=MESH)
        for j, (px, py) in enumerate(chips) for k in range(n)]

    def landed(j, k, which, from_far):
        px, py = chips[j]
        piece = outs[k].at[2 * px + py, half(k, which)]
        send, recv = (send_far, recv_far) if from_far else (send_sib, recv_sib)
        return pltpu.make_async_remote_copy(src_ref=piece, dst_ref=piece, send_sem=send.at[j * n + k],
                                            recv_sem=recv.at[j * n + k], device_id=sibling, device_id_type=MESH)

    return local, far, landed, c


def _gather_start(srcs, outs, sems):
    local, far, _, _ = _gather_copies(srcs, outs, sems)
    for cp in local + far:
        cp.start()


def _gather_finish(srcs, outs, sems):
    local, far, landed, c = _gather_copies(srcs, outs, sems)
    pairs = [(j, k) for j in range(3) for k in range(len(srcs))]
    passed = []
    for j, k in pairs:
        landed(j, k, c, True).wait_recv()
        passed.append(landed(j, k, c, False))
        passed[-1].start()
    for j, k in pairs:
        landed(j, k, 1 - c, False).wait_recv()
    for cp in far + passed:
        cp.wait_send()
    for cp in local:
        cp.wait()


def _call_with_gather(body, *, name, grid, in_specs, out_specs, out_shape, args, gather, scratch_shapes=(),
                      vmem_mib=48):
    n_in, n_out, n_scr, n_g = len(args), len(out_shape), len(scratch_shapes), len(gather)
    n_steps = 1
    for g in grid:
        n_steps *= g

    def wrapped(*refs):
        ins, g_in = refs[:n_in], refs[n_in:n_in + n_g]
        outs = refs[n_in + n_g:n_in + n_g + n_out]
        g_out = refs[n_in + n_g + n_out:n_in + 2 * n_g + n_out]
        scratch = refs[n_in + 2 * n_g + n_out:n_in + 2 * n_g + n_out + n_scr]
        sems = refs[n_in + 2 * n_g + n_out + n_scr:]
        step = 0
        for axis, g in enumerate(grid):
            step = step * g + pl.program_id(axis)
        if n_g:
            @pl.when(step == 0)
            def _():
                _gather_start(g_in, g_out, sems)

        body(*ins, *outs, *scratch)
        if n_g:
            @pl.when(step == n_steps - 1)
            def _():
                _gather_finish(g_in, g_out, sems)

    hbm = pl.BlockSpec(memory_space=pltpu.HBM)
    return pl.pallas_call(
        wrapped, name=name, grid=grid,
        in_specs=list(in_specs) + [hbm] * n_g, out_specs=list(out_specs) + [hbm] * n_g,
        out_shape=list(out_shape) + [jax.ShapeDtypeStruct((N_CHIPS,) + g.shape, BF16) for g in gather],
        scratch_shapes=list(scratch_shapes) + (_gather_sems(n_g) if n_g else []),
        compiler_params=_params(("arbitrary",) * len(grid), vmem_mib),
    )(*args, *gather)


def _allgather_weights(shards, small, casts):
    n = len(shards)
    cast_out = [(k, r0, r1) for k, (_, ranges) in enumerate(casts) for r0, r1 in ranges]
    n_c, n_co = len(casts), len(cast_out)

    def body(*refs):
        ins, small_in, cast_in = refs[:n], refs[n], refs[n + 1:n + 1 + n_c]
        refs = refs[n + 1 + n_c:]
        outs, small_out, cast_dst = refs[:n], refs[n], refs[n + 1:n + 1 + n_co]
        refs = refs[n + 1 + n_co:]
        cast, cast_buf = refs[:n], refs[n:n + n_co]
        send_far, recv_far, send_sib, recv_sib, send_small, recv_small, local_sem, cast_sem = refs[n + n_co:]
        x, y, c, chips = _mesh_place()
        me = 2 * x + y
        sibling = (x, y, 1 - c)

        def half(k, which):
            rows = ins[k].shape[0] // 2
            return pl.ds(pl.multiple_of(which * rows, 16), rows)

        local = []
        for k in range(n):
            cast[k][...] = ins[k][...].astype(BF16)
            local.append(pltpu.make_async_copy(cast[k], outs[k].at[me], local_sem.at[k]))
            local[-1].start()
        local.append(pltpu.make_async_copy(small_in, small_out.at[me], local_sem.at[n]))
        local[-1].start()

        sends = []
        for j, (px, py) in enumerate(chips):
            for k in range(n):
                cp = pltpu.make_async_remote_copy(
                    src_ref=cast[k].at[half(k, c)], dst_ref=outs[k].at[me, half(k, c)],
                    send_sem=send_far.at[j * n + k], recv_sem=recv_far.at[j * n + k],
                    device_id=(px, py, c), device_id_type=MESH)
                cp.start()
                sends.append(cp)
            cp = pltpu.make_async_remote_copy(
                src_ref=small_in, dst_ref=small_out.at[me], send_sem=send_small.at[j], recv_sem=recv_small.at[j],
                device_id=(px, py, c), device_id_type=MESH)
            cp.start()
            sends.append(cp)

        for i, (k, r0, r1) in enumerate(cast_out):
            cast_buf[i][...] = cast_in[k][r0:r1, :].astype(BF16)
            local.append(pltpu.make_async_copy(cast_buf[i], cast_dst[i], cast_sem.at[i]))
            local[-1].start()

        def landed(j, k, which, sems_s, sems_r, device):
            px, py = chips[j]
            piece = outs[k].at[2 * px + py, half(k, which)]
            return pltpu.make_async_remote_copy(
                src_ref=piece, dst_ref=piece, send_sem=sems_s.at[j * n + k], recv_sem=sems_r.at[j * n + k],
                device_id=device, device_id_type=MESH)

        for j in range(len(chips)):
            for k in range(n):
                landed(j, k, c, send_far, recv_far, sibling).wait_recv()
                cp = landed(j, k, c, send_sib, recv_sib, sibling)
                cp.start()
                sends.append(cp)
        for j, (px, py) in enumerate(chips):
            for k in range(n):
                landed(j, k, 1 - c, send_sib, recv_sib, sibling).wait_recv()
            pltpu.make_async_remote_copy(
                src_ref=small_in, dst_ref=small_out.at[2 * px + py], send_sem=send_small.at[j],
                recv_sem=recv_small.at[j], device_id=(px, py, c), device_id_type=MESH).wait_recv()
        for cp in sends:
            cp.wait_send()
        for cp in local:
            cp.wait()

    vmem = pl.BlockSpec(memory_space=pltpu.VMEM)
    hbm = pl.BlockSpec(memory_space=pltpu.HBM)
    cast_shapes = [(r1 - r0, casts[k][0].shape[1]) for k, r0, r1 in cast_out]
    res = pl.pallas_call(
        body, name="allgather_weights",
        in_specs=[vmem] * (n + 1 + n_c), out_specs=[hbm] * (n + 1 + n_co),
        out_shape=[jax.ShapeDtypeStruct((N_CHIPS,) + s.shape, BF16) for s in shards]
        + [jax.ShapeDtypeStruct((N_CHIPS,) + small.shape, F32)]
        + [jax.ShapeDtypeStruct(s, BF16) for s in cast_shapes],
        scratch_shapes=[pltpu.VMEM(s.shape, BF16) for s in shards] + [pltpu.VMEM(s, BF16) for s in cast_shapes]
        + [pltpu.SemaphoreType.DMA((3 * n,)), pltpu.SemaphoreType.DMA((3 * n,)),
           pltpu.SemaphoreType.DMA((3 * n,)), pltpu.SemaphoreType.DMA((3 * n,)),
           pltpu.SemaphoreType.DMA((3,)), pltpu.SemaphoreType.DMA((3,)),
           pltpu.SemaphoreType.DMA((n + 1,)), pltpu.SemaphoreType.DMA((n_co,))],
        compiler_params=_params(None, 40),
    )(*shards, small, *[a for a, _ in casts])
    return res[:n], res[n], res[n + 1:]


def _adamw(w, g, m, v):
    m = ADAM_B1 * m + (1.0 - ADAM_B1) * g
    v = ADAM_B2 * v + (1.0 - ADAM_B2) * (g * g)
    m_hat = m / (1.0 - ADAM_B1 ** ADAM_STEP)
    v_hat = v / (1.0 - ADAM_B2 ** ADAM_STEP)
    delta = -ADAM_LR * (m_hat / (jnp.sqrt(v_hat) + ADAM_EPS) + ADAM_WD * w)
    return delta, m, v


RS_PIECE_ROWS = 128
RS_PIECE_COLS = 512


def _reduce_adam_all(grads, ws, ms, vs):
    n_w = len(grads)
    P, C = RS_PIECE_ROWS, RS_PIECE_COLS
    pieces = []
    for k, g in enumerate(grads):
        hr, cols = g.shape[1] // 2, g.shape[2]
        pr, pc = min(hr, P), min(cols, C)
        pieces += [(k, ro, hr, co, pr, pc) for ro in range(0, hr, pr) for co in range(0, cols, pc)]
    n = len(pieces)

    def body(*refs):
        g_in, w_in, m_in, v_in = (refs[i * n_w:(i + 1) * n_w] for i in range(4))
        g_out, d_out, m_out, v_out = (refs[(4 + i) * n_w:(5 + i) * n_w] for i in range(4))
        (gm, go, sb1, rb1, part, sb2, rb2, fin, wmv, outs,
         ld_sem, wmv_sem, s1_send, s1_recv, s2_send, s2_recv, s3_send, s3_recv, out_sem) = refs[8 * n_w:]
        x, y, c, chips = _mesh_place()
        me = 2 * x + y
        sibling = (x, y, 1 - c)

        def at_hbm(i, which):
            _, ro, hr, co, pr, pc = pieces[i]
            half = c if which == 0 else 1 - c
            return pl.ds(pl.multiple_of(half * hr + ro, 64), pr), pl.ds(co, pc)

        def win(i):
            return pl.ds(0, pieces[i][4]), pl.ds(0, pieces[i][5])

        every = slice(None)

        def loads(i):
            k, s = pieces[i][0], i % 3
            return [pltpu.make_async_copy(g_in[k].at[(every,) + at_hbm(i, h)], buf.at[(s, every) + win(i)], ld_sem.at[s, h])
                    for h, buf in enumerate((gm, go))]

        def wmv_loads(i):
            k, s = pieces[i][0], i % 2
            return [pltpu.make_async_copy(src[k].at[at_hbm(i, h)], wmv.at[(s, a, h) + win(i)], wmv_sem.at[s, a, h])
                    for a, src in enumerate((w_in, m_in, v_in)) for h in range(2)]

        def stores(i):
            k, s = pieces[i][0], i % 2
            return [pltpu.make_async_copy(outs.at[(s, a, h) + win(i)], dst[k].at[at_hbm(i, h)], out_sem.at[s, a, h])
                    for a, dst in enumerate((g_out, d_out, m_out, v_out)) for h in range(2)]

        def swap1(i):
            s = i % 2
            return pltpu.make_async_remote_copy(
                src_ref=sb1.at[(s, every) + win(i)], dst_ref=rb1.at[(s, every) + win(i)],
                send_sem=s1_send.at[s], recv_sem=s1_recv.at[s], device_id=sibling, device_id_type=MESH)

        def far2(i, j):
            s = i % 2
            px, py = chips[j]
            return pltpu.make_async_remote_copy(
                src_ref=sb2.at[(s, j) + win(i)], dst_ref=rb2.at[(s, j) + win(i)],
                send_sem=s2_send.at[s, j], recv_sem=s2_recv.at[s, j], device_id=(px, py, c), device_id_type=MESH)

        def swap3(i):
            s = i % 2
            return pltpu.make_async_remote_copy(
                src_ref=fin.at[(s, 0) + win(i)], dst_ref=fin.at[(s, 1) + win(i)],
                send_sem=s3_send.at[s], recv_sem=s3_recv.at[s], device_id=sibling, device_id_type=MESH)

        def stage0(i):
            for cp in loads(i):
                cp.start()

        def stage1(i):
            s, s3 = i % 2, i % 3
            for cp in loads(i):
                cp.wait()
            sb1[(s, every) + win(i)] = go[(s3, every) + win(i)].astype(BF16)
            swap1(i).start()

        def stage2(i):
            s, s3 = i % 2, i % 3
            swap1(i).wait()
            part[(s, every) + win(i)] = gm[(s3, every) + win(i)] + rb1[(s, every) + win(i)].astype(F32)
            for j, (px, py) in enumerate(chips):
                sb2[(s, j) + win(i)] = part[(s, 2 * px + py) + win(i)].astype(BF16)
                far2(i, j).start()

        def stage3(i):
            s = i % 2
            total = part[(s, me) + win(i)]
            for j in range(3):
                far2(i, j).wait()
                total = total + rb2[(s, j) + win(i)].astype(F32)
            fin[(s, 0) + win(i)] = total
            swap3(i).start()
            for cp in wmv_loads(i):
                cp.start()

        def stage4(i):
            s = i % 2
            if i >= 2:
                for cp in stores(i - 2):
                    cp.wait()
            swap3(i).wait()
            for cp in wmv_loads(i):
                cp.wait()
            both = (every,) + win(i)
            g = fin[(s,) + both]
            delta, m_new, v_new = _adamw(wmv[(s, 0) + both], g, wmv[(s, 1) + both], wmv[(s, 2) + both])
            outs[(s, 0) + both] = g
            outs[(s, 1) + both] = delta
            outs[(s, 2) + both] = m_new
            outs[(s, 3) + both] = v_new
            for cp in stores(i):
                cp.start()

        stages = (stage0, stage1, stage2, stage3, stage4)
        for t in range(n + len(stages) - 1):
            for age in reversed(range(len(stages))):
                if 0 <= t - age < n:
                    stages[age](t - age)
        for i in range(max(0, n - 2), n):
            for cp in stores(i):
                cp.wait()

    hbm = pl.BlockSpec(memory_space=pltpu.HBM)
    outs = pl.pallas_call(
        body, name="reduce_adam_all",
        in_specs=[hbm] * (4 * n_w), out_specs=[hbm] * (4 * n_w),
        out_shape=[jax.ShapeDtypeStruct(w.shape, F32) for _ in range(4) for w in ws],
        scratch_shapes=[
            pltpu.VMEM((3, N_CHIPS, P, C), F32), pltpu.VMEM((3, N_CHIPS, P, C), F32),
            pltpu.VMEM((2, N_CHIPS, P, C), BF16), pltpu.VMEM((2, N_CHIPS, P, C), BF16),
            pltpu.VMEM((2, N_CHIPS, P, C), F32),
            pltpu.VMEM((2, 3, P, C), BF16), pltpu.VMEM((2, 3, P, C), BF16),
            pltpu.VMEM((2, 2, P, C), F32),
            pltpu.VMEM((2, 3, 2, P, C), F32), pltpu.VMEM((2, 4, 2, P, C), F32),
            pltpu.SemaphoreType.DMA((3, 2)), pltpu.SemaphoreType.DMA((2, 3, 2)),
            pltpu.SemaphoreType.DMA((2,)), pltpu.SemaphoreType.DMA((2,)),
            pltpu.SemaphoreType.DMA((2, 3)), pltpu.SemaphoreType.DMA((2, 3)),
            pltpu.SemaphoreType.DMA((2,)), pltpu.SemaphoreType.DMA((2,)),
            pltpu.SemaphoreType.DMA((2, 4, 2))],
        compiler_params=_params(None, 48),
    )(*grads, *ws, *ms, *vs)
    return [outs[i * n_w:(i + 1) * n_w] for i in range(4)]


def _allreduce_small(part):
    n_dev = 8

    def body(part_ref, out_ref, buf, send_sem, recv_sem):
        x, y, c, _ = _mesh_place()
        me = 4 * x + 2 * y + c
        buf[me] = part_ref[...]
        sends = []
        for k in range(1, n_dev):
            peer = ((1 - x) if k & 4 else x, (1 - y) if k & 2 else y, (1 - c) if k & 1 else c)
            cp = pltpu.make_async_remote_copy(src_ref=part_ref, dst_ref=buf.at[me], send_sem=send_sem.at[k - 1],
                                              recv_sem=recv_sem.at[k - 1], device_id=peer, device_id_type=MESH)
            cp.start()
            sends.append(cp)
        for cp in sends:
            cp.wait_recv()
        total = buf[0]
        for s in range(1, n_dev):
            total = total + buf[s]
        out_ref[...] = total
        for cp in sends:
            cp.wait_send()

    vmem = pl.BlockSpec(memory_space=pltpu.VMEM)
    return pl.pallas_call(
        body, name="allreduce_small", in_specs=[vmem], out_specs=vmem,
        out_shape=jax.ShapeDtypeStruct(part.shape, F32),
        scratch_shapes=[pltpu.VMEM((n_dev,) + part.shape, F32),
                        pltpu.SemaphoreType.DMA((n_dev - 1,)), pltpu.SemaphoreType.DMA((n_dev - 1,))],
    )(part)


def _adam_small(w, g, m, v):
    def body(w_ref, g_ref, m_ref, v_ref, d_ref, mo_ref, vo_ref):
        delta, m_new, v_new = _adamw(w_ref[...], g_ref[...], m_ref[...], v_ref[...])
        d_ref[...] = delta
        mo_ref[...] = m_new
        vo_ref[...] = v_new

    vmem = pl.BlockSpec(memory_space=pltpu.VMEM)
    return pl.pallas_call(
        body, name="adam_small", in_specs=[vmem] * 4, out_specs=[vmem] * 3,
        out_shape=[jax.ShapeDtypeStruct(w.shape, F32)] * 3,
    )(w, g, m, v)


BIG = ("a_w_in", "a_w_group", "a_w_out", "w_kv", "b_w_in", "b_w_out", "ple_w", "ple_gate_w")
SMALL = ("a_norm", "a_scale", "kv_norm", "b_norm", "k_norm", "b_q_norm")
SMALL_SHARDED = ("a_norm", "a_scale")
WEIGHTS = ("a_norm", "a_w_in", "a_w_group", "a_scale", "a_w_out", "kv_norm", "w_kv", "k_norm", "b_norm", "b_w_in",
           "b_q_norm", "b_w_out", "ple_w", "ple_gate_w")


def _as_matrix(a):
    return a.reshape(-1, a.shape[-1])


def _pack_small(arrs):
    rows = [jnp.pad(a.reshape(-1), (0, D_MODEL - a.size)) for a in arrs]
    rows += [jnp.zeros((D_MODEL,), F32)] * (8 - len(rows))
    return jnp.stack(rows)


def kernel(x, p, a_norm, a_w_in, a_w_group, a_scale, a_w_out, kv_norm, w_kv, k_norm, b_norm, b_w_in, b_q_norm, b_w_out, ple_w, ple_gate_w, loss_target, m_a_norm, m_a_w_in, m_a_w_group, m_a_scale, m_a_w_out, m_kv_norm, m_w_kv, m_k_norm, m_b_norm, m_b_w_in, m_b_q_norm, m_b_w_out, m_ple_w, m_ple_gate_w, v_a_norm, v_a_w_in, v_a_w_group, v_a_scale, v_a_w_out, v_kv_norm, v_w_kv, v_k_norm, v_b_norm, v_b_w_in, v_b_q_norm, v_b_w_out, v_ple_w, v_ple_gate_w):
    wts = dict(a_norm=a_norm, a_w_in=a_w_in, a_w_group=a_w_group, a_scale=a_scale, a_w_out=a_w_out, kv_norm=kv_norm,
               w_kv=w_kv, k_norm=k_norm, b_norm=b_norm, b_w_in=b_w_in, b_q_norm=b_q_norm, b_w_out=b_w_out,
               ple_w=ple_w, ple_gate_w=ple_gate_w)
    mom = dict(a_norm=m_a_norm, a_w_in=m_a_w_in, a_w_group=m_a_w_group, a_scale=m_a_scale, a_w_out=m_a_w_out,
               kv_norm=m_kv_norm, w_kv=m_w_kv, k_norm=m_k_norm, b_norm=m_b_norm, b_w_in=m_b_w_in,
               b_q_norm=m_b_q_norm, b_w_out=m_b_w_out, ple_w=m_ple_w, ple_gate_w=m_ple_gate_w)
    var = dict(a_norm=v_a_norm, a_w_in=v_a_w_in, a_w_group=v_a_w_group, a_scale=v_a_scale, a_w_out=v_a_w_out,
               kv_norm=v_kv_norm, w_kv=v_w_kv, k_norm=v_k_norm, b_norm=v_b_norm, b_w_in=v_b_w_in,
               b_q_norm=v_b_q_norm, b_w_out=v_b_w_out, ple_w=v_ple_w, ple_gate_w=v_ple_gate_w)
    S = x.shape[1]
    chip = 2 * lax.axis_index("x") + lax.axis_index("y")

    sharded_small = jnp.concatenate([a_norm.reshape(1, 256), a_scale.reshape(1, 256), jnp.zeros((6, 256), F32)], axis=0)
    later = ("a_w_group", "a_w_out", "w_kv", "b_w_in", "b_w_out", "ple_w", "ple_gate_w")
    (a_w_in_full,), small_full, copies = _allgather_weights(
        [_as_matrix(a_w_in)], sharded_small,
        [(_as_matrix(wts[n]), [(0, 256), (256, 512)] if n.startswith("ple") else [(0, _as_matrix(wts[n]).shape[0])])
         for n in later])
    local = dict(zip(("a_w_group", "a_w_out", "w_kv", "b_w_in", "b_w_out", "ple_w0", "ple_w1", "ple_gate_w0",
                      "ple_gate_w1"), copies))
    full = dict(a_w_in=a_w_in_full,
                a_norm=small_full[:, 0, :].reshape(1, D_MODEL), a_scale=small_full[:, 1, :].reshape(1, D_MODEL),
                kv_norm=kv_norm.reshape(1, D_MODEL), b_norm=b_norm.reshape(1, D_MODEL), k_norm=k_norm, b_q_norm=b_q_norm)

    grad_x, grads, small_part = _local_step(x.reshape(S, D_MODEL), p, loss_target.reshape(S, D_MODEL), full, local)

    out_g, out_d, out_m, out_v = {}, {}, {}, {}
    reduced = _reduce_adam_all([grads[n] for n in BIG], *[[_as_matrix(t[n]) for n in BIG] for t in (wts, mom, var)])
    for out, res in zip((out_g, out_d, out_m, out_v), reduced):
        for n, t in zip(BIG, res):
            out[n] = t.reshape(wts[n].shape)

    small_sum = _allreduce_small(small_part)
    loss = small_sum[len(SMALL), 0]
    small_rows = []
    for i, n in enumerate(SMALL):
        row = small_sum[i]
        if n in SMALL_SHARDED:
            row = lax.dynamic_slice(row, (chip * 256,), (256,))
        else:
            row = row[:wts[n].size]
        small_rows.append(row)
    g_small = _pack_small(small_rows)
    d_small, m_small, v_small = _adam_small(_pack_small([wts[n] for n in SMALL]), g_small,
                                            _pack_small([mom[n] for n in SMALL]), _pack_small([var[n] for n in SMALL]))
    for i, n in enumerate(SMALL):
        shape, size = wts[n].shape, wts[n].size
        out_g[n], out_d[n], out_m[n], out_v[n] = (t[i, :size].reshape(shape) for t in (g_small, d_small, m_small, v_small))

    return (loss, grad_x.reshape(1, S, D_MODEL), *[out_g[n] for n in WEIGHTS], *[out_d[n] for n in WEIGHTS],
            *[out_m[n] for n in WEIGHTS], *[out_v[n] for n in WEIGHTS])
```

```python
import functools

import jax
import jax.numpy as jnp
from jax import lax
from jax.experimental import pallas as pl
from jax.experimental.pallas import tpu as pltpu

F32 = jnp.float32
BF16 = jnp.bfloat16
MESH = pl.DeviceIdType.MESH

D_MODEL = 1024
N_HEADS = 16
HEAD_DIM = 64
PLE_DIM = 256
N_GROUPS = 4
GROUP_DIM = 256
POOL_WINDOWS = (2, 4, 8, 16)
N_CHIPS = 4
EPS = 1e-6
SB_SCALE = HEAD_DIM ** -0.5

ADAM_LR = 0.001
ADAM_B1 = 0.9
ADAM_B2 = 0.999
ADAM_EPS = 1e-08
ADAM_WD = 0.01
ADAM_STEP = 10

ROW_TILE = 256
ATT_Q_TILE = 512
ATT_K_TILE = 256
WGRAD_SEQ_TILE = 1024
WGRAD_ACC_BYTES = 4 * 1024 * 1024
MIB = 1024 * 1024


def _params(semantics=None, vmem_mib=48):
    return pltpu.CompilerParams(dimension_semantics=semantics, vmem_limit_bytes=vmem_mib * MIB)


def _dot(a, b):
    return jnp.dot(a, b, preferred_element_type=F32)


def _dot_nt(a, b):
    return lax.dot_general(a, b, (((1,), (1,)), ((), ())), preferred_element_type=F32)


def _dot_tn(a, b):
    return lax.dot_general(a, b, (((0,), (0,)), ((), ())), preferred_element_type=F32)


def _hilo(x):
    hi = x.astype(BF16)
    lo = (x - hi.astype(F32)).astype(BF16)
    return hi, lo


def _dot_hilo(x, w):
    hi, lo = _hilo(x)
    return _dot(hi, w) + _dot(lo, w)


def _sigmoid(z):
    return jax.nn.sigmoid(z)


def _dsilu(z, sg):
    return sg * (1.0 + z * (1.0 - sg))


def _mask_bf16(cond):
    return jnp.where(cond, 1.0, 0.0).astype(BF16)


def _head_mean_matrix():
    r = lax.broadcasted_iota(jnp.int32, (256, 256), 0) // HEAD_DIM
    c = lax.broadcasted_iota(jnp.int32, (256, 256), 1) // HEAD_DIM
    return _mask_bf16(r == c)


def _head_mean(x, bd):
    parts = []
    for s in range(x.shape[1] // 256):
        parts.append(_dot_hilo(x[:, s * 256:(s + 1) * 256], bd))
    out = parts[0] if len(parts) == 1 else jnp.concatenate(parts, axis=1)
    return out * (1.0 / HEAD_DIM)


def _a_in(x, gain, w_sh, gather=()):
    S = x.shape[0]
    tm = 512
    nsh, _, wn = w_sh.shape

    def body(x_ref, g_ref, w_ref, uz_ref, h_ref):
        @pl.when(pl.program_id(1) == 0)
        def _():
            xv = x_ref[...]
            r = lax.rsqrt(jnp.mean(xv * xv, axis=-1, keepdims=True) + EPS)
            h_ref[...] = (xv * r * g_ref[...]).astype(BF16)

        uz_ref[...] = _dot(h_ref[...], w_ref[0])

    return _call_with_gather(
        body, name="a_in", grid=(S // tm, nsh),
        in_specs=[pl.BlockSpec((tm, D_MODEL), lambda i, j: (i, 0)),
                  pl.BlockSpec((1, D_MODEL), lambda i, j: (0, 0)),
                  pl.BlockSpec((1, D_MODEL, wn), lambda i, j: (j, 0, 0))],
        out_specs=[pl.BlockSpec((tm, wn), lambda i, j: (i, j)),
                   pl.BlockSpec((tm, D_MODEL), lambda i, j: (i, 0))],
        out_shape=[jax.ShapeDtypeStruct((S, nsh * wn), F32),
                   jax.ShapeDtypeStruct((S, D_MODEL), BF16)],
        args=(x, gain, w_sh), gather=gather)


def _inv_count(first_row, rows, w):
    t1 = first_row + 1 + lax.broadcasted_iota(jnp.int32, (rows, 1), 0)
    return 1.0 / jnp.minimum(t1, w).astype(F32)


def _group_weight(wg_ref, g):
    return jnp.concatenate([wg_ref[sh, g] for sh in range(N_CHIPS)], axis=0)


def _a_mix(uz, wg, scale, gather=()):
    S = uz.shape[0]
    tm = ROW_TILE

    def body(u_ref, up_ref, z_ref, wg_ref, sc_ref, ga_ref, p_ref):
        i = pl.program_id(0)
        row = lax.broadcasted_iota(jnp.int32, (tm, tm), 0)
        col = lax.broadcasted_iota(jnp.int32, (tm, tm), 1)
        d = row - col
        for g, w in enumerate(POOL_WINDOWS):
            cols = slice(g * GROUP_DIM, (g + 1) * GROUP_DIM)
            t_main = _mask_bf16((d >= 0) & (d < w))
            t_halo = _mask_bf16(d + tm < w)
            u = u_ref[:, cols]
            up = jnp.where(i > 0, up_ref[:, cols], 0.0)
            hi, lo = _hilo(u)
            hip, lop = _hilo(up)
            wsum = _dot(t_main, hi) + _dot(t_main, lo) + _dot(t_halo, hip) + _dot(t_halo, lop)
            pooled = (wsum * _inv_count(i * tm, tm, w) - u).astype(BF16)
            p_ref[:, cols] = pooled
            mraw = _dot(pooled, _group_weight(wg_ref, g))
            z = z_ref[:, cols]
            ga_ref[:, cols] = (mraw * sc_ref[:, cols] * (z * _sigmoid(z))).astype(BF16)

    return _call_with_gather(
        body, name="a_mix", grid=(S // tm,),
        in_specs=[pl.BlockSpec((tm, D_MODEL), lambda i: (i, 0)),
                  pl.BlockSpec((tm, D_MODEL), lambda i: (jnp.maximum(i - 1, 0), 0)),
                  pl.BlockSpec((tm, D_MODEL), lambda i: (i, 1)),
                  pl.BlockSpec((N_CHIPS, N_GROUPS, 64, GROUP_DIM), lambda i: (0, 0, 0, 0)),
                  pl.BlockSpec((1, D_MODEL), lambda i: (0, 0))],
        out_specs=[pl.BlockSpec((tm, D_MODEL), lambda i: (i, 0)),
                   pl.BlockSpec((tm, D_MODEL), lambda i: (i, 0))],
        out_shape=[jax.ShapeDtypeStruct((S, D_MODEL), BF16),
                   jax.ShapeDtypeStruct((S, D_MODEL), BF16)],
        args=(uz, uz, uz, wg, scale), gather=gather)


def _out_ple(name, gated, x_in, w_out, p, layer, ple_w, ple_g, target=None, gather=()):
    S = x_in.shape[0]
    tm = ROW_TILE
    with_loss = target is not None

    def body(*refs):
        if with_loss:
            g_ref, x_ref, wo_ref, p_ref, pw_ref, pg_ref, t_ref, xm_ref, dx_ref, e_ref, gt_ref, loss_ref = refs
        else:
            g_ref, x_ref, wo_ref, p_ref, pw_ref, pg_ref, xm_ref, xo_ref, e_ref, gt_ref = refs
        xm = x_ref[...] + _dot(g_ref[...], wo_ref[...])
        xm_ref[...] = xm
        pb = p_ref[...].astype(BF16)
        e = jnp.concatenate([_dot(pb, pw_ref[sh]) for sh in range(N_CHIPS)], axis=1)
        pg = jnp.concatenate([pg_ref[sh] for sh in range(N_CHIPS)], axis=0)
        gate = _sigmoid(_dot(xm.astype(BF16), pg))
        e_ref[...] = e.astype(BF16)
        gt_ref[...] = gate.astype(BF16)
        xo = xm + e * gate
        if with_loss:
            diff = xo - t_ref[...]
            dx_ref[...] = diff * (1.0 / D_MODEL)

            @pl.when(pl.program_id(0) == 0)
            def _():
                loss_ref[...] = jnp.zeros_like(loss_ref)

            loss_ref[...] += jnp.sum(diff * diff) * (0.5 / D_MODEL)
        else:
            xo_ref[...] = xo

    row = pl.BlockSpec((tm, D_MODEL), lambda i: (i, 0))
    in_specs = [row, row,
                pl.BlockSpec((D_MODEL, D_MODEL), lambda i: (0, 0)),
                pl.BlockSpec((None, None, tm, PLE_DIM), lambda i: (layer, 0, i, 0)),
                pl.BlockSpec((N_CHIPS, PLE_DIM, 256), lambda i: (0, 0, 0)),
                pl.BlockSpec((N_CHIPS, 256, D_MODEL), lambda i: (0, 0, 0))]
    args = [gated, x_in, w_out, p, ple_w, ple_g]
    out_specs = [row, row, row, row]
    out_shape = [jax.ShapeDtypeStruct((S, D_MODEL), F32), jax.ShapeDtypeStruct((S, D_MODEL), F32),
                 jax.ShapeDtypeStruct((S, D_MODEL), BF16), jax.ShapeDtypeStruct((S, D_MODEL), BF16)]
    if with_loss:
        in_specs.append(row)
        args.append(target)
        out_specs.append(pl.BlockSpec((8, 128), lambda i: (0, 0)))
        out_shape.append(jax.ShapeDtypeStruct((8, 128), F32))
    return _call_with_gather(body, name=name, grid=(S // tm,), in_specs=in_specs, out_specs=out_specs,
                             out_shape=out_shape, args=args, gather=gather)


def _b_in(x, kv_gain, b_gain, k_gain_t, q_gain_t, w_kv, w_in, gather=()):
    S = x.shape[0]
    tm = ROW_TILE

    def body(x_ref, kvg_ref, bg_ref, kg_ref, qg_ref, wkv_ref, win_ref,
             hkv_ref, hb_ref, kraw_ref, qraw_ref, k_ref, q_ref, v_ref, z_ref):
        xv = x_ref[...]
        y = xv * lax.rsqrt(jnp.mean(xv * xv, axis=-1, keepdims=True) + EPS)
        hkv = (y * kvg_ref[...]).astype(BF16)
        hb = (y * bg_ref[...]).astype(BF16)
        hkv_ref[...] = hkv
        hb_ref[...] = hb
        bd = _head_mean_matrix()

        def head_norm(raw, gain):
            rr = lax.rsqrt(_head_mean(raw * raw, bd) + EPS)
            return raw * rr * gain

        for sh in range(N_CHIPS):
            kvc = _dot(hkv, wkv_ref[sh])
            qzc = _dot(hb, win_ref[sh])
            cols = slice((sh % 2) * 512, (sh % 2) * 512 + 512)
            if sh < 2:
                kraw_ref[:, cols] = kvc.astype(BF16)
                qraw_ref[:, cols] = qzc.astype(BF16)
                k_ref[:, cols] = head_norm(kvc, kg_ref[:, cols]).astype(BF16)
                q_ref[:, cols] = (head_norm(qzc, qg_ref[:, cols]) * SB_SCALE).astype(BF16)
            else:
                v_ref[:, cols] = kvc.astype(BF16)
                z_ref[:, cols] = qzc.astype(BF16)

    row = pl.BlockSpec((tm, D_MODEL), lambda i: (i, 0))
    vec = pl.BlockSpec((1, D_MODEL), lambda i: (0, 0))
    wsp = pl.BlockSpec((N_CHIPS, D_MODEL, 512), lambda i: (0, 0, 0))
    return _call_with_gather(
        body, name="b_in", grid=(S // tm,),
        in_specs=[row, vec, vec, vec, vec, wsp, wsp],
        out_specs=[row] * 8,
        out_shape=[jax.ShapeDtypeStruct((S, D_MODEL), BF16)] * 8,
        args=(x, kv_gain, b_gain, k_gain_t, q_gain_t, w_kv, w_in), gather=gather, vmem_mib=56)


def _softplus_parts(z):
    e = jnp.exp(-jnp.abs(z))
    return -(jnp.maximum(z, 0.0) + jnp.log(1.0 + e)), e


def _add_rows(total, first_row, update):
    if first_row == 0:
        return total + update
    return jnp.concatenate([total[:first_row], total[first_row:] + update], axis=0)


def _attn_fwd(q, k, v, zgate):
    S = q.shape[0]
    tq, tk = ATT_Q_TILE, ATT_K_TILE
    kpq = tq // tk

    def body(q_ref, k_ref, v_ref, z_ref, o_ref, g_ref, lt_ref):
        qi = pl.program_id(1)
        lane = lax.broadcasted_iota(jnp.int32, (1, 128), 1)
        ri = lax.broadcasted_iota(jnp.int32, (tk, tk), 0)
        ci = lax.broadcasted_iota(jnp.int32, (tk, tk), 1)
        later_mat = _mask_bf16(ri > ci)
        t_idx = qi * tq + lax.broadcasted_iota(jnp.int32, (tq, tk), 0)
        s_off = lax.broadcasted_iota(jnp.int32, (tq, tk), 1)
        qv = q_ref[...]
        first = lane < HEAD_DIM
        q_heads = (jnp.where(first, qv, jnp.zeros_like(qv)), jnp.where(first, jnp.zeros_like(qv), qv))

        def step(kj_last, carry, masked):
            chains = [(d, h) for d in range(kpq) for h in range(2)]
            r0 = [(kpq - 1 - d) * tk if masked else 0 for d in range(kpq)]
            s0 = [pl.multiple_of((kj_last - d) * tk, tk) for d in range(kpq)]
            kb = [k_ref[pl.ds(s, tk), :] for s in s0]
            vb = [v_ref[pl.ds(s, tk), :] for s in s0]
            visible = [(s + s_off < t_idx)[r:] for s, r in zip(s0, r0)] if masked else None
            z = {c: _dot_nt(q_heads[c[1]][r0[c[0]]:], kb[c[0]]) for c in chains}
            run = [carry[0], carry[2]]
            log_own, later, run_at = {}, {}, {}
            for c in chains:
                kj, h = c
                lk = _softplus_parts(z[c])[0]
                if masked:
                    lk = jnp.where(visible[kj], lk, 0.0)
                log_own[c] = z[c] + lk
                later[c] = _dot(lk.astype(BF16), later_mat)
                run_at[c] = run[h][r0[kj]:]
                run[h] = _add_rows(run[h], r0[kj], jnp.sum(lk, axis=-1, keepdims=True))
            acc = [carry[1], carry[3]]
            for c in chains:
                kj, h = c
                a = jnp.exp(log_own[c] + later[c] + run_at[c])
                if masked:
                    a = jnp.where(visible[kj], a, 0.0)
                acc[h] = _add_rows(acc[h], r0[kj], _dot(a.astype(BF16), vb[kj]))
            return run[0], acc[0], run[1], acc[1]

        zero1, zero128 = jnp.zeros((tq, 1), F32), jnp.zeros((tq, 128), F32)
        carry = step(qi * kpq + kpq - 1, (zero1, zero128, zero1, zero128), True)
        carry = lax.fori_loop(0, qi, lambda n, c: step((qi - n) * kpq - 1, c, False), carry)
        o_tot = jnp.where(first, carry[1], carry[3])
        l_tot = jnp.where(first, carry[0], carry[2])
        o_ref[...] = o_tot.astype(BF16)
        lt_ref[...] = l_tot
        zz = z_ref[...].astype(F32)
        g_ref[...] = (o_tot * (zz * _sigmoid(zz))).astype(BF16)

    blk = pl.BlockSpec((tq, 128), lambda hp, qi: (qi, hp))
    seq = pl.BlockSpec((S, 128), lambda hp, qi: (0, hp))
    return pl.pallas_call(
        body, name="attn_fwd", grid=(D_MODEL // 128, S // tq),
        in_specs=[blk, seq, seq, blk], out_specs=[blk, blk, blk],
        out_shape=[jax.ShapeDtypeStruct((S, D_MODEL), BF16)] * 2 + [jax.ShapeDtypeStruct((S, D_MODEL), F32)],
        compiler_params=_params(("parallel", "arbitrary")),
    )(q, k, v, zgate)


def _ple_out_bwd(name, dx_out, e, gate, ple_g, w_out):
    S = dx_out.shape[0]
    tm = ROW_TILE

    def body(dx_ref, e_ref, gt_ref, pg_ref, wo_ref, de_ref, dgp_ref, dxm_ref, dg_ref):
        dxo = dx_ref[...]
        ev = e_ref[...].astype(F32)
        gv = gt_ref[...].astype(F32)
        de_ref[...] = (dxo * gv).astype(BF16)
        dgp = (dxo * ev * gv * (1.0 - gv)).astype(BF16)
        dgp_ref[...] = dgp
        pg = jnp.concatenate([pg_ref[sh] for sh in range(N_CHIPS)], axis=0)
        dxm = dxo + _dot_nt(dgp, pg)
        dxm_ref[...] = dxm
        dg_ref[...] = _dot_nt(dxm.astype(BF16), wo_ref[...]).astype(BF16)

    row = pl.BlockSpec((tm, D_MODEL), lambda i: (i, 0))
    return pl.pallas_call(
        body, name=name, grid=(S // tm,),
        in_specs=[row, row, row,
                  pl.BlockSpec((N_CHIPS, 256, D_MODEL), lambda i: (0, 0, 0)),
                  pl.BlockSpec((D_MODEL, D_MODEL), lambda i: (0, 0))],
        out_specs=[row, row, row, row],
        out_shape=[jax.ShapeDtypeStruct((S, D_MODEL), BF16), jax.ShapeDtypeStruct((S, D_MODEL), BF16),
                   jax.ShapeDtypeStruct((S, D_MODEL), F32), jax.ShapeDtypeStruct((S, D_MODEL), BF16)],
        compiler_params=_params(("arbitrary",)),
    )(dx_out, e, gate, ple_g, w_out)


def _attn_bwd(q, k, v, ltot, dgated, o, zgate, reduce=None):
    S = q.shape[0]
    tq, tk = ATT_Q_TILE, ATT_K_TILE
    kpq = tq // tk
    nq = S // tq

    def body(q_ref, k_ref, v_ref, lt_ref, dg_ref, o_ref, z_ref, dq_ref, dk_ref, dv_ref, dz_ref, dk_acc, dv_acc):
        qi = pl.program_id(1)

        @pl.when(qi == 0)
        def _():
            dk_acc[...] = jnp.zeros_like(dk_acc)
            dv_acc[...] = jnp.zeros_like(dv_acc)

        lane = lax.broadcasted_iota(jnp.int32, (1, 128), 1)
        ri = lax.broadcasted_iota(jnp.int32, (tk, tk), 0)
        ci = lax.broadcasted_iota(jnp.int32, (tk, tk), 1)
        later_mat = _mask_bf16(ri > ci)
        before_mat = _mask_bf16(ri < ci)
        t_idx = qi * tq + lax.broadcasted_iota(jnp.int32, (tq, tk), 0)
        s_off = lax.broadcasted_iota(jnp.int32, (tq, tk), 1)
        zz = z_ref[...].astype(F32)
        sg = _sigmoid(zz)
        dgv = dg_ref[...].astype(F32)
        dz_ref[...] = (dgv * o_ref[...].astype(F32) * _dsilu(zz, sg)).astype(BF16)
        dob = (dgv * (zz * sg)).astype(BF16)
        ltv = lt_ref[...]
        qv = q_ref[...]
        first = lane < HEAD_DIM
        masks = (first, jnp.logical_not(first))
        q_heads = [jnp.where(hm, qv, jnp.zeros_like(qv)) for hm in masks]
        do_heads = [jnp.where(hm, dob, jnp.zeros_like(dob)) for hm in masks]
        totals = [jnp.max(jnp.where(hm, ltv, -jnp.inf), axis=-1, keepdims=True) for hm in masks]

        def step(kj_first, carry, masked):
            chains = [(d, h) for d in range(kpq) for h in range(2)]
            r0 = [d * tk if masked else 0 for d in range(kpq)]
            s0 = [pl.multiple_of((kj_first + d) * tk, tk) for d in range(kpq)]
            kb = [k_ref[pl.ds(s, tk), :] for s in s0]
            vb = [v_ref[pl.ds(s, tk), :] for s in s0]
            visible = [(s + s_off < t_idx)[r:] for s, r in zip(s0, r0)] if masked else None
            z = {c: _dot_nt(q_heads[c[1]][r0[c[0]]:], kb[c[0]]) for c in chains}
            da = {c: _dot_nt(do_heads[c[1]][r0[c[0]]:], vb[c[0]]) for c in chains}
            run = [carry[0], carry[3]]
            log_own, beta, later, base = {}, {}, {}, {}
            for c in chains:
                kj, h = c
                lk = _softplus_parts(z[c])[0]
                if masked:
                    lk = jnp.where(visible[kj], lk, 0.0)
                log_own[c] = z[c] + lk
                beta[c] = jnp.exp(log_own[c]).astype(BF16)
                later[c] = _dot(lk.astype(BF16), later_mat)
                run[h] = _add_rows(run[h], r0[kj], jnp.sum(lk, axis=-1, keepdims=True))
                base[c] = (totals[h] - run[h])[r0[kj]:]
            grun = [carry[1], carry[4]]
            a_bf, g_bf, gbefore, grun_at = {}, {}, {}, {}
            for c in chains:
                kj, h = c
                a = jnp.exp(log_own[c] + later[c] + base[c])
                if masked:
                    a = jnp.where(visible[kj], a, 0.0)
                a_bf[c] = a.astype(BF16)
                g = da[c] * a
                g_bf[c] = g.astype(BF16)
                gbefore[c] = _dot(g_bf[c], before_mat)
                grun_at[c] = grun[h][r0[kj]:]
                grun[h] = _add_rows(grun[h], r0[kj], jnp.sum(g, axis=-1, keepdims=True))
            dq = [carry[2], carry[5]]
            dk_blk = [jnp.zeros((tk, 128), F32) for _ in range(kpq)]
            dv_blk = [jnp.zeros((tk, 128), F32) for _ in range(kpq)]
            for c in chains:
                kj, h = c
                g = g_bf[c].astype(F32)
                dz = g - beta[c].astype(F32) * (g + gbefore[c] + grun_at[c])
                if masked:
                    dz = jnp.where(visible[kj], dz, 0.0)
                dzb = dz.astype(BF16)
                dq[h] = _add_rows(dq[h], r0[kj], _dot(dzb, kb[kj]))
                dk_blk[kj] = dk_blk[kj] + _dot_tn(dzb, q_heads[h][r0[kj]:])
                dv_blk[kj] = dv_blk[kj] + _dot_tn(a_bf[c], do_heads[h][r0[kj]:])
            for d in range(kpq):
                dk_acc[pl.ds(s0[d], tk), :] += dk_blk[d]
                dv_acc[pl.ds(s0[d], tk), :] += dv_blk[d]
            return run[0], grun[0], dq[0], run[1], grun[1], dq[1]

        zero1, zero128 = jnp.zeros((tq, 1), F32), jnp.zeros((tq, 128), F32)
        carry = lax.fori_loop(0, qi, lambda n, c: step(n * kpq, c, False),
                              (zero1, zero1, zero128, zero1, zero1, zero128))
        carry = step(qi * kpq, carry, True)
        dq_ref[...] = jnp.where(first, carry[2], carry[5]).astype(BF16)

        @pl.when(qi == nq - 1)
        def _():
            dk_ref[...] = dk_acc[...].astype(BF16)
            dv_ref[...] = dv_acc[...].astype(BF16)

    blk = pl.BlockSpec((tq, 128), lambda hp, qi: (qi, hp))
    seq = pl.BlockSpec((S, 128), lambda hp, qi: (0, hp))
    return _call_with_gather(
        body, name="attn_bwd", grid=(D_MODEL // 128, nq),
        in_specs=[blk, seq, seq, blk, blk, blk, blk], out_specs=[blk, seq, seq, blk],
        out_shape=[jax.ShapeDtypeStruct((S, D_MODEL), BF16)] * 4,
        scratch_shapes=[pltpu.VMEM((S, 128), F32), pltpu.VMEM((S, 128), F32)],
        args=(q, k, v, ltot, dgated, o, zgate), reduce=reduce, vmem_mib=56)


def _rms_bwd(xv, dh_gain_sum):
    r = lax.rsqrt(jnp.mean(xv * xv, axis=-1, keepdims=True) + EPS)
    xhat = xv * r
    dx = r * (dh_gain_sum - xhat * jnp.mean(dh_gain_sum * xhat, axis=-1, keepdims=True))
    return dx, xhat


def _b_in_bwd(dq, dk, dv, dz, q_raw, k_raw, x, dx_mid, q_gain_t, k_gain_t, b_gain, kv_gain, w_in, w_kv):
    S = x.shape[0]
    tm = ROW_TILE

    def body(dq_ref, dk_ref, dv_ref, dz_ref, qr_ref, kr_ref, x_ref, dxm_ref, qg_ref, kg_ref, bg_ref, kvg_ref,
             win_ref, wkv_ref, dqz_ref, dkv_ref, dx_ref, small_ref):
        @pl.when(pl.program_id(0) == 0)
        def _():
            small_ref[...] = jnp.zeros_like(small_ref)

        bd = _head_mean_matrix()

        def head_norm_bwd(dy_ref, raw_ref, gain, scale):
            raw = raw_ref[...].astype(F32)
            rr = lax.rsqrt(_head_mean(raw * raw, bd) + EPS)
            xhat = raw * rr
            dy = dy_ref[...].astype(F32) * scale
            gdy = dy * gain
            draw = rr * (gdy - xhat * _head_mean(gdy * xhat, bd))
            return draw.astype(BF16), jnp.sum(dy * xhat, axis=0, keepdims=True)

        dqr, dqg = head_norm_bwd(dq_ref, qr_ref, qg_ref[...], SB_SCALE)
        dkr, dkg = head_norm_bwd(dk_ref, kr_ref, kg_ref[...], 1.0)
        dqz_ref[:, :D_MODEL] = dqr
        dqz_ref[:, D_MODEL:] = dz_ref[...]
        dkv_ref[:, :D_MODEL] = dkr
        dkv_ref[:, D_MODEL:] = dv_ref[...]
        dhb = jnp.zeros((tm, D_MODEL), F32)
        dhkv = jnp.zeros((tm, D_MODEL), F32)
        for sh in range(N_CHIPS):
            cols = slice(sh * 512, (sh + 1) * 512)
            dhb = dhb + _dot_nt(dqz_ref[:, cols], win_ref[sh])
            dhkv = dhkv + _dot_nt(dkv_ref[:, cols], wkv_ref[sh])
        dx, xhat = _rms_bwd(x_ref[...], dhb * bg_ref[...] + dhkv * kvg_ref[...])
        dx_ref[...] = dxm_ref[...] + dx
        small_ref[0:1, :] += dqg
        small_ref[1:2, :] += dkg
        small_ref[2:3, :] += jnp.sum(dhb * xhat, axis=0, keepdims=True)
        small_ref[3:4, :] += jnp.sum(dhkv * xhat, axis=0, keepdims=True)

    row = pl.BlockSpec((tm, D_MODEL), lambda i: (i, 0))
    wide = pl.BlockSpec((tm, 2 * D_MODEL), lambda i: (i, 0))
    vec = pl.BlockSpec((1, D_MODEL), lambda i: (0, 0))
    wsp = pl.BlockSpec((N_CHIPS, D_MODEL, 512), lambda i: (0, 0, 0))
    return pl.pallas_call(
        body, name="b_in_bwd", grid=(S // tm,),
        in_specs=[row] * 8 + [vec] * 4 + [wsp, wsp],
        out_specs=[wide, wide, row, pl.BlockSpec((8, D_MODEL), lambda i: (0, 0))],
        out_shape=[jax.ShapeDtypeStruct((S, 2 * D_MODEL), BF16), jax.ShapeDtypeStruct((S, 2 * D_MODEL), BF16),
                   jax.ShapeDtypeStruct((S, D_MODEL), F32), jax.ShapeDtypeStruct((8, D_MODEL), F32)],
        compiler_params=_params(("arbitrary",), 56),
    )(dq, dk, dv, dz, q_raw, k_raw, x, dx_mid, q_gain_t, k_gain_t, b_gain, kv_gain, w_in, w_kv)


def _a_mix_bwd(dgated, uz, pooled, wg, scale, w_in, x, dx_mid, gain, reduce=None):
    S = x.shape[0]
    tm = ROW_TILE
    n = S // tm

    def body(dg_ref, z_ref, p_ref, wg_ref, sc_ref, win_ref, x_ref, dxm_ref, gn_ref,
             duz_ref, dmr_ref, dx_ref, small_ref, halo_hi, halo_lo):
        i = pl.program_id(0)

        @pl.when(i == 0)
        def _():
            small_ref[...] = jnp.zeros_like(small_ref)
            halo_hi[...] = jnp.zeros_like(halo_hi)
            halo_lo[...] = jnp.zeros_like(halo_lo)

        first_row = (n - 1 - i) * tm
        row = lax.broadcasted_iota(jnp.int32, (tm, tm), 0)
        col = lax.broadcasted_iota(jnp.int32, (tm, tm), 1)
        d = col - row
        for g, w in enumerate(POOL_WINDOWS):
            cols = slice(g * GROUP_DIM, (g + 1) * GROUP_DIM)
            wgg = _group_weight(wg_ref, g)
            sc = sc_ref[:, cols]
            mraw = _dot(p_ref[:, cols], wgg)
            z = z_ref[:, cols]
            sg = _sigmoid(z)
            dga = dg_ref[:, cols].astype(F32)
            dm = dga * (z * sg)
            duz_ref[:, D_MODEL + g * GROUP_DIM:D_MODEL + (g + 1) * GROUP_DIM] = (
                dga * (mraw * sc) * _dsilu(z, sg)).astype(BF16)
            small_ref[0:1, cols] += jnp.sum(dm * mraw, axis=0, keepdims=True)
            dmr = (dm * sc).astype(BF16)
            dmr_ref[:, cols] = dmr
            dp = _dot_nt(dmr, wgg)
            hi, lo = _hilo(dp * _inv_count(first_row, tm, w))
            t_main = _mask_bf16((d >= 0) & (d < w))
            t_halo = _mask_bf16(d + tm < w)
            du = (_dot(t_main, hi) + _dot(t_main, lo) + _dot(t_halo, halo_hi[:, cols]) + _dot(t_halo, halo_lo[:, cols])
                  - dp)
            halo_hi[:, cols] = hi
            halo_lo[:, cols] = lo
            duz_ref[:, cols] = du.astype(BF16)
        dh = jnp.zeros((tm, D_MODEL), F32)
        for sh in range(N_CHIPS):
            dh = dh + _dot_nt(duz_ref[:, sh * 512:(sh + 1) * 512], win_ref[sh])
        dx, xhat = _rms_bwd(x_ref[...], dh * gn_ref[...])
        dx_ref[...] = dxm_ref[...] + dx
        small_ref[1:2, :] += jnp.sum(dh * xhat, axis=0, keepdims=True)

    rev = lambda i: (n - 1 - i, 0)
    row = pl.BlockSpec((tm, D_MODEL), rev)
    vec = pl.BlockSpec((1, D_MODEL), lambda i: (0, 0))
    return _call_with_gather(
        body, name="a_mix_bwd", grid=(n,),
        in_specs=[row,
                  pl.BlockSpec((tm, D_MODEL), lambda i: (n - 1 - i, 1)),
                  row,
                  pl.BlockSpec((N_CHIPS, N_GROUPS, 64, GROUP_DIM), lambda i: (0, 0, 0, 0)),
                  vec,
                  pl.BlockSpec((N_CHIPS, D_MODEL, 512), lambda i: (0, 0, 0)),
                  row, row, vec],
        out_specs=[pl.BlockSpec((tm, 2 * D_MODEL), rev), row, row,
                   pl.BlockSpec((8, D_MODEL), lambda i: (0, 0))],
        out_shape=[jax.ShapeDtypeStruct((S, 2 * D_MODEL), BF16), jax.ShapeDtypeStruct((S, D_MODEL), BF16),
                   jax.ShapeDtypeStruct((S, D_MODEL), F32), jax.ShapeDtypeStruct((8, D_MODEL), F32)],
        scratch_shapes=[pltpu.VMEM((tm, D_MODEL), BF16), pltpu.VMEM((tm, D_MODEL), BF16)],
        args=(dgated, uz, pooled, wg, scale, w_in, x, dx_mid, gain), reduce=reduce, vmem_mib=56)


def _wgrad(name, a, dy, n_shards, a_spec=None, k_dim=None):
    S, n_cols = dy.shape
    ts = WGRAD_SEQ_TILE
    k_dim = a.shape[-1] if k_dim is None else k_dim
    wn = n_cols // n_shards
    tk = min(k_dim, WGRAD_ACC_BYTES // (4 * n_cols))
    nst = S // ts

    def body(a_ref, dy_ref, out_ref, acc):
        st = pl.program_id(1)

        @pl.when(st == 0)
        def _():
            acc[...] = jnp.zeros_like(acc)

        acc[...] += _dot_tn(a_ref[...].astype(BF16), dy_ref[...].astype(BF16))

        @pl.when(st == nst - 1)
        def _():
            for sh in range(n_shards):
                out_ref[sh] = acc[:, sh * wn:(sh + 1) * wn]

    if a_spec is None:
        a_spec = pl.BlockSpec((ts, tk), lambda kt, st: (st, kt))
    return pl.pallas_call(
        body, name=name, grid=(k_dim // tk, nst),
        in_specs=[a_spec, pl.BlockSpec((ts, n_cols), lambda kt, st: (st, 0))],
        out_specs=pl.BlockSpec((n_shards, tk, wn), lambda kt, st: (0, kt, 0)),
        out_shape=jax.ShapeDtypeStruct((n_shards, k_dim, wn), F32),
        scratch_shapes=[pltpu.VMEM((tk, n_cols), F32)],
        compiler_params=_params(("parallel", "arbitrary")),
    )(a, dy)


def _wgrad_ple(name, p, layer, de):
    ts = WGRAD_SEQ_TILE
    spec = pl.BlockSpec((None, None, ts, PLE_DIM), lambda kt, st: (layer, 0, st, 0))
    return _wgrad(name, p, de, N_CHIPS, a_spec=spec, k_dim=PLE_DIM)


def _wgrad_group(pooled, dmr):
    S = pooled.shape[0]
    ts = WGRAD_SEQ_TILE
    nst = S // ts

    def body(p_ref, d_ref, out_ref, acc):
        st = pl.program_id(1)

        @pl.when(st == 0)
        def _():
            acc[...] = jnp.zeros_like(acc)

        acc[...] += _dot_tn(p_ref[...], d_ref[...])

        @pl.when(st == nst - 1)
        def _():
            for sh in range(N_CHIPS):
                out_ref[sh] = acc[sh * 64:(sh + 1) * 64, :]

    blk = pl.BlockSpec((ts, GROUP_DIM), lambda g, st: (st, g))
    return pl.pallas_call(
        body, name="wgrad_group", grid=(N_GROUPS, nst),
        in_specs=[blk, blk],
        out_specs=pl.BlockSpec((N_CHIPS, None, 64, GROUP_DIM), lambda g, st: (0, g, 0, 0)),
        out_shape=jax.ShapeDtypeStruct((N_CHIPS, N_GROUPS, 64, GROUP_DIM), F32),
        scratch_shapes=[pltpu.VMEM((GROUP_DIM, GROUP_DIM), F32)],
        compiler_params=_params(("parallel", "arbitrary")),
    )(pooled, dmr)


GATHER_AT = {
    "a_in": ("a_w_group", "a_w_out", "ple_w0", "ple_gate_w0"),
    "a_mix": ("w_kv",),
    "a_out_ple": ("b_w_in",),
    "b_in": ("b_w_out", "ple_w1", "ple_gate_w1"),
}


REDUCE_AT = {
    "attn_bwd": ("b_w_out", "ple_w1", "ple_gate_w1"),
    "a_mix_bwd": ("w_kv", "b_w_in"),
}


def _local_step(x, p, target, w, local=None, state=None):
    w = dict(w)

    def run(fn, host, n_out, *args, **kwargs):
        names = GATHER_AT[host] if local is not None else ()
        res = fn(*args, gather=[local[n] for n in names], **kwargs)
        w.update(zip(names, res[n_out:]))
        return res[:n_out]

    k_gain_t = jnp.tile(w["k_norm"].reshape(1, HEAD_DIM), (1, N_HEADS))
    q_gain_t = jnp.tile(w["b_q_norm"].reshape(1, HEAD_DIM), (1, N_HEADS))

    uz, h_a = run(_a_in, "a_in", 2, x, w["a_norm"], w["a_w_in"])
    wg4 = w["a_w_group"].reshape(N_CHIPS, N_GROUPS, 64, GROUP_DIM)
    wa_out = w["a_w_out"].reshape(D_MODEL, D_MODEL)
    gated_a, pooled = run(_a_mix, "a_mix", 2, uz, wg4, w["a_scale"])
    x1, x2, e_a, gate_a = run(_out_ple, "a_out_ple", 4, "a_out_ple", gated_a, x, wa_out, p, 0,
                              w["ple_w0"], w["ple_gate_w0"])
    h_kv, h_b, k_raw, q_raw, k, q, v, z_b = run(
        _b_in, "b_in", 8, x2, w["kv_norm"], w["b_norm"], k_gain_t, q_gain_t, w["w_kv"], w["b_w_in"])
    wb_out = w["b_w_out"].reshape(D_MODEL, D_MODEL)
    o, gated_b, ltot = _attn_fwd(q, k, v, z_b)
    x3, dx4, e_b, gate_b, loss_blk = _out_ple("b_out_ple", gated_b, x2, wb_out, p, 1, w["ple_w1"], w["ple_gate_w1"],
                                              target=target)

    grads, updates = {}, {}

    def hosted(fn, host, n_out, *args):
        if state is None:
            return fn(*args)
        names = REDUCE_AT[host]
        res = fn(*args, reduce=([grads.pop(n) for n in names], *[[t[n] for n in names] for t in state]))
        for i, n in enumerate(names):
            updates[n] = tuple(group[i] for group in res[n_out:])
        return res[:n_out]

    de_b, dgp_b, dx3, dgated_b = _ple_out_bwd("b_ple_out_bwd", dx4, e_b, gate_b, w["ple_gate_w1"], wb_out)
    grads["b_w_out"] = _wgrad("wgrad_b_out", gated_b, dx3, 1).reshape(N_CHIPS, 256, D_MODEL)
    grads["ple_w1"] = _wgrad_ple("wgrad_ple1", p, 1, de_b)
    grads["ple_gate_w1"] = _wgrad("wgrad_gate1", x3, dgp_b, 1).reshape(N_CHIPS, 256, D_MODEL)
    dq, dk, dv, dz_b = hosted(_attn_bwd, "attn_bwd", 4, q, k, v, ltot, dgated_b, o, z_b)
    dqz, dkv, dx2, small_b = _b_in_bwd(dq, dk, dv, dz_b, q_raw, k_raw, x2, dx3, q_gain_t, k_gain_t,
                                       w["b_norm"], w["kv_norm"], w["b_w_in"], w["w_kv"])
    grads["w_kv"] = _wgrad("wgrad_kv", h_kv, dkv, N_CHIPS)
    grads["b_w_in"] = _wgrad("wgrad_b_in", h_b, dqz, N_CHIPS)
    de_a, dgp_a, dx1, dgated_a = _ple_out_bwd("a_ple_out_bwd", dx2, e_a, gate_a, w["ple_gate_w0"], wa_out)
    grads["a_w_out"] = _wgrad("wgrad_a_out", gated_a, dx1, 1).reshape(N_CHIPS, 256, D_MODEL)
    grads["ple_w0"] = _wgrad_ple("wgrad_ple0", p, 0, de_a)
    grads["ple_gate_w0"] = _wgrad("wgrad_gate0", x1, dgp_a, 1).reshape(N_CHIPS, 256, D_MODEL)
    duz, dmr, grad_x, small_a = hosted(_a_mix_bwd, "a_mix_bwd", 4, dgated_a, uz, pooled, wg4, w["a_scale"],
                                       w["a_w_in"], x, dx1, w["a_norm"])
    grads["a_w_in"] = _wgrad("wgrad_a_in", h_a, duz, N_CHIPS)
    grads["a_w_group"] = _wgrad_group(pooled, dmr).reshape(N_CHIPS, N_GROUPS * 64, GROUP_DIM)

    fold = lambda row: jnp.pad(row.reshape(N_HEADS, HEAD_DIM).sum(axis=0), (0, D_MODEL - HEAD_DIM))
    small = jnp.stack([small_a[1], small_a[0], small_b[3], small_b[2], fold(small_b[1]), fold(small_b[0]),
                       jnp.pad(loss_blk[0], (0, D_MODEL - loss_blk.shape[1])), jnp.zeros((D_MODEL,), F32)])
    return grad_x, grads, updates, small


def _mesh_place():
    x, y, c = lax.axis_index("x"), lax.axis_index("y"), lax.axis_index("c")
    other_chips = [(1 - x, y), (x, 1 - y), (1 - x, 1 - y)]
    return x, y, c, other_chips


def _gather_sems(n):
    return [pltpu.SemaphoreType.DMA((3 * n,)), pltpu.SemaphoreType.DMA((3 * n,)),
            pltpu.SemaphoreType.DMA((3 * n,)), pltpu.SemaphoreType.DMA((3 * n,)), pltpu.SemaphoreType.DMA((n,))]


def _gather_copies(srcs, outs, sems):
    send_far, recv_far, send_sib, recv_sib, local_sem = sems
    n = len(srcs)
    x, y, c, chips = _mesh_place()
    me = 2 * x + y
    sibling = (x, y, 1 - c)

    def half(k, which):
        rows = srcs[k].shape[0] // 2
        return pl.ds(pl.multiple_of(which * rows, 16), rows)

    local = [pltpu.make_async_copy(srcs[k], outs[k].at[me], local_sem.at[k]) for k in range(n)]
    far = [pltpu.make_async_remote_copy(
        src_ref=srcs[k].at[half(k, c)], dst_ref=outs[k].at[me, half(k, c)],
        send_sem=send_far.at[j * n + k], recv_sem=recv_far.at[j * n + k], device_id=(px, py, c), device_id_type=MESH)
        for j, (px, py) in enumerate(chips) for k in range(n)]

    def landed(j, k, which, from_far):
        px, py = chips[j]
        piece = outs[k].at[2 * px + py, half(k, which)]
        send, recv = (send_far, recv_far) if from_far else (send_sib, recv_sib)
        return pltpu.make_async_remote_copy(src_ref=piece, dst_ref=piece, send_sem=send.at[j * n + k],
                                            recv_sem=recv.at[j * n + k], device_id=sibling, device_id_type=MESH)

    return local, far, landed, c


def _gather_start(srcs, outs, sems):
    local, far, _, _ = _gather_copies(srcs, outs, sems)
    for cp in local + far:
        cp.start()


def _gather_finish(srcs, outs, sems):
    local, far, landed, c = _gather_copies(srcs, outs, sems)
    pairs = [(j, k) for j in range(3) for k in range(len(srcs))]
    passed = []
    for j, k in pairs:
        landed(j, k, c, True).wait_recv()
        passed.append(landed(j, k, c, False))
        passed[-1].start()
    for j, k in pairs:
        landed(j, k, 1 - c, False).wait_recv()
    for cp in far + passed:
        cp.wait_send()
    for cp in local:
        cp.wait()


def _call_with_gather(body, *, name, grid, in_specs, out_specs, out_shape, args, gather=(), reduce=None,
                      scratch_shapes=(), vmem_mib=48):
    n_in, n_out, n_scr, n_g = len(args), len(out_shape), len(scratch_shapes), len(gather)
    n_r = len(reduce[0]) if reduce else 0
    pieces = _reduce_pieces(reduce[0]) if reduce else []
    reduce_args = [a for group in reduce for a in group] if reduce else []
    gather_sems = _gather_sems(n_g) if n_g else []
    n_steps = 1
    for g in grid:
        n_steps *= g

    def wrapped(*refs):
        refs = list(refs)
        take = lambda count: [refs.pop(0) for _ in range(count)]
        ins, g_in, r_in = take(n_in), take(n_g), take(4 * n_r)
        outs, g_out, r_out = take(n_out), take(n_g), take(4 * n_r)
        scratch, sems, r_scratch = take(n_scr), take(len(gather_sems)), refs
        step = 0
        for axis, g in enumerate(grid):
            step = step * g + pl.program_id(axis)
        if n_g:
            @pl.when(step == 0)
            def _():
                _gather_start(g_in, g_out, sems)

        if n_r:
            ticks, drain = _reduce_ticks(pieces, n_r, (*r_in, *r_out, *r_scratch))
            per_step = -(-len(ticks) // n_steps)
            for t, tick in enumerate(ticks):
                pl.when(step == t // per_step)(tick)

        body(*ins, *outs, *scratch)
        if n_r:
            pl.when(step == n_steps - 1)(drain)
        if n_g:
            @pl.when(step == n_steps - 1)
            def _():
                _gather_finish(g_in, g_out, sems)

    hbm = pl.BlockSpec(memory_space=pltpu.HBM)
    res = pl.pallas_call(
        wrapped, name=name, grid=grid,
        in_specs=list(in_specs) + [hbm] * (n_g + 4 * n_r), out_specs=list(out_specs) + [hbm] * (n_g + 4 * n_r),
        out_shape=list(out_shape) + [jax.ShapeDtypeStruct((N_CHIPS,) + g.shape, BF16) for g in gather]
        + ([jax.ShapeDtypeStruct(w.shape, F32) for _ in range(4) for w in reduce[1]] if reduce else []),
        scratch_shapes=list(scratch_shapes) + gather_sems + (_reduce_scratch() if reduce else []),
        compiler_params=_params(("arbitrary",) * len(grid), vmem_mib),
    )(*args, *gather, *reduce_args)
    if not reduce:
        return res
    plain = list(res[:n_out + n_g])
    return plain + [res[n_out + n_g + i * n_r:n_out + n_g + (i + 1) * n_r] for i in range(4)]


def _allgather_weights(shards, small, casts):
    n = len(shards)
    cast_out = [(k, r0, r1) for k, (_, ranges) in enumerate(casts) for r0, r1 in ranges]
    n_c, n_co = len(casts), len(cast_out)

    def body(*refs):
        ins, small_in, cast_in = refs[:n], refs[n], refs[n + 1:n + 1 + n_c]
        refs = refs[n + 1 + n_c:]
        outs, small_out, cast_dst = refs[:n], refs[n], refs[n + 1:n + 1 + n_co]
        refs = refs[n + 1 + n_co:]
        cast, cast_buf = refs[:n], refs[n:n + n_co]
        send_far, recv_far, send_sib, recv_sib, send_small, recv_small, local_sem, cast_sem = refs[n + n_co:]
        x, y, c, chips = _mesh_place()
        me = 2 * x + y
        sibling = (x, y, 1 - c)

        def half(k, which):
            rows = ins[k].shape[0] // 2
            return pl.ds(pl.multiple_of(which * rows, 16), rows)

        local = []
        for k in range(n):
            cast[k][...] = ins[k][...].astype(BF16)
            local.append(pltpu.make_async_copy(cast[k], outs[k].at[me], local_sem.at[k]))
            local[-1].start()
        local.append(pltpu.make_async_copy(small_in, small_out.at[me], local_sem.at[n]))
        local[-1].start()

        sends = []
        for j, (px, py) in enumerate(chips):
            for k in range(n):
                cp = pltpu.make_async_remote_copy(
                    src_ref=cast[k].at[half(k, c)], dst_ref=outs[k].at[me, half(k, c)],
                    send_sem=send_far.at[j * n + k], recv_sem=recv_far.at[j * n + k],
                    device_id=(px, py, c), device_id_type=MESH)
                cp.start()
                sends.append(cp)
            cp = pltpu.make_async_remote_copy(
                src_ref=small_in, dst_ref=small_out.at[me], send_sem=send_small.at[j], recv_sem=recv_small.at[j],
                device_id=(px, py, c), device_id_type=MESH)
            cp.start()
            sends.append(cp)

        for i, (k, r0, r1) in enumerate(cast_out):
            cast_buf[i][...] = cast_in[k][r0:r1, :].astype(BF16)
            local.append(pltpu.make_async_copy(cast_buf[i], cast_dst[i], cast_sem.at[i]))
            local[-1].start()

        def landed(j, k, which, sems_s, sems_r, device):
            px, py = chips[j]
            piece = outs[k].at[2 * px + py, half(k, which)]
            return pltpu.make_async_remote_copy(
                src_ref=piece, dst_ref=piece, send_sem=sems_s.at[j * n + k], recv_sem=sems_r.at[j * n + k],
                device_id=device, device_id_type=MESH)

        for j in range(len(chips)):
            for k in range(n):
                landed(j, k, c, send_far, recv_far, sibling).wait_recv()
                cp = landed(j, k, c, send_sib, recv_sib, sibling)
                cp.start()
                sends.append(cp)
        for j, (px, py) in enumerate(chips):
            for k in range(n):
                landed(j, k, 1 - c, send_sib, recv_sib, sibling).wait_recv()
            pltpu.make_async_remote_copy(
                src_ref=small_in, dst_ref=small_out.at[2 * px + py], send_sem=send_small.at[j],
                recv_sem=recv_small.at[j], device_id=(px, py, c), device_id_type=MESH).wait_recv()
        for cp in sends:
            cp.wait_send()
        for cp in local:
            cp.wait()

    vmem = pl.BlockSpec(memory_space=pltpu.VMEM)
    hbm = pl.BlockSpec(memory_space=pltpu.HBM)
    cast_shapes = [(r1 - r0, casts[k][0].shape[1]) for k, r0, r1 in cast_out]
    res = pl.pallas_call(
        body, name="allgather_weights",
        in_specs=[vmem] * (n + 1 + n_c), out_specs=[hbm] * (n + 1 + n_co),
        out_shape=[jax.ShapeDtypeStruct((N_CHIPS,) + s.shape, BF16) for s in shards]
        + [jax.ShapeDtypeStruct((N_CHIPS,) + small.shape, F32)]
        + [jax.ShapeDtypeStruct(s, BF16) for s in cast_shapes],
        scratch_shapes=[pltpu.VMEM(s.shape, BF16) for s in shards] + [pltpu.VMEM(s, BF16) for s in cast_shapes]
        + [pltpu.SemaphoreType.DMA((3 * n,)), pltpu.SemaphoreType.DMA((3 * n,)),
           pltpu.SemaphoreType.DMA((3 * n,)), pltpu.SemaphoreType.DMA((3 * n,)),
           pltpu.SemaphoreType.DMA((3,)), pltpu.SemaphoreType.DMA((3,)),
           pltpu.SemaphoreType.DMA((n + 1,)), pltpu.SemaphoreType.DMA((n_co,))],
        compiler_params=_params(None, 40),
    )(*shards, small, *[a for a, _ in casts])
    return res[:n], res[n], res[n + 1:]


def _adamw(w, g, m, v):
    m = ADAM_B1 * m + (1.0 - ADAM_B1) * g
    v = ADAM_B2 * v + (1.0 - ADAM_B2) * (g * g)
    m_hat = m / (1.0 - ADAM_B1 ** ADAM_STEP)
    v_hat = v / (1.0 - ADAM_B2 ** ADAM_STEP)
    delta = -ADAM_LR * (m_hat / (jnp.sqrt(v_hat) + ADAM_EPS) + ADAM_WD * w)
    return delta, m, v


RS_PIECE_ROWS = 128
RS_PIECE_COLS = 512


def _reduce_adam_all(grads, ws, ms, vs):
    n_w = len(grads)
    pieces = _reduce_pieces(grads)

    def body(*refs):
        ticks, drain = _reduce_ticks(pieces, n_w, refs)
        for tick in ticks:
            tick()
        drain()

    hbm = pl.BlockSpec(memory_space=pltpu.HBM)
    outs = pl.pallas_call(
        body, name="reduce_adam_all",
        in_specs=[hbm] * (4 * n_w), out_specs=[hbm] * (4 * n_w),
        out_shape=[jax.ShapeDtypeStruct(w.shape, F32) for _ in range(4) for w in ws],
        scratch_shapes=_reduce_scratch(),
        compiler_params=_params(None, 48),
    )(*grads, *ws, *ms, *vs)
    return [outs[i * n_w:(i + 1) * n_w] for i in range(4)]


def _reduce_pieces(grads):
    pieces = []
    for k, g in enumerate(grads):
        hr, cols = g.shape[1] // 2, g.shape[2]
        pr, pc = min(hr, RS_PIECE_ROWS), min(cols, RS_PIECE_COLS)
        pieces += [(k, ro, hr, co, pr, pc) for ro in range(0, hr, pr) for co in range(0, cols, pc)]
    return pieces


def _reduce_scratch():
    P, C = RS_PIECE_ROWS, RS_PIECE_COLS
    return [
        pltpu.VMEM((3, N_CHIPS, P, C), F32), pltpu.VMEM((3, N_CHIPS, P, C), F32),
        pltpu.VMEM((2, N_CHIPS, P, C), BF16), pltpu.VMEM((2, N_CHIPS, P, C), BF16),
        pltpu.VMEM((2, N_CHIPS, P, C), F32),
        pltpu.VMEM((2, 3, P, C), BF16), pltpu.VMEM((2, 3, P, C), BF16),
        pltpu.VMEM((2, 2, P, C), F32),
        pltpu.VMEM((2, 3, 2, P, C), F32), pltpu.VMEM((2, 4, 2, P, C), F32),
        pltpu.SemaphoreType.DMA((3, 2)), pltpu.SemaphoreType.DMA((2, 3, 2)),
        pltpu.SemaphoreType.DMA((2,)), pltpu.SemaphoreType.DMA((2,)),
        pltpu.SemaphoreType.DMA((2, 3)), pltpu.SemaphoreType.DMA((2, 3)),
        pltpu.SemaphoreType.DMA((2,)), pltpu.SemaphoreType.DMA((2,)),
        pltpu.SemaphoreType.DMA((2, 4, 2))]


def _reduce_ticks(pieces, n_w, refs):
    n = len(pieces)

    def build(*refs):
        g_in, w_in, m_in, v_in = (refs[i * n_w:(i + 1) * n_w] for i in range(4))
        g_out, d_out, m_out, v_out = (refs[(4 + i) * n_w:(5 + i) * n_w] for i in range(4))
        (gm, go, sb1, rb1, part, sb2, rb2, fin, wmv, outs,
         ld_sem, wmv_sem, s1_send, s1_recv, s2_send, s2_recv, s3_send, s3_recv, out_sem) = refs[8 * n_w:]
        x, y, c, chips = _mesh_place()
        me = 2 * x + y
        sibling = (x, y, 1 - c)

        def at_hbm(i, which):
            _, ro, hr, co, pr, pc = pieces[i]
            half = c if which == 0 else 1 - c
            return pl.ds(pl.multiple_of(half * hr + ro, 64), pr), pl.ds(co, pc)

        def win(i):
            return pl.ds(0, pieces[i][4]), pl.ds(0, pieces[i][5])

        every = slice(None)

        def loads(i):
            k, s = pieces[i][0], i % 3
            return [pltpu.make_async_copy(g_in[k].at[(every,) + at_hbm(i, h)], buf.at[(s, every) + win(i)], ld_sem.at[s, h])
                    for h, buf in enumerate((gm, go))]

        def wmv_loads(i):
            k, s = pieces[i][0], i % 2
            return [pltpu.make_async_copy(src[k].at[at_hbm(i, h)], wmv.at[(s, a, h) + win(i)], wmv_sem.at[s, a, h])
                    for a, src in enumerate((w_in, m_in, v_in)) for h in range(2)]

        def stores(i):
            k, s = pieces[i][0], i % 2
            return [pltpu.make_async_copy(outs.at[(s, a, h) + win(i)], dst[k].at[at_hbm(i, h)], out_sem.at[s, a, h])
                    for a, dst in enumerate((g_out, d_out, m_out, v_out)) for h in range(2)]

        def swap1(i):
            s = i % 2
            return pltpu.make_async_remote_copy(
                src_ref=sb1.at[(s, every) + win(i)], dst_ref=rb1.at[(s, every) + win(i)],
                send_sem=s1_send.at[s], recv_sem=s1_recv.at[s], device_id=sibling, device_id_type=MESH)

        def far2(i, j):
            s = i % 2
            px, py = chips[j]
            return pltpu.make_async_remote_copy(
                src_ref=sb2.at[(s, j) + win(i)], dst_ref=rb2.at[(s, j) + win(i)],
                send_sem=s2_send.at[s, j], recv_sem=s2_recv.at[s, j], device_id=(px, py, c), device_id_type=MESH)

        def swap3(i):
            s = i % 2
            return pltpu.make_async_remote_copy(
                src_ref=fin.at[(s, 0) + win(i)], dst_ref=fin.at[(s, 1) + win(i)],
                send_sem=s3_send.at[s], recv_sem=s3_recv.at[s], device_id=sibling, device_id_type=MESH)

        def stage0(i):
            for cp in loads(i):
                cp.start()

        def stage1(i):
            s, s3 = i % 2, i % 3
            for cp in loads(i):
                cp.wait()
            sb1[(s, every) + win(i)] = go[(s3, every) + win(i)].astype(BF16)
            swap1(i).start()

        def stage2(i):
            s, s3 = i % 2, i % 3
            swap1(i).wait()
            part[(s, every) + win(i)] = gm[(s3, every) + win(i)] + rb1[(s, every) + win(i)].astype(F32)
            for j, (px, py) in enumerate(chips):
                sb2[(s, j) + win(i)] = part[(s, 2 * px + py) + win(i)].astype(BF16)
                far2(i, j).start()

        def stage3(i):
            s = i % 2
            total = part[(s, me) + win(i)]
            for j in range(3):
                far2(i, j).wait()
                total = total + rb2[(s, j) + win(i)].astype(F32)
            fin[(s, 0) + win(i)] = total
            swap3(i).start()
            for cp in wmv_loads(i):
                cp.start()

        def stage4(i):
            s = i % 2
            if i >= 2:
                for cp in stores(i - 2):
                    cp.wait()
            swap3(i).wait()
            for cp in wmv_loads(i):
                cp.wait()
            both = (every,) + win(i)
            g = fin[(s,) + both]
            delta, m_new, v_new = _adamw(wmv[(s, 0) + both], g, wmv[(s, 1) + both], wmv[(s, 2) + both])
            outs[(s, 0) + both] = g
            outs[(s, 1) + both] = delta
            outs[(s, 2) + both] = m_new
            outs[(s, 3) + both] = v_new
            for cp in stores(i):
                cp.start()

        stages = (stage0, stage1, stage2, stage3, stage4)

        def tick(t):
            for age in reversed(range(len(stages))):
                if 0 <= t - age < n:
                    stages[age](t - age)

        def drain():
            for i in range(max(0, n - 2), n):
                for cp in stores(i):
                    cp.wait()

        return [functools.partial(tick, t) for t in range(n + len(stages) - 1)], drain

    return build(*refs)


def _allreduce_small(part):
    n_dev = 8

    def body(part_ref, out_ref, buf, send_sem, recv_sem):
        x, y, c, _ = _mesh_place()
        me = 4 * x + 2 * y + c
        buf[me] = part_ref[...]
        sends = []
        for k in range(1, n_dev):
            peer = ((1 - x) if k & 4 else x, (1 - y) if k & 2 else y, (1 - c) if k & 1 else c)
            cp = pltpu.make_async_remote_copy(src_ref=part_ref, dst_ref=buf.at[me], send_sem=send_sem.at[k - 1],
                                              recv_sem=recv_sem.at[k - 1], device_id=peer, device_id_type=MESH)
            cp.start()
            sends.append(cp)
        for cp in sends:
            cp.wait_recv()
        total = buf[0]
        for s in range(1, n_dev):
            total = total + buf[s]
        out_ref[...] = total
        for cp in sends:
            cp.wait_send()

    vmem = pl.BlockSpec(memory_space=pltpu.VMEM)
    return pl.pallas_call(
        body, name="allreduce_small", in_specs=[vmem], out_specs=vmem,
        out_shape=jax.ShapeDtypeStruct(part.shape, F32),
        scratch_shapes=[pltpu.VMEM((n_dev,) + part.shape, F32),
                        pltpu.SemaphoreType.DMA((n_dev - 1,)), pltpu.SemaphoreType.DMA((n_dev - 1,))],
    )(part)


def _adam_small(w, g, m, v):
    def body(w_ref, g_ref, m_ref, v_ref, d_ref, mo_ref, vo_ref):
        delta, m_new, v_new = _adamw(w_ref[...], g_ref[...], m_ref[...], v_ref[...])
        d_ref[...] = delta
        mo_ref[...] = m_new
        vo_ref[...] = v_new

    vmem = pl.BlockSpec(memory_space=pltpu.VMEM)
    return pl.pallas_call(
        body, name="adam_small", in_specs=[vmem] * 4, out_specs=[vmem] * 3,
        out_shape=[jax.ShapeDtypeStruct(w.shape, F32)] * 3,
    )(w, g, m, v)


BIG = ("a_w_in", "a_w_group", "a_w_out", "w_kv", "b_w_in", "b_w_out", "ple_w", "ple_gate_w")
SMALL = ("a_norm", "a_scale", "kv_norm", "b_norm", "k_norm", "b_q_norm")
SMALL_SHARDED = ("a_norm", "a_scale")
WEIGHTS = ("a_norm", "a_w_in", "a_w_group", "a_scale", "a_w_out", "kv_norm", "w_kv", "k_norm", "b_norm", "b_w_in",
           "b_q_norm", "b_w_out", "ple_w", "ple_gate_w")


def _as_matrix(a):
    return a.reshape(-1, a.shape[-1])


def _pack_small(arrs):
    rows = [jnp.pad(a.reshape(-1), (0, D_MODEL - a.size)) for a in arrs]
    rows += [jnp.zeros((D_MODEL,), F32)] * (8 - len(rows))
    return jnp.stack(rows)


def kernel(x, p, a_norm, a_w_in, a_w_group, a_scale, a_w_out, kv_norm, w_kv, k_norm, b_norm, b_w_in, b_q_norm, b_w_out, ple_w, ple_gate_w, loss_target, m_a_norm, m_a_w_in, m_a_w_group, m_a_scale, m_a_w_out, m_kv_norm, m_w_kv, m_k_norm, m_b_norm, m_b_w_in, m_b_q_norm, m_b_w_out, m_ple_w, m_ple_gate_w, v_a_norm, v_a_w_in, v_a_w_group, v_a_scale, v_a_w_out, v_kv_norm, v_w_kv, v_k_norm, v_b_norm, v_b_w_in, v_b_q_norm, v_b_w_out, v_ple_w, v_ple_gate_w):
    wts = dict(a_norm=a_norm, a_w_in=a_w_in, a_w_group=a_w_group, a_scale=a_scale, a_w_out=a_w_out, kv_norm=kv_norm,
               w_kv=w_kv, k_norm=k_norm, b_norm=b_norm, b_w_in=b_w_in, b_q_norm=b_q_norm, b_w_out=b_w_out,
               ple_w=ple_w, ple_gate_w=ple_gate_w)
    mom = dict(a_norm=m_a_norm, a_w_in=m_a_w_in, a_w_group=m_a_w_group, a_scale=m_a_scale, a_w_out=m_a_w_out,
               kv_norm=m_kv_norm, w_kv=m_w_kv, k_norm=m_k_norm, b_norm=m_b_norm, b_w_in=m_b_w_in,
               b_q_norm=m_b_q_norm, b_w_out=m_b_w_out, ple_w=m_ple_w, ple_gate_w=m_ple_gate_w)
    var = dict(a_norm=v_a_norm, a_w_in=v_a_w_in, a_w_group=v_a_w_group, a_scale=v_a_scale, a_w_out=v_a_w_out,
               kv_norm=v_kv_norm, w_kv=v_w_kv, k_norm=v_k_norm, b_norm=v_b_norm, b_w_in=v_b_w_in,
               b_q_norm=v_b_q_norm, b_w_out=v_b_w_out, ple_w=v_ple_w, ple_gate_w=v_ple_gate_w)
    S = x.shape[1]
    chip = 2 * lax.axis_index("x") + lax.axis_index("y")

    sharded_small = jnp.concatenate([a_norm.reshape(1, 256), a_scale.reshape(1, 256), jnp.zeros((6, 256), F32)], axis=0)
    later = ("a_w_group", "a_w_out", "w_kv", "b_w_in", "b_w_out", "ple_w", "ple_gate_w")
    (a_w_in_full,), small_full, copies = _allgather_weights(
        [_as_matrix(a_w_in)], sharded_small,
        [(_as_matrix(wts[n]), [(0, 256), (256, 512)] if n.startswith("ple") else [(0, _as_matrix(wts[n]).shape[0])])
         for n in later])
    local = dict(zip(("a_w_group", "a_w_out", "w_kv", "b_w_in", "b_w_out", "ple_w0", "ple_w1", "ple_gate_w0",
                      "ple_gate_w1"), copies))
    full = dict(a_w_in=a_w_in_full,
                a_norm=small_full[:, 0, :].reshape(1, D_MODEL), a_scale=small_full[:, 1, :].reshape(1, D_MODEL),
                kv_norm=kv_norm.reshape(1, D_MODEL), b_norm=b_norm.reshape(1, D_MODEL), k_norm=k_norm, b_q_norm=b_q_norm)

    def shards(t):
        out = {}
        for n in BIG:
            mat = _as_matrix(t[n])
            if n.startswith("ple"):
                out[n + "0"], out[n + "1"] = mat[:256], mat[256:]
            else:
                out[n] = mat
        return out

    state = (shards(wts), shards(mom), shards(var))
    grad_x, grads, updates, small_part = _local_step(x.reshape(S, D_MODEL), p, loss_target.reshape(S, D_MODEL),
                                                     full, local, state)

    names = sorted(grads)
    reduced = _reduce_adam_all([grads[n] for n in names], *[[t[n] for n in names] for t in state])
    for i, n in enumerate(names):
        updates[n] = tuple(group[i] for group in reduced)
    out_g, out_d, out_m, out_v = {}, {}, {}, {}
    for n in BIG:
        for i, out in enumerate((out_g, out_d, out_m, out_v)):
            if n.startswith("ple"):
                out[n] = jnp.stack([updates[n + "0"][i], updates[n + "1"][i]]).reshape(wts[n].shape)
            else:
                out[n] = updates[n][i].reshape(wts[n].shape)

    small_sum = _allreduce_small(small_part)
    loss = small_sum[len(SMALL), 0]
    small_rows = []
    for i, n in enumerate(SMALL):
        row = small_sum[i]
        if n in SMALL_SHARDED:
            row = lax.dynamic_slice(row, (chip * 256,), (256,))
        else:
            row = row[:wts[n].size]
        small_rows.append(row)
    g_small = _pack_small(small_rows)
    d_small, m_small, v_small = _adam_small(_pack_small([wts[n] for n in SMALL]), g_small,
                                            _pack_small([mom[n] for n in SMALL]), _pack_small([var[n] for n in SMALL]))
    for i, n in enumerate(SMALL):
        shape, size = wts[n].shape, wts[n].size
        out_g[n], out_d[n], out_m[n], out_v[n] = (t[i, :size].reshape(shape) for t in (g_small, d_small, m_small, v_small))

    return (loss, grad_x.reshape(1, S, D_MODEL), *[out_g[n] for n in WEIGHTS], *[out_d[n] for n in WEIGHTS],
            *[out_m[n] for n in WEIGHTS], *[out_v[n] for n in WEIGHTS])
```

```python
import functools

import jax
import jax.numpy as jnp
from jax import lax
from jax.experimental import pallas as pl
from jax.experimental.pallas import tpu as pltpu

F32 = jnp.float32
BF16 = jnp.bfloat16
MESH = pl.DeviceIdType.MESH

D_MODEL = 1024
N_HEADS = 16
HEAD_DIM = 64
PLE_DIM = 256
N_GROUPS = 4
GROUP_DIM = 256
POOL_WINDOWS = (2, 4, 8, 16)
N_CHIPS = 4
EPS = 1e-6
SB_SCALE = HEAD_DIM ** -0.5

ADAM_LR = 0.001
ADAM_B1 = 0.9
ADAM_B2 = 0.999
ADAM_EPS = 1e-08
ADAM_WD = 0.01
ADAM_STEP = 10

ROW_TILE = 256
ATT_Q_TILE = 512
ATT_K_TILE = 256
WGRAD_SEQ_TILE = 1024
WGRAD_ACC_BYTES = 4 * 1024 * 1024
MIB = 1024 * 1024


def _params(semantics=None, vmem_mib=48):
    return pltpu.CompilerParams(dimension_semantics=semantics, vmem_limit_bytes=vmem_mib * MIB)


def _dot(a, b):
    return jnp.dot(a, b, preferred_element_type=F32)


def _dot_nt(a, b):
    return lax.dot_general(a, b, (((1,), (1,)), ((), ())), preferred_element_type=F32)


def _dot_tn(a, b):
    return lax.dot_general(a, b, (((0,), (0,)), ((), ())), preferred_element_type=F32)


def _hilo(x):
    hi = x.astype(BF16)
    lo = (x - hi.astype(F32)).astype(BF16)
    return hi, lo


def _dot_hilo(x, w):
    hi, lo = _hilo(x)
    return _dot(hi, w) + _dot(lo, w)


def _sigmoid(z):
    return jax.nn.sigmoid(z)


def _dsilu(z, sg):
    return sg * (1.0 + z * (1.0 - sg))


def _mask_bf16(cond):
    return jnp.where(cond, 1.0, 0.0).astype(BF16)


def _head_mean_matrix():
    r = lax.broadcasted_iota(jnp.int32, (256, 256), 0) // HEAD_DIM
    c = lax.broadcasted_iota(jnp.int32, (256, 256), 1) // HEAD_DIM
    return _mask_bf16(r == c)


def _head_mean(x, bd):
    parts = []
    for s in range(x.shape[1] // 256):
        parts.append(_dot_hilo(x[:, s * 256:(s + 1) * 256], bd))
    out = parts[0] if len(parts) == 1 else jnp.concatenate(parts, axis=1)
    return out * (1.0 / HEAD_DIM)


def _a_in(x, gain, w_sh, gather=()):
    S = x.shape[0]
    tm = 512
    nsh, _, wn = w_sh.shape

    def body(x_ref, g_ref, w_ref, uz_ref, h_ref):
        @pl.when(pl.program_id(1) == 0)
        def _():
            xv = x_ref[...]
            r = lax.rsqrt(jnp.mean(xv * xv, axis=-1, keepdims=True) + EPS)
            h_ref[...] = (xv * r * g_ref[...]).astype(BF16)

        uz_ref[...] = _dot(h_ref[...], w_ref[0])

    return _call_with_gather(
        body, name="a_in", grid=(S // tm, nsh),
        in_specs=[pl.BlockSpec((tm, D_MODEL), lambda i, j: (i, 0)),
                  pl.BlockSpec((1, D_MODEL), lambda i, j: (0, 0)),
                  pl.BlockSpec((1, D_MODEL, wn), lambda i, j: (j, 0, 0))],
        out_specs=[pl.BlockSpec((tm, wn), lambda i, j: (i, j)),
                   pl.BlockSpec((tm, D_MODEL), lambda i, j: (i, 0))],
        out_shape=[jax.ShapeDtypeStruct((S, nsh * wn), F32),
                   jax.ShapeDtypeStruct((S, D_MODEL), BF16)],
        args=(x, gain, w_sh), gather=gather)


def _inv_count(first_row, rows, w):
    t1 = first_row + 1 + lax.broadcasted_iota(jnp.int32, (rows, 1), 0)
    return 1.0 / jnp.minimum(t1, w).astype(F32)


def _group_weight(wg_ref, g):
    return jnp.concatenate([wg_ref[sh, g] for sh in range(N_CHIPS)], axis=0)


def _a_mix(uz, wg, scale, gather=()):
    S = uz.shape[0]
    tm = ROW_TILE

    def body(u_ref, up_ref, z_ref, wg_ref, sc_ref, ga_ref, p_ref):
        i = pl.program_id(0)
        row = lax.broadcasted_iota(jnp.int32, (tm, tm), 0)
        col = lax.broadcasted_iota(jnp.int32, (tm, tm), 1)
        d = row - col
        for g, w in enumerate(POOL_WINDOWS):
            cols = slice(g * GROUP_DIM, (g + 1) * GROUP_DIM)
            t_main = _mask_bf16((d >= 0) & (d < w))
            t_halo = _mask_bf16(d + tm < w)
            u = u_ref[:, cols]
            up = jnp.where(i > 0, up_ref[:, cols], 0.0)
            hi, lo = _hilo(u)
            hip, lop = _hilo(up)
            wsum = _dot(t_main, hi) + _dot(t_main, lo) + _dot(t_halo, hip) + _dot(t_halo, lop)
            pooled = (wsum * _inv_count(i * tm, tm, w) - u).astype(BF16)
            p_ref[:, cols] = pooled
            mraw = _dot(pooled, _group_weight(wg_ref, g))
            z = z_ref[:, cols]
            ga_ref[:, cols] = (mraw * sc_ref[:, cols] * (z * _sigmoid(z))).astype(BF16)

    return _call_with_gather(
        body, name="a_mix", grid=(S // tm,),
        in_specs=[pl.BlockSpec((tm, D_MODEL), lambda i: (i, 0)),
                  pl.BlockSpec((tm, D_MODEL), lambda i: (jnp.maximum(i - 1, 0), 0)),
                  pl.BlockSpec((tm, D_MODEL), lambda i: (i, 1)),
                  pl.BlockSpec((N_CHIPS, N_GROUPS, 64, GROUP_DIM), lambda i: (0, 0, 0, 0)),
                  pl.BlockSpec((1, D_MODEL), lambda i: (0, 0))],
        out_specs=[pl.BlockSpec((tm, D_MODEL), lambda i: (i, 0)),
                   pl.BlockSpec((tm, D_MODEL), lambda i: (i, 0))],
        out_shape=[jax.ShapeDtypeStruct((S, D_MODEL), BF16),
                   jax.ShapeDtypeStruct((S, D_MODEL), BF16)],
        args=(uz, uz, uz, wg, scale), gather=gather)


def _out_ple(name, gated, x_in, w_out, p, layer, ple_w, ple_g, target=None, gather=()):
    S = x_in.shape[0]
    tm = ROW_TILE
    with_loss = target is not None

    def body(*refs):
        if with_loss:
            g_ref, x_ref, wo_ref, p_ref, pw_ref, pg_ref, t_ref, xm_ref, dx_ref, e_ref, gt_ref, loss_ref = refs
        else:
            g_ref, x_ref, wo_ref, p_ref, pw_ref, pg_ref, xm_ref, xo_ref, e_ref, gt_ref = refs
        xm = x_ref[...] + _dot(g_ref[...], wo_ref[...])
        xm_ref[...] = xm
        pb = p_ref[...].astype(BF16)
        e = jnp.concatenate([_dot(pb, pw_ref[sh]) for sh in range(N_CHIPS)], axis=1)
        pg = jnp.concatenate([pg_ref[sh] for sh in range(N_CHIPS)], axis=0)
        gate = _sigmoid(_dot(xm.astype(BF16), pg))
        e_ref[...] = e.astype(BF16)
        gt_ref[...] = gate.astype(BF16)
        xo = xm + e * gate
        if with_loss:
            diff = xo - t_ref[...]
            dx_ref[...] = diff * (1.0 / D_MODEL)

            @pl.when(pl.program_id(0) == 0)
            def _():
                loss_ref[...] = jnp.zeros_like(loss_ref)

            loss_ref[...] += jnp.sum(diff * diff) * (0.5 / D_MODEL)
        else:
            xo_ref[...] = xo

    row = pl.BlockSpec((tm, D_MODEL), lambda i: (i, 0))
    in_specs = [row, row,
                pl.BlockSpec((D_MODEL, D_MODEL), lambda i: (0, 0)),
                pl.BlockSpec((None, None, tm, PLE_DIM), lambda i: (layer, 0, i, 0)),
                pl.BlockSpec((N_CHIPS, PLE_DIM, 256), lambda i: (0, 0, 0)),
                pl.BlockSpec((N_CHIPS, 256, D_MODEL), lambda i: (0, 0, 0))]
    args = [gated, x_in, w_out, p, ple_w, ple_g]
    out_specs = [row, row, row, row]
    out_shape = [jax.ShapeDtypeStruct((S, D_MODEL), F32), jax.ShapeDtypeStruct((S, D_MODEL), F32),
                 jax.ShapeDtypeStruct((S, D_MODEL), BF16), jax.ShapeDtypeStruct((S, D_MODEL), BF16)]
    if with_loss:
        in_specs.append(row)
        args.append(target)
        out_specs.append(pl.BlockSpec((8, 128), lambda i: (0, 0)))
        out_shape.append(jax.ShapeDtypeStruct((8, 128), F32))
    return _call_with_gather(body, name=name, grid=(S // tm,), in_specs=in_specs, out_specs=out_specs,
                             out_shape=out_shape, args=args, gather=gather)


def _b_in(x, kv_gain, b_gain, k_gain_t, q_gain_t, w_kv, w_in, gather=()):
    S = x.shape[0]
    tm = ROW_TILE

    def body(x_ref, kvg_ref, bg_ref, kg_ref, qg_ref, wkv_ref, win_ref,
             hkv_ref, hb_ref, kraw_ref, qraw_ref, k_ref, q_ref, v_ref, z_ref):
        xv = x_ref[...]
        y = xv * lax.rsqrt(jnp.mean(xv * xv, axis=-1, keepdims=True) + EPS)
        hkv = (y * kvg_ref[...]).astype(BF16)
        hb = (y * bg_ref[...]).astype(BF16)
        hkv_ref[...] = hkv
        hb_ref[...] = hb
        bd = _head_mean_matrix()

        def head_norm(raw, gain):
            rr = lax.rsqrt(_head_mean(raw * raw, bd) + EPS)
            return raw * rr * gain

        for sh in range(N_CHIPS):
            kvc = _dot(hkv, wkv_ref[sh])
            qzc = _dot(hb, win_ref[sh])
            cols = slice((sh % 2) * 512, (sh % 2) * 512 + 512)
            if sh < 2:
                kraw_ref[:, cols] = kvc.astype(BF16)
                qraw_ref[:, cols] = qzc.astype(BF16)
                k_ref[:, cols] = head_norm(kvc, kg_ref[:, cols]).astype(BF16)
                q_ref[:, cols] = (head_norm(qzc, qg_ref[:, cols]) * SB_SCALE).astype(BF16)
            else:
                v_ref[:, cols] = kvc.astype(BF16)
                z_ref[:, cols] = qzc.astype(BF16)

    row = pl.BlockSpec((tm, D_MODEL), lambda i: (i, 0))
    vec = pl.BlockSpec((1, D_MODEL), lambda i: (0, 0))
    wsp = pl.BlockSpec((N_CHIPS, D_MODEL, 512), lambda i: (0, 0, 0))
    return _call_with_gather(
        body, name="b_in", grid=(S // tm,),
        in_specs=[row, vec, vec, vec, vec, wsp, wsp],
        out_specs=[row] * 8,
        out_shape=[jax.ShapeDtypeStruct((S, D_MODEL), BF16)] * 8,
        args=(x, kv_gain, b_gain, k_gain_t, q_gain_t, w_kv, w_in), gather=gather, vmem_mib=56)


def _softplus_parts(z):
    e = jnp.exp(-jnp.abs(z))
    return -(jnp.maximum(z, 0.0) + jnp.log(1.0 + e)), e


def _add_rows(total, first_row, update):
    if first_row == 0:
        return total + update
    return jnp.concatenate([total[:first_row], total[first_row:] + update], axis=0)


def _attn_fwd(q, k, v, zgate):
    S = q.shape[0]
    tq, tk = ATT_Q_TILE, ATT_K_TILE
    kpq = tq // tk

    def body(q_ref, k_ref, v_ref, z_ref, o_ref, g_ref, lt_ref):
        qi = pl.program_id(1)
        lane = lax.broadcasted_iota(jnp.int32, (1, 128), 1)
        ri = lax.broadcasted_iota(jnp.int32, (tk, tk), 0)
        ci = lax.broadcasted_iota(jnp.int32, (tk, tk), 1)
        later_mat = _mask_bf16(ri > ci)
        t_idx = qi * tq + lax.broadcasted_iota(jnp.int32, (tq, tk), 0)
        s_off = lax.broadcasted_iota(jnp.int32, (tq, tk), 1)
        qv = q_ref[...]
        first = lane < HEAD_DIM
        q_heads = (jnp.where(first, qv, jnp.zeros_like(qv)), jnp.where(first, jnp.zeros_like(qv), qv))

        def step(kj_last, carry, masked):
            chains = [(d, h) for d in range(kpq) for h in range(2)]
            r0 = [(kpq - 1 - d) * tk if masked else 0 for d in range(kpq)]
            s0 = [pl.multiple_of((kj_last - d) * tk, tk) for d in range(kpq)]
            kb = [k_ref[pl.ds(s, tk), :] for s in s0]
            vb = [v_ref[pl.ds(s, tk), :] for s in s0]
            visible = [(s + s_off < t_idx)[r:] for s, r in zip(s0, r0)] if masked else None
            z = {c: _dot_nt(q_heads[c[1]][r0[c[0]]:], kb[c[0]]) for c in chains}
            run = [carry[0], carry[2]]
            log_own, later, run_at = {}, {}, {}
            for c in chains:
                kj, h = c
                lk = _softplus_parts(z[c])[0]
                if masked:
                    lk = jnp.where(visible[kj], lk, 0.0)
                log_own[c] = z[c] + lk
                later[c] = _dot(lk.astype(BF16), later_mat)
                run_at[c] = run[h][r0[kj]:]
                run[h] = _add_rows(run[h], r0[kj], jnp.sum(lk, axis=-1, keepdims=True))
            acc = [carry[1], carry[3]]
            for c in chains:
                kj, h = c
                a = jnp.exp(log_own[c] + later[c] + run_at[c])
                if masked:
                    a = jnp.where(visible[kj], a, 0.0)
                acc[h] = _add_rows(acc[h], r0[kj], _dot(a.astype(BF16), vb[kj]))
            return run[0], acc[0], run[1], acc[1]

        zero1, zero128 = jnp.zeros((tq, 1), F32), jnp.zeros((tq, 128), F32)
        carry = step(qi * kpq + kpq - 1, (zero1, zero128, zero1, zero128), True)
        carry = lax.fori_loop(0, qi, lambda n, c: step((qi - n) * kpq - 1, c, False), carry)
        o_tot = jnp.where(first, carry[1], carry[3])
        l_tot = jnp.where(first, carry[0], carry[2])
        o_ref[...] = o_tot.astype(BF16)
        lt_ref[...] = l_tot
        zz = z_ref[...].astype(F32)
        g_ref[...] = (o_tot * (zz * _sigmoid(zz))).astype(BF16)

    blk = pl.BlockSpec((tq, 128), lambda hp, qi: (qi, hp))
    seq = pl.BlockSpec((S, 128), lambda hp, qi: (0, hp))
    return pl.pallas_call(
        body, name="attn_fwd", grid=(D_MODEL // 128, S // tq),
        in_specs=[blk, seq, seq, blk], out_specs=[blk, blk, blk],
        out_shape=[jax.ShapeDtypeStruct((S, D_MODEL), BF16)] * 2 + [jax.ShapeDtypeStruct((S, D_MODEL), F32)],
        compiler_params=_params(("parallel", "arbitrary")),
    )(q, k, v, zgate)


def _ple_out_bwd(name, dx_out, e, gate, ple_g, w_out):
    S = dx_out.shape[0]
    tm = ROW_TILE

    def body(dx_ref, e_ref, gt_ref, pg_ref, wo_ref, de_ref, dgp_ref, dxm_ref, dg_ref):
        dxo = dx_ref[...]
        ev = e_ref[...].astype(F32)
        gv = gt_ref[...].astype(F32)
        de_ref[...] = (dxo * gv).astype(BF16)
        dgp = (dxo * ev * gv * (1.0 - gv)).astype(BF16)
        dgp_ref[...] = dgp
        pg = jnp.concatenate([pg_ref[sh] for sh in range(N_CHIPS)], axis=0)
        dxm = dxo + _dot_nt(dgp, pg)
        dxm_ref[...] = dxm
        dg_ref[...] = _dot_nt(dxm.astype(BF16), wo_ref[...]).astype(BF16)

    row = pl.BlockSpec((tm, D_MODEL), lambda i: (i, 0))
    return pl.pallas_call(
        body, name=name, grid=(S // tm,),
        in_specs=[row, row, row,
                  pl.BlockSpec((N_CHIPS, 256, D_MODEL), lambda i: (0, 0, 0)),
                  pl.BlockSpec((D_MODEL, D_MODEL), lambda i: (0, 0))],
        out_specs=[row, row, row, row],
        out_shape=[jax.ShapeDtypeStruct((S, D_MODEL), BF16), jax.ShapeDtypeStruct((S, D_MODEL), BF16),
                   jax.ShapeDtypeStruct((S, D_MODEL), F32), jax.ShapeDtypeStruct((S, D_MODEL), BF16)],
        compiler_params=_params(("arbitrary",)),
    )(dx_out, e, gate, ple_g, w_out)


def _attn_bwd(q, k, v, ltot, dgated, o, zgate, reduce=None):
    S = q.shape[0]
    tq, tk = ATT_Q_TILE, ATT_K_TILE
    kpq = tq // tk
    nq = S // tq

    def body(q_ref, k_ref, v_ref, lt_ref, dg_ref, o_ref, z_ref, dq_ref, dk_ref, dv_ref, dz_ref, dk_acc, dv_acc):
        qi = pl.program_id(1)

        @pl.when(qi == 0)
        def _():
            dk_acc[...] = jnp.zeros_like(dk_acc)
            dv_acc[...] = jnp.zeros_like(dv_acc)

        lane = lax.broadcasted_iota(jnp.int32, (1, 128), 1)
        ri = lax.broadcasted_iota(jnp.int32, (tk, tk), 0)
        ci = lax.broadcasted_iota(jnp.int32, (tk, tk), 1)
        later_mat = _mask_bf16(ri > ci)
        before_mat = _mask_bf16(ri < ci)
        t_idx = qi * tq + lax.broadcasted_iota(jnp.int32, (tq, tk), 0)
        s_off = lax.broadcasted_iota(jnp.int32, (tq, tk), 1)
        zz = z_ref[...].astype(F32)
        sg = _sigmoid(zz)
        dgv = dg_ref[...].astype(F32)
        dz_ref[...] = (dgv * o_ref[...].astype(F32) * _dsilu(zz, sg)).astype(BF16)
        dob = (dgv * (zz * sg)).astype(BF16)
        ltv = lt_ref[...]
        qv = q_ref[...]
        first = lane < HEAD_DIM
        masks = (first, jnp.logical_not(first))
        q_heads = [jnp.where(hm, qv, jnp.zeros_like(qv)) for hm in masks]
        do_heads = [jnp.where(hm, dob, jnp.zeros_like(dob)) for hm in masks]
        totals = [jnp.max(jnp.where(hm, ltv, -jnp.inf), axis=-1, keepdims=True) for hm in masks]

        def step(kj_first, carry, masked):
            chains = [(d, h) for d in range(kpq) for h in range(2)]
            r0 = [d * tk if masked else 0 for d in range(kpq)]
            s0 = [pl.multiple_of((kj_first + d) * tk, tk) for d in range(kpq)]
            kb = [k_ref[pl.ds(s, tk), :] for s in s0]
            vb = [v_ref[pl.ds(s, tk), :] for s in s0]
            visible = [(s + s_off < t_idx)[r:] for s, r in zip(s0, r0)] if masked else None
            z = {c: _dot_nt(q_heads[c[1]][r0[c[0]]:], kb[c[0]]) for c in chains}
            da = {c: _dot_nt(do_heads[c[1]][r0[c[0]]:], vb[c[0]]) for c in chains}
            run = [carry[0], carry[3]]
            log_own, beta, later, base = {}, {}, {}, {}
            for c in chains:
                kj, h = c
                lk = _softplus_parts(z[c])[0]
                if masked:
                    lk = jnp.where(visible[kj], lk, 0.0)
                log_own[c] = z[c] + lk
                beta[c] = jnp.exp(log_own[c]).astype(BF16)
                later[c] = _dot(lk.astype(BF16), later_mat)
                run[h] = _add_rows(run[h], r0[kj], jnp.sum(lk, axis=-1, keepdims=True))
                base[c] = (totals[h] - run[h])[r0[kj]:]
            grun = [carry[1], carry[4]]
            a_bf, g_bf, gbefore, grun_at = {}, {}, {}, {}
            for c in chains:
                kj, h = c
                a = jnp.exp(log_own[c] + later[c] + base[c])
                if masked:
                    a = jnp.where(visible[kj], a, 0.0)
                a_bf[c] = a.astype(BF16)
                g = da[c] * a
                g_bf[c] = g.astype(BF16)
                gbefore[c] = _dot(g_bf[c], before_mat)
                grun_at[c] = grun[h][r0[kj]:]
                grun[h] = _add_rows(grun[h], r0[kj], jnp.sum(g, axis=-1, keepdims=True))
            dq = [carry[2], carry[5]]
            dk_blk = [jnp.zeros((tk, 128), F32) for _ in range(kpq)]
            dv_blk = [jnp.zeros((tk, 128), F32) for _ in range(kpq)]
            for c in chains:
                kj, h = c
                g = g_bf[c].astype(F32)
                dz = g - beta[c].astype(F32) * (g + gbefore[c] + grun_at[c])
                if masked:
                    dz = jnp.where(visible[kj], dz, 0.0)
                dzb = dz.astype(BF16)
                dq[h] = _add_rows(dq[h], r0[kj], _dot(dzb, kb[kj]))
                dk_blk[kj] = dk_blk[kj] + _dot_tn(dzb, q_heads[h][r0[kj]:])
                dv_blk[kj] = dv_blk[kj] + _dot_tn(a_bf[c], do_heads[h][r0[kj]:])
            for d in range(kpq):
                dk_acc[pl.ds(s0[d], tk), :] += dk_blk[d]
                dv_acc[pl.ds(s0[d], tk), :] += dv_blk[d]
            return run[0], grun[0], dq[0], run[1], grun[1], dq[1]

        zero1, zero128 = jnp.zeros((tq, 1), F32), jnp.zeros((tq, 128), F32)
        carry = lax.fori_loop(0, qi, lambda n, c: step(n * kpq, c, False),
                              (zero1, zero1, zero128, zero1, zero1, zero128))
        carry = step(qi * kpq, carry, True)
        dq_ref[...] = jnp.where(first, carry[2], carry[5]).astype(BF16)

        @pl.when(qi == nq - 1)
        def _():
            dk_ref[...] = dk_acc[...].astype(BF16)
            dv_ref[...] = dv_acc[...].astype(BF16)

    blk = pl.BlockSpec((tq, 128), lambda hp, qi: (qi, hp))
    seq = pl.BlockSpec((S, 128), lambda hp, qi: (0, hp))
    return _call_with_gather(
        body, name="attn_bwd", grid=(D_MODEL // 128, nq),
        in_specs=[blk, seq, seq, blk, blk, blk, blk], out_specs=[blk, seq, seq, blk],
        out_shape=[jax.ShapeDtypeStruct((S, D_MODEL), BF16)] * 4,
        scratch_shapes=[pltpu.VMEM((S, 128), F32), pltpu.VMEM((S, 128), F32)],
        args=(q, k, v, ltot, dgated, o, zgate), reduce=reduce, vmem_mib=56)


def _rms_bwd(xv, dh_gain_sum):
    r = lax.rsqrt(jnp.mean(xv * xv, axis=-1, keepdims=True) + EPS)
    xhat = xv * r
    dx = r * (dh_gain_sum - xhat * jnp.mean(dh_gain_sum * xhat, axis=-1, keepdims=True))
    return dx, xhat


def _b_in_bwd(dq, dk, dv, dz, q_raw, k_raw, x, dx_mid, q_gain_t, k_gain_t, b_gain, kv_gain, w_in, w_kv):
    S = x.shape[0]
    tm = ROW_TILE

    def body(dq_ref, dk_ref, dv_ref, dz_ref, qr_ref, kr_ref, x_ref, dxm_ref, qg_ref, kg_ref, bg_ref, kvg_ref,
             win_ref, wkv_ref, dqz_ref, dkv_ref, dx_ref, small_ref):
        @pl.when(pl.program_id(0) == 0)
        def _():
            small_ref[...] = jnp.zeros_like(small_ref)

        bd = _head_mean_matrix()

        def head_norm_bwd(dy_ref, raw_ref, gain, scale):
            raw = raw_ref[...].astype(F32)
            rr = lax.rsqrt(_head_mean(raw * raw, bd) + EPS)
            xhat = raw * rr
            dy = dy_ref[...].astype(F32) * scale
            gdy = dy * gain
            draw = rr * (gdy - xhat * _head_mean(gdy * xhat, bd))
            return draw.astype(BF16), jnp.sum(dy * xhat, axis=0, keepdims=True)

        dqr, dqg = head_norm_bwd(dq_ref, qr_ref, qg_ref[...], SB_SCALE)
        dkr, dkg = head_norm_bwd(dk_ref, kr_ref, kg_ref[...], 1.0)
        dqz_ref[:, :D_MODEL] = dqr
        dqz_ref[:, D_MODEL:] = dz_ref[...]
        dkv_ref[:, :D_MODEL] = dkr
        dkv_ref[:, D_MODEL:] = dv_ref[...]
        dhb = jnp.zeros((tm, D_MODEL), F32)
        dhkv = jnp.zeros((tm, D_MODEL), F32)
        for sh in range(N_CHIPS):
            cols = slice(sh * 512, (sh + 1) * 512)
            dhb = dhb + _dot_nt(dqz_ref[:, cols], win_ref[sh])
            dhkv = dhkv + _dot_nt(dkv_ref[:, cols], wkv_ref[sh])
        dx, xhat = _rms_bwd(x_ref[...], dhb * bg_ref[...] + dhkv * kvg_ref[...])
        dx_ref[...] = dxm_ref[...] + dx
        small_ref[0:1, :] += dqg
        small_ref[1:2, :] += dkg
        small_ref[2:3, :] += jnp.sum(dhb * xhat, axis=0, keepdims=True)
        small_ref[3:4, :] += jnp.sum(dhkv * xhat, axis=0, keepdims=True)

    row = pl.BlockSpec((tm, D_MODEL), lambda i: (i, 0))
    wide = pl.BlockSpec((tm, 2 * D_MODEL), lambda i: (i, 0))
    vec = pl.BlockSpec((1, D_MODEL), lambda i: (0, 0))
    wsp = pl.BlockSpec((N_CHIPS, D_MODEL, 512), lambda i: (0, 0, 0))
    return pl.pallas_call(
        body, name="b_in_bwd", grid=(S // tm,),
        in_specs=[row] * 8 + [vec] * 4 + [wsp, wsp],
        out_specs=[wide, wide, row, pl.BlockSpec((8, D_MODEL), lambda i: (0, 0))],
        out_shape=[jax.ShapeDtypeStruct((S, 2 * D_MODEL), BF16), jax.ShapeDtypeStruct((S, 2 * D_MODEL), BF16),
                   jax.ShapeDtypeStruct((S, D_MODEL), F32), jax.ShapeDtypeStruct((8, D_MODEL), F32)],
        compiler_params=_params(("arbitrary",), 56),
    )(dq, dk, dv, dz, q_raw, k_raw, x, dx_mid, q_gain_t, k_gain_t, b_gain, kv_gain, w_in, w_kv)


def _a_mix_bwd(dgated, uz, pooled, wg, scale, w_in, x, dx_mid, gain, reduce=None):
    S = x.shape[0]
    tm = ROW_TILE
    n = S // tm

    def body(dg_ref, z_ref, p_ref, wg_ref, sc_ref, win_ref, x_ref, dxm_ref, gn_ref,
             duz_ref, dmr_ref, dx_ref, small_ref, halo_hi, halo_lo):
        i = pl.program_id(0)

        @pl.when(i == 0)
        def _():
            small_ref[...] = jnp.zeros_like(small_ref)
            halo_hi[...] = jnp.zeros_like(halo_hi)
            halo_lo[...] = jnp.zeros_like(halo_lo)

        first_row = (n - 1 - i) * tm
        row = lax.broadcasted_iota(jnp.int32, (tm, tm), 0)
        col = lax.broadcasted_iota(jnp.int32, (tm, tm), 1)
        d = col - row
        for g, w in enumerate(POOL_WINDOWS):
            cols = slice(g * GROUP_DIM, (g + 1) * GROUP_DIM)
            wgg = _group_weight(wg_ref, g)
            sc = sc_ref[:, cols]
            mraw = _dot(p_ref[:, cols], wgg)
            z = z_ref[:, cols]
            sg = _sigmoid(z)
            dga = dg_ref[:, cols].astype(F32)
            dm = dga * (z * sg)
            duz_ref[:, D_MODEL + g * GROUP_DIM:D_MODEL + (g + 1) * GROUP_DIM] = (
                dga * (mraw * sc) * _dsilu(z, sg)).astype(BF16)
            small_ref[0:1, cols] += jnp.sum(dm * mraw, axis=0, keepdims=True)
            dmr = (dm * sc).astype(BF16)
            dmr_ref[:, cols] = dmr
            dp = _dot_nt(dmr, wgg)
            hi, lo = _hilo(dp * _inv_count(first_row, tm, w))
            t_main = _mask_bf16((d >= 0) & (d < w))
            t_halo = _mask_bf16(d + tm < w)
            du = (_dot(t_main, hi) + _dot(t_main, lo) + _dot(t_halo, halo_hi[:, cols]) + _dot(t_halo, halo_lo[:, cols])
                  - dp)
            halo_hi[:, cols] = hi
            halo_lo[:, cols] = lo
            duz_ref[:, cols] = du.astype(BF16)
        dh = jnp.zeros((tm, D_MODEL), F32)
        for sh in range(N_CHIPS):
            dh = dh + _dot_nt(duz_ref[:, sh * 512:(sh + 1) * 512], win_ref[sh])
        dx, xhat = _rms_bwd(x_ref[...], dh * gn_ref[...])
        dx_ref[...] = dxm_ref[...] + dx
        small_ref[1:2, :] += jnp.sum(dh * xhat, axis=0, keepdims=True)

    rev = lambda i: (n - 1 - i, 0)
    row = pl.BlockSpec((tm, D_MODEL), rev)
    vec = pl.BlockSpec((1, D_MODEL), lambda i: (0, 0))
    return _call_with_gather(
        body, name="a_mix_bwd", grid=(n,),
        in_specs=[row,
                  pl.BlockSpec((tm, D_MODEL), lambda i: (n - 1 - i, 1)),
                  row,
                  pl.BlockSpec((N_CHIPS, N_GROUPS, 64, GROUP_DIM), lambda i: (0, 0, 0, 0)),
                  vec,
                  pl.BlockSpec((N_CHIPS, D_MODEL, 512), lambda i: (0, 0, 0)),
                  row, row, vec],
        out_specs=[pl.BlockSpec((tm, 2 * D_MODEL), rev), row, row,
                   pl.BlockSpec((8, D_MODEL), lambda i: (0, 0))],
        out_shape=[jax.ShapeDtypeStruct((S, 2 * D_MODEL), BF16), jax.ShapeDtypeStruct((S, D_MODEL), BF16),
                   jax.ShapeDtypeStruct((S, D_MODEL), F32), jax.ShapeDtypeStruct((8, D_MODEL), F32)],
        scratch_shapes=[pltpu.VMEM((tm, D_MODEL), BF16), pltpu.VMEM((tm, D_MODEL), BF16)],
        args=(dgated, uz, pooled, wg, scale, w_in, x, dx_mid, gain), reduce=reduce, vmem_mib=56)


def _wgrad(name, a, dy, n_shards, a_spec=None, k_dim=None):
    S, n_cols = dy.shape
    ts = WGRAD_SEQ_TILE
    k_dim = a.shape[-1] if k_dim is None else k_dim
    wn = n_cols // n_shards
    tk = min(k_dim, WGRAD_ACC_BYTES // (4 * n_cols))
    nst = S // ts

    def body(a_ref, dy_ref, out_ref, acc):
        st = pl.program_id(1)

        @pl.when(st == 0)
        def _():
            acc[...] = jnp.zeros_like(acc)

        acc[...] += _dot_tn(a_ref[...].astype(BF16), dy_ref[...].astype(BF16))

        @pl.when(st == nst - 1)
        def _():
            for sh in range(n_shards):
                out_ref[sh] = acc[:, sh * wn:(sh + 1) * wn]

    if a_spec is None:
        a_spec = pl.BlockSpec((ts, tk), lambda kt, st: (st, kt))
    return pl.pallas_call(
        body, name=name, grid=(k_dim // tk, nst),
        in_specs=[a_spec, pl.BlockSpec((ts, n_cols), lambda kt, st: (st, 0))],
        out_specs=pl.BlockSpec((n_shards, tk, wn), lambda kt, st: (0, kt, 0)),
        out_shape=jax.ShapeDtypeStruct((n_shards, k_dim, wn), F32),
        scratch_shapes=[pltpu.VMEM((tk, n_cols), F32)],
        compiler_params=_params(("parallel", "arbitrary")),
    )(a, dy)


def _wgrad_ple(name, p, layer, de):
    ts = WGRAD_SEQ_TILE
    spec = pl.BlockSpec((None, None, ts, PLE_DIM), lambda kt, st: (layer, 0, st, 0))
    return _wgrad(name, p, de, N_CHIPS, a_spec=spec, k_dim=PLE_DIM)


def _wgrad_group(pooled, dmr):
    S = pooled.shape[0]
    ts = WGRAD_SEQ_TILE
    nst = S // ts

    def body(p_ref, d_ref, out_ref, acc):
        st = pl.program_id(1)

        @pl.when(st == 0)
        def _():
            acc[...] = jnp.zeros_like(acc)

        acc[...] += _dot_tn(p_ref[...], d_ref[...])

        @pl.when(st == nst - 1)
        def _():
            for sh in range(N_CHIPS):
                out_ref[sh] = acc[sh * 64:(sh + 1) * 64, :]

    blk = pl.BlockSpec((ts, GROUP_DIM), lambda g, st: (st, g))
    return pl.pallas_call(
        body, name="wgrad_group", grid=(N_GROUPS, nst),
        in_specs=[blk, blk],
        out_specs=pl.BlockSpec((N_CHIPS, None, 64, GROUP_DIM), lambda g, st: (0, g, 0, 0)),
        out_shape=jax.ShapeDtypeStruct((N_CHIPS, N_GROUPS, 64, GROUP_DIM), F32),
        scratch_shapes=[pltpu.VMEM((GROUP_DIM, GROUP_DIM), F32)],
        compiler_params=_params(("parallel", "arbitrary")),
    )(pooled, dmr)


GATHER_AT = {
    "a_in": ("a_w_group", "a_w_out", "ple_w0", "ple_gate_w0"),
    "a_mix": ("w_kv",),
    "a_out_ple": ("b_w_in",),
    "b_in": ("b_w_out", "ple_w1", "ple_gate_w1"),
}


REDUCE_AT = {
    "attn_bwd": ("b_w_out", "ple_w1", "ple_gate_w1"),
    "a_mix_bwd": ("w_kv", "b_w_in"),
}


def _local_step(x, p, target, w, local=None, state=None):
    w = dict(w)

    def run(fn, host, n_out, *args, **kwargs):
        names = GATHER_AT[host] if local is not None else ()
        res = fn(*args, gather=[local[n] for n in names], **kwargs)
        w.update(zip(names, res[n_out:]))
        return res[:n_out]

    k_gain_t = jnp.tile(w["k_norm"].reshape(1, HEAD_DIM), (1, N_HEADS))
    q_gain_t = jnp.tile(w["b_q_norm"].reshape(1, HEAD_DIM), (1, N_HEADS))

    uz, h_a = run(_a_in, "a_in", 2, x, w["a_norm"], w["a_w_in"])
    wg4 = w["a_w_group"].reshape(N_CHIPS, N_GROUPS, 64, GROUP_DIM)
    wa_out = w["a_w_out"].reshape(D_MODEL, D_MODEL)
    gated_a, pooled = run(_a_mix, "a_mix", 2, uz, wg4, w["a_scale"])
    x1, x2, e_a, gate_a = run(_out_ple, "a_out_ple", 4, "a_out_ple", gated_a, x, wa_out, p, 0,
                              w["ple_w0"], w["ple_gate_w0"])
    h_kv, h_b, k_raw, q_raw, k, q, v, z_b = run(
        _b_in, "b_in", 8, x2, w["kv_norm"], w["b_norm"], k_gain_t, q_gain_t, w["w_kv"], w["b_w_in"])
    wb_out = w["b_w_out"].reshape(D_MODEL, D_MODEL)
    o, gated_b, ltot = _attn_fwd(q, k, v, z_b)
    x3, dx4, e_b, gate_b, loss_blk = _out_ple("b_out_ple", gated_b, x2, wb_out, p, 1, w["ple_w1"], w["ple_gate_w1"],
                                              target=target)

    grads, updates = {}, {}

    def hosted(fn, host, n_out, *args):
        if state is None:
            return fn(*args)
        names = REDUCE_AT[host]
        res = fn(*args, reduce=([grads.pop(n) for n in names], *[[t[n] for n in names] for t in state]))
        for i, n in enumerate(names):
            updates[n] = tuple(group[i] for group in res[n_out:])
        return res[:n_out]

    de_b, dgp_b, dx3, dgated_b = _ple_out_bwd("b_ple_out_bwd", dx4, e_b, gate_b, w["ple_gate_w1"], wb_out)
    grads["b_w_out"] = _wgrad("wgrad_b_out", gated_b, dx3, 1).reshape(N_CHIPS, 256, D_MODEL)
    grads["ple_w1"] = _wgrad_ple("wgrad_ple1", p, 1, de_b)
    grads["ple_gate_w1"] = _wgrad("wgrad_gate1", x3, dgp_b, 1).reshape(N_CHIPS, 256, D_MODEL)
    dq, dk, dv, dz_b = hosted(_attn_bwd, "attn_bwd", 4, q, k, v, ltot, dgated_b, o, z_b)
    dqz, dkv, dx2, small_b = _b_in_bwd(dq, dk, dv, dz_b, q_raw, k_raw, x2, dx3, q_gain_t, k_gain_t,
                                       w["b_norm"], w["kv_norm"], w["b_w_in"], w["w_kv"])
    grads["w_kv"] = _wgrad("wgrad_kv", h_kv, dkv, N_CHIPS)
    grads["b_w_in"] = _wgrad("wgrad_b_in", h_b, dqz, N_CHIPS)
    de_a, dgp_a, dx1, dgated_a = _ple_out_bwd("a_ple_out_bwd", dx2, e_a, gate_a, w["ple_gate_w0"], wa_out)
    grads["a_w_out"] = _wgrad("wgrad_a_out", gated_a, dx1, 1).reshape(N_CHIPS, 256, D_MODEL)
    grads["ple_w0"] = _wgrad_ple("wgrad_ple0", p, 0, de_a)
    grads["ple_gate_w0"] = _wgrad("wgrad_gate0", x1, dgp_a, 1).reshape(N_CHIPS, 256, D_MODEL)
    duz, dmr, grad_x, small_a = hosted(_a_mix_bwd, "a_mix_bwd", 4, dgated_a, uz, pooled, wg4, w["a_scale"],
                                       w["a_w_in"], x, dx1, w["a_norm"])
    grads["a_w_in"] = _wgrad("wgrad_a_in", h_a, duz, N_CHIPS)
    grads["a_w_group"] = _wgrad_group(pooled, dmr).reshape(N_CHIPS, N_GROUPS * 64, GROUP_DIM)

    fold = lambda row: jnp.pad(row.reshape(N_HEADS, HEAD_DIM).sum(axis=0), (0, D_MODEL - HEAD_DIM))
    small = jnp.stack([small_a[1], small_a[0], small_b[3], small_b[2], fold(small_b[1]), fold(small_b[0]),
                       jnp.pad(loss_blk[0], (0, D_MODEL - loss_blk.shape[1])), jnp.zeros((D_MODEL,), F32)])
    return grad_x, grads, updates, small


def _mesh_place():
    x, y, c = lax.axis_index("x"), lax.axis_index("y"), lax.axis_index("c")
    other_chips = [(1 - x, y), (x, 1 - y), (1 - x, 1 - y)]
    return x, y, c, other_chips


def _gather_sems(n):
    return [pltpu.SemaphoreType.DMA((3 * n,)), pltpu.SemaphoreType.DMA((3 * n,)),
            pltpu.SemaphoreType.DMA((3 * n,)), pltpu.SemaphoreType.DMA((3 * n,)), pltpu.SemaphoreType.DMA((n,))]


def _gather_copies(srcs, outs, sems):
    send_far, recv_far, send_sib, recv_sib, local_sem = sems
    n = len(srcs)
    x, y, c, chips = _mesh_place()
    me = 2 * x + y
    sibling = (x, y, 1 - c)

    def half(k, which):
        rows = srcs[k].shape[0] // 2
        return pl.ds(pl.multiple_of(which * rows, 16), rows)

    local = [pltpu.make_async_copy(srcs[k], outs[k].at[me], local_sem.at[k]) for k in range(n)]
    far = [pltpu.make_async_remote_copy(
        src_ref=srcs[k].at[half(k, c)], dst_ref=outs[k].at[me, half(k, c)],
        send_sem=send_far.at[j * n + k], recv_sem=recv_far.at[j * n + k], device_id=(px, py, c), device_id_type=MESH)
        for j, (px, py) in enumerate(chips) for k in range(n)]

    def landed(j, k, which, from_far):
        px, py = chips[j]
        piece = outs[k].at[2 * px + py, half(k, which)]
        send, recv = (send_far, recv_far) if from_far else (send_sib, recv_sib)
        return pltpu.make_async_remote_copy(src_ref=piece, dst_ref=piece, send_sem=send.at[j * n + k],
                                            recv_sem=recv.at[j * n + k], device_id=sibling, device_id_type=MESH)

    return local, far, landed, c


def _gather_start(srcs, outs, sems):
    local, far, _, _ = _gather_copies(srcs, outs, sems)
    for cp in local + far:
        cp.start()


def _gather_pass_on(srcs, outs, sems):
    _, _, landed, c = _gather_copies(srcs, outs, sems)
    for j in range(3):
        for k in range(len(srcs)):
            landed(j, k, c, True).wait_recv()
            landed(j, k, c, False).start()


def _gather_finish(srcs, outs, sems):
    local, far, landed, c = _gather_copies(srcs, outs, sems)
    pairs = [(j, k) for j in range(3) for k in range(len(srcs))]
    for j, k in pairs:
        landed(j, k, 1 - c, False).wait_recv()
    for cp in far + [landed(j, k, c, False) for j, k in pairs]:
        cp.wait_send()
    for cp in local:
        cp.wait()


def _call_with_gather(body, *, name, grid, in_specs, out_specs, out_shape, args, gather=(), reduce=None,
                      scratch_shapes=(), vmem_mib=48):
    n_in, n_out, n_scr, n_g = len(args), len(out_shape), len(scratch_shapes), len(gather)
    n_r = len(reduce[0]) if reduce else 0
    pieces = _reduce_pieces(reduce[0]) if reduce else []
    reduce_args = [a for group in reduce for a in group] if reduce else []
    gather_sems = _gather_sems(n_g) if n_g else []
    n_steps = 1
    for g in grid:
        n_steps *= g

    def wrapped(*refs):
        refs = list(refs)
        take = lambda count: [refs.pop(0) for _ in range(count)]
        ins, g_in, r_in = take(n_in), take(n_g), take(4 * n_r)
        outs, g_out, r_out = take(n_out), take(n_g), take(4 * n_r)
        scratch, sems, r_scratch = take(n_scr), take(len(gather_sems)), refs
        step = 0
        for axis, g in enumerate(grid):
            step = step * g + pl.program_id(axis)
        if n_g:
            @pl.when(step == 0)
            def _():
                _gather_start(g_in, g_out, sems)

        if n_r:
            ticks, drain = _reduce_ticks(pieces, n_r, (*r_in, *r_out, *r_scratch))
            for t, tick in enumerate(ticks[:n_steps]):
                pl.when(step == t)(tick)

        body(*ins, *outs, *scratch)
        if n_r:
            for tick in ticks[n_steps:]:
                pl.when(step == n_steps - 1)(tick)
            pl.when(step == n_steps - 1)(drain)
        if n_g:
            @pl.when(step == max(n_steps - 2, 0))
            def _():
                _gather_pass_on(g_in, g_out, sems)

            @pl.when(step == n_steps - 1)
            def _():
                _gather_finish(g_in, g_out, sems)

    hbm = pl.BlockSpec(memory_space=pltpu.HBM)
    res = pl.pallas_call(
        wrapped, name=name, grid=grid,
        in_specs=list(in_specs) + [hbm] * (n_g + 4 * n_r), out_specs=list(out_specs) + [hbm] * (n_g + 4 * n_r),
        out_shape=list(out_shape) + [jax.ShapeDtypeStruct((N_CHIPS,) + g.shape, BF16) for g in gather]
        + ([jax.ShapeDtypeStruct(w.shape, F32) for _ in range(4) for w in reduce[1]] if reduce else []),
        scratch_shapes=list(scratch_shapes) + gather_sems + (_reduce_scratch() if reduce else []),
        compiler_params=_params(("arbitrary",) * len(grid), vmem_mib),
    )(*args, *gather, *reduce_args)
    if not reduce:
        return res
    plain = list(res[:n_out + n_g])
    return plain + [res[n_out + n_g + i * n_r:n_out + n_g + (i + 1) * n_r] for i in range(4)]


def _allgather_weights(shards, small, casts):
    n = len(shards)
    cast_out = [(k, r0, r1) for k, (_, ranges) in enumerate(casts) for r0, r1 in ranges]
    n_c, n_co = len(casts), len(cast_out)

    def body(*refs):
        ins, small_in, cast_in = refs[:n], refs[n], refs[n + 1:n + 1 + n_c]
        refs = refs[n + 1 + n_c:]
        outs, small_out, cast_dst = refs[:n], refs[n], refs[n + 1:n + 1 + n_co]
        refs = refs[n + 1 + n_co:]
        cast, cast_buf = refs[:n], refs[n:n + n_co]
        send_far, recv_far, send_sib, recv_sib, send_small, recv_small, local_sem, cast_sem = refs[n + n_co:]
        x, y, c, chips = _mesh_place()
        me = 2 * x + y
        sibling = (x, y, 1 - c)

        def half(k, which):
            rows = ins[k].shape[0] // 2
            return pl.ds(pl.multiple_of(which * rows, 16), rows)

        local = []
        for k in range(n):
            cast[k][...] = ins[k][...].astype(BF16)
            local.append(pltpu.make_async_copy(cast[k], outs[k].at[me], local_sem.at[k]))
            local[-1].start()
        local.append(pltpu.make_async_copy(small_in, small_out.at[me], local_sem.at[n]))
        local[-1].start()

        sends = []
        for j, (px, py) in enumerate(chips):
            for k in range(n):
                cp = pltpu.make_async_remote_copy(
                    src_ref=cast[k].at[half(k, c)], dst_ref=outs[k].at[me, half(k, c)],
                    send_sem=send_far.at[j * n + k], recv_sem=recv_far.at[j * n + k],
                    device_id=(px, py, c), device_id_type=MESH)
                cp.start()
                sends.append(cp)
            cp = pltpu.make_async_remote_copy(
                src_ref=small_in, dst_ref=small_out.at[me], send_sem=send_small.at[j], recv_sem=recv_small.at[j],
                device_id=(px, py, c), device_id_type=MESH)
            cp.start()
            sends.append(cp)

        for i, (k, r0, r1) in enumerate(cast_out):
            cast_buf[i][...] = cast_in[k][r0:r1, :].astype(BF16)
            local.append(pltpu.make_async_copy(cast_buf[i], cast_dst[i], cast_sem.at[i]))
            local[-1].start()

        def landed(j, k, which, sems_s, sems_r, device):
            px, py = chips[j]
            piece = outs[k].at[2 * px + py, half(k, which)]
            return pltpu.make_async_remote_copy(
                src_ref=piece, dst_ref=piece, send_sem=sems_s.at[j * n + k], recv_sem=sems_r.at[j * n + k],
                device_id=device, device_id_type=MESH)

        for j in range(len(chips)):
            for k in range(n):
                landed(j, k, c, send_far, recv_far, sibling).wait_recv()
                cp = landed(j, k, c, send_sib, recv_sib, sibling)
                cp.start()
                sends.append(cp)
        for j, (px, py) in enumerate(chips):
            for k in range(n):
                landed(j, k, 1 - c, send_sib, recv_sib, sibling).wait_recv()
            pltpu.make_async_remote_copy(
                src_ref=small_in, dst_ref=small_out.at[2 * px + py], send_sem=send_small.at[j],
                recv_sem=recv_small.at[j], device_id=(px, py, c), device_id_type=MESH).wait_recv()
        for cp in sends:
            cp.wait_send()
        for cp in local:
            cp.wait()

    vmem = pl.BlockSpec(memory_space=pltpu.VMEM)
    hbm = pl.BlockSpec(memory_space=pltpu.HBM)
    cast_shapes = [(r1 - r0, casts[k][0].shape[1]) for k, r0, r1 in cast_out]
    res = pl.pallas_call(
        body, name="allgather_weights",
        in_specs=[vmem] * (n + 1 + n_c), out_specs=[hbm] * (n + 1 + n_co),
        out_shape=[jax.ShapeDtypeStruct((N_CHIPS,) + s.shape, BF16) for s in shards]
        + [jax.ShapeDtypeStruct((N_CHIPS,) + small.shape, F32)]
        + [jax.ShapeDtypeStruct(s, BF16) for s in cast_shapes],
        scratch_shapes=[pltpu.VMEM(s.shape, BF16) for s in shards] + [pltpu.VMEM(s, BF16) for s in cast_shapes]
        + [pltpu.SemaphoreType.DMA((3 * n,)), pltpu.SemaphoreType.DMA((3 * n,)),
           pltpu.SemaphoreType.DMA((3 * n,)), pltpu.SemaphoreType.DMA((3 * n,)),
           pltpu.SemaphoreType.DMA((3,)), pltpu.SemaphoreType.DMA((3,)),
           pltpu.SemaphoreType.DMA((n + 1,)), pltpu.SemaphoreType.DMA((n_co,))],
        compiler_params=_params(None, 40),
    )(*shards, small, *[a for a, _ in casts])
    return res[:n], res[n], res[n + 1:]


def _adamw(w, g, m, v):
    m = ADAM_B1 * m + (1.0 - ADAM_B1) * g
    v = ADAM_B2 * v + (1.0 - ADAM_B2) * (g * g)
    m_hat = m / (1.0 - ADAM_B1 ** ADAM_STEP)
    v_hat = v / (1.0 - ADAM_B2 ** ADAM_STEP)
    delta = -ADAM_LR * (m_hat / (jnp.sqrt(v_hat) + ADAM_EPS) + ADAM_WD * w)
    return delta, m, v


RS_PIECE_ROWS = 128
RS_PIECE_COLS = 512


def _reduce_adam_all(grads, ws, ms, vs):
    n_w = len(grads)
    pieces = _reduce_pieces(grads)

    def body(*refs):
        ticks, drain = _reduce_ticks(pieces, n_w, refs)
        for tick in ticks:
            tick()
        drain()

    hbm = pl.BlockSpec(memory_space=pltpu.HBM)
    outs = pl.pallas_call(
        body, name="reduce_adam_all",
        in_specs=[hbm] * (4 * n_w), out_specs=[hbm] * (4 * n_w),
        out_shape=[jax.ShapeDtypeStruct(w.shape, F32) for _ in range(4) for w in ws],
        scratch_shapes=_reduce_scratch(),
        compiler_params=_params(None, 48),
    )(*grads, *ws, *ms, *vs)
    return [outs[i * n_w:(i + 1) * n_w] for i in range(4)]


def _reduce_pieces(grads):
    pieces = []
    for k, g in enumerate(grads):
        hr, cols = g.shape[1] // 2, g.shape[2]
        pr, pc = min(hr, RS_PIECE_ROWS), min(cols, RS_PIECE_COLS)
        pieces += [(k, ro, hr, co, pr, pc) for ro in range(0, hr, pr) for co in range(0, cols, pc)]
    return pieces


def _reduce_scratch():
    P, C = RS_PIECE_ROWS, RS_PIECE_COLS
    return [
        pltpu.VMEM((3, N_CHIPS, P, C), F32), pltpu.VMEM((3, N_CHIPS, P, C), F32),
        pltpu.VMEM((2, N_CHIPS, P, C), BF16), pltpu.VMEM((2, N_CHIPS, P, C), BF16),
        pltpu.VMEM((2, N_CHIPS, P, C), F32),
        pltpu.VMEM((2, 3, P, C), BF16), pltpu.VMEM((2, 3, P, C), BF16),
        pltpu.VMEM((2, 2, P, C), F32),
        pltpu.VMEM((2, 3, 2, P, C), F32), pltpu.VMEM((2, 4, 2, P, C), F32),
        pltpu.SemaphoreType.DMA((3, 2)), pltpu.SemaphoreType.DMA((2, 3, 2)),
        pltpu.SemaphoreType.DMA((2,)), pltpu.SemaphoreType.DMA((2,)),
        pltpu.SemaphoreType.DMA((2, 3)), pltpu.SemaphoreType.DMA((2, 3)),
        pltpu.SemaphoreType.DMA((2,)), pltpu.SemaphoreType.DMA((2,)),
        pltpu.SemaphoreType.DMA((2, 4, 2))]


def _reduce_ticks(pieces, n_w, refs):
    n = len(pieces)

    def build(*refs):
        g_in, w_in, m_in, v_in = (refs[i * n_w:(i + 1) * n_w] for i in range(4))
        g_out, d_out, m_out, v_out = (refs[(4 + i) * n_w:(5 + i) * n_w] for i in range(4))
        (gm, go, sb1, rb1, part, sb2, rb2, fin, wmv, outs,
         ld_sem, wmv_sem, s1_send, s1_recv, s2_send, s2_recv, s3_send, s3_recv, out_sem) = refs[8 * n_w:]
        x, y, c, chips = _mesh_place()
        me = 2 * x + y
        sibling = (x, y, 1 - c)

        def at_hbm(i, which):
            _, ro, hr, co, pr, pc = pieces[i]
            half = c if which == 0 else 1 - c
            return pl.ds(pl.multiple_of(half * hr + ro, 64), pr), pl.ds(co, pc)

        def win(i):
            return pl.ds(0, pieces[i][4]), pl.ds(0, pieces[i][5])

        every = slice(None)

        def loads(i):
            k, s = pieces[i][0], i % 3
            return [pltpu.make_async_copy(g_in[k].at[(every,) + at_hbm(i, h)], buf.at[(s, every) + win(i)], ld_sem.at[s, h])
                    for h, buf in enumerate((gm, go))]

        def wmv_loads(i):
            k, s = pieces[i][0], i % 2
            return [pltpu.make_async_copy(src[k].at[at_hbm(i, h)], wmv.at[(s, a, h) + win(i)], wmv_sem.at[s, a, h])
                    for a, src in enumerate((w_in, m_in, v_in)) for h in range(2)]

        def stores(i):
            k, s = pieces[i][0], i % 2
            return [pltpu.make_async_copy(outs.at[(s, a, h) + win(i)], dst[k].at[at_hbm(i, h)], out_sem.at[s, a, h])
                    for a, dst in enumerate((g_out, d_out, m_out, v_out)) for h in range(2)]

        def swap1(i):
            s = i % 2
            return pltpu.make_async_remote_copy(
                src_ref=sb1.at[(s, every) + win(i)], dst_ref=rb1.at[(s, every) + win(i)],
                send_sem=s1_send.at[s], recv_sem=s1_recv.at[s], device_id=sibling, device_id_type=MESH)

        def far2(i, j):
            s = i % 2
            px, py = chips[j]
            return pltpu.make_async_remote_copy(
                src_ref=sb2.at[(s, j) + win(i)], dst_ref=rb2.at[(s, j) + win(i)],
                send_sem=s2_send.at[s, j], recv_sem=s2_recv.at[s, j], device_id=(px, py, c), device_id_type=MESH)

        def swap3(i):
            s = i % 2
            return pltpu.make_async_remote_copy(
                src_ref=fin.at[(s, 0) + win(i)], dst_ref=fin.at[(s, 1) + win(i)],
                send_sem=s3_send.at[s], recv_sem=s3_recv.at[s], device_id=sibling, device_id_type=MESH)

        def stage0(i):
            for cp in loads(i):
                cp.start()

        def stage1(i):
            s, s3 = i % 2, i % 3
            for cp in loads(i):
                cp.wait()
            sb1[(s, every) + win(i)] = go[(s3, every) + win(i)].astype(BF16)
            swap1(i).start()

        def stage2(i):
            s, s3 = i % 2, i % 3
            swap1(i).wait()
            part[(s, every) + win(i)] = gm[(s3, every) + win(i)] + rb1[(s, every) + win(i)].astype(F32)
            for j, (px, py) in enumerate(chips):
                sb2[(s, j) + win(i)] = part[(s, 2 * px + py) + win(i)].astype(BF16)
                far2(i, j).start()

        def stage3(i):
            s = i % 2
            total = part[(s, me) + win(i)]
            for j in range(3):
                far2(i, j).wait()
                total = total + rb2[(s, j) + win(i)].astype(F32)
            fin[(s, 0) + win(i)] = total
            swap3(i).start()
            for cp in wmv_loads(i):
                cp.start()

        def stage4(i):
            s = i % 2
            if i >= 2:
                for cp in stores(i - 2):
                    cp.wait()
            swap3(i).wait()
            for cp in wmv_loads(i):
                cp.wait()
            both = (every,) + win(i)
            g = fin[(s,) + both]
            delta, m_new, v_new = _adamw(wmv[(s, 0) + both], g, wmv[(s, 1) + both], wmv[(s, 2) + both])
            outs[(s, 0) + both] = g
            outs[(s, 1) + both] = delta
            outs[(s, 2) + both] = m_new
            outs[(s, 3) + both] = v_new
            for cp in stores(i):
                cp.start()

        stages = (stage0, stage1, stage2, stage3, stage4)

        def tick(t):
            for age in reversed(range(len(stages))):
                if 0 <= t - age < n:
                    stages[age](t - age)

        def drain():
            for i in range(max(0, n - 2), n):
                for cp in stores(i):
                    cp.wait()

        return [functools.partial(tick, t) for t in range(n + len(stages) - 1)], drain

    return build(*refs)


def _allreduce_small(part):
    n_dev = 8

    def body(part_ref, out_ref, buf, send_sem, recv_sem):
        x, y, c, _ = _mesh_place()
        me = 4 * x + 2 * y + c
        buf[me] = part_ref[...]
        sends = []
        for k in range(1, n_dev):
            peer = ((1 - x) if k & 4 else x, (1 - y) if k & 2 else y, (1 - c) if k & 1 else c)
            cp = pltpu.make_async_remote_copy(src_ref=part_ref, dst_ref=buf.at[me], send_sem=send_sem.at[k - 1],
                                              recv_sem=recv_sem.at[k - 1], device_id=peer, device_id_type=MESH)
            cp.start()
            sends.append(cp)
        for cp in sends:
            cp.wait_recv()
        total = buf[0]
        for s in range(1, n_dev):
            total = total + buf[s]
        out_ref[...] = total
        for cp in sends:
            cp.wait_send()

    vmem = pl.BlockSpec(memory_space=pltpu.VMEM)
    return pl.pallas_call(
        body, name="allreduce_small", in_specs=[vmem], out_specs=vmem,
        out_shape=jax.ShapeDtypeStruct(part.shape, F32),
        scratch_shapes=[pltpu.VMEM((n_dev,) + part.shape, F32),
                        pltpu.SemaphoreType.DMA((n_dev - 1,)), pltpu.SemaphoreType.DMA((n_dev - 1,))],
    )(part)


def _adam_small(w, g, m, v):
    def body(w_ref, g_ref, m_ref, v_ref, d_ref, mo_ref, vo_ref):
        delta, m_new, v_new = _adamw(w_ref[...], g_ref[...], m_ref[...], v_ref[...])
        d_ref[...] = delta
        mo_ref[...] = m_new
        vo_ref[...] = v_new

    vmem = pl.BlockSpec(memory_space=pltpu.VMEM)
    return pl.pallas_call(
        body, name="adam_small", in_specs=[vmem] * 4, out_specs=[vmem] * 3,
        out_shape=[jax.ShapeDtypeStruct(w.shape, F32)] * 3,
    )(w, g, m, v)


BIG = ("a_w_in", "a_w_group", "a_w_out", "w_kv", "b_w_in", "b_w_out", "ple_w", "ple_gate_w")
SMALL = ("a_norm", "a_scale", "kv_norm", "b_norm", "k_norm", "b_q_norm")
SMALL_SHARDED = ("a_norm", "a_scale")
WEIGHTS = ("a_norm", "a_w_in", "a_w_group", "a_scale", "a_w_out", "kv_norm", "w_kv", "k_norm", "b_norm", "b_w_in",
           "b_q_norm", "b_w_out", "ple_w", "ple_gate_w")


def _as_matrix(a):
    return a.reshape(-1, a.shape[-1])


def _pack_small(arrs):
    rows = [jnp.pad(a.reshape(-1), (0, D_MODEL - a.size)) for a in arrs]
    rows += [jnp.zeros((D_MODEL,), F32)] * (8 - len(rows))
    return jnp.stack(rows)


def kernel(x, p, a_norm, a_w_in, a_w_group, a_scale, a_w_out, kv_norm, w_kv, k_norm, b_norm, b_w_in, b_q_norm, b_w_out, ple_w, ple_gate_w, loss_target, m_a_norm, m_a_w_in, m_a_w_group, m_a_scale, m_a_w_out, m_kv_norm, m_w_kv, m_k_norm, m_b_norm, m_b_w_in, m_b_q_norm, m_b_w_out, m_ple_w, m_ple_gate_w, v_a_norm, v_a_w_in, v_a_w_group, v_a_scale, v_a_w_out, v_kv_norm, v_w_kv, v_k_norm, v_b_norm, v_b_w_in, v_b_q_norm, v_b_w_out, v_ple_w, v_ple_gate_w):
    wts = dict(a_norm=a_norm, a_w_in=a_w_in, a_w_group=a_w_group, a_scale=a_scale, a_w_out=a_w_out, kv_norm=kv_norm,
               w_kv=w_kv, k_norm=k_norm, b_norm=b_norm, b_w_in=b_w_in, b_q_norm=b_q_norm, b_w_out=b_w_out,
               ple_w=ple_w, ple_gate_w=ple_gate_w)
    mom = dict(a_norm=m_a_norm, a_w_in=m_a_w_in, a_w_group=m_a_w_group, a_scale=m_a_scale, a_w_out=m_a_w_out,
               kv_norm=m_kv_norm, w_kv=m_w_kv, k_norm=m_k_norm, b_norm=m_b_norm, b_w_in=m_b_w_in,
               b_q_norm=m_b_q_norm, b_w_out=m_b_w_out, ple_w=m_ple_w, ple_gate_w=m_ple_gate_w)
    var = dict(a_norm=v_a_norm, a_w_in=v_a_w_in, a_w_group=v_a_w_group, a_scale=v_a_scale, a_w_out=v_a_w_out,
               kv_norm=v_kv_norm, w_kv=v_w_kv, k_norm=v_k_norm, b_norm=v_b_norm, b_w_in=v_b_w_in,
               b_q_norm=v_b_q_norm, b_w_out=v_b_w_out, ple_w=v_ple_w, ple_gate_w=v_ple_gate_w)
    S = x.shape[1]
    chip = 2 * lax.axis_index("x") + lax.axis_index("y")

    sharded_small = jnp.concatenate([a_norm.reshape(1, 256), a_scale.reshape(1, 256), jnp.zeros((6, 256), F32)], axis=0)
    later = ("a_w_group", "a_w_out", "w_kv", "b_w_in", "b_w_out", "ple_w", "ple_gate_w")
    (a_w_in_full,), small_full, copies = _allgather_weights(
        [_as_matrix(a_w_in)], sharded_small,
        [(_as_matrix(wts[n]), [(0, 256), (256, 512)] if n.startswith("ple") else [(0, _as_matrix(wts[n]).shape[0])])
         for n in later])
    local = dict(zip(("a_w_group", "a_w_out", "w_kv", "b_w_in", "b_w_out", "ple_w0", "ple_w1", "ple_gate_w0",
                      "ple_gate_w1"), copies))
    full = dict(a_w_in=a_w_in_full,
                a_norm=small_full[:, 0, :].reshape(1, D_MODEL), a_scale=small_full[:, 1, :].reshape(1, D_MODEL),
                kv_norm=kv_norm.reshape(1, D_MODEL), b_norm=b_norm.reshape(1, D_MODEL), k_norm=k_norm, b_q_norm=b_q_norm)

    def shards(t):
        out = {}
        for n in BIG:
            mat = _as_matrix(t[n])
            if n.startswith("ple"):
                out[n + "0"], out[n + "1"] = mat[:256], mat[256:]
            else:
                out[n] = mat
        return out

    state = (shards(wts), shards(mom), shards(var))
    grad_x, grads, updates, small_part = _local_step(x.reshape(S, D_MODEL), p, loss_target.reshape(S, D_MODEL),
                                                     full, local, state)

    names = sorted(grads)
    reduced = _reduce_adam_all([grads[n] for n in names], *[[t[n] for n in names] for t in state])
    for i, n in enumerate(names):
        updates[n] = tuple(group[i] for group in reduced)
    out_g, out_d, out_m, out_v = {}, {}, {}, {}
    for n in BIG:
        for i, out in enumerate((out_g, out_d, out_m, out_v)):
            if n.startswith("ple"):
                out[n] = jnp.stack([updates[n + "0"][i], updates[n + "1"][i]]).reshape(wts[n].shape)
            else:
                out[n] = updates[n][i].reshape(wts[n].shape)

    small_sum = _allreduce_small(small_part)
    loss = small_sum[len(SMALL), 0]
    small_rows = []
    for i, n in enumerate(SMALL):
        row = small_sum[i]
        if n in SMALL_SHARDED:
            row = lax.dynamic_slice(row, (chip * 256,), (256,))
        else:
            row = row[:wts[n].size]
        small_rows.append(row)
    g_small = _pack_small(small_rows)
    d_small, m_small, v_small = _adam_small(_pack_small([wts[n] for n in SMALL]), g_small,
                                            _pack_small([mom[n] for n in SMALL]), _pack_small([var[n] for n in SMALL]))
    for i, n in enumerate(SMALL):
        shape, size = wts[n].shape, wts[n].size
        out_g[n], out_d[n], out_m[n], out_v[n] = (t[i, :size].reshape(shape) for t in (g_small, d_small, m_small, v_small))

    return (loss, grad_x.reshape(1, S, D_MODEL), *[out_g[n] for n in WEIGHTS], *[out_d[n] for n in WEIGHTS],
            *[out_m[n] for n in WEIGHTS], *[out_v[n] for n in WEIGHTS])
```

```python
import functools

import jax
import jax.numpy as jnp
from jax import lax
from jax.experimental import pallas as pl
from jax.experimental.pallas import tpu as pltpu

F32 = jnp.float32
BF16 = jnp.bfloat16
MESH = pl.DeviceIdType.MESH

D_MODEL = 1024
N_HEADS = 16
HEAD_DIM = 64
PLE_DIM = 256
N_GROUPS = 4
GROUP_DIM = 256
POOL_WINDOWS = (2, 4, 8, 16)
N_CHIPS = 4
EPS = 1e-6
SB_SCALE = HEAD_DIM ** -0.5

ADAM_LR = 0.001
ADAM_B1 = 0.9
ADAM_B2 = 0.999
ADAM_EPS = 1e-08
ADAM_WD = 0.01
ADAM_STEP = 10

ROW_TILE = 256
EXP_UNDERFLOW = -104.0
ATT_Q_TILE = 512
ATT_K_TILE = 256
WGRAD_SEQ_TILE = 1024
WGRAD_ACC_BYTES = 4 * 1024 * 1024
MIB = 1024 * 1024


def _params(semantics=None, vmem_mib=48):
    return pltpu.CompilerParams(dimension_semantics=semantics, vmem_limit_bytes=vmem_mib * MIB)


def _dot(a, b):
    return jnp.dot(a, b, preferred_element_type=F32)


def _dot_nt(a, b):
    return lax.dot_general(a, b, (((1,), (1,)), ((), ())), preferred_element_type=F32)


def _dot_tn(a, b):
    return lax.dot_general(a, b, (((0,), (0,)), ((), ())), preferred_element_type=F32)


def _hilo(x):
    hi = x.astype(BF16)
    lo = (x - hi.astype(F32)).astype(BF16)
    return hi, lo


def _dot_hilo(x, w):
    hi, lo = _hilo(x)
    return _dot(hi, w) + _dot(lo, w)


def _sigmoid(z):
    return jax.nn.sigmoid(z)


def _dsilu(z, sg):
    return sg * (1.0 + z * (1.0 - sg))


def _mask_bf16(cond):
    return jnp.where(cond, 1.0, 0.0).astype(BF16)


def _head_mean_matrix():
    r = lax.broadcasted_iota(jnp.int32, (256, 256), 0) // HEAD_DIM
    c = lax.broadcasted_iota(jnp.int32, (256, 256), 1) // HEAD_DIM
    return _mask_bf16(r == c)


def _head_mean(x, bd):
    parts = []
    for s in range(x.shape[1] // 256):
        parts.append(_dot_hilo(x[:, s * 256:(s + 1) * 256], bd))
    out = parts[0] if len(parts) == 1 else jnp.concatenate(parts, axis=1)
    return out * (1.0 / HEAD_DIM)


def _a_in(x, gain, w_sh, gather=()):
    S = x.shape[0]
    tm = 512
    nsh, _, wn = w_sh.shape

    def body(x_ref, g_ref, w_ref, uz_ref, h_ref):
        @pl.when(pl.program_id(1) == 0)
        def _():
            xv = x_ref[...]
            r = lax.rsqrt(jnp.mean(xv * xv, axis=-1, keepdims=True) + EPS)
            h_ref[...] = (xv * r * g_ref[...]).astype(BF16)

        uz_ref[...] = _dot(h_ref[...], w_ref[0])

    return _call_with_gather(
        body, name="a_in", grid=(S // tm, nsh),
        in_specs=[pl.BlockSpec((tm, D_MODEL), lambda i, j: (i, 0)),
                  pl.BlockSpec((1, D_MODEL), lambda i, j: (0, 0)),
                  pl.BlockSpec((1, D_MODEL, wn), lambda i, j: (j, 0, 0))],
        out_specs=[pl.BlockSpec((tm, wn), lambda i, j: (i, j)),
                   pl.BlockSpec((tm, D_MODEL), lambda i, j: (i, 0))],
        out_shape=[jax.ShapeDtypeStruct((S, nsh * wn), F32),
                   jax.ShapeDtypeStruct((S, D_MODEL), BF16)],
        args=(x, gain, w_sh), gather=gather)


def _inv_count(first_row, rows, w):
    t1 = first_row + 1 + lax.broadcasted_iota(jnp.int32, (rows, 1), 0)
    return 1.0 / jnp.minimum(t1, w).astype(F32)


def _group_weight(wg_ref, g):
    return jnp.concatenate([wg_ref[sh, g] for sh in range(N_CHIPS)], axis=0)


def _a_mix(uz, wg, scale, gather=()):
    S = uz.shape[0]
    tm = ROW_TILE

    def body(u_ref, up_ref, z_ref, wg_ref, sc_ref, ga_ref, p_ref):
        i = pl.program_id(0)
        row = lax.broadcasted_iota(jnp.int32, (tm, tm), 0)
        col = lax.broadcasted_iota(jnp.int32, (tm, tm), 1)
        d = row - col
        for g, w in enumerate(POOL_WINDOWS):
            cols = slice(g * GROUP_DIM, (g + 1) * GROUP_DIM)
            t_main = _mask_bf16((d >= 0) & (d < w))
            t_halo = _mask_bf16(d + tm < w)
            u = u_ref[:, cols]
            up = jnp.where(i > 0, up_ref[:, cols], 0.0)
            hi, lo = _hilo(u)
            hip, lop = _hilo(up)
            wsum = _dot(t_main, hi) + _dot(t_main, lo) + _dot(t_halo, hip) + _dot(t_halo, lop)
            pooled = (wsum * _inv_count(i * tm, tm, w) - u).astype(BF16)
            p_ref[:, cols] = pooled
            mraw = _dot(pooled, _group_weight(wg_ref, g))
            z = z_ref[:, cols]
            ga_ref[:, cols] = (mraw * sc_ref[:, cols] * (z * _sigmoid(z))).astype(BF16)

    return _call_with_gather(
        body, name="a_mix", grid=(S // tm,),
        in_specs=[pl.BlockSpec((tm, D_MODEL), lambda i: (i, 0)),
                  pl.BlockSpec((tm, D_MODEL), lambda i: (jnp.maximum(i - 1, 0), 0)),
                  pl.BlockSpec((tm, D_MODEL), lambda i: (i, 1)),
                  pl.BlockSpec((N_CHIPS, N_GROUPS, 64, GROUP_DIM), lambda i: (0, 0, 0, 0)),
                  pl.BlockSpec((1, D_MODEL), lambda i: (0, 0))],
        out_specs=[pl.BlockSpec((tm, D_MODEL), lambda i: (i, 0)),
                   pl.BlockSpec((tm, D_MODEL), lambda i: (i, 0))],
        out_shape=[jax.ShapeDtypeStruct((S, D_MODEL), BF16),
                   jax.ShapeDtypeStruct((S, D_MODEL), BF16)],
        args=(uz, uz, uz, wg, scale), gather=gather)


def _out_ple(name, gated, x_in, w_out, p, layer, ple_w, ple_g, target=None, gather=()):
    S = x_in.shape[0]
    tm = ROW_TILE
    with_loss = target is not None

    def body(*refs):
        if with_loss:
            g_ref, x_ref, wo_ref, p_ref, pw_ref, pg_ref, t_ref, xm_ref, dx_ref, e_ref, gt_ref, loss_ref = refs
        else:
            g_ref, x_ref, wo_ref, p_ref, pw_ref, pg_ref, xm_ref, xo_ref, e_ref, gt_ref = refs
        xm = x_ref[...] + _dot(g_ref[...], wo_ref[...])
        xm_ref[...] = xm
        pb = p_ref[...].astype(BF16)
        e = jnp.concatenate([_dot(pb, pw_ref[sh]) for sh in range(N_CHIPS)], axis=1)
        pg = jnp.concatenate([pg_ref[sh] for sh in range(N_CHIPS)], axis=0)
        gate = _sigmoid(_dot(xm.astype(BF16), pg))
        e_ref[...] = e.astype(BF16)
        gt_ref[...] = gate.astype(BF16)
        xo = xm + e * gate
        if with_loss:
            diff = xo - t_ref[...]
            dx_ref[...] = diff * (1.0 / D_MODEL)

            @pl.when(pl.program_id(0) == 0)
            def _():
                loss_ref[...] = jnp.zeros_like(loss_ref)

            loss_ref[...] += jnp.sum(diff * diff) * (0.5 / D_MODEL)
        else:
            xo_ref[...] = xo

    row = pl.BlockSpec((tm, D_MODEL), lambda i: (i, 0))
    in_specs = [row, row,
                pl.BlockSpec((D_MODEL, D_MODEL), lambda i: (0, 0)),
                pl.BlockSpec((None, None, tm, PLE_DIM), lambda i: (layer, 0, i, 0)),
                pl.BlockSpec((N_CHIPS, PLE_DIM, 256), lambda i: (0, 0, 0)),
                pl.BlockSpec((N_CHIPS, 256, D_MODEL), lambda i: (0, 0, 0))]
    args = [gated, x_in, w_out, p, ple_w, ple_g]
    out_specs = [row, row, row, row]
    out_shape = [jax.ShapeDtypeStruct((S, D_MODEL), F32), jax.ShapeDtypeStruct((S, D_MODEL), F32),
                 jax.ShapeDtypeStruct((S, D_MODEL), BF16), jax.ShapeDtypeStruct((S, D_MODEL), BF16)]
    if with_loss:
        in_specs.append(row)
        args.append(target)
        out_specs.append(pl.BlockSpec((8, 128), lambda i: (0, 0)))
        out_shape.append(jax.ShapeDtypeStruct((8, 128), F32))
    return _call_with_gather(body, name=name, grid=(S // tm,), in_specs=in_specs, out_specs=out_specs,
                             out_shape=out_shape, args=args, gather=gather)


def _b_in(x, kv_gain, b_gain, k_gain_t, q_gain_t, w_kv, w_in, gather=()):
    S = x.shape[0]
    tm = ROW_TILE

    def body(x_ref, kvg_ref, bg_ref, kg_ref, qg_ref, wkv_ref, win_ref,
             hkv_ref, hb_ref, kraw_ref, qraw_ref, k_ref, q_ref, v_ref, z_ref):
        xv = x_ref[...]
        y = xv * lax.rsqrt(jnp.mean(xv * xv, axis=-1, keepdims=True) + EPS)
        hkv = (y * kvg_ref[...]).astype(BF16)
        hb = (y * bg_ref[...]).astype(BF16)
        hkv_ref[...] = hkv
        hb_ref[...] = hb
        bd = _head_mean_matrix()

        def head_norm(raw, gain):
            rr = lax.rsqrt(_head_mean(raw * raw, bd) + EPS)
            return raw * rr * gain

        for sh in range(N_CHIPS):
            kvc = _dot(hkv, wkv_ref[sh])
            qzc = _dot(hb, win_ref[sh])
            cols = slice((sh % 2) * 512, (sh % 2) * 512 + 512)
            if sh < 2:
                kraw_ref[:, cols] = kvc.astype(BF16)
                qraw_ref[:, cols] = qzc.astype(BF16)
                k_ref[:, cols] = head_norm(kvc, kg_ref[:, cols]).astype(BF16)
                q_ref[:, cols] = (head_norm(qzc, qg_ref[:, cols]) * SB_SCALE).astype(BF16)
            else:
                v_ref[:, cols] = kvc.astype(BF16)
                z_ref[:, cols] = qzc.astype(BF16)

    row = pl.BlockSpec((tm, D_MODEL), lambda i: (i, 0))
    vec = pl.BlockSpec((1, D_MODEL), lambda i: (0, 0))
    wsp = pl.BlockSpec((N_CHIPS, D_MODEL, 512), lambda i: (0, 0, 0))
    return _call_with_gather(
        body, name="b_in", grid=(S // tm,),
        in_specs=[row, vec, vec, vec, vec, wsp, wsp],
        out_specs=[row] * 8,
        out_shape=[jax.ShapeDtypeStruct((S, D_MODEL), BF16)] * 8,
        args=(x, kv_gain, b_gain, k_gain_t, q_gain_t, w_kv, w_in), gather=gather, vmem_mib=56)


def _softplus_parts(z):
    e = jnp.exp(-jnp.abs(z))
    return -(jnp.maximum(z, 0.0) + jnp.log(1.0 + e)), e


def _add_rows(total, first_row, update):
    if first_row == 0:
        return total + update
    return jnp.concatenate([total[:first_row], total[first_row:] + update], axis=0)


def _attn_fwd(q, k, v, zgate):
    S = q.shape[0]
    tq, tk = ATT_Q_TILE, ATT_K_TILE
    kpq = tq // tk

    def body(q_ref, k_ref, v_ref, z_ref, o_ref, g_ref, lt_ref, steps_ref):
        qi = pl.program_id(1)
        lane = lax.broadcasted_iota(jnp.int32, (1, 128), 1)
        ri = lax.broadcasted_iota(jnp.int32, (tk, tk), 0)
        ci = lax.broadcasted_iota(jnp.int32, (tk, tk), 1)
        later_mat = _mask_bf16(ri > ci)
        t_idx = qi * tq + lax.broadcasted_iota(jnp.int32, (tq, tk), 0)
        s_off = lax.broadcasted_iota(jnp.int32, (tq, tk), 1)
        qv = q_ref[...]
        first = lane < HEAD_DIM
        q_heads = (jnp.where(first, qv, jnp.zeros_like(qv)), jnp.where(first, jnp.zeros_like(qv), qv))

        def step(kj_last, carry, masked):
            chains = [(d, h) for d in range(kpq) for h in range(2)]
            r0 = [(kpq - 1 - d) * tk if masked else 0 for d in range(kpq)]
            s0 = [pl.multiple_of((kj_last - d) * tk, tk) for d in range(kpq)]
            kb = [k_ref[pl.ds(s, tk), :] for s in s0]
            vb = [v_ref[pl.ds(s, tk), :] for s in s0]
            visible = [(s + s_off < t_idx)[r:] for s, r in zip(s0, r0)] if masked else None
            z = {c: _dot_nt(q_heads[c[1]][r0[c[0]]:], kb[c[0]]) for c in chains}
            run = [carry[0], carry[2]]
            log_own, later, run_at = {}, {}, {}
            for c in chains:
                kj, h = c
                lk = _softplus_parts(z[c])[0]
                if masked:
                    lk = jnp.where(visible[kj], lk, 0.0)
                log_own[c] = z[c] + lk
                later[c] = _dot(lk.astype(BF16), later_mat)
                run_at[c] = run[h][r0[kj]:]
                run[h] = _add_rows(run[h], r0[kj], jnp.sum(lk, axis=-1, keepdims=True))
            acc = [carry[1], carry[3]]
            for c in chains:
                kj, h = c
                a = jnp.exp(log_own[c] + later[c] + run_at[c])
                if masked:
                    a = jnp.where(visible[kj], a, 0.0)
                acc[h] = _add_rows(acc[h], r0[kj], _dot(a.astype(BF16), vb[kj]))
            return run[0], acc[0], run[1], acc[1]

        zero1, zero128 = jnp.zeros((tq, 1), F32), jnp.zeros((tq, 128), F32)
        carry = step(qi * kpq + kpq - 1, (zero1, zero128, zero1, zero128), True)

        def more(c):
            return (c[0] < qi) & (jnp.maximum(jnp.max(c[1]), jnp.max(c[3])) > EXP_UNDERFLOW)

        done, *carry = lax.while_loop(more, lambda c: (c[0] + 1, *step((qi - c[0]) * kpq - 1, c[1:], False)),
                                      (jnp.int32(0), *carry))
        steps_ref[...] = jnp.full(steps_ref.shape, done, F32)
        o_tot = jnp.where(first, carry[1], carry[3])
        l_tot = jnp.where(first, carry[0], carry[2])
        o_ref[...] = o_tot.astype(BF16)
        lt_ref[...] = l_tot
        zz = z_ref[...].astype(F32)
        g_ref[...] = (o_tot * (zz * _sigmoid(zz))).astype(BF16)

    blk = pl.BlockSpec((tq, 128), lambda hp, qi: (qi, hp))
    seq = pl.BlockSpec((S, 128), lambda hp, qi: (0, hp))
    return pl.pallas_call(
        body, name="attn_fwd", grid=(D_MODEL // 128, S // tq),
        in_specs=[blk, seq, seq, blk],
        out_specs=[blk, blk, blk, pl.BlockSpec((None, None, 8, 128), lambda hp, qi: (hp, qi, 0, 0))],
        out_shape=[jax.ShapeDtypeStruct((S, D_MODEL), BF16)] * 2 + [jax.ShapeDtypeStruct((S, D_MODEL), F32)]
        + [jax.ShapeDtypeStruct((D_MODEL // 128, S // tq, 8, 128), F32)],
        compiler_params=_params(("parallel", "arbitrary")),
    )(q, k, v, zgate)


def _ple_out_bwd(name, dx_out, e, gate, ple_g, w_out):
    S = dx_out.shape[0]
    tm = ROW_TILE

    def body(dx_ref, e_ref, gt_ref, pg_ref, wo_ref, de_ref, dgp_ref, dxm_ref, dg_ref):
        dxo = dx_ref[...]
        ev = e_ref[...].astype(F32)
        gv = gt_ref[...].astype(F32)
        de_ref[...] = (dxo * gv).astype(BF16)
        dgp = (dxo * ev * gv * (1.0 - gv)).astype(BF16)
        dgp_ref[...] = dgp
        pg = jnp.concatenate([pg_ref[sh] for sh in range(N_CHIPS)], axis=0)
        dxm = dxo + _dot_nt(dgp, pg)
        dxm_ref[...] = dxm
        dg_ref[...] = _dot_nt(dxm.astype(BF16), wo_ref[...]).astype(BF16)

    row = pl.BlockSpec((tm, D_MODEL), lambda i: (i, 0))
    return pl.pallas_call(
        body, name=name, grid=(S // tm,),
        in_specs=[row, row, row,
                  pl.BlockSpec((N_CHIPS, 256, D_MODEL), lambda i: (0, 0, 0)),
                  pl.BlockSpec((D_MODEL, D_MODEL), lambda i: (0, 0))],
        out_specs=[row, row, row, row],
        out_shape=[jax.ShapeDtypeStruct((S, D_MODEL), BF16), jax.ShapeDtypeStruct((S, D_MODEL), BF16),
                   jax.ShapeDtypeStruct((S, D_MODEL), F32), jax.ShapeDtypeStruct((S, D_MODEL), BF16)],
        compiler_params=_params(("arbitrary",)),
    )(dx_out, e, gate, ple_g, w_out)


def _attn_bwd(q, k, v, ltot, steps, dgated, o, zgate, reduce=None):
    S = q.shape[0]
    tq, tk = ATT_Q_TILE, ATT_K_TILE
    kpq = tq // tk
    nq = S // tq

    def body(q_ref, k_ref, v_ref, lt_ref, steps_ref, dg_ref, o_ref, z_ref, dq_ref, dk_ref, dv_ref, dz_ref,
             dk_acc, dv_acc):
        qi = pl.program_id(1)

        @pl.when(qi == 0)
        def _():
            dk_acc[...] = jnp.zeros_like(dk_acc)
            dv_acc[...] = jnp.zeros_like(dv_acc)

        lane = lax.broadcasted_iota(jnp.int32, (1, 128), 1)
        ri = lax.broadcasted_iota(jnp.int32, (tk, tk), 0)
        ci = lax.broadcasted_iota(jnp.int32, (tk, tk), 1)
        later_mat = _mask_bf16(ri > ci)
        before_mat = _mask_bf16(ri < ci)
        t_idx = qi * tq + lax.broadcasted_iota(jnp.int32, (tq, tk), 0)
        s_off = lax.broadcasted_iota(jnp.int32, (tq, tk), 1)
        zz = z_ref[...].astype(F32)
        sg = _sigmoid(zz)
        dgv = dg_ref[...].astype(F32)
        dz_ref[...] = (dgv * o_ref[...].astype(F32) * _dsilu(zz, sg)).astype(BF16)
        dob = (dgv * (zz * sg)).astype(BF16)
        ltv = lt_ref[...]
        qv = q_ref[...]
        first = lane < HEAD_DIM
        masks = (first, jnp.logical_not(first))
        q_heads = [jnp.where(hm, qv, jnp.zeros_like(qv)) for hm in masks]
        do_heads = [jnp.where(hm, dob, jnp.zeros_like(dob)) for hm in masks]
        totals = [jnp.max(jnp.where(hm, ltv, -jnp.inf), axis=-1, keepdims=True) for hm in masks]

        def step(kj_first, carry, masked):
            chains = [(d, h) for d in range(kpq) for h in range(2)]
            r0 = [d * tk if masked else 0 for d in range(kpq)]
            s0 = [pl.multiple_of((kj_first + d) * tk, tk) for d in range(kpq)]
            kb = [k_ref[pl.ds(s, tk), :] for s in s0]
            vb = [v_ref[pl.ds(s, tk), :] for s in s0]
            visible = [(s + s_off < t_idx)[r:] for s, r in zip(s0, r0)] if masked else None
            z = {c: _dot_nt(q_heads[c[1]][r0[c[0]]:], kb[c[0]]) for c in chains}
            da = {c: _dot_nt(do_heads[c[1]][r0[c[0]]:], vb[c[0]]) for c in chains}
            run = [carry[0], carry[3]]
            log_own, beta, later, base = {}, {}, {}, {}
            for c in chains:
                kj, h = c
                lk = _softplus_parts(z[c])[0]
                if masked:
                    lk = jnp.where(visible[kj], lk, 0.0)
                log_own[c] = z[c] + lk
                beta[c] = jnp.exp(log_own[c]).astype(BF16)
                later[c] = _dot(lk.astype(BF16), later_mat)
                run[h] = _add_rows(run[h], r0[kj], jnp.sum(lk, axis=-1, keepdims=True))
                base[c] = (totals[h] - run[h])[r0[kj]:]
            grun = [carry[1], carry[4]]
            a_bf, g_bf, gbefore, grun_at = {}, {}, {}, {}
            for c in chains:
                kj, h = c
                a = jnp.exp(log_own[c] + later[c] + base[c])
                if masked:
                    a = jnp.where(visible[kj], a, 0.0)
                a_bf[c] = a.astype(BF16)
                g = da[c] * a
                g_bf[c] = g.astype(BF16)
                gbefore[c] = _dot(g_bf[c], before_mat)
                grun_at[c] = grun[h][r0[kj]:]
                grun[h] = _add_rows(grun[h], r0[kj], jnp.sum(g, axis=-1, keepdims=True))
            dq = [carry[2], carry[5]]
            dk_blk = [jnp.zeros((tk, 128), F32) for _ in range(kpq)]
            dv_blk = [jnp.zeros((tk, 128), F32) for _ in range(kpq)]
            for c in chains:
                kj, h = c
                g = g_bf[c].astype(F32)
                dz = g - beta[c].astype(F32) * (g + gbefore[c] + grun_at[c])
                if masked:
                    dz = jnp.where(visible[kj], dz, 0.0)
                dzb = dz.astype(BF16)
                dq[h] = _add_rows(dq[h], r0[kj], _dot(dzb, kb[kj]))
                dk_blk[kj] = dk_blk[kj] + _dot_tn(dzb, q_heads[h][r0[kj]:])
                dv_blk[kj] = dv_blk[kj] + _dot_tn(a_bf[c], do_heads[h][r0[kj]:])
            for d in range(kpq):
                dk_acc[pl.ds(s0[d], tk), :] += dk_blk[d]
                dv_acc[pl.ds(s0[d], tk), :] += dv_blk[d]
            return run[0], grun[0], dq[0], run[1], grun[1], dq[1]

        taken = jnp.clip(jnp.max(steps_ref[...]).astype(jnp.int32), 0, qi)
        zero1, zero128 = jnp.zeros((tq, 1), F32), jnp.zeros((tq, 128), F32)
        carry = lax.fori_loop(qi - taken, qi, lambda n, c: step(n * kpq, c, False),
                              (zero1, zero1, zero128, zero1, zero1, zero128))
        carry = step(qi * kpq, carry, True)
        dq_ref[...] = jnp.where(first, carry[2], carry[5]).astype(BF16)

        @pl.when(qi == nq - 1)
        def _():
            dk_ref[...] = dk_acc[...].astype(BF16)
            dv_ref[...] = dv_acc[...].astype(BF16)

    blk = pl.BlockSpec((tq, 128), lambda hp, qi: (qi, hp))
    seq = pl.BlockSpec((S, 128), lambda hp, qi: (0, hp))
    return _call_with_gather(
        body, name="attn_bwd", grid=(D_MODEL // 128, nq),
        in_specs=[blk, seq, seq, blk, pl.BlockSpec((None, None, 8, 128), lambda hp, qi: (hp, qi, 0, 0)),
                  blk, blk, blk],
        out_specs=[blk, seq, seq, blk],
        out_shape=[jax.ShapeDtypeStruct((S, D_MODEL), BF16)] * 4,
        scratch_shapes=[pltpu.VMEM((S, 128), F32), pltpu.VMEM((S, 128), F32)],
        args=(q, k, v, ltot, steps, dgated, o, zgate), reduce=reduce, vmem_mib=56)


def _rms_bwd(xv, dh_gain_sum):
    r = lax.rsqrt(jnp.mean(xv * xv, axis=-1, keepdims=True) + EPS)
    xhat = xv * r
    dx = r * (dh_gain_sum - xhat * jnp.mean(dh_gain_sum * xhat, axis=-1, keepdims=True))
    return dx, xhat


def _b_in_bwd(dq, dk, dv, dz, q_raw, k_raw, x, dx_mid, q_gain_t, k_gain_t, b_gain, kv_gain, w_in, w_kv):
    S = x.shape[0]
    tm = ROW_TILE

    def body(dq_ref, dk_ref, dv_ref, dz_ref, qr_ref, kr_ref, x_ref, dxm_ref, qg_ref, kg_ref, bg_ref, kvg_ref,
             win_ref, wkv_ref, dqz_ref, dkv_ref, dx_ref, small_ref):
        @pl.when(pl.program_id(0) == 0)
        def _():
            small_ref[...] = jnp.zeros_like(small_ref)

        bd = _head_mean_matrix()

        def head_norm_bwd(dy_ref, raw_ref, gain, scale):
            raw = raw_ref[...].astype(F32)
            rr = lax.rsqrt(_head_mean(raw * raw, bd) + EPS)
            xhat = raw * rr
            dy = dy_ref[...].astype(F32) * scale
            gdy = dy * gain
            draw = rr * (gdy - xhat * _head_mean(gdy * xhat, bd))
            return draw.astype(BF16), jnp.sum(dy * xhat, axis=0, keepdims=True)

        dqr, dqg = head_norm_bwd(dq_ref, qr_ref, qg_ref[...], SB_SCALE)
        dkr, dkg = head_norm_bwd(dk_ref, kr_ref, kg_ref[...], 1.0)
        dqz_ref[:, :D_MODEL] = dqr
        dqz_ref[:, D_MODEL:] = dz_ref[...]
        dkv_ref[:, :D_MODEL] = dkr
        dkv_ref[:, D_MODEL:] = dv_ref[...]
        dhb = jnp.zeros((tm, D_MODEL), F32)
        dhkv = jnp.zeros((tm, D_MODEL), F32)
        for sh in range(N_CHIPS):
            cols = slice(sh * 512, (sh + 1) * 512)
            dhb = dhb + _dot_nt(dqz_ref[:, cols], win_ref[sh])
            dhkv = dhkv + _dot_nt(dkv_ref[:, cols], wkv_ref[sh])
        dx, xhat = _rms_bwd(x_ref[...], dhb * bg_ref[...] + dhkv * kvg_ref[...])
        dx_ref[...] = dxm_ref[...] + dx
        small_ref[0:1, :] += dqg
        small_ref[1:2, :] += dkg
        small_ref[2:3, :] += jnp.sum(dhb * xhat, axis=0, keepdims=True)
        small_ref[3:4, :] += jnp.sum(dhkv * xhat, axis=0, keepdims=True)

    row = pl.BlockSpec((tm, D_MODEL), lambda i: (i, 0))
    wide = pl.BlockSpec((tm, 2 * D_MODEL), lambda i: (i, 0))
    vec = pl.BlockSpec((1, D_MODEL), lambda i: (0, 0))
    wsp = pl.BlockSpec((N_CHIPS, D_MODEL, 512), lambda i: (0, 0, 0))
    return pl.pallas_call(
        body, name="b_in_bwd", grid=(S // tm,),
        in_specs=[row] * 8 + [vec] * 4 + [wsp, wsp],
        out_specs=[wide, wide, row, pl.BlockSpec((8, D_MODEL), lambda i: (0, 0))],
        out_shape=[jax.ShapeDtypeStruct((S, 2 * D_MODEL), BF16), jax.ShapeDtypeStruct((S, 2 * D_MODEL), BF16),
                   jax.ShapeDtypeStruct((S, D_MODEL), F32), jax.ShapeDtypeStruct((8, D_MODEL), F32)],
        compiler_params=_params(("arbitrary",), 56),
    )(dq, dk, dv, dz, q_raw, k_raw, x, dx_mid, q_gain_t, k_gain_t, b_gain, kv_gain, w_in, w_kv)


def _a_mix_bwd(dgated, uz, pooled, wg, scale, w_in, x, dx_mid, gain, reduce=None):
    S = x.shape[0]
    tm = ROW_TILE
    n = S // tm

    def body(dg_ref, z_ref, p_ref, wg_ref, sc_ref, win_ref, x_ref, dxm_ref, gn_ref,
             duz_ref, dmr_ref, dx_ref, small_ref, halo_hi, halo_lo):
        i = pl.program_id(0)

        @pl.when(i == 0)
        def _():
            small_ref[...] = jnp.zeros_like(small_ref)
            halo_hi[...] = jnp.zeros_like(halo_hi)
            halo_lo[...] = jnp.zeros_like(halo_lo)

        first_row = (n - 1 - i) * tm
        row = lax.broadcasted_iota(jnp.int32, (tm, tm), 0)
        col = lax.broadcasted_iota(jnp.int32, (tm, tm), 1)
        d = col - row
        for g, w in enumerate(POOL_WINDOWS):
            cols = slice(g * GROUP_DIM, (g + 1) * GROUP_DIM)
            wgg = _group_weight(wg_ref, g)
            sc = sc_ref[:, cols]
            mraw = _dot(p_ref[:, cols], wgg)
            z = z_ref[:, cols]
            sg = _sigmoid(z)
            dga = dg_ref[:, cols].astype(F32)
            dm = dga * (z * sg)
            duz_ref[:, D_MODEL + g * GROUP_DIM:D_MODEL + (g + 1) * GROUP_DIM] = (
                dga * (mraw * sc) * _dsilu(z, sg)).astype(BF16)
            small_ref[0:1, cols] += jnp.sum(dm * mraw, axis=0, keepdims=True)
            dmr = (dm * sc).astype(BF16)
            dmr_ref[:, cols] = dmr
            dp = _dot_nt(dmr, wgg)
            hi, lo = _hilo(dp * _inv_count(first_row, tm, w))
            t_main = _mask_bf16((d >= 0) & (d < w))
            t_halo = _mask_bf16(d + tm < w)
            du = (_dot(t_main, hi) + _dot(t_main, lo) + _dot(t_halo, halo_hi[:, cols]) + _dot(t_halo, halo_lo[:, cols])
                  - dp)
            halo_hi[:, cols] = hi
            halo_lo[:, cols] = lo
            duz_ref[:, cols] = du.astype(BF16)
        dh = jnp.zeros((tm, D_MODEL), F32)
        for sh in range(N_CHIPS):
            dh = dh + _dot_nt(duz_ref[:, sh * 512:(sh + 1) * 512], win_ref[sh])
        dx, xhat = _rms_bwd(x_ref[...], dh * gn_ref[...])
        dx_ref[...] = dxm_ref[...] + dx
        small_ref[1:2, :] += jnp.sum(dh * xhat, axis=0, keepdims=True)

    rev = lambda i: (n - 1 - i, 0)
    row = pl.BlockSpec((tm, D_MODEL), rev)
    vec = pl.BlockSpec((1, D_MODEL), lambda i: (0, 0))
    return _call_with_gather(
        body, name="a_mix_bwd", grid=(n,),
        in_specs=[row,
                  pl.BlockSpec((tm, D_MODEL), lambda i: (n - 1 - i, 1)),
                  row,
                  pl.BlockSpec((N_CHIPS, N_GROUPS, 64, GROUP_DIM), lambda i: (0, 0, 0, 0)),
                  vec,
                  pl.BlockSpec((N_CHIPS, D_MODEL, 512), lambda i: (0, 0, 0)),
                  row, row, vec],
        out_specs=[pl.BlockSpec((tm, 2 * D_MODEL), rev), row, row,
                   pl.BlockSpec((8, D_MODEL), lambda i: (0, 0))],
        out_shape=[jax.ShapeDtypeStruct((S, 2 * D_MODEL), BF16), jax.ShapeDtypeStruct((S, D_MODEL), BF16),
                   jax.ShapeDtypeStruct((S, D_MODEL), F32), jax.ShapeDtypeStruct((8, D_MODEL), F32)],
        scratch_shapes=[pltpu.VMEM((tm, D_MODEL), BF16), pltpu.VMEM((tm, D_MODEL), BF16)],
        args=(dgated, uz, pooled, wg, scale, w_in, x, dx_mid, gain), reduce=reduce, vmem_mib=56)


def _wgrad(name, a, dy, n_shards, a_spec=None, k_dim=None):
    S, n_cols = dy.shape
    ts = WGRAD_SEQ_TILE
    k_dim = a.shape[-1] if k_dim is None else k_dim
    wn = n_cols // n_shards
    tk = min(k_dim, WGRAD_ACC_BYTES // (4 * n_cols))
    nst = S // ts

    def body(a_ref, dy_ref, out_ref, acc):
        st = pl.program_id(1)

        @pl.when(st == 0)
        def _():
            acc[...] = jnp.zeros_like(acc)

        acc[...] += _dot_tn(a_ref[...].astype(BF16), dy_ref[...].astype(BF16))

        @pl.when(st == nst - 1)
        def _():
            for sh in range(n_shards):
                out_ref[sh] = acc[:, sh * wn:(sh + 1) * wn]

    if a_spec is None:
        a_spec = pl.BlockSpec((ts, tk), lambda kt, st: (st, kt))
    return pl.pallas_call(
        body, name=name, grid=(k_dim // tk, nst),
        in_specs=[a_spec, pl.BlockSpec((ts, n_cols), lambda kt, st: (st, 0))],
        out_specs=pl.BlockSpec((n_shards, tk, wn), lambda kt, st: (0, kt, 0)),
        out_shape=jax.ShapeDtypeStruct((n_shards, k_dim, wn), F32),
        scratch_shapes=[pltpu.VMEM((tk, n_cols), F32)],
        compiler_params=_params(("parallel", "arbitrary")),
    )(a, dy)


def _wgrad_ple(name, p, layer, de):
    ts = WGRAD_SEQ_TILE
    spec = pl.BlockSpec((None, None, ts, PLE_DIM), lambda kt, st: (layer, 0, st, 0))
    return _wgrad(name, p, de, N_CHIPS, a_spec=spec, k_dim=PLE_DIM)


def _wgrad_group(pooled, dmr):
    S = pooled.shape[0]
    ts = WGRAD_SEQ_TILE
    nst = S // ts

    def body(p_ref, d_ref, out_ref, acc):
        st = pl.program_id(1)

        @pl.when(st == 0)
        def _():
            acc[...] = jnp.zeros_like(acc)

        acc[...] += _dot_tn(p_ref[...], d_ref[...])

        @pl.when(st == nst - 1)
        def _():
            for sh in range(N_CHIPS):
                out_ref[sh] = acc[sh * 64:(sh + 1) * 64, :]

    blk = pl.BlockSpec((ts, GROUP_DIM), lambda g, st: (st, g))
    return pl.pallas_call(
        body, name="wgrad_group", grid=(N_GROUPS, nst),
        in_specs=[blk, blk],
        out_specs=pl.BlockSpec((N_CHIPS, None, 64, GROUP_DIM), lambda g, st: (0, g, 0, 0)),
        out_shape=jax.ShapeDtypeStruct((N_CHIPS, N_GROUPS, 64, GROUP_DIM), F32),
        scratch_shapes=[pltpu.VMEM((GROUP_DIM, GROUP_DIM), F32)],
        compiler_params=_params(("parallel", "arbitrary")),
    )(pooled, dmr)


GATHER_AT = {
    "a_in": ("a_w_group", "a_w_out", "ple_w0", "ple_gate_w0"),
    "a_mix": ("w_kv",),
    "a_out_ple": ("b_w_in",),
    "b_in": ("b_w_out", "ple_w1", "ple_gate_w1"),
}


REDUCE_AT = {
    "attn_bwd": ("b_w_out", "ple_w1", "ple_gate_w1"),
    "a_mix_bwd": ("w_kv", "b_w_in"),
}


def _local_step(x, p, target, w, local=None, state=None):
    w = dict(w)

    def run(fn, host, n_out, *args, **kwargs):
        names = GATHER_AT[host] if local is not None else ()
        res = fn(*args, gather=[local[n] for n in names], **kwargs)
        w.update(zip(names, res[n_out:]))
        return res[:n_out]

    k_gain_t = jnp.tile(w["k_norm"].reshape(1, HEAD_DIM), (1, N_HEADS))
    q_gain_t = jnp.tile(w["b_q_norm"].reshape(1, HEAD_DIM), (1, N_HEADS))

    uz, h_a = run(_a_in, "a_in", 2, x, w["a_norm"], w["a_w_in"])
    wg4 = w["a_w_group"].reshape(N_CHIPS, N_GROUPS, 64, GROUP_DIM)
    wa_out = w["a_w_out"].reshape(D_MODEL, D_MODEL)
    gated_a, pooled = run(_a_mix, "a_mix", 2, uz, wg4, w["a_scale"])
    x1, x2, e_a, gate_a = run(_out_ple, "a_out_ple", 4, "a_out_ple", gated_a, x, wa_out, p, 0,
                              w["ple_w0"], w["ple_gate_w0"])
    h_kv, h_b, k_raw, q_raw, k, q, v, z_b = run(
        _b_in, "b_in", 8, x2, w["kv_norm"], w["b_norm"], k_gain_t, q_gain_t, w["w_kv"], w["b_w_in"])
    wb_out = w["b_w_out"].reshape(D_MODEL, D_MODEL)
    o, gated_b, ltot, att_steps = _attn_fwd(q, k, v, z_b)
    x3, dx4, e_b, gate_b, loss_blk = _out_ple("b_out_ple", gated_b, x2, wb_out, p, 1, w["ple_w1"], w["ple_gate_w1"],
                                              target=target)

    grads, updates = {}, {}

    def hosted(fn, host, n_out, *args):
        if state is None:
            return fn(*args)
        names = REDUCE_AT[host]
        res = fn(*args, reduce=([grads.pop(n) for n in names], *[[t[n] for n in names] for t in state]))
        for i, n in enumerate(names):
            updates[n] = tuple(group[i] for group in res[n_out:])
        return res[:n_out]

    de_b, dgp_b, dx3, dgated_b = _ple_out_bwd("b_ple_out_bwd", dx4, e_b, gate_b, w["ple_gate_w1"], wb_out)
    grads["b_w_out"] = _wgrad("wgrad_b_out", gated_b, dx3, 1).reshape(N_CHIPS, 256, D_MODEL)
    grads["ple_w1"] = _wgrad_ple("wgrad_ple1", p, 1, de_b)
    grads["ple_gate_w1"] = _wgrad("wgrad_gate1", x3, dgp_b, 1).reshape(N_CHIPS, 256, D_MODEL)
    dq, dk, dv, dz_b = hosted(_attn_bwd, "attn_bwd", 4, q, k, v, ltot, att_steps, dgated_b, o, z_b)
    dqz, dkv, dx2, small_b = _b_in_bwd(dq, dk, dv, dz_b, q_raw, k_raw, x2, dx3, q_gain_t, k_gain_t,
                                       w["b_norm"], w["kv_norm"], w["b_w_in"], w["w_kv"])
    grads["w_kv"] = _wgrad("wgrad_kv", h_kv, dkv, N_CHIPS)
    grads["b_w_in"] = _wgrad("wgrad_b_in", h_b, dqz, N_CHIPS)
    de_a, dgp_a, dx1, dgated_a = _ple_out_bwd("a_ple_out_bwd", dx2, e_a, gate_a, w["ple_gate_w0"], wa_out)
    grads["a_w_out"] = _wgrad("wgrad_a_out", gated_a, dx1, 1).reshape(N_CHIPS, 256, D_MODEL)
    grads["ple_w0"] = _wgrad_ple("wgrad_ple0", p, 0, de_a)
    grads["ple_gate_w0"] = _wgrad("wgrad_gate0", x1, dgp_a, 1).reshape(N_CHIPS, 256, D_MODEL)
    duz, dmr, grad_x, small_a = hosted(_a_mix_bwd, "a_mix_bwd", 4, dgated_a, uz, pooled, wg4, w["a_scale"],
                                       w["a_w_in"], x, dx1, w["a_norm"])
    grads["a_w_in"] = _wgrad("wgrad_a_in", h_a, duz, N_CHIPS)
    grads["a_w_group"] = _wgrad_group(pooled, dmr).reshape(N_CHIPS, N_GROUPS * 64, GROUP_DIM)

    fold = lambda row: jnp.pad(row.reshape(N_HEADS, HEAD_DIM).sum(axis=0), (0, D_MODEL - HEAD_DIM))
    small = jnp.stack([small_a[1], small_a[0], small_b[3], small_b[2], fold(small_b[1]), fold(small_b[0]),
                       jnp.pad(loss_blk[0], (0, D_MODEL - loss_blk.shape[1])), jnp.zeros((D_MODEL,), F32)])
    return grad_x, grads, updates, small


def _mesh_place():
    x, y, c = lax.axis_index("x"), lax.axis_index("y"), lax.axis_index("c")
    other_chips = [(1 - x, y), (x, 1 - y), (1 - x, 1 - y)]
    return x, y, c, other_chips


def _gather_sems(n):
    return [pltpu.SemaphoreType.DMA((3 * n,)), pltpu.SemaphoreType.DMA((3 * n,)),
            pltpu.SemaphoreType.DMA((3 * n,)), pltpu.SemaphoreType.DMA((3 * n,)), pltpu.SemaphoreType.DMA((n,))]


def _gather_copies(srcs, outs, sems):
    send_far, recv_far, send_sib, recv_sib, local_sem = sems
    n = len(srcs)
    x, y, c, chips = _mesh_place()
    me = 2 * x + y
    sibling = (x, y, 1 - c)

    def half(k, which):
        rows = srcs[k].shape[0] // 2
        return pl.ds(pl.multiple_of(which * rows, 16), rows)

    local = [pltpu.make_async_copy(srcs[k], outs[k].at[me], local_sem.at[k]) for k in range(n)]
    far = [pltpu.make_async_remote_copy(
        src_ref=srcs[k].at[half(k, c)], dst_ref=outs[k].at[me, half(k, c)],
        send_sem=send_far.at[j * n + k], recv_sem=recv_far.at[j * n + k], device_id=(px, py, c), device_id_type=MESH)
        for j, (px, py) in enumerate(chips) for k in range(n)]

    def landed(j, k, which, from_far):
        px, py = chips[j]
        piece = outs[k].at[2 * px + py, half(k, which)]
        send, recv = (send_far, recv_far) if from_far else (send_sib, recv_sib)
        return pltpu.make_async_remote_copy(src_ref=piece, dst_ref=piece, send_sem=send.at[j * n + k],
                                            recv_sem=recv.at[j * n + k], device_id=sibling, device_id_type=MESH)

    return local, far, landed, c


def _gather_start(srcs, outs, sems):
    local, far, _, _ = _gather_copies(srcs, outs, sems)
    for cp in local + far:
        cp.start()


def _gather_pass_on(srcs, outs, sems):
    _, _, landed, c = _gather_copies(srcs, outs, sems)
    for j in range(3):
        for k in range(len(srcs)):
            landed(j, k, c, True).wait_recv()
            landed(j, k, c, False).start()


def _gather_finish(srcs, outs, sems):
    local, far, landed, c = _gather_copies(srcs, outs, sems)
    pairs = [(j, k) for j in range(3) for k in range(len(srcs))]
    for j, k in pairs:
        landed(j, k, 1 - c, False).wait_recv()
    for cp in far + [landed(j, k, c, False) for j, k in pairs]:
        cp.wait_send()
    for cp in local:
        cp.wait()


def _call_with_gather(body, *, name, grid, in_specs, out_specs, out_shape, args, gather=(), reduce=None,
                      scratch_shapes=(), vmem_mib=48):
    n_in, n_out, n_scr, n_g = len(args), len(out_shape), len(scratch_shapes), len(gather)
    n_r = len(reduce[0]) if reduce else 0
    pieces = _reduce_pieces(reduce[0]) if reduce else []
    reduce_args = [a for group in reduce for a in group] if reduce else []
    gather_sems = _gather_sems(n_g) if n_g else []
    n_steps = 1
    for g in grid:
        n_steps *= g

    def wrapped(*refs):
        refs = list(refs)
        take = lambda count: [refs.pop(0) for _ in range(count)]
        ins, g_in, r_in = take(n_in), take(n_g), take(4 * n_r)
        outs, g_out, r_out = take(n_out), take(n_g), take(4 * n_r)
        scratch, sems, r_scratch = take(n_scr), take(len(gather_sems)), refs
        step = 0
        for axis, g in enumerate(grid):
            step = step * g + pl.program_id(axis)
        if n_g:
            @pl.when(step == 0)
            def _():
                _gather_start(g_in, g_out, sems)

        if n_r:
            ticks, drain = _reduce_ticks(pieces, n_r, (*r_in, *r_out, *r_scratch))
            for t, tick in enumerate(ticks[:n_steps]):
                pl.when(step == t)(tick)

        body(*ins, *outs, *scratch)
        if n_r:
            for tick in ticks[n_steps:]:
                pl.when(step == n_steps - 1)(tick)
            pl.when(step == n_steps - 1)(drain)
        if n_g:
            @pl.when(step == max(n_steps - 2, 0))
            def _():
                _gather_pass_on(g_in, g_out, sems)

            @pl.when(step == n_steps - 1)
            def _():
                _gather_finish(g_in, g_out, sems)

    hbm = pl.BlockSpec(memory_space=pltpu.HBM)
    res = pl.pallas_call(
        wrapped, name=name, grid=grid,
        in_specs=list(in_specs) + [hbm] * (n_g + 4 * n_r), out_specs=list(out_specs) + [hbm] * (n_g + 4 * n_r),
        out_shape=list(out_shape) + [jax.ShapeDtypeStruct((N_CHIPS,) + g.shape, BF16) for g in gather]
        + ([jax.ShapeDtypeStruct(w.shape, F32) for _ in range(4) for w in reduce[1]] if reduce else []),
        scratch_shapes=list(scratch_shapes) + gather_sems + (_reduce_scratch() if reduce else []),
        compiler_params=_params(("arbitrary",) * len(grid), vmem_mib),
    )(*args, *gather, *reduce_args)
    if not reduce:
        return res
    plain = list(res[:n_out + n_g])
    return plain + [res[n_out + n_g + i * n_r:n_out + n_g + (i + 1) * n_r] for i in range(4)]


def _allgather_weights(shards, small, casts):
    n = len(shards)
    cast_out = [(k, r0, r1) for k, (_, ranges) in enumerate(casts) for r0, r1 in ranges]
    n_c, n_co = len(casts), len(cast_out)

    def body(*refs):
        ins, small_in, cast_in = refs[:n], refs[n], refs[n + 1:n + 1 + n_c]
        refs = refs[n + 1 + n_c:]
        outs, small_out, cast_dst = refs[:n], refs[n], refs[n + 1:n + 1 + n_co]
        refs = refs[n + 1 + n_co:]
        cast, cast_buf = refs[:n], refs[n:n + n_co]
        send_far, recv_far, send_sib, recv_sib, send_small, recv_small, local_sem, cast_sem = refs[n + n_co:]
        x, y, c, chips = _mesh_place()
        me = 2 * x + y
        sibling = (x, y, 1 - c)

        def half(k, which):
            rows = ins[k].shape[0] // 2
            return pl.ds(pl.multiple_of(which * rows, 16), rows)

        local = []
        for k in range(n):
            cast[k][...] = ins[k][...].astype(BF16)
            local.append(pltpu.make_async_copy(cast[k], outs[k].at[me], local_sem.at[k]))
            local[-1].start()
        local.append(pltpu.make_async_copy(small_in, small_out.at[me], local_sem.at[n]))
        local[-1].start()

        sends = []
        for j, (px, py) in enumerate(chips):
            for k in range(n):
                cp = pltpu.make_async_remote_copy(
                    src_ref=cast[k].at[half(k, c)], dst_ref=outs[k].at[me, half(k, c)],
                    send_sem=send_far.at[j * n + k], recv_sem=recv_far.at[j * n + k],
                    device_id=(px, py, c), device_id_type=MESH)
                cp.start()
                sends.append(cp)
            cp = pltpu.make_async_remote_copy(
                src_ref=small_in, dst_ref=small_out.at[me], send_sem=send_small.at[j], recv_sem=recv_small.at[j],
                device_id=(px, py, c), device_id_type=MESH)
            cp.start()
            sends.append(cp)

        for i, (k, r0, r1) in enumerate(cast_out):
            cast_buf[i][...] = cast_in[k][r0:r1, :].astype(BF16)
            local.append(pltpu.make_async_copy(cast_buf[i], cast_dst[i], cast_sem.at[i]))
            local[-1].start()

        def landed(j, k, which, sems_s, sems_r, device):
            px, py = chips[j]
            piece = outs[k].at[2 * px + py, half(k, which)]
            return pltpu.make_async_remote_copy(
                src_ref=piece, dst_ref=piece, send_sem=sems_s.at[j * n + k], recv_sem=sems_r.at[j * n + k],
                device_id=device, device_id_type=MESH)

        for j in range(len(chips)):
            for k in range(n):
                landed(j, k, c, send_far, recv_far, sibling).wait_recv()
                cp = landed(j, k, c, send_sib, recv_sib, sibling)
                cp.start()
                sends.append(cp)
        for j, (px, py) in enumerate(chips):
            for k in range(n):
                landed(j, k, 1 - c, send_sib, recv_sib, sibling).wait_recv()
            pltpu.make_async_remote_copy(
                src_ref=small_in, dst_ref=small_out.at[2 * px + py], send_sem=send_small.at[j],
                recv_sem=recv_small.at[j], device_id=(px, py, c), device_id_type=MESH).wait_recv()
        for cp in sends:
            cp.wait_send()
        for cp in local:
            cp.wait()

    vmem = pl.BlockSpec(memory_space=pltpu.VMEM)
    hbm = pl.BlockSpec(memory_space=pltpu.HBM)
    cast_shapes = [(r1 - r0, casts[k][0].shape[1]) for k, r0, r1 in cast_out]
    res = pl.pallas_call(
        body, name="allgather_weights",
        in_specs=[vmem] * (n + 1 + n_c), out_specs=[hbm] * (n + 1 + n_co),
        out_shape=[jax.ShapeDtypeStruct((N_CHIPS,) + s.shape, BF16) for s in shards]
        + [jax.ShapeDtypeStruct((N_CHIPS,) + small.shape, F32)]
        + [jax.ShapeDtypeStruct(s, BF16) for s in cast_shapes],
        scratch_shapes=[pltpu.VMEM(s.shape, BF16) for s in shards] + [pltpu.VMEM(s, BF16) for s in cast_shapes]
        + [pltpu.SemaphoreType.DMA((3 * n,)), pltpu.SemaphoreType.DMA((3 * n,)),
           pltpu.SemaphoreType.DMA((3 * n,)), pltpu.SemaphoreType.DMA((3 * n,)),
           pltpu.SemaphoreType.DMA((3,)), pltpu.SemaphoreType.DMA((3,)),
           pltpu.SemaphoreType.DMA((n + 1,)), pltpu.SemaphoreType.DMA((n_co,))],
        compiler_params=_params(None, 40),
    )(*shards, small, *[a for a, _ in casts])
    return res[:n], res[n], res[n + 1:]


def _adamw(w, g, m, v):
    m = ADAM_B1 * m + (1.0 - ADAM_B1) * g
    v = ADAM_B2 * v + (1.0 - ADAM_B2) * (g * g)
    m_hat = m / (1.0 - ADAM_B1 ** ADAM_STEP)
    v_hat = v / (1.0 - ADAM_B2 ** ADAM_STEP)
    delta = -ADAM_LR * (m_hat / (jnp.sqrt(v_hat) + ADAM_EPS) + ADAM_WD * w)
    return delta, m, v


RS_PIECE_ROWS = 128
RS_PIECE_COLS = 512


def _reduce_adam_all(grads, ws, ms, vs):
    n_w = len(grads)
    pieces = _reduce_pieces(grads)

    def body(*refs):
        ticks, drain = _reduce_ticks(pieces, n_w, refs)
        for tick in ticks:
            tick()
        drain()

    hbm = pl.BlockSpec(memory_space=pltpu.HBM)
    outs = pl.pallas_call(
        body, name="reduce_adam_all",
        in_specs=[hbm] * (4 * n_w), out_specs=[hbm] * (4 * n_w),
        out_shape=[jax.ShapeDtypeStruct(w.shape, F32) for _ in range(4) for w in ws],
        scratch_shapes=_reduce_scratch(),
        compiler_params=_params(None, 48),
    )(*grads, *ws, *ms, *vs)
    return [outs[i * n_w:(i + 1) * n_w] for i in range(4)]


def _reduce_pieces(grads):
    pieces = []
    for k, g in enumerate(grads):
        hr, cols = g.shape[1] // 2, g.shape[2]
        pr, pc = min(hr, RS_PIECE_ROWS), min(cols, RS_PIECE_COLS)
        pieces += [(k, ro, hr, co, pr, pc) for ro in range(0, hr, pr) for co in range(0, cols, pc)]
    return pieces


def _reduce_scratch():
    P, C = RS_PIECE_ROWS, RS_PIECE_COLS
    return [
        pltpu.VMEM((3, N_CHIPS, P, C), F32), pltpu.VMEM((3, N_CHIPS, P, C), F32),
        pltpu.VMEM((2, N_CHIPS, P, C), BF16), pltpu.VMEM((2, N_CHIPS, P, C), BF16),
        pltpu.VMEM((2, N_CHIPS, P, C), F32),
        pltpu.VMEM((2, 3, P, C), BF16), pltpu.VMEM((2, 3, P, C), BF16),
        pltpu.VMEM((2, 2, P, C), F32),
        pltpu.VMEM((2, 3, 2, P, C), F32), pltpu.VMEM((2, 4, 2, P, C), F32),
        pltpu.SemaphoreType.DMA((3, 2)), pltpu.SemaphoreType.DMA((2, 3, 2)),
        pltpu.SemaphoreType.DMA((2,)), pltpu.SemaphoreType.DMA((2,)),
        pltpu.SemaphoreType.DMA((2, 3)), pltpu.SemaphoreType.DMA((2, 3)),
        pltpu.SemaphoreType.DMA((2,)), pltpu.SemaphoreType.DMA((2,)),
        pltpu.SemaphoreType.DMA((2, 4, 2))]


def _reduce_ticks(pieces, n_w, refs):
    n = len(pieces)

    def build(*refs):
        g_in, w_in, m_in, v_in = (refs[i * n_w:(i + 1) * n_w] for i in range(4))
        g_out, d_out, m_out, v_out = (refs[(4 + i) * n_w:(5 + i) * n_w] for i in range(4))
        (gm, go, sb1, rb1, part, sb2, rb2, fin, wmv, outs,
         ld_sem, wmv_sem, s1_send, s1_recv, s2_send, s2_recv, s3_send, s3_recv, out_sem) = refs[8 * n_w:]
        x, y, c, chips = _mesh_place()
        me = 2 * x + y
        sibling = (x, y, 1 - c)

        def at_hbm(i, which):
            _, ro, hr, co, pr, pc = pieces[i]
            half = c if which == 0 else 1 - c
            return pl.ds(pl.multiple_of(half * hr + ro, 64), pr), pl.ds(co, pc)

        def win(i):
            return pl.ds(0, pieces[i][4]), pl.ds(0, pieces[i][5])

        every = slice(None)

        def loads(i):
            k, s = pieces[i][0], i % 3
            return [pltpu.make_async_copy(g_in[k].at[(every,) + at_hbm(i, h)], buf.at[(s, every) + win(i)], ld_sem.at[s, h])
                    for h, buf in enumerate((gm, go))]

        def wmv_loads(i):
            k, s = pieces[i][0], i % 2
            return [pltpu.make_async_copy(src[k].at[at_hbm(i, h)], wmv.at[(s, a, h) + win(i)], wmv_sem.at[s, a, h])
                    for a, src in enumerate((w_in, m_in, v_in)) for h in range(2)]

        def stores(i):
            k, s = pieces[i][0], i % 2
            return [pltpu.make_async_copy(outs.at[(s, a, h) + win(i)], dst[k].at[at_hbm(i, h)], out_sem.at[s, a, h])
                    for a, dst in enumerate((g_out, d_out, m_out, v_out)) for h in range(2)]

        def swap1(i):
            s = i % 2
            return pltpu.make_async_remote_copy(
                src_ref=sb1.at[(s, every) + win(i)], dst_ref=rb1.at[(s, every) + win(i)],
                send_sem=s1_send.at[s], recv_sem=s1_recv.at[s], device_id=sibling, device_id_type=MESH)

        def far2(i, j):
            s = i % 2
            px, py = chips[j]
            return pltpu.make_async_remote_copy(
                src_ref=sb2.at[(s, j) + win(i)], dst_ref=rb2.at[(s, j) + win(i)],
                send_sem=s2_send.at[s, j], recv_sem=s2_recv.at[s, j], device_id=(px, py, c), device_id_type=MESH)

        def swap3(i):
            s = i % 2
            return pltpu.make_async_remote_copy(
                src_ref=fin.at[(s, 0) + win(i)], dst_ref=fin.at[(s, 1) + win(i)],
                send_sem=s3_send.at[s], recv_sem=s3_recv.at[s], device_id=sibling, device_id_type=MESH)

        def stage0(i):
            for cp in loads(i):
                cp.start()

        def stage1(i):
            s, s3 = i % 2, i % 3
            for cp in loads(i):
                cp.wait()
            sb1[(s, every) + win(i)] = go[(s3, every) + win(i)].astype(BF16)
            swap1(i).start()

        def stage2(i):
            s, s3 = i % 2, i % 3
            swap1(i).wait()
            part[(s, every) + win(i)] = gm[(s3, every) + win(i)] + rb1[(s, every) + win(i)].astype(F32)
            for j, (px, py) in enumerate(chips):
                sb2[(s, j) + win(i)] = part[(s, 2 * px + py) + win(i)].astype(BF16)
                far2(i, j).start()

        def stage3(i):
            s = i % 2
            total = part[(s, me) + win(i)]
            for j in range(3):
                far2(i, j).wait()
                total = total + rb2[(s, j) + win(i)].astype(F32)
            fin[(s, 0) + win(i)] = total
            swap3(i).start()
            for cp in wmv_loads(i):
                cp.start()

        def stage4(i):
            s = i % 2
            if i >= 2:
                for cp in stores(i - 2):
                    cp.wait()
            swap3(i).wait()
            for cp in wmv_loads(i):
                cp.wait()
            both = (every,) + win(i)
            g = fin[(s,) + both]
            delta, m_new, v_new = _adamw(wmv[(s, 0) + both], g, wmv[(s, 1) + both], wmv[(s, 2) + both])
            outs[(s, 0) + both] = g
            outs[(s, 1) + both] = delta
            outs[(s, 2) + both] = m_new
            outs[(s, 3) + both] = v_new
            for cp in stores(i):
                cp.start()

        stages = (stage0, stage1, stage2, stage3, stage4)

        def tick(t):
            for age in reversed(range(len(stages))):
                if 0 <= t - age < n:
                    stages[age](t - age)

        def drain():
            for i in range(max(0, n - 2), n):
                for cp in stores(i):
                    cp.wait()

        return [functools.partial(tick, t) for t in range(n + len(stages) - 1)], drain

    return build(*refs)


def _allreduce_small(part):
    n_dev = 8

    def body(part_ref, out_ref, buf, send_sem, recv_sem):
        x, y, c, _ = _mesh_place()
        me = 4 * x + 2 * y + c
        buf[me] = part_ref[...]
        sends = []
        for k in range(1, n_dev):
            peer = ((1 - x) if k & 4 else x, (1 - y) if k & 2 else y, (1 - c) if k & 1 else c)
            cp = pltpu.make_async_remote_copy(src_ref=part_ref, dst_ref=buf.at[me], send_sem=send_sem.at[k - 1],
                                              recv_sem=recv_sem.at[k - 1], device_id=peer, device_id_type=MESH)
            cp.start()
            sends.append(cp)
        for cp in sends:
            cp.wait_recv()
        total = buf[0]
        for s in range(1, n_dev):
            total = total + buf[s]
        out_ref[...] = total
        for cp in sends:
            cp.wait_send()

    vmem = pl.BlockSpec(memory_space=pltpu.VMEM)
    return pl.pallas_call(
        body, name="allreduce_small", in_specs=[vmem], out_specs=vmem,
        out_shape=jax.ShapeDtypeStruct(part.shape, F32),
        scratch_shapes=[pltpu.VMEM((n_dev,) + part.shape, F32),
                        pltpu.SemaphoreType.DMA((n_dev - 1,)), pltpu.SemaphoreType.DMA((n_dev - 1,))],
    )(part)


def _adam_small(w, g, m, v):
    def body(w_ref, g_ref, m_ref, v_ref, d_ref, mo_ref, vo_ref):
        delta, m_new, v_new = _adamw(w_ref[...], g_ref[...], m_ref[...], v_ref[...])
        d_ref[...] = delta
        mo_ref[...] = m_new
        vo_ref[...] = v_new

    vmem = pl.BlockSpec(memory_space=pltpu.VMEM)
    return pl.pallas_call(
        body, name="adam_small", in_specs=[vmem] * 4, out_specs=[vmem] * 3,
        out_shape=[jax.ShapeDtypeStruct(w.shape, F32)] * 3,
    )(w, g, m, v)


BIG = ("a_w_in", "a_w_group", "a_w_out", "w_kv", "b_w_in", "b_w_out", "ple_w", "ple_gate_w")
SMALL = ("a_norm", "a_scale", "kv_norm", "b_norm", "k_norm", "b_q_norm")
SMALL_SHARDED = ("a_norm", "a_scale")
WEIGHTS = ("a_norm", "a_w_in", "a_w_group", "a_scale", "a_w_out", "kv_norm", "w_kv", "k_norm", "b_norm", "b_w_in",
           "b_q_norm", "b_w_out", "ple_w", "ple_gate_w")


def _as_matrix(a):
    return a.reshape(-1, a.shape[-1])


def _pack_small(arrs):
    rows = [jnp.pad(a.reshape(-1), (0, D_MODEL - a.size)) for a in arrs]
    rows += [jnp.zeros((D_MODEL,), F32)] * (8 - len(rows))
    return jnp.stack(rows)


def kernel(x, p, a_norm, a_w_in, a_w_group, a_scale, a_w_out, kv_norm, w_kv, k_norm, b_norm, b_w_in, b_q_norm, b_w_out, ple_w, ple_gate_w, loss_target, m_a_norm, m_a_w_in, m_a_w_group, m_a_scale, m_a_w_out, m_kv_norm, m_w_kv, m_k_norm, m_b_norm, m_b_w_in, m_b_q_norm, m_b_w_out, m_ple_w, m_ple_gate_w, v_a_norm, v_a_w_in, v_a_w_group, v_a_scale, v_a_w_out, v_kv_norm, v_w_kv, v_k_norm, v_b_norm, v_b_w_in, v_b_q_norm, v_b_w_out, v_ple_w, v_ple_gate_w):
    wts = dict(a_norm=a_norm, a_w_in=a_w_in, a_w_group=a_w_group, a_scale=a_scale, a_w_out=a_w_out, kv_norm=kv_norm,
               w_kv=w_kv, k_norm=k_norm, b_norm=b_norm, b_w_in=b_w_in, b_q_norm=b_q_norm, b_w_out=b_w_out,
               ple_w=ple_w, ple_gate_w=ple_gate_w)
    mom = dict(a_norm=m_a_norm, a_w_in=m_a_w_in, a_w_group=m_a_w_group, a_scale=m_a_scale, a_w_out=m_a_w_out,
               kv_norm=m_kv_norm, w_kv=m_w_kv, k_norm=m_k_norm, b_norm=m_b_norm, b_w_in=m_b_w_in,
               b_q_norm=m_b_q_norm, b_w_out=m_b_w_out, ple_w=m_ple_w, ple_gate_w=m_ple_gate_w)
    var = dict(a_norm=v_a_norm, a_w_in=v_a_w_in, a_w_group=v_a_w_group, a_scale=v_a_scale, a_w_out=v_a_w_out,
               kv_norm=v_kv_norm, w_kv=v_w_kv, k_norm=v_k_norm, b_norm=v_b_norm, b_w_in=v_b_w_in,
               b_q_norm=v_b_q_norm, b_w_out=v_b_w_out, ple_w=v_ple_w, ple_gate_w=v_ple_gate_w)
    S = x.shape[1]
    chip = 2 * lax.axis_index("x") + lax.axis_index("y")

    sharded_small = jnp.concatenate([a_norm.reshape(1, 256), a_scale.reshape(1, 256), jnp.zeros((6, 256), F32)], axis=0)
    later = ("a_w_group", "a_w_out", "w_kv", "b_w_in", "b_w_out", "ple_w", "ple_gate_w")
    (a_w_in_full,), small_full, copies = _allgather_weights(
        [_as_matrix(a_w_in)], sharded_small,
        [(_as_matrix(wts[n]), [(0, 256), (256, 512)] if n.startswith("ple") else [(0, _as_matrix(wts[n]).shape[0])])
         for n in later])
    local = dict(zip(("a_w_group", "a_w_out", "w_kv", "b_w_in", "b_w_out", "ple_w0", "ple_w1", "ple_gate_w0",
                      "ple_gate_w1"), copies))
    full = dict(a_w_in=a_w_in_full,
                a_norm=small_full[:, 0, :].reshape(1, D_MODEL), a_scale=small_full[:, 1, :].reshape(1, D_MODEL),
                kv_norm=kv_norm.reshape(1, D_MODEL), b_norm=b_norm.reshape(1, D_MODEL), k_norm=k_norm, b_q_norm=b_q_norm)

    def shards(t):
        out = {}
        for n in BIG:
            mat = _as_matrix(t[n])
            if n.startswith("ple"):
                out[n + "0"], out[n + "1"] = mat[:256], mat[256:]
            else:
                out[n] = mat
        return out

    state = (shards(wts), shards(mom), shards(var))
    grad_x, grads, updates, small_part = _local_step(x.reshape(S, D_MODEL), p, loss_target.reshape(S, D_MODEL),
                                                     full, local, state)

    names = sorted(grads)
    reduced = _reduce_adam_all([grads[n] for n in names], *[[t[n] for n in names] for t in state])
    for i, n in enumerate(names):
        updates[n] = tuple(group[i] for group in reduced)
    out_g, out_d, out_m, out_v = {}, {}, {}, {}
    for n in BIG:
        for i, out in enumerate((out_g, out_d, out_m, out_v)):
            if n.startswith("ple"):
                out[n] = jnp.stack([updates[n + "0"][i], updates[n + "1"][i]]).reshape(wts[n].shape)
            else:
                out[n] = updates[n][i].reshape(wts[n].shape)

    small_sum = _allreduce_small(small_part)
    loss = small_sum[len(SMALL), 0]
    small_rows = []
    for i, n in enumerate(SMALL):
        row = small_sum[i]
        if n in SMALL_SHARDED:
            row = lax.dynamic_slice(row, (chip * 256,), (256,))
        else:
            row = row[:wts[n].size]
        small_rows.append(row)
    g_small = _pack_small(small_rows)
    d_small, m_small, v_small = _adam_small(_pack_small([wts[n] for n in SMALL]), g_small,
                                            _pack_small([mom[n] for n in SMALL]), _pack_small([var[n] for n in SMALL]))
    for i, n in enumerate(SMALL):
        shape, size = wts[n].shape, wts[n].size
        out_g[n], out_d[n], out_m[n], out_v[n] = (t[i, :size].reshape(shape) for t in (g_small, d_small, m_small, v_small))

    return (loss, grad_x.reshape(1, S, D_MODEL), *[out_g[n] for n in WEIGHTS], *[out_d[n] for n in WEIGHTS],
            *[out_m[n] for n in WEIGHTS], *[out_v[n] for n in WEIGHTS])
```

```python
import functools

import jax
import jax.numpy as jnp
from jax import lax
from jax.experimental import pallas as pl
from jax.experimental.pallas import tpu as pltpu

F32 = jnp.float32
BF16 = jnp.bfloat16
MESH = pl.DeviceIdType.MESH

D_MODEL = 1024
N_HEADS = 16
HEAD_DIM = 64
PLE_DIM = 256
N_GROUPS = 4
GROUP_DIM = 256
POOL_WINDOWS = (2, 4, 8, 16)
N_CHIPS = 4
EPS = 1e-6
SB_SCALE = HEAD_DIM ** -0.5

ADAM_LR = 0.001
ADAM_B1 = 0.9
ADAM_B2 = 0.999
ADAM_EPS = 1e-08
ADAM_WD = 0.01
ADAM_STEP = 10

ROW_TILE = 256
EXP_UNDERFLOW = -104.0
ATT_Q_TILE = 512
ATT_K_TILE = 256
WGRAD_SEQ_TILE = 1024
WGRAD_ACC_BYTES = 4 * 1024 * 1024
MIB = 1024 * 1024


def _params(semantics=None, vmem_mib=48):
    return pltpu.CompilerParams(dimension_semantics=semantics, vmem_limit_bytes=vmem_mib * MIB)


def _dot(a, b):
    return jnp.dot(a, b, preferred_element_type=F32)


def _dot_nt(a, b):
    return lax.dot_general(a, b, (((1,), (1,)), ((), ())), preferred_element_type=F32)


def _dot_tn(a, b):
    return lax.dot_general(a, b, (((0,), (0,)), ((), ())), preferred_element_type=F32)


def _hilo(x):
    hi = x.astype(BF16)
    lo = (x - hi.astype(F32)).astype(BF16)
    return hi, lo


def _dot_hilo(x, w):
    hi, lo = _hilo(x)
    return _dot(hi, w) + _dot(lo, w)


def _sigmoid(z):
    return jax.nn.sigmoid(z)


def _dsilu(z, sg):
    return sg * (1.0 + z * (1.0 - sg))


def _mask_bf16(cond):
    return jnp.where(cond, 1.0, 0.0).astype(BF16)


def _head_mean_matrix():
    r = lax.broadcasted_iota(jnp.int32, (256, 256), 0) // HEAD_DIM
    c = lax.broadcasted_iota(jnp.int32, (256, 256), 1) // HEAD_DIM
    return _mask_bf16(r == c)


def _head_mean(x, bd):
    parts = []
    for s in range(x.shape[1] // 256):
        parts.append(_dot_hilo(x[:, s * 256:(s + 1) * 256], bd))
    out = parts[0] if len(parts) == 1 else jnp.concatenate(parts, axis=1)
    return out * (1.0 / HEAD_DIM)


def _a_in(x, gain, w_sh, gather=()):
    S = x.shape[0]
    tm = 512
    nsh, _, wn = w_sh.shape

    def body(x_ref, g_ref, w_ref, uz_ref, h_ref):
        @pl.when(pl.program_id(1) == 0)
        def _():
            xv = x_ref[...]
            r = lax.rsqrt(jnp.mean(xv * xv, axis=-1, keepdims=True) + EPS)
            h_ref[...] = (xv * r * g_ref[...]).astype(BF16)

        uz_ref[...] = _dot(h_ref[...], w_ref[0])

    return _call_with_gather(
        body, name="a_in", grid=(S // tm, nsh),
        in_specs=[pl.BlockSpec((tm, D_MODEL), lambda i, j: (i, 0)),
                  pl.BlockSpec((1, D_MODEL), lambda i, j: (0, 0)),
                  pl.BlockSpec((1, D_MODEL, wn), lambda i, j: (j, 0, 0))],
        out_specs=[pl.BlockSpec((tm, wn), lambda i, j: (i, j)),
                   pl.BlockSpec((tm, D_MODEL), lambda i, j: (i, 0))],
        out_shape=[jax.ShapeDtypeStruct((S, nsh * wn), F32),
                   jax.ShapeDtypeStruct((S, D_MODEL), BF16)],
        args=(x, gain, w_sh), gather=gather)


def _inv_count(first_row, rows, w):
    t1 = first_row + 1 + lax.broadcasted_iota(jnp.int32, (rows, 1), 0)
    return 1.0 / jnp.minimum(t1, w).astype(F32)


def _group_weight(wg_ref, g):
    return jnp.concatenate([wg_ref[sh, g] for sh in range(N_CHIPS)], axis=0)


def _a_mix(uz, wg, scale, gather=()):
    S = uz.shape[0]
    tm = ROW_TILE

    def body(u_ref, up_ref, z_ref, wg_ref, sc_ref, ga_ref, p_ref):
        i = pl.program_id(0)
        row = lax.broadcasted_iota(jnp.int32, (tm, tm), 0)
        col = lax.broadcasted_iota(jnp.int32, (tm, tm), 1)
        d = row - col
        for g, w in enumerate(POOL_WINDOWS):
            cols = slice(g * GROUP_DIM, (g + 1) * GROUP_DIM)
            t_main = _mask_bf16((d >= 0) & (d < w))
            t_halo = _mask_bf16(d + tm < w)
            u = u_ref[:, cols]
            up = jnp.where(i > 0, up_ref[:, cols], 0.0)
            hi, lo = _hilo(u)
            hip, lop = _hilo(up)
            wsum = _dot(t_main, hi) + _dot(t_main, lo) + _dot(t_halo, hip) + _dot(t_halo, lop)
            pooled = (wsum * _inv_count(i * tm, tm, w) - u).astype(BF16)
            p_ref[:, cols] = pooled
            mraw = _dot(pooled, _group_weight(wg_ref, g))
            z = z_ref[:, cols]
            ga_ref[:, cols] = (mraw * sc_ref[:, cols] * (z * _sigmoid(z))).astype(BF16)

    return _call_with_gather(
        body, name="a_mix", grid=(S // tm,),
        in_specs=[pl.BlockSpec((tm, D_MODEL), lambda i: (i, 0)),
                  pl.BlockSpec((tm, D_MODEL), lambda i: (jnp.maximum(i - 1, 0), 0)),
                  pl.BlockSpec((tm, D_MODEL), lambda i: (i, 1)),
                  pl.BlockSpec((N_CHIPS, N_GROUPS, 64, GROUP_DIM), lambda i: (0, 0, 0, 0)),
                  pl.BlockSpec((1, D_MODEL), lambda i: (0, 0))],
        out_specs=[pl.BlockSpec((tm, D_MODEL), lambda i: (i, 0)),
                   pl.BlockSpec((tm, D_MODEL), lambda i: (i, 0))],
        out_shape=[jax.ShapeDtypeStruct((S, D_MODEL), BF16),
                   jax.ShapeDtypeStruct((S, D_MODEL), BF16)],
        args=(uz, uz, uz, wg, scale), gather=gather)


def _out_ple(name, gated, x_in, w_out, p, layer, ple_w, ple_g, target=None, gather=()):
    S = x_in.shape[0]
    tm = ROW_TILE
    with_loss = target is not None

    def body(*refs):
        if with_loss:
            g_ref, x_ref, wo_ref, p_ref, pw_ref, pg_ref, t_ref, xm_ref, dx_ref, e_ref, gt_ref, loss_ref = refs
        else:
            g_ref, x_ref, wo_ref, p_ref, pw_ref, pg_ref, xm_ref, xo_ref, e_ref, gt_ref = refs
        xm = x_ref[...] + _dot(g_ref[...], wo_ref[...])
        xm_ref[...] = xm
        pb = p_ref[...].astype(BF16)
        e = jnp.concatenate([_dot(pb, pw_ref[sh]) for sh in range(N_CHIPS)], axis=1)
        pg = jnp.concatenate([pg_ref[sh] for sh in range(N_CHIPS)], axis=0)
        gate = _sigmoid(_dot(xm.astype(BF16), pg))
        e_ref[...] = e.astype(BF16)
        gt_ref[...] = gate.astype(BF16)
        xo = xm + e * gate
        if with_loss:
            diff = xo - t_ref[...]
            dx_ref[...] = diff * (1.0 / D_MODEL)

            @pl.when(pl.program_id(0) == 0)
            def _():
                loss_ref[...] = jnp.zeros_like(loss_ref)

            loss_ref[...] += jnp.sum(diff * diff) * (0.5 / D_MODEL)
        else:
            xo_ref[...] = xo

    row = pl.BlockSpec((tm, D_MODEL), lambda i: (i, 0))
    in_specs = [row, row,
                pl.BlockSpec((D_MODEL, D_MODEL), lambda i: (0, 0)),
                pl.BlockSpec((None, None, tm, PLE_DIM), lambda i: (layer, 0, i, 0)),
                pl.BlockSpec((N_CHIPS, PLE_DIM, 256), lambda i: (0, 0, 0)),
                pl.BlockSpec((N_CHIPS, 256, D_MODEL), lambda i: (0, 0, 0))]
    args = [gated, x_in, w_out, p, ple_w, ple_g]
    out_specs = [row, row, row, row]
    out_shape = [jax.ShapeDtypeStruct((S, D_MODEL), F32), jax.ShapeDtypeStruct((S, D_MODEL), F32),
                 jax.ShapeDtypeStruct((S, D_MODEL), BF16), jax.ShapeDtypeStruct((S, D_MODEL), BF16)]
    if with_loss:
        in_specs.append(row)
        args.append(target)
        out_specs.append(pl.BlockSpec((8, 128), lambda i: (0, 0)))
        out_shape.append(jax.ShapeDtypeStruct((8, 128), F32))
    return _call_with_gather(body, name=name, grid=(S // tm,), in_specs=in_specs, out_specs=out_specs,
                             out_shape=out_shape, args=args, gather=gather)


def _b_in(x, kv_gain, b_gain, k_gain_t, q_gain_t, w_kv, w_in, gather=()):
    S = x.shape[0]
    tm = ROW_TILE

    def body(x_ref, kvg_ref, bg_ref, kg_ref, qg_ref, wkv_ref, win_ref,
             hkv_ref, hb_ref, kraw_ref, qraw_ref, k_ref, q_ref, v_ref, z_ref):
        xv = x_ref[...]
        y = xv * lax.rsqrt(jnp.mean(xv * xv, axis=-1, keepdims=True) + EPS)
        hkv = (y * kvg_ref[...]).astype(BF16)
        hb = (y * bg_ref[...]).astype(BF16)
        hkv_ref[...] = hkv
        hb_ref[...] = hb
        bd = _head_mean_matrix()

        def head_norm(raw, gain):
            rr = lax.rsqrt(_head_mean(raw * raw, bd) + EPS)
            return raw * rr * gain

        for sh in range(N_CHIPS):
            kvc = _dot(hkv, wkv_ref[sh])
            qzc = _dot(hb, win_ref[sh])
            cols = slice((sh % 2) * 512, (sh % 2) * 512 + 512)
            if sh < 2:
                kraw_ref[:, cols] = kvc.astype(BF16)
                qraw_ref[:, cols] = qzc.astype(BF16)
                k_ref[:, cols] = head_norm(kvc, kg_ref[:, cols]).astype(BF16)
                q_ref[:, cols] = (head_norm(qzc, qg_ref[:, cols]) * SB_SCALE).astype(BF16)
            else:
                v_ref[:, cols] = kvc.astype(BF16)
                z_ref[:, cols] = qzc.astype(BF16)

    row = pl.BlockSpec((tm, D_MODEL), lambda i: (i, 0))
    vec = pl.BlockSpec((1, D_MODEL), lambda i: (0, 0))
    wsp = pl.BlockSpec((N_CHIPS, D_MODEL, 512), lambda i: (0, 0, 0))
    return _call_with_gather(
        body, name="b_in", grid=(S // tm,),
        in_specs=[row, vec, vec, vec, vec, wsp, wsp],
        out_specs=[row] * 8,
        out_shape=[jax.ShapeDtypeStruct((S, D_MODEL), BF16)] * 8,
        args=(x, kv_gain, b_gain, k_gain_t, q_gain_t, w_kv, w_in), gather=gather, vmem_mib=56)


def _softplus_parts(z):
    e = jnp.exp(-jnp.abs(z))
    return -(jnp.maximum(z, 0.0) + jnp.log(1.0 + e)), e


def _add_rows(total, rows, update):
    lo, hi = rows
    parts = ([total[:lo]] if lo else []) + [total[lo:hi] + update] + ([total[hi:]] if hi < total.shape[0] else [])
    return parts[0] if len(parts) == 1 else jnp.concatenate(parts, axis=0)


def _attn_fwd(q, k, v, zgate):
    S = q.shape[0]
    tq, tk = ATT_Q_TILE, ATT_K_TILE
    kpq = tq // tk
    assert kpq == 2

    def body(q_ref, k_ref, v_ref, z_ref, o_ref, g_ref, lt_ref, steps_ref):
        qi = pl.program_id(1)
        lane = lax.broadcasted_iota(jnp.int32, (1, 128), 1)
        ri = lax.broadcasted_iota(jnp.int32, (tk, tk), 0)
        ci = lax.broadcasted_iota(jnp.int32, (tk, tk), 1)
        later_mat = _mask_bf16(ri > ci)
        t_idx = qi * tq + lax.broadcasted_iota(jnp.int32, (tq, tk), 0)
        s_off = lax.broadcasted_iota(jnp.int32, (tq, tk), 1)
        qv = q_ref[...]
        first = lane < HEAD_DIM
        q_heads = (jnp.where(first, qv, jnp.zeros_like(qv)), jnp.where(first, jnp.zeros_like(qv), qv))

        def step(blocks, carry):
            chains = [(b, h) for b in range(len(blocks)) for h in range(2)]
            rows = [r for _, r, _ in blocks]
            s0 = [pl.multiple_of(kj * tk, tk) for kj, _, _ in blocks]
            kb = [k_ref[pl.ds(s, tk), :] for s in s0]
            vb = [v_ref[pl.ds(s, tk), :] for s in s0]
            visible = [(s + s_off < t_idx)[lo:hi] if masked else None
                       for s, (_, (lo, hi), masked) in zip(s0, blocks)]
            z = {c: _dot_nt(q_heads[c[1]][rows[c[0]][0]:rows[c[0]][1]], kb[c[0]]) for c in chains}
            run = [carry[0], carry[2]]
            log_own, later, run_at = {}, {}, {}
            for c in chains:
                b, h = c
                lk = _softplus_parts(z[c])[0]
                if visible[b] is not None:
                    lk = jnp.where(visible[b], lk, 0.0)
                log_own[c] = z[c] + lk
                later[c] = _dot(lk.astype(BF16), later_mat)
                run_at[c] = run[h][rows[b][0]:rows[b][1]]
                run[h] = _add_rows(run[h], rows[b], jnp.sum(lk, axis=-1, keepdims=True))
            acc = [carry[1], carry[3]]
            for c in chains:
                b, h = c
                a = jnp.exp(log_own[c] + later[c] + run_at[c])
                if visible[b] is not None:
                    a = jnp.where(visible[b], a, 0.0)
                acc[h] = _add_rows(acc[h], rows[b], _dot(a.astype(BF16), vb[b]))
            return run[0], acc[0], run[1], acc[1]

        zero1, zero128 = jnp.zeros((tq, 1), F32), jnp.zeros((tq, 128), F32)
        carry = step([(qi * kpq + 1, (tk, tq), True), (qi * kpq, (0, tq), True)], (zero1, zero128, zero1, zero128))

        def low(run):
            return jnp.max(run)

        def pair_more(c):
            return (c[0] < qi) & (jnp.maximum(low(c[1][tk:]), low(c[3][tk:])) > EXP_UNDERFLOW)

        def pair_step(c):
            last = (qi - c[0]) * kpq - 1
            return (c[0] + 1, *step([(last, (0, tq), False), (last - 1, (0, tq), False)], c[1:]))

        pairs, *carry = lax.while_loop(pair_more, pair_step, (jnp.int32(0), *carry))
        left = (qi - pairs) * kpq

        def single_more(c):
            return (c[0] < left) & (jnp.maximum(low(c[1][:tk]), low(c[3][:tk])) > EXP_UNDERFLOW)

        def single_step(c):
            return (c[0] + 1, *step([(left - 1 - c[0], (0, tk), False)], c[1:]))

        singles, *carry = lax.while_loop(single_more, single_step, (jnp.int32(0), *carry))
        steps_ref[...] = jnp.concatenate([jnp.full((4, 128), pairs, F32), jnp.full((4, 128), singles, F32)], axis=0)
        o_tot = jnp.where(first, carry[1], carry[3])
        l_tot = jnp.where(first, carry[0], carry[2])
        o_ref[...] = o_tot.astype(BF16)
        lt_ref[...] = l_tot
        zz = z_ref[...].astype(F32)
        g_ref[...] = (o_tot * (zz * _sigmoid(zz))).astype(BF16)

    blk = pl.BlockSpec((tq, 128), lambda hp, qi: (qi, hp))
    seq = pl.BlockSpec((S, 128), lambda hp, qi: (0, hp))
    return pl.pallas_call(
        body, name="attn_fwd", grid=(D_MODEL // 128, S // tq),
        in_specs=[blk, seq, seq, blk],
        out_specs=[blk, blk, blk, pl.BlockSpec((None, None, 8, 128), lambda hp, qi: (hp, qi, 0, 0))],
        out_shape=[jax.ShapeDtypeStruct((S, D_MODEL), BF16)] * 2 + [jax.ShapeDtypeStruct((S, D_MODEL), F32)]
        + [jax.ShapeDtypeStruct((D_MODEL // 128, S // tq, 8, 128), F32)],
        compiler_params=_params(("parallel", "arbitrary")),
    )(q, k, v, zgate)


def _ple_out_bwd(name, dx_out, e, gate, ple_g, w_out):
    S = dx_out.shape[0]
    tm = ROW_TILE

    def body(dx_ref, e_ref, gt_ref, pg_ref, wo_ref, de_ref, dgp_ref, dxm_ref, dg_ref):
        dxo = dx_ref[...]
        ev = e_ref[...].astype(F32)
        gv = gt_ref[...].astype(F32)
        de_ref[...] = (dxo * gv).astype(BF16)
        dgp = (dxo * ev * gv * (1.0 - gv)).astype(BF16)
        dgp_ref[...] = dgp
        pg = jnp.concatenate([pg_ref[sh] for sh in range(N_CHIPS)], axis=0)
        dxm = dxo + _dot_nt(dgp, pg)
        dxm_ref[...] = dxm
        dg_ref[...] = _dot_nt(dxm.astype(BF16), wo_ref[...]).astype(BF16)

    row = pl.BlockSpec((tm, D_MODEL), lambda i: (i, 0))
    return pl.pallas_call(
        body, name=name, grid=(S // tm,),
        in_specs=[row, row, row,
                  pl.BlockSpec((N_CHIPS, 256, D_MODEL), lambda i: (0, 0, 0)),
                  pl.BlockSpec((D_MODEL, D_MODEL), lambda i: (0, 0))],
        out_specs=[row, row, row, row],
        out_shape=[jax.ShapeDtypeStruct((S, D_MODEL), BF16), jax.ShapeDtypeStruct((S, D_MODEL), BF16),
                   jax.ShapeDtypeStruct((S, D_MODEL), F32), jax.ShapeDtypeStruct((S, D_MODEL), BF16)],
        compiler_params=_params(("arbitrary",)),
    )(dx_out, e, gate, ple_g, w_out)


def _attn_bwd(q, k, v, ltot, steps, dgated, o, zgate, reduce=None):
    S = q.shape[0]
    tq, tk = ATT_Q_TILE, ATT_K_TILE
    kpq = tq // tk
    nq = S // tq

    def body(q_ref, k_ref, v_ref, lt_ref, steps_ref, dg_ref, o_ref, z_ref, dq_ref, dk_ref, dv_ref, dz_ref,
             dk_acc, dv_acc):
        qi = pl.program_id(1)

        @pl.when(qi == 0)
        def _():
            dk_acc[...] = jnp.zeros_like(dk_acc)
            dv_acc[...] = jnp.zeros_like(dv_acc)

        lane = lax.broadcasted_iota(jnp.int32, (1, 128), 1)
        ri = lax.broadcasted_iota(jnp.int32, (tk, tk), 0)
        ci = lax.broadcasted_iota(jnp.int32, (tk, tk), 1)
        later_mat = _mask_bf16(ri > ci)
        before_mat = _mask_bf16(ri < ci)
        t_idx = qi * tq + lax.broadcasted_iota(jnp.int32, (tq, tk), 0)
        s_off = lax.broadcasted_iota(jnp.int32, (tq, tk), 1)
        zz = z_ref[...].astype(F32)
        sg = _sigmoid(zz)
        dgv = dg_ref[...].astype(F32)
        dz_ref[...] = (dgv * o_ref[...].astype(F32) * _dsilu(zz, sg)).astype(BF16)
        dob = (dgv * (zz * sg)).astype(BF16)
        ltv = lt_ref[...]
        qv = q_ref[...]
        first = lane < HEAD_DIM
        masks = (first, jnp.logical_not(first))
        q_heads = [jnp.where(hm, qv, jnp.zeros_like(qv)) for hm in masks]
        do_heads = [jnp.where(hm, dob, jnp.zeros_like(dob)) for hm in masks]
        totals = [jnp.max(jnp.where(hm, ltv, -jnp.inf), axis=-1, keepdims=True) for hm in masks]

        def step(blocks, carry):
            chains = [(b, h) for b in range(len(blocks)) for h in range(2)]
            rows = [r for _, r, _ in blocks]
            cut = lambda t, b: t[rows[b][0]:rows[b][1]]
            s0 = [pl.multiple_of(kj * tk, tk) for kj, _, _ in blocks]
            kb = [k_ref[pl.ds(s, tk), :] for s in s0]
            vb = [v_ref[pl.ds(s, tk), :] for s in s0]
            visible = [(s + s_off < t_idx)[lo:hi] if masked else None
                       for s, (_, (lo, hi), masked) in zip(s0, blocks)]
            z = {c: _dot_nt(cut(q_heads[c[1]], c[0]), kb[c[0]]) for c in chains}
            da = {c: _dot_nt(cut(do_heads[c[1]], c[0]), vb[c[0]]) for c in chains}
            run = [carry[0], carry[3]]
            log_own, beta, later, base = {}, {}, {}, {}
            for c in chains:
                b, h = c
                lk = _softplus_parts(z[c])[0]
                if visible[b] is not None:
                    lk = jnp.where(visible[b], lk, 0.0)
                log_own[c] = z[c] + lk
                beta[c] = jnp.exp(log_own[c]).astype(BF16)
                later[c] = _dot(lk.astype(BF16), later_mat)
                run[h] = _add_rows(run[h], rows[b], jnp.sum(lk, axis=-1, keepdims=True))
                base[c] = cut(totals[h] - run[h], b)
            grun = [carry[1], carry[4]]
            a_bf, g_bf, gbefore, grun_at = {}, {}, {}, {}
            for c in chains:
                b, h = c
                a = jnp.exp(log_own[c] + later[c] + base[c])
                if visible[b] is not None:
                    a = jnp.where(visible[b], a, 0.0)
                a_bf[c] = a.astype(BF16)
                g = da[c] * a
                g_bf[c] = g.astype(BF16)
                gbefore[c] = _dot(g_bf[c], before_mat)
                grun_at[c] = cut(grun[h], b)
                grun[h] = _add_rows(grun[h], rows[b], jnp.sum(g, axis=-1, keepdims=True))
            dq = [carry[2], carry[5]]
            dk_blk = [jnp.zeros((tk, 128), F32) for _ in blocks]
            dv_blk = [jnp.zeros((tk, 128), F32) for _ in blocks]
            for c in chains:
                b, h = c
                g = g_bf[c].astype(F32)
                dz = g - beta[c].astype(F32) * (g + gbefore[c] + grun_at[c])
                if visible[b] is not None:
                    dz = jnp.where(visible[b], dz, 0.0)
                dzb = dz.astype(BF16)
                dq[h] = _add_rows(dq[h], rows[b], _dot(dzb, kb[b]))
                dk_blk[b] = dk_blk[b] + _dot_tn(dzb, cut(q_heads[h], b))
                dv_blk[b] = dv_blk[b] + _dot_tn(a_bf[c], cut(do_heads[h], b))
            for b in range(len(blocks)):
                dk_acc[pl.ds(s0[b], tk), :] += dk_blk[b]
                dv_acc[pl.ds(s0[b], tk), :] += dv_blk[b]
            return run[0], grun[0], dq[0], run[1], grun[1], dq[1]

        pairs = jnp.clip(jnp.max(steps_ref[0:4, :]).astype(jnp.int32), 0, qi)
        left = (qi - pairs) * kpq
        singles = jnp.clip(jnp.max(steps_ref[4:8, :]).astype(jnp.int32), 0, left)
        zero1, zero128 = jnp.zeros((tq, 1), F32), jnp.zeros((tq, 128), F32)
        carry = lax.fori_loop(left - singles, left, lambda kj, c: step([(kj, (0, tk), False)], c),
                              (zero1, zero1, zero128, zero1, zero1, zero128))
        carry = lax.fori_loop(qi - pairs, qi,
                              lambda n, c: step([(n * kpq, (0, tq), False), (n * kpq + 1, (0, tq), False)], c), carry)
        carry = step([(qi * kpq, (0, tq), True), (qi * kpq + 1, (tk, tq), True)], carry)
        dq_ref[...] = jnp.where(first, carry[2], carry[5]).astype(BF16)

        @pl.when(qi == nq - 1)
        def _():
            dk_ref[...] = dk_acc[...].astype(BF16)
            dv_ref[...] = dv_acc[...].astype(BF16)

    blk = pl.BlockSpec((tq, 128), lambda hp, qi: (qi, hp))
    seq = pl.BlockSpec((S, 128), lambda hp, qi: (0, hp))
    return _call_with_gather(
        body, name="attn_bwd", grid=(D_MODEL // 128, nq),
        in_specs=[blk, seq, seq, blk, pl.BlockSpec((None, None, 8, 128), lambda hp, qi: (hp, qi, 0, 0)),
                  blk, blk, blk],
        out_specs=[blk, seq, seq, blk],
        out_shape=[jax.ShapeDtypeStruct((S, D_MODEL), BF16)] * 4,
        scratch_shapes=[pltpu.VMEM((S, 128), F32), pltpu.VMEM((S, 128), F32)],
        args=(q, k, v, ltot, steps, dgated, o, zgate), reduce=reduce, vmem_mib=56)


def _rms_bwd(xv, dh_gain_sum):
    r = lax.rsqrt(jnp.mean(xv * xv, axis=-1, keepdims=True) + EPS)
    xhat = xv * r
    dx = r * (dh_gain_sum - xhat * jnp.mean(dh_gain_sum * xhat, axis=-1, keepdims=True))
    return dx, xhat


def _b_in_bwd(dq, dk, dv, dz, q_raw, k_raw, x, dx_mid, q_gain_t, k_gain_t, b_gain, kv_gain, w_in, w_kv):
    S = x.shape[0]
    tm = ROW_TILE

    def body(dq_ref, dk_ref, dv_ref, dz_ref, qr_ref, kr_ref, x_ref, dxm_ref, qg_ref, kg_ref, bg_ref, kvg_ref,
             win_ref, wkv_ref, dqz_ref, dkv_ref, dx_ref, small_ref):
        @pl.when(pl.program_id(0) == 0)
        def _():
            small_ref[...] = jnp.zeros_like(small_ref)

        bd = _head_mean_matrix()

        def head_norm_bwd(dy_ref, raw_ref, gain, scale):
            raw = raw_ref[...].astype(F32)
            rr = lax.rsqrt(_head_mean(raw * raw, bd) + EPS)
            xhat = raw * rr
            dy = dy_ref[...].astype(F32) * scale
            gdy = dy * gain
            draw = rr * (gdy - xhat * _head_mean(gdy * xhat, bd))
            return draw.astype(BF16), jnp.sum(dy * xhat, axis=0, keepdims=True)

        dqr, dqg = head_norm_bwd(dq_ref, qr_ref, qg_ref[...], SB_SCALE)
        dkr, dkg = head_norm_bwd(dk_ref, kr_ref, kg_ref[...], 1.0)
        dqz_ref[:, :D_MODEL] = dqr
        dqz_ref[:, D_MODEL:] = dz_ref[...]
        dkv_ref[:, :D_MODEL] = dkr
        dkv_ref[:, D_MODEL:] = dv_ref[...]
        dhb = jnp.zeros((tm, D_MODEL), F32)
        dhkv = jnp.zeros((tm, D_MODEL), F32)
        for sh in range(N_CHIPS):
            cols = slice(sh * 512, (sh + 1) * 512)
            dhb = dhb + _dot_nt(dqz_ref[:, cols], win_ref[sh])
            dhkv = dhkv + _dot_nt(dkv_ref[:, cols], wkv_ref[sh])
        dx, xhat = _rms_bwd(x_ref[...], dhb * bg_ref[...] + dhkv * kvg_ref[...])
        dx_ref[...] = dxm_ref[...] + dx
        small_ref[0:1, :] += dqg
        small_ref[1:2, :] += dkg
        small_ref[2:3, :] += jnp.sum(dhb * xhat, axis=0, keepdims=True)
        small_ref[3:4, :] += jnp.sum(dhkv * xhat, axis=0, keepdims=True)

    row = pl.BlockSpec((tm, D_MODEL), lambda i: (i, 0))
    wide = pl.BlockSpec((tm, 2 * D_MODEL), lambda i: (i, 0))
    vec = pl.BlockSpec((1, D_MODEL), lambda i: (0, 0))
    wsp = pl.BlockSpec((N_CHIPS, D_MODEL, 512), lambda i: (0, 0, 0))
    return pl.pallas_call(
        body, name="b_in_bwd", grid=(S // tm,),
        in_specs=[row] * 8 + [vec] * 4 + [wsp, wsp],
        out_specs=[wide, wide, row, pl.BlockSpec((8, D_MODEL), lambda i: (0, 0))],
        out_shape=[jax.ShapeDtypeStruct((S, 2 * D_MODEL), BF16), jax.ShapeDtypeStruct((S, 2 * D_MODEL), BF16),
                   jax.ShapeDtypeStruct((S, D_MODEL), F32), jax.ShapeDtypeStruct((8, D_MODEL), F32)],
        compiler_params=_params(("arbitrary",), 56),
    )(dq, dk, dv, dz, q_raw, k_raw, x, dx_mid, q_gain_t, k_gain_t, b_gain, kv_gain, w_in, w_kv)


def _a_mix_bwd(dgated, uz, pooled, wg, scale, w_in, x, dx_mid, gain, reduce=None):
    S = x.shape[0]
    tm = ROW_TILE
    n = S // tm

    def body(dg_ref, z_ref, p_ref, wg_ref, sc_ref, win_ref, x_ref, dxm_ref, gn_ref,
             duz_ref, dmr_ref, dx_ref, small_ref, halo_hi, halo_lo):
        i = pl.program_id(0)

        @pl.when(i == 0)
        def _():
            small_ref[...] = jnp.zeros_like(small_ref)
            halo_hi[...] = jnp.zeros_like(halo_hi)
            halo_lo[...] = jnp.zeros_like(halo_lo)

        first_row = (n - 1 - i) * tm
        row = lax.broadcasted_iota(jnp.int32, (tm, tm), 0)
        col = lax.broadcasted_iota(jnp.int32, (tm, tm), 1)
        d = col - row
        for g, w in enumerate(POOL_WINDOWS):
            cols = slice(g * GROUP_DIM, (g + 1) * GROUP_DIM)
            wgg = _group_weight(wg_ref, g)
            sc = sc_ref[:, cols]
            mraw = _dot(p_ref[:, cols], wgg)
            z = z_ref[:, cols]
            sg = _sigmoid(z)
            dga = dg_ref[:, cols].astype(F32)
            dm = dga * (z * sg)
            duz_ref[:, D_MODEL + g * GROUP_DIM:D_MODEL + (g + 1) * GROUP_DIM] = (
                dga * (mraw * sc) * _dsilu(z, sg)).astype(BF16)
            small_ref[0:1, cols] += jnp.sum(dm * mraw, axis=0, keepdims=True)
            dmr = (dm * sc).astype(BF16)
            dmr_ref[:, cols] = dmr
            dp = _dot_nt(dmr, wgg)
            hi, lo = _hilo(dp * _inv_count(first_row, tm, w))
            t_main = _mask_bf16((d >= 0) & (d < w))
            t_halo = _mask_bf16(d + tm < w)
            du = (_dot(t_main, hi) + _dot(t_main, lo) + _dot(t_halo, halo_hi[:, cols]) + _dot(t_halo, halo_lo[:, cols])
                  - dp)
            halo_hi[:, cols] = hi
            halo_lo[:, cols] = lo
            duz_ref[:, cols] = du.astype(BF16)
        dh = jnp.zeros((tm, D_MODEL), F32)
        for sh in range(N_CHIPS):
            dh = dh + _dot_nt(duz_ref[:, sh * 512:(sh + 1) * 512], win_ref[sh])
        dx, xhat = _rms_bwd(x_ref[...], dh * gn_ref[...])
        dx_ref[...] = dxm_ref[...] + dx
        small_ref[1:2, :] += jnp.sum(dh * xhat, axis=0, keepdims=True)

    rev = lambda i: (n - 1 - i, 0)
    row = pl.BlockSpec((tm, D_MODEL), rev)
    vec = pl.BlockSpec((1, D_MODEL), lambda i: (0, 0))
    return _call_with_gather(
        body, name="a_mix_bwd", grid=(n,),
        in_specs=[row,
                  pl.BlockSpec((tm, D_MODEL), lambda i: (n - 1 - i, 1)),
                  row,
                  pl.BlockSpec((N_CHIPS, N_GROUPS, 64, GROUP_DIM), lambda i: (0, 0, 0, 0)),
                  vec,
                  pl.BlockSpec((N_CHIPS, D_MODEL, 512), lambda i: (0, 0, 0)),
                  row, row, vec],
        out_specs=[pl.BlockSpec((tm, 2 * D_MODEL), rev), row, row,
                   pl.BlockSpec((8, D_MODEL), lambda i: (0, 0))],
        out_shape=[jax.ShapeDtypeStruct((S, 2 * D_MODEL), BF16), jax.ShapeDtypeStruct((S, D_MODEL), BF16),
                   jax.ShapeDtypeStruct((S, D_MODEL), F32), jax.ShapeDtypeStruct((8, D_MODEL), F32)],
        scratch_shapes=[pltpu.VMEM((tm, D_MODEL), BF16), pltpu.VMEM((tm, D_MODEL), BF16)],
        args=(dgated, uz, pooled, wg, scale, w_in, x, dx_mid, gain), reduce=reduce, vmem_mib=56)


def _wgrad(name, a, dy, n_shards, a_spec=None, k_dim=None):
    S, n_cols = dy.shape
    ts = WGRAD_SEQ_TILE
    k_dim = a.shape[-1] if k_dim is None else k_dim
    wn = n_cols // n_shards
    tk = min(k_dim, WGRAD_ACC_BYTES // (4 * n_cols))
    nst = S // ts

    def body(a_ref, dy_ref, out_ref, acc):
        st = pl.program_id(1)

        @pl.when(st == 0)
        def _():
            acc[...] = jnp.zeros_like(acc)

        acc[...] += _dot_tn(a_ref[...].astype(BF16), dy_ref[...].astype(BF16))

        @pl.when(st == nst - 1)
        def _():
            for sh in range(n_shards):
                out_ref[sh] = acc[:, sh * wn:(sh + 1) * wn]

    if a_spec is None:
        a_spec = pl.BlockSpec((ts, tk), lambda kt, st: (st, kt))
    return pl.pallas_call(
        body, name=name, grid=(k_dim // tk, nst),
        in_specs=[a_spec, pl.BlockSpec((ts, n_cols), lambda kt, st: (st, 0))],
        out_specs=pl.BlockSpec((n_shards, tk, wn), lambda kt, st: (0, kt, 0)),
        out_shape=jax.ShapeDtypeStruct((n_shards, k_dim, wn), F32),
        scratch_shapes=[pltpu.VMEM((tk, n_cols), F32)],
        compiler_params=_params(("parallel", "arbitrary")),
    )(a, dy)


def _wgrad_ple(name, p, layer, de):
    ts = WGRAD_SEQ_TILE
    spec = pl.BlockSpec((None, None, ts, PLE_DIM), lambda kt, st: (layer, 0, st, 0))
    return _wgrad(name, p, de, N_CHIPS, a_spec=spec, k_dim=PLE_DIM)


def _wgrad_group(pooled, dmr):
    S = pooled.shape[0]
    ts = WGRAD_SEQ_TILE
    nst = S // ts

    def body(p_ref, d_ref, out_ref, acc):
        st = pl.program_id(1)

        @pl.when(st == 0)
        def _():
            acc[...] = jnp.zeros_like(acc)

        acc[...] += _dot_tn(p_ref[...], d_ref[...])

        @pl.when(st == nst - 1)
        def _():
            for sh in range(N_CHIPS):
                out_ref[sh] = acc[sh * 64:(sh + 1) * 64, :]

    blk = pl.BlockSpec((ts, GROUP_DIM), lambda g, st: (st, g))
    return pl.pallas_call(
        body, name="wgrad_group", grid=(N_GROUPS, nst),
        in_specs=[blk, blk],
        out_specs=pl.BlockSpec((N_CHIPS, None, 64, GROUP_DIM), lambda g, st: (0, g, 0, 0)),
        out_shape=jax.ShapeDtypeStruct((N_CHIPS, N_GROUPS, 64, GROUP_DIM), F32),
        scratch_shapes=[pltpu.VMEM((GROUP_DIM, GROUP_DIM), F32)],
        compiler_params=_params(("parallel", "arbitrary")),
    )(pooled, dmr)


GATHER_AT = {
    "a_in": ("a_w_group", "a_w_out", "ple_w0", "ple_gate_w0"),
    "a_mix": ("w_kv",),
    "a_out_ple": ("b_w_in",),
    "b_in": ("b_w_out", "ple_w1", "ple_gate_w1"),
}


REDUCE_AT = {
    "attn_bwd": ("b_w_out", "ple_w1", "ple_gate_w1"),
    "a_mix_bwd": ("w_kv", "b_w_in"),
}


def _local_step(x, p, target, w, local=None, state=None):
    w = dict(w)

    def run(fn, host, n_out, *args, **kwargs):
        names = GATHER_AT[host] if local is not None else ()
        res = fn(*args, gather=[local[n] for n in names], **kwargs)
        w.update(zip(names, res[n_out:]))
        return res[:n_out]

    k_gain_t = jnp.tile(w["k_norm"].reshape(1, HEAD_DIM), (1, N_HEADS))
    q_gain_t = jnp.tile(w["b_q_norm"].reshape(1, HEAD_DIM), (1, N_HEADS))

    uz, h_a = run(_a_in, "a_in", 2, x, w["a_norm"], w["a_w_in"])
    wg4 = w["a_w_group"].reshape(N_CHIPS, N_GROUPS, 64, GROUP_DIM)
    wa_out = w["a_w_out"].reshape(D_MODEL, D_MODEL)
    gated_a, pooled = run(_a_mix, "a_mix", 2, uz, wg4, w["a_scale"])
    x1, x2, e_a, gate_a = run(_out_ple, "a_out_ple", 4, "a_out_ple", gated_a, x, wa_out, p, 0,
                              w["ple_w0"], w["ple_gate_w0"])
    h_kv, h_b, k_raw, q_raw, k, q, v, z_b = run(
        _b_in, "b_in", 8, x2, w["kv_norm"], w["b_norm"], k_gain_t, q_gain_t, w["w_kv"], w["b_w_in"])
    wb_out = w["b_w_out"].reshape(D_MODEL, D_MODEL)
    o, gated_b, ltot, att_steps = _attn_fwd(q, k, v, z_b)
    x3, dx4, e_b, gate_b, loss_blk = _out_ple("b_out_ple", gated_b, x2, wb_out, p, 1, w["ple_w1"], w["ple_gate_w1"],
                                              target=target)

    grads, updates = {}, {}

    def hosted(fn, host, n_out, *args):
        if state is None:
            return fn(*args)
        names = REDUCE_AT[host]
        res = fn(*args, reduce=([grads.pop(n) for n in names], *[[t[n] for n in names] for t in state]))
        for i, n in enumerate(names):
            updates[n] = tuple(group[i] for group in res[n_out:])
        return res[:n_out]

    de_b, dgp_b, dx3, dgated_b = _ple_out_bwd("b_ple_out_bwd", dx4, e_b, gate_b, w["ple_gate_w1"], wb_out)
    grads["b_w_out"] = _wgrad("wgrad_b_out", gated_b, dx3, 1).reshape(N_CHIPS, 256, D_MODEL)
    grads["ple_w1"] = _wgrad_ple("wgrad_ple1", p, 1, de_b)
    grads["ple_gate_w1"] = _wgrad("wgrad_gate1", x3, dgp_b, 1).reshape(N_CHIPS, 256, D_MODEL)
    dq, dk, dv, dz_b = hosted(_attn_bwd, "attn_bwd", 4, q, k, v, ltot, att_steps, dgated_b, o, z_b)
    dqz, dkv, dx2, small_b = _b_in_bwd(dq, dk, dv, dz_b, q_raw, k_raw, x2, dx3, q_gain_t, k_gain_t,
                                       w["b_norm"], w["kv_norm"], w["b_w_in"], w["w_kv"])
    grads["w_kv"] = _wgrad("wgrad_kv", h_kv, dkv, N_CHIPS)
    grads["b_w_in"] = _wgrad("wgrad_b_in", h_b, dqz, N_CHIPS)
    de_a, dgp_a, dx1, dgated_a = _ple_out_bwd("a_ple_out_bwd", dx2, e_a, gate_a, w["ple_gate_w0"], wa_out)
    grads["a_w_out"] = _wgrad("wgrad_a_out", gated_a, dx1, 1).reshape(N_CHIPS, 256, D_MODEL)
    grads["ple_w0"] = _wgrad_ple("wgrad_ple0", p, 0, de_a)
    grads["ple_gate_w0"] = _wgrad("wgrad_gate0", x1, dgp_a, 1).reshape(N_CHIPS, 256, D_MODEL)
    duz, dmr, grad_x, small_a = hosted(_a_mix_bwd, "a_mix_bwd", 4, dgated_a, uz, pooled, wg4, w["a_scale"],
                                       w["a_w_in"], x, dx1, w["a_norm"])
    grads["a_w_in"] = _wgrad("wgrad_a_in", h_a, duz, N_CHIPS)
    grads["a_w_group"] = _wgrad_group(pooled, dmr).reshape(N_CHIPS, N_GROUPS * 64, GROUP_DIM)

    fold = lambda row: jnp.pad(row.reshape(N_HEADS, HEAD_DIM).sum(axis=0), (0, D_MODEL - HEAD_DIM))
    small = jnp.stack([small_a[1], small_a[0], small_b[3], small_b[2], fold(small_b[1]), fold(small_b[0]),
                       jnp.pad(loss_blk[0], (0, D_MODEL - loss_blk.shape[1])), jnp.zeros((D_MODEL,), F32)])
    return grad_x, grads, updates, small


def _mesh_place():
    x, y, c = lax.axis_index("x"), lax.axis_index("y"), lax.axis_index("c")
    other_chips = [(1 - x, y), (x, 1 - y), (1 - x, 1 - y)]
    return x, y, c, other_chips


def _gather_sems(n):
    return [pltpu.SemaphoreType.DMA((3 * n,)), pltpu.SemaphoreType.DMA((3 * n,)),
            pltpu.SemaphoreType.DMA((3 * n,)), pltpu.SemaphoreType.DMA((3 * n,)), pltpu.SemaphoreType.DMA((n,))]


def _gather_copies(srcs, outs, sems):
    send_far, recv_far, send_sib, recv_sib, local_sem = sems
    n = len(srcs)
    x, y, c, chips = _mesh_place()
    me = 2 * x + y
    sibling = (x, y, 1 - c)

    def half(k, which):
        rows = srcs[k].shape[0] // 2
        return pl.ds(pl.multiple_of(which * rows, 16), rows)

    local = [pltpu.make_async_copy(srcs[k], outs[k].at[me], local_sem.at[k]) for k in range(n)]
    far = [pltpu.make_async_remote_copy(
        src_ref=srcs[k].at[half(k, c)], dst_ref=outs[k].at[me, half(k, c)],
        send_sem=send_far.at[j * n + k], recv_sem=recv_far.at[j * n + k], device_id=(px, py, c), device_id_type=MESH)
        for j, (px, py) in enumerate(chips) for k in range(n)]

    def landed(j, k, which, from_far):
        px, py = chips[j]
        piece = outs[k].at[2 * px + py, half(k, which)]
        send, recv = (send_far, recv_far) if from_far else (send_sib, recv_sib)
        return pltpu.make_async_remote_copy(src_ref=piece, dst_ref=piece, send_sem=send.at[j * n + k],
                                            recv_sem=recv.at[j * n + k], device_id=sibling, device_id_type=MESH)

    return local, far, landed, c


def _gather_start(srcs, outs, sems):
    local, far, _, _ = _gather_copies(srcs, outs, sems)
    for cp in local + far:
        cp.start()


def _gather_pass_on(srcs, outs, sems):
    _, _, landed, c = _gather_copies(srcs, outs, sems)
    for j in range(3):
        for k in range(len(srcs)):
            landed(j, k, c, True).wait_recv()
            landed(j, k, c, False).start()


def _gather_finish(srcs, outs, sems):
    local, far, landed, c = _gather_copies(srcs, outs, sems)
    pairs = [(j, k) for j in range(3) for k in range(len(srcs))]
    for j, k in pairs:
        landed(j, k, 1 - c, False).wait_recv()
    for cp in far + [landed(j, k, c, False) for j, k in pairs]:
        cp.wait_send()
    for cp in local:
        cp.wait()


def _call_with_gather(body, *, name, grid, in_specs, out_specs, out_shape, args, gather=(), reduce=None,
                      scratch_shapes=(), vmem_mib=48):
    n_in, n_out, n_scr, n_g = len(args), len(out_shape), len(scratch_shapes), len(gather)
    n_r = len(reduce[0]) if reduce else 0
    pieces = _reduce_pieces(reduce[0]) if reduce else []
    reduce_args = [a for group in reduce for a in group] if reduce else []
    gather_sems = _gather_sems(n_g) if n_g else []
    n_steps = 1
    for g in grid:
        n_steps *= g

    def wrapped(*refs):
        refs = list(refs)
        take = lambda count: [refs.pop(0) for _ in range(count)]
        ins, g_in, r_in = take(n_in), take(n_g), take(4 * n_r)
        outs, g_out, r_out = take(n_out), take(n_g), take(4 * n_r)
        scratch, sems, r_scratch = take(n_scr), take(len(gather_sems)), refs
        step = 0
        for axis, g in enumerate(grid):
            step = step * g + pl.program_id(axis)
        if n_g:
            @pl.when(step == 0)
            def _():
                _gather_start(g_in, g_out, sems)

        if n_r:
            ticks, drain = _reduce_ticks(pieces, n_r, (*r_in, *r_out, *r_scratch))
            for t, tick in enumerate(ticks[:n_steps]):
                pl.when(step == t)(tick)

        body(*ins, *outs, *scratch)
        if n_r:
            for tick in ticks[n_steps:]:
                pl.when(step == n_steps - 1)(tick)
            pl.when(step == n_steps - 1)(drain)
        if n_g:
            @pl.when(step == max(n_steps - 2, 0))
            def _():
                _gather_pass_on(g_in, g_out, sems)

            @pl.when(step == n_steps - 1)
            def _():
                _gather_finish(g_in, g_out, sems)

    hbm = pl.BlockSpec(memory_space=pltpu.HBM)
    res = pl.pallas_call(
        wrapped, name=name, grid=grid,
        in_specs=list(in_specs) + [hbm] * (n_g + 4 * n_r), out_specs=list(out_specs) + [hbm] * (n_g + 4 * n_r),
        out_shape=list(out_shape) + [jax.ShapeDtypeStruct((N_CHIPS,) + g.shape, BF16) for g in gather]
        + ([jax.ShapeDtypeStruct(w.shape, F32) for _ in range(4) for w in reduce[1]] if reduce else []),
        scratch_shapes=list(scratch_shapes) + gather_sems + (_reduce_scratch() if reduce else []),
        compiler_params=_params(("arbitrary",) * len(grid), vmem_mib),
    )(*args, *gather, *reduce_args)
    if not reduce:
        return res
    plain = list(res[:n_out + n_g])
    return plain + [res[n_out + n_g + i * n_r:n_out + n_g + (i + 1) * n_r] for i in range(4)]


def _allgather_weights(shards, small, casts):
    n = len(shards)
    cast_out = [(k, r0, r1) for k, (_, ranges) in enumerate(casts) for r0, r1 in ranges]
    n_c, n_co = len(casts), len(cast_out)

    def body(*refs):
        ins, small_in, cast_in = refs[:n], refs[n], refs[n + 1:n + 1 + n_c]
        refs = refs[n + 1 + n_c:]
        outs, small_out, cast_dst = refs[:n], refs[n], refs[n + 1:n + 1 + n_co]
        refs = refs[n + 1 + n_co:]
        cast, cast_buf = refs[:n], refs[n:n + n_co]
        send_far, recv_far, send_sib, recv_sib, send_small, recv_small, local_sem, cast_sem = refs[n + n_co:]
        x, y, c, chips = _mesh_place()
        me = 2 * x + y
        sibling = (x, y, 1 - c)

        def half(k, which):
            rows = ins[k].shape[0] // 2
            return pl.ds(pl.multiple_of(which * rows, 16), rows)

        local = []
        for k in range(n):
            cast[k][...] = ins[k][...].astype(BF16)
            local.append(pltpu.make_async_copy(cast[k], outs[k].at[me], local_sem.at[k]))
            local[-1].start()
        local.append(pltpu.make_async_copy(small_in, small_out.at[me], local_sem.at[n]))
        local[-1].start()

        sends = []
        for j, (px, py) in enumerate(chips):
            for k in range(n):
                cp = pltpu.make_async_remote_copy(
                    src_ref=cast[k].at[half(k, c)], dst_ref=outs[k].at[me, half(k, c)],
                    send_sem=send_far.at[j * n + k], recv_sem=recv_far.at[j * n + k],
                    device_id=(px, py, c), device_id_type=MESH)
                cp.start()
                sends.append(cp)
            cp = pltpu.make_async_remote_copy(
                src_ref=small_in, dst_ref=small_out.at[me], send_sem=send_small.at[j], recv_sem=recv_small.at[j],
                device_id=(px, py, c), device_id_type=MESH)
            cp.start()
            sends.append(cp)

        for i, (k, r0, r1) in enumerate(cast_out):
            cast_buf[i][...] = cast_in[k][r0:r1, :].astype(BF16)
            local.append(pltpu.make_async_copy(cast_buf[i], cast_dst[i], cast_sem.at[i]))
            local[-1].start()

        def landed(j, k, which, sems_s, sems_r, device):
            px, py = chips[j]
            piece = outs[k].at[2 * px + py, half(k, which)]
            return pltpu.make_async_remote_copy(
                src_ref=piece, dst_ref=piece, send_sem=sems_s.at[j * n + k], recv_sem=sems_r.at[j * n + k],
                device_id=device, device_id_type=MESH)

        for j in range(len(chips)):
            for k in range(n):
                landed(j, k, c, send_far, recv_far, sibling).wait_recv()
                cp = landed(j, k, c, send_sib, recv_sib, sibling)
                cp.start()
                sends.append(cp)
        for j, (px, py) in enumerate(chips):
            for k in range(n):
                landed(j, k, 1 - c, send_sib, recv_sib, sibling).wait_recv()
            pltpu.make_async_remote_copy(
                src_ref=small_in, dst_ref=small_out.at[2 * px + py], send_sem=send_small.at[j],
                recv_sem=recv_small.at[j], device_id=(px, py, c), device_id_type=MESH).wait_recv()
        for cp in sends:
            cp.wait_send()
        for cp in local:
            cp.wait()

    vmem = pl.BlockSpec(memory_space=pltpu.VMEM)
    hbm = pl.BlockSpec(memory_space=pltpu.HBM)
    cast_shapes = [(r1 - r0, casts[k][0].shape[1]) for k, r0, r1 in cast_out]
    res = pl.pallas_call(
        body, name="allgather_weights",
        in_specs=[vmem] * (n + 1 + n_c), out_specs=[hbm] * (n + 1 + n_co),
        out_shape=[jax.ShapeDtypeStruct((N_CHIPS,) + s.shape, BF16) for s in shards]
        + [jax.ShapeDtypeStruct((N_CHIPS,) + small.shape, F32)]
        + [jax.ShapeDtypeStruct(s, BF16) for s in cast_shapes],
        scratch_shapes=[pltpu.VMEM(s.shape, BF16) for s in shards] + [pltpu.VMEM(s, BF16) for s in cast_shapes]
        + [pltpu.SemaphoreType.DMA((3 * n,)), pltpu.SemaphoreType.DMA((3 * n,)),
           pltpu.SemaphoreType.DMA((3 * n,)), pltpu.SemaphoreType.DMA((3 * n,)),
           pltpu.SemaphoreType.DMA((3,)), pltpu.SemaphoreType.DMA((3,)),
           pltpu.SemaphoreType.DMA((n + 1,)), pltpu.SemaphoreType.DMA((n_co,))],
        compiler_params=_params(None, 40),
    )(*shards, small, *[a for a, _ in casts])
    return res[:n], res[n], res[n + 1:]


def _adamw(w, g, m, v):
    m = ADAM_B1 * m + (1.0 - ADAM_B1) * g
    v = ADAM_B2 * v + (1.0 - ADAM_B2) * (g * g)
    m_hat = m / (1.0 - ADAM_B1 ** ADAM_STEP)
    v_hat = v / (1.0 - ADAM_B2 ** ADAM_STEP)
    delta = -ADAM_LR * (m_hat / (jnp.sqrt(v_hat) + ADAM_EPS) + ADAM_WD * w)
    return delta, m, v


RS_PIECE_ROWS = 128
RS_PIECE_COLS = 512


def _reduce_adam_all(grads, ws, ms, vs):
    n_w = len(grads)
    pieces = _reduce_pieces(grads)

    def body(*refs):
        ticks, drain = _reduce_ticks(pieces, n_w, refs)
        for tick in ticks:
            tick()
        drain()

    hbm = pl.BlockSpec(memory_space=pltpu.HBM)
    outs = pl.pallas_call(
        body, name="reduce_adam_all",
        in_specs=[hbm] * (4 * n_w), out_specs=[hbm] * (4 * n_w),
        out_shape=[jax.ShapeDtypeStruct(w.shape, F32) for _ in range(4) for w in ws],
        scratch_shapes=_reduce_scratch(),
        compiler_params=_params(None, 48),
    )(*grads, *ws, *ms, *vs)
    return [outs[i * n_w:(i + 1) * n_w] for i in range(4)]


def _reduce_pieces(grads):
    pieces = []
    for k, g in enumerate(grads):
        hr, cols = g.shape[1] // 2, g.shape[2]
        pr, pc = min(hr, RS_PIECE_ROWS), min(cols, RS_PIECE_COLS)
        pieces += [(k, ro, hr, co, pr, pc) for ro in range(0, hr, pr) for co in range(0, cols, pc)]
    return pieces


def _reduce_scratch():
    P, C = RS_PIECE_ROWS, RS_PIECE_COLS
    return [
        pltpu.VMEM((3, N_CHIPS, P, C), F32), pltpu.VMEM((3, N_CHIPS, P, C), F32),
        pltpu.VMEM((2, N_CHIPS, P, C), BF16), pltpu.VMEM((2, N_CHIPS, P, C), BF16),
        pltpu.VMEM((2, N_CHIPS, P, C), F32),
        pltpu.VMEM((2, 3, P, C), BF16), pltpu.VMEM((2, 3, P, C), BF16),
        pltpu.VMEM((2, 2, P, C), F32),
        pltpu.VMEM((2, 3, 2, P, C), F32), pltpu.VMEM((2, 4, 2, P, C), F32),
        pltpu.SemaphoreType.DMA((3, 2)), pltpu.SemaphoreType.DMA((2, 3, 2)),
        pltpu.SemaphoreType.DMA((2,)), pltpu.SemaphoreType.DMA((2,)),
        pltpu.SemaphoreType.DMA((2, 3)), pltpu.SemaphoreType.DMA((2, 3)),
        pltpu.SemaphoreType.DMA((2,)), pltpu.SemaphoreType.DMA((2,)),
        pltpu.SemaphoreType.DMA((2, 4, 2))]


def _reduce_ticks(pieces, n_w, refs):
    n = len(pieces)

    def build(*refs):
        g_in, w_in, m_in, v_in = (refs[i * n_w:(i + 1) * n_w] for i in range(4))
        g_out, d_out, m_out, v_out = (refs[(4 + i) * n_w:(5 + i) * n_w] for i in range(4))
        (gm, go, sb1, rb1, part, sb2, rb2, fin, wmv, outs,
         ld_sem, wmv_sem, s1_send, s1_recv, s2_send, s2_recv, s3_send, s3_recv, out_sem) = refs[8 * n_w:]
        x, y, c, chips = _mesh_place()
        me = 2 * x + y
        sibling = (x, y, 1 - c)

        def at_hbm(i, which):
            _, ro, hr, co, pr, pc = pieces[i]
            half = c if which == 0 else 1 - c
            return pl.ds(pl.multiple_of(half * hr + ro, 64), pr), pl.ds(co, pc)

        def win(i):
            return pl.ds(0, pieces[i][4]), pl.ds(0, pieces[i][5])

        every = slice(None)

        def loads(i):
            k, s = pieces[i][0], i % 3
            return [pltpu.make_async_copy(g_in[k].at[(every,) + at_hbm(i, h)], buf.at[(s, every) + win(i)], ld_sem.at[s, h])
                    for h, buf in enumerate((gm, go))]

        def wmv_loads(i):
            k, s = pieces[i][0], i % 2
            return [pltpu.make_async_copy(src[k].at[at_hbm(i, h)], wmv.at[(s, a, h) + win(i)], wmv_sem.at[s, a, h])
                    for a, src in enumerate((w_in, m_in, v_in)) for h in range(2)]

        def stores(i):
            k, s = pieces[i][0], i % 2
            return [pltpu.make_async_copy(outs.at[(s, a, h) + win(i)], dst[k].at[at_hbm(i, h)], out_sem.at[s, a, h])
                    for a, dst in enumerate((g_out, d_out, m_out, v_out)) for h in range(2)]

        def swap1(i):
            s = i % 2
            return pltpu.make_async_remote_copy(
                src_ref=sb1.at[(s, every) + win(i)], dst_ref=rb1.at[(s, every) + win(i)],
                send_sem=s1_send.at[s], recv_sem=s1_recv.at[s], device_id=sibling, device_id_type=MESH)

        def far2(i, j):
            s = i % 2
            px, py = chips[j]
            return pltpu.make_async_remote_copy(
                src_ref=sb2.at[(s, j) + win(i)], dst_ref=rb2.at[(s, j) + win(i)],
                send_sem=s2_send.at[s, j], recv_sem=s2_recv.at[s, j], device_id=(px, py, c), device_id_type=MESH)

        def swap3(i):
            s = i % 2
            return pltpu.make_async_remote_copy(
                src_ref=fin.at[(s, 0) + win(i)], dst_ref=fin.at[(s, 1) + win(i)],
                send_sem=s3_send.at[s], recv_sem=s3_recv.at[s], device_id=sibling, device_id_type=MESH)

        def stage0(i):
            for cp in loads(i):
                cp.start()

        def stage1(i):
            s, s3 = i % 2, i % 3
            for cp in loads(i):
                cp.wait()
            sb1[(s, every) + win(i)] = go[(s3, every) + win(i)].astype(BF16)
            swap1(i).start()

        def stage2(i):
            s, s3 = i % 2, i % 3
            swap1(i).wait()
            part[(s, every) + win(i)] = gm[(s3, every) + win(i)] + rb1[(s, every) + win(i)].astype(F32)
            for j, (px, py) in enumerate(chips):
                sb2[(s, j) + win(i)] = part[(s, 2 * px + py) + win(i)].astype(BF16)
                far2(i, j).start()

        def stage3(i):
            s = i % 2
            total = part[(s, me) + win(i)]
            for j in range(3):
                far2(i, j).wait()
                total = total + rb2[(s, j) + win(i)].astype(F32)
            fin[(s, 0) + win(i)] = total
            swap3(i).start()
            for cp in wmv_loads(i):
                cp.start()

        def stage4(i):
            s = i % 2
            if i >= 2:
                for cp in stores(i - 2):
                    cp.wait()
            swap3(i).wait()
            for cp in wmv_loads(i):
                cp.wait()
            both = (every,) + win(i)
            g = fin[(s,) + both]
            delta, m_new, v_new = _adamw(wmv[(s, 0) + both], g, wmv[(s, 1) + both], wmv[(s, 2) + both])
            outs[(s, 0) + both] = g
            outs[(s, 1) + both] = delta
            outs[(s, 2) + both] = m_new
            outs[(s, 3) + both] = v_new
            for cp in stores(i):
                cp.start()

        stages = (stage0, stage1, stage2, stage3, stage4)

        def tick(t):
            for age in reversed(range(len(stages))):
                if 0 <= t - age < n:
                    stages[age](t - age)

        def drain():
            for i in range(max(0, n - 2), n):
                for cp in stores(i):
                    cp.wait()

        return [functools.partial(tick, t) for t in range(n + len(stages) - 1)], drain

    return build(*refs)


def _allreduce_small(part):
    n_dev = 8

    def body(part_ref, out_ref, buf, send_sem, recv_sem):
        x, y, c, _ = _mesh_place()
        me = 4 * x + 2 * y + c
        buf[me] = part_ref[...]
        sends = []
        for k in range(1, n_dev):
            peer = ((1 - x) if k & 4 else x, (1 - y) if k & 2 else y, (1 - c) if k & 1 else c)
            cp = pltpu.make_async_remote_copy(src_ref=part_ref, dst_ref=buf.at[me], send_sem=send_sem.at[k - 1],
                                              recv_sem=recv_sem.at[k - 1], device_id=peer, device_id_type=MESH)
            cp.start()
            sends.append(cp)
        for cp in sends:
            cp.wait_recv()
        total = buf[0]
        for s in range(1, n_dev):
            total = total + buf[s]
        out_ref[...] = total
        for cp in sends:
            cp.wait_send()

    vmem = pl.BlockSpec(memory_space=pltpu.VMEM)
    return pl.pallas_call(
        body, name="allreduce_small", in_specs=[vmem], out_specs=vmem,
        out_shape=jax.ShapeDtypeStruct(part.shape, F32),
        scratch_shapes=[pltpu.VMEM((n_dev,) + part.shape, F32),
                        pltpu.SemaphoreType.DMA((n_dev - 1,)), pltpu.SemaphoreType.DMA((n_dev - 1,))],
    )(part)


def _adam_small(w, g, m, v):
    def body(w_ref, g_ref, m_ref, v_ref, d_ref, mo_ref, vo_ref):
        delta, m_new, v_new = _adamw(w_ref[...], g_ref[...], m_ref[...], v_ref[...])
        d_ref[...] = delta
        mo_ref[...] = m_new
        vo_ref[...] = v_new

    vmem = pl.BlockSpec(memory_space=pltpu.VMEM)
    return pl.pallas_call(
        body, name="adam_small", in_specs=[vmem] * 4, out_specs=[vmem] * 3,
        out_shape=[jax.ShapeDtypeStruct(w.shape, F32)] * 3,
    )(w, g, m, v)


BIG = ("a_w_in", "a_w_group", "a_w_out", "w_kv", "b_w_in", "b_w_out", "ple_w", "ple_gate_w")
SMALL = ("a_norm", "a_scale", "kv_norm", "b_norm", "k_norm", "b_q_norm")
SMALL_SHARDED = ("a_norm", "a_scale")
WEIGHTS = ("a_norm", "a_w_in", "a_w_group", "a_scale", "a_w_out", "kv_norm", "w_kv", "k_norm", "b_norm", "b_w_in",
           "b_q_norm", "b_w_out", "ple_w", "ple_gate_w")


def _as_matrix(a):
    return a.reshape(-1, a.shape[-1])


def _pack_small(arrs):
    rows = [jnp.pad(a.reshape(-1), (0, D_MODEL - a.size)) for a in arrs]
    rows += [jnp.zeros((D_MODEL,), F32)] * (8 - len(rows))
    return jnp.stack(rows)


def kernel(x, p, a_norm, a_w_in, a_w_group, a_scale, a_w_out, kv_norm, w_kv, k_norm, b_norm, b_w_in, b_q_norm, b_w_out, ple_w, ple_gate_w, loss_target, m_a_norm, m_a_w_in, m_a_w_group, m_a_scale, m_a_w_out, m_kv_norm, m_w_kv, m_k_norm, m_b_norm, m_b_w_in, m_b_q_norm, m_b_w_out, m_ple_w, m_ple_gate_w, v_a_norm, v_a_w_in, v_a_w_group, v_a_scale, v_a_w_out, v_kv_norm, v_w_kv, v_k_norm, v_b_norm, v_b_w_in, v_b_q_norm, v_b_w_out, v_ple_w, v_ple_gate_w):
    wts = dict(a_norm=a_norm, a_w_in=a_w_in, a_w_group=a_w_group, a_scale=a_scale, a_w_out=a_w_out, kv_norm=kv_norm,
               w_kv=w_kv, k_norm=k_norm, b_norm=b_norm, b_w_in=b_w_in, b_q_norm=b_q_norm, b_w_out=b_w_out,
               ple_w=ple_w, ple_gate_w=ple_gate_w)
    mom = dict(a_norm=m_a_norm, a_w_in=m_a_w_in, a_w_group=m_a_w_group, a_scale=m_a_scale, a_w_out=m_a_w_out,
               kv_norm=m_kv_norm, w_kv=m_w_kv, k_norm=m_k_norm, b_norm=m_b_norm, b_w_in=m_b_w_in,
               b_q_norm=m_b_q_norm, b_w_out=m_b_w_out, ple_w=m_ple_w, ple_gate_w=m_ple_gate_w)
    var = dict(a_norm=v_a_norm, a_w_in=v_a_w_in, a_w_group=v_a_w_group, a_scale=v_a_scale, a_w_out=v_a_w_out,
               kv_norm=v_kv_norm, w_kv=v_w_kv, k_norm=v_k_norm, b_norm=v_b_norm, b_w_in=v_b_w_in,
               b_q_norm=v_b_q_norm, b_w_out=v_b_w_out, ple_w=v_ple_w, ple_gate_w=v_ple_gate_w)
    S = x.shape[1]
    chip = 2 * lax.axis_index("x") + lax.axis_index("y")

    sharded_small = jnp.concatenate([a_norm.reshape(1, 256), a_scale.reshape(1, 256), jnp.zeros((6, 256), F32)], axis=0)
    later = ("a_w_group", "a_w_out", "w_kv", "b_w_in", "b_w_out", "ple_w", "ple_gate_w")
    (a_w_in_full,), small_full, copies = _allgather_weights(
        [_as_matrix(a_w_in)], sharded_small,
        [(_as_matrix(wts[n]), [(0, 256), (256, 512)] if n.startswith("ple") else [(0, _as_matrix(wts[n]).shape[0])])
         for n in later])
    local = dict(zip(("a_w_group", "a_w_out", "w_kv", "b_w_in", "b_w_out", "ple_w0", "ple_w1", "ple_gate_w0",
                      "ple_gate_w1"), copies))
    full = dict(a_w_in=a_w_in_full,
                a_norm=small_full[:, 0, :].reshape(1, D_MODEL), a_scale=small_full[:, 1, :].reshape(1, D_MODEL),
                kv_norm=kv_norm.reshape(1, D_MODEL), b_norm=b_norm.reshape(1, D_MODEL), k_norm=k_norm, b_q_norm=b_q_norm)

    def shards(t):
        out = {}
        for n in BIG:
            mat = _as_matrix(t[n])
            if n.startswith("ple"):
                out[n + "0"], out[n + "1"] = mat[:256], mat[256:]
            else:
                out[n] = mat
        return out

    state = (shards(wts), shards(mom), shards(var))
    grad_x, grads, updates, small_part = _local_step(x.reshape(S, D_MODEL), p, loss_target.reshape(S, D_MODEL),
                                                     full, local, state)

    names = sorted(grads)
    reduced = _reduce_adam_all([grads[n] for n in names], *[[t[n] for n in names] for t in state])
    for i, n in enumerate(names):
        updates[n] = tuple(group[i] for group in reduced)
    out_g, out_d, out_m, out_v = {}, {}, {}, {}
    for n in BIG:
        for i, out in enumerate((out_g, out_d, out_m, out_v)):
            if n.startswith("ple"):
                out[n] = jnp.stack([updates[n + "0"][i], updates[n + "1"][i]]).reshape(wts[n].shape)
            else:
                out[n] = updates[n][i].reshape(wts[n].shape)

    small_sum = _allreduce_small(small_part)
    loss = small_sum[len(SMALL), 0]
    small_rows = []
    for i, n in enumerate(SMALL):
        row = small_sum[i]
        if n in SMALL_SHARDED:
            row = lax.dynamic_slice(row, (chip * 256,), (256,))
        else:
            row = row[:wts[n].size]
        small_rows.append(row)
    g_small = _pack_small(small_rows)
    d_small, m_small, v_small = _adam_small(_pack_small([wts[n] for n in SMALL]), g_small,
                                            _pack_small([mom[n] for n in SMALL]), _pack_small([var[n] for n in SMALL]))
    for i, n in enumerate(SMALL):
        shape, size = wts[n].shape, wts[n].size
        out_g[n], out_d[n], out_m[n], out_v[n] = (t[i, :size].reshape(shape) for t in (g_small, d_small, m_small, v_small))

    return (loss, grad_x.reshape(1, S, D_MODEL), *[out_g[n] for n in WEIGHTS], *[out_d[n] for n in WEIGHTS],
            *[out_m[n] for n in WEIGHTS], *[out_v[n] for n in WEIGHTS])
```

```python
import functools

import jax
import jax.numpy as jnp
from jax import lax
from jax.experimental import pallas as pl
from jax.experimental.pallas import tpu as pltpu

F32 = jnp.float32
BF16 = jnp.bfloat16
MESH = pl.DeviceIdType.MESH

D_MODEL = 1024
N_HEADS = 16
HEAD_DIM = 64
PLE_DIM = 256
N_GROUPS = 4
GROUP_DIM = 256
POOL_WINDOWS = (2, 4, 8, 16)
N_CHIPS = 4
EPS = 1e-6
SB_SCALE = HEAD_DIM ** -0.5

ADAM_LR = 0.001
ADAM_B1 = 0.9
ADAM_B2 = 0.999
ADAM_EPS = 1e-08
ADAM_WD = 0.01
ADAM_STEP = 10

ROW_TILE = 256
WIDE_ROW_TILE = 512
EXP_UNDERFLOW = -104.0
ATT_Q_TILE = 512
ATT_K_TILE = 256
WGRAD_SEQ_TILE = 1024
WGRAD_ACC_BYTES = 4 * 1024 * 1024
MIB = 1024 * 1024


def _params(semantics=None, vmem_mib=48):
    return pltpu.CompilerParams(dimension_semantics=semantics, vmem_limit_bytes=vmem_mib * MIB)


def _dot(a, b):
    return jnp.dot(a, b, preferred_element_type=F32)


def _dot_nt(a, b):
    return lax.dot_general(a, b, (((1,), (1,)), ((), ())), preferred_element_type=F32)


def _dot_tn(a, b):
    return lax.dot_general(a, b, (((0,), (0,)), ((), ())), preferred_element_type=F32)


def _hilo(x):
    hi = x.astype(BF16)
    lo = (x - hi.astype(F32)).astype(BF16)
    return hi, lo


def _dot_hilo(x, w):
    hi, lo = _hilo(x)
    return _dot(hi, w) + _dot(lo, w)


def _sigmoid(z):
    return jax.nn.sigmoid(z)


def _dsilu(z, sg):
    return sg * (1.0 + z * (1.0 - sg))


def _mask_bf16(cond):
    return jnp.where(cond, 1.0, 0.0).astype(BF16)


def _head_mean_matrix():
    r = lax.broadcasted_iota(jnp.int32, (256, 256), 0) // HEAD_DIM
    c = lax.broadcasted_iota(jnp.int32, (256, 256), 1) // HEAD_DIM
    return _mask_bf16(r == c)


def _head_mean(x, bd):
    parts = []
    for s in range(x.shape[1] // 256):
        parts.append(_dot_hilo(x[:, s * 256:(s + 1) * 256], bd))
    out = parts[0] if len(parts) == 1 else jnp.concatenate(parts, axis=1)
    return out * (1.0 / HEAD_DIM)


def _a_in(x, gain, w_sh, gather=()):
    S = x.shape[0]
    tm = 512
    nsh, _, wn = w_sh.shape

    def body(x_ref, g_ref, w_ref, uz_ref, h_ref):
        @pl.when(pl.program_id(1) == 0)
        def _():
            xv = x_ref[...]
            r = lax.rsqrt(jnp.mean(xv * xv, axis=-1, keepdims=True) + EPS)
            h_ref[...] = (xv * r * g_ref[...]).astype(BF16)

        uz_ref[...] = _dot(h_ref[...], w_ref[0])

    return _call_with_gather(
        body, name="a_in", grid=(S // tm, nsh),
        in_specs=[pl.BlockSpec((tm, D_MODEL), lambda i, j: (i, 0)),
                  pl.BlockSpec((1, D_MODEL), lambda i, j: (0, 0)),
                  pl.BlockSpec((1, D_MODEL, wn), lambda i, j: (j, 0, 0))],
        out_specs=[pl.BlockSpec((tm, wn), lambda i, j: (i, j)),
                   pl.BlockSpec((tm, D_MODEL), lambda i, j: (i, 0))],
        out_shape=[jax.ShapeDtypeStruct((S, nsh * wn), F32),
                   jax.ShapeDtypeStruct((S, D_MODEL), BF16)],
        args=(x, gain, w_sh), gather=gather)


def _inv_count(first_row, rows, w):
    t1 = first_row + 1 + lax.broadcasted_iota(jnp.int32, (rows, 1), 0)
    return 1.0 / jnp.minimum(t1, w).astype(F32)


def _group_weight(wg_ref, g):
    return jnp.concatenate([wg_ref[sh, g] for sh in range(N_CHIPS)], axis=0)


def _a_mix(uz, wg, scale, gather=()):
    S = uz.shape[0]
    tm = ROW_TILE

    def body(u_ref, up_ref, z_ref, wg_ref, sc_ref, ga_ref, p_ref):
        i = pl.program_id(0)
        row = lax.broadcasted_iota(jnp.int32, (tm, tm), 0)
        col = lax.broadcasted_iota(jnp.int32, (tm, tm), 1)
        d = row - col
        for g, w in enumerate(POOL_WINDOWS):
            cols = slice(g * GROUP_DIM, (g + 1) * GROUP_DIM)
            t_main = _mask_bf16((d >= 0) & (d < w))
            t_halo = _mask_bf16(d + tm < w)
            u = u_ref[:, cols]
            up = jnp.where(i > 0, up_ref[:, cols], 0.0)
            hi, lo = _hilo(u)
            hip, lop = _hilo(up)
            wsum = _dot(t_main, hi) + _dot(t_main, lo) + _dot(t_halo, hip) + _dot(t_halo, lop)
            pooled = (wsum * _inv_count(i * tm, tm, w) - u).astype(BF16)
            p_ref[:, cols] = pooled
            mraw = _dot(pooled, _group_weight(wg_ref, g))
            z = z_ref[:, cols]
            ga_ref[:, cols] = (mraw * sc_ref[:, cols] * (z * _sigmoid(z))).astype(BF16)

    return _call_with_gather(
        body, name="a_mix", grid=(S // tm,),
        in_specs=[pl.BlockSpec((tm, D_MODEL), lambda i: (i, 0)),
                  pl.BlockSpec((tm, D_MODEL), lambda i: (jnp.maximum(i - 1, 0), 0)),
                  pl.BlockSpec((tm, D_MODEL), lambda i: (i, 1)),
                  pl.BlockSpec((N_CHIPS, N_GROUPS, 64, GROUP_DIM), lambda i: (0, 0, 0, 0)),
                  pl.BlockSpec((1, D_MODEL), lambda i: (0, 0))],
        out_specs=[pl.BlockSpec((tm, D_MODEL), lambda i: (i, 0)),
                   pl.BlockSpec((tm, D_MODEL), lambda i: (i, 0))],
        out_shape=[jax.ShapeDtypeStruct((S, D_MODEL), BF16),
                   jax.ShapeDtypeStruct((S, D_MODEL), BF16)],
        args=(uz, uz, uz, wg, scale), gather=gather)


def _out_ple(name, gated, x_in, w_out, p, layer, ple_w, ple_g, target=None, gather=()):
    S = x_in.shape[0]
    tm = WIDE_ROW_TILE
    with_loss = target is not None

    def body(*refs):
        if with_loss:
            g_ref, x_ref, wo_ref, p_ref, pw_ref, pg_ref, t_ref, xm_ref, dx_ref, e_ref, gt_ref, loss_ref = refs
        else:
            g_ref, x_ref, wo_ref, p_ref, pw_ref, pg_ref, xm_ref, xo_ref, e_ref, gt_ref = refs
        xm = x_ref[...] + _dot(g_ref[...], wo_ref[...])
        xm_ref[...] = xm
        pb = p_ref[...].astype(BF16)
        e = jnp.concatenate([_dot(pb, pw_ref[sh]) for sh in range(N_CHIPS)], axis=1)
        pg = jnp.concatenate([pg_ref[sh] for sh in range(N_CHIPS)], axis=0)
        gate = _sigmoid(_dot(xm.astype(BF16), pg))
        e_ref[...] = e.astype(BF16)
        gt_ref[...] = gate.astype(BF16)
        xo = xm + e * gate
        if with_loss:
            diff = xo - t_ref[...]
            dx_ref[...] = diff * (1.0 / D_MODEL)

            @pl.when(pl.program_id(0) == 0)
            def _():
                loss_ref[...] = jnp.zeros_like(loss_ref)

            loss_ref[...] += jnp.sum(diff * diff) * (0.5 / D_MODEL)
        else:
            xo_ref[...] = xo

    row = pl.BlockSpec((tm, D_MODEL), lambda i: (i, 0))
    in_specs = [row, row,
                pl.BlockSpec((D_MODEL, D_MODEL), lambda i: (0, 0)),
                pl.BlockSpec((None, None, tm, PLE_DIM), lambda i: (layer, 0, i, 0)),
                pl.BlockSpec((N_CHIPS, PLE_DIM, 256), lambda i: (0, 0, 0)),
                pl.BlockSpec((N_CHIPS, 256, D_MODEL), lambda i: (0, 0, 0))]
    args = [gated, x_in, w_out, p, ple_w, ple_g]
    out_specs = [row, row, row, row]
    out_shape = [jax.ShapeDtypeStruct((S, D_MODEL), F32), jax.ShapeDtypeStruct((S, D_MODEL), F32),
                 jax.ShapeDtypeStruct((S, D_MODEL), BF16), jax.ShapeDtypeStruct((S, D_MODEL), BF16)]
    if with_loss:
        in_specs.append(row)
        args.append(target)
        out_specs.append(pl.BlockSpec((8, 128), lambda i: (0, 0)))
        out_shape.append(jax.ShapeDtypeStruct((8, 128), F32))
    return _call_with_gather(body, name=name, grid=(S // tm,), in_specs=in_specs, out_specs=out_specs,
                             out_shape=out_shape, args=args, gather=gather)


def _b_in(x, kv_gain, b_gain, k_gain_t, q_gain_t, w_kv, w_in, gather=()):
    S = x.shape[0]
    tm = ROW_TILE

    def body(x_ref, kvg_ref, bg_ref, kg_ref, qg_ref, wkv_ref, win_ref,
             hkv_ref, hb_ref, kraw_ref, qraw_ref, k_ref, q_ref, v_ref, z_ref):
        xv = x_ref[...]
        y = xv * lax.rsqrt(jnp.mean(xv * xv, axis=-1, keepdims=True) + EPS)
        hkv = (y * kvg_ref[...]).astype(BF16)
        hb = (y * bg_ref[...]).astype(BF16)
        hkv_ref[...] = hkv
        hb_ref[...] = hb
        bd = _head_mean_matrix()

        def head_norm(raw, gain):
            rr = lax.rsqrt(_head_mean(raw * raw, bd) + EPS)
            return raw * rr * gain

        for sh in range(N_CHIPS):
            kvc = _dot(hkv, wkv_ref[sh])
            qzc = _dot(hb, win_ref[sh])
            cols = slice((sh % 2) * 512, (sh % 2) * 512 + 512)
            if sh < 2:
                kraw_ref[:, cols] = kvc.astype(BF16)
                qraw_ref[:, cols] = qzc.astype(BF16)
                k_ref[:, cols] = head_norm(kvc, kg_ref[:, cols]).astype(BF16)
                q_ref[:, cols] = (head_norm(qzc, qg_ref[:, cols]) * SB_SCALE).astype(BF16)
            else:
                v_ref[:, cols] = kvc.astype(BF16)
                z_ref[:, cols] = qzc.astype(BF16)

    row = pl.BlockSpec((tm, D_MODEL), lambda i: (i, 0))
    vec = pl.BlockSpec((1, D_MODEL), lambda i: (0, 0))
    wsp = pl.BlockSpec((N_CHIPS, D_MODEL, 512), lambda i: (0, 0, 0))
    return _call_with_gather(
        body, name="b_in", grid=(S // tm,),
        in_specs=[row, vec, vec, vec, vec, wsp, wsp],
        out_specs=[row] * 8,
        out_shape=[jax.ShapeDtypeStruct((S, D_MODEL), BF16)] * 8,
        args=(x, kv_gain, b_gain, k_gain_t, q_gain_t, w_kv, w_in), gather=gather, vmem_mib=56)


def _softplus_parts(z):
    e = jnp.exp(-jnp.abs(z))
    return -(jnp.maximum(z, 0.0) + jnp.log(1.0 + e)), e


def _add_rows(total, rows, update):
    lo, hi = rows
    parts = ([total[:lo]] if lo else []) + [total[lo:hi] + update] + ([total[hi:]] if hi < total.shape[0] else [])
    return parts[0] if len(parts) == 1 else jnp.concatenate(parts, axis=0)


def _attn_fwd(q, k, v, zgate):
    S = q.shape[0]
    tq, tk = ATT_Q_TILE, ATT_K_TILE
    kpq = tq // tk
    assert kpq == 2

    def body(q_ref, k_ref, v_ref, z_ref, o_ref, g_ref, lt_ref, steps_ref):
        qi = pl.program_id(1)
        lane = lax.broadcasted_iota(jnp.int32, (1, 128), 1)
        ri = lax.broadcasted_iota(jnp.int32, (tk, tk), 0)
        ci = lax.broadcasted_iota(jnp.int32, (tk, tk), 1)
        later_mat = _mask_bf16(ri > ci)
        causal = ci < ri
        qv = q_ref[...]
        first = lane < HEAD_DIM
        q_heads = (jnp.where(first, qv, jnp.zeros_like(qv)), jnp.where(first, jnp.zeros_like(qv), qv))

        def step(blocks, carry):
            chains = [(b, h) for b in range(len(blocks)) for h in range(2)]
            rows = [r for _, r, _ in blocks]
            s0 = [pl.multiple_of(kj * tk, tk) for kj, _, _ in blocks]
            kb = [k_ref[pl.ds(s, tk), :] for s in s0]
            vb = [v_ref[pl.ds(s, tk), :] for s in s0]
            visible = [causal if masked else None for _, _, masked in blocks]
            z = {c: _dot_nt(q_heads[c[1]][rows[c[0]][0]:rows[c[0]][1]], kb[c[0]]) for c in chains}
            run = [carry[0], carry[2]]
            log_own, later, run_at = {}, {}, {}
            for c in chains:
                b, h = c
                lk = _softplus_parts(z[c])[0]
                if visible[b] is not None:
                    lk = jnp.where(visible[b], lk, 0.0)
                log_own[c] = z[c] + lk
                later[c] = _dot(lk.astype(BF16), later_mat)
                run_at[c] = run[h][rows[b][0]:rows[b][1]]
                run[h] = _add_rows(run[h], rows[b], jnp.sum(lk, axis=-1, keepdims=True))
            acc = [carry[1], carry[3]]
            for c in chains:
                b, h = c
                a = jnp.exp(log_own[c] + later[c] + run_at[c])
                if visible[b] is not None:
                    a = jnp.where(visible[b], a, 0.0)
                acc[h] = _add_rows(acc[h], rows[b], _dot(a.astype(BF16), vb[b]))
            return run[0], acc[0], run[1], acc[1]

        zero1, zero128 = jnp.zeros((tq, 1), F32), jnp.zeros((tq, 128), F32)
        carry = step([(qi * kpq + 1, (tk, tq), True), (qi * kpq, (tk, tq), False), (qi * kpq, (0, tk), True)],
                     (zero1, zero128, zero1, zero128))

        def low(run):
            return jnp.max(run)

        def pair_more(c):
            return (c[0] < qi) & (jnp.maximum(low(c[1][tk:]), low(c[3][tk:])) > EXP_UNDERFLOW)

        def pair_step(c):
            last = (qi - c[0]) * kpq - 1
            return (c[0] + 1, *step([(last, (0, tq), False), (last - 1, (0, tq), False)], c[1:]))

        pairs, *carry = lax.while_loop(pair_more, pair_step, (jnp.int32(0), *carry))
        left = (qi - pairs) * kpq

        def single_more(c):
            return (c[0] < left) & (jnp.maximum(low(c[1][:tk]), low(c[3][:tk])) > EXP_UNDERFLOW)

        def single_step(c):
            return (c[0] + 1, *step([(left - 1 - c[0], (0, tk), False)], c[1:]))

        singles, *carry = lax.while_loop(single_more, single_step, (jnp.int32(0), *carry))
        steps_ref[...] = jnp.concatenate([jnp.full((4, 128), pairs, F32), jnp.full((4, 128), singles, F32)], axis=0)
        o_tot = jnp.where(first, carry[1], carry[3])
        l_tot = jnp.where(first, carry[0], carry[2])
        o_ref[...] = o_tot.astype(BF16)
        lt_ref[...] = l_tot
        zz = z_ref[...].astype(F32)
        g_ref[...] = (o_tot * (zz * _sigmoid(zz))).astype(BF16)

    blk = pl.BlockSpec((tq, 128), lambda hp, qi: (qi, hp))
    seq = pl.BlockSpec((S, 128), lambda hp, qi: (0, hp))
    return pl.pallas_call(
        body, name="attn_fwd", grid=(D_MODEL // 128, S // tq),
        in_specs=[blk, seq, seq, blk],
        out_specs=[blk, blk, blk, pl.BlockSpec((None, None, 8, 128), lambda hp, qi: (hp, qi, 0, 0))],
        out_shape=[jax.ShapeDtypeStruct((S, D_MODEL), BF16)] * 2 + [jax.ShapeDtypeStruct((S, D_MODEL), F32)]
        + [jax.ShapeDtypeStruct((D_MODEL // 128, S // tq, 8, 128), F32)],
        compiler_params=_params(("parallel", "arbitrary")),
    )(q, k, v, zgate)


def _ple_out_bwd(name, dx_out, e, gate, ple_g, w_out):
    S = dx_out.shape[0]
    tm = WIDE_ROW_TILE

    def body(dx_ref, e_ref, gt_ref, pg_ref, wo_ref, de_ref, dgp_ref, dxm_ref, dg_ref):
        dxo = dx_ref[...]
        ev = e_ref[...].astype(F32)
        gv = gt_ref[...].astype(F32)
        de_ref[...] = (dxo * gv).astype(BF16)
        dgp = (dxo * ev * gv * (1.0 - gv)).astype(BF16)
        dgp_ref[...] = dgp
        pg = jnp.concatenate([pg_ref[sh] for sh in range(N_CHIPS)], axis=0)
        dxm = dxo + _dot_nt(dgp, pg)
        dxm_ref[...] = dxm
        dg_ref[...] = _dot_nt(dxm.astype(BF16), wo_ref[...]).astype(BF16)

    row = pl.BlockSpec((tm, D_MODEL), lambda i: (i, 0))
    return pl.pallas_call(
        body, name=name, grid=(S // tm,),
        in_specs=[row, row, row,
                  pl.BlockSpec((N_CHIPS, 256, D_MODEL), lambda i: (0, 0, 0)),
                  pl.BlockSpec((D_MODEL, D_MODEL), lambda i: (0, 0))],
        out_specs=[row, row, row, row],
        out_shape=[jax.ShapeDtypeStruct((S, D_MODEL), BF16), jax.ShapeDtypeStruct((S, D_MODEL), BF16),
                   jax.ShapeDtypeStruct((S, D_MODEL), F32), jax.ShapeDtypeStruct((S, D_MODEL), BF16)],
        compiler_params=_params(("arbitrary",)),
    )(dx_out, e, gate, ple_g, w_out)


def _attn_bwd(q, k, v, ltot, steps, dgated, o, zgate, reduce=None):
    S = q.shape[0]
    tq, tk = ATT_Q_TILE, ATT_K_TILE
    kpq = tq // tk
    nq = S // tq

    def body(q_ref, k_ref, v_ref, lt_ref, steps_ref, dg_ref, o_ref, z_ref, dq_ref, dk_ref, dv_ref, dz_ref,
             dk_acc, dv_acc):
        qi = pl.program_id(1)

        @pl.when(qi == 0)
        def _():
            dk_acc[...] = jnp.zeros_like(dk_acc)
            dv_acc[...] = jnp.zeros_like(dv_acc)

        lane = lax.broadcasted_iota(jnp.int32, (1, 128), 1)
        ri = lax.broadcasted_iota(jnp.int32, (tk, tk), 0)
        ci = lax.broadcasted_iota(jnp.int32, (tk, tk), 1)
        later_mat = _mask_bf16(ri > ci)
        before_mat = _mask_bf16(ri < ci)
        causal = ci < ri
        zz = z_ref[...].astype(F32)
        sg = _sigmoid(zz)
        dgv = dg_ref[...].astype(F32)
        dz_ref[...] = (dgv * o_ref[...].astype(F32) * _dsilu(zz, sg)).astype(BF16)
        dob = (dgv * (zz * sg)).astype(BF16)
        ltv = lt_ref[...]
        qv = q_ref[...]
        first = lane < HEAD_DIM
        masks = (first, jnp.logical_not(first))
        q_heads = [jnp.where(hm, qv, jnp.zeros_like(qv)) for hm in masks]
        do_heads = [jnp.where(hm, dob, jnp.zeros_like(dob)) for hm in masks]
        totals = [jnp.max(jnp.where(hm, ltv, -jnp.inf), axis=-1, keepdims=True) for hm in masks]

        def step(blocks, carry):
            chains = [(b, h) for b in range(len(blocks)) for h in range(2)]
            rows = [r for _, r, _ in blocks]
            cut = lambda t, b: t[rows[b][0]:rows[b][1]]
            s0 = [pl.multiple_of(kj * tk, tk) for kj, _, _ in blocks]
            kb = [k_ref[pl.ds(s, tk), :] for s in s0]
            vb = [v_ref[pl.ds(s, tk), :] for s in s0]
            visible = [causal if masked else None for _, _, masked in blocks]
            z = {c: _dot_nt(cut(q_heads[c[1]], c[0]), kb[c[0]]) for c in chains}
            da = {c: _dot_nt(cut(do_heads[c[1]], c[0]), vb[c[0]]) for c in chains}
            run = [carry[0], carry[3]]
            log_own, beta, later, base = {}, {}, {}, {}
            for c in chains:
                b, h = c
                lk = _softplus_parts(z[c])[0]
                if visible[b] is not None:
                    lk = jnp.where(visible[b], lk, 0.0)
                log_own[c] = z[c] + lk
                beta[c] = jnp.exp(log_own[c]).astype(BF16)
                later[c] = _dot(lk.astype(BF16), later_mat)
                run[h] = _add_rows(run[h], rows[b], jnp.sum(lk, axis=-1, keepdims=True))
                base[c] = cut(totals[h] - run[h], b)
            grun = [carry[1], carry[4]]
            a_bf, g_bf, gbefore, grun_at = {}, {}, {}, {}
            for c in chains:
                b, h = c
                a = jnp.exp(log_own[c] + later[c] + base[c])
                if visible[b] is not None:
                    a = jnp.where(visible[b], a, 0.0)
                a_bf[c] = a.astype(BF16)
                g = da[c] * a
                g_bf[c] = g.astype(BF16)
                gbefore[c] = _dot(g_bf[c], before_mat)
                grun_at[c] = cut(grun[h], b)
                grun[h] = _add_rows(grun[h], rows[b], jnp.sum(g, axis=-1, keepdims=True))
            dq = [carry[2], carry[5]]
            dk_blk = [jnp.zeros((tk, 128), F32) for _ in blocks]
            dv_blk = [jnp.zeros((tk, 128), F32) for _ in blocks]
            for c in chains:
                b, h = c
                g = g_bf[c].astype(F32)
                dz = g - beta[c].astype(F32) * (g + gbefore[c] + grun_at[c])
                if visible[b] is not None:
                    dz = jnp.where(visible[b], dz, 0.0)
                dzb = dz.astype(BF16)
                dq[h] = _add_rows(dq[h], rows[b], _dot(dzb, kb[b]))
                dk_blk[b] = dk_blk[b] + _dot_tn(dzb, cut(q_heads[h], b))
                dv_blk[b] = dv_blk[b] + _dot_tn(a_bf[c], cut(do_heads[h], b))
            for b in range(len(blocks)):
                dk_acc[pl.ds(s0[b], tk), :] += dk_blk[b]
                dv_acc[pl.ds(s0[b], tk), :] += dv_blk[b]
            return run[0], grun[0], dq[0], run[1], grun[1], dq[1]

        pairs = jnp.clip(jnp.max(steps_ref[0:4, :]).astype(jnp.int32), 0, qi)
        left = (qi - pairs) * kpq
        singles = jnp.clip(jnp.max(steps_ref[4:8, :]).astype(jnp.int32), 0, left)
        zero1, zero128 = jnp.zeros((tq, 1), F32), jnp.zeros((tq, 128), F32)
        carry = lax.fori_loop(left - singles, left, lambda kj, c: step([(kj, (0, tk), False)], c),
                              (zero1, zero1, zero128, zero1, zero1, zero128))
        carry = lax.fori_loop(qi - pairs, qi,
                              lambda n, c: step([(n * kpq, (0, tq), False), (n * kpq + 1, (0, tq), False)], c), carry)
        carry = step([(qi * kpq, (0, tk), True), (qi * kpq, (tk, tq), False), (qi * kpq + 1, (tk, tq), True)], carry)
        dq_ref[...] = jnp.where(first, carry[2], carry[5]).astype(BF16)

        @pl.when(qi == nq - 1)
        def _():
            dk_ref[...] = dk_acc[...].astype(BF16)
            dv_ref[...] = dv_acc[...].astype(BF16)

    blk = pl.BlockSpec((tq, 128), lambda hp, qi: (qi, hp))
    seq = pl.BlockSpec((S, 128), lambda hp, qi: (0, hp))
    return _call_with_gather(
        body, name="attn_bwd", grid=(D_MODEL // 128, nq),
        in_specs=[blk, seq, seq, blk, pl.BlockSpec((None, None, 8, 128), lambda hp, qi: (hp, qi, 0, 0)),
                  blk, blk, blk],
        out_specs=[blk, seq, seq, blk],
        out_shape=[jax.ShapeDtypeStruct((S, D_MODEL), BF16)] * 4,
        scratch_shapes=[pltpu.VMEM((S, 128), F32), pltpu.VMEM((S, 128), F32)],
        args=(q, k, v, ltot, steps, dgated, o, zgate), reduce=reduce, vmem_mib=56)


def _rms_bwd(xv, dh_gain_sum):
    r = lax.rsqrt(jnp.mean(xv * xv, axis=-1, keepdims=True) + EPS)
    xhat = xv * r
    dx = r * (dh_gain_sum - xhat * jnp.mean(dh_gain_sum * xhat, axis=-1, keepdims=True))
    return dx, xhat


def _b_in_bwd(dq, dk, dv, dz, q_raw, k_raw, x, dx_mid, q_gain_t, k_gain_t, b_gain, kv_gain, w_in, w_kv):
    S = x.shape[0]
    tm = ROW_TILE

    def body(dq_ref, dk_ref, dv_ref, dz_ref, qr_ref, kr_ref, x_ref, dxm_ref, qg_ref, kg_ref, bg_ref, kvg_ref,
             win_ref, wkv_ref, dqz_ref, dkv_ref, dx_ref, small_ref):
        @pl.when(pl.program_id(0) == 0)
        def _():
            small_ref[...] = jnp.zeros_like(small_ref)

        bd = _head_mean_matrix()

        def head_norm_bwd(dy_ref, raw_ref, gain, scale):
            raw = raw_ref[...].astype(F32)
            rr = lax.rsqrt(_head_mean(raw * raw, bd) + EPS)
            xhat = raw * rr
            dy = dy_ref[...].astype(F32) * scale
            gdy = dy * gain
            draw = rr * (gdy - xhat * _head_mean(gdy * xhat, bd))
            return draw.astype(BF16), jnp.sum(dy * xhat, axis=0, keepdims=True)

        dqr, dqg = head_norm_bwd(dq_ref, qr_ref, qg_ref[...], SB_SCALE)
        dkr, dkg = head_norm_bwd(dk_ref, kr_ref, kg_ref[...], 1.0)
        dqz_ref[:, :D_MODEL] = dqr
        dqz_ref[:, D_MODEL:] = dz_ref[...]
        dkv_ref[:, :D_MODEL] = dkr
        dkv_ref[:, D_MODEL:] = dv_ref[...]
        dhb = jnp.zeros((tm, D_MODEL), F32)
        dhkv = jnp.zeros((tm, D_MODEL), F32)
        for sh in range(N_CHIPS):
            cols = slice(sh * 512, (sh + 1) * 512)
            dhb = dhb + _dot_nt(dqz_ref[:, cols], win_ref[sh])
            dhkv = dhkv + _dot_nt(dkv_ref[:, cols], wkv_ref[sh])
        dx, xhat = _rms_bwd(x_ref[...], dhb * bg_ref[...] + dhkv * kvg_ref[...])
        dx_ref[...] = dxm_ref[...] + dx
        small_ref[0:1, :] += dqg
        small_ref[1:2, :] += dkg
        small_ref[2:3, :] += jnp.sum(dhb * xhat, axis=0, keepdims=True)
        small_ref[3:4, :] += jnp.sum(dhkv * xhat, axis=0, keepdims=True)

    row = pl.BlockSpec((tm, D_MODEL), lambda i: (i, 0))
    wide = pl.BlockSpec((tm, 2 * D_MODEL), lambda i: (i, 0))
    vec = pl.BlockSpec((1, D_MODEL), lambda i: (0, 0))
    wsp = pl.BlockSpec((N_CHIPS, D_MODEL, 512), lambda i: (0, 0, 0))
    return pl.pallas_call(
        body, name="b_in_bwd", grid=(S // tm,),
        in_specs=[row] * 8 + [vec] * 4 + [wsp, wsp],
        out_specs=[wide, wide, row, pl.BlockSpec((8, D_MODEL), lambda i: (0, 0))],
        out_shape=[jax.ShapeDtypeStruct((S, 2 * D_MODEL), BF16), jax.ShapeDtypeStruct((S, 2 * D_MODEL), BF16),
                   jax.ShapeDtypeStruct((S, D_MODEL), F32), jax.ShapeDtypeStruct((8, D_MODEL), F32)],
        compiler_params=_params(("arbitrary",), 56),
    )(dq, dk, dv, dz, q_raw, k_raw, x, dx_mid, q_gain_t, k_gain_t, b_gain, kv_gain, w_in, w_kv)


def _a_mix_bwd(dgated, uz, pooled, wg, scale, w_in, x, dx_mid, gain, reduce=None):
    S = x.shape[0]
    tm = ROW_TILE
    n = S // tm

    def body(dg_ref, z_ref, p_ref, wg_ref, sc_ref, win_ref, x_ref, dxm_ref, gn_ref,
             duz_ref, dmr_ref, dx_ref, small_ref, halo_hi, halo_lo):
        i = pl.program_id(0)

        @pl.when(i == 0)
        def _():
            small_ref[...] = jnp.zeros_like(small_ref)
            halo_hi[...] = jnp.zeros_like(halo_hi)
            halo_lo[...] = jnp.zeros_like(halo_lo)

        first_row = (n - 1 - i) * tm
        row = lax.broadcasted_iota(jnp.int32, (tm, tm), 0)
        col = lax.broadcasted_iota(jnp.int32, (tm, tm), 1)
        d = col - row
        for g, w in enumerate(POOL_WINDOWS):
            cols = slice(g * GROUP_DIM, (g + 1) * GROUP_DIM)
            wgg = _group_weight(wg_ref, g)
            sc = sc_ref[:, cols]
            mraw = _dot(p_ref[:, cols], wgg)
            z = z_ref[:, cols]
            sg = _sigmoid(z)
            dga = dg_ref[:, cols].astype(F32)
            dm = dga * (z * sg)
            duz_ref[:, D_MODEL + g * GROUP_DIM:D_MODEL + (g + 1) * GROUP_DIM] = (
                dga * (mraw * sc) * _dsilu(z, sg)).astype(BF16)
            small_ref[0:1, cols] += jnp.sum(dm * mraw, axis=0, keepdims=True)
            dmr = (dm * sc).astype(BF16)
            dmr_ref[:, cols] = dmr
            dp = _dot_nt(dmr, wgg)
            hi, lo = _hilo(dp * _inv_count(first_row, tm, w))
            t_main = _mask_bf16((d >= 0) & (d < w))
            t_halo = _mask_bf16(d + tm < w)
            du = (_dot(t_main, hi) + _dot(t_main, lo) + _dot(t_halo, halo_hi[:, cols]) + _dot(t_halo, halo_lo[:, cols])
                  - dp)
            halo_hi[:, cols] = hi
            halo_lo[:, cols] = lo
            duz_ref[:, cols] = du.astype(BF16)
        dh = jnp.zeros((tm, D_MODEL), F32)
        for sh in range(N_CHIPS):
            dh = dh + _dot_nt(duz_ref[:, sh * 512:(sh + 1) * 512], win_ref[sh])
        dx, xhat = _rms_bwd(x_ref[...], dh * gn_ref[...])
        dx_ref[...] = dxm_ref[...] + dx
        small_ref[1:2, :] += jnp.sum(dh * xhat, axis=0, keepdims=True)

    rev = lambda i: (n - 1 - i, 0)
    row = pl.BlockSpec((tm, D_MODEL), rev)
    vec = pl.BlockSpec((1, D_MODEL), lambda i: (0, 0))
    return _call_with_gather(
        body, name="a_mix_bwd", grid=(n,),
        in_specs=[row,
                  pl.BlockSpec((tm, D_MODEL), lambda i: (n - 1 - i, 1)),
                  row,
                  pl.BlockSpec((N_CHIPS, N_GROUPS, 64, GROUP_DIM), lambda i: (0, 0, 0, 0)),
                  vec,
                  pl.BlockSpec((N_CHIPS, D_MODEL, 512), lambda i: (0, 0, 0)),
                  row, row, vec],
        out_specs=[pl.BlockSpec((tm, 2 * D_MODEL), rev), row, row,
                   pl.BlockSpec((8, D_MODEL), lambda i: (0, 0))],
        out_shape=[jax.ShapeDtypeStruct((S, 2 * D_MODEL), BF16), jax.ShapeDtypeStruct((S, D_MODEL), BF16),
                   jax.ShapeDtypeStruct((S, D_MODEL), F32), jax.ShapeDtypeStruct((8, D_MODEL), F32)],
        scratch_shapes=[pltpu.VMEM((tm, D_MODEL), BF16), pltpu.VMEM((tm, D_MODEL), BF16)],
        args=(dgated, uz, pooled, wg, scale, w_in, x, dx_mid, gain), reduce=reduce, vmem_mib=56)


def _wgrad(name, a, dy, n_shards, a_spec=None, k_dim=None):
    S, n_cols = dy.shape
    ts = WGRAD_SEQ_TILE
    k_dim = a.shape[-1] if k_dim is None else k_dim
    wn = n_cols // n_shards
    tk = min(k_dim, WGRAD_ACC_BYTES // (4 * n_cols))
    nst = S // ts

    def body(a_ref, dy_ref, out_ref, acc):
        st = pl.program_id(1)

        @pl.when(st == 0)
        def _():
            acc[...] = jnp.zeros_like(acc)

        acc[...] += _dot_tn(a_ref[...].astype(BF16), dy_ref[...].astype(BF16))

        @pl.when(st == nst - 1)
        def _():
            for sh in range(n_shards):
                out_ref[sh] = acc[:, sh * wn:(sh + 1) * wn]

    if a_spec is None:
        a_spec = pl.BlockSpec((ts, tk), lambda kt, st: (st, kt))
    return pl.pallas_call(
        body, name=name, grid=(k_dim // tk, nst),
        in_specs=[a_spec, pl.BlockSpec((ts, n_cols), lambda kt, st: (st, 0))],
        out_specs=pl.BlockSpec((n_shards, tk, wn), lambda kt, st: (0, kt, 0)),
        out_shape=jax.ShapeDtypeStruct((n_shards, k_dim, wn), F32),
        scratch_shapes=[pltpu.VMEM((tk, n_cols), F32)],
        compiler_params=_params(("parallel", "arbitrary")),
    )(a, dy)


def _wgrad_ple(name, p, layer, de):
    ts = WGRAD_SEQ_TILE
    spec = pl.BlockSpec((None, None, ts, PLE_DIM), lambda kt, st: (layer, 0, st, 0))
    return _wgrad(name, p, de, N_CHIPS, a_spec=spec, k_dim=PLE_DIM)


def _wgrad_group(pooled, dmr):
    S = pooled.shape[0]
    ts = WGRAD_SEQ_TILE
    nst = S // ts

    def body(p_ref, d_ref, out_ref, acc):
        st = pl.program_id(1)

        @pl.when(st == 0)
        def _():
            acc[...] = jnp.zeros_like(acc)

        acc[...] += _dot_tn(p_ref[...], d_ref[...])

        @pl.when(st == nst - 1)
        def _():
            for sh in range(N_CHIPS):
                out_ref[sh] = acc[sh * 64:(sh + 1) * 64, :]

    blk = pl.BlockSpec((ts, GROUP_DIM), lambda g, st: (st, g))
    return pl.pallas_call(
        body, name="wgrad_group", grid=(N_GROUPS, nst),
        in_specs=[blk, blk],
        out_specs=pl.BlockSpec((N_CHIPS, None, 64, GROUP_DIM), lambda g, st: (0, g, 0, 0)),
        out_shape=jax.ShapeDtypeStruct((N_CHIPS, N_GROUPS, 64, GROUP_DIM), F32),
        scratch_shapes=[pltpu.VMEM((GROUP_DIM, GROUP_DIM), F32)],
        compiler_params=_params(("parallel", "arbitrary")),
    )(pooled, dmr)


GATHER_AT = {
    "a_in": ("a_w_group", "a_w_out", "ple_w0", "ple_gate_w0"),
    "a_mix": ("w_kv",),
    "a_out_ple": ("b_w_in",),
    "b_in": ("b_w_out", "ple_w1", "ple_gate_w1"),
}


REDUCE_AT = {
    "attn_bwd": ("b_w_out", "ple_w1", "ple_gate_w1"),
    "a_mix_bwd": ("w_kv", "b_w_in"),
}


def _local_step(x, p, target, w, local=None, state=None):
    w = dict(w)

    def run(fn, host, n_out, *args, **kwargs):
        names = GATHER_AT[host] if local is not None else ()
        res = fn(*args, gather=[local[n] for n in names], **kwargs)
        w.update(zip(names, res[n_out:]))
        return res[:n_out]

    k_gain_t = jnp.tile(w["k_norm"].reshape(1, HEAD_DIM), (1, N_HEADS))
    q_gain_t = jnp.tile(w["b_q_norm"].reshape(1, HEAD_DIM), (1, N_HEADS))

    uz, h_a = run(_a_in, "a_in", 2, x, w["a_norm"], w["a_w_in"])
    wg4 = w["a_w_group"].reshape(N_CHIPS, N_GROUPS, 64, GROUP_DIM)
    wa_out = w["a_w_out"].reshape(D_MODEL, D_MODEL)
    gated_a, pooled = run(_a_mix, "a_mix", 2, uz, wg4, w["a_scale"])
    x1, x2, e_a, gate_a = run(_out_ple, "a_out_ple", 4, "a_out_ple", gated_a, x, wa_out, p, 0,
                              w["ple_w0"], w["ple_gate_w0"])
    h_kv, h_b, k_raw, q_raw, k, q, v, z_b = run(
        _b_in, "b_in", 8, x2, w["kv_norm"], w["b_norm"], k_gain_t, q_gain_t, w["w_kv"], w["b_w_in"])
    wb_out = w["b_w_out"].reshape(D_MODEL, D_MODEL)
    o, gated_b, ltot, att_steps = _attn_fwd(q, k, v, z_b)
    x3, dx4, e_b, gate_b, loss_blk = _out_ple("b_out_ple", gated_b, x2, wb_out, p, 1, w["ple_w1"], w["ple_gate_w1"],
                                              target=target)

    grads, updates = {}, {}

    def hosted(fn, host, n_out, *args):
        if state is None:
            return fn(*args)
        names = REDUCE_AT[host]
        res = fn(*args, reduce=([grads.pop(n) for n in names], *[[t[n] for n in names] for t in state]))
        for i, n in enumerate(names):
            updates[n] = tuple(group[i] for group in res[n_out:])
        return res[:n_out]

    de_b, dgp_b, dx3, dgated_b = _ple_out_bwd("b_ple_out_bwd", dx4, e_b, gate_b, w["ple_gate_w1"], wb_out)
    grads["b_w_out"] = _wgrad("wgrad_b_out", gated_b, dx3, 1).reshape(N_CHIPS, 256, D_MODEL)
    grads["ple_w1"] = _wgrad_ple("wgrad_ple1", p, 1, de_b)
    grads["ple_gate_w1"] = _wgrad("wgrad_gate1", x3, dgp_b, 1).reshape(N_CHIPS, 256, D_MODEL)
    dq, dk, dv, dz_b = hosted(_attn_bwd, "attn_bwd", 4, q, k, v, ltot, att_steps, dgated_b, o, z_b)
    dqz, dkv, dx2, small_b = _b_in_bwd(dq, dk, dv, dz_b, q_raw, k_raw, x2, dx3, q_gain_t, k_gain_t,
                                       w["b_norm"], w["kv_norm"], w["b_w_in"], w["w_kv"])
    grads["w_kv"] = _wgrad("wgrad_kv", h_kv, dkv, N_CHIPS)
    grads["b_w_in"] = _wgrad("wgrad_b_in", h_b, dqz, N_CHIPS)
    de_a, dgp_a, dx1, dgated_a = _ple_out_bwd("a_ple_out_bwd", dx2, e_a, gate_a, w["ple_gate_w0"], wa_out)
    grads["a_w_out"] = _wgrad("wgrad_a_out", gated_a, dx1, 1).reshape(N_CHIPS, 256, D_MODEL)
    grads["ple_w0"] = _wgrad_ple("wgrad_ple0", p, 0, de_a)
    grads["ple_gate_w0"] = _wgrad("wgrad_gate0", x1, dgp_a, 1).reshape(N_CHIPS, 256, D_MODEL)
    duz, dmr, grad_x, small_a = hosted(_a_mix_bwd, "a_mix_bwd", 4, dgated_a, uz, pooled, wg4, w["a_scale"],
                                       w["a_w_in"], x, dx1, w["a_norm"])
    grads["a_w_in"] = _wgrad("wgrad_a_in", h_a, duz, N_CHIPS)
    grads["a_w_group"] = _wgrad_group(pooled, dmr).reshape(N_CHIPS, N_GROUPS * 64, GROUP_DIM)

    fold = lambda row: jnp.pad(row.reshape(N_HEADS, HEAD_DIM).sum(axis=0), (0, D_MODEL - HEAD_DIM))
    small = jnp.stack([small_a[1], small_a[0], small_b[3], small_b[2], fold(small_b[1]), fold(small_b[0]),
                       jnp.pad(loss_blk[0], (0, D_MODEL - loss_blk.shape[1])), jnp.zeros((D_MODEL,), F32)])
    return grad_x, grads, updates, small


def _mesh_place():
    x, y, c = lax.axis_index("x"), lax.axis_index("y"), lax.axis_index("c")
    other_chips = [(1 - x, y), (x, 1 - y), (1 - x, 1 - y)]
    return x, y, c, other_chips


def _gather_sems(n):
    return [pltpu.SemaphoreType.DMA((3 * n,)), pltpu.SemaphoreType.DMA((3 * n,)),
            pltpu.SemaphoreType.DMA((3 * n,)), pltpu.SemaphoreType.DMA((3 * n,)), pltpu.SemaphoreType.DMA((n,))]


def _gather_copies(srcs, outs, sems):
    send_far, recv_far, send_sib, recv_sib, local_sem = sems
    n = len(srcs)
    x, y, c, chips = _mesh_place()
    me = 2 * x + y
    sibling = (x, y, 1 - c)

    def half(k, which):
        rows = srcs[k].shape[0] // 2
        return pl.ds(pl.multiple_of(which * rows, 16), rows)

    local = [pltpu.make_async_copy(srcs[k], outs[k].at[me], local_sem.at[k]) for k in range(n)]
    far = [pltpu.make_async_remote_copy(
        src_ref=srcs[k].at[half(k, c)], dst_ref=outs[k].at[me, half(k, c)],
        send_sem=send_far.at[j * n + k], recv_sem=recv_far.at[j * n + k], device_id=(px, py, c), device_id_type=MESH)
        for j, (px, py) in enumerate(chips) for k in range(n)]

    def landed(j, k, which, from_far):
        px, py = chips[j]
        piece = outs[k].at[2 * px + py, half(k, which)]
        send, recv = (send_far, recv_far) if from_far else (send_sib, recv_sib)
        return pltpu.make_async_remote_copy(src_ref=piece, dst_ref=piece, send_sem=send.at[j * n + k],
                                            recv_sem=recv.at[j * n + k], device_id=sibling, device_id_type=MESH)

    return local, far, landed, c


def _gather_start(srcs, outs, sems):
    local, far, _, _ = _gather_copies(srcs, outs, sems)
    for cp in local + far:
        cp.start()


def _gather_pass_on(srcs, outs, sems):
    _, _, landed, c = _gather_copies(srcs, outs, sems)
    for j in range(3):
        for k in range(len(srcs)):
            landed(j, k, c, True).wait_recv()
            landed(j, k, c, False).start()


def _gather_finish(srcs, outs, sems):
    local, far, landed, c = _gather_copies(srcs, outs, sems)
    pairs = [(j, k) for j in range(3) for k in range(len(srcs))]
    for j, k in pairs:
        landed(j, k, 1 - c, False).wait_recv()
    for cp in far + [landed(j, k, c, False) for j, k in pairs]:
        cp.wait_send()
    for cp in local:
        cp.wait()


def _call_with_gather(body, *, name, grid, in_specs, out_specs, out_shape, args, gather=(), reduce=None,
                      scratch_shapes=(), vmem_mib=48):
    n_in, n_out, n_scr, n_g = len(args), len(out_shape), len(scratch_shapes), len(gather)
    n_r = len(reduce[0]) if reduce else 0
    pieces = _reduce_pieces(reduce[0]) if reduce else []
    reduce_args = [a for group in reduce for a in group] if reduce else []
    gather_sems = _gather_sems(n_g) if n_g else []
    n_steps = 1
    for g in grid:
        n_steps *= g

    def wrapped(*refs):
        refs = list(refs)
        take = lambda count: [refs.pop(0) for _ in range(count)]
        ins, g_in, r_in = take(n_in), take(n_g), take(4 * n_r)
        outs, g_out, r_out = take(n_out), take(n_g), take(4 * n_r)
        scratch, sems, r_scratch = take(n_scr), take(len(gather_sems)), refs
        step = 0
        for axis, g in enumerate(grid):
            step = step * g + pl.program_id(axis)
        if n_g:
            @pl.when(step == 0)
            def _():
                _gather_start(g_in, g_out, sems)

        if n_r:
            ticks, drain = _reduce_ticks(pieces, n_r, (*r_in, *r_out, *r_scratch))
            for t, tick in enumerate(ticks[:n_steps]):
                pl.when(step == t)(tick)

        body(*ins, *outs, *scratch)
        if n_r:
            for tick in ticks[n_steps:]:
                pl.when(step == n_steps - 1)(tick)
            pl.when(step == n_steps - 1)(drain)
        if n_g:
            @pl.when(step == max(n_steps - 2, 0))
            def _():
                _gather_pass_on(g_in, g_out, sems)

            @pl.when(step == n_steps - 1)
            def _():
                _gather_finish(g_in, g_out, sems)

    hbm = pl.BlockSpec(memory_space=pltpu.HBM)
    res = pl.pallas_call(
        wrapped, name=name, grid=grid,
        in_specs=list(in_specs) + [hbm] * (n_g + 4 * n_r), out_specs=list(out_specs) + [hbm] * (n_g + 4 * n_r),
        out_shape=list(out_shape) + [jax.ShapeDtypeStruct((N_CHIPS,) + g.shape, BF16) for g in gather]
        + ([jax.ShapeDtypeStruct(w.shape, F32) for _ in range(4) for w in reduce[1]] if reduce else []),
        scratch_shapes=list(scratch_shapes) + gather_sems + (_reduce_scratch() if reduce else []),
        compiler_params=_params(("arbitrary",) * len(grid), vmem_mib),
    )(*args, *gather, *reduce_args)
    if not reduce:
        return res
    plain = list(res[:n_out + n_g])
    return plain + [res[n_out + n_g + i * n_r:n_out + n_g + (i + 1) * n_r] for i in range(4)]


def _allgather_weights(shards, small, casts):
    n = len(shards)
    cast_out = [(k, r0, r1) for k, (_, ranges) in enumerate(casts) for r0, r1 in ranges]
    n_c, n_co = len(casts), len(cast_out)

    def body(*refs):
        ins, small_in, cast_in = refs[:n], refs[n], refs[n + 1:n + 1 + n_c]
        refs = refs[n + 1 + n_c:]
        outs, small_out, cast_dst = refs[:n], refs[n], refs[n + 1:n + 1 + n_co]
        refs = refs[n + 1 + n_co:]
        cast, cast_buf = refs[:n], refs[n:n + n_co]
        send_far, recv_far, send_sib, recv_sib, send_small, recv_small, local_sem, cast_sem = refs[n + n_co:]
        x, y, c, chips = _mesh_place()
        me = 2 * x + y
        sibling = (x, y, 1 - c)

        def half(k, which):
            rows = ins[k].shape[0] // 2
            return pl.ds(pl.multiple_of(which * rows, 16), rows)

        local = []
        for k in range(n):
            cast[k][...] = ins[k][...].astype(BF16)
            local.append(pltpu.make_async_copy(cast[k], outs[k].at[me], local_sem.at[k]))
            local[-1].start()
        local.append(pltpu.make_async_copy(small_in, small_out.at[me], local_sem.at[n]))
        local[-1].start()

        sends = []
        for j, (px, py) in enumerate(chips):
            for k in range(n):
                cp = pltpu.make_async_remote_copy(
                    src_ref=cast[k].at[half(k, c)], dst_ref=outs[k].at[me, half(k, c)],
                    send_sem=send_far.at[j * n + k], recv_sem=recv_far.at[j * n + k],
                    device_id=(px, py, c), device_id_type=MESH)
                cp.start()
                sends.append(cp)
            cp = pltpu.make_async_remote_copy(
                src_ref=small_in, dst_ref=small_out.at[me], send_sem=send_small.at[j], recv_sem=recv_small.at[j],
                device_id=(px, py, c), device_id_type=MESH)
            cp.start()
            sends.append(cp)

        for i, (k, r0, r1) in enumerate(cast_out):
            cast_buf[i][...] = cast_in[k][r0:r1, :].astype(BF16)
            local.append(pltpu.make_async_copy(cast_buf[i], cast_dst[i], cast_sem.at[i]))
            local[-1].start()

        def landed(j, k, which, sems_s, sems_r, device):
            px, py = chips[j]
            piece = outs[k].at[2 * px + py, half(k, which)]
            return pltpu.make_async_remote_copy(
                src_ref=piece, dst_ref=piece, send_sem=sems_s.at[j * n + k], recv_sem=sems_r.at[j * n + k],
                device_id=device, device_id_type=MESH)

        for j in range(len(chips)):
            for k in range(n):
                landed(j, k, c, send_far, recv_far, sibling).wait_recv()
                cp = landed(j, k, c, send_sib, recv_sib, sibling)
                cp.start()
                sends.append(cp)
        for j, (px, py) in enumerate(chips):
            for k in range(n):
                landed(j, k, 1 - c, send_sib, recv_sib, sibling).wait_recv()
            pltpu.make_async_remote_copy(
                src_ref=small_in, dst_ref=small_out.at[2 * px + py], send_sem=send_small.at[j],
                recv_sem=recv_small.at[j], device_id=(px, py, c), device_id_type=MESH).wait_recv()
        for cp in sends:
            cp.wait_send()
        for cp in local:
            cp.wait()

    vmem = pl.BlockSpec(memory_space=pltpu.VMEM)
    hbm = pl.BlockSpec(memory_space=pltpu.HBM)
    cast_shapes = [(r1 - r0, casts[k][0].shape[1]) for k, r0, r1 in cast_out]
    res = pl.pallas_call(
        body, name="allgather_weights",
        in_specs=[vmem] * (n + 1 + n_c), out_specs=[hbm] * (n + 1 + n_co),
        out_shape=[jax.ShapeDtypeStruct((N_CHIPS,) + s.shape, BF16) for s in shards]
        + [jax.ShapeDtypeStruct((N_CHIPS,) + small.shape, F32)]
        + [jax.ShapeDtypeStruct(s, BF16) for s in cast_shapes],
        scratch_shapes=[pltpu.VMEM(s.shape, BF16) for s in shards] + [pltpu.VMEM(s, BF16) for s in cast_shapes]
        + [pltpu.SemaphoreType.DMA((3 * n,)), pltpu.SemaphoreType.DMA((3 * n,)),
           pltpu.SemaphoreType.DMA((3 * n,)), pltpu.SemaphoreType.DMA((3 * n,)),
           pltpu.SemaphoreType.DMA((3,)), pltpu.SemaphoreType.DMA((3,)),
           pltpu.SemaphoreType.DMA((n + 1,)), pltpu.SemaphoreType.DMA((n_co,))],
        compiler_params=_params(None, 40),
    )(*shards, small, *[a for a, _ in casts])
    return res[:n], res[n], res[n + 1:]


def _adamw(w, g, m, v):
    m = ADAM_B1 * m + (1.0 - ADAM_B1) * g
    v = ADAM_B2 * v + (1.0 - ADAM_B2) * (g * g)
    m_hat = m / (1.0 - ADAM_B1 ** ADAM_STEP)
    v_hat = v / (1.0 - ADAM_B2 ** ADAM_STEP)
    delta = -ADAM_LR * (m_hat / (jnp.sqrt(v_hat) + ADAM_EPS) + ADAM_WD * w)
    return delta, m, v


RS_PIECE_ROWS = 128
RS_PIECE_COLS = 512


def _reduce_adam_all(grads, ws, ms, vs):
    n_w = len(grads)
    pieces = _reduce_pieces(grads)

    def body(*refs):
        ticks, drain = _reduce_ticks(pieces, n_w, refs)
        for tick in ticks:
            tick()
        drain()

    hbm = pl.BlockSpec(memory_space=pltpu.HBM)
    outs = pl.pallas_call(
        body, name="reduce_adam_all",
        in_specs=[hbm] * (4 * n_w), out_specs=[hbm] * (4 * n_w),
        out_shape=[jax.ShapeDtypeStruct(w.shape, F32) for _ in range(4) for w in ws],
        scratch_shapes=_reduce_scratch(),
        compiler_params=_params(None, 48),
    )(*grads, *ws, *ms, *vs)
    return [outs[i * n_w:(i + 1) * n_w] for i in range(4)]


def _reduce_pieces(grads):
    pieces = []
    for k, g in enumerate(grads):
        hr, cols = g.shape[1] // 2, g.shape[2]
        pr, pc = min(hr, RS_PIECE_ROWS), min(cols, RS_PIECE_COLS)
        pieces += [(k, ro, hr, co, pr, pc) for ro in range(0, hr, pr) for co in range(0, cols, pc)]
    return pieces


def _reduce_scratch():
    P, C = RS_PIECE_ROWS, RS_PIECE_COLS
    return [
        pltpu.VMEM((3, N_CHIPS, P, C), F32), pltpu.VMEM((3, N_CHIPS, P, C), F32),
        pltpu.VMEM((2, N_CHIPS, P, C), BF16), pltpu.VMEM((2, N_CHIPS, P, C), BF16),
        pltpu.VMEM((2, N_CHIPS, P, C), F32),
        pltpu.VMEM((2, 3, P, C), BF16), pltpu.VMEM((2, 3, P, C), BF16),
        pltpu.VMEM((2, 2, P, C), F32),
        pltpu.VMEM((2, 3, 2, P, C), F32), pltpu.VMEM((2, 4, 2, P, C), F32),
        pltpu.SemaphoreType.DMA((3, 2)), pltpu.SemaphoreType.DMA((2, 3, 2)),
        pltpu.SemaphoreType.DMA((2,)), pltpu.SemaphoreType.DMA((2,)),
        pltpu.SemaphoreType.DMA((2, 3)), pltpu.SemaphoreType.DMA((2, 3)),
        pltpu.SemaphoreType.DMA((2,)), pltpu.SemaphoreType.DMA((2,)),
        pltpu.SemaphoreType.DMA((2, 4, 2))]


def _reduce_ticks(pieces, n_w, refs):
    n = len(pieces)

    def build(*refs):
        g_in, w_in, m_in, v_in = (refs[i * n_w:(i + 1) * n_w] for i in range(4))
        g_out, d_out, m_out, v_out = (refs[(4 + i) * n_w:(5 + i) * n_w] for i in range(4))
        (gm, go, sb1, rb1, part, sb2, rb2, fin, wmv, outs,
         ld_sem, wmv_sem, s1_send, s1_recv, s2_send, s2_recv, s3_send, s3_recv, out_sem) = refs[8 * n_w:]
        x, y, c, chips = _mesh_place()
        me = 2 * x + y
        sibling = (x, y, 1 - c)

        def at_hbm(i, which):
            _, ro, hr, co, pr, pc = pieces[i]
            half = c if which == 0 else 1 - c
            return pl.ds(pl.multiple_of(half * hr + ro, 64), pr), pl.ds(co, pc)

        def win(i):
            return pl.ds(0, pieces[i][4]), pl.ds(0, pieces[i][5])

        every = slice(None)

        def loads(i):
            k, s = pieces[i][0], i % 3
            return [pltpu.make_async_copy(g_in[k].at[(every,) + at_hbm(i, h)], buf.at[(s, every) + win(i)], ld_sem.at[s, h])
                    for h, buf in enumerate((gm, go))]

        def wmv_loads(i):
            k, s = pieces[i][0], i % 2
            return [pltpu.make_async_copy(src[k].at[at_hbm(i, h)], wmv.at[(s, a, h) + win(i)], wmv_sem.at[s, a, h])
                    for a, src in enumerate((w_in, m_in, v_in)) for h in range(2)]

        def stores(i):
            k, s = pieces[i][0], i % 2
            return [pltpu.make_async_copy(outs.at[(s, a, h) + win(i)], dst[k].at[at_hbm(i, h)], out_sem.at[s, a, h])
                    for a, dst in enumerate((g_out, d_out, m_out, v_out)) for h in range(2)]

        def swap1(i):
            s = i % 2
            return pltpu.make_async_remote_copy(
                src_ref=sb1.at[(s, every) + win(i)], dst_ref=rb1.at[(s, every) + win(i)],
                send_sem=s1_send.at[s], recv_sem=s1_recv.at[s], device_id=sibling, device_id_type=MESH)

        def far2(i, j):
            s = i % 2
            px, py = chips[j]
            return pltpu.make_async_remote_copy(
                src_ref=sb2.at[(s, j) + win(i)], dst_ref=rb2.at[(s, j) + win(i)],
                send_sem=s2_send.at[s, j], recv_sem=s2_recv.at[s, j], device_id=(px, py, c), device_id_type=MESH)

        def swap3(i):
            s = i % 2
            return pltpu.make_async_remote_copy(
                src_ref=fin.at[(s, 0) + win(i)], dst_ref=fin.at[(s, 1) + win(i)],
                send_sem=s3_send.at[s], recv_sem=s3_recv.at[s], device_id=sibling, device_id_type=MESH)

        def stage0(i):
            for cp in loads(i):
                cp.start()

        def stage1(i):
            s, s3 = i % 2, i % 3
            for cp in loads(i):
                cp.wait()
            sb1[(s, every) + win(i)] = go[(s3, every) + win(i)].astype(BF16)
            swap1(i).start()

        def stage2(i):
            s, s3 = i % 2, i % 3
            swap1(i).wait()
            part[(s, every) + win(i)] = gm[(s3, every) + win(i)] + rb1[(s, every) + win(i)].astype(F32)
            for j, (px, py) in enumerate(chips):
                sb2[(s, j) + win(i)] = part[(s, 2 * px + py) + win(i)].astype(BF16)
                far2(i, j).start()

        def stage3(i):
            s = i % 2
            total = part[(s, me) + win(i)]
            for j in range(3):
                far2(i, j).wait()
                total = total + rb2[(s, j) + win(i)].astype(F32)
            fin[(s, 0) + win(i)] = total
            swap3(i).start()
            for cp in wmv_loads(i):
                cp.start()

        def stage4(i):
            s = i % 2
            if i >= 2:
                for cp in stores(i - 2):
                    cp.wait()
            swap3(i).wait()
            for cp in wmv_loads(i):
                cp.wait()
            both = (every,) + win(i)
            g = fin[(s,) + both]
            delta, m_new, v_new = _adamw(wmv[(s, 0) + both], g, wmv[(s, 1) + both], wmv[(s, 2) + both])
            outs[(s, 0) + both] = g
            outs[(s, 1) + both] = delta
            outs[(s, 2) + both] = m_new
            outs[(s, 3) + both] = v_new
            for cp in stores(i):
                cp.start()

        stages = (stage0, stage1, stage2, stage3, stage4)

        def tick(t):
            for age in reversed(range(len(stages))):
                if 0 <= t - age < n:
                    stages[age](t - age)

        def drain():
            for i in range(max(0, n - 2), n):
                for cp in stores(i):
                    cp.wait()

        return [functools.partial(tick, t) for t in range(n + len(stages) - 1)], drain

    return build(*refs)


def _allreduce_small(part):
    n_dev = 8

    def body(part_ref, out_ref, buf, send_sem, recv_sem):
        x, y, c, _ = _mesh_place()
        me = 4 * x + 2 * y + c
        buf[me] = part_ref[...]
        sends = []
        for k in range(1, n_dev):
            peer = ((1 - x) if k & 4 else x, (1 - y) if k & 2 else y, (1 - c) if k & 1 else c)
            cp = pltpu.make_async_remote_copy(src_ref=part_ref, dst_ref=buf.at[me], send_sem=send_sem.at[k - 1],
                                              recv_sem=recv_sem.at[k - 1], device_id=peer, device_id_type=MESH)
            cp.start()
            sends.append(cp)
        for cp in sends:
            cp.wait_recv()
        total = buf[0]
        for s in range(1, n_dev):
            total = total + buf[s]
        out_ref[...] = total
        for cp in sends:
            cp.wait_send()

    vmem = pl.BlockSpec(memory_space=pltpu.VMEM)
    return pl.pallas_call(
        body, name="allreduce_small", in_specs=[vmem], out_specs=vmem,
        out_shape=jax.ShapeDtypeStruct(part.shape, F32),
        scratch_shapes=[pltpu.VMEM((n_dev,) + part.shape, F32),
                        pltpu.SemaphoreType.DMA((n_dev - 1,)), pltpu.SemaphoreType.DMA((n_dev - 1,))],
    )(part)


def _adam_small(w, g, m, v):
    def body(w_ref, g_ref, m_ref, v_ref, d_ref, mo_ref, vo_ref):
        delta, m_new, v_new = _adamw(w_ref[...], g_ref[...], m_ref[...], v_ref[...])
        d_ref[...] = delta
        mo_ref[...] = m_new
        vo_ref[...] = v_new

    vmem = pl.BlockSpec(memory_space=pltpu.VMEM)
    return pl.pallas_call(
        body, name="adam_small", in_specs=[vmem] * 4, out_specs=[vmem] * 3,
        out_shape=[jax.ShapeDtypeStruct(w.shape, F32)] * 3,
    )(w, g, m, v)


BIG = ("a_w_in", "a_w_group", "a_w_out", "w_kv", "b_w_in", "b_w_out", "ple_w", "ple_gate_w")
SMALL = ("a_norm", "a_scale", "kv_norm", "b_norm", "k_norm", "b_q_norm")
SMALL_SHARDED = ("a_norm", "a_scale")
WEIGHTS = ("a_norm", "a_w_in", "a_w_group", "a_scale", "a_w_out", "kv_norm", "w_kv", "k_norm", "b_norm", "b_w_in",
           "b_q_norm", "b_w_out", "ple_w", "ple_gate_w")


def _as_matrix(a):
    return a.reshape(-1, a.shape[-1])


def _pack_small(arrs):
    rows = [jnp.pad(a.reshape(-1), (0, D_MODEL - a.size)) for a in arrs]
    rows += [jnp.zeros((D_MODEL,), F32)] * (8 - len(rows))
    return jnp.stack(rows)


def kernel(x, p, a_norm, a_w_in, a_w_group, a_scale, a_w_out, kv_norm, w_kv, k_norm, b_norm, b_w_in, b_q_norm, b_w_out, ple_w, ple_gate_w, loss_target, m_a_norm, m_a_w_in, m_a_w_group, m_a_scale, m_a_w_out, m_kv_norm, m_w_kv, m_k_norm, m_b_norm, m_b_w_in, m_b_q_norm, m_b_w_out, m_ple_w, m_ple_gate_w, v_a_norm, v_a_w_in, v_a_w_group, v_a_scale, v_a_w_out, v_kv_norm, v_w_kv, v_k_norm, v_b_norm, v_b_w_in, v_b_q_norm, v_b_w_out, v_ple_w, v_ple_gate_w):
    wts = dict(a_norm=a_norm, a_w_in=a_w_in, a_w_group=a_w_group, a_scale=a_scale, a_w_out=a_w_out, kv_norm=kv_norm,
               w_kv=w_kv, k_norm=k_norm, b_norm=b_norm, b_w_in=b_w_in, b_q_norm=b_q_norm, b_w_out=b_w_out,
               ple_w=ple_w, ple_gate_w=ple_gate_w)
    mom = dict(a_norm=m_a_norm, a_w_in=m_a_w_in, a_w_group=m_a_w_group, a_scale=m_a_scale, a_w_out=m_a_w_out,
               kv_norm=m_kv_norm, w_kv=m_w_kv, k_norm=m_k_norm, b_norm=m_b_norm, b_w_in=m_b_w_in,
               b_q_norm=m_b_q_norm, b_w_out=m_b_w_out, ple_w=m_ple_w, ple_gate_w=m_ple_gate_w)
    var = dict(a_norm=v_a_norm, a_w_in=v_a_w_in, a_w_group=v_a_w_group, a_scale=v_a_scale, a_w_out=v_a_w_out,
               kv_norm=v_kv_norm, w_kv=v_w_kv, k_norm=v_k_norm, b_norm=v_b_norm, b_w_in=v_b_w_in,
               b_q_norm=v_b_q_norm, b_w_out=v_b_w_out, ple_w=v_ple_w, ple_gate_w=v_ple_gate_w)
    S = x.shape[1]
    chip = 2 * lax.axis_index("x") + lax.axis_index("y")

    sharded_small = jnp.concatenate([a_norm.reshape(1, 256), a_scale.reshape(1, 256), jnp.zeros((6, 256), F32)], axis=0)
    later = ("a_w_group", "a_w_out", "w_kv", "b_w_in", "b_w_out", "ple_w", "ple_gate_w")
    (a_w_in_full,), small_full, copies = _allgather_weights(
        [_as_matrix(a_w_in)], sharded_small,
        [(_as_matrix(wts[n]), [(0, 256), (256, 512)] if n.startswith("ple") else [(0, _as_matrix(wts[n]).shape[0])])
         for n in later])
    local = dict(zip(("a_w_group", "a_w_out", "w_kv", "b_w_in", "b_w_out", "ple_w0", "ple_w1", "ple_gate_w0",
                      "ple_gate_w1"), copies))
    full = dict(a_w_in=a_w_in_full,
                a_norm=small_full[:, 0, :].reshape(1, D_MODEL), a_scale=small_full[:, 1, :].reshape(1, D_MODEL),
                kv_norm=kv_norm.reshape(1, D_MODEL), b_norm=b_norm.reshape(1, D_MODEL), k_norm=k_norm, b_q_norm=b_q_norm)

    def shards(t):
        out = {}
        for n in BIG:
            mat = _as_matrix(t[n])
            if n.startswith("ple"):
                out[n + "0"], out[n + "1"] = mat[:256], mat[256:]
            else:
                out[n] = mat
        return out

    state = (shards(wts), shards(mom), shards(var))
    grad_x, grads, updates, small_part = _local_step(x.reshape(S, D_MODEL), p, loss_target.reshape(S, D_MODEL),
                                                     full, local, state)

    names = sorted(grads)
    reduced = _reduce_adam_all([grads[n] for n in names], *[[t[n] for n in names] for t in state])
    for i, n in enumerate(names):
        updates[n] = tuple(group[i] for group in reduced)
    out_g, out_d, out_m, out_v = {}, {}, {}, {}
    for n in BIG:
        for i, out in enumerate((out_g, out_d, out_m, out_v)):
            if n.startswith("ple"):
                out[n] = jnp.stack([updates[n + "0"][i], updates[n + "1"][i]]).reshape(wts[n].shape)
            else:
                out[n] = updates[n][i].reshape(wts[n].shape)

    small_sum = _allreduce_small(small_part)
    loss = small_sum[len(SMALL), 0]
    small_rows = []
    for i, n in enumerate(SMALL):
        row = small_sum[i]
        if n in SMALL_SHARDED:
            row = lax.dynamic_slice(row, (chip * 256,), (256,))
        else:
            row = row[:wts[n].size]
        small_rows.append(row)
    g_small = _pack_small(small_rows)
    d_small, m_small, v_small = _adam_small(_pack_small([wts[n] for n in SMALL]), g_small,
                                            _pack_small([mom[n] for n in SMALL]), _pack_small([var[n] for n in SMALL]))
    for i, n in enumerate(SMALL):
        shape, size = wts[n].shape, wts[n].size
        out_g[n], out_d[n], out_m[n], out_v[n] = (t[i, :size].reshape(shape) for t in (g_small, d_small, m_small, v_small))

    return (loss, grad_x.reshape(1, S, D_MODEL), *[out_g[n] for n in WEIGHTS], *[out_d[n] for n in WEIGHTS],
            *[out_m[n] for n in WEIGHTS], *[out_v[n] for n in WEIGHTS])
```

```python
import functools

import jax
import jax.numpy as jnp
from jax import lax
from jax.experimental import pallas as pl
from jax.experimental.pallas import tpu as pltpu

F32 = jnp.float32
BF16 = jnp.bfloat16
MESH = pl.DeviceIdType.MESH

D_MODEL = 1024
N_HEADS = 16
HEAD_DIM = 64
PLE_DIM = 256
N_GROUPS = 4
GROUP_DIM = 256
POOL_WINDOWS = (2, 4, 8, 16)
N_CHIPS = 4
EPS = 1e-6
SB_SCALE = HEAD_DIM ** -0.5

ADAM_LR = 0.001
ADAM_B1 = 0.9
ADAM_B2 = 0.999
ADAM_EPS = 1e-08
ADAM_WD = 0.01
ADAM_STEP = 10

ROW_TILE = 256
WIDE_ROW_TILE = 512
EXP_UNDERFLOW = -104.0
ATT_Q_TILE = 512
ATT_K_TILE = 256
WGRAD_SEQ_TILE = 1024
WGRAD_ACC_BYTES = 4 * 1024 * 1024
MIB = 1024 * 1024


def _params(semantics=None, vmem_mib=48):
    return pltpu.CompilerParams(dimension_semantics=semantics, vmem_limit_bytes=vmem_mib * MIB)


def _dot(a, b):
    return jnp.dot(a, b, preferred_element_type=F32)


def _dot_nt(a, b):
    return lax.dot_general(a, b, (((1,), (1,)), ((), ())), preferred_element_type=F32)


def _dot_tn(a, b):
    return lax.dot_general(a, b, (((0,), (0,)), ((), ())), preferred_element_type=F32)


def _hilo(x):
    hi = x.astype(BF16)
    lo = (x - hi.astype(F32)).astype(BF16)
    return hi, lo


def _dot_hilo(x, w):
    hi, lo = _hilo(x)
    return _dot(hi, w) + _dot(lo, w)


def _sigmoid(z):
    return jax.nn.sigmoid(z)


def _dsilu(z, sg):
    return sg * (1.0 + z * (1.0 - sg))


def _mask_bf16(cond):
    return jnp.where(cond, 1.0, 0.0).astype(BF16)


def _head_mean_matrix():
    r = lax.broadcasted_iota(jnp.int32, (256, 256), 0) // HEAD_DIM
    c = lax.broadcasted_iota(jnp.int32, (256, 256), 1) // HEAD_DIM
    return _mask_bf16(r == c)


def _head_mean(x, bd):
    parts = []
    for s in range(x.shape[1] // 256):
        parts.append(_dot_hilo(x[:, s * 256:(s + 1) * 256], bd))
    out = parts[0] if len(parts) == 1 else jnp.concatenate(parts, axis=1)
    return out * (1.0 / HEAD_DIM)


def _a_in(x, gain, w_sh, gather=()):
    S = x.shape[0]
    tm = 512
    nsh, _, wn = w_sh.shape

    def body(x_ref, g_ref, w_ref, uz_ref, h_ref):
        @pl.when(pl.program_id(1) == 0)
        def _():
            xv = x_ref[...]
            r = lax.rsqrt(jnp.mean(xv * xv, axis=-1, keepdims=True) + EPS)
            h_ref[...] = (xv * r * g_ref[...]).astype(BF16)

        uz_ref[...] = _dot(h_ref[...], w_ref[0])

    return _call_with_gather(
        body, name="a_in", grid=(S // tm, nsh),
        in_specs=[pl.BlockSpec((tm, D_MODEL), lambda i, j: (i, 0)),
                  pl.BlockSpec((1, D_MODEL), lambda i, j: (0, 0)),
                  pl.BlockSpec((1, D_MODEL, wn), lambda i, j: (j, 0, 0))],
        out_specs=[pl.BlockSpec((tm, wn), lambda i, j: (i, j)),
                   pl.BlockSpec((tm, D_MODEL), lambda i, j: (i, 0))],
        out_shape=[jax.ShapeDtypeStruct((S, nsh * wn), F32),
                   jax.ShapeDtypeStruct((S, D_MODEL), BF16)],
        args=(x, gain, w_sh), gather=gather)


def _inv_count(first_row, rows, w):
    t1 = first_row + 1 + lax.broadcasted_iota(jnp.int32, (rows, 1), 0)
    return 1.0 / jnp.minimum(t1, w).astype(F32)


def _group_weight(wg_ref, g):
    return jnp.concatenate([wg_ref[sh, g] for sh in range(N_CHIPS)], axis=0)


def _a_mix(uz, wg, scale, gather=()):
    S = uz.shape[0]
    tm = ROW_TILE

    def body(u_ref, up_ref, z_ref, wg_ref, sc_ref, ga_ref, p_ref):
        i = pl.program_id(0)
        row = lax.broadcasted_iota(jnp.int32, (tm, tm), 0)
        col = lax.broadcasted_iota(jnp.int32, (tm, tm), 1)
        d = row - col
        for g, w in enumerate(POOL_WINDOWS):
            cols = slice(g * GROUP_DIM, (g + 1) * GROUP_DIM)
            t_main = _mask_bf16((d >= 0) & (d < w))
            t_halo = _mask_bf16(d + tm < w)
            u = u_ref[:, cols]
            up = jnp.where(i > 0, up_ref[:, cols], 0.0)
            hi, lo = _hilo(u)
            hip, lop = _hilo(up)
            wsum = _dot(t_main, hi) + _dot(t_main, lo) + _dot(t_halo, hip) + _dot(t_halo, lop)
            pooled = (wsum * _inv_count(i * tm, tm, w) - u).astype(BF16)
            p_ref[:, cols] = pooled
            mraw = _dot(pooled, _group_weight(wg_ref, g))
            z = z_ref[:, cols]
            ga_ref[:, cols] = (mraw * sc_ref[:, cols] * (z * _sigmoid(z))).astype(BF16)

    return _call_with_gather(
        body, name="a_mix", grid=(S // tm,),
        in_specs=[pl.BlockSpec((tm, D_MODEL), lambda i: (i, 0)),
                  pl.BlockSpec((tm, D_MODEL), lambda i: (jnp.maximum(i - 1, 0), 0)),
                  pl.BlockSpec((tm, D_MODEL), lambda i: (i, 1)),
                  pl.BlockSpec((N_CHIPS, N_GROUPS, 64, GROUP_DIM), lambda i: (0, 0, 0, 0)),
                  pl.BlockSpec((1, D_MODEL), lambda i: (0, 0))],
        out_specs=[pl.BlockSpec((tm, D_MODEL), lambda i: (i, 0)),
                   pl.BlockSpec((tm, D_MODEL), lambda i: (i, 0))],
        out_shape=[jax.ShapeDtypeStruct((S, D_MODEL), BF16),
                   jax.ShapeDtypeStruct((S, D_MODEL), BF16)],
        args=(uz, uz, uz, wg, scale), gather=gather)


def _out_ple(name, gated, x_in, w_out, p, layer, ple_w, ple_g, target=None, gather=()):
    S = x_in.shape[0]
    tm = WIDE_ROW_TILE
    with_loss = target is not None

    def body(*refs):
        if with_loss:
            g_ref, x_ref, wo_ref, p_ref, pw_ref, pg_ref, t_ref, xm_ref, dx_ref, e_ref, gt_ref, loss_ref = refs
        else:
            g_ref, x_ref, wo_ref, p_ref, pw_ref, pg_ref, xm_ref, xo_ref, e_ref, gt_ref = refs
        xm = x_ref[...] + _dot(g_ref[...], wo_ref[...])
        xm_ref[...] = xm
        pb = p_ref[...].astype(BF16)
        e = jnp.concatenate([_dot(pb, pw_ref[sh]) for sh in range(N_CHIPS)], axis=1)
        pg = jnp.concatenate([pg_ref[sh] for sh in range(N_CHIPS)], axis=0)
        gate = _sigmoid(_dot(xm.astype(BF16), pg))
        e_ref[...] = e.astype(BF16)
        gt_ref[...] = gate.astype(BF16)
        xo = xm + e * gate
        if with_loss:
            diff = xo - t_ref[...]
            dx_ref[...] = diff * (1.0 / D_MODEL)

            @pl.when(pl.program_id(0) == 0)
            def _():
                loss_ref[...] = jnp.zeros_like(loss_ref)

            loss_ref[...] += jnp.sum(diff * diff) * (0.5 / D_MODEL)
        else:
            xo_ref[...] = xo

    row = pl.BlockSpec((tm, D_MODEL), lambda i: (i, 0))
    in_specs = [row, row,
                pl.BlockSpec((D_MODEL, D_MODEL), lambda i: (0, 0)),
                pl.BlockSpec((None, None, tm, PLE_DIM), lambda i: (layer, 0, i, 0)),
                pl.BlockSpec((N_CHIPS, PLE_DIM, 256), lambda i: (0, 0, 0)),
                pl.BlockSpec((N_CHIPS, 256, D_MODEL), lambda i: (0, 0, 0))]
    args = [gated, x_in, w_out, p, ple_w, ple_g]
    out_specs = [row, row, row, row]
    out_shape = [jax.ShapeDtypeStruct((S, D_MODEL), F32), jax.ShapeDtypeStruct((S, D_MODEL), F32),
                 jax.ShapeDtypeStruct((S, D_MODEL), BF16), jax.ShapeDtypeStruct((S, D_MODEL), BF16)]
    if with_loss:
        in_specs.append(row)
        args.append(target)
        out_specs.append(pl.BlockSpec((8, 128), lambda i: (0, 0)))
        out_shape.append(jax.ShapeDtypeStruct((8, 128), F32))
    return _call_with_gather(body, name=name, grid=(S // tm,), in_specs=in_specs, out_specs=out_specs,
                             out_shape=out_shape, args=args, gather=gather)


def _b_in(x, kv_gain, b_gain, k_gain_t, q_gain_t, w_kv, w_in, gather=()):
    S = x.shape[0]
    tm = ROW_TILE

    def body(x_ref, kvg_ref, bg_ref, kg_ref, qg_ref, wkv_ref, win_ref,
             hkv_ref, hb_ref, kraw_ref, qraw_ref, k_ref, q_ref, v_ref, z_ref):
        xv = x_ref[...]
        y = xv * lax.rsqrt(jnp.mean(xv * xv, axis=-1, keepdims=True) + EPS)
        hkv = (y * kvg_ref[...]).astype(BF16)
        hb = (y * bg_ref[...]).astype(BF16)
        hkv_ref[...] = hkv
        hb_ref[...] = hb
        bd = _head_mean_matrix()

        def head_norm(raw, gain):
            rr = lax.rsqrt(_head_mean(raw * raw, bd) + EPS)
            return raw * rr * gain

        for sh in range(N_CHIPS):
            kvc = _dot(hkv, wkv_ref[sh])
            qzc = _dot(hb, win_ref[sh])
            cols = slice((sh % 2) * 512, (sh % 2) * 512 + 512)
            if sh < 2:
                kraw_ref[:, cols] = kvc.astype(BF16)
                qraw_ref[:, cols] = qzc.astype(BF16)
                k_ref[:, cols] = head_norm(kvc, kg_ref[:, cols]).astype(BF16)
                q_ref[:, cols] = (head_norm(qzc, qg_ref[:, cols]) * SB_SCALE).astype(BF16)
            else:
                v_ref[:, cols] = kvc.astype(BF16)
                z_ref[:, cols] = qzc.astype(BF16)

    row = pl.BlockSpec((tm, D_MODEL), lambda i: (i, 0))
    vec = pl.BlockSpec((1, D_MODEL), lambda i: (0, 0))
    wsp = pl.BlockSpec((N_CHIPS, D_MODEL, 512), lambda i: (0, 0, 0))
    return _call_with_gather(
        body, name="b_in", grid=(S // tm,),
        in_specs=[row, vec, vec, vec, vec, wsp, wsp],
        out_specs=[row] * 8,
        out_shape=[jax.ShapeDtypeStruct((S, D_MODEL), BF16)] * 8,
        args=(x, kv_gain, b_gain, k_gain_t, q_gain_t, w_kv, w_in), gather=gather, vmem_mib=56)


def _softplus_parts(z):
    e = jnp.exp(-jnp.abs(z))
    return -(jnp.maximum(z, 0.0) + jnp.log(1.0 + e)), e


def _add_rows(total, rows, update):
    lo, hi = rows
    parts = ([total[:lo]] if lo else []) + [total[lo:hi] + update] + ([total[hi:]] if hi < total.shape[0] else [])
    return parts[0] if len(parts) == 1 else jnp.concatenate(parts, axis=0)


def _attn_fwd(q, k, v, zgate, gather=()):
    S = q.shape[0]
    tq, tk = ATT_Q_TILE, ATT_K_TILE
    kpq = tq // tk
    assert kpq == 2

    def body(q_ref, k_ref, v_ref, z_ref, o_ref, g_ref, lt_ref, steps_ref):
        qi = pl.program_id(1)
        lane = lax.broadcasted_iota(jnp.int32, (1, 128), 1)
        ri = lax.broadcasted_iota(jnp.int32, (tk, tk), 0)
        ci = lax.broadcasted_iota(jnp.int32, (tk, tk), 1)
        later_mat = _mask_bf16(ri > ci)
        causal = ci < ri
        qv = q_ref[...]
        first = lane < HEAD_DIM
        q_heads = (jnp.where(first, qv, jnp.zeros_like(qv)), jnp.where(first, jnp.zeros_like(qv), qv))

        def step(blocks, carry):
            chains = [(b, h) for b in range(len(blocks)) for h in range(2)]
            rows = [r for _, r, _ in blocks]
            s0 = [pl.multiple_of(kj * tk, tk) for kj, _, _ in blocks]
            kb = [k_ref[pl.ds(s, tk), :] for s in s0]
            vb = [v_ref[pl.ds(s, tk), :] for s in s0]
            visible = [causal if masked else None for _, _, masked in blocks]
            z = {c: _dot_nt(q_heads[c[1]][rows[c[0]][0]:rows[c[0]][1]], kb[c[0]]) for c in chains}
            run = [carry[0], carry[2]]
            log_own, later, run_at = {}, {}, {}
            for c in chains:
                b, h = c
                lk = _softplus_parts(z[c])[0]
                if visible[b] is not None:
                    lk = jnp.where(visible[b], lk, 0.0)
                log_own[c] = z[c] + lk
                later[c] = _dot(lk.astype(BF16), later_mat)
                run_at[c] = run[h][rows[b][0]:rows[b][1]]
                run[h] = _add_rows(run[h], rows[b], jnp.sum(lk, axis=-1, keepdims=True))
            acc = [carry[1], carry[3]]
            for c in chains:
                b, h = c
                a = jnp.exp(log_own[c] + later[c] + run_at[c])
                if visible[b] is not None:
                    a = jnp.where(visible[b], a, 0.0)
                acc[h] = _add_rows(acc[h], rows[b], _dot(a.astype(BF16), vb[b]))
            return run[0], acc[0], run[1], acc[1]

        zero1, zero128 = jnp.zeros((tq, 1), F32), jnp.zeros((tq, 128), F32)
        carry = step([(qi * kpq + 1, (tk, tq), True), (qi * kpq, (tk, tq), False), (qi * kpq, (0, tk), True)],
                     (zero1, zero128, zero1, zero128))

        def low(run):
            return jnp.max(run)

        def pair_more(c):
            return (c[0] < qi) & (jnp.maximum(low(c[1][tk:]), low(c[3][tk:])) > EXP_UNDERFLOW)

        def pair_step(c):
            last = (qi - c[0]) * kpq - 1
            return (c[0] + 1, *step([(last, (0, tq), False), (last - 1, (0, tq), False)], c[1:]))

        pairs, *carry = lax.while_loop(pair_more, pair_step, (jnp.int32(0), *carry))
        left = (qi - pairs) * kpq

        def single_more(c):
            return (c[0] < left) & (jnp.maximum(low(c[1][:tk]), low(c[3][:tk])) > EXP_UNDERFLOW)

        def single_step(c):
            return (c[0] + 1, *step([(left - 1 - c[0], (0, tk), False)], c[1:]))

        singles, *carry = lax.while_loop(single_more, single_step, (jnp.int32(0), *carry))
        steps_ref[...] = jnp.concatenate([jnp.full((4, 128), pairs, F32), jnp.full((4, 128), singles, F32)], axis=0)
        o_tot = jnp.where(first, carry[1], carry[3])
        l_tot = jnp.where(first, carry[0], carry[2])
        o_ref[...] = o_tot.astype(BF16)
        lt_ref[...] = l_tot
        zz = z_ref[...].astype(F32)
        g_ref[...] = (o_tot * (zz * _sigmoid(zz))).astype(BF16)

    blk = pl.BlockSpec((tq, 128), lambda hp, qi: (qi, hp))
    seq = pl.BlockSpec((S, 128), lambda hp, qi: (0, hp))
    return _call_with_gather(
        body, name="attn_fwd", grid=(D_MODEL // 128, S // tq),
        in_specs=[blk, seq, seq, blk],
        out_specs=[blk, blk, blk, pl.BlockSpec((None, None, 8, 128), lambda hp, qi: (hp, qi, 0, 0))],
        out_shape=[jax.ShapeDtypeStruct((S, D_MODEL), BF16)] * 2 + [jax.ShapeDtypeStruct((S, D_MODEL), F32)]
        + [jax.ShapeDtypeStruct((D_MODEL // 128, S // tq, 8, 128), F32)],
        args=(q, k, v, zgate), gather=gather)


def _ple_out_bwd(name, dx_out, e, gate, ple_g, w_out):
    S = dx_out.shape[0]
    tm = WIDE_ROW_TILE

    def body(dx_ref, e_ref, gt_ref, pg_ref, wo_ref, de_ref, dgp_ref, dxm_ref, dg_ref):
        dxo = dx_ref[...]
        ev = e_ref[...].astype(F32)
        gv = gt_ref[...].astype(F32)
        de_ref[...] = (dxo * gv).astype(BF16)
        dgp = (dxo * ev * gv * (1.0 - gv)).astype(BF16)
        dgp_ref[...] = dgp
        pg = jnp.concatenate([pg_ref[sh] for sh in range(N_CHIPS)], axis=0)
        dxm = dxo + _dot_nt(dgp, pg)
        dxm_ref[...] = dxm
        dg_ref[...] = _dot_nt(dxm.astype(BF16), wo_ref[...]).astype(BF16)

    row = pl.BlockSpec((tm, D_MODEL), lambda i: (i, 0))
    return pl.pallas_call(
        body, name=name, grid=(S // tm,),
        in_specs=[row, row, row,
                  pl.BlockSpec((N_CHIPS, 256, D_MODEL), lambda i: (0, 0, 0)),
                  pl.BlockSpec((D_MODEL, D_MODEL), lambda i: (0, 0))],
        out_specs=[row, row, row, row],
        out_shape=[jax.ShapeDtypeStruct((S, D_MODEL), BF16), jax.ShapeDtypeStruct((S, D_MODEL), BF16),
                   jax.ShapeDtypeStruct((S, D_MODEL), F32), jax.ShapeDtypeStruct((S, D_MODEL), BF16)],
        compiler_params=_params(("arbitrary",)),
    )(dx_out, e, gate, ple_g, w_out)


def _attn_bwd(q, k, v, ltot, steps, dgated, o, zgate, reduce=None):
    S = q.shape[0]
    tq, tk = ATT_Q_TILE, ATT_K_TILE
    kpq = tq // tk
    nq = S // tq

    def body(q_ref, k_ref, v_ref, lt_ref, steps_ref, dg_ref, o_ref, z_ref, dq_ref, dk_ref, dv_ref, dz_ref,
             dk_acc, dv_acc):
        qi = pl.program_id(1)

        @pl.when(qi == 0)
        def _():
            dk_acc[...] = jnp.zeros_like(dk_acc)
            dv_acc[...] = jnp.zeros_like(dv_acc)

        lane = lax.broadcasted_iota(jnp.int32, (1, 128), 1)
        ri = lax.broadcasted_iota(jnp.int32, (tk, tk), 0)
        ci = lax.broadcasted_iota(jnp.int32, (tk, tk), 1)
        later_mat = _mask_bf16(ri > ci)
        before_mat = _mask_bf16(ri < ci)
        causal = ci < ri
        zz = z_ref[...].astype(F32)
        sg = _sigmoid(zz)
        dgv = dg_ref[...].astype(F32)
        dz_ref[...] = (dgv * o_ref[...].astype(F32) * _dsilu(zz, sg)).astype(BF16)
        dob = (dgv * (zz * sg)).astype(BF16)
        ltv = lt_ref[...]
        qv = q_ref[...]
        first = lane < HEAD_DIM
        masks = (first, jnp.logical_not(first))
        q_heads = [jnp.where(hm, qv, jnp.zeros_like(qv)) for hm in masks]
        do_heads = [jnp.where(hm, dob, jnp.zeros_like(dob)) for hm in masks]
        totals = [jnp.max(jnp.where(hm, ltv, -jnp.inf), axis=-1, keepdims=True) for hm in masks]

        def step(blocks, carry):
            chains = [(b, h) for b in range(len(blocks)) for h in range(2)]
            rows = [r for _, r, _ in blocks]
            cut = lambda t, b: t[rows[b][0]:rows[b][1]]
            s0 = [pl.multiple_of(kj * tk, tk) for kj, _, _ in blocks]
            kb = [k_ref[pl.ds(s, tk), :] for s in s0]
            vb = [v_ref[pl.ds(s, tk), :] for s in s0]
            visible = [causal if masked else None for _, _, masked in blocks]
            z = {c: _dot_nt(cut(q_heads[c[1]], c[0]), kb[c[0]]) for c in chains}
            da = {c: _dot_nt(cut(do_heads[c[1]], c[0]), vb[c[0]]) for c in chains}
            run = [carry[0], carry[3]]
            log_own, beta, later, base = {}, {}, {}, {}
            for c in chains:
                b, h = c
                lk = _softplus_parts(z[c])[0]
                if visible[b] is not None:
                    lk = jnp.where(visible[b], lk, 0.0)
                log_own[c] = z[c] + lk
                beta[c] = jnp.exp(log_own[c]).astype(BF16)
                later[c] = _dot(lk.astype(BF16), later_mat)
                run[h] = _add_rows(run[h], rows[b], jnp.sum(lk, axis=-1, keepdims=True))
                base[c] = cut(totals[h] - run[h], b)
            grun = [carry[1], carry[4]]
            a_bf, g_bf, gbefore, grun_at = {}, {}, {}, {}
            for c in chains:
                b, h = c
                a = jnp.exp(log_own[c] + later[c] + base[c])
                if visible[b] is not None:
                    a = jnp.where(visible[b], a, 0.0)
                a_bf[c] = a.astype(BF16)
                g = da[c] * a
                g_bf[c] = g.astype(BF16)
                gbefore[c] = _dot(g_bf[c], before_mat)
                grun_at[c] = cut(grun[h], b)
                grun[h] = _add_rows(grun[h], rows[b], jnp.sum(g, axis=-1, keepdims=True))
            dq = [carry[2], carry[5]]
            dk_blk = [jnp.zeros((tk, 128), F32) for _ in blocks]
            dv_blk = [jnp.zeros((tk, 128), F32) for _ in blocks]
            for c in chains:
                b, h = c
                g = g_bf[c].astype(F32)
                dz = g - beta[c].astype(F32) * (g + gbefore[c] + grun_at[c])
                if visible[b] is not None:
                    dz = jnp.where(visible[b], dz, 0.0)
                dzb = dz.astype(BF16)
                dq[h] = _add_rows(dq[h], rows[b], _dot(dzb, kb[b]))
                dk_blk[b] = dk_blk[b] + _dot_tn(dzb, cut(q_heads[h], b))
                dv_blk[b] = dv_blk[b] + _dot_tn(a_bf[c], cut(do_heads[h], b))
            for b in range(len(blocks)):
                dk_acc[pl.ds(s0[b], tk), :] += dk_blk[b]
                dv_acc[pl.ds(s0[b], tk), :] += dv_blk[b]
            return run[0], grun[0], dq[0], run[1], grun[1], dq[1]

        pairs = jnp.clip(jnp.max(steps_ref[0:4, :]).astype(jnp.int32), 0, qi)
        left = (qi - pairs) * kpq
        singles = jnp.clip(jnp.max(steps_ref[4:8, :]).astype(jnp.int32), 0, left)
        zero1, zero128 = jnp.zeros((tq, 1), F32), jnp.zeros((tq, 128), F32)
        carry = lax.fori_loop(left - singles, left, lambda kj, c: step([(kj, (0, tk), False)], c),
                              (zero1, zero1, zero128, zero1, zero1, zero128))
        carry = lax.fori_loop(qi - pairs, qi,
                              lambda n, c: step([(n * kpq, (0, tq), False), (n * kpq + 1, (0, tq), False)], c), carry)
        carry = step([(qi * kpq, (0, tk), True), (qi * kpq, (tk, tq), False), (qi * kpq + 1, (tk, tq), True)], carry)
        dq_ref[...] = jnp.where(first, carry[2], carry[5]).astype(BF16)

        @pl.when(qi == nq - 1)
        def _():
            dk_ref[...] = dk_acc[...].astype(BF16)
            dv_ref[...] = dv_acc[...].astype(BF16)

    blk = pl.BlockSpec((tq, 128), lambda hp, qi: (qi, hp))
    seq = pl.BlockSpec((S, 128), lambda hp, qi: (0, hp))
    return _call_with_gather(
        body, name="attn_bwd", grid=(D_MODEL // 128, nq),
        in_specs=[blk, seq, seq, blk, pl.BlockSpec((None, None, 8, 128), lambda hp, qi: (hp, qi, 0, 0)),
                  blk, blk, blk],
        out_specs=[blk, seq, seq, blk],
        out_shape=[jax.ShapeDtypeStruct((S, D_MODEL), BF16)] * 4,
        scratch_shapes=[pltpu.VMEM((S, 128), F32), pltpu.VMEM((S, 128), F32)],
        args=(q, k, v, ltot, steps, dgated, o, zgate), reduce=reduce, vmem_mib=56)


def _rms_bwd(xv, dh_gain_sum):
    r = lax.rsqrt(jnp.mean(xv * xv, axis=-1, keepdims=True) + EPS)
    xhat = xv * r
    dx = r * (dh_gain_sum - xhat * jnp.mean(dh_gain_sum * xhat, axis=-1, keepdims=True))
    return dx, xhat


def _b_in_bwd(dq, dk, dv, dz, q_raw, k_raw, x, dx_mid, q_gain_t, k_gain_t, b_gain, kv_gain, w_in, w_kv):
    S = x.shape[0]
    tm = ROW_TILE

    def body(dq_ref, dk_ref, dv_ref, dz_ref, qr_ref, kr_ref, x_ref, dxm_ref, qg_ref, kg_ref, bg_ref, kvg_ref,
             win_ref, wkv_ref, dqz_ref, dkv_ref, dx_ref, small_ref):
        @pl.when(pl.program_id(0) == 0)
        def _():
            small_ref[...] = jnp.zeros_like(small_ref)

        bd = _head_mean_matrix()

        def head_norm_bwd(dy_ref, raw_ref, gain, scale):
            raw = raw_ref[...].astype(F32)
            rr = lax.rsqrt(_head_mean(raw * raw, bd) + EPS)
            xhat = raw * rr
            dy = dy_ref[...].astype(F32) * scale
            gdy = dy * gain
            draw = rr * (gdy - xhat * _head_mean(gdy * xhat, bd))
            return draw.astype(BF16), jnp.sum(dy * xhat, axis=0, keepdims=True)

        dqr, dqg = head_norm_bwd(dq_ref, qr_ref, qg_ref[...], SB_SCALE)
        dkr, dkg = head_norm_bwd(dk_ref, kr_ref, kg_ref[...], 1.0)
        dqz_ref[:, :D_MODEL] = dqr
        dqz_ref[:, D_MODEL:] = dz_ref[...]
        dkv_ref[:, :D_MODEL] = dkr
        dkv_ref[:, D_MODEL:] = dv_ref[...]
        dhb = jnp.zeros((tm, D_MODEL), F32)
        dhkv = jnp.zeros((tm, D_MODEL), F32)
        for sh in range(N_CHIPS):
            cols = slice(sh * 512, (sh + 1) * 512)
            dhb = dhb + _dot_nt(dqz_ref[:, cols], win_ref[sh])
            dhkv = dhkv + _dot_nt(dkv_ref[:, cols], wkv_ref[sh])
        dx, xhat = _rms_bwd(x_ref[...], dhb * bg_ref[...] + dhkv * kvg_ref[...])
        dx_ref[...] = dxm_ref[...] + dx
        small_ref[0:1, :] += dqg
        small_ref[1:2, :] += dkg
        small_ref[2:3, :] += jnp.sum(dhb * xhat, axis=0, keepdims=True)
        small_ref[3:4, :] += jnp.sum(dhkv * xhat, axis=0, keepdims=True)

    row = pl.BlockSpec((tm, D_MODEL), lambda i: (i, 0))
    wide = pl.BlockSpec((tm, 2 * D_MODEL), lambda i: (i, 0))
    vec = pl.BlockSpec((1, D_MODEL), lambda i: (0, 0))
    wsp = pl.BlockSpec((N_CHIPS, D_MODEL, 512), lambda i: (0, 0, 0))
    return pl.pallas_call(
        body, name="b_in_bwd", grid=(S // tm,),
        in_specs=[row] * 8 + [vec] * 4 + [wsp, wsp],
        out_specs=[wide, wide, row, pl.BlockSpec((8, D_MODEL), lambda i: (0, 0))],
        out_shape=[jax.ShapeDtypeStruct((S, 2 * D_MODEL), BF16), jax.ShapeDtypeStruct((S, 2 * D_MODEL), BF16),
                   jax.ShapeDtypeStruct((S, D_MODEL), F32), jax.ShapeDtypeStruct((8, D_MODEL), F32)],
        compiler_params=_params(("arbitrary",), 56),
    )(dq, dk, dv, dz, q_raw, k_raw, x, dx_mid, q_gain_t, k_gain_t, b_gain, kv_gain, w_in, w_kv)


def _a_mix_bwd(dgated, uz, pooled, wg, scale, w_in, x, dx_mid, gain, reduce=None):
    S = x.shape[0]
    tm = ROW_TILE
    n = S // tm

    def body(dg_ref, z_ref, p_ref, wg_ref, sc_ref, win_ref, x_ref, dxm_ref, gn_ref,
             duz_ref, dmr_ref, dx_ref, small_ref, halo_hi, halo_lo):
        i = pl.program_id(0)

        @pl.when(i == 0)
        def _():
            small_ref[...] = jnp.zeros_like(small_ref)
            halo_hi[...] = jnp.zeros_like(halo_hi)
            halo_lo[...] = jnp.zeros_like(halo_lo)

        first_row = (n - 1 - i) * tm
        row = lax.broadcasted_iota(jnp.int32, (tm, tm), 0)
        col = lax.broadcasted_iota(jnp.int32, (tm, tm), 1)
        d = col - row
        for g, w in enumerate(POOL_WINDOWS):
            cols = slice(g * GROUP_DIM, (g + 1) * GROUP_DIM)
            wgg = _group_weight(wg_ref, g)
            sc = sc_ref[:, cols]
            mraw = _dot(p_ref[:, cols], wgg)
            z = z_ref[:, cols]
            sg = _sigmoid(z)
            dga = dg_ref[:, cols].astype(F32)
            dm = dga * (z * sg)
            duz_ref[:, D_MODEL + g * GROUP_DIM:D_MODEL + (g + 1) * GROUP_DIM] = (
                dga * (mraw * sc) * _dsilu(z, sg)).astype(BF16)
            small_ref[0:1, cols] += jnp.sum(dm * mraw, axis=0, keepdims=True)
            dmr = (dm * sc).astype(BF16)
            dmr_ref[:, cols] = dmr
            dp = _dot_nt(dmr, wgg)
            hi, lo = _hilo(dp * _inv_count(first_row, tm, w))
            t_main = _mask_bf16((d >= 0) & (d < w))
            t_halo = _mask_bf16(d + tm < w)
            du = (_dot(t_main, hi) + _dot(t_main, lo) + _dot(t_halo, halo_hi[:, cols]) + _dot(t_halo, halo_lo[:, cols])
                  - dp)
            halo_hi[:, cols] = hi
            halo_lo[:, cols] = lo
            duz_ref[:, cols] = du.astype(BF16)
        dh = jnp.zeros((tm, D_MODEL), F32)
        for sh in range(N_CHIPS):
            dh = dh + _dot_nt(duz_ref[:, sh * 512:(sh + 1) * 512], win_ref[sh])
        dx, xhat = _rms_bwd(x_ref[...], dh * gn_ref[...])
        dx_ref[...] = dxm_ref[...] + dx
        small_ref[1:2, :] += jnp.sum(dh * xhat, axis=0, keepdims=True)

    rev = lambda i: (n - 1 - i, 0)
    row = pl.BlockSpec((tm, D_MODEL), rev)
    vec = pl.BlockSpec((1, D_MODEL), lambda i: (0, 0))
    return _call_with_gather(
        body, name="a_mix_bwd", grid=(n,),
        in_specs=[row,
                  pl.BlockSpec((tm, D_MODEL), lambda i: (n - 1 - i, 1)),
                  row,
                  pl.BlockSpec((N_CHIPS, N_GROUPS, 64, GROUP_DIM), lambda i: (0, 0, 0, 0)),
                  vec,
                  pl.BlockSpec((N_CHIPS, D_MODEL, 512), lambda i: (0, 0, 0)),
                  row, row, vec],
        out_specs=[pl.BlockSpec((tm, 2 * D_MODEL), rev), row, row,
                   pl.BlockSpec((8, D_MODEL), lambda i: (0, 0))],
        out_shape=[jax.ShapeDtypeStruct((S, 2 * D_MODEL), BF16), jax.ShapeDtypeStruct((S, D_MODEL), BF16),
                   jax.ShapeDtypeStruct((S, D_MODEL), F32), jax.ShapeDtypeStruct((8, D_MODEL), F32)],
        scratch_shapes=[pltpu.VMEM((tm, D_MODEL), BF16), pltpu.VMEM((tm, D_MODEL), BF16)],
        args=(dgated, uz, pooled, wg, scale, w_in, x, dx_mid, gain), reduce=reduce, vmem_mib=56)


def _wgrad(name, a, dy, n_shards, a_spec=None, k_dim=None):
    S, n_cols = dy.shape
    ts = WGRAD_SEQ_TILE
    k_dim = a.shape[-1] if k_dim is None else k_dim
    wn = n_cols // n_shards
    tk = min(k_dim, WGRAD_ACC_BYTES // (4 * n_cols))
    nst = S // ts

    def body(a_ref, dy_ref, out_ref, acc):
        st = pl.program_id(1)

        @pl.when(st == 0)
        def _():
            acc[...] = jnp.zeros_like(acc)

        acc[...] += _dot_tn(a_ref[...].astype(BF16), dy_ref[...].astype(BF16))

        @pl.when(st == nst - 1)
        def _():
            for sh in range(n_shards):
                out_ref[sh] = acc[:, sh * wn:(sh + 1) * wn]

    if a_spec is None:
        a_spec = pl.BlockSpec((ts, tk), lambda kt, st: (st, kt))
    return pl.pallas_call(
        body, name=name, grid=(k_dim // tk, nst),
        in_specs=[a_spec, pl.BlockSpec((ts, n_cols), lambda kt, st: (st, 0))],
        out_specs=pl.BlockSpec((n_shards, tk, wn), lambda kt, st: (0, kt, 0)),
        out_shape=jax.ShapeDtypeStruct((n_shards, k_dim, wn), F32),
        scratch_shapes=[pltpu.VMEM((tk, n_cols), F32)],
        compiler_params=_params(("parallel", "arbitrary")),
    )(a, dy)


def _wgrad_ple(name, p, layer, de):
    ts = WGRAD_SEQ_TILE
    spec = pl.BlockSpec((None, None, ts, PLE_DIM), lambda kt, st: (layer, 0, st, 0))
    return _wgrad(name, p, de, N_CHIPS, a_spec=spec, k_dim=PLE_DIM)


def _wgrad_group(pooled, dmr):
    S = pooled.shape[0]
    ts = WGRAD_SEQ_TILE
    nst = S // ts

    def body(p_ref, d_ref, out_ref, acc):
        st = pl.program_id(1)

        @pl.when(st == 0)
        def _():
            acc[...] = jnp.zeros_like(acc)

        acc[...] += _dot_tn(p_ref[...], d_ref[...])

        @pl.when(st == nst - 1)
        def _():
            for sh in range(N_CHIPS):
                out_ref[sh] = acc[sh * 64:(sh + 1) * 64, :]

    blk = pl.BlockSpec((ts, GROUP_DIM), lambda g, st: (st, g))
    return pl.pallas_call(
        body, name="wgrad_group", grid=(N_GROUPS, nst),
        in_specs=[blk, blk],
        out_specs=pl.BlockSpec((N_CHIPS, None, 64, GROUP_DIM), lambda g, st: (0, g, 0, 0)),
        out_shape=jax.ShapeDtypeStruct((N_CHIPS, N_GROUPS, 64, GROUP_DIM), F32),
        scratch_shapes=[pltpu.VMEM((GROUP_DIM, GROUP_DIM), F32)],
        compiler_params=_params(("parallel", "arbitrary")),
    )(pooled, dmr)


GATHER_AT = {
    "a_in": ("a_w_group", "a_w_out", "ple_w0", "ple_gate_w0"),
    "a_mix": ("w_kv",),
    "a_out_ple": ("b_w_in",),
    "attn_fwd": ("b_w_out", "ple_w1", "ple_gate_w1"),
}


REDUCE_AT = {
    "attn_bwd": ("b_w_out", "ple_w1", "ple_gate_w1"),
    "a_mix_bwd": ("w_kv", "b_w_in"),
}


def _local_step(x, p, target, w, local=None, state=None):
    w = dict(w)

    def run(fn, host, n_out, *args, **kwargs):
        names = GATHER_AT[host] if local is not None else ()
        res = fn(*args, gather=[local[n] for n in names], **kwargs)
        w.update(zip(names, res[n_out:]))
        return res[:n_out]

    k_gain_t = jnp.tile(w["k_norm"].reshape(1, HEAD_DIM), (1, N_HEADS))
    q_gain_t = jnp.tile(w["b_q_norm"].reshape(1, HEAD_DIM), (1, N_HEADS))

    uz, h_a = run(_a_in, "a_in", 2, x, w["a_norm"], w["a_w_in"])
    wg4 = w["a_w_group"].reshape(N_CHIPS, N_GROUPS, 64, GROUP_DIM)
    wa_out = w["a_w_out"].reshape(D_MODEL, D_MODEL)
    gated_a, pooled = run(_a_mix, "a_mix", 2, uz, wg4, w["a_scale"])
    x1, x2, e_a, gate_a = run(_out_ple, "a_out_ple", 4, "a_out_ple", gated_a, x, wa_out, p, 0,
                              w["ple_w0"], w["ple_gate_w0"])
    h_kv, h_b, k_raw, q_raw, k, q, v, z_b = _b_in(
        x2, w["kv_norm"], w["b_norm"], k_gain_t, q_gain_t, w["w_kv"], w["b_w_in"])
    o, gated_b, ltot, att_steps = run(_attn_fwd, "attn_fwd", 4, q, k, v, z_b)
    wb_out = w["b_w_out"].reshape(D_MODEL, D_MODEL)
    x3, dx4, e_b, gate_b, loss_blk = _out_ple("b_out_ple", gated_b, x2, wb_out, p, 1, w["ple_w1"], w["ple_gate_w1"],
                                              target=target)

    grads, updates = {}, {}

    def hosted(fn, host, n_out, *args):
        if state is None:
            return fn(*args)
        names = REDUCE_AT[host]
        res = fn(*args, reduce=([grads.pop(n) for n in names], *[[t[n] for n in names] for t in state]))
        for i, n in enumerate(names):
            updates[n] = tuple(group[i] for group in res[n_out:])
        return res[:n_out]

    de_b, dgp_b, dx3, dgated_b = _ple_out_bwd("b_ple_out_bwd", dx4, e_b, gate_b, w["ple_gate_w1"], wb_out)
    grads["b_w_out"] = _wgrad("wgrad_b_out", gated_b, dx3, 1).reshape(N_CHIPS, 256, D_MODEL)
    grads["ple_w1"] = _wgrad_ple("wgrad_ple1", p, 1, de_b)
    grads["ple_gate_w1"] = _wgrad("wgrad_gate1", x3, dgp_b, 1).reshape(N_CHIPS, 256, D_MODEL)
    dq, dk, dv, dz_b = hosted(_attn_bwd, "attn_bwd", 4, q, k, v, ltot, att_steps, dgated_b, o, z_b)
    dqz, dkv, dx2, small_b = _b_in_bwd(dq, dk, dv, dz_b, q_raw, k_raw, x2, dx3, q_gain_t, k_gain_t,
                                       w["b_norm"], w["kv_norm"], w["b_w_in"], w["w_kv"])
    grads["w_kv"] = _wgrad("wgrad_kv", h_kv, dkv, N_CHIPS)
    grads["b_w_in"] = _wgrad("wgrad_b_in", h_b, dqz, N_CHIPS)
    de_a, dgp_a, dx1, dgated_a = _ple_out_bwd("a_ple_out_bwd", dx2, e_a, gate_a, w["ple_gate_w0"], wa_out)
    grads["a_w_out"] = _wgrad("wgrad_a_out", gated_a, dx1, 1).reshape(N_CHIPS, 256, D_MODEL)
    grads["ple_w0"] = _wgrad_ple("wgrad_ple0", p, 0, de_a)
    grads["ple_gate_w0"] = _wgrad("wgrad_gate0", x1, dgp_a, 1).reshape(N_CHIPS, 256, D_MODEL)
    duz, dmr, grad_x, small_a = hosted(_a_mix_bwd, "a_mix_bwd", 4, dgated_a, uz, pooled, wg4, w["a_scale"],
                                       w["a_w_in"], x, dx1, w["a_norm"])
    grads["a_w_in"] = _wgrad("wgrad_a_in", h_a, duz, N_CHIPS)
    grads["a_w_group"] = _wgrad_group(pooled, dmr).reshape(N_CHIPS, N_GROUPS * 64, GROUP_DIM)

    fold = lambda row: jnp.pad(row.reshape(N_HEADS, HEAD_DIM).sum(axis=0), (0, D_MODEL - HEAD_DIM))
    small = jnp.stack([small_a[1], small_a[0], small_b[3], small_b[2], fold(small_b[1]), fold(small_b[0]),
                       jnp.pad(loss_blk[0], (0, D_MODEL - loss_blk.shape[1])), jnp.zeros((D_MODEL,), F32)])
    return grad_x, grads, updates, small


def _mesh_place():
    x, y, c = lax.axis_index("x"), lax.axis_index("y"), lax.axis_index("c")
    other_chips = [(1 - x, y), (x, 1 - y), (1 - x, 1 - y)]
    return x, y, c, other_chips


def _gather_sems(n):
    return [pltpu.SemaphoreType.DMA((3 * n,)), pltpu.SemaphoreType.DMA((3 * n,)),
            pltpu.SemaphoreType.DMA((3 * n,)), pltpu.SemaphoreType.DMA((3 * n,)), pltpu.SemaphoreType.DMA((n,))]


def _gather_copies(srcs, outs, sems):
    send_far, recv_far, send_sib, recv_sib, local_sem = sems
    n = len(srcs)
    x, y, c, chips = _mesh_place()
    me = 2 * x + y
    sibling = (x, y, 1 - c)

    def half(k, which):
        rows = srcs[k].shape[0] // 2
        return pl.ds(pl.multiple_of(which * rows, 16), rows)

    local = [pltpu.make_async_copy(srcs[k], outs[k].at[me], local_sem.at[k]) for k in range(n)]
    far = [pltpu.make_async_remote_copy(
        src_ref=srcs[k].at[half(k, c)], dst_ref=outs[k].at[me, half(k, c)],
        send_sem=send_far.at[j * n + k], recv_sem=recv_far.at[j * n + k], device_id=(px, py, c), device_id_type=MESH)
        for j, (px, py) in enumerate(chips) for k in range(n)]

    def landed(j, k, which, from_far):
        px, py = chips[j]
        piece = outs[k].at[2 * px + py, half(k, which)]
        send, recv = (send_far, recv_far) if from_far else (send_sib, recv_sib)
        return pltpu.make_async_remote_copy(src_ref=piece, dst_ref=piece, send_sem=send.at[j * n + k],
                                            recv_sem=recv.at[j * n + k], device_id=sibling, device_id_type=MESH)

    return local, far, landed, c


def _gather_start(srcs, outs, sems):
    local, far, _, _ = _gather_copies(srcs, outs, sems)
    for cp in local + far:
        cp.start()


def _gather_pass_on(srcs, outs, sems):
    _, _, landed, c = _gather_copies(srcs, outs, sems)
    for j in range(3):
        for k in range(len(srcs)):
            landed(j, k, c, True).wait_recv()
            landed(j, k, c, False).start()


def _gather_finish(srcs, outs, sems):
    local, far, landed, c = _gather_copies(srcs, outs, sems)
    pairs = [(j, k) for j in range(3) for k in range(len(srcs))]
    for j, k in pairs:
        landed(j, k, 1 - c, False).wait_recv()
    for cp in far + [landed(j, k, c, False) for j, k in pairs]:
        cp.wait_send()
    for cp in local:
        cp.wait()


def _call_with_gather(body, *, name, grid, in_specs, out_specs, out_shape, args, gather=(), reduce=None,
                      scratch_shapes=(), vmem_mib=48):
    n_in, n_out, n_scr, n_g = len(args), len(out_shape), len(scratch_shapes), len(gather)
    n_r = len(reduce[0]) if reduce else 0
    pieces = _reduce_pieces(reduce[0]) if reduce else []
    reduce_args = [a for group in reduce for a in group] if reduce else []
    gather_sems = _gather_sems(n_g) if n_g else []
    n_steps = 1
    for g in grid:
        n_steps *= g

    def wrapped(*refs):
        refs = list(refs)
        take = lambda count: [refs.pop(0) for _ in range(count)]
        ins, g_in, r_in = take(n_in), take(n_g), take(4 * n_r)
        outs, g_out, r_out = take(n_out), take(n_g), take(4 * n_r)
        scratch, sems, r_scratch = take(n_scr), take(len(gather_sems)), refs
        step = 0
        for axis, g in enumerate(grid):
            step = step * g + pl.program_id(axis)
        if n_g:
            @pl.when(step == 0)
            def _():
                _gather_start(g_in, g_out, sems)

        if n_r:
            ticks, drain = _reduce_ticks(pieces, n_r, (*r_in, *r_out, *r_scratch))
            for t, tick in enumerate(ticks[:n_steps]):
                pl.when(step == t)(tick)

        body(*ins, *outs, *scratch)
        if n_r:
            for tick in ticks[n_steps:]:
                pl.when(step == n_steps - 1)(tick)
            pl.when(step == n_steps - 1)(drain)
        if n_g:
            @pl.when(step == max(n_steps - 2, 0))
            def _():
                _gather_pass_on(g_in, g_out, sems)

            @pl.when(step == n_steps - 1)
            def _():
                _gather_finish(g_in, g_out, sems)

    hbm = pl.BlockSpec(memory_space=pltpu.HBM)
    res = pl.pallas_call(
        wrapped, name=name, grid=grid,
        in_specs=list(in_specs) + [hbm] * (n_g + 4 * n_r), out_specs=list(out_specs) + [hbm] * (n_g + 4 * n_r),
        out_shape=list(out_shape) + [jax.ShapeDtypeStruct((N_CHIPS,) + g.shape, BF16) for g in gather]
        + ([jax.ShapeDtypeStruct(w.shape, F32) for _ in range(4) for w in reduce[1]] if reduce else []),
        scratch_shapes=list(scratch_shapes) + gather_sems + (_reduce_scratch() if reduce else []),
        compiler_params=_params(("arbitrary",) * len(grid), vmem_mib),
    )(*args, *gather, *reduce_args)
    if not reduce:
        return res
    plain = list(res[:n_out + n_g])
    return plain + [res[n_out + n_g + i * n_r:n_out + n_g + (i + 1) * n_r] for i in range(4)]


def _allgather_weights(shards, small, casts):
    n = len(shards)
    cast_out = [(k, r0, r1) for k, (_, ranges) in enumerate(casts) for r0, r1 in ranges]
    n_c, n_co = len(casts), len(cast_out)

    def body(*refs):
        ins, small_in, cast_in = refs[:n], refs[n], refs[n + 1:n + 1 + n_c]
        refs = refs[n + 1 + n_c:]
        outs, small_out, cast_dst = refs[:n], refs[n], refs[n + 1:n + 1 + n_co]
        refs = refs[n + 1 + n_co:]
        cast, cast_buf = refs[:n], refs[n:n + n_co]
        send_far, recv_far, send_sib, recv_sib, send_small, recv_small, local_sem, cast_sem = refs[n + n_co:]
        x, y, c, chips = _mesh_place()
        me = 2 * x + y
        sibling = (x, y, 1 - c)

        def half(k, which):
            rows = ins[k].shape[0] // 2
            return pl.ds(pl.multiple_of(which * rows, 16), rows)

        local = []
        for k in range(n):
            cast[k][...] = ins[k][...].astype(BF16)
            local.append(pltpu.make_async_copy(cast[k], outs[k].at[me], local_sem.at[k]))
            local[-1].start()
        local.append(pltpu.make_async_copy(small_in, small_out.at[me], local_sem.at[n]))
        local[-1].start()

        sends = []
        for j, (px, py) in enumerate(chips):
            for k in range(n):
                cp = pltpu.make_async_remote_copy(
                    src_ref=cast[k].at[half(k, c)], dst_ref=outs[k].at[me, half(k, c)],
                    send_sem=send_far.at[j * n + k], recv_sem=recv_far.at[j * n + k],
                    device_id=(px, py, c), device_id_type=MESH)
                cp.start()
                sends.append(cp)
            cp = pltpu.make_async_remote_copy(
                src_ref=small_in, dst_ref=small_out.at[me], send_sem=send_small.at[j], recv_sem=recv_small.at[j],
                device_id=(px, py, c), device_id_type=MESH)
            cp.start()
            sends.append(cp)

        for i, (k, r0, r1) in enumerate(cast_out):
            cast_buf[i][...] = cast_in[k][r0:r1, :].astype(BF16)
            local.append(pltpu.make_async_copy(cast_buf[i], cast_dst[i], cast_sem.at[i]))
            local[-1].start()

        def landed(j, k, which, sems_s, sems_r, device):
            px, py = chips[j]
            piece = outs[k].at[2 * px + py, half(k, which)]
            return pltpu.make_async_remote_copy(
                src_ref=piece, dst_ref=piece, send_sem=sems_s.at[j * n + k], recv_sem=sems_r.at[j * n + k],
                device_id=device, device_id_type=MESH)

        for j in range(len(chips)):
            for k in range(n):
                landed(j, k, c, send_far, recv_far, sibling).wait_recv()
                cp = landed(j, k, c, send_sib, recv_sib, sibling)
                cp.start()
                sends.append(cp)
        for j, (px, py) in enumerate(chips):
            for k in range(n):
                landed(j, k, 1 - c, send_sib, recv_sib, sibling).wait_recv()
            pltpu.make_async_remote_copy(
                src_ref=small_in, dst_ref=small_out.at[2 * px + py], send_sem=send_small.at[j],
                recv_sem=recv_small.at[j], device_id=(px, py, c), device_id_type=MESH).wait_recv()
        for cp in sends:
            cp.wait_send()
        for cp in local:
            cp.wait()

    vmem = pl.BlockSpec(memory_space=pltpu.VMEM)
    hbm = pl.BlockSpec(memory_space=pltpu.HBM)
    cast_shapes = [(r1 - r0, casts[k][0].shape[1]) for k, r0, r1 in cast_out]
    res = pl.pallas_call(
        body, name="allgather_weights",
        in_specs=[vmem] * (n + 1 + n_c), out_specs=[hbm] * (n + 1 + n_co),
        out_shape=[jax.ShapeDtypeStruct((N_CHIPS,) + s.shape, BF16) for s in shards]
        + [jax.ShapeDtypeStruct((N_CHIPS,) + small.shape, F32)]
        + [jax.ShapeDtypeStruct(s, BF16) for s in cast_shapes],
        scratch_shapes=[pltpu.VMEM(s.shape, BF16) for s in shards] + [pltpu.VMEM(s, BF16) for s in cast_shapes]
        + [pltpu.SemaphoreType.DMA((3 * n,)), pltpu.SemaphoreType.DMA((3 * n,)),
           pltpu.SemaphoreType.DMA((3 * n,)), pltpu.SemaphoreType.DMA((3 * n,)),
           pltpu.SemaphoreType.DMA((3,)), pltpu.SemaphoreType.DMA((3,)),
           pltpu.SemaphoreType.DMA((n + 1,)), pltpu.SemaphoreType.DMA((n_co,))],
        compiler_params=_params(None, 40),
    )(*shards, small, *[a for a, _ in casts])
    return res[:n], res[n], res[n + 1:]


def _adamw(w, g, m, v):
    m = ADAM_B1 * m + (1.0 - ADAM_B1) * g
    v = ADAM_B2 * v + (1.0 - ADAM_B2) * (g * g)
    m_hat = m / (1.0 - ADAM_B1 ** ADAM_STEP)
    v_hat = v / (1.0 - ADAM_B2 ** ADAM_STEP)
    delta = -ADAM_LR * (m_hat / (jnp.sqrt(v_hat) + ADAM_EPS) + ADAM_WD * w)
    return delta, m, v


RS_PIECE_ROWS = 128
RS_PIECE_COLS = 512


def _reduce_adam_all(grads, ws, ms, vs, small):
    n_w = len(grads)
    pieces = _reduce_pieces(grads)
    n_small = len(_small_sum_scratch(small.shape))

    def body(*refs):
        refs = list(refs)
        small_in = refs.pop(4 * n_w)
        small_out = refs.pop(8 * n_w)
        small_scratch = [refs.pop() for _ in range(n_small)][::-1]
        sends = _small_sum_start(small_in, *small_scratch)
        ticks, drain = _reduce_ticks(pieces, n_w, refs)
        for tick in ticks:
            tick()
        drain()
        _small_sum_finish(sends, small_scratch[0], small_out)

    hbm = pl.BlockSpec(memory_space=pltpu.HBM)
    vmem = pl.BlockSpec(memory_space=pltpu.VMEM)
    outs = pl.pallas_call(
        body, name="reduce_adam_all",
        in_specs=[hbm] * (4 * n_w) + [vmem], out_specs=[hbm] * (4 * n_w) + [vmem],
        out_shape=[jax.ShapeDtypeStruct(w.shape, F32) for _ in range(4) for w in ws]
        + [jax.ShapeDtypeStruct(small.shape, F32)],
        scratch_shapes=_reduce_scratch() + _small_sum_scratch(small.shape),
        compiler_params=_params(None, 48),
    )(*grads, *ws, *ms, *vs, small)
    return [outs[i * n_w:(i + 1) * n_w] for i in range(4)], outs[4 * n_w]


def _reduce_pieces(grads):
    pieces = []
    for k, g in enumerate(grads):
        hr, cols = g.shape[1] // 2, g.shape[2]
        pr, pc = min(hr, RS_PIECE_ROWS), min(cols, RS_PIECE_COLS)
        pieces += [(k, ro, hr, co, pr, pc) for ro in range(0, hr, pr) for co in range(0, cols, pc)]
    return pieces


def _reduce_scratch():
    P, C = RS_PIECE_ROWS, RS_PIECE_COLS
    return [
        pltpu.VMEM((3, N_CHIPS, P, C), F32), pltpu.VMEM((3, N_CHIPS, P, C), F32),
        pltpu.VMEM((2, N_CHIPS, P, C), BF16), pltpu.VMEM((2, N_CHIPS, P, C), BF16),
        pltpu.VMEM((2, N_CHIPS, P, C), F32),
        pltpu.VMEM((2, 3, P, C), BF16), pltpu.VMEM((2, 3, P, C), BF16),
        pltpu.VMEM((2, 2, P, C), F32),
        pltpu.VMEM((2, 3, 2, P, C), F32), pltpu.VMEM((2, 4, 2, P, C), F32),
        pltpu.SemaphoreType.DMA((3, 2)), pltpu.SemaphoreType.DMA((2, 3, 2)),
        pltpu.SemaphoreType.DMA((2,)), pltpu.SemaphoreType.DMA((2,)),
        pltpu.SemaphoreType.DMA((2, 3)), pltpu.SemaphoreType.DMA((2, 3)),
        pltpu.SemaphoreType.DMA((2,)), pltpu.SemaphoreType.DMA((2,)),
        pltpu.SemaphoreType.DMA((2, 4, 2))]


def _reduce_ticks(pieces, n_w, refs):
    n = len(pieces)

    def build(*refs):
        g_in, w_in, m_in, v_in = (refs[i * n_w:(i + 1) * n_w] for i in range(4))
        g_out, d_out, m_out, v_out = (refs[(4 + i) * n_w:(5 + i) * n_w] for i in range(4))
        (gm, go, sb1, rb1, part, sb2, rb2, fin, wmv, outs,
         ld_sem, wmv_sem, s1_send, s1_recv, s2_send, s2_recv, s3_send, s3_recv, out_sem) = refs[8 * n_w:]
        x, y, c, chips = _mesh_place()
        me = 2 * x + y
        sibling = (x, y, 1 - c)

        def at_hbm(i, which):
            _, ro, hr, co, pr, pc = pieces[i]
            half = c if which == 0 else 1 - c
            return pl.ds(pl.multiple_of(half * hr + ro, 64), pr), pl.ds(co, pc)

        def win(i):
            return pl.ds(0, pieces[i][4]), pl.ds(0, pieces[i][5])

        every = slice(None)

        def loads(i):
            k, s = pieces[i][0], i % 3
            return [pltpu.make_async_copy(g_in[k].at[(every,) + at_hbm(i, h)], buf.at[(s, every) + win(i)], ld_sem.at[s, h])
                    for h, buf in enumerate((gm, go))]

        def wmv_loads(i):
            k, s = pieces[i][0], i % 2
            return [pltpu.make_async_copy(src[k].at[at_hbm(i, h)], wmv.at[(s, a, h) + win(i)], wmv_sem.at[s, a, h])
                    for a, src in enumerate((w_in, m_in, v_in)) for h in range(2)]

        def stores(i):
            k, s = pieces[i][0], i % 2
            return [pltpu.make_async_copy(outs.at[(s, a, h) + win(i)], dst[k].at[at_hbm(i, h)], out_sem.at[s, a, h])
                    for a, dst in enumerate((g_out, d_out, m_out, v_out)) for h in range(2)]

        def swap1(i):
            s = i % 2
            return pltpu.make_async_remote_copy(
                src_ref=sb1.at[(s, every) + win(i)], dst_ref=rb1.at[(s, every) + win(i)],
                send_sem=s1_send.at[s], recv_sem=s1_recv.at[s], device_id=sibling, device_id_type=MESH)

        def far2(i, j):
            s = i % 2
            px, py = chips[j]
            return pltpu.make_async_remote_copy(
                src_ref=sb2.at[(s, j) + win(i)], dst_ref=rb2.at[(s, j) + win(i)],
                send_sem=s2_send.at[s, j], recv_sem=s2_recv.at[s, j], device_id=(px, py, c), device_id_type=MESH)

        def swap3(i):
            s = i % 2
            return pltpu.make_async_remote_copy(
                src_ref=fin.at[(s, 0) + win(i)], dst_ref=fin.at[(s, 1) + win(i)],
                send_sem=s3_send.at[s], recv_sem=s3_recv.at[s], device_id=sibling, device_id_type=MESH)

        def stage0(i):
            for cp in loads(i):
                cp.start()

        def stage1(i):
            s, s3 = i % 2, i % 3
            for cp in loads(i):
                cp.wait()
            sb1[(s, every) + win(i)] = go[(s3, every) + win(i)].astype(BF16)
            swap1(i).start()

        def stage2(i):
            s, s3 = i % 2, i % 3
            swap1(i).wait()
            part[(s, every) + win(i)] = gm[(s3, every) + win(i)] + rb1[(s, every) + win(i)].astype(F32)
            for j, (px, py) in enumerate(chips):
                sb2[(s, j) + win(i)] = part[(s, 2 * px + py) + win(i)].astype(BF16)
                far2(i, j).start()

        def stage3(i):
            s = i % 2
            total = part[(s, me) + win(i)]
            for j in range(3):
                far2(i, j).wait()
                total = total + rb2[(s, j) + win(i)].astype(F32)
            fin[(s, 0) + win(i)] = total
            swap3(i).start()
            for cp in wmv_loads(i):
                cp.start()

        def stage4(i):
            s = i % 2
            if i >= 2:
                for cp in stores(i - 2):
                    cp.wait()
            swap3(i).wait()
            for cp in wmv_loads(i):
                cp.wait()
            both = (every,) + win(i)
            g = fin[(s,) + both]
            delta, m_new, v_new = _adamw(wmv[(s, 0) + both], g, wmv[(s, 1) + both], wmv[(s, 2) + both])
            outs[(s, 0) + both] = g
            outs[(s, 1) + both] = delta
            outs[(s, 2) + both] = m_new
            outs[(s, 3) + both] = v_new
            for cp in stores(i):
                cp.start()

        stages = (stage0, stage1, stage2, stage3, stage4)

        def tick(t):
            for age in reversed(range(len(stages))):
                if 0 <= t - age < n:
                    stages[age](t - age)

        def drain():
            for i in range(max(0, n - 2), n):
                for cp in stores(i):
                    cp.wait()

        return [functools.partial(tick, t) for t in range(n + len(stages) - 1)], drain

    return build(*refs)


N_DEVICES = 8


def _small_sum_scratch(shape):
    return [pltpu.VMEM((N_DEVICES,) + shape, F32),
            pltpu.SemaphoreType.DMA((N_DEVICES - 1,)), pltpu.SemaphoreType.DMA((N_DEVICES - 1,))]


def _small_sum_start(part_ref, buf, send_sem, recv_sem):
    x, y, c, _ = _mesh_place()
    me = 4 * x + 2 * y + c
    buf[me] = part_ref[...]
    sends = []
    for k in range(1, N_DEVICES):
        peer = ((1 - x) if k & 4 else x, (1 - y) if k & 2 else y, (1 - c) if k & 1 else c)
        cp = pltpu.make_async_remote_copy(src_ref=part_ref, dst_ref=buf.at[me], send_sem=send_sem.at[k - 1],
                                          recv_sem=recv_sem.at[k - 1], device_id=peer, device_id_type=MESH)
        cp.start()
        sends.append(cp)
    return sends


def _small_sum_finish(sends, buf, out_ref):
    for cp in sends:
        cp.wait_recv()
    total = buf[0]
    for s in range(1, N_DEVICES):
        total = total + buf[s]
    out_ref[...] = total
    for cp in sends:
        cp.wait_send()


def _adam_small(w, g, m, v):
    def body(w_ref, g_ref, m_ref, v_ref, d_ref, mo_ref, vo_ref):
        delta, m_new, v_new = _adamw(w_ref[...], g_ref[...], m_ref[...], v_ref[...])
        d_ref[...] = delta
        mo_ref[...] = m_new
        vo_ref[...] = v_new

    vmem = pl.BlockSpec(memory_space=pltpu.VMEM)
    return pl.pallas_call(
        body, name="adam_small", in_specs=[vmem] * 4, out_specs=[vmem] * 3,
        out_shape=[jax.ShapeDtypeStruct(w.shape, F32)] * 3,
    )(w, g, m, v)


BIG = ("a_w_in", "a_w_group", "a_w_out", "w_kv", "b_w_in", "b_w_out", "ple_w", "ple_gate_w")
SMALL = ("a_norm", "a_scale", "kv_norm", "b_norm", "k_norm", "b_q_norm")
SMALL_SHARDED = ("a_norm", "a_scale")
WEIGHTS = ("a_norm", "a_w_in", "a_w_group", "a_scale", "a_w_out", "kv_norm", "w_kv", "k_norm", "b_norm", "b_w_in",
           "b_q_norm", "b_w_out", "ple_w", "ple_gate_w")


def _as_matrix(a):
    return a.reshape(-1, a.shape[-1])


def _pack_small(arrs):
    rows = [jnp.pad(a.reshape(-1), (0, D_MODEL - a.size)) for a in arrs]
    rows += [jnp.zeros((D_MODEL,), F32)] * (8 - len(rows))
    return jnp.stack(rows)


def kernel(x, p, a_norm, a_w_in, a_w_group, a_scale, a_w_out, kv_norm, w_kv, k_norm, b_norm, b_w_in, b_q_norm, b_w_out, ple_w, ple_gate_w, loss_target, m_a_norm, m_a_w_in, m_a_w_group, m_a_scale, m_a_w_out, m_kv_norm, m_w_kv, m_k_norm, m_b_norm, m_b_w_in, m_b_q_norm, m_b_w_out, m_ple_w, m_ple_gate_w, v_a_norm, v_a_w_in, v_a_w_group, v_a_scale, v_a_w_out, v_kv_norm, v_w_kv, v_k_norm, v_b_norm, v_b_w_in, v_b_q_norm, v_b_w_out, v_ple_w, v_ple_gate_w):
    wts = dict(a_norm=a_norm, a_w_in=a_w_in, a_w_group=a_w_group, a_scale=a_scale, a_w_out=a_w_out, kv_norm=kv_norm,
               w_kv=w_kv, k_norm=k_norm, b_norm=b_norm, b_w_in=b_w_in, b_q_norm=b_q_norm, b_w_out=b_w_out,
               ple_w=ple_w, ple_gate_w=ple_gate_w)
    mom = dict(a_norm=m_a_norm, a_w_in=m_a_w_in, a_w_group=m_a_w_group, a_scale=m_a_scale, a_w_out=m_a_w_out,
               kv_norm=m_kv_norm, w_kv=m_w_kv, k_norm=m_k_norm, b_norm=m_b_norm, b_w_in=m_b_w_in,
               b_q_norm=m_b_q_norm, b_w_out=m_b_w_out, ple_w=m_ple_w, ple_gate_w=m_ple_gate_w)
    var = dict(a_norm=v_a_norm, a_w_in=v_a_w_in, a_w_group=v_a_w_group, a_scale=v_a_scale, a_w_out=v_a_w_out,
               kv_norm=v_kv_norm, w_kv=v_w_kv, k_norm=v_k_norm, b_norm=v_b_norm, b_w_in=v_b_w_in,
               b_q_norm=v_b_q_norm, b_w_out=v_b_w_out, ple_w=v_ple_w, ple_gate_w=v_ple_gate_w)
    S = x.shape[1]
    chip = 2 * lax.axis_index("x") + lax.axis_index("y")

    sharded_small = jnp.concatenate([a_norm.reshape(1, 256), a_scale.reshape(1, 256), jnp.zeros((6, 256), F32)], axis=0)
    later = ("a_w_group", "a_w_out", "w_kv", "b_w_in", "b_w_out", "ple_w", "ple_gate_w")
    (a_w_in_full,), small_full, copies = _allgather_weights(
        [_as_matrix(a_w_in)], sharded_small,
        [(_as_matrix(wts[n]), [(0, 256), (256, 512)] if n.startswith("ple") else [(0, _as_matrix(wts[n]).shape[0])])
         for n in later])
    local = dict(zip(("a_w_group", "a_w_out", "w_kv", "b_w_in", "b_w_out", "ple_w0", "ple_w1", "ple_gate_w0",
                      "ple_gate_w1"), copies))
    full = dict(a_w_in=a_w_in_full,
                a_norm=small_full[:, 0, :].reshape(1, D_MODEL), a_scale=small_full[:, 1, :].reshape(1, D_MODEL),
                kv_norm=kv_norm.reshape(1, D_MODEL), b_norm=b_norm.reshape(1, D_MODEL), k_norm=k_norm, b_q_norm=b_q_norm)

    def shards(t):
        out = {}
        for n in BIG:
            mat = _as_matrix(t[n])
            if n.startswith("ple"):
                out[n + "0"], out[n + "1"] = mat[:256], mat[256:]
            else:
                out[n] = mat
        return out

    state = (shards(wts), shards(mom), shards(var))
    grad_x, grads, updates, small_part = _local_step(x.reshape(S, D_MODEL), p, loss_target.reshape(S, D_MODEL),
                                                     full, local, state)

    names = sorted(grads)
    reduced, small_sum = _reduce_adam_all([grads[n] for n in names], *[[t[n] for n in names] for t in state],
                                          small_part)
    for i, n in enumerate(names):
        updates[n] = tuple(group[i] for group in reduced)
    out_g, out_d, out_m, out_v = {}, {}, {}, {}
    for n in BIG:
        for i, out in enumerate((out_g, out_d, out_m, out_v)):
            if n.startswith("ple"):
                out[n] = jnp.stack([updates[n + "0"][i], updates[n + "1"][i]]).reshape(wts[n].shape)
            else:
                out[n] = updates[n][i].reshape(wts[n].shape)

    loss = small_sum[len(SMALL), 0]
    small_rows = []
    for i, n in enumerate(SMALL):
        row = small_sum[i]
        if n in SMALL_SHARDED:
            row = lax.dynamic_slice(row, (chip * 256,), (256,))
        else:
            row = row[:wts[n].size]
        small_rows.append(row)
    g_small = _pack_small(small_rows)
    d_small, m_small, v_small = _adam_small(_pack_small([wts[n] for n in SMALL]), g_small,
                                            _pack_small([mom[n] for n in SMALL]), _pack_small([var[n] for n in SMALL]))
    for i, n in enumerate(SMALL):
        shape, size = wts[n].shape, wts[n].size
        out_g[n], out_d[n], out_m[n], out_v[n] = (t[i, :size].reshape(shape) for t in (g_small, d_small, m_small, v_small))

    return (loss, grad_x.reshape(1, S, D_MODEL), *[out_g[n] for n in WEIGHTS], *[out_d[n] for n in WEIGHTS],
            *[out_m[n] for n in WEIGHTS], *[out_v[n] for n in WEIGHTS])
```

```python
import functools

import jax
import jax.numpy as jnp
from jax import lax
from jax.experimental import pallas as pl
from jax.experimental.pallas import tpu as pltpu

F32 = jnp.float32
BF16 = jnp.bfloat16
MESH = pl.DeviceIdType.MESH

D_MODEL = 1024
N_HEADS = 16
HEAD_DIM = 64
PLE_DIM = 256
N_GROUPS = 4
GROUP_DIM = 256
POOL_WINDOWS = (2, 4, 8, 16)
N_CHIPS = 4
EPS = 1e-6
SB_SCALE = HEAD_DIM ** -0.5

ADAM_LR = 0.001
ADAM_B1 = 0.9
ADAM_B2 = 0.999
ADAM_EPS = 1e-08
ADAM_WD = 0.01
ADAM_STEP = 10

ROW_TILE = 256
WIDE_ROW_TILE = 512
EXP_UNDERFLOW = -104.0
ATT_Q_TILE = 512
ATT_K_TILE = 256
WGRAD_SEQ_TILE = 1024
WGRAD_ACC_BYTES = 4 * 1024 * 1024
MIB = 1024 * 1024


def _params(semantics=None, vmem_mib=48):
    return pltpu.CompilerParams(dimension_semantics=semantics, vmem_limit_bytes=vmem_mib * MIB)


def _dot(a, b):
    return jnp.dot(a, b, preferred_element_type=F32)


def _dot_nt(a, b):
    return lax.dot_general(a, b, (((1,), (1,)), ((), ())), preferred_element_type=F32)


def _dot_tn(a, b):
    return lax.dot_general(a, b, (((0,), (0,)), ((), ())), preferred_element_type=F32)


def _hilo(x):
    hi = x.astype(BF16)
    lo = (x - hi.astype(F32)).astype(BF16)
    return hi, lo


def _dot_hilo(x, w):
    hi, lo = _hilo(x)
    return _dot(hi, w) + _dot(lo, w)


def _sigmoid(z):
    return jax.nn.sigmoid(z)


def _dsilu(z, sg):
    return sg * (1.0 + z * (1.0 - sg))


def _mask_bf16(cond):
    return jnp.where(cond, 1.0, 0.0).astype(BF16)


def _head_mean_matrix():
    r = lax.broadcasted_iota(jnp.int32, (256, 256), 0) // HEAD_DIM
    c = lax.broadcasted_iota(jnp.int32, (256, 256), 1) // HEAD_DIM
    return _mask_bf16(r == c)


def _head_mean(x, bd):
    parts = []
    for s in range(x.shape[1] // 256):
        parts.append(_dot_hilo(x[:, s * 256:(s + 1) * 256], bd))
    out = parts[0] if len(parts) == 1 else jnp.concatenate(parts, axis=1)
    return out * (1.0 / HEAD_DIM)


def _a_in(x, gain, w_sh, gather=()):
    S = x.shape[0]
    tm = 512
    nsh, _, wn = w_sh.shape

    def body(x_ref, g_ref, w_ref, uz_ref, h_ref):
        @pl.when(pl.program_id(1) == 0)
        def _():
            xv = x_ref[...]
            r = lax.rsqrt(jnp.mean(xv * xv, axis=-1, keepdims=True) + EPS)
            h_ref[...] = (xv * r * g_ref[...]).astype(BF16)

        uz_ref[...] = _dot(h_ref[...], w_ref[0])

    return _call_with_gather(
        body, name="a_in", grid=(S // tm, nsh),
        in_specs=[pl.BlockSpec((tm, D_MODEL), lambda i, j: (i, 0)),
                  pl.BlockSpec((1, D_MODEL), lambda i, j: (0, 0)),
                  pl.BlockSpec((1, D_MODEL, wn), lambda i, j: (j, 0, 0))],
        out_specs=[pl.BlockSpec((tm, wn), lambda i, j: (i, j)),
                   pl.BlockSpec((tm, D_MODEL), lambda i, j: (i, 0))],
        out_shape=[jax.ShapeDtypeStruct((S, nsh * wn), F32),
                   jax.ShapeDtypeStruct((S, D_MODEL), BF16)],
        args=(x, gain, w_sh), gather=gather)


def _inv_count(first_row, rows, w):
    t1 = first_row + 1 + lax.broadcasted_iota(jnp.int32, (rows, 1), 0)
    return 1.0 / jnp.minimum(t1, w).astype(F32)


def _group_weight(wg_ref, g):
    return jnp.concatenate([wg_ref[sh, g] for sh in range(N_CHIPS)], axis=0)


def _a_mix(uz, wg, scale, gather=()):
    S = uz.shape[0]
    tm = ROW_TILE

    def body(u_ref, up_ref, z_ref, wg_ref, sc_ref, ga_ref, p_ref):
        i = pl.program_id(0)
        row = lax.broadcasted_iota(jnp.int32, (tm, tm), 0)
        col = lax.broadcasted_iota(jnp.int32, (tm, tm), 1)
        d = row - col
        for g, w in enumerate(POOL_WINDOWS):
            cols = slice(g * GROUP_DIM, (g + 1) * GROUP_DIM)
            t_main = _mask_bf16((d >= 0) & (d < w))
            t_halo = _mask_bf16(d + tm < w)
            u = u_ref[:, cols]
            up = jnp.where(i > 0, up_ref[:, cols], 0.0)
            hi, lo = _hilo(u)
            hip, lop = _hilo(up)
            wsum = _dot(t_main, hi) + _dot(t_main, lo) + _dot(t_halo, hip) + _dot(t_halo, lop)
            pooled = (wsum * _inv_count(i * tm, tm, w) - u).astype(BF16)
            p_ref[:, cols] = pooled
            mraw = _dot(pooled, _group_weight(wg_ref, g))
            z = z_ref[:, cols]
            ga_ref[:, cols] = (mraw * sc_ref[:, cols] * (z * _sigmoid(z))).astype(BF16)

    return _call_with_gather(
        body, name="a_mix", grid=(S // tm,),
        in_specs=[pl.BlockSpec((tm, D_MODEL), lambda i: (i, 0)),
                  pl.BlockSpec((tm, D_MODEL), lambda i: (jnp.maximum(i - 1, 0), 0)),
                  pl.BlockSpec((tm, D_MODEL), lambda i: (i, 1)),
                  pl.BlockSpec((N_CHIPS, N_GROUPS, 64, GROUP_DIM), lambda i: (0, 0, 0, 0)),
                  pl.BlockSpec((1, D_MODEL), lambda i: (0, 0))],
        out_specs=[pl.BlockSpec((tm, D_MODEL), lambda i: (i, 0)),
                   pl.BlockSpec((tm, D_MODEL), lambda i: (i, 0))],
        out_shape=[jax.ShapeDtypeStruct((S, D_MODEL), BF16),
                   jax.ShapeDtypeStruct((S, D_MODEL), BF16)],
        args=(uz, uz, uz, wg, scale), gather=gather)


def _out_ple(name, gated, x_in, w_out, p, layer, ple_w, ple_g, target=None, gather=()):
    S = x_in.shape[0]
    tm = WIDE_ROW_TILE
    with_loss = target is not None

    def body(*refs):
        if with_loss:
            g_ref, x_ref, wo_ref, p_ref, pw_ref, pg_ref, t_ref, xm_ref, dx_ref, e_ref, gt_ref, loss_ref = refs
        else:
            g_ref, x_ref, wo_ref, p_ref, pw_ref, pg_ref, xm_ref, xo_ref, e_ref, gt_ref = refs
        xm = x_ref[...] + _dot(g_ref[...], wo_ref[...])
        xm_ref[...] = xm
        pb = p_ref[...].astype(BF16)
        e = jnp.concatenate([_dot(pb, pw_ref[sh]) for sh in range(N_CHIPS)], axis=1)
        pg = jnp.concatenate([pg_ref[sh] for sh in range(N_CHIPS)], axis=0)
        gate = _sigmoid(_dot(xm.astype(BF16), pg))
        e_ref[...] = e.astype(BF16)
        gt_ref[...] = gate.astype(BF16)
        xo = xm + e * gate
        if with_loss:
            diff = xo - t_ref[...]
            dx_ref[...] = diff * (1.0 / D_MODEL)

            @pl.when(pl.program_id(0) == 0)
            def _():
                loss_ref[...] = jnp.zeros_like(loss_ref)

            loss_ref[...] += jnp.sum(diff * diff) * (0.5 / D_MODEL)
        else:
            xo_ref[...] = xo

    row = pl.BlockSpec((tm, D_MODEL), lambda i: (i, 0))
    in_specs = [row, row,
                pl.BlockSpec((D_MODEL, D_MODEL), lambda i: (0, 0)),
                pl.BlockSpec((None, None, tm, PLE_DIM), lambda i: (layer, 0, i, 0)),
                pl.BlockSpec((N_CHIPS, PLE_DIM, 256), lambda i: (0, 0, 0)),
                pl.BlockSpec((N_CHIPS, 256, D_MODEL), lambda i: (0, 0, 0))]
    args = [gated, x_in, w_out, p, ple_w, ple_g]
    out_specs = [row, row, row, row]
    out_shape = [jax.ShapeDtypeStruct((S, D_MODEL), F32), jax.ShapeDtypeStruct((S, D_MODEL), F32),
                 jax.ShapeDtypeStruct((S, D_MODEL), BF16), jax.ShapeDtypeStruct((S, D_MODEL), BF16)]
    if with_loss:
        in_specs.append(row)
        args.append(target)
        out_specs.append(pl.BlockSpec((8, 128), lambda i: (0, 0)))
        out_shape.append(jax.ShapeDtypeStruct((8, 128), F32))
    return _call_with_gather(body, name=name, grid=(S // tm,), in_specs=in_specs, out_specs=out_specs,
                             out_shape=out_shape, args=args, gather=gather)


def _b_in(x, kv_gain, b_gain, k_gain_t, q_gain_t, w_kv, w_in, gather=()):
    S = x.shape[0]
    tm = ROW_TILE

    def body(x_ref, kvg_ref, bg_ref, kg_ref, qg_ref, wkv_ref, win_ref,
             hkv_ref, hb_ref, kraw_ref, qraw_ref, k_ref, q_ref, v_ref, z_ref):
        xv = x_ref[...]
        y = xv * lax.rsqrt(jnp.mean(xv * xv, axis=-1, keepdims=True) + EPS)
        hkv = (y * kvg_ref[...]).astype(BF16)
        hb = (y * bg_ref[...]).astype(BF16)
        hkv_ref[...] = hkv
        hb_ref[...] = hb
        bd = _head_mean_matrix()

        def head_norm(raw, gain):
            rr = lax.rsqrt(_head_mean(raw * raw, bd) + EPS)
            return raw * rr * gain

        for sh in range(N_CHIPS):
            kvc = _dot(hkv, wkv_ref[sh])
            qzc = _dot(hb, win_ref[sh])
            cols = slice((sh % 2) * 512, (sh % 2) * 512 + 512)
            if sh < 2:
                kraw_ref[:, cols] = kvc.astype(BF16)
                qraw_ref[:, cols] = qzc.astype(BF16)
                k_ref[:, cols] = head_norm(kvc, kg_ref[:, cols]).astype(BF16)
                q_ref[:, cols] = (head_norm(qzc, qg_ref[:, cols]) * SB_SCALE).astype(BF16)
            else:
                v_ref[:, cols] = kvc.astype(BF16)
                z_ref[:, cols] = qzc.astype(BF16)

    row = pl.BlockSpec((tm, D_MODEL), lambda i: (i, 0))
    vec = pl.BlockSpec((1, D_MODEL), lambda i: (0, 0))
    wsp = pl.BlockSpec((N_CHIPS, D_MODEL, 512), lambda i: (0, 0, 0))
    return _call_with_gather(
        body, name="b_in", grid=(S // tm,),
        in_specs=[row, vec, vec, vec, vec, wsp, wsp],
        out_specs=[row] * 8,
        out_shape=[jax.ShapeDtypeStruct((S, D_MODEL), BF16)] * 8,
        args=(x, kv_gain, b_gain, k_gain_t, q_gain_t, w_kv, w_in), gather=gather, vmem_mib=56)


def _softplus_parts(z):
    e = jnp.exp(-jnp.abs(z))
    return -(jnp.maximum(z, 0.0) + jnp.log(1.0 + e)), e


def _add_rows(total, rows, update):
    lo, hi = rows
    parts = ([total[:lo]] if lo else []) + [total[lo:hi] + update] + ([total[hi:]] if hi < total.shape[0] else [])
    return parts[0] if len(parts) == 1 else jnp.concatenate(parts, axis=0)


def _attn_fwd(q, k, v, zgate, gather=()):
    S = q.shape[0]
    tq, tk = ATT_Q_TILE, ATT_K_TILE
    kpq = tq // tk
    assert kpq == 2

    def body(q_ref, k_ref, v_ref, z_ref, o_ref, g_ref, lt_ref, steps_ref):
        qi = pl.program_id(1)
        lane = lax.broadcasted_iota(jnp.int32, (1, 128), 1)
        ri = lax.broadcasted_iota(jnp.int32, (tk, tk), 0)
        ci = lax.broadcasted_iota(jnp.int32, (tk, tk), 1)
        later_mat = _mask_bf16(ri > ci)
        causal = ci < ri
        qv = q_ref[...]
        first = lane < HEAD_DIM
        q_heads = (jnp.where(first, qv, jnp.zeros_like(qv)), jnp.where(first, jnp.zeros_like(qv), qv))

        def step(blocks, carry):
            chains = [(b, h) for b in range(len(blocks)) for h in range(2)]
            rows = [r for _, r, _ in blocks]
            s0 = [pl.multiple_of(kj * tk, tk) for kj, _, _ in blocks]
            kb = [k_ref[pl.ds(s, tk), :] for s in s0]
            vb = [v_ref[pl.ds(s, tk), :] for s in s0]
            visible = [causal if masked else None for _, _, masked in blocks]
            z = {c: _dot_nt(q_heads[c[1]][rows[c[0]][0]:rows[c[0]][1]], kb[c[0]]) for c in chains}
            run = [carry[0], carry[2]]
            log_own, later, run_at = {}, {}, {}
            for c in chains:
                b, h = c
                lk = _softplus_parts(z[c])[0]
                if visible[b] is not None:
                    lk = jnp.where(visible[b], lk, 0.0)
                log_own[c] = z[c] + lk
                later[c] = _dot(lk.astype(BF16), later_mat)
                run_at[c] = run[h][rows[b][0]:rows[b][1]]
                run[h] = _add_rows(run[h], rows[b], jnp.sum(lk, axis=-1, keepdims=True))
            acc = [carry[1], carry[3]]
            for c in chains:
                b, h = c
                a = jnp.exp(log_own[c] + later[c] + run_at[c])
                if visible[b] is not None:
                    a = jnp.where(visible[b], a, 0.0)
                acc[h] = _add_rows(acc[h], rows[b], _dot(a.astype(BF16), vb[b]))
            return run[0], acc[0], run[1], acc[1]

        zero1, zero128 = jnp.zeros((tq, 1), F32), jnp.zeros((tq, 128), F32)
        carry = step([(qi * kpq + 1, (tk, tq), True), (qi * kpq, (tk, tq), False), (qi * kpq, (0, tk), True)],
                     (zero1, zero128, zero1, zero128))

        def low(run):
            return jnp.max(run)

        def pair_more(c):
            return (c[0] < qi) & (jnp.maximum(low(c[1][tk:]), low(c[3][tk:])) > EXP_UNDERFLOW)

        def pair_step(c):
            last = (qi - c[0]) * kpq - 1
            return (c[0] + 1, *step([(last, (0, tq), False), (last - 1, (0, tq), False)], c[1:]))

        pairs, *carry = lax.while_loop(pair_more, pair_step, (jnp.int32(0), *carry))
        left = (qi - pairs) * kpq

        def single_more(c):
            return (c[0] < left) & (jnp.maximum(low(c[1][:tk]), low(c[3][:tk])) > EXP_UNDERFLOW)

        def single_step(c):
            return (c[0] + 1, *step([(left - 1 - c[0], (0, tk), False)], c[1:]))

        singles, *carry = lax.while_loop(single_more, single_step, (jnp.int32(0), *carry))
        steps_ref[...] = jnp.concatenate([jnp.full((4, 128), pairs, F32), jnp.full((4, 128), singles, F32)], axis=0)
        o_tot = jnp.where(first, carry[1], carry[3])
        l_tot = jnp.where(first, carry[0], carry[2])
        o_ref[...] = o_tot.astype(BF16)
        lt_ref[...] = l_tot
        zz = z_ref[...].astype(F32)
        g_ref[...] = (o_tot * (zz * _sigmoid(zz))).astype(BF16)

    blk = pl.BlockSpec((tq, 128), lambda hp, qi: (qi, hp))
    seq = pl.BlockSpec((S, 128), lambda hp, qi: (0, hp))
    return _call_with_gather(
        body, name="attn_fwd", grid=(D_MODEL // 128, S // tq),
        in_specs=[blk, seq, seq, blk],
        out_specs=[blk, blk, blk, pl.BlockSpec((None, None, 8, 128), lambda hp, qi: (hp, qi, 0, 0))],
        out_shape=[jax.ShapeDtypeStruct((S, D_MODEL), BF16)] * 2 + [jax.ShapeDtypeStruct((S, D_MODEL), F32)]
        + [jax.ShapeDtypeStruct((D_MODEL // 128, S // tq, 8, 128), F32)],
        args=(q, k, v, zgate), gather=gather)


def _ple_out_bwd(name, dx_out, e, gate, ple_g, w_out):
    S = dx_out.shape[0]
    tm = WIDE_ROW_TILE

    def body(dx_ref, e_ref, gt_ref, pg_ref, wo_ref, de_ref, dgp_ref, dxm_ref, dg_ref):
        dxo = dx_ref[...]
        ev = e_ref[...].astype(F32)
        gv = gt_ref[...].astype(F32)
        de_ref[...] = (dxo * gv).astype(BF16)
        dgp = (dxo * ev * gv * (1.0 - gv)).astype(BF16)
        dgp_ref[...] = dgp
        pg = jnp.concatenate([pg_ref[sh] for sh in range(N_CHIPS)], axis=0)
        dxm = dxo + _dot_nt(dgp, pg)
        dxm_ref[...] = dxm
        dg_ref[...] = _dot_nt(dxm.astype(BF16), wo_ref[...]).astype(BF16)

    row = pl.BlockSpec((tm, D_MODEL), lambda i: (i, 0))
    return pl.pallas_call(
        body, name=name, grid=(S // tm,),
        in_specs=[row, row, row,
                  pl.BlockSpec((N_CHIPS, 256, D_MODEL), lambda i: (0, 0, 0)),
                  pl.BlockSpec((D_MODEL, D_MODEL), lambda i: (0, 0))],
        out_specs=[row, row, row, row],
        out_shape=[jax.ShapeDtypeStruct((S, D_MODEL), BF16), jax.ShapeDtypeStruct((S, D_MODEL), BF16),
                   jax.ShapeDtypeStruct((S, D_MODEL), F32), jax.ShapeDtypeStruct((S, D_MODEL), BF16)],
        compiler_params=_params(("arbitrary",)),
    )(dx_out, e, gate, ple_g, w_out)


def _attn_bwd(q, k, v, ltot, steps, dgated, o, zgate, reduce=None):
    S = q.shape[0]
    tq, tk = ATT_Q_TILE, ATT_K_TILE
    kpq = tq // tk
    nq = S // tq

    def body(q_ref, k_ref, v_ref, lt_ref, steps_ref, dg_ref, o_ref, z_ref, dq_ref, dk_ref, dv_ref, dz_ref,
             dk_acc, dv_acc):
        qi = pl.program_id(1)

        @pl.when(qi == 0)
        def _():
            dk_acc[...] = jnp.zeros_like(dk_acc)
            dv_acc[...] = jnp.zeros_like(dv_acc)

        lane = lax.broadcasted_iota(jnp.int32, (1, 128), 1)
        ri = lax.broadcasted_iota(jnp.int32, (tk, tk), 0)
        ci = lax.broadcasted_iota(jnp.int32, (tk, tk), 1)
        later_mat = _mask_bf16(ri > ci)
        before_mat = _mask_bf16(ri < ci)
        causal = ci < ri
        zz = z_ref[...].astype(F32)
        sg = _sigmoid(zz)
        dgv = dg_ref[...].astype(F32)
        dz_ref[...] = (dgv * o_ref[...].astype(F32) * _dsilu(zz, sg)).astype(BF16)
        dob = (dgv * (zz * sg)).astype(BF16)
        ltv = lt_ref[...]
        qv = q_ref[...]
        first = lane < HEAD_DIM
        masks = (first, jnp.logical_not(first))
        q_heads = [jnp.where(hm, qv, jnp.zeros_like(qv)) for hm in masks]
        do_heads = [jnp.where(hm, dob, jnp.zeros_like(dob)) for hm in masks]
        totals = [jnp.max(jnp.where(hm, ltv, -jnp.inf), axis=-1, keepdims=True) for hm in masks]

        def step(blocks, carry):
            chains = [(b, h) for b in range(len(blocks)) for h in range(2)]
            rows = [r for _, r, _ in blocks]
            cut = lambda t, b: t[rows[b][0]:rows[b][1]]
            s0 = [pl.multiple_of(kj * tk, tk) for kj, _, _ in blocks]
            kb = [k_ref[pl.ds(s, tk), :] for s in s0]
            vb = [v_ref[pl.ds(s, tk), :] for s in s0]
            visible = [causal if masked else None for _, _, masked in blocks]
            z = {c: _dot_nt(cut(q_heads[c[1]], c[0]), kb[c[0]]) for c in chains}
            da = {c: _dot_nt(cut(do_heads[c[1]], c[0]), vb[c[0]]) for c in chains}
            run = [carry[0], carry[3]]
            log_own, beta, later, base = {}, {}, {}, {}
            for c in chains:
                b, h = c
                lk = _softplus_parts(z[c])[0]
                if visible[b] is not None:
                    lk = jnp.where(visible[b], lk, 0.0)
                log_own[c] = z[c] + lk
                beta[c] = jnp.exp(log_own[c]).astype(BF16)
                later[c] = _dot(lk.astype(BF16), later_mat)
                run[h] = _add_rows(run[h], rows[b], jnp.sum(lk, axis=-1, keepdims=True))
                base[c] = cut(totals[h] - run[h], b)
            grun = [carry[1], carry[4]]
            a_bf, g_bf, gbefore, grun_at = {}, {}, {}, {}
            for c in chains:
                b, h = c
                a = jnp.exp(log_own[c] + later[c] + base[c])
                if visible[b] is not None:
                    a = jnp.where(visible[b], a, 0.0)
                a_bf[c] = a.astype(BF16)
                g = da[c] * a
                g_bf[c] = g.astype(BF16)
                gbefore[c] = _dot(g_bf[c], before_mat)
                grun_at[c] = cut(grun[h], b)
                grun[h] = _add_rows(grun[h], rows[b], jnp.sum(g, axis=-1, keepdims=True))
            dq = [carry[2], carry[5]]
            dk_blk = [jnp.zeros((tk, 128), F32) for _ in blocks]
            dv_blk = [jnp.zeros((tk, 128), F32) for _ in blocks]
            for c in chains:
                b, h = c
                g = g_bf[c].astype(F32)
                dz = g - beta[c].astype(F32) * (g + gbefore[c] + grun_at[c])
                if visible[b] is not None:
                    dz = jnp.where(visible[b], dz, 0.0)
                dzb = dz.astype(BF16)
                dq[h] = _add_rows(dq[h], rows[b], _dot(dzb, kb[b]))
                dk_blk[b] = dk_blk[b] + _dot_tn(dzb, cut(q_heads[h], b))
                dv_blk[b] = dv_blk[b] + _dot_tn(a_bf[c], cut(do_heads[h], b))
            for b in range(len(blocks)):
                dk_acc[pl.ds(s0[b], tk), :] += dk_blk[b]
                dv_acc[pl.ds(s0[b], tk), :] += dv_blk[b]
            return run[0], grun[0], dq[0], run[1], grun[1], dq[1]

        pairs = jnp.clip(jnp.max(steps_ref[0:4, :]).astype(jnp.int32), 0, qi)
        left = (qi - pairs) * kpq
        singles = jnp.clip(jnp.max(steps_ref[4:8, :]).astype(jnp.int32), 0, left)
        zero1, zero128 = jnp.zeros((tq, 1), F32), jnp.zeros((tq, 128), F32)
        carry = lax.fori_loop(left - singles, left, lambda kj, c: step([(kj, (0, tk), False)], c),
                              (zero1, zero1, zero128, zero1, zero1, zero128))
        carry = lax.fori_loop(qi - pairs, qi,
                              lambda n, c: step([(n * kpq, (0, tq), False), (n * kpq + 1, (0, tq), False)], c), carry)
        carry = step([(qi * kpq, (0, tk), True), (qi * kpq, (tk, tq), False), (qi * kpq + 1, (tk, tq), True)], carry)
        dq_ref[...] = jnp.where(first, carry[2], carry[5]).astype(BF16)

        @pl.when(qi == nq - 1)
        def _():
            dk_ref[...] = dk_acc[...].astype(BF16)
            dv_ref[...] = dv_acc[...].astype(BF16)

    blk = pl.BlockSpec((tq, 128), lambda hp, qi: (qi, hp))
    seq = pl.BlockSpec((S, 128), lambda hp, qi: (0, hp))
    return _call_with_gather(
        body, name="attn_bwd", grid=(D_MODEL // 128, nq),
        in_specs=[blk, seq, seq, blk, pl.BlockSpec((None, None, 8, 128), lambda hp, qi: (hp, qi, 0, 0)),
                  blk, blk, blk],
        out_specs=[blk, seq, seq, blk],
        out_shape=[jax.ShapeDtypeStruct((S, D_MODEL), BF16)] * 4,
        scratch_shapes=[pltpu.VMEM((S, 128), F32), pltpu.VMEM((S, 128), F32)],
        args=(q, k, v, ltot, steps, dgated, o, zgate), reduce=reduce, vmem_mib=56)


def _rms_bwd(xv, dh_gain_sum):
    r = lax.rsqrt(jnp.mean(xv * xv, axis=-1, keepdims=True) + EPS)
    xhat = xv * r
    dx = r * (dh_gain_sum - xhat * jnp.mean(dh_gain_sum * xhat, axis=-1, keepdims=True))
    return dx, xhat


def _b_in_bwd(dq, dk, dv, dz, q_raw, k_raw, x, dx_mid, q_gain_t, k_gain_t, b_gain, kv_gain, w_in, w_kv):
    S = x.shape[0]
    tm = ROW_TILE

    def body(dq_ref, dk_ref, dv_ref, dz_ref, qr_ref, kr_ref, x_ref, dxm_ref, qg_ref, kg_ref, bg_ref, kvg_ref,
             win_ref, wkv_ref, dqz_ref, dkv_ref, dx_ref, small_ref):
        @pl.when(pl.program_id(0) == 0)
        def _():
            small_ref[...] = jnp.zeros_like(small_ref)

        bd = _head_mean_matrix()

        def head_norm_bwd(dy_ref, raw_ref, gain, scale):
            raw = raw_ref[...].astype(F32)
            rr = lax.rsqrt(_head_mean(raw * raw, bd) + EPS)
            xhat = raw * rr
            dy = dy_ref[...].astype(F32) * scale
            gdy = dy * gain
            draw = rr * (gdy - xhat * _head_mean(gdy * xhat, bd))
            return draw.astype(BF16), jnp.sum(dy * xhat, axis=0, keepdims=True)

        dqr, dqg = head_norm_bwd(dq_ref, qr_ref, qg_ref[...], SB_SCALE)
        dkr, dkg = head_norm_bwd(dk_ref, kr_ref, kg_ref[...], 1.0)
        dqz_ref[:, :D_MODEL] = dqr
        dqz_ref[:, D_MODEL:] = dz_ref[...]
        dkv_ref[:, :D_MODEL] = dkr
        dkv_ref[:, D_MODEL:] = dv_ref[...]
        dhb = jnp.zeros((tm, D_MODEL), F32)
        dhkv = jnp.zeros((tm, D_MODEL), F32)
        for sh in range(N_CHIPS):
            cols = slice(sh * 512, (sh + 1) * 512)
            dhb = dhb + _dot_nt(dqz_ref[:, cols], win_ref[sh])
            dhkv = dhkv + _dot_nt(dkv_ref[:, cols], wkv_ref[sh])
        dx, xhat = _rms_bwd(x_ref[...], dhb * bg_ref[...] + dhkv * kvg_ref[...])
        dx_ref[...] = dxm_ref[...] + dx
        small_ref[0:1, :] += dqg
        small_ref[1:2, :] += dkg
        small_ref[2:3, :] += jnp.sum(dhb * xhat, axis=0, keepdims=True)
        small_ref[3:4, :] += jnp.sum(dhkv * xhat, axis=0, keepdims=True)

    row = pl.BlockSpec((tm, D_MODEL), lambda i: (i, 0))
    wide = pl.BlockSpec((tm, 2 * D_MODEL), lambda i: (i, 0))
    vec = pl.BlockSpec((1, D_MODEL), lambda i: (0, 0))
    wsp = pl.BlockSpec((N_CHIPS, D_MODEL, 512), lambda i: (0, 0, 0))
    return pl.pallas_call(
        body, name="b_in_bwd", grid=(S // tm,),
        in_specs=[row] * 8 + [vec] * 4 + [wsp, wsp],
        out_specs=[wide, wide, row, pl.BlockSpec((8, D_MODEL), lambda i: (0, 0))],
        out_shape=[jax.ShapeDtypeStruct((S, 2 * D_MODEL), BF16), jax.ShapeDtypeStruct((S, 2 * D_MODEL), BF16),
                   jax.ShapeDtypeStruct((S, D_MODEL), F32), jax.ShapeDtypeStruct((8, D_MODEL), F32)],
        compiler_params=_params(("arbitrary",), 56),
    )(dq, dk, dv, dz, q_raw, k_raw, x, dx_mid, q_gain_t, k_gain_t, b_gain, kv_gain, w_in, w_kv)


def _a_mix_bwd(dgated, uz, pooled, wg, scale, w_in, x, dx_mid, gain, reduce=None):
    S = x.shape[0]
    tm = ROW_TILE
    n = S // tm

    def body(dg_ref, z_ref, p_ref, wg_ref, sc_ref, win_ref, x_ref, dxm_ref, gn_ref,
             duz_ref, dmr_ref, dx_ref, small_ref, halo_hi, halo_lo):
        i = pl.program_id(0)

        @pl.when(i == 0)
        def _():
            small_ref[...] = jnp.zeros_like(small_ref)
            halo_hi[...] = jnp.zeros_like(halo_hi)
            halo_lo[...] = jnp.zeros_like(halo_lo)

        first_row = (n - 1 - i) * tm
        row = lax.broadcasted_iota(jnp.int32, (tm, tm), 0)
        col = lax.broadcasted_iota(jnp.int32, (tm, tm), 1)
        d = col - row
        for g, w in enumerate(POOL_WINDOWS):
            cols = slice(g * GROUP_DIM, (g + 1) * GROUP_DIM)
            wgg = _group_weight(wg_ref, g)
            sc = sc_ref[:, cols]
            mraw = _dot(p_ref[:, cols], wgg)
            z = z_ref[:, cols]
            sg = _sigmoid(z)
            dga = dg_ref[:, cols].astype(F32)
            dm = dga * (z * sg)
            duz_ref[:, D_MODEL + g * GROUP_DIM:D_MODEL + (g + 1) * GROUP_DIM] = (
                dga * (mraw * sc) * _dsilu(z, sg)).astype(BF16)
            small_ref[0:1, cols] += jnp.sum(dm * mraw, axis=0, keepdims=True)
            dmr = (dm * sc).astype(BF16)
            dmr_ref[:, cols] = dmr
            dp = _dot_nt(dmr, wgg)
            hi, lo = _hilo(dp * _inv_count(first_row, tm, w))
            t_main = _mask_bf16((d >= 0) & (d < w))
            t_halo = _mask_bf16(d + tm < w)
            du = (_dot(t_main, hi) + _dot(t_main, lo) + _dot(t_halo, halo_hi[:, cols]) + _dot(t_halo, halo_lo[:, cols])
                  - dp)
            halo_hi[:, cols] = hi
            halo_lo[:, cols] = lo
            duz_ref[:, cols] = du.astype(BF16)
        dh = jnp.zeros((tm, D_MODEL), F32)
        for sh in range(N_CHIPS):
            dh = dh + _dot_nt(duz_ref[:, sh * 512:(sh + 1) * 512], win_ref[sh])
        dx, xhat = _rms_bwd(x_ref[...], dh * gn_ref[...])
        dx_ref[...] = dxm_ref[...] + dx
        small_ref[1:2, :] += jnp.sum(dh * xhat, axis=0, keepdims=True)

    rev = lambda i: (n - 1 - i, 0)
    row = pl.BlockSpec((tm, D_MODEL), rev)
    vec = pl.BlockSpec((1, D_MODEL), lambda i: (0, 0))
    return _call_with_gather(
        body, name="a_mix_bwd", grid=(n,),
        in_specs=[row,
                  pl.BlockSpec((tm, D_MODEL), lambda i: (n - 1 - i, 1)),
                  row,
                  pl.BlockSpec((N_CHIPS, N_GROUPS, 64, GROUP_DIM), lambda i: (0, 0, 0, 0)),
                  vec,
                  pl.BlockSpec((N_CHIPS, D_MODEL, 512), lambda i: (0, 0, 0)),
                  row, row, vec],
        out_specs=[pl.BlockSpec((tm, 2 * D_MODEL), rev), row, row,
                   pl.BlockSpec((8, D_MODEL), lambda i: (0, 0))],
        out_shape=[jax.ShapeDtypeStruct((S, 2 * D_MODEL), BF16), jax.ShapeDtypeStruct((S, D_MODEL), BF16),
                   jax.ShapeDtypeStruct((S, D_MODEL), F32), jax.ShapeDtypeStruct((8, D_MODEL), F32)],
        scratch_shapes=[pltpu.VMEM((tm, D_MODEL), BF16), pltpu.VMEM((tm, D_MODEL), BF16)],
        args=(dgated, uz, pooled, wg, scale, w_in, x, dx_mid, gain), reduce=reduce, vmem_mib=56)


def _wgrad(name, a, dy, n_shards, a_spec=None, k_dim=None):
    S, n_cols = dy.shape
    ts = WGRAD_SEQ_TILE
    k_dim = a.shape[-1] if k_dim is None else k_dim
    wn = n_cols // n_shards
    tk = min(k_dim, WGRAD_ACC_BYTES // (4 * n_cols))
    nst = S // ts

    def body(a_ref, dy_ref, out_ref, acc):
        st = pl.program_id(1)

        @pl.when(st == 0)
        def _():
            acc[...] = jnp.zeros_like(acc)

        acc[...] += _dot_tn(a_ref[...].astype(BF16), dy_ref[...].astype(BF16))

        @pl.when(st == nst - 1)
        def _():
            for sh in range(n_shards):
                out_ref[sh] = acc[:, sh * wn:(sh + 1) * wn]

    if a_spec is None:
        a_spec = pl.BlockSpec((ts, tk), lambda kt, st: (st, kt))
    return pl.pallas_call(
        body, name=name, grid=(k_dim // tk, nst),
        in_specs=[a_spec, pl.BlockSpec((ts, n_cols), lambda kt, st: (st, 0))],
        out_specs=pl.BlockSpec((n_shards, tk, wn), lambda kt, st: (0, kt, 0)),
        out_shape=jax.ShapeDtypeStruct((n_shards, k_dim, wn), F32),
        scratch_shapes=[pltpu.VMEM((tk, n_cols), F32)],
        compiler_params=_params(("parallel", "arbitrary")),
    )(a, dy)


def _wgrad_ple(name, p, layer, de):
    ts = WGRAD_SEQ_TILE
    spec = pl.BlockSpec((None, None, ts, PLE_DIM), lambda kt, st: (layer, 0, st, 0))
    return _wgrad(name, p, de, N_CHIPS, a_spec=spec, k_dim=PLE_DIM)


def _wgrad_group(pooled, dmr):
    S = pooled.shape[0]
    ts = WGRAD_SEQ_TILE
    nst = S // ts

    def body(p_ref, d_ref, out_ref, acc):
        st = pl.program_id(1)

        @pl.when(st == 0)
        def _():
            acc[...] = jnp.zeros_like(acc)

        acc[...] += _dot_tn(p_ref[...], d_ref[...])

        @pl.when(st == nst - 1)
        def _():
            for sh in range(N_CHIPS):
                out_ref[sh] = acc[sh * 64:(sh + 1) * 64, :]

    blk = pl.BlockSpec((ts, GROUP_DIM), lambda g, st: (st, g))
    return pl.pallas_call(
        body, name="wgrad_group", grid=(N_GROUPS, nst),
        in_specs=[blk, blk],
        out_specs=pl.BlockSpec((N_CHIPS, None, 64, GROUP_DIM), lambda g, st: (0, g, 0, 0)),
        out_shape=jax.ShapeDtypeStruct((N_CHIPS, N_GROUPS, 64, GROUP_DIM), F32),
        scratch_shapes=[pltpu.VMEM((GROUP_DIM, GROUP_DIM), F32)],
        compiler_params=_params(("parallel", "arbitrary")),
    )(pooled, dmr)


GATHER_AT = {
    "a_in": ("a_w_group", "a_w_out", "ple_w0", "ple_gate_w0"),
    "a_mix": ("w_kv",),
    "a_out_ple": ("b_w_in",),
    "attn_fwd": ("b_w_out", "ple_w1", "ple_gate_w1"),
}


REDUCE_AT = {
    "attn_bwd": ("b_w_out", "ple_w1", "ple_gate_w1"),
    "a_mix_bwd": ("w_kv", "b_w_in"),
}


def _local_step(x, p, target, w, local=None, state=None):
    w = dict(w)

    def run(fn, host, n_out, *args, **kwargs):
        names = GATHER_AT[host] if local is not None else ()
        res = fn(*args, gather=[local[n] for n in names], **kwargs)
        w.update(zip(names, res[n_out:]))
        return res[:n_out]

    k_gain_t = jnp.tile(w["k_norm"].reshape(1, HEAD_DIM), (1, N_HEADS))
    q_gain_t = jnp.tile(w["b_q_norm"].reshape(1, HEAD_DIM), (1, N_HEADS))

    uz, h_a = run(_a_in, "a_in", 2, x, w["a_norm"], w["a_w_in"])
    wg4 = w["a_w_group"].reshape(N_CHIPS, N_GROUPS, 64, GROUP_DIM)
    wa_out = w["a_w_out"].reshape(D_MODEL, D_MODEL)
    gated_a, pooled = run(_a_mix, "a_mix", 2, uz, wg4, w["a_scale"])
    x1, x2, e_a, gate_a = run(_out_ple, "a_out_ple", 4, "a_out_ple", gated_a, x, wa_out, p, 0,
                              w["ple_w0"], w["ple_gate_w0"])
    h_kv, h_b, k_raw, q_raw, k, q, v, z_b = _b_in(
        x2, w["kv_norm"], w["b_norm"], k_gain_t, q_gain_t, w["w_kv"], w["b_w_in"])
    o, gated_b, ltot, att_steps = run(_attn_fwd, "attn_fwd", 4, q, k, v, z_b)
    wb_out = w["b_w_out"].reshape(D_MODEL, D_MODEL)
    x3, dx4, e_b, gate_b, loss_blk = _out_ple("b_out_ple", gated_b, x2, wb_out, p, 1, w["ple_w1"], w["ple_gate_w1"],
                                              target=target)

    grads, updates = {}, {}

    def hosted(fn, host, n_out, *args):
        if state is None:
            return fn(*args)
        names = REDUCE_AT[host]
        res = fn(*args, reduce=([grads.pop(n) for n in names], *[[t[n] for n in names] for t in state[:3]],
                                [state[3][n] for n in names]))
        for i, n in enumerate(names):
            updates[n] = tuple(group[i] for group in res[n_out:])
        return res[:n_out]

    de_b, dgp_b, dx3, dgated_b = _ple_out_bwd("b_ple_out_bwd", dx4, e_b, gate_b, w["ple_gate_w1"], wb_out)
    grads["b_w_out"] = _wgrad("wgrad_b_out", gated_b, dx3, 1).reshape(N_CHIPS, 256, D_MODEL)
    grads["ple_w1"] = _wgrad_ple("wgrad_ple1", p, 1, de_b)
    grads["ple_gate_w1"] = _wgrad("wgrad_gate1", x3, dgp_b, 1).reshape(N_CHIPS, 256, D_MODEL)
    dq, dk, dv, dz_b = hosted(_attn_bwd, "attn_bwd", 4, q, k, v, ltot, att_steps, dgated_b, o, z_b)
    dqz, dkv, dx2, small_b = _b_in_bwd(dq, dk, dv, dz_b, q_raw, k_raw, x2, dx3, q_gain_t, k_gain_t,
                                       w["b_norm"], w["kv_norm"], w["b_w_in"], w["w_kv"])
    grads["w_kv"] = _wgrad("wgrad_kv", h_kv, dkv, N_CHIPS)
    grads["b_w_in"] = _wgrad("wgrad_b_in", h_b, dqz, N_CHIPS)
    de_a, dgp_a, dx1, dgated_a = _ple_out_bwd("a_ple_out_bwd", dx2, e_a, gate_a, w["ple_gate_w0"], wa_out)
    grads["a_w_out"] = _wgrad("wgrad_a_out", gated_a, dx1, 1).reshape(N_CHIPS, 256, D_MODEL)
    grads["ple_w0"] = _wgrad_ple("wgrad_ple0", p, 0, de_a)
    grads["ple_gate_w0"] = _wgrad("wgrad_gate0", x1, dgp_a, 1).reshape(N_CHIPS, 256, D_MODEL)
    duz, dmr, grad_x, small_a = hosted(_a_mix_bwd, "a_mix_bwd", 4, dgated_a, uz, pooled, wg4, w["a_scale"],
                                       w["a_w_in"], x, dx1, w["a_norm"])
    grads["a_w_in"] = _wgrad("wgrad_a_in", h_a, duz, N_CHIPS)
    grads["a_w_group"] = _wgrad_group(pooled, dmr).reshape(N_CHIPS, N_GROUPS * 64, GROUP_DIM)

    fold = lambda row: jnp.pad(row.reshape(N_HEADS, HEAD_DIM).sum(axis=0), (0, D_MODEL - HEAD_DIM))
    small = jnp.stack([small_a[1], small_a[0], small_b[3], small_b[2], fold(small_b[1]), fold(small_b[0]),
                       jnp.pad(loss_blk[0], (0, D_MODEL - loss_blk.shape[1])), jnp.zeros((D_MODEL,), F32)])
    return grad_x, grads, updates, small


def _mesh_place():
    x, y, c = lax.axis_index("x"), lax.axis_index("y"), lax.axis_index("c")
    other_chips = [(1 - x, y), (x, 1 - y), (1 - x, 1 - y)]
    return x, y, c, other_chips


def _gather_sems(n):
    return [pltpu.SemaphoreType.DMA((3 * n,)), pltpu.SemaphoreType.DMA((3 * n,)),
            pltpu.SemaphoreType.DMA((3 * n,)), pltpu.SemaphoreType.DMA((3 * n,)), pltpu.SemaphoreType.DMA((n,))]


def _gather_copies(srcs, outs, sems):
    send_far, recv_far, send_sib, recv_sib, local_sem = sems
    n = len(srcs)
    x, y, c, chips = _mesh_place()
    me = 2 * x + y
    sibling = (x, y, 1 - c)

    def half(k, which):
        rows = srcs[k].shape[0] // 2
        return pl.ds(pl.multiple_of(which * rows, 16), rows)

    local = [pltpu.make_async_copy(srcs[k], outs[k].at[me], local_sem.at[k]) for k in range(n)]
    far = [pltpu.make_async_remote_copy(
        src_ref=srcs[k].at[half(k, c)], dst_ref=outs[k].at[me, half(k, c)],
        send_sem=send_far.at[j * n + k], recv_sem=recv_far.at[j * n + k], device_id=(px, py, c), device_id_type=MESH)
        for j, (px, py) in enumerate(chips) for k in range(n)]

    def landed(j, k, which, from_far):
        px, py = chips[j]
        piece = outs[k].at[2 * px + py, half(k, which)]
        send, recv = (send_far, recv_far) if from_far else (send_sib, recv_sib)
        return pltpu.make_async_remote_copy(src_ref=piece, dst_ref=piece, send_sem=send.at[j * n + k],
                                            recv_sem=recv.at[j * n + k], device_id=sibling, device_id_type=MESH)

    return local, far, landed, c


def _gather_start(srcs, outs, sems):
    local, far, _, _ = _gather_copies(srcs, outs, sems)
    for cp in local + far:
        cp.start()


def _gather_pass_on(srcs, outs, sems):
    _, _, landed, c = _gather_copies(srcs, outs, sems)
    for j in range(3):
        for k in range(len(srcs)):
            landed(j, k, c, True).wait_recv()
            landed(j, k, c, False).start()


def _gather_finish(srcs, outs, sems):
    local, far, landed, c = _gather_copies(srcs, outs, sems)
    pairs = [(j, k) for j in range(3) for k in range(len(srcs))]
    for j, k in pairs:
        landed(j, k, 1 - c, False).wait_recv()
    for cp in far + [landed(j, k, c, False) for j, k in pairs]:
        cp.wait_send()
    for cp in local:
        cp.wait()


def _call_with_gather(body, *, name, grid, in_specs, out_specs, out_shape, args, gather=(), reduce=None,
                      scratch_shapes=(), vmem_mib=48):
    n_in, n_out, n_scr, n_g = len(args), len(out_shape), len(scratch_shapes), len(gather)
    n_r = len(reduce[0]) if reduce else 0
    pieces = _reduce_pieces(reduce[0], reduce[4]) if reduce else []
    reduce_args = [a for group in reduce[:4] for a in group] if reduce else []
    gather_sems = _gather_sems(n_g) if n_g else []
    n_steps = 1
    for g in grid:
        n_steps *= g

    def wrapped(*refs):
        refs = list(refs)
        take = lambda count: [refs.pop(0) for _ in range(count)]
        ins, g_in, r_in = take(n_in), take(n_g), take(4 * n_r)
        outs, g_out, r_out = take(n_out), take(n_g), take(4 * n_r)
        scratch, sems, r_scratch = take(n_scr), take(len(gather_sems)), refs
        step = 0
        for axis, g in enumerate(grid):
            step = step * g + pl.program_id(axis)
        if n_g:
            @pl.when(step == 0)
            def _():
                _gather_start(g_in, g_out, sems)

        if n_r:
            ticks, drain = _reduce_ticks(pieces, n_r, (*r_in, *r_out, *r_scratch))
            for t, tick in enumerate(ticks[:n_steps]):
                pl.when(step == t)(tick)

        body(*ins, *outs, *scratch)
        if n_r:
            for tick in ticks[n_steps:]:
                pl.when(step == n_steps - 1)(tick)
            pl.when(step == n_steps - 1)(drain)
        if n_g:
            @pl.when(step == max(n_steps - 2, 0))
            def _():
                _gather_pass_on(g_in, g_out, sems)

            @pl.when(step == n_steps - 1)
            def _():
                _gather_finish(g_in, g_out, sems)

    hbm = pl.BlockSpec(memory_space=pltpu.HBM)
    res = pl.pallas_call(
        wrapped, name=name, grid=grid,
        in_specs=list(in_specs) + [hbm] * (n_g + 4 * n_r), out_specs=list(out_specs) + [hbm] * (n_g + 4 * n_r),
        out_shape=list(out_shape) + [jax.ShapeDtypeStruct((N_CHIPS,) + g.shape, BF16) for g in gather]
        + ([jax.ShapeDtypeStruct(w.shape, F32) for _ in range(4) for w in reduce[1]] if reduce else []),
        scratch_shapes=list(scratch_shapes) + gather_sems + (_reduce_scratch() if reduce else []),
        compiler_params=_params(("arbitrary",) * len(grid), vmem_mib),
    )(*args, *gather, *reduce_args)
    if not reduce:
        return res
    plain = list(res[:n_out + n_g])
    return plain + [res[n_out + n_g + i * n_r:n_out + n_g + (i + 1) * n_r] for i in range(4)]


def _allgather_weights(shards, small, casts):
    n = len(shards)
    cast_out = [(k, r0, r1) for k, (_, ranges) in enumerate(casts) for r0, r1 in ranges]
    n_c, n_co = len(casts), len(cast_out)

    def body(*refs):
        ins, small_in, cast_in = refs[:n], refs[n], refs[n + 1:n + 1 + n_c]
        refs = refs[n + 1 + n_c:]
        outs, small_out, cast_dst = refs[:n], refs[n], refs[n + 1:n + 1 + n_co]
        refs = refs[n + 1 + n_co:]
        cast, cast_buf = refs[:n], refs[n:n + n_co]
        send_far, recv_far, send_sib, recv_sib, send_small, recv_small, local_sem, cast_sem = refs[n + n_co:]
        x, y, c, chips = _mesh_place()
        me = 2 * x + y
        sibling = (x, y, 1 - c)

        def half(k, which):
            rows = ins[k].shape[0] // 2
            return pl.ds(pl.multiple_of(which * rows, 16), rows)

        local = []
        for k in range(n):
            cast[k][...] = ins[k][...].astype(BF16)
            local.append(pltpu.make_async_copy(cast[k], outs[k].at[me], local_sem.at[k]))
            local[-1].start()
        local.append(pltpu.make_async_copy(small_in, small_out.at[me], local_sem.at[n]))
        local[-1].start()

        sends = []
        for j, (px, py) in enumerate(chips):
            for k in range(n):
                cp = pltpu.make_async_remote_copy(
                    src_ref=cast[k].at[half(k, c)], dst_ref=outs[k].at[me, half(k, c)],
                    send_sem=send_far.at[j * n + k], recv_sem=recv_far.at[j * n + k],
                    device_id=(px, py, c), device_id_type=MESH)
                cp.start()
                sends.append(cp)
            cp = pltpu.make_async_remote_copy(
                src_ref=small_in, dst_ref=small_out.at[me], send_sem=send_small.at[j], recv_sem=recv_small.at[j],
                device_id=(px, py, c), device_id_type=MESH)
            cp.start()
            sends.append(cp)

        for i, (k, r0, r1) in enumerate(cast_out):
            cast_buf[i][...] = cast_in[k][r0:r1, :].astype(BF16)
            local.append(pltpu.make_async_copy(cast_buf[i], cast_dst[i], cast_sem.at[i]))
            local[-1].start()

        def landed(j, k, which, sems_s, sems_r, device):
            px, py = chips[j]
            piece = outs[k].at[2 * px + py, half(k, which)]
            return pltpu.make_async_remote_copy(
                src_ref=piece, dst_ref=piece, send_sem=sems_s.at[j * n + k], recv_sem=sems_r.at[j * n + k],
                device_id=device, device_id_type=MESH)

        for j in range(len(chips)):
            for k in range(n):
                landed(j, k, c, send_far, recv_far, sibling).wait_recv()
                cp = landed(j, k, c, send_sib, recv_sib, sibling)
                cp.start()
                sends.append(cp)
        for j, (px, py) in enumerate(chips):
            for k in range(n):
                landed(j, k, 1 - c, send_sib, recv_sib, sibling).wait_recv()
            pltpu.make_async_remote_copy(
                src_ref=small_in, dst_ref=small_out.at[2 * px + py], send_sem=send_small.at[j],
                recv_sem=recv_small.at[j], device_id=(px, py, c), device_id_type=MESH).wait_recv()
        for cp in sends:
            cp.wait_send()
        for cp in local:
            cp.wait()

    vmem = pl.BlockSpec(memory_space=pltpu.VMEM)
    hbm = pl.BlockSpec(memory_space=pltpu.HBM)
    cast_shapes = [(r1 - r0, casts[k][0].shape[1]) for k, r0, r1 in cast_out]
    res = pl.pallas_call(
        body, name="allgather_weights",
        in_specs=[vmem] * (n + 1 + n_c), out_specs=[hbm] * (n + 1 + n_co),
        out_shape=[jax.ShapeDtypeStruct((N_CHIPS,) + s.shape, BF16) for s in shards]
        + [jax.ShapeDtypeStruct((N_CHIPS,) + small.shape, F32)]
        + [jax.ShapeDtypeStruct(s, BF16) for s in cast_shapes],
        scratch_shapes=[pltpu.VMEM(s.shape, BF16) for s in shards] + [pltpu.VMEM(s, BF16) for s in cast_shapes]
        + [pltpu.SemaphoreType.DMA((3 * n,)), pltpu.SemaphoreType.DMA((3 * n,)),
           pltpu.SemaphoreType.DMA((3 * n,)), pltpu.SemaphoreType.DMA((3 * n,)),
           pltpu.SemaphoreType.DMA((3,)), pltpu.SemaphoreType.DMA((3,)),
           pltpu.SemaphoreType.DMA((n + 1,)), pltpu.SemaphoreType.DMA((n_co,))],
        compiler_params=_params(None, 40),
    )(*shards, small, *[a for a, _ in casts])
    return res[:n], res[n], res[n + 1:]


def _adamw(w, g, m, v):
    m = ADAM_B1 * m + (1.0 - ADAM_B1) * g
    v = ADAM_B2 * v + (1.0 - ADAM_B2) * (g * g)
    m_hat = m / (1.0 - ADAM_B1 ** ADAM_STEP)
    v_hat = v / (1.0 - ADAM_B2 ** ADAM_STEP)
    delta = -ADAM_LR * (m_hat / (jnp.sqrt(v_hat) + ADAM_EPS) + ADAM_WD * w)
    return delta, m, v


RS_PIECE_ROWS = 128
RS_PIECE_COLS = 512


def _reduce_adam_all(grads, ws, ms, vs, small, bases=None, seeds=None):
    n_w = len(grads)
    pieces = _reduce_pieces(grads, bases)
    n_small = len(_small_sum_scratch(small.shape))
    seeds = seeds or [None] * n_w
    seeded = [(k, a) for k, seed in enumerate(seeds) if seed is not None for a in range(4)]
    n_in = 4 * n_w + 1

    def body(*refs):
        refs = list(refs)
        del refs[n_in:n_in + len(seeded)]
        small_in = refs.pop(4 * n_w)
        small_out = refs.pop(8 * n_w)
        small_scratch = [refs.pop() for _ in range(n_small)][::-1]
        sends = _small_sum_start(small_in, *small_scratch)
        ticks, drain = _reduce_ticks(pieces, n_w, refs)
        for tick in ticks:
            tick()
        drain()
        _small_sum_finish(sends, small_scratch[0], small_out)

    hbm = pl.BlockSpec(memory_space=pltpu.HBM)
    vmem = pl.BlockSpec(memory_space=pltpu.VMEM)
    outs = pl.pallas_call(
        body, name="reduce_adam_all",
        in_specs=[hbm] * (4 * n_w) + [vmem] + [hbm] * len(seeded), out_specs=[hbm] * (4 * n_w) + [vmem],
        out_shape=[jax.ShapeDtypeStruct(w.shape, F32) for _ in range(4) for w in ws]
        + [jax.ShapeDtypeStruct(small.shape, F32)],
        input_output_aliases={n_in + i: a * n_w + k for i, (k, a) in enumerate(seeded)},
        scratch_shapes=_reduce_scratch() + _small_sum_scratch(small.shape),
        compiler_params=_params(None, 48),
    )(*grads, *ws, *ms, *vs, small, *[seeds[k][a] for k, a in seeded])
    return [outs[i * n_w:(i + 1) * n_w] for i in range(4)], outs[4 * n_w]


def _reduce_pieces(grads, bases=None):
    pieces = []
    for k, g in enumerate(grads):
        hr, cols = g.shape[1] // 2, g.shape[2]
        pr, pc = min(hr, RS_PIECE_ROWS), min(cols, RS_PIECE_COLS)
        base = bases[k] if bases else 0
        pieces += [(k, ro, hr, co, pr, pc, base) for ro in range(0, hr, pr) for co in range(0, cols, pc)]
    return pieces


def _reduce_scratch():
    P, C = RS_PIECE_ROWS, RS_PIECE_COLS
    return [
        pltpu.VMEM((3, N_CHIPS, P, C), F32), pltpu.VMEM((3, N_CHIPS, P, C), F32),
        pltpu.VMEM((2, N_CHIPS, P, C), BF16), pltpu.VMEM((2, N_CHIPS, P, C), BF16),
        pltpu.VMEM((2, N_CHIPS, P, C), F32),
        pltpu.VMEM((2, 3, P, C), BF16), pltpu.VMEM((2, 3, P, C), BF16),
        pltpu.VMEM((2, 2, P, C), F32),
        pltpu.VMEM((2, 3, 2, P, C), F32), pltpu.VMEM((2, 4, 2, P, C), F32),
        pltpu.SemaphoreType.DMA((3, 2)), pltpu.SemaphoreType.DMA((2, 3, 2)),
        pltpu.SemaphoreType.DMA((2,)), pltpu.SemaphoreType.DMA((2,)),
        pltpu.SemaphoreType.DMA((2, 3)), pltpu.SemaphoreType.DMA((2, 3)),
        pltpu.SemaphoreType.DMA((2,)), pltpu.SemaphoreType.DMA((2,)),
        pltpu.SemaphoreType.DMA((2, 4, 2))]


def _reduce_ticks(pieces, n_w, refs):
    n = len(pieces)

    def build(*refs):
        g_in, w_in, m_in, v_in = (refs[i * n_w:(i + 1) * n_w] for i in range(4))
        g_out, d_out, m_out, v_out = (refs[(4 + i) * n_w:(5 + i) * n_w] for i in range(4))
        (gm, go, sb1, rb1, part, sb2, rb2, fin, wmv, outs,
         ld_sem, wmv_sem, s1_send, s1_recv, s2_send, s2_recv, s3_send, s3_recv, out_sem) = refs[8 * n_w:]
        x, y, c, chips = _mesh_place()
        me = 2 * x + y
        sibling = (x, y, 1 - c)

        def at_hbm(i, which, in_shard):
            _, ro, hr, co, pr, pc, base = pieces[i]
            half = c if which == 0 else 1 - c
            return pl.ds(pl.multiple_of((base if in_shard else 0) + half * hr + ro, 64), pr), pl.ds(co, pc)

        def win(i):
            return pl.ds(0, pieces[i][4]), pl.ds(0, pieces[i][5])

        every = slice(None)

        def loads(i):
            k, s = pieces[i][0], i % 3
            return [pltpu.make_async_copy(g_in[k].at[(every,) + at_hbm(i, h, False)], buf.at[(s, every) + win(i)],
                                          ld_sem.at[s, h])
                    for h, buf in enumerate((gm, go))]

        def wmv_loads(i):
            k, s = pieces[i][0], i % 2
            return [pltpu.make_async_copy(src[k].at[at_hbm(i, h, True)], wmv.at[(s, a, h) + win(i)], wmv_sem.at[s, a, h])
                    for a, src in enumerate((w_in, m_in, v_in)) for h in range(2)]

        def stores(i):
            k, s = pieces[i][0], i % 2
            return [pltpu.make_async_copy(outs.at[(s, a, h) + win(i)], dst[k].at[at_hbm(i, h, True)], out_sem.at[s, a, h])
                    for a, dst in enumerate((g_out, d_out, m_out, v_out)) for h in range(2)]

        def swap1(i):
            s = i % 2
            return pltpu.make_async_remote_copy(
                src_ref=sb1.at[(s, every) + win(i)], dst_ref=rb1.at[(s, every) + win(i)],
                send_sem=s1_send.at[s], recv_sem=s1_recv.at[s], device_id=sibling, device_id_type=MESH)

        def far2(i, j):
            s = i % 2
            px, py = chips[j]
            return pltpu.make_async_remote_copy(
                src_ref=sb2.at[(s, j) + win(i)], dst_ref=rb2.at[(s, j) + win(i)],
                send_sem=s2_send.at[s, j], recv_sem=s2_recv.at[s, j], device_id=(px, py, c), device_id_type=MESH)

        def swap3(i):
            s = i % 2
            return pltpu.make_async_remote_copy(
                src_ref=fin.at[(s, 0) + win(i)], dst_ref=fin.at[(s, 1) + win(i)],
                send_sem=s3_send.at[s], recv_sem=s3_recv.at[s], device_id=sibling, device_id_type=MESH)

        def stage0(i):
            for cp in loads(i):
                cp.start()

        def stage1(i):
            s, s3 = i % 2, i % 3
            for cp in loads(i):
                cp.wait()
            sb1[(s, every) + win(i)] = go[(s3, every) + win(i)].astype(BF16)
            swap1(i).start()

        def stage2(i):
            s, s3 = i % 2, i % 3
            swap1(i).wait()
            part[(s, every) + win(i)] = gm[(s3, every) + win(i)] + rb1[(s, every) + win(i)].astype(F32)
            for j, (px, py) in enumerate(chips):
                sb2[(s, j) + win(i)] = part[(s, 2 * px + py) + win(i)].astype(BF16)
                far2(i, j).start()

        def stage3(i):
            s = i % 2
            total = part[(s, me) + win(i)]
            for j in range(3):
                far2(i, j).wait()
                total = total + rb2[(s, j) + win(i)].astype(F32)
            fin[(s, 0) + win(i)] = total
            swap3(i).start()
            for cp in wmv_loads(i):
                cp.start()

        def stage4(i):
            s = i % 2
            if i >= 2:
                for cp in stores(i - 2):
                    cp.wait()
            swap3(i).wait()
            for cp in wmv_loads(i):
                cp.wait()
            both = (every,) + win(i)
            g = fin[(s,) + both]
            delta, m_new, v_new = _adamw(wmv[(s, 0) + both], g, wmv[(s, 1) + both], wmv[(s, 2) + both])
            outs[(s, 0) + both] = g
            outs[(s, 1) + both] = delta
            outs[(s, 2) + both] = m_new
            outs[(s, 3) + both] = v_new
            for cp in stores(i):
                cp.start()

        stages = (stage0, stage1, stage2, stage3, stage4)

        def tick(t):
            for age in reversed(range(len(stages))):
                if 0 <= t - age < n:
                    stages[age](t - age)

        def drain():
            for i in range(max(0, n - 2), n):
                for cp in stores(i):
                    cp.wait()

        return [functools.partial(tick, t) for t in range(n + len(stages) - 1)], drain

    return build(*refs)


N_DEVICES = 8


def _small_sum_scratch(shape):
    return [pltpu.VMEM((N_DEVICES,) + shape, F32),
            pltpu.SemaphoreType.DMA((N_DEVICES - 1,)), pltpu.SemaphoreType.DMA((N_DEVICES - 1,))]


def _small_sum_start(part_ref, buf, send_sem, recv_sem):
    x, y, c, _ = _mesh_place()
    me = 4 * x + 2 * y + c
    buf[me] = part_ref[...]
    sends = []
    for k in range(1, N_DEVICES):
        peer = ((1 - x) if k & 4 else x, (1 - y) if k & 2 else y, (1 - c) if k & 1 else c)
        cp = pltpu.make_async_remote_copy(src_ref=part_ref, dst_ref=buf.at[me], send_sem=send_sem.at[k - 1],
                                          recv_sem=recv_sem.at[k - 1], device_id=peer, device_id_type=MESH)
        cp.start()
        sends.append(cp)
    return sends


def _small_sum_finish(sends, buf, out_ref):
    for cp in sends:
        cp.wait_recv()
    total = buf[0]
    for s in range(1, N_DEVICES):
        total = total + buf[s]
    out_ref[...] = total
    for cp in sends:
        cp.wait_send()


def _adam_small(w, g, m, v):
    def body(w_ref, g_ref, m_ref, v_ref, d_ref, mo_ref, vo_ref):
        delta, m_new, v_new = _adamw(w_ref[...], g_ref[...], m_ref[...], v_ref[...])
        d_ref[...] = delta
        mo_ref[...] = m_new
        vo_ref[...] = v_new

    vmem = pl.BlockSpec(memory_space=pltpu.VMEM)
    return pl.pallas_call(
        body, name="adam_small", in_specs=[vmem] * 4, out_specs=[vmem] * 3,
        out_shape=[jax.ShapeDtypeStruct(w.shape, F32)] * 3,
    )(w, g, m, v)


BIG = ("a_w_in", "a_w_group", "a_w_out", "w_kv", "b_w_in", "b_w_out", "ple_w", "ple_gate_w")
SMALL = ("a_norm", "a_scale", "kv_norm", "b_norm", "k_norm", "b_q_norm")
SMALL_SHARDED = ("a_norm", "a_scale")
WEIGHTS = ("a_norm", "a_w_in", "a_w_group", "a_scale", "a_w_out", "kv_norm", "w_kv", "k_norm", "b_norm", "b_w_in",
           "b_q_norm", "b_w_out", "ple_w", "ple_gate_w")


def _as_matrix(a):
    return a.reshape(-1, a.shape[-1])


def _pack_small(arrs):
    rows = [jnp.pad(a.reshape(-1), (0, D_MODEL - a.size)) for a in arrs]
    rows += [jnp.zeros((D_MODEL,), F32)] * (8 - len(rows))
    return jnp.stack(rows)


def kernel(x, p, a_norm, a_w_in, a_w_group, a_scale, a_w_out, kv_norm, w_kv, k_norm, b_norm, b_w_in, b_q_norm, b_w_out, ple_w, ple_gate_w, loss_target, m_a_norm, m_a_w_in, m_a_w_group, m_a_scale, m_a_w_out, m_kv_norm, m_w_kv, m_k_norm, m_b_norm, m_b_w_in, m_b_q_norm, m_b_w_out, m_ple_w, m_ple_gate_w, v_a_norm, v_a_w_in, v_a_w_group, v_a_scale, v_a_w_out, v_kv_norm, v_w_kv, v_k_norm, v_b_norm, v_b_w_in, v_b_q_norm, v_b_w_out, v_ple_w, v_ple_gate_w):
    wts = dict(a_norm=a_norm, a_w_in=a_w_in, a_w_group=a_w_group, a_scale=a_scale, a_w_out=a_w_out, kv_norm=kv_norm,
               w_kv=w_kv, k_norm=k_norm, b_norm=b_norm, b_w_in=b_w_in, b_q_norm=b_q_norm, b_w_out=b_w_out,
               ple_w=ple_w, ple_gate_w=ple_gate_w)
    mom = dict(a_norm=m_a_norm, a_w_in=m_a_w_in, a_w_group=m_a_w_group, a_scale=m_a_scale, a_w_out=m_a_w_out,
               kv_norm=m_kv_norm, w_kv=m_w_kv, k_norm=m_k_norm, b_norm=m_b_norm, b_w_in=m_b_w_in,
               b_q_norm=m_b_q_norm, b_w_out=m_b_w_out, ple_w=m_ple_w, ple_gate_w=m_ple_gate_w)
    var = dict(a_norm=v_a_norm, a_w_in=v_a_w_in, a_w_group=v_a_w_group, a_scale=v_a_scale, a_w_out=v_a_w_out,
               kv_norm=v_kv_norm, w_kv=v_w_kv, k_norm=v_k_norm, b_norm=v_b_norm, b_w_in=v_b_w_in,
               b_q_norm=v_b_q_norm, b_w_out=v_b_w_out, ple_w=v_ple_w, ple_gate_w=v_ple_gate_w)
    S = x.shape[1]
    chip = 2 * lax.axis_index("x") + lax.axis_index("y")

    sharded_small = jnp.concatenate([a_norm.reshape(1, 256), a_scale.reshape(1, 256), jnp.zeros((6, 256), F32)], axis=0)
    later = ("a_w_group", "a_w_out", "w_kv", "b_w_in", "b_w_out", "ple_w", "ple_gate_w")
    (a_w_in_full,), small_full, copies = _allgather_weights(
        [_as_matrix(a_w_in)], sharded_small,
        [(_as_matrix(wts[n]), [(0, 256), (256, 512)] if n.startswith("ple") else [(0, _as_matrix(wts[n]).shape[0])])
         for n in later])
    local = dict(zip(("a_w_group", "a_w_out", "w_kv", "b_w_in", "b_w_out", "ple_w0", "ple_w1", "ple_gate_w0",
                      "ple_gate_w1"), copies))
    full = dict(a_w_in=a_w_in_full,
                a_norm=small_full[:, 0, :].reshape(1, D_MODEL), a_scale=small_full[:, 1, :].reshape(1, D_MODEL),
                kv_norm=kv_norm.reshape(1, D_MODEL), b_norm=b_norm.reshape(1, D_MODEL), k_norm=k_norm, b_q_norm=b_q_norm)

    def shards(t):
        out = {}
        for n in BIG:
            for entry in ((n + "0", n + "1") if n.startswith("ple") else (n,)):
                out[entry] = _as_matrix(t[n])
        return out

    base = {n: 256 if n.startswith("ple") and n.endswith("1") else 0 for n in shards(wts)}
    state = (shards(wts), shards(mom), shards(var), base)
    grad_x, grads, updates, small_part = _local_step(x.reshape(S, D_MODEL), p, loss_target.reshape(S, D_MODEL),
                                                     full, local, state)

    names = sorted(grads)
    reduced, small_sum = _reduce_adam_all(
        [grads[n] for n in names], *[[t[n] for n in names] for t in state[:3]], small_part,
        bases=[base[n] for n in names], seeds=[updates.get(n[:-1] + "1") if n.startswith("ple") else None for n in names])
    for i, n in enumerate(names):
        updates[n] = tuple(group[i] for group in reduced)
    out_g, out_d, out_m, out_v = {}, {}, {}, {}
    for n in BIG:
        for i, out in enumerate((out_g, out_d, out_m, out_v)):
            out[n] = updates[n + "0" if n.startswith("ple") else n][i].reshape(wts[n].shape)

    loss = small_sum[len(SMALL), 0]
    small_rows = []
    for i, n in enumerate(SMALL):
        row = small_sum[i]
        if n in SMALL_SHARDED:
            row = lax.dynamic_slice(row, (chip * 256,), (256,))
        else:
            row = row[:wts[n].size]
        small_rows.append(row)
    g_small = _pack_small(small_rows)
    d_small, m_small, v_small = _adam_small(_pack_small([wts[n] for n in SMALL]), g_small,
                                            _pack_small([mom[n] for n in SMALL]), _pack_small([var[n] for n in SMALL]))
    for i, n in enumerate(SMALL):
        shape, size = wts[n].shape, wts[n].size
        out_g[n], out_d[n], out_m[n], out_v[n] = (t[i, :size].reshape(shape) for t in (g_small, d_small, m_small, v_small))

    return (loss, grad_x.reshape(1, S, D_MODEL), *[out_g[n] for n in WEIGHTS], *[out_d[n] for n in WEIGHTS],
            *[out_m[n] for n in WEIGHTS], *[out_v[n] for n in WEIGHTS])
```

```python
import functools

import jax
import jax.numpy as jnp
from jax import lax
from jax.experimental import pallas as pl
from jax.experimental.pallas import tpu as pltpu

F32 = jnp.float32
BF16 = jnp.bfloat16
MESH = pl.DeviceIdType.MESH

D_MODEL = 1024
N_HEADS = 16
HEAD_DIM = 64
PLE_DIM = 256
N_GROUPS = 4
GROUP_DIM = 256
POOL_WINDOWS = (2, 4, 8, 16)
N_CHIPS = 4
EPS = 1e-6
SB_SCALE = HEAD_DIM ** -0.5

ADAM_LR = 0.001
ADAM_B1 = 0.9
ADAM_B2 = 0.999
ADAM_EPS = 1e-08
ADAM_WD = 0.01
ADAM_STEP = 10

ROW_TILE = 256
WIDE_ROW_TILE = 512
EXP_UNDERFLOW = -104.0
ATT_Q_TILE = 512
ATT_K_TILE = 256
WGRAD_SEQ_TILE = 1024
WGRAD_ACC_BYTES = 4 * 1024 * 1024
MIB = 1024 * 1024


def _params(semantics=None, vmem_mib=48):
    return pltpu.CompilerParams(dimension_semantics=semantics, vmem_limit_bytes=vmem_mib * MIB)


def _dot(a, b):
    return jnp.dot(a, b, preferred_element_type=F32)


def _dot_nt(a, b):
    return lax.dot_general(a, b, (((1,), (1,)), ((), ())), preferred_element_type=F32)


def _dot_tn(a, b):
    return lax.dot_general(a, b, (((0,), (0,)), ((), ())), preferred_element_type=F32)


def _hilo(x):
    hi = x.astype(BF16)
    lo = (x - hi.astype(F32)).astype(BF16)
    return hi, lo


def _dot_hilo(x, w):
    hi, lo = _hilo(x)
    return _dot(hi, w) + _dot(lo, w)


def _sigmoid(z):
    return jax.nn.sigmoid(z)


def _dsilu(z, sg):
    return sg * (1.0 + z * (1.0 - sg))


def _mask_bf16(cond):
    return jnp.where(cond, 1.0, 0.0).astype(BF16)


def _head_mean_matrix():
    r = lax.broadcasted_iota(jnp.int32, (256, 256), 0) // HEAD_DIM
    c = lax.broadcasted_iota(jnp.int32, (256, 256), 1) // HEAD_DIM
    return _mask_bf16(r == c)


def _head_mean(x, bd):
    parts = []
    for s in range(x.shape[1] // 256):
        parts.append(_dot_hilo(x[:, s * 256:(s + 1) * 256], bd))
    out = parts[0] if len(parts) == 1 else jnp.concatenate(parts, axis=1)
    return out * (1.0 / HEAD_DIM)


def _a_in(x, gain, w_sh, gather=()):
    S = x.shape[0]
    tm = 512
    nsh, _, wn = w_sh.shape

    def body(x_ref, g_ref, w_ref, uz_ref, h_ref):
        @pl.when(pl.program_id(1) == 0)
        def _():
            xv = x_ref[...]
            r = lax.rsqrt(jnp.mean(xv * xv, axis=-1, keepdims=True) + EPS)
            h_ref[...] = (xv * r * g_ref[...]).astype(BF16)

        uz_ref[...] = _dot(h_ref[...], w_ref[0])

    return _call_with_gather(
        body, name="a_in", grid=(S // tm, nsh),
        in_specs=[pl.BlockSpec((tm, D_MODEL), lambda i, j: (i, 0)),
                  pl.BlockSpec((1, D_MODEL), lambda i, j: (0, 0)),
                  pl.BlockSpec((1, D_MODEL, wn), lambda i, j: (j, 0, 0))],
        out_specs=[pl.BlockSpec((tm, wn), lambda i, j: (i, j)),
                   pl.BlockSpec((tm, D_MODEL), lambda i, j: (i, 0))],
        out_shape=[jax.ShapeDtypeStruct((S, nsh * wn), F32),
                   jax.ShapeDtypeStruct((S, D_MODEL), BF16)],
        args=(x, gain, w_sh), gather=gather)


def _inv_count(first_row, rows, w):
    t1 = first_row + 1 + lax.broadcasted_iota(jnp.int32, (rows, 1), 0)
    return 1.0 / jnp.minimum(t1, w).astype(F32)


def _group_weight(wg_ref, g):
    return jnp.concatenate([wg_ref[sh, g] for sh in range(N_CHIPS)], axis=0)


def _a_mix(uz, wg, scale, gather=()):
    S = uz.shape[0]
    tm = ROW_TILE

    def body(u_ref, up_ref, z_ref, wg_ref, sc_ref, ga_ref, p_ref):
        i = pl.program_id(0)
        row = lax.broadcasted_iota(jnp.int32, (tm, tm), 0)
        col = lax.broadcasted_iota(jnp.int32, (tm, tm), 1)
        d = row - col
        for g, w in enumerate(POOL_WINDOWS):
            cols = slice(g * GROUP_DIM, (g + 1) * GROUP_DIM)
            t_main = _mask_bf16((d >= 0) & (d < w))
            t_halo = _mask_bf16(d + tm < w)
            u = u_ref[:, cols]
            up = jnp.where(i > 0, up_ref[:, cols], 0.0)
            hi, lo = _hilo(u)
            hip, lop = _hilo(up)
            wsum = _dot(t_main, hi) + _dot(t_main, lo) + _dot(t_halo, hip) + _dot(t_halo, lop)
            pooled = (wsum * _inv_count(i * tm, tm, w) - u).astype(BF16)
            p_ref[:, cols] = pooled
            mraw = _dot(pooled, _group_weight(wg_ref, g))
            z = z_ref[:, cols]
            ga_ref[:, cols] = (mraw * sc_ref[:, cols] * (z * _sigmoid(z))).astype(BF16)

    return _call_with_gather(
        body, name="a_mix", grid=(S // tm,),
        in_specs=[pl.BlockSpec((tm, D_MODEL), lambda i: (i, 0)),
                  pl.BlockSpec((tm, D_MODEL), lambda i: (jnp.maximum(i - 1, 0), 0)),
                  pl.BlockSpec((tm, D_MODEL), lambda i: (i, 1)),
                  pl.BlockSpec((N_CHIPS, N_GROUPS, 64, GROUP_DIM), lambda i: (0, 0, 0, 0)),
                  pl.BlockSpec((1, D_MODEL), lambda i: (0, 0))],
        out_specs=[pl.BlockSpec((tm, D_MODEL), lambda i: (i, 0)),
                   pl.BlockSpec((tm, D_MODEL), lambda i: (i, 0))],
        out_shape=[jax.ShapeDtypeStruct((S, D_MODEL), BF16),
                   jax.ShapeDtypeStruct((S, D_MODEL), BF16)],
        args=(uz, uz, uz, wg, scale), gather=gather)


def _out_ple(name, gated, x_in, w_out, p, layer, ple_w, ple_g, target=None, gather=()):
    S = x_in.shape[0]
    tm = WIDE_ROW_TILE
    with_loss = target is not None

    def body(*refs):
        if with_loss:
            g_ref, x_ref, wo_ref, p_ref, pw_ref, pg_ref, t_ref, xm_ref, dx_ref, e_ref, gt_ref, loss_ref = refs
        else:
            g_ref, x_ref, wo_ref, p_ref, pw_ref, pg_ref, xm_ref, xo_ref, e_ref, gt_ref = refs
        xm = x_ref[...] + _dot(g_ref[...], wo_ref[...])
        xm_ref[...] = xm
        pb = p_ref[...].astype(BF16)
        e = jnp.concatenate([_dot(pb, pw_ref[sh]) for sh in range(N_CHIPS)], axis=1)
        pg = jnp.concatenate([pg_ref[sh] for sh in range(N_CHIPS)], axis=0)
        gate = _sigmoid(_dot(xm.astype(BF16), pg))
        e_ref[...] = e.astype(BF16)
        gt_ref[...] = gate.astype(BF16)
        xo = xm + e * gate
        if with_loss:
            diff = xo - t_ref[...]
            dx_ref[...] = diff * (1.0 / D_MODEL)

            @pl.when(pl.program_id(0) == 0)
            def _():
                loss_ref[...] = jnp.zeros_like(loss_ref)

            loss_ref[...] += jnp.sum(diff * diff) * (0.5 / D_MODEL)
        else:
            xo_ref[...] = xo

    row = pl.BlockSpec((tm, D_MODEL), lambda i: (i, 0))
    in_specs = [row, row,
                pl.BlockSpec((D_MODEL, D_MODEL), lambda i: (0, 0)),
                pl.BlockSpec((None, None, tm, PLE_DIM), lambda i: (layer, 0, i, 0)),
                pl.BlockSpec((N_CHIPS, PLE_DIM, 256), lambda i: (0, 0, 0)),
                pl.BlockSpec((N_CHIPS, 256, D_MODEL), lambda i: (0, 0, 0))]
    args = [gated, x_in, w_out, p, ple_w, ple_g]
    out_specs = [row, row, row, row]
    out_shape = [jax.ShapeDtypeStruct((S, D_MODEL), F32), jax.ShapeDtypeStruct((S, D_MODEL), F32),
                 jax.ShapeDtypeStruct((S, D_MODEL), BF16), jax.ShapeDtypeStruct((S, D_MODEL), BF16)]
    if with_loss:
        in_specs.append(row)
        args.append(target)
        out_specs.append(pl.BlockSpec((8, 128), lambda i: (0, 0)))
        out_shape.append(jax.ShapeDtypeStruct((8, 128), F32))
    return _call_with_gather(body, name=name, grid=(S // tm,), in_specs=in_specs, out_specs=out_specs,
                             out_shape=out_shape, args=args, gather=gather)


def _b_in(x, kv_gain, b_gain, k_gain_t, q_gain_t, w_kv, w_in, gather=()):
    S = x.shape[0]
    tm = ROW_TILE

    def body(x_ref, kvg_ref, bg_ref, kg_ref, qg_ref, wkv_ref, win_ref,
             hkv_ref, hb_ref, kraw_ref, qraw_ref, k_ref, q_ref, v_ref, z_ref):
        xv = x_ref[...]
        y = xv * lax.rsqrt(jnp.mean(xv * xv, axis=-1, keepdims=True) + EPS)
        hkv = (y * kvg_ref[...]).astype(BF16)
        hb = (y * bg_ref[...]).astype(BF16)
        hkv_ref[...] = hkv
        hb_ref[...] = hb
        bd = _head_mean_matrix()

        def head_norm(raw, gain):
            rr = lax.rsqrt(_head_mean(raw * raw, bd) + EPS)
            return raw * rr * gain

        for sh in range(N_CHIPS):
            kvc = _dot(hkv, wkv_ref[sh])
            qzc = _dot(hb, win_ref[sh])
            cols = slice((sh % 2) * 512, (sh % 2) * 512 + 512)
            if sh < 2:
                kraw_ref[:, cols] = kvc.astype(BF16)
                qraw_ref[:, cols] = qzc.astype(BF16)
                k_ref[:, cols] = head_norm(kvc, kg_ref[:, cols]).astype(BF16)
                q_ref[:, cols] = (head_norm(qzc, qg_ref[:, cols]) * SB_SCALE).astype(BF16)
            else:
                v_ref[:, cols] = kvc.astype(BF16)
                z_ref[:, cols] = qzc.astype(BF16)

    row = pl.BlockSpec((tm, D_MODEL), lambda i: (i, 0))
    vec = pl.BlockSpec((1, D_MODEL), lambda i: (0, 0))
    wsp = pl.BlockSpec((N_CHIPS, D_MODEL, 512), lambda i: (0, 0, 0))
    return _call_with_gather(
        body, name="b_in", grid=(S // tm,),
        in_specs=[row, vec, vec, vec, vec, wsp, wsp],
        out_specs=[row] * 8,
        out_shape=[jax.ShapeDtypeStruct((S, D_MODEL), BF16)] * 8,
        args=(x, kv_gain, b_gain, k_gain_t, q_gain_t, w_kv, w_in), gather=gather, vmem_mib=56)


def _softplus_parts(z):
    e = jnp.exp(-jnp.abs(z))
    return -(jnp.maximum(z, 0.0) + jnp.log(1.0 + e)), e


def _add_rows(total, rows, update):
    lo, hi = rows
    parts = ([total[:lo]] if lo else []) + [total[lo:hi] + update] + ([total[hi:]] if hi < total.shape[0] else [])
    return parts[0] if len(parts) == 1 else jnp.concatenate(parts, axis=0)


def _attn_fwd(q, k, v, zgate, gather=()):
    S = q.shape[0]
    tq, tk = ATT_Q_TILE, ATT_K_TILE
    kpq = tq // tk
    assert kpq == 2

    def body(q_ref, k_ref, v_ref, z_ref, o_ref, g_ref, lt_ref, steps_ref):
        qi = pl.program_id(1)
        lane = lax.broadcasted_iota(jnp.int32, (1, 128), 1)
        ri = lax.broadcasted_iota(jnp.int32, (tk, tk), 0)
        ci = lax.broadcasted_iota(jnp.int32, (tk, tk), 1)
        later_mat = _mask_bf16(ri > ci)
        causal = ci < ri
        qv = q_ref[...]
        first = lane < HEAD_DIM
        q_heads = (jnp.where(first, qv, jnp.zeros_like(qv)), jnp.where(first, jnp.zeros_like(qv), qv))

        def step(blocks, carry):
            chains = [(b, h) for b in range(len(blocks)) for h in range(2)]
            rows = [r for _, r, _ in blocks]
            s0 = [pl.multiple_of(kj * tk, tk) for kj, _, _ in blocks]
            kb = [k_ref[pl.ds(s, tk), :] for s in s0]
            vb = [v_ref[pl.ds(s, tk), :] for s in s0]
            visible = [causal if masked else None for _, _, masked in blocks]
            z = {c: _dot_nt(q_heads[c[1]][rows[c[0]][0]:rows[c[0]][1]], kb[c[0]]) for c in chains}
            run = [carry[0], carry[2]]
            log_own, later, run_at = {}, {}, {}
            for c in chains:
                b, h = c
                lk = _softplus_parts(z[c])[0]
                if visible[b] is not None:
                    lk = jnp.where(visible[b], lk, 0.0)
                log_own[c] = z[c] + lk
                later[c] = _dot(lk.astype(BF16), later_mat)
                run_at[c] = run[h][rows[b][0]:rows[b][1]]
                run[h] = _add_rows(run[h], rows[b], jnp.sum(lk, axis=-1, keepdims=True))
            acc = [carry[1], carry[3]]
            for c in chains:
                b, h = c
                a = jnp.exp(log_own[c] + later[c] + run_at[c])
                if visible[b] is not None:
                    a = jnp.where(visible[b], a, 0.0)
                acc[h] = _add_rows(acc[h], rows[b], _dot(a.astype(BF16), vb[b]))
            return run[0], acc[0], run[1], acc[1]

        zero1, zero128 = jnp.zeros((tq, 1), F32), jnp.zeros((tq, 128), F32)
        carry = step([(qi * kpq + 1, (tk, tq), True), (qi * kpq, (tk, tq), False), (qi * kpq, (0, tk), True)],
                     (zero1, zero128, zero1, zero128))

        def low(run):
            return jnp.max(run)

        def pair_more(c):
            return (c[0] < qi) & (jnp.maximum(low(c[1][tk:]), low(c[3][tk:])) > EXP_UNDERFLOW)

        def pair_step(c):
            last = (qi - c[0]) * kpq - 1
            return (c[0] + 1, *step([(last, (0, tq), False), (last - 1, (0, tq), False)], c[1:]))

        pairs, *carry = lax.while_loop(pair_more, pair_step, (jnp.int32(0), *carry))
        left = (qi - pairs) * kpq

        def single_more(c):
            return (c[0] < left) & (jnp.maximum(low(c[1][:tk]), low(c[3][:tk])) > EXP_UNDERFLOW)

        def single_step(c):
            return (c[0] + 1, *step([(left - 1 - c[0], (0, tk), False)], c[1:]))

        singles, *carry = lax.while_loop(single_more, single_step, (jnp.int32(0), *carry))
        steps_ref[...] = jnp.concatenate([jnp.full((4, 128), pairs, F32), jnp.full((4, 128), singles, F32)], axis=0)
        o_tot = jnp.where(first, carry[1], carry[3])
        l_tot = jnp.where(first, carry[0], carry[2])
        o_ref[...] = o_tot.astype(BF16)
        lt_ref[...] = l_tot
        zz = z_ref[...].astype(F32)
        g_ref[...] = (o_tot * (zz * _sigmoid(zz))).astype(BF16)

    blk = pl.BlockSpec((tq, 128), lambda hp, qi: (qi, hp))
    seq = pl.BlockSpec((S, 128), lambda hp, qi: (0, hp))
    return _call_with_gather(
        body, name="attn_fwd", grid=(D_MODEL // 128, S // tq),
        in_specs=[blk, seq, seq, blk],
        out_specs=[blk, blk, blk, pl.BlockSpec((None, None, 8, 128), lambda hp, qi: (hp, qi, 0, 0))],
        out_shape=[jax.ShapeDtypeStruct((S, D_MODEL), BF16)] * 2 + [jax.ShapeDtypeStruct((S, D_MODEL), F32)]
        + [jax.ShapeDtypeStruct((D_MODEL // 128, S // tq, 8, 128), F32)],
        args=(q, k, v, zgate), gather=gather)


def _ple_out_bwd(name, dx_out, e, gate, ple_g, w_out, reduce=None):
    S = dx_out.shape[0]
    tm = ROW_TILE if reduce else WIDE_ROW_TILE

    def body(dx_ref, e_ref, gt_ref, pg_ref, wo_ref, de_ref, dgp_ref, dxm_ref, dg_ref):
        dxo = dx_ref[...]
        ev = e_ref[...].astype(F32)
        gv = gt_ref[...].astype(F32)
        de_ref[...] = (dxo * gv).astype(BF16)
        dgp = (dxo * ev * gv * (1.0 - gv)).astype(BF16)
        dgp_ref[...] = dgp
        pg = jnp.concatenate([pg_ref[sh] for sh in range(N_CHIPS)], axis=0)
        dxm = dxo + _dot_nt(dgp, pg)
        dxm_ref[...] = dxm
        dg_ref[...] = _dot_nt(dxm.astype(BF16), wo_ref[...]).astype(BF16)

    row = pl.BlockSpec((tm, D_MODEL), lambda i: (i, 0))
    return _call_with_gather(
        body, name=name, grid=(S // tm,),
        in_specs=[row, row, row,
                  pl.BlockSpec((N_CHIPS, 256, D_MODEL), lambda i: (0, 0, 0)),
                  pl.BlockSpec((D_MODEL, D_MODEL), lambda i: (0, 0))],
        out_specs=[row, row, row, row],
        out_shape=[jax.ShapeDtypeStruct((S, D_MODEL), BF16), jax.ShapeDtypeStruct((S, D_MODEL), BF16),
                   jax.ShapeDtypeStruct((S, D_MODEL), F32), jax.ShapeDtypeStruct((S, D_MODEL), BF16)],
        args=(dx_out, e, gate, ple_g, w_out), reduce=reduce)


def _attn_bwd(q, k, v, ltot, steps, dgated, o, zgate, reduce=None):
    S = q.shape[0]
    tq, tk = ATT_Q_TILE, ATT_K_TILE
    kpq = tq // tk
    nq = S // tq

    def body(q_ref, k_ref, v_ref, lt_ref, steps_ref, dg_ref, o_ref, z_ref, dq_ref, dk_ref, dv_ref, dz_ref,
             dk_acc, dv_acc):
        qi = pl.program_id(1)

        @pl.when(qi == 0)
        def _():
            dk_acc[...] = jnp.zeros_like(dk_acc)
            dv_acc[...] = jnp.zeros_like(dv_acc)

        lane = lax.broadcasted_iota(jnp.int32, (1, 128), 1)
        ri = lax.broadcasted_iota(jnp.int32, (tk, tk), 0)
        ci = lax.broadcasted_iota(jnp.int32, (tk, tk), 1)
        later_mat = _mask_bf16(ri > ci)
        before_mat = _mask_bf16(ri < ci)
        causal = ci < ri
        zz = z_ref[...].astype(F32)
        sg = _sigmoid(zz)
        dgv = dg_ref[...].astype(F32)
        dz_ref[...] = (dgv * o_ref[...].astype(F32) * _dsilu(zz, sg)).astype(BF16)
        dob = (dgv * (zz * sg)).astype(BF16)
        ltv = lt_ref[...]
        qv = q_ref[...]
        first = lane < HEAD_DIM
        masks = (first, jnp.logical_not(first))
        q_heads = [jnp.where(hm, qv, jnp.zeros_like(qv)) for hm in masks]
        do_heads = [jnp.where(hm, dob, jnp.zeros_like(dob)) for hm in masks]
        totals = [jnp.max(jnp.where(hm, ltv, -jnp.inf), axis=-1, keepdims=True) for hm in masks]

        def step(blocks, carry):
            chains = [(b, h) for b in range(len(blocks)) for h in range(2)]
            rows = [r for _, r, _ in blocks]
            cut = lambda t, b: t[rows[b][0]:rows[b][1]]
            s0 = [pl.multiple_of(kj * tk, tk) for kj, _, _ in blocks]
            kb = [k_ref[pl.ds(s, tk), :] for s in s0]
            vb = [v_ref[pl.ds(s, tk), :] for s in s0]
            visible = [causal if masked else None for _, _, masked in blocks]
            z = {c: _dot_nt(cut(q_heads[c[1]], c[0]), kb[c[0]]) for c in chains}
            da = {c: _dot_nt(cut(do_heads[c[1]], c[0]), vb[c[0]]) for c in chains}
            run = [carry[0], carry[3]]
            log_own, beta, later, base = {}, {}, {}, {}
            for c in chains:
                b, h = c
                lk = _softplus_parts(z[c])[0]
                if visible[b] is not None:
                    lk = jnp.where(visible[b], lk, 0.0)
                log_own[c] = z[c] + lk
                beta[c] = jnp.exp(log_own[c]).astype(BF16)
                later[c] = _dot(lk.astype(BF16), later_mat)
                run[h] = _add_rows(run[h], rows[b], jnp.sum(lk, axis=-1, keepdims=True))
                base[c] = cut(totals[h] - run[h], b)
            grun = [carry[1], carry[4]]
            a_bf, g_bf, gbefore, grun_at = {}, {}, {}, {}
            for c in chains:
                b, h = c
                a = jnp.exp(log_own[c] + later[c] + base[c])
                if visible[b] is not None:
                    a = jnp.where(visible[b], a, 0.0)
                a_bf[c] = a.astype(BF16)
                g = da[c] * a
                g_bf[c] = g.astype(BF16)
                gbefore[c] = _dot(g_bf[c], before_mat)
                grun_at[c] = cut(grun[h], b)
                grun[h] = _add_rows(grun[h], rows[b], jnp.sum(g, axis=-1, keepdims=True))
            dq = [carry[2], carry[5]]
            dk_blk = [jnp.zeros((tk, 128), F32) for _ in blocks]
            dv_blk = [jnp.zeros((tk, 128), F32) for _ in blocks]
            for c in chains:
                b, h = c
                g = g_bf[c].astype(F32)
                dz = g - beta[c].astype(F32) * (g + gbefore[c] + grun_at[c])
                if visible[b] is not None:
                    dz = jnp.where(visible[b], dz, 0.0)
                dzb = dz.astype(BF16)
                dq[h] = _add_rows(dq[h], rows[b], _dot(dzb, kb[b]))
                dk_blk[b] = dk_blk[b] + _dot_tn(dzb, cut(q_heads[h], b))
                dv_blk[b] = dv_blk[b] + _dot_tn(a_bf[c], cut(do_heads[h], b))
            for b in range(len(blocks)):
                dk_acc[pl.ds(s0[b], tk), :] += dk_blk[b]
                dv_acc[pl.ds(s0[b], tk), :] += dv_blk[b]
            return run[0], grun[0], dq[0], run[1], grun[1], dq[1]

        pairs = jnp.clip(jnp.max(steps_ref[0:4, :]).astype(jnp.int32), 0, qi)
        left = (qi - pairs) * kpq
        singles = jnp.clip(jnp.max(steps_ref[4:8, :]).astype(jnp.int32), 0, left)
        zero1, zero128 = jnp.zeros((tq, 1), F32), jnp.zeros((tq, 128), F32)
        carry = lax.fori_loop(left - singles, left, lambda kj, c: step([(kj, (0, tk), False)], c),
                              (zero1, zero1, zero128, zero1, zero1, zero128))
        carry = lax.fori_loop(qi - pairs, qi,
                              lambda n, c: step([(n * kpq, (0, tq), False), (n * kpq + 1, (0, tq), False)], c), carry)
        carry = step([(qi * kpq, (0, tk), True), (qi * kpq, (tk, tq), False), (qi * kpq + 1, (tk, tq), True)], carry)
        dq_ref[...] = jnp.where(first, carry[2], carry[5]).astype(BF16)

        @pl.when(qi == nq - 1)
        def _():
            dk_ref[...] = dk_acc[...].astype(BF16)
            dv_ref[...] = dv_acc[...].astype(BF16)

    blk = pl.BlockSpec((tq, 128), lambda hp, qi: (qi, hp))
    seq = pl.BlockSpec((S, 128), lambda hp, qi: (0, hp))
    return _call_with_gather(
        body, name="attn_bwd", grid=(D_MODEL // 128, nq),
        in_specs=[blk, seq, seq, blk, pl.BlockSpec((None, None, 8, 128), lambda hp, qi: (hp, qi, 0, 0)),
                  blk, blk, blk],
        out_specs=[blk, seq, seq, blk],
        out_shape=[jax.ShapeDtypeStruct((S, D_MODEL), BF16)] * 4,
        scratch_shapes=[pltpu.VMEM((S, 128), F32), pltpu.VMEM((S, 128), F32)],
        args=(q, k, v, ltot, steps, dgated, o, zgate), reduce=reduce, vmem_mib=56)


def _rms_bwd(xv, dh_gain_sum):
    r = lax.rsqrt(jnp.mean(xv * xv, axis=-1, keepdims=True) + EPS)
    xhat = xv * r
    dx = r * (dh_gain_sum - xhat * jnp.mean(dh_gain_sum * xhat, axis=-1, keepdims=True))
    return dx, xhat


def _b_in_bwd(dq, dk, dv, dz, q_raw, k_raw, x, dx_mid, q_gain_t, k_gain_t, b_gain, kv_gain, w_in, w_kv):
    S = x.shape[0]
    tm = ROW_TILE

    def body(dq_ref, dk_ref, dv_ref, dz_ref, qr_ref, kr_ref, x_ref, dxm_ref, qg_ref, kg_ref, bg_ref, kvg_ref,
             win_ref, wkv_ref, dqz_ref, dkv_ref, dx_ref, small_ref):
        @pl.when(pl.program_id(0) == 0)
        def _():
            small_ref[...] = jnp.zeros_like(small_ref)

        bd = _head_mean_matrix()

        def head_norm_bwd(dy_ref, raw_ref, gain, scale):
            raw = raw_ref[...].astype(F32)
            rr = lax.rsqrt(_head_mean(raw * raw, bd) + EPS)
            xhat = raw * rr
            dy = dy_ref[...].astype(F32) * scale
            gdy = dy * gain
            draw = rr * (gdy - xhat * _head_mean(gdy * xhat, bd))
            return draw.astype(BF16), jnp.sum(dy * xhat, axis=0, keepdims=True)

        dqr, dqg = head_norm_bwd(dq_ref, qr_ref, qg_ref[...], SB_SCALE)
        dkr, dkg = head_norm_bwd(dk_ref, kr_ref, kg_ref[...], 1.0)
        dqz_ref[:, :D_MODEL] = dqr
        dqz_ref[:, D_MODEL:] = dz_ref[...]
        dkv_ref[:, :D_MODEL] = dkr
        dkv_ref[:, D_MODEL:] = dv_ref[...]
        dhb = jnp.zeros((tm, D_MODEL), F32)
        dhkv = jnp.zeros((tm, D_MODEL), F32)
        for sh in range(N_CHIPS):
            cols = slice(sh * 512, (sh + 1) * 512)
            dhb = dhb + _dot_nt(dqz_ref[:, cols], win_ref[sh])
            dhkv = dhkv + _dot_nt(dkv_ref[:, cols], wkv_ref[sh])
        dx, xhat = _rms_bwd(x_ref[...], dhb * bg_ref[...] + dhkv * kvg_ref[...])
        dx_ref[...] = dxm_ref[...] + dx
        small_ref[0:1, :] += dqg
        small_ref[1:2, :] += dkg
        small_ref[2:3, :] += jnp.sum(dhb * xhat, axis=0, keepdims=True)
        small_ref[3:4, :] += jnp.sum(dhkv * xhat, axis=0, keepdims=True)

    row = pl.BlockSpec((tm, D_MODEL), lambda i: (i, 0))
    wide = pl.BlockSpec((tm, 2 * D_MODEL), lambda i: (i, 0))
    vec = pl.BlockSpec((1, D_MODEL), lambda i: (0, 0))
    wsp = pl.BlockSpec((N_CHIPS, D_MODEL, 512), lambda i: (0, 0, 0))
    return pl.pallas_call(
        body, name="b_in_bwd", grid=(S // tm,),
        in_specs=[row] * 8 + [vec] * 4 + [wsp, wsp],
        out_specs=[wide, wide, row, pl.BlockSpec((8, D_MODEL), lambda i: (0, 0))],
        out_shape=[jax.ShapeDtypeStruct((S, 2 * D_MODEL), BF16), jax.ShapeDtypeStruct((S, 2 * D_MODEL), BF16),
                   jax.ShapeDtypeStruct((S, D_MODEL), F32), jax.ShapeDtypeStruct((8, D_MODEL), F32)],
        compiler_params=_params(("arbitrary",), 56),
    )(dq, dk, dv, dz, q_raw, k_raw, x, dx_mid, q_gain_t, k_gain_t, b_gain, kv_gain, w_in, w_kv)


def _a_mix_bwd(dgated, uz, pooled, wg, scale, w_in, x, dx_mid, gain, reduce=None):
    S = x.shape[0]
    tm = ROW_TILE
    n = S // tm

    def body(dg_ref, z_ref, p_ref, wg_ref, sc_ref, win_ref, x_ref, dxm_ref, gn_ref,
             duz_ref, dmr_ref, dx_ref, small_ref, halo_hi, halo_lo):
        i = pl.program_id(0)

        @pl.when(i == 0)
        def _():
            small_ref[...] = jnp.zeros_like(small_ref)
            halo_hi[...] = jnp.zeros_like(halo_hi)
            halo_lo[...] = jnp.zeros_like(halo_lo)

        first_row = (n - 1 - i) * tm
        row = lax.broadcasted_iota(jnp.int32, (tm, tm), 0)
        col = lax.broadcasted_iota(jnp.int32, (tm, tm), 1)
        d = col - row
        for g, w in enumerate(POOL_WINDOWS):
            cols = slice(g * GROUP_DIM, (g + 1) * GROUP_DIM)
            wgg = _group_weight(wg_ref, g)
            sc = sc_ref[:, cols]
            mraw = _dot(p_ref[:, cols], wgg)
            z = z_ref[:, cols]
            sg = _sigmoid(z)
            dga = dg_ref[:, cols].astype(F32)
            dm = dga * (z * sg)
            duz_ref[:, D_MODEL + g * GROUP_DIM:D_MODEL + (g + 1) * GROUP_DIM] = (
                dga * (mraw * sc) * _dsilu(z, sg)).astype(BF16)
            small_ref[0:1, cols] += jnp.sum(dm * mraw, axis=0, keepdims=True)
            dmr = (dm * sc).astype(BF16)
            dmr_ref[:, cols] = dmr
            dp = _dot_nt(dmr, wgg)
            hi, lo = _hilo(dp * _inv_count(first_row, tm, w))
            t_main = _mask_bf16((d >= 0) & (d < w))
            t_halo = _mask_bf16(d + tm < w)
            du = (_dot(t_main, hi) + _dot(t_main, lo) + _dot(t_halo, halo_hi[:, cols]) + _dot(t_halo, halo_lo[:, cols])
                  - dp)
            halo_hi[:, cols] = hi
            halo_lo[:, cols] = lo
            duz_ref[:, cols] = du.astype(BF16)
        dh = jnp.zeros((tm, D_MODEL), F32)
        for sh in range(N_CHIPS):
            dh = dh + _dot_nt(duz_ref[:, sh * 512:(sh + 1) * 512], win_ref[sh])
        dx, xhat = _rms_bwd(x_ref[...], dh * gn_ref[...])
        dx_ref[...] = dxm_ref[...] + dx
        small_ref[1:2, :] += jnp.sum(dh * xhat, axis=0, keepdims=True)

    rev = lambda i: (n - 1 - i, 0)
    row = pl.BlockSpec((tm, D_MODEL), rev)
    vec = pl.BlockSpec((1, D_MODEL), lambda i: (0, 0))
    return _call_with_gather(
        body, name="a_mix_bwd", grid=(n,),
        in_specs=[row,
                  pl.BlockSpec((tm, D_MODEL), lambda i: (n - 1 - i, 1)),
                  row,
                  pl.BlockSpec((N_CHIPS, N_GROUPS, 64, GROUP_DIM), lambda i: (0, 0, 0, 0)),
                  vec,
                  pl.BlockSpec((N_CHIPS, D_MODEL, 512), lambda i: (0, 0, 0)),
                  row, row, vec],
        out_specs=[pl.BlockSpec((tm, 2 * D_MODEL), rev), row, row,
                   pl.BlockSpec((8, D_MODEL), lambda i: (0, 0))],
        out_shape=[jax.ShapeDtypeStruct((S, 2 * D_MODEL), BF16), jax.ShapeDtypeStruct((S, D_MODEL), BF16),
                   jax.ShapeDtypeStruct((S, D_MODEL), F32), jax.ShapeDtypeStruct((8, D_MODEL), F32)],
        scratch_shapes=[pltpu.VMEM((tm, D_MODEL), BF16), pltpu.VMEM((tm, D_MODEL), BF16)],
        args=(dgated, uz, pooled, wg, scale, w_in, x, dx_mid, gain), reduce=reduce, vmem_mib=56)


def _wgrad(name, a, dy, n_shards, a_spec=None, k_dim=None):
    S, n_cols = dy.shape
    ts = WGRAD_SEQ_TILE
    k_dim = a.shape[-1] if k_dim is None else k_dim
    wn = n_cols // n_shards
    tk = min(k_dim, WGRAD_ACC_BYTES // (4 * n_cols))
    nst = S // ts

    def body(a_ref, dy_ref, out_ref, acc):
        st = pl.program_id(1)

        @pl.when(st == 0)
        def _():
            acc[...] = jnp.zeros_like(acc)

        acc[...] += _dot_tn(a_ref[...].astype(BF16), dy_ref[...].astype(BF16))

        @pl.when(st == nst - 1)
        def _():
            for sh in range(n_shards):
                out_ref[sh] = acc[:, sh * wn:(sh + 1) * wn]

    if a_spec is None:
        a_spec = pl.BlockSpec((ts, tk), lambda kt, st: (st, kt))
    return pl.pallas_call(
        body, name=name, grid=(k_dim // tk, nst),
        in_specs=[a_spec, pl.BlockSpec((ts, n_cols), lambda kt, st: (st, 0))],
        out_specs=pl.BlockSpec((n_shards, tk, wn), lambda kt, st: (0, kt, 0)),
        out_shape=jax.ShapeDtypeStruct((n_shards, k_dim, wn), F32),
        scratch_shapes=[pltpu.VMEM((tk, n_cols), F32)],
        compiler_params=_params(("parallel", "arbitrary")),
    )(a, dy)


def _wgrad_ple(name, p, layer, de):
    ts = WGRAD_SEQ_TILE
    spec = pl.BlockSpec((None, None, ts, PLE_DIM), lambda kt, st: (layer, 0, st, 0))
    return _wgrad(name, p, de, N_CHIPS, a_spec=spec, k_dim=PLE_DIM)


def _wgrad_group(pooled, dmr):
    S = pooled.shape[0]
    ts = WGRAD_SEQ_TILE
    nst = S // ts

    def body(p_ref, d_ref, out_ref, acc):
        st = pl.program_id(1)

        @pl.when(st == 0)
        def _():
            acc[...] = jnp.zeros_like(acc)

        acc[...] += _dot_tn(p_ref[...], d_ref[...])

        @pl.when(st == nst - 1)
        def _():
            for sh in range(N_CHIPS):
                out_ref[sh] = acc[sh * 64:(sh + 1) * 64, :]

    blk = pl.BlockSpec((ts, GROUP_DIM), lambda g, st: (st, g))
    return pl.pallas_call(
        body, name="wgrad_group", grid=(N_GROUPS, nst),
        in_specs=[blk, blk],
        out_specs=pl.BlockSpec((N_CHIPS, None, 64, GROUP_DIM), lambda g, st: (0, g, 0, 0)),
        out_shape=jax.ShapeDtypeStruct((N_CHIPS, N_GROUPS, 64, GROUP_DIM), F32),
        scratch_shapes=[pltpu.VMEM((GROUP_DIM, GROUP_DIM), F32)],
        compiler_params=_params(("parallel", "arbitrary")),
    )(pooled, dmr)


GATHER_AT = {
    "a_in": ("a_w_group", "a_w_out", "ple_w0", "ple_gate_w0"),
    "a_mix": ("w_kv",),
    "a_out_ple": ("b_w_in",),
    "attn_fwd": ("b_w_out", "ple_w1", "ple_gate_w1"),
}


REDUCE_AT = {
    "attn_bwd": ("b_w_out", "ple_w1", "ple_gate_w1"),
    "a_ple_out_bwd": ("w_kv",),
    "a_mix_bwd": ("b_w_in",),
}


def _local_step(x, p, target, w, local=None, state=None):
    w = dict(w)

    def run(fn, host, n_out, *args, **kwargs):
        names = GATHER_AT[host] if local is not None else ()
        res = fn(*args, gather=[local[n] for n in names], **kwargs)
        w.update(zip(names, res[n_out:]))
        return res[:n_out]

    k_gain_t = jnp.tile(w["k_norm"].reshape(1, HEAD_DIM), (1, N_HEADS))
    q_gain_t = jnp.tile(w["b_q_norm"].reshape(1, HEAD_DIM), (1, N_HEADS))

    uz, h_a = run(_a_in, "a_in", 2, x, w["a_norm"], w["a_w_in"])
    wg4 = w["a_w_group"].reshape(N_CHIPS, N_GROUPS, 64, GROUP_DIM)
    wa_out = w["a_w_out"].reshape(D_MODEL, D_MODEL)
    gated_a, pooled = run(_a_mix, "a_mix", 2, uz, wg4, w["a_scale"])
    x1, x2, e_a, gate_a = run(_out_ple, "a_out_ple", 4, "a_out_ple", gated_a, x, wa_out, p, 0,
                              w["ple_w0"], w["ple_gate_w0"])
    h_kv, h_b, k_raw, q_raw, k, q, v, z_b = _b_in(
        x2, w["kv_norm"], w["b_norm"], k_gain_t, q_gain_t, w["w_kv"], w["b_w_in"])
    o, gated_b, ltot, att_steps = run(_attn_fwd, "attn_fwd", 4, q, k, v, z_b)
    wb_out = w["b_w_out"].reshape(D_MODEL, D_MODEL)
    x3, dx4, e_b, gate_b, loss_blk = _out_ple("b_out_ple", gated_b, x2, wb_out, p, 1, w["ple_w1"], w["ple_gate_w1"],
                                              target=target)

    grads, updates = {}, {}

    def hosted(fn, host, n_out, *args):
        if state is None:
            return fn(*args)
        names = REDUCE_AT[host]
        res = fn(*args, reduce=([grads.pop(n) for n in names], *[[t[n] for n in names] for t in state[:3]],
                                [state[3][n] for n in names]))
        for i, n in enumerate(names):
            updates[n] = tuple(group[i] for group in res[n_out:])
        return res[:n_out]

    de_b, dgp_b, dx3, dgated_b = _ple_out_bwd("b_ple_out_bwd", dx4, e_b, gate_b, w["ple_gate_w1"], wb_out)
    grads["b_w_out"] = _wgrad("wgrad_b_out", gated_b, dx3, 1).reshape(N_CHIPS, 256, D_MODEL)
    grads["ple_w1"] = _wgrad_ple("wgrad_ple1", p, 1, de_b)
    grads["ple_gate_w1"] = _wgrad("wgrad_gate1", x3, dgp_b, 1).reshape(N_CHIPS, 256, D_MODEL)
    dq, dk, dv, dz_b = hosted(_attn_bwd, "attn_bwd", 4, q, k, v, ltot, att_steps, dgated_b, o, z_b)
    dqz, dkv, dx2, small_b = _b_in_bwd(dq, dk, dv, dz_b, q_raw, k_raw, x2, dx3, q_gain_t, k_gain_t,
                                       w["b_norm"], w["kv_norm"], w["b_w_in"], w["w_kv"])
    grads["w_kv"] = _wgrad("wgrad_kv", h_kv, dkv, N_CHIPS)
    grads["b_w_in"] = _wgrad("wgrad_b_in", h_b, dqz, N_CHIPS)
    de_a, dgp_a, dx1, dgated_a = hosted(_ple_out_bwd, "a_ple_out_bwd", 4, "a_ple_out_bwd", dx2, e_a, gate_a,
                                        w["ple_gate_w0"], wa_out)
    grads["a_w_out"] = _wgrad("wgrad_a_out", gated_a, dx1, 1).reshape(N_CHIPS, 256, D_MODEL)
    grads["ple_w0"] = _wgrad_ple("wgrad_ple0", p, 0, de_a)
    grads["ple_gate_w0"] = _wgrad("wgrad_gate0", x1, dgp_a, 1).reshape(N_CHIPS, 256, D_MODEL)
    duz, dmr, grad_x, small_a = hosted(_a_mix_bwd, "a_mix_bwd", 4, dgated_a, uz, pooled, wg4, w["a_scale"],
                                       w["a_w_in"], x, dx1, w["a_norm"])
    grads["a_w_in"] = _wgrad("wgrad_a_in", h_a, duz, N_CHIPS)
    grads["a_w_group"] = _wgrad_group(pooled, dmr).reshape(N_CHIPS, N_GROUPS * 64, GROUP_DIM)

    fold = lambda row: jnp.pad(row.reshape(N_HEADS, HEAD_DIM).sum(axis=0), (0, D_MODEL - HEAD_DIM))
    small = jnp.stack([small_a[1], small_a[0], small_b[3], small_b[2], fold(small_b[1]), fold(small_b[0]),
                       jnp.pad(loss_blk[0], (0, D_MODEL - loss_blk.shape[1])), jnp.zeros((D_MODEL,), F32)])
    return grad_x, grads, updates, small


def _mesh_place():
    x, y, c = lax.axis_index("x"), lax.axis_index("y"), lax.axis_index("c")
    other_chips = [(1 - x, y), (x, 1 - y), (1 - x, 1 - y)]
    return x, y, c, other_chips


def _gather_sems(n):
    return [pltpu.SemaphoreType.DMA((3 * n,)), pltpu.SemaphoreType.DMA((3 * n,)),
            pltpu.SemaphoreType.DMA((3 * n,)), pltpu.SemaphoreType.DMA((3 * n,)), pltpu.SemaphoreType.DMA((n,))]


def _gather_copies(srcs, outs, sems):
    send_far, recv_far, send_sib, recv_sib, local_sem = sems
    n = len(srcs)
    x, y, c, chips = _mesh_place()
    me = 2 * x + y
    sibling = (x, y, 1 - c)

    def half(k, which):
        rows = srcs[k].shape[0] // 2
        return pl.ds(pl.multiple_of(which * rows, 16), rows)

    local = [pltpu.make_async_copy(srcs[k], outs[k].at[me], local_sem.at[k]) for k in range(n)]
    far = [pltpu.make_async_remote_copy(
        src_ref=srcs[k].at[half(k, c)], dst_ref=outs[k].at[me, half(k, c)],
        send_sem=send_far.at[j * n + k], recv_sem=recv_far.at[j * n + k], device_id=(px, py, c), device_id_type=MESH)
        for j, (px, py) in enumerate(chips) for k in range(n)]

    def landed(j, k, which, from_far):
        px, py = chips[j]
        piece = outs[k].at[2 * px + py, half(k, which)]
        send, recv = (send_far, recv_far) if from_far else (send_sib, recv_sib)
        return pltpu.make_async_remote_copy(src_ref=piece, dst_ref=piece, send_sem=send.at[j * n + k],
                                            recv_sem=recv.at[j * n + k], device_id=sibling, device_id_type=MESH)

    return local, far, landed, c


def _gather_start(srcs, outs, sems):
    local, far, _, _ = _gather_copies(srcs, outs, sems)
    for cp in local + far:
        cp.start()


def _gather_pass_on(srcs, outs, sems):
    _, _, landed, c = _gather_copies(srcs, outs, sems)
    for j in range(3):
        for k in range(len(srcs)):
            landed(j, k, c, True).wait_recv()
            landed(j, k, c, False).start()


def _gather_finish(srcs, outs, sems):
    local, far, landed, c = _gather_copies(srcs, outs, sems)
    pairs = [(j, k) for j in range(3) for k in range(len(srcs))]
    for j, k in pairs:
        landed(j, k, 1 - c, False).wait_recv()
    for cp in far + [landed(j, k, c, False) for j, k in pairs]:
        cp.wait_send()
    for cp in local:
        cp.wait()


def _call_with_gather(body, *, name, grid, in_specs, out_specs, out_shape, args, gather=(), reduce=None,
                      scratch_shapes=(), vmem_mib=48):
    n_in, n_out, n_scr, n_g = len(args), len(out_shape), len(scratch_shapes), len(gather)
    n_r = len(reduce[0]) if reduce else 0
    pieces = _reduce_pieces(reduce[0], reduce[4]) if reduce else []
    reduce_args = [a for group in reduce[:4] for a in group] if reduce else []
    gather_sems = _gather_sems(n_g) if n_g else []
    n_steps = 1
    for g in grid:
        n_steps *= g

    def wrapped(*refs):
        refs = list(refs)
        take = lambda count: [refs.pop(0) for _ in range(count)]
        ins, g_in, r_in = take(n_in), take(n_g), take(4 * n_r)
        outs, g_out, r_out = take(n_out), take(n_g), take(4 * n_r)
        scratch, sems, r_scratch = take(n_scr), take(len(gather_sems)), refs
        step = 0
        for axis, g in enumerate(grid):
            step = step * g + pl.program_id(axis)
        if n_g:
            @pl.when(step == 0)
            def _():
                _gather_start(g_in, g_out, sems)

        if n_r:
            ticks, drain = _reduce_ticks(pieces, n_r, (*r_in, *r_out, *r_scratch))
            for t, tick in enumerate(ticks[:n_steps]):
                pl.when(step == t)(tick)

        body(*ins, *outs, *scratch)
        if n_r:
            for tick in ticks[n_steps:]:
                pl.when(step == n_steps - 1)(tick)
            pl.when(step == n_steps - 1)(drain)
        if n_g:
            @pl.when(step == max(n_steps - 2, 0))
            def _():
                _gather_pass_on(g_in, g_out, sems)

            @pl.when(step == n_steps - 1)
            def _():
                _gather_finish(g_in, g_out, sems)

    hbm = pl.BlockSpec(memory_space=pltpu.HBM)
    res = pl.pallas_call(
        wrapped, name=name, grid=grid,
        in_specs=list(in_specs) + [hbm] * (n_g + 4 * n_r), out_specs=list(out_specs) + [hbm] * (n_g + 4 * n_r),
        out_shape=list(out_shape) + [jax.ShapeDtypeStruct((N_CHIPS,) + g.shape, BF16) for g in gather]
        + ([jax.ShapeDtypeStruct(w.shape, F32) for _ in range(4) for w in reduce[1]] if reduce else []),
        scratch_shapes=list(scratch_shapes) + gather_sems + (_reduce_scratch() if reduce else []),
        compiler_params=_params(("arbitrary",) * len(grid), vmem_mib),
    )(*args, *gather, *reduce_args)
    if not reduce:
        return res
    plain = list(res[:n_out + n_g])
    return plain + [res[n_out + n_g + i * n_r:n_out + n_g + (i + 1) * n_r] for i in range(4)]


def _allgather_weights(shards, small, casts):
    n = len(shards)
    cast_out = [(k, r0, r1) for k, (_, ranges) in enumerate(casts) for r0, r1 in ranges]
    n_c, n_co = len(casts), len(cast_out)

    def body(*refs):
        ins, small_in, cast_in = refs[:n], refs[n], refs[n + 1:n + 1 + n_c]
        refs = refs[n + 1 + n_c:]
        outs, small_out, cast_dst = refs[:n], refs[n], refs[n + 1:n + 1 + n_co]
        refs = refs[n + 1 + n_co:]
        cast, cast_buf = refs[:n], refs[n:n + n_co]
        send_far, recv_far, send_sib, recv_sib, send_small, recv_small, local_sem, cast_sem = refs[n + n_co:]
        x, y, c, chips = _mesh_place()
        me = 2 * x + y
        sibling = (x, y, 1 - c)

        def half(k, which):
            rows = ins[k].shape[0] // 2
            return pl.ds(pl.multiple_of(which * rows, 16), rows)

        local = []
        for k in range(n):
            cast[k][...] = ins[k][...].astype(BF16)
            local.append(pltpu.make_async_copy(cast[k], outs[k].at[me], local_sem.at[k]))
            local[-1].start()
        local.append(pltpu.make_async_copy(small_in, small_out.at[me], local_sem.at[n]))
        local[-1].start()

        sends = []
        for j, (px, py) in enumerate(chips):
            for k in range(n):
                cp = pltpu.make_async_remote_copy(
                    src_ref=cast[k].at[half(k, c)], dst_ref=outs[k].at[me, half(k, c)],
                    send_sem=send_far.at[j * n + k], recv_sem=recv_far.at[j * n + k],
                    device_id=(px, py, c), device_id_type=MESH)
                cp.start()
                sends.append(cp)
            cp = pltpu.make_async_remote_copy(
                src_ref=small_in, dst_ref=small_out.at[me], send_sem=send_small.at[j], recv_sem=recv_small.at[j],
                device_id=(px, py, c), device_id_type=MESH)
            cp.start()
            sends.append(cp)

        for i, (k, r0, r1) in enumerate(cast_out):
            cast_buf[i][...] = cast_in[k][r0:r1, :].astype(BF16)
            local.append(pltpu.make_async_copy(cast_buf[i], cast_dst[i], cast_sem.at[i]))
            local[-1].start()

        def landed(j, k, which, sems_s, sems_r, device):
            px, py = chips[j]
            piece = outs[k].at[2 * px + py, half(k, which)]
            return pltpu.make_async_remote_copy(
                src_ref=piece, dst_ref=piece, send_sem=sems_s.at[j * n + k], recv_sem=sems_r.at[j * n + k],
                device_id=device, device_id_type=MESH)

        for j in range(len(chips)):
            for k in range(n):
                landed(j, k, c, send_far, recv_far, sibling).wait_recv()
                cp = landed(j, k, c, send_sib, recv_sib, sibling)
                cp.start()
                sends.append(cp)
        for j, (px, py) in enumerate(chips):
            for k in range(n):
                landed(j, k, 1 - c, send_sib, recv_sib, sibling).wait_recv()
            pltpu.make_async_remote_copy(
                src_ref=small_in, dst_ref=small_out.at[2 * px + py], send_sem=send_small.at[j],
                recv_sem=recv_small.at[j], device_id=(px, py, c), device_id_type=MESH).wait_recv()
        for cp in sends:
            cp.wait_send()
        for cp in local:
            cp.wait()

    vmem = pl.BlockSpec(memory_space=pltpu.VMEM)
    hbm = pl.BlockSpec(memory_space=pltpu.HBM)
    cast_shapes = [(r1 - r0, casts[k][0].shape[1]) for k, r0, r1 in cast_out]
    res = pl.pallas_call(
        body, name="allgather_weights",
        in_specs=[vmem] * (n + 1 + n_c), out_specs=[hbm] * (n + 1 + n_co),
        out_shape=[jax.ShapeDtypeStruct((N_CHIPS,) + s.shape, BF16) for s in shards]
        + [jax.ShapeDtypeStruct((N_CHIPS,) + small.shape, F32)]
        + [jax.ShapeDtypeStruct(s, BF16) for s in cast_shapes],
        scratch_shapes=[pltpu.VMEM(s.shape, BF16) for s in shards] + [pltpu.VMEM(s, BF16) for s in cast_shapes]
        + [pltpu.SemaphoreType.DMA((3 * n,)), pltpu.SemaphoreType.DMA((3 * n,)),
           pltpu.SemaphoreType.DMA((3 * n,)), pltpu.SemaphoreType.DMA((3 * n,)),
           pltpu.SemaphoreType.DMA((3,)), pltpu.SemaphoreType.DMA((3,)),
           pltpu.SemaphoreType.DMA((n + 1,)), pltpu.SemaphoreType.DMA((n_co,))],
        compiler_params=_params(None, 40),
    )(*shards, small, *[a for a, _ in casts])
    return res[:n], res[n], res[n + 1:]


def _adamw(w, g, m, v):
    m = ADAM_B1 * m + (1.0 - ADAM_B1) * g
    v = ADAM_B2 * v + (1.0 - ADAM_B2) * (g * g)
    m_hat = m / (1.0 - ADAM_B1 ** ADAM_STEP)
    v_hat = v / (1.0 - ADAM_B2 ** ADAM_STEP)
    delta = -ADAM_LR * (m_hat / (jnp.sqrt(v_hat) + ADAM_EPS) + ADAM_WD * w)
    return delta, m, v


RS_PIECE_ROWS = 128
RS_PIECE_COLS = 512


def _reduce_adam_all(grads, ws, ms, vs, small, bases=None, seeds=None):
    n_w = len(grads)
    pieces = _reduce_pieces(grads, bases)
    n_small = len(_small_sum_scratch(small.shape))
    seeds = seeds or [None] * n_w
    seeded = [(k, a) for k, seed in enumerate(seeds) if seed is not None for a in range(4)]
    n_in = 4 * n_w + 1

    def body(*refs):
        refs = list(refs)
        del refs[n_in:n_in + len(seeded)]
        small_in = refs.pop(4 * n_w)
        small_out = refs.pop(8 * n_w)
        small_scratch = [refs.pop() for _ in range(n_small)][::-1]
        sends = _small_sum_start(small_in, *small_scratch)
        ticks, drain = _reduce_ticks(pieces, n_w, refs)
        for tick in ticks:
            tick()
        drain()
        _small_sum_finish(sends, small_scratch[0], small_out)

    hbm = pl.BlockSpec(memory_space=pltpu.HBM)
    vmem = pl.BlockSpec(memory_space=pltpu.VMEM)
    outs = pl.pallas_call(
        body, name="reduce_adam_all",
        in_specs=[hbm] * (4 * n_w) + [vmem] + [hbm] * len(seeded), out_specs=[hbm] * (4 * n_w) + [vmem],
        out_shape=[jax.ShapeDtypeStruct(w.shape, F32) for _ in range(4) for w in ws]
        + [jax.ShapeDtypeStruct(small.shape, F32)],
        input_output_aliases={n_in + i: a * n_w + k for i, (k, a) in enumerate(seeded)},
        scratch_shapes=_reduce_scratch() + _small_sum_scratch(small.shape),
        compiler_params=_params(None, 48),
    )(*grads, *ws, *ms, *vs, small, *[seeds[k][a] for k, a in seeded])
    return [outs[i * n_w:(i + 1) * n_w] for i in range(4)], outs[4 * n_w]


def _reduce_pieces(grads, bases=None):
    pieces = []
    for k, g in enumerate(grads):
        hr, cols = g.shape[1] // 2, g.shape[2]
        pr, pc = min(hr, RS_PIECE_ROWS), min(cols, RS_PIECE_COLS)
        base = bases[k] if bases else 0
        pieces += [(k, ro, hr, co, pr, pc, base) for ro in range(0, hr, pr) for co in range(0, cols, pc)]
    return pieces


def _reduce_scratch():
    P, C = RS_PIECE_ROWS, RS_PIECE_COLS
    return [
        pltpu.VMEM((3, N_CHIPS, P, C), F32), pltpu.VMEM((3, N_CHIPS, P, C), F32),
        pltpu.VMEM((2, N_CHIPS, P, C), BF16), pltpu.VMEM((2, N_CHIPS, P, C), BF16),
        pltpu.VMEM((2, N_CHIPS, P, C), F32),
        pltpu.VMEM((2, 3, P, C), BF16), pltpu.VMEM((2, 3, P, C), BF16),
        pltpu.VMEM((2, 2, P, C), F32),
        pltpu.VMEM((2, 3, 2, P, C), F32), pltpu.VMEM((2, 4, 2, P, C), F32),
        pltpu.SemaphoreType.DMA((3, 2)), pltpu.SemaphoreType.DMA((2, 3, 2)),
        pltpu.SemaphoreType.DMA((2,)), pltpu.SemaphoreType.DMA((2,)),
        pltpu.SemaphoreType.DMA((2, 3)), pltpu.SemaphoreType.DMA((2, 3)),
        pltpu.SemaphoreType.DMA((2,)), pltpu.SemaphoreType.DMA((2,)),
        pltpu.SemaphoreType.DMA((2, 4, 2))]


def _reduce_ticks(pieces, n_w, refs):
    n = len(pieces)

    def build(*refs):
        g_in, w_in, m_in, v_in = (refs[i * n_w:(i + 1) * n_w] for i in range(4))
        g_out, d_out, m_out, v_out = (refs[(4 + i) * n_w:(5 + i) * n_w] for i in range(4))
        (gm, go, sb1, rb1, part, sb2, rb2, fin, wmv, outs,
         ld_sem, wmv_sem, s1_send, s1_recv, s2_send, s2_recv, s3_send, s3_recv, out_sem) = refs[8 * n_w:]
        x, y, c, chips = _mesh_place()
        me = 2 * x + y
        sibling = (x, y, 1 - c)

        def at_hbm(i, which, in_shard):
            _, ro, hr, co, pr, pc, base = pieces[i]
            half = c if which == 0 else 1 - c
            return pl.ds(pl.multiple_of((base if in_shard else 0) + half * hr + ro, 64), pr), pl.ds(co, pc)

        def win(i):
            return pl.ds(0, pieces[i][4]), pl.ds(0, pieces[i][5])

        every = slice(None)

        def loads(i):
            k, s = pieces[i][0], i % 3
            return [pltpu.make_async_copy(g_in[k].at[(every,) + at_hbm(i, h, False)], buf.at[(s, every) + win(i)],
                                          ld_sem.at[s, h])
                    for h, buf in enumerate((gm, go))]

        def wmv_loads(i):
            k, s = pieces[i][0], i % 2
            return [pltpu.make_async_copy(src[k].at[at_hbm(i, h, True)], wmv.at[(s, a, h) + win(i)], wmv_sem.at[s, a, h])
                    for a, src in enumerate((w_in, m_in, v_in)) for h in range(2)]

        def stores(i):
            k, s = pieces[i][0], i % 2
            return [pltpu.make_async_copy(outs.at[(s, a, h) + win(i)], dst[k].at[at_hbm(i, h, True)], out_sem.at[s, a, h])
                    for a, dst in enumerate((g_out, d_out, m_out, v_out)) for h in range(2)]

        def swap1(i):
            s = i % 2
            return pltpu.make_async_remote_copy(
                src_ref=sb1.at[(s, every) + win(i)], dst_ref=rb1.at[(s, every) + win(i)],
                send_sem=s1_send.at[s], recv_sem=s1_recv.at[s], device_id=sibling, device_id_type=MESH)

        def far2(i, j):
            s = i % 2
            px, py = chips[j]
            return pltpu.make_async_remote_copy(
                src_ref=sb2.at[(s, j) + win(i)], dst_ref=rb2.at[(s, j) + win(i)],
                send_sem=s2_send.at[s, j], recv_sem=s2_recv.at[s, j], device_id=(px, py, c), device_id_type=MESH)

        def swap3(i):
            s = i % 2
            return pltpu.make_async_remote_copy(
                src_ref=fin.at[(s, 0) + win(i)], dst_ref=fin.at[(s, 1) + win(i)],
                send_sem=s3_send.at[s], recv_sem=s3_recv.at[s], device_id=sibling, device_id_type=MESH)

        def stage0(i):
            for cp in loads(i):
                cp.start()

        def stage1(i):
            s, s3 = i % 2, i % 3
            for cp in loads(i):
                cp.wait()
            sb1[(s, every) + win(i)] = go[(s3, every) + win(i)].astype(BF16)
            swap1(i).start()

        def stage2(i):
            s, s3 = i % 2, i % 3
            swap1(i).wait()
            part[(s, every) + win(i)] = gm[(s3, every) + win(i)] + rb1[(s, every) + win(i)].astype(F32)
            for j, (px, py) in enumerate(chips):
                sb2[(s, j) + win(i)] = part[(s, 2 * px + py) + win(i)].astype(BF16)
                far2(i, j).start()

        def stage3(i):
            s = i % 2
            total = part[(s, me) + win(i)]
            for j in range(3):
                far2(i, j).wait()
                total = total + rb2[(s, j) + win(i)].astype(F32)
            fin[(s, 0) + win(i)] = total
            swap3(i).start()
            for cp in wmv_loads(i):
                cp.start()

        def stage4(i):
            s = i % 2
            if i >= 2:
                for cp in stores(i - 2):
                    cp.wait()
            swap3(i).wait()
            for cp in wmv_loads(i):
                cp.wait()
            both = (every,) + win(i)
            g = fin[(s,) + both]
            delta, m_new, v_new = _adamw(wmv[(s, 0) + both], g, wmv[(s, 1) + both], wmv[(s, 2) + both])
            outs[(s, 0) + both] = g
            outs[(s, 1) + both] = delta
            outs[(s, 2) + both] = m_new
            outs[(s, 3) + both] = v_new
            for cp in stores(i):
                cp.start()

        stages = (stage0, stage1, stage2, stage3, stage4)

        def tick(t):
            for age in reversed(range(len(stages))):
                if 0 <= t - age < n:
                    stages[age](t - age)

        def drain():
            for i in range(max(0, n - 2), n):
                for cp in stores(i):
                    cp.wait()

        return [functools.partial(tick, t) for t in range(n + len(stages) - 1)], drain

    return build(*refs)


N_DEVICES = 8


def _small_sum_scratch(shape):
    return [pltpu.VMEM((N_DEVICES,) + shape, F32),
            pltpu.SemaphoreType.DMA((N_DEVICES - 1,)), pltpu.SemaphoreType.DMA((N_DEVICES - 1,))]


def _small_sum_start(part_ref, buf, send_sem, recv_sem):
    x, y, c, _ = _mesh_place()
    me = 4 * x + 2 * y + c
    buf[me] = part_ref[...]
    sends = []
    for k in range(1, N_DEVICES):
        peer = ((1 - x) if k & 4 else x, (1 - y) if k & 2 else y, (1 - c) if k & 1 else c)
        cp = pltpu.make_async_remote_copy(src_ref=part_ref, dst_ref=buf.at[me], send_sem=send_sem.at[k - 1],
                                          recv_sem=recv_sem.at[k - 1], device_id=peer, device_id_type=MESH)
        cp.start()
        sends.append(cp)
    return sends


def _small_sum_finish(sends, buf, out_ref):
    for cp in sends:
        cp.wait_recv()
    total = buf[0]
    for s in range(1, N_DEVICES):
        total = total + buf[s]
    out_ref[...] = total
    for cp in sends:
        cp.wait_send()


def _adam_small(w, g, m, v):
    def body(w_ref, g_ref, m_ref, v_ref, d_ref, mo_ref, vo_ref):
        delta, m_new, v_new = _adamw(w_ref[...], g_ref[...], m_ref[...], v_ref[...])
        d_ref[...] = delta
        mo_ref[...] = m_new
        vo_ref[...] = v_new

    vmem = pl.BlockSpec(memory_space=pltpu.VMEM)
    return pl.pallas_call(
        body, name="adam_small", in_specs=[vmem] * 4, out_specs=[vmem] * 3,
        out_shape=[jax.ShapeDtypeStruct(w.shape, F32)] * 3,
    )(w, g, m, v)


BIG = ("a_w_in", "a_w_group", "a_w_out", "w_kv", "b_w_in", "b_w_out", "ple_w", "ple_gate_w")
SMALL = ("a_norm", "a_scale", "kv_norm", "b_norm", "k_norm", "b_q_norm")
SMALL_SHARDED = ("a_norm", "a_scale")
WEIGHTS = ("a_norm", "a_w_in", "a_w_group", "a_scale", "a_w_out", "kv_norm", "w_kv", "k_norm", "b_norm", "b_w_in",
           "b_q_norm", "b_w_out", "ple_w", "ple_gate_w")


def _as_matrix(a):
    return a.reshape(-1, a.shape[-1])


def _pack_small(arrs):
    rows = [jnp.pad(a.reshape(-1), (0, D_MODEL - a.size)) for a in arrs]
    rows += [jnp.zeros((D_MODEL,), F32)] * (8 - len(rows))
    return jnp.stack(rows)


def kernel(x, p, a_norm, a_w_in, a_w_group, a_scale, a_w_out, kv_norm, w_kv, k_norm, b_norm, b_w_in, b_q_norm, b_w_out, ple_w, ple_gate_w, loss_target, m_a_norm, m_a_w_in, m_a_w_group, m_a_scale, m_a_w_out, m_kv_norm, m_w_kv, m_k_norm, m_b_norm, m_b_w_in, m_b_q_norm, m_b_w_out, m_ple_w, m_ple_gate_w, v_a_norm, v_a_w_in, v_a_w_group, v_a_scale, v_a_w_out, v_kv_norm, v_w_kv, v_k_norm, v_b_norm, v_b_w_in, v_b_q_norm, v_b_w_out, v_ple_w, v_ple_gate_w):
    wts = dict(a_norm=a_norm, a_w_in=a_w_in, a_w_group=a_w_group, a_scale=a_scale, a_w_out=a_w_out, kv_norm=kv_norm,
               w_kv=w_kv, k_norm=k_norm, b_norm=b_norm, b_w_in=b_w_in, b_q_norm=b_q_norm, b_w_out=b_w_out,
               ple_w=ple_w, ple_gate_w=ple_gate_w)
    mom = dict(a_norm=m_a_norm, a_w_in=m_a_w_in, a_w_group=m_a_w_group, a_scale=m_a_scale, a_w_out=m_a_w_out,
               kv_norm=m_kv_norm, w_kv=m_w_kv, k_norm=m_k_norm, b_norm=m_b_norm, b_w_in=m_b_w_in,
               b_q_norm=m_b_q_norm, b_w_out=m_b_w_out, ple_w=m_ple_w, ple_gate_w=m_ple_gate_w)
    var = dict(a_norm=v_a_norm, a_w_in=v_a_w_in, a_w_group=v_a_w_group, a_scale=v_a_scale, a_w_out=v_a_w_out,
               kv_norm=v_kv_norm, w_kv=v_w_kv, k_norm=v_k_norm, b_norm=v_b_norm, b_w_in=v_b_w_in,
               b_q_norm=v_b_q_norm, b_w_out=v_b_w_out, ple_w=v_ple_w, ple_gate_w=v_ple_gate_w)
    S = x.shape[1]
    chip = 2 * lax.axis_index("x") + lax.axis_index("y")

    sharded_small = jnp.concatenate([a_norm.reshape(1, 256), a_scale.reshape(1, 256), jnp.zeros((6, 256), F32)], axis=0)
    later = ("a_w_group", "a_w_out", "w_kv", "b_w_in", "b_w_out", "ple_w", "ple_gate_w")
    (a_w_in_full,), small_full, copies = _allgather_weights(
        [_as_matrix(a_w_in)], sharded_small,
        [(_as_matrix(wts[n]), [(0, 256), (256, 512)] if n.startswith("ple") else [(0, _as_matrix(wts[n]).shape[0])])
         for n in later])
    local = dict(zip(("a_w_group", "a_w_out", "w_kv", "b_w_in", "b_w_out", "ple_w0", "ple_w1", "ple_gate_w0",
                      "ple_gate_w1"), copies))
    full = dict(a_w_in=a_w_in_full,
                a_norm=small_full[:, 0, :].reshape(1, D_MODEL), a_scale=small_full[:, 1, :].reshape(1, D_MODEL),
                kv_norm=kv_norm.reshape(1, D_MODEL), b_norm=b_norm.reshape(1, D_MODEL), k_norm=k_norm, b_q_norm=b_q_norm)

    def shards(t):
        out = {}
        for n in BIG:
            for entry in ((n + "0", n + "1") if n.startswith("ple") else (n,)):
                out[entry] = _as_matrix(t[n])
        return out

    base = {n: 256 if n.startswith("ple") and n.endswith("1") else 0 for n in shards(wts)}
    state = (shards(wts), shards(mom), shards(var), base)
    grad_x, grads, updates, small_part = _local_step(x.reshape(S, D_MODEL), p, loss_target.reshape(S, D_MODEL),
                                                     full, local, state)

    names = sorted(grads)
    reduced, small_sum = _reduce_adam_all(
        [grads[n] for n in names], *[[t[n] for n in names] for t in state[:3]], small_part,
        bases=[base[n] for n in names], seeds=[updates.get(n[:-1] + "1") if n.startswith("ple") else None for n in names])
    for i, n in enumerate(names):
        updates[n] = tuple(group[i] for group in reduced)
    out_g, out_d, out_m, out_v = {}, {}, {}, {}
    for n in BIG:
        for i, out in enumerate((out_g, out_d, out_m, out_v)):
            out[n] = updates[n + "0" if n.startswith("ple") else n][i].reshape(wts[n].shape)

    loss = small_sum[len(SMALL), 0]
    small_rows = []
    for i, n in enumerate(SMALL):
        row = small_sum[i]
        if n in SMALL_SHARDED:
            row = lax.dynamic_slice(row, (chip * 256,), (256,))
        else:
            row = row[:wts[n].size]
        small_rows.append(row)
    g_small = _pack_small(small_rows)
    d_small, m_small, v_small = _adam_small(_pack_small([wts[n] for n in SMALL]), g_small,
                                            _pack_small([mom[n] for n in SMALL]), _pack_small([var[n] for n in SMALL]))
    for i, n in enumerate(SMALL):
        shape, size = wts[n].shape, wts[n].size
        out_g[n], out_d[n], out_m[n], out_v[n] = (t[i, :size].reshape(shape) for t in (g_small, d_small, m_small, v_small))

    return (loss, grad_x.reshape(1, S, D_MODEL), *[out_g[n] for n in WEIGHTS], *[out_d[n] for n in WEIGHTS],
            *[out_m[n] for n in WEIGHTS], *[out_v[n] for n in WEIGHTS])
```

```python
import functools

import jax
import jax.numpy as jnp
from jax import lax
from jax.experimental import pallas as pl
from jax.experimental.pallas import tpu as pltpu

F32 = jnp.float32
BF16 = jnp.bfloat16
MESH = pl.DeviceIdType.MESH

D_MODEL = 1024
N_HEADS = 16
HEAD_DIM = 64
PLE_DIM = 256
N_GROUPS = 4
GROUP_DIM = 256
POOL_WINDOWS = (2, 4, 8, 16)
N_CHIPS = 4
EPS = 1e-6
SB_SCALE = HEAD_DIM ** -0.5

ADAM_LR = 0.001
ADAM_B1 = 0.9
ADAM_B2 = 0.999
ADAM_EPS = 1e-08
ADAM_WD = 0.01
ADAM_STEP = 10

ROW_TILE = 256
WIDE_ROW_TILE = 512
EXP_UNDERFLOW = -104.0
ATT_Q_TILE = 512
ATT_K_TILE = 256
WGRAD_SEQ_TILE = 1024
WGRAD_ACC_BYTES = 4 * 1024 * 1024
MIB = 1024 * 1024


def _params(semantics=None, vmem_mib=48):
    return pltpu.CompilerParams(dimension_semantics=semantics, vmem_limit_bytes=vmem_mib * MIB)


def _dot(a, b):
    return jnp.dot(a, b, preferred_element_type=F32)


def _dot_nt(a, b):
    return lax.dot_general(a, b, (((1,), (1,)), ((), ())), preferred_element_type=F32)


def _dot_tn(a, b):
    return lax.dot_general(a, b, (((0,), (0,)), ((), ())), preferred_element_type=F32)


def _hilo(x):
    hi = x.astype(BF16)
    lo = (x - hi.astype(F32)).astype(BF16)
    return hi, lo


def _dot_hilo(x, w):
    hi, lo = _hilo(x)
    return _dot(hi, w) + _dot(lo, w)


def _sigmoid(z):
    return jax.nn.sigmoid(z)


def _dsilu(z, sg):
    return sg * (1.0 + z * (1.0 - sg))


def _mask_bf16(cond):
    return jnp.where(cond, 1.0, 0.0).astype(BF16)


def _head_mean_matrix():
    r = lax.broadcasted_iota(jnp.int32, (256, 256), 0) // HEAD_DIM
    c = lax.broadcasted_iota(jnp.int32, (256, 256), 1) // HEAD_DIM
    return _mask_bf16(r == c)


def _head_mean(x, bd):
    parts = []
    for s in range(x.shape[1] // 256):
        parts.append(_dot_hilo(x[:, s * 256:(s + 1) * 256], bd))
    out = parts[0] if len(parts) == 1 else jnp.concatenate(parts, axis=1)
    return out * (1.0 / HEAD_DIM)


def _a_in(x, gain, w_sh, gather=()):
    S = x.shape[0]
    tm = 512
    nsh, _, wn = w_sh.shape

    def body(x_ref, g_ref, w_ref, uz_ref, h_ref):
        @pl.when(pl.program_id(1) == 0)
        def _():
            xv = x_ref[...]
            r = lax.rsqrt(jnp.mean(xv * xv, axis=-1, keepdims=True) + EPS)
            h_ref[...] = (xv * r * g_ref[...]).astype(BF16)

        uz_ref[...] = _dot(h_ref[...], w_ref[0])

    return _call_with_gather(
        body, name="a_in", grid=(S // tm, nsh),
        in_specs=[pl.BlockSpec((tm, D_MODEL), lambda i, j: (i, 0)),
                  pl.BlockSpec((1, D_MODEL), lambda i, j: (0, 0)),
                  pl.BlockSpec((1, D_MODEL, wn), lambda i, j: (j, 0, 0))],
        out_specs=[pl.BlockSpec((tm, wn), lambda i, j: (i, j)),
                   pl.BlockSpec((tm, D_MODEL), lambda i, j: (i, 0))],
        out_shape=[jax.ShapeDtypeStruct((S, nsh * wn), F32),
                   jax.ShapeDtypeStruct((S, D_MODEL), BF16)],
        args=(x, gain, w_sh), gather=gather)


def _inv_count(first_row, rows, w):
    t1 = first_row + 1 + lax.broadcasted_iota(jnp.int32, (rows, 1), 0)
    return 1.0 / jnp.minimum(t1, w).astype(F32)


def _group_weight(wg_ref, g):
    return jnp.concatenate([wg_ref[sh, g] for sh in range(N_CHIPS)], axis=0)


def _a_mix(uz, wg, scale, gather=()):
    S = uz.shape[0]
    tm = ROW_TILE

    def body(u_ref, up_ref, z_ref, wg_ref, sc_ref, ga_ref, p_ref):
        i = pl.program_id(0)
        row = lax.broadcasted_iota(jnp.int32, (tm, tm), 0)
        col = lax.broadcasted_iota(jnp.int32, (tm, tm), 1)
        d = row - col
        for g, w in enumerate(POOL_WINDOWS):
            cols = slice(g * GROUP_DIM, (g + 1) * GROUP_DIM)
            t_main = _mask_bf16((d >= 0) & (d < w))
            t_halo = _mask_bf16(d + tm < w)
            u = u_ref[:, cols]
            up = jnp.where(i > 0, up_ref[:, cols], 0.0)
            hi, lo = _hilo(u)
            hip, lop = _hilo(up)
            wsum = _dot(t_main, hi) + _dot(t_main, lo) + _dot(t_halo, hip) + _dot(t_halo, lop)
            pooled = (wsum * _inv_count(i * tm, tm, w) - u).astype(BF16)
            p_ref[:, cols] = pooled
            mraw = _dot(pooled, _group_weight(wg_ref, g))
            z = z_ref[:, cols]
            ga_ref[:, cols] = (mraw * sc_ref[:, cols] * (z * _sigmoid(z))).astype(BF16)

    return _call_with_gather(
        body, name="a_mix", grid=(S // tm,),
        in_specs=[pl.BlockSpec((tm, D_MODEL), lambda i: (i, 0)),
                  pl.BlockSpec((tm, D_MODEL), lambda i: (jnp.maximum(i - 1, 0), 0)),
                  pl.BlockSpec((tm, D_MODEL), lambda i: (i, 1)),
                  pl.BlockSpec((N_CHIPS, N_GROUPS, 64, GROUP_DIM), lambda i: (0, 0, 0, 0)),
                  pl.BlockSpec((1, D_MODEL), lambda i: (0, 0))],
        out_specs=[pl.BlockSpec((tm, D_MODEL), lambda i: (i, 0)),
                   pl.BlockSpec((tm, D_MODEL), lambda i: (i, 0))],
        out_shape=[jax.ShapeDtypeStruct((S, D_MODEL), BF16),
                   jax.ShapeDtypeStruct((S, D_MODEL), BF16)],
        args=(uz, uz, uz, wg, scale), gather=gather)


def _out_ple(name, gated, x_in, w_out, p, layer, ple_w, ple_g, target=None, gather=()):
    S = x_in.shape[0]
    tm = WIDE_ROW_TILE
    with_loss = target is not None

    def body(*refs):
        if with_loss:
            g_ref, x_ref, wo_ref, p_ref, pw_ref, pg_ref, t_ref, xm_ref, dx_ref, e_ref, gt_ref, loss_ref = refs
        else:
            g_ref, x_ref, wo_ref, p_ref, pw_ref, pg_ref, xm_ref, xo_ref, e_ref, gt_ref = refs
        xm = x_ref[...] + _dot(g_ref[...], wo_ref[...])
        xm_ref[...] = xm
        pb = p_ref[...].astype(BF16)
        e = jnp.concatenate([_dot(pb, pw_ref[sh]) for sh in range(N_CHIPS)], axis=1)
        pg = jnp.concatenate([pg_ref[sh] for sh in range(N_CHIPS)], axis=0)
        gate = _sigmoid(_dot(xm.astype(BF16), pg))
        e_ref[...] = e.astype(BF16)
        gt_ref[...] = gate.astype(BF16)
        xo = xm + e * gate
        if with_loss:
            diff = xo - t_ref[...]
            dx_ref[...] = diff * (1.0 / D_MODEL)

            @pl.when(pl.program_id(0) == 0)
            def _():
                loss_ref[...] = jnp.zeros_like(loss_ref)

            loss_ref[...] += jnp.sum(diff * diff) * (0.5 / D_MODEL)
        else:
            xo_ref[...] = xo

    row = pl.BlockSpec((tm, D_MODEL), lambda i: (i, 0))
    in_specs = [row, row,
                pl.BlockSpec((D_MODEL, D_MODEL), lambda i: (0, 0)),
                pl.BlockSpec((None, None, tm, PLE_DIM), lambda i: (layer, 0, i, 0)),
                pl.BlockSpec((N_CHIPS, PLE_DIM, 256), lambda i: (0, 0, 0)),
                pl.BlockSpec((N_CHIPS, 256, D_MODEL), lambda i: (0, 0, 0))]
    args = [gated, x_in, w_out, p, ple_w, ple_g]
    out_specs = [row, row, row, row]
    out_shape = [jax.ShapeDtypeStruct((S, D_MODEL), F32), jax.ShapeDtypeStruct((S, D_MODEL), F32),
                 jax.ShapeDtypeStruct((S, D_MODEL), BF16), jax.ShapeDtypeStruct((S, D_MODEL), BF16)]
    if with_loss:
        in_specs.append(row)
        args.append(target)
        out_specs.append(pl.BlockSpec((8, 128), lambda i: (0, 0)))
        out_shape.append(jax.ShapeDtypeStruct((8, 128), F32))
    return _call_with_gather(body, name=name, grid=(S // tm,), in_specs=in_specs, out_specs=out_specs,
                             out_shape=out_shape, args=args, gather=gather)


def _b_in(x, kv_gain, b_gain, k_gain_t, q_gain_t, w_kv, w_in, gather=()):
    S = x.shape[0]
    tm = ROW_TILE

    def body(x_ref, kvg_ref, bg_ref, kg_ref, qg_ref, wkv_ref, win_ref,
             hkv_ref, hb_ref, kraw_ref, qraw_ref, k_ref, q_ref, v_ref, z_ref):
        xv = x_ref[...]
        y = xv * lax.rsqrt(jnp.mean(xv * xv, axis=-1, keepdims=True) + EPS)
        hkv = (y * kvg_ref[...]).astype(BF16)
        hb = (y * bg_ref[...]).astype(BF16)
        hkv_ref[...] = hkv
        hb_ref[...] = hb
        bd = _head_mean_matrix()

        def head_norm(raw, gain):
            rr = lax.rsqrt(_head_mean(raw * raw, bd) + EPS)
            return raw * rr * gain

        for sh in range(N_CHIPS):
            kvc = _dot(hkv, wkv_ref[sh])
            qzc = _dot(hb, win_ref[sh])
            cols = slice((sh % 2) * 512, (sh % 2) * 512 + 512)
            if sh < 2:
                kraw_ref[:, cols] = kvc.astype(BF16)
                qraw_ref[:, cols] = qzc.astype(BF16)
                k_ref[:, cols] = head_norm(kvc, kg_ref[:, cols]).astype(BF16)
                q_ref[:, cols] = (head_norm(qzc, qg_ref[:, cols]) * SB_SCALE).astype(BF16)
            else:
                v_ref[:, cols] = kvc.astype(BF16)
                z_ref[:, cols] = qzc.astype(BF16)

    row = pl.BlockSpec((tm, D_MODEL), lambda i: (i, 0))
    vec = pl.BlockSpec((1, D_MODEL), lambda i: (0, 0))
    wsp = pl.BlockSpec((N_CHIPS, D_MODEL, 512), lambda i: (0, 0, 0))
    return _call_with_gather(
        body, name="b_in", grid=(S // tm,),
        in_specs=[row, vec, vec, vec, vec, wsp, wsp],
        out_specs=[row] * 8,
        out_shape=[jax.ShapeDtypeStruct((S, D_MODEL), BF16)] * 8,
        args=(x, kv_gain, b_gain, k_gain_t, q_gain_t, w_kv, w_in), gather=gather, vmem_mib=56)


def _softplus_parts(z):
    e = jnp.exp(-jnp.abs(z))
    return -(jnp.maximum(z, 0.0) + jnp.log(1.0 + e)), e


def _add_rows(total, rows, update):
    lo, hi = rows
    parts = ([total[:lo]] if lo else []) + [total[lo:hi] + update] + ([total[hi:]] if hi < total.shape[0] else [])
    return parts[0] if len(parts) == 1 else jnp.concatenate(parts, axis=0)


def _attn_fwd(q, k, v, zgate, gather=()):
    S = q.shape[0]
    tq, tk = ATT_Q_TILE, ATT_K_TILE
    kpq = tq // tk
    assert kpq == 2

    def body(q_ref, k_ref, v_ref, z_ref, o_ref, g_ref, lt_ref, steps_ref):
        qi = pl.program_id(1)
        lane = lax.broadcasted_iota(jnp.int32, (1, 128), 1)
        ri = lax.broadcasted_iota(jnp.int32, (tk, tk), 0)
        ci = lax.broadcasted_iota(jnp.int32, (tk, tk), 1)
        later_mat = _mask_bf16(ri > ci)
        causal = ci < ri
        qv = q_ref[...]
        first = lane < HEAD_DIM
        q_heads = (jnp.where(first, qv, jnp.zeros_like(qv)), jnp.where(first, jnp.zeros_like(qv), qv))

        def step(blocks, carry):
            chains = [(b, h) for b in range(len(blocks)) for h in range(2)]
            rows = [r for _, r, _ in blocks]
            s0 = [pl.multiple_of(kj * tk, tk) for kj, _, _ in blocks]
            kb = [k_ref[pl.ds(s, tk), :] for s in s0]
            vb = [v_ref[pl.ds(s, tk), :] for s in s0]
            visible = [causal if masked else None for _, _, masked in blocks]
            z = {c: _dot_nt(q_heads[c[1]][rows[c[0]][0]:rows[c[0]][1]], kb[c[0]]) for c in chains}
            run = [carry[0], carry[2]]
            log_own, later, run_at = {}, {}, {}
            for c in chains:
                b, h = c
                lk = _softplus_parts(z[c])[0]
                if visible[b] is not None:
                    lk = jnp.where(visible[b], lk, 0.0)
                log_own[c] = z[c] + lk
                later[c] = _dot(lk.astype(BF16), later_mat)
                run_at[c] = run[h][rows[b][0]:rows[b][1]]
                run[h] = _add_rows(run[h], rows[b], jnp.sum(lk, axis=-1, keepdims=True))
            acc = [carry[1], carry[3]]
            for c in chains:
                b, h = c
                a = jnp.exp(log_own[c] + later[c] + run_at[c])
                if visible[b] is not None:
                    a = jnp.where(visible[b], a, 0.0)
                acc[h] = _add_rows(acc[h], rows[b], _dot(a.astype(BF16), vb[b]))
            return run[0], acc[0], run[1], acc[1]

        zero1, zero128 = jnp.zeros((tq, 1), F32), jnp.zeros((tq, 128), F32)
        carry = step([(qi * kpq + 1, (tk, tq), True), (qi * kpq, (tk, tq), False), (qi * kpq, (0, tk), True)],
                     (zero1, zero128, zero1, zero128))

        def low(run):
            return jnp.max(run)

        def pair_more(c):
            return (c[0] < qi) & (jnp.maximum(low(c[1][tk:]), low(c[3][tk:])) > EXP_UNDERFLOW)

        def pair_step(c):
            last = (qi - c[0]) * kpq - 1
            return (c[0] + 1, *step([(last, (0, tq), False), (last - 1, (0, tq), False)], c[1:]))

        pairs, *carry = lax.while_loop(pair_more, pair_step, (jnp.int32(0), *carry))
        left = (qi - pairs) * kpq

        def single_more(c):
            return (c[0] < left) & (jnp.maximum(low(c[1][:tk]), low(c[3][:tk])) > EXP_UNDERFLOW)

        def single_step(c):
            return (c[0] + 1, *step([(left - 1 - c[0], (0, tk), False)], c[1:]))

        singles, *carry = lax.while_loop(single_more, single_step, (jnp.int32(0), *carry))
        steps_ref[...] = jnp.concatenate([jnp.full((4, 128), pairs, F32), jnp.full((4, 128), singles, F32)], axis=0)
        o_tot = jnp.where(first, carry[1], carry[3])
        l_tot = jnp.where(first, carry[0], carry[2])
        o_ref[...] = o_tot.astype(BF16)
        lt_ref[...] = l_tot
        zz = z_ref[...].astype(F32)
        g_ref[...] = (o_tot * (zz * _sigmoid(zz))).astype(BF16)

    blk = pl.BlockSpec((tq, 128), lambda hp, qi: (qi, hp))
    seq = pl.BlockSpec((S, 128), lambda hp, qi: (0, hp))
    return _call_with_gather(
        body, name="attn_fwd", grid=(D_MODEL // 128, S // tq),
        in_specs=[blk, seq, seq, blk],
        out_specs=[blk, blk, blk, pl.BlockSpec((None, None, 8, 128), lambda hp, qi: (hp, qi, 0, 0))],
        out_shape=[jax.ShapeDtypeStruct((S, D_MODEL), BF16)] * 2 + [jax.ShapeDtypeStruct((S, D_MODEL), F32)]
        + [jax.ShapeDtypeStruct((D_MODEL // 128, S // tq, 8, 128), F32)],
        args=(q, k, v, zgate), gather=gather)


def _ple_out_bwd(name, dx_out, e, gate, ple_g, w_out):
    S = dx_out.shape[0]
    tm = WIDE_ROW_TILE

    def body(dx_ref, e_ref, gt_ref, pg_ref, wo_ref, de_ref, dgp_ref, dxm_ref, dg_ref):
        dxo = dx_ref[...]
        ev = e_ref[...].astype(F32)
        gv = gt_ref[...].astype(F32)
        de_ref[...] = (dxo * gv).astype(BF16)
        dgp = (dxo * ev * gv * (1.0 - gv)).astype(BF16)
        dgp_ref[...] = dgp
        pg = jnp.concatenate([pg_ref[sh] for sh in range(N_CHIPS)], axis=0)
        dxm = dxo + _dot_nt(dgp, pg)
        dxm_ref[...] = dxm
        dg_ref[...] = _dot_nt(dxm.astype(BF16), wo_ref[...]).astype(BF16)

    row = pl.BlockSpec((tm, D_MODEL), lambda i: (i, 0))
    return pl.pallas_call(
        body, name=name, grid=(S // tm,),
        in_specs=[row, row, row,
                  pl.BlockSpec((N_CHIPS, 256, D_MODEL), lambda i: (0, 0, 0)),
                  pl.BlockSpec((D_MODEL, D_MODEL), lambda i: (0, 0))],
        out_specs=[row, row, row, row],
        out_shape=[jax.ShapeDtypeStruct((S, D_MODEL), BF16), jax.ShapeDtypeStruct((S, D_MODEL), BF16),
                   jax.ShapeDtypeStruct((S, D_MODEL), F32), jax.ShapeDtypeStruct((S, D_MODEL), BF16)],
        compiler_params=_params(("arbitrary",)),
    )(dx_out, e, gate, ple_g, w_out)


def _attn_bwd(q, k, v, ltot, steps, dgated, o, zgate, reduce=None):
    S = q.shape[0]
    tq, tk = ATT_Q_TILE, ATT_K_TILE
    kpq = tq // tk
    nq = S // tq

    def body(q_ref, k_ref, v_ref, lt_ref, steps_ref, dg_ref, o_ref, z_ref, dq_ref, dk_ref, dv_ref, dz_ref,
             dk_acc, dv_acc):
        qi = pl.program_id(1)

        @pl.when(qi == 0)
        def _():
            dk_acc[...] = jnp.zeros_like(dk_acc)
            dv_acc[...] = jnp.zeros_like(dv_acc)

        lane = lax.broadcasted_iota(jnp.int32, (1, 128), 1)
        ri = lax.broadcasted_iota(jnp.int32, (tk, tk), 0)
        ci = lax.broadcasted_iota(jnp.int32, (tk, tk), 1)
        later_mat = _mask_bf16(ri > ci)
        before_mat = _mask_bf16(ri < ci)
        causal = ci < ri
        zz = z_ref[...].astype(F32)
        sg = _sigmoid(zz)
        dgv = dg_ref[...].astype(F32)
        dz_ref[...] = (dgv * o_ref[...].astype(F32) * _dsilu(zz, sg)).astype(BF16)
        dob = (dgv * (zz * sg)).astype(BF16)
        ltv = lt_ref[...]
        qv = q_ref[...]
        first = lane < HEAD_DIM
        masks = (first, jnp.logical_not(first))
        q_heads = [jnp.where(hm, qv, jnp.zeros_like(qv)) for hm in masks]
        do_heads = [jnp.where(hm, dob, jnp.zeros_like(dob)) for hm in masks]
        totals = [jnp.max(jnp.where(hm, ltv, -jnp.inf), axis=-1, keepdims=True) for hm in masks]

        def step(blocks, carry):
            chains = [(b, h) for b in range(len(blocks)) for h in range(2)]
            rows = [r for _, r, _ in blocks]
            cut = lambda t, b: t[rows[b][0]:rows[b][1]]
            s0 = [pl.multiple_of(kj * tk, tk) for kj, _, _ in blocks]
            kb = [k_ref[pl.ds(s, tk), :] for s in s0]
            vb = [v_ref[pl.ds(s, tk), :] for s in s0]
            visible = [causal if masked else None for _, _, masked in blocks]
            z = {c: _dot_nt(cut(q_heads[c[1]], c[0]), kb[c[0]]) for c in chains}
            da = {c: _dot_nt(cut(do_heads[c[1]], c[0]), vb[c[0]]) for c in chains}
            run = [carry[0], carry[3]]
            log_own, beta, later, base = {}, {}, {}, {}
            for c in chains:
                b, h = c
                lk = _softplus_parts(z[c])[0]
                if visible[b] is not None:
                    lk = jnp.where(visible[b], lk, 0.0)
                log_own[c] = z[c] + lk
                beta[c] = jnp.exp(log_own[c]).astype(BF16)
                later[c] = _dot(lk.astype(BF16), later_mat)
                run[h] = _add_rows(run[h], rows[b], jnp.sum(lk, axis=-1, keepdims=True))
                base[c] = cut(totals[h] - run[h], b)
            grun = [carry[1], carry[4]]
            a_bf, g_bf, gbefore, grun_at = {}, {}, {}, {}
            for c in chains:
                b, h = c
                a = jnp.exp(log_own[c] + later[c] + base[c])
                if visible[b] is not None:
                    a = jnp.where(visible[b], a, 0.0)
                a_bf[c] = a.astype(BF16)
                g = da[c] * a
                g_bf[c] = g.astype(BF16)
                gbefore[c] = _dot(g_bf[c], before_mat)
                grun_at[c] = cut(grun[h], b)
                grun[h] = _add_rows(grun[h], rows[b], jnp.sum(g, axis=-1, keepdims=True))
            dq = [carry[2], carry[5]]
            dk_blk = [jnp.zeros((tk, 128), F32) for _ in blocks]
            dv_blk = [jnp.zeros((tk, 128), F32) for _ in blocks]
            for c in chains:
                b, h = c
                g = g_bf[c].astype(F32)
                dz = g - beta[c].astype(F32) * (g + gbefore[c] + grun_at[c])
                if visible[b] is not None:
                    dz = jnp.where(visible[b], dz, 0.0)
                dzb = dz.astype(BF16)
                dq[h] = _add_rows(dq[h], rows[b], _dot(dzb, kb[b]))
                dk_blk[b] = dk_blk[b] + _dot_tn(dzb, cut(q_heads[h], b))
                dv_blk[b] = dv_blk[b] + _dot_tn(a_bf[c], cut(do_heads[h], b))
            for b in range(len(blocks)):
                dk_acc[pl.ds(s0[b], tk), :] += dk_blk[b]
                dv_acc[pl.ds(s0[b], tk), :] += dv_blk[b]
            return run[0], grun[0], dq[0], run[1], grun[1], dq[1]

        pairs = jnp.clip(jnp.max(steps_ref[0:4, :]).astype(jnp.int32), 0, qi)
        left = (qi - pairs) * kpq
        singles = jnp.clip(jnp.max(steps_ref[4:8, :]).astype(jnp.int32), 0, left)
        zero1, zero128 = jnp.zeros((tq, 1), F32), jnp.zeros((tq, 128), F32)
        carry = lax.fori_loop(left - singles, left, lambda kj, c: step([(kj, (0, tk), False)], c),
                              (zero1, zero1, zero128, zero1, zero1, zero128))
        carry = lax.fori_loop(qi - pairs, qi,
                              lambda n, c: step([(n * kpq, (0, tq), False), (n * kpq + 1, (0, tq), False)], c), carry)
        carry = step([(qi * kpq, (0, tk), True), (qi * kpq, (tk, tq), False), (qi * kpq + 1, (tk, tq), True)], carry)
        dq_ref[...] = jnp.where(first, carry[2], carry[5]).astype(BF16)

        @pl.when(qi == nq - 1)
        def _():
            dk_ref[...] = dk_acc[...].astype(BF16)
            dv_ref[...] = dv_acc[...].astype(BF16)

    blk = pl.BlockSpec((tq, 128), lambda hp, qi: (qi, hp))
    seq = pl.BlockSpec((S, 128), lambda hp, qi: (0, hp))
    return _call_with_gather(
        body, name="attn_bwd", grid=(D_MODEL // 128, nq),
        in_specs=[blk, seq, seq, blk, pl.BlockSpec((None, None, 8, 128), lambda hp, qi: (hp, qi, 0, 0)),
                  blk, blk, blk],
        out_specs=[blk, seq, seq, blk],
        out_shape=[jax.ShapeDtypeStruct((S, D_MODEL), BF16)] * 4,
        scratch_shapes=[pltpu.VMEM((S, 128), F32), pltpu.VMEM((S, 128), F32)],
        args=(q, k, v, ltot, steps, dgated, o, zgate), reduce=reduce, vmem_mib=56)


def _rms_bwd(xv, dh_gain_sum):
    r = lax.rsqrt(jnp.mean(xv * xv, axis=-1, keepdims=True) + EPS)
    xhat = xv * r
    dx = r * (dh_gain_sum - xhat * jnp.mean(dh_gain_sum * xhat, axis=-1, keepdims=True))
    return dx, xhat


def _b_in_bwd(dq, dk, dv, dz, q_raw, k_raw, x, dx_mid, q_gain_t, k_gain_t, b_gain, kv_gain, w_in, w_kv):
    S = x.shape[0]
    tm = ROW_TILE

    def body(dq_ref, dk_ref, dv_ref, dz_ref, qr_ref, kr_ref, x_ref, dxm_ref, qg_ref, kg_ref, bg_ref, kvg_ref,
             win_ref, wkv_ref, dqz_ref, dkv_ref, dx_ref, small_ref):
        @pl.when(pl.program_id(0) == 0)
        def _():
            small_ref[...] = jnp.zeros_like(small_ref)

        bd = _head_mean_matrix()

        def head_norm_bwd(dy_ref, raw_ref, gain, scale):
            raw = raw_ref[...].astype(F32)
            rr = lax.rsqrt(_head_mean(raw * raw, bd) + EPS)
            xhat = raw * rr
            dy = dy_ref[...].astype(F32) * scale
            gdy = dy * gain
            draw = rr * (gdy - xhat * _head_mean(gdy * xhat, bd))
            return draw.astype(BF16), jnp.sum(dy * xhat, axis=0, keepdims=True)

        dqr, dqg = head_norm_bwd(dq_ref, qr_ref, qg_ref[...], SB_SCALE)
        dkr, dkg = head_norm_bwd(dk_ref, kr_ref, kg_ref[...], 1.0)
        dqz_ref[:, :D_MODEL] = dqr
        dqz_ref[:, D_MODEL:] = dz_ref[...]
        dkv_ref[:, :D_MODEL] = dkr
        dkv_ref[:, D_MODEL:] = dv_ref[...]
        dhb = jnp.zeros((tm, D_MODEL), F32)
        dhkv = jnp.zeros((tm, D_MODEL), F32)
        for sh in range(N_CHIPS):
            cols = slice(sh * 512, (sh + 1) * 512)
            dhb = dhb + _dot_nt(dqz_ref[:, cols], win_ref[sh])
            dhkv = dhkv + _dot_nt(dkv_ref[:, cols], wkv_ref[sh])
        dx, xhat = _rms_bwd(x_ref[...], dhb * bg_ref[...] + dhkv * kvg_ref[...])
        dx_ref[...] = dxm_ref[...] + dx
        small_ref[0:1, :] += dqg
        small_ref[1:2, :] += dkg
        small_ref[2:3, :] += jnp.sum(dhb * xhat, axis=0, keepdims=True)
        small_ref[3:4, :] += jnp.sum(dhkv * xhat, axis=0, keepdims=True)

    row = pl.BlockSpec((tm, D_MODEL), lambda i: (i, 0))
    wide = pl.BlockSpec((tm, 2 * D_MODEL), lambda i: (i, 0))
    vec = pl.BlockSpec((1, D_MODEL), lambda i: (0, 0))
    wsp = pl.BlockSpec((N_CHIPS, D_MODEL, 512), lambda i: (0, 0, 0))
    return pl.pallas_call(
        body, name="b_in_bwd", grid=(S // tm,),
        in_specs=[row] * 8 + [vec] * 4 + [wsp, wsp],
        out_specs=[wide, wide, row, pl.BlockSpec((8, D_MODEL), lambda i: (0, 0))],
        out_shape=[jax.ShapeDtypeStruct((S, 2 * D_MODEL), BF16), jax.ShapeDtypeStruct((S, 2 * D_MODEL), BF16),
                   jax.ShapeDtypeStruct((S, D_MODEL), F32), jax.ShapeDtypeStruct((8, D_MODEL), F32)],
        compiler_params=_params(("arbitrary",), 56),
    )(dq, dk, dv, dz, q_raw, k_raw, x, dx_mid, q_gain_t, k_gain_t, b_gain, kv_gain, w_in, w_kv)


def _a_mix_bwd(dgated, uz, pooled, wg, scale, w_in, x, dx_mid, gain, reduce=None):
    S = x.shape[0]
    tm = ROW_TILE
    n = S // tm

    def body(dg_ref, z_ref, p_ref, wg_ref, sc_ref, win_ref, x_ref, dxm_ref, gn_ref,
             duz_ref, dmr_ref, dx_ref, small_ref, halo_hi, halo_lo):
        i = pl.program_id(0)

        @pl.when(i == 0)
        def _():
            small_ref[...] = jnp.zeros_like(small_ref)
            halo_hi[...] = jnp.zeros_like(halo_hi)
            halo_lo[...] = jnp.zeros_like(halo_lo)

        first_row = (n - 1 - i) * tm
        row = lax.broadcasted_iota(jnp.int32, (tm, tm), 0)
        col = lax.broadcasted_iota(jnp.int32, (tm, tm), 1)
        d = col - row
        for g, w in enumerate(POOL_WINDOWS):
            cols = slice(g * GROUP_DIM, (g + 1) * GROUP_DIM)
            wgg = _group_weight(wg_ref, g)
            sc = sc_ref[:, cols]
            mraw = _dot(p_ref[:, cols], wgg)
            z = z_ref[:, cols]
            sg = _sigmoid(z)
            dga = dg_ref[:, cols].astype(F32)
            dm = dga * (z * sg)
            duz_ref[:, D_MODEL + g * GROUP_DIM:D_MODEL + (g + 1) * GROUP_DIM] = (
                dga * (mraw * sc) * _dsilu(z, sg)).astype(BF16)
            small_ref[0:1, cols] += jnp.sum(dm * mraw, axis=0, keepdims=True)
            dmr = (dm * sc).astype(BF16)
            dmr_ref[:, cols] = dmr
            dp = _dot_nt(dmr, wgg)
            hi, lo = _hilo(dp * _inv_count(first_row, tm, w))
            t_main = _mask_bf16((d >= 0) & (d < w))
            t_halo = _mask_bf16(d + tm < w)
            du = (_dot(t_main, hi) + _dot(t_main, lo) + _dot(t_halo, halo_hi[:, cols]) + _dot(t_halo, halo_lo[:, cols])
                  - dp)
            halo_hi[:, cols] = hi
            halo_lo[:, cols] = lo
            duz_ref[:, cols] = du.astype(BF16)
        dh = jnp.zeros((tm, D_MODEL), F32)
        for sh in range(N_CHIPS):
            dh = dh + _dot_nt(duz_ref[:, sh * 512:(sh + 1) * 512], win_ref[sh])
        dx, xhat = _rms_bwd(x_ref[...], dh * gn_ref[...])
        dx_ref[...] = dxm_ref[...] + dx
        small_ref[1:2, :] += jnp.sum(dh * xhat, axis=0, keepdims=True)

    rev = lambda i: (n - 1 - i, 0)
    row = pl.BlockSpec((tm, D_MODEL), rev)
    vec = pl.BlockSpec((1, D_MODEL), lambda i: (0, 0))
    return _call_with_gather(
        body, name="a_mix_bwd", grid=(n,),
        in_specs=[row,
                  pl.BlockSpec((tm, D_MODEL), lambda i: (n - 1 - i, 1)),
                  row,
                  pl.BlockSpec((N_CHIPS, N_GROUPS, 64, GROUP_DIM), lambda i: (0, 0, 0, 0)),
                  vec,
                  pl.BlockSpec((N_CHIPS, D_MODEL, 512), lambda i: (0, 0, 0)),
                  row, row, vec],
        out_specs=[pl.BlockSpec((tm, 2 * D_MODEL), rev), row, row,
                   pl.BlockSpec((8, D_MODEL), lambda i: (0, 0))],
        out_shape=[jax.ShapeDtypeStruct((S, 2 * D_MODEL), BF16), jax.ShapeDtypeStruct((S, D_MODEL), BF16),
                   jax.ShapeDtypeStruct((S, D_MODEL), F32), jax.ShapeDtypeStruct((8, D_MODEL), F32)],
        scratch_shapes=[pltpu.VMEM((tm, D_MODEL), BF16), pltpu.VMEM((tm, D_MODEL), BF16)],
        args=(dgated, uz, pooled, wg, scale, w_in, x, dx_mid, gain), reduce=reduce, vmem_mib=56)


def _wgrad(name, a, dy, n_shards, a_spec=None, k_dim=None):
    S, n_cols = dy.shape
    ts = WGRAD_SEQ_TILE
    k_dim = a.shape[-1] if k_dim is None else k_dim
    wn = n_cols // n_shards
    tk = min(k_dim, WGRAD_ACC_BYTES // (4 * n_cols))
    nst = S // ts

    def body(a_ref, dy_ref, out_ref, acc):
        st = pl.program_id(1)

        @pl.when(st == 0)
        def _():
            acc[...] = jnp.zeros_like(acc)

        acc[...] += _dot_tn(a_ref[...].astype(BF16), dy_ref[...].astype(BF16))

        @pl.when(st == nst - 1)
        def _():
            for sh in range(n_shards):
                out_ref[sh] = acc[:, sh * wn:(sh + 1) * wn]

    if a_spec is None:
        a_spec = pl.BlockSpec((ts, tk), lambda kt, st: (st, kt))
    return pl.pallas_call(
        body, name=name, grid=(k_dim // tk, nst),
        in_specs=[a_spec, pl.BlockSpec((ts, n_cols), lambda kt, st: (st, 0))],
        out_specs=pl.BlockSpec((n_shards, tk, wn), lambda kt, st: (0, kt, 0)),
        out_shape=jax.ShapeDtypeStruct((n_shards, k_dim, wn), F32),
        scratch_shapes=[pltpu.VMEM((tk, n_cols), F32)],
        compiler_params=_params(("parallel", "arbitrary")),
    )(a, dy)


def _wgrad_ple(name, p, layer, de):
    ts = WGRAD_SEQ_TILE
    spec = pl.BlockSpec((None, None, ts, PLE_DIM), lambda kt, st: (layer, 0, st, 0))
    return _wgrad(name, p, de, N_CHIPS, a_spec=spec, k_dim=PLE_DIM)


def _wgrad_group(pooled, dmr):
    S = pooled.shape[0]
    ts = WGRAD_SEQ_TILE
    nst = S // ts

    def body(p_ref, d_ref, out_ref, acc):
        st = pl.program_id(1)

        @pl.when(st == 0)
        def _():
            acc[...] = jnp.zeros_like(acc)

        acc[...] += _dot_tn(p_ref[...], d_ref[...])

        @pl.when(st == nst - 1)
        def _():
            for sh in range(N_CHIPS):
                out_ref[sh] = acc[sh * 64:(sh + 1) * 64, :]

    blk = pl.BlockSpec((ts, GROUP_DIM), lambda g, st: (st, g))
    return pl.pallas_call(
        body, name="wgrad_group", grid=(N_GROUPS, nst),
        in_specs=[blk, blk],
        out_specs=pl.BlockSpec((N_CHIPS, None, 64, GROUP_DIM), lambda g, st: (0, g, 0, 0)),
        out_shape=jax.ShapeDtypeStruct((N_CHIPS, N_GROUPS, 64, GROUP_DIM), F32),
        scratch_shapes=[pltpu.VMEM((GROUP_DIM, GROUP_DIM), F32)],
        compiler_params=_params(("parallel", "arbitrary")),
    )(pooled, dmr)


GATHER_AT = {
    "a_in": ("a_w_group", "a_w_out", "ple_w0", "ple_gate_w0"),
    "a_mix": ("w_kv",),
    "a_out_ple": ("b_w_in",),
    "attn_fwd": ("b_w_out", "ple_w1", "ple_gate_w1"),
}


REDUCE_AT = {
    "attn_bwd": ("b_w_out", "ple_w1", "ple_gate_w1"),
    "a_mix_bwd": ("a_w_out", "ple_w0", "ple_gate_w0"),
}


def _local_step(x, p, target, w, local=None, state=None):
    w = dict(w)

    def run(fn, host, n_out, *args, **kwargs):
        names = GATHER_AT[host] if local is not None else ()
        res = fn(*args, gather=[local[n] for n in names], **kwargs)
        w.update(zip(names, res[n_out:]))
        return res[:n_out]

    k_gain_t = jnp.tile(w["k_norm"].reshape(1, HEAD_DIM), (1, N_HEADS))
    q_gain_t = jnp.tile(w["b_q_norm"].reshape(1, HEAD_DIM), (1, N_HEADS))

    uz, h_a = run(_a_in, "a_in", 2, x, w["a_norm"], w["a_w_in"])
    wg4 = w["a_w_group"].reshape(N_CHIPS, N_GROUPS, 64, GROUP_DIM)
    wa_out = w["a_w_out"].reshape(D_MODEL, D_MODEL)
    gated_a, pooled = run(_a_mix, "a_mix", 2, uz, wg4, w["a_scale"])
    x1, x2, e_a, gate_a = run(_out_ple, "a_out_ple", 4, "a_out_ple", gated_a, x, wa_out, p, 0,
                              w["ple_w0"], w["ple_gate_w0"])
    h_kv, h_b, k_raw, q_raw, k, q, v, z_b = _b_in(
        x2, w["kv_norm"], w["b_norm"], k_gain_t, q_gain_t, w["w_kv"], w["b_w_in"])
    o, gated_b, ltot, att_steps = run(_attn_fwd, "attn_fwd", 4, q, k, v, z_b)
    wb_out = w["b_w_out"].reshape(D_MODEL, D_MODEL)
    x3, dx4, e_b, gate_b, loss_blk = _out_ple("b_out_ple", gated_b, x2, wb_out, p, 1, w["ple_w1"], w["ple_gate_w1"],
                                              target=target)

    grads, updates = {}, {}

    def hosted(fn, host, n_out, *args):
        if state is None:
            return fn(*args)
        names = REDUCE_AT[host]
        seeds = [updates.get(n[:-1] + "1") if n.startswith("ple") and n.endswith("0") else None for n in names]
        res = fn(*args, reduce=([grads.pop(n) for n in names], *[[t[n] for n in names] for t in state[:3]],
                                [state[3][n] for n in names], seeds))
        for i, n in enumerate(names):
            updates[n] = tuple(group[i] for group in res[n_out:])
        return res[:n_out]

    de_b, dgp_b, dx3, dgated_b = _ple_out_bwd("b_ple_out_bwd", dx4, e_b, gate_b, w["ple_gate_w1"], wb_out)
    grads["b_w_out"] = _wgrad("wgrad_b_out", gated_b, dx3, 1).reshape(N_CHIPS, 256, D_MODEL)
    grads["ple_w1"] = _wgrad_ple("wgrad_ple1", p, 1, de_b)
    grads["ple_gate_w1"] = _wgrad("wgrad_gate1", x3, dgp_b, 1).reshape(N_CHIPS, 256, D_MODEL)
    dq, dk, dv, dz_b = hosted(_attn_bwd, "attn_bwd", 4, q, k, v, ltot, att_steps, dgated_b, o, z_b)
    dqz, dkv, dx2, small_b = _b_in_bwd(dq, dk, dv, dz_b, q_raw, k_raw, x2, dx3, q_gain_t, k_gain_t,
                                       w["b_norm"], w["kv_norm"], w["b_w_in"], w["w_kv"])
    grads["w_kv"] = _wgrad("wgrad_kv", h_kv, dkv, N_CHIPS)
    grads["b_w_in"] = _wgrad("wgrad_b_in", h_b, dqz, N_CHIPS)
    de_a, dgp_a, dx1, dgated_a = _ple_out_bwd("a_ple_out_bwd", dx2, e_a, gate_a, w["ple_gate_w0"], wa_out)
    grads["a_w_out"] = _wgrad("wgrad_a_out", gated_a, dx1, 1).reshape(N_CHIPS, 256, D_MODEL)
    grads["ple_w0"] = _wgrad_ple("wgrad_ple0", p, 0, de_a)
    grads["ple_gate_w0"] = _wgrad("wgrad_gate0", x1, dgp_a, 1).reshape(N_CHIPS, 256, D_MODEL)
    duz, dmr, grad_x, small_a = hosted(_a_mix_bwd, "a_mix_bwd", 4, dgated_a, uz, pooled, wg4, w["a_scale"],
                                       w["a_w_in"], x, dx1, w["a_norm"])
    grads["a_w_in"] = _wgrad("wgrad_a_in", h_a, duz, N_CHIPS)
    grads["a_w_group"] = _wgrad_group(pooled, dmr).reshape(N_CHIPS, N_GROUPS * 64, GROUP_DIM)

    fold = lambda row: jnp.pad(row.reshape(N_HEADS, HEAD_DIM).sum(axis=0), (0, D_MODEL - HEAD_DIM))
    small = jnp.stack([small_a[1], small_a[0], small_b[3], small_b[2], fold(small_b[1]), fold(small_b[0]),
                       jnp.pad(loss_blk[0], (0, D_MODEL - loss_blk.shape[1])), jnp.zeros((D_MODEL,), F32)])
    return grad_x, grads, updates, small


def _mesh_place():
    x, y, c = lax.axis_index("x"), lax.axis_index("y"), lax.axis_index("c")
    other_chips = [(1 - x, y), (x, 1 - y), (1 - x, 1 - y)]
    return x, y, c, other_chips


def _gather_sems(n):
    return [pltpu.SemaphoreType.DMA((3 * n,)), pltpu.SemaphoreType.DMA((3 * n,)),
            pltpu.SemaphoreType.DMA((3 * n,)), pltpu.SemaphoreType.DMA((3 * n,)), pltpu.SemaphoreType.DMA((n,))]


def _gather_copies(srcs, outs, sems):
    send_far, recv_far, send_sib, recv_sib, local_sem = sems
    n = len(srcs)
    x, y, c, chips = _mesh_place()
    me = 2 * x + y
    sibling = (x, y, 1 - c)

    def half(k, which):
        rows = srcs[k].shape[0] // 2
        return pl.ds(pl.multiple_of(which * rows, 16), rows)

    local = [pltpu.make_async_copy(srcs[k], outs[k].at[me], local_sem.at[k]) for k in range(n)]
    far = [pltpu.make_async_remote_copy(
        src_ref=srcs[k].at[half(k, c)], dst_ref=outs[k].at[me, half(k, c)],
        send_sem=send_far.at[j * n + k], recv_sem=recv_far.at[j * n + k], device_id=(px, py, c), device_id_type=MESH)
        for j, (px, py) in enumerate(chips) for k in range(n)]

    def landed(j, k, which, from_far):
        px, py = chips[j]
        piece = outs[k].at[2 * px + py, half(k, which)]
        send, recv = (send_far, recv_far) if from_far else (send_sib, recv_sib)
        return pltpu.make_async_remote_copy(src_ref=piece, dst_ref=piece, send_sem=send.at[j * n + k],
                                            recv_sem=recv.at[j * n + k], device_id=sibling, device_id_type=MESH)

    return local, far, landed, c


def _gather_start(srcs, outs, sems):
    local, far, _, _ = _gather_copies(srcs, outs, sems)
    for cp in local + far:
        cp.start()


def _gather_pass_on(srcs, outs, sems):
    _, _, landed, c = _gather_copies(srcs, outs, sems)
    for j in range(3):
        for k in range(len(srcs)):
            landed(j, k, c, True).wait_recv()
            landed(j, k, c, False).start()


def _gather_finish(srcs, outs, sems):
    local, far, landed, c = _gather_copies(srcs, outs, sems)
    pairs = [(j, k) for j in range(3) for k in range(len(srcs))]
    for j, k in pairs:
        landed(j, k, 1 - c, False).wait_recv()
    for cp in far + [landed(j, k, c, False) for j, k in pairs]:
        cp.wait_send()
    for cp in local:
        cp.wait()


def _call_with_gather(body, *, name, grid, in_specs, out_specs, out_shape, args, gather=(), reduce=None,
                      scratch_shapes=(), vmem_mib=48):
    n_in, n_out, n_scr, n_g = len(args), len(out_shape), len(scratch_shapes), len(gather)
    n_r = len(reduce[0]) if reduce else 0
    pieces = _reduce_pieces(reduce[0], reduce[4]) if reduce else []
    reduce_args = [a for group in reduce[:4] for a in group] if reduce else []
    seeds = reduce[5] if reduce else []
    seeded = [(k, a) for k, seed in enumerate(seeds) if seed is not None for a in range(4)]
    gather_sems = _gather_sems(n_g) if n_g else []
    n_steps = 1
    for g in grid:
        n_steps *= g

    def wrapped(*refs):
        refs = list(refs)
        take = lambda count: [refs.pop(0) for _ in range(count)]
        ins, g_in, r_in = take(n_in), take(n_g), take(4 * n_r)
        take(len(seeded))
        outs, g_out, r_out = take(n_out), take(n_g), take(4 * n_r)
        scratch, sems, r_scratch = take(n_scr), take(len(gather_sems)), refs
        step = 0
        for axis, g in enumerate(grid):
            step = step * g + pl.program_id(axis)
        if n_g:
            @pl.when(step == 0)
            def _():
                _gather_start(g_in, g_out, sems)

        if n_r:
            ticks, drain = _reduce_ticks(pieces, n_r, (*r_in, *r_out, *r_scratch))
            for t, tick in enumerate(ticks[:n_steps]):
                pl.when(step == t)(tick)

        body(*ins, *outs, *scratch)
        if n_r:
            for tick in ticks[n_steps:]:
                pl.when(step == n_steps - 1)(tick)
            pl.when(step == n_steps - 1)(drain)
        if n_g:
            @pl.when(step == max(n_steps - 2, 0))
            def _():
                _gather_pass_on(g_in, g_out, sems)

            @pl.when(step == n_steps - 1)
            def _():
                _gather_finish(g_in, g_out, sems)

    hbm = pl.BlockSpec(memory_space=pltpu.HBM)
    res = pl.pallas_call(
        wrapped, name=name, grid=grid,
        in_specs=list(in_specs) + [hbm] * (n_g + 4 * n_r + len(seeded)),
        out_specs=list(out_specs) + [hbm] * (n_g + 4 * n_r),
        out_shape=list(out_shape) + [jax.ShapeDtypeStruct((N_CHIPS,) + g.shape, BF16) for g in gather]
        + ([jax.ShapeDtypeStruct(w.shape, F32) for _ in range(4) for w in reduce[1]] if reduce else []),
        input_output_aliases={n_in + n_g + 4 * n_r + i: n_out + n_g + a * n_r + k for i, (k, a) in enumerate(seeded)},
        scratch_shapes=list(scratch_shapes) + gather_sems + (_reduce_scratch() if reduce else []),
        compiler_params=_params(("arbitrary",) * len(grid), vmem_mib),
    )(*args, *gather, *reduce_args, *[seeds[k][a] for k, a in seeded])
    if not reduce:
        return res
    plain = list(res[:n_out + n_g])
    return plain + [res[n_out + n_g + i * n_r:n_out + n_g + (i + 1) * n_r] for i in range(4)]


def _allgather_weights(shards, small, casts):
    n = len(shards)
    cast_out = [(k, r0, r1) for k, (_, ranges) in enumerate(casts) for r0, r1 in ranges]
    n_c, n_co = len(casts), len(cast_out)

    def body(*refs):
        ins, small_in, cast_in = refs[:n], refs[n], refs[n + 1:n + 1 + n_c]
        refs = refs[n + 1 + n_c:]
        outs, small_out, cast_dst = refs[:n], refs[n], refs[n + 1:n + 1 + n_co]
        refs = refs[n + 1 + n_co:]
        cast, cast_buf = refs[:n], refs[n:n + n_co]
        send_far, recv_far, send_sib, recv_sib, send_small, recv_small, local_sem, cast_sem = refs[n + n_co:]
        x, y, c, chips = _mesh_place()
        me = 2 * x + y
        sibling = (x, y, 1 - c)

        def half(k, which):
            rows = ins[k].shape[0] // 2
            return pl.ds(pl.multiple_of(which * rows, 16), rows)

        local = []
        for k in range(n):
            cast[k][...] = ins[k][...].astype(BF16)
            local.append(pltpu.make_async_copy(cast[k], outs[k].at[me], local_sem.at[k]))
            local[-1].start()
        local.append(pltpu.make_async_copy(small_in, small_out.at[me], local_sem.at[n]))
        local[-1].start()

        sends = []
        for j, (px, py) in enumerate(chips):
            for k in range(n):
                cp = pltpu.make_async_remote_copy(
                    src_ref=cast[k].at[half(k, c)], dst_ref=outs[k].at[me, half(k, c)],
                    send_sem=send_far.at[j * n + k], recv_sem=recv_far.at[j * n + k],
                    device_id=(px, py, c), device_id_type=MESH)
                cp.start()
                sends.append(cp)
            cp = pltpu.make_async_remote_copy(
                src_ref=small_in, dst_ref=small_out.at[me], send_sem=send_small.at[j], recv_sem=recv_small.at[j],
                device_id=(px, py, c), device_id_type=MESH)
            cp.start()
            sends.append(cp)

        for i, (k, r0, r1) in enumerate(cast_out):
            cast_buf[i][...] = cast_in[k][r0:r1, :].astype(BF16)
            local.append(pltpu.make_async_copy(cast_buf[i], cast_dst[i], cast_sem.at[i]))
            local[-1].start()

        def landed(j, k, which, sems_s, sems_r, device):
            px, py = chips[j]
            piece = outs[k].at[2 * px + py, half(k, which)]
            return pltpu.make_async_remote_copy(
                src_ref=piece, dst_ref=piece, send_sem=sems_s.at[j * n + k], recv_sem=sems_r.at[j * n + k],
                device_id=device, device_id_type=MESH)

        for j in range(len(chips)):
            for k in range(n):
                landed(j, k, c, send_far, recv_far, sibling).wait_recv()
                cp = landed(j, k, c, send_sib, recv_sib, sibling)
                cp.start()
                sends.append(cp)
        for j, (px, py) in enumerate(chips):
            for k in range(n):
                landed(j, k, 1 - c, send_sib, recv_sib, sibling).wait_recv()
            pltpu.make_async_remote_copy(
                src_ref=small_in, dst_ref=small_out.at[2 * px + py], send_sem=send_small.at[j],
                recv_sem=recv_small.at[j], device_id=(px, py, c), device_id_type=MESH).wait_recv()
        for cp in sends:
            cp.wait_send()
        for cp in local:
            cp.wait()

    vmem = pl.BlockSpec(memory_space=pltpu.VMEM)
    hbm = pl.BlockSpec(memory_space=pltpu.HBM)
    cast_shapes = [(r1 - r0, casts[k][0].shape[1]) for k, r0, r1 in cast_out]
    res = pl.pallas_call(
        body, name="allgather_weights",
        in_specs=[vmem] * (n + 1 + n_c), out_specs=[hbm] * (n + 1 + n_co),
        out_shape=[jax.ShapeDtypeStruct((N_CHIPS,) + s.shape, BF16) for s in shards]
        + [jax.ShapeDtypeStruct((N_CHIPS,) + small.shape, F32)]
        + [jax.ShapeDtypeStruct(s, BF16) for s in cast_shapes],
        scratch_shapes=[pltpu.VMEM(s.shape, BF16) for s in shards] + [pltpu.VMEM(s, BF16) for s in cast_shapes]
        + [pltpu.SemaphoreType.DMA((3 * n,)), pltpu.SemaphoreType.DMA((3 * n,)),
           pltpu.SemaphoreType.DMA((3 * n,)), pltpu.SemaphoreType.DMA((3 * n,)),
           pltpu.SemaphoreType.DMA((3,)), pltpu.SemaphoreType.DMA((3,)),
           pltpu.SemaphoreType.DMA((n + 1,)), pltpu.SemaphoreType.DMA((n_co,))],
        compiler_params=_params(None, 40),
    )(*shards, small, *[a for a, _ in casts])
    return res[:n], res[n], res[n + 1:]


def _adamw(w, g, m, v):
    m = ADAM_B1 * m + (1.0 - ADAM_B1) * g
    v = ADAM_B2 * v + (1.0 - ADAM_B2) * (g * g)
    m_hat = m / (1.0 - ADAM_B1 ** ADAM_STEP)
    v_hat = v / (1.0 - ADAM_B2 ** ADAM_STEP)
    delta = -ADAM_LR * (m_hat / (jnp.sqrt(v_hat) + ADAM_EPS) + ADAM_WD * w)
    return delta, m, v


RS_PIECE_ROWS = 128
RS_PIECE_COLS = 512


def _reduce_adam_all(grads, ws, ms, vs, small, bases=None, seeds=None):
    n_w = len(grads)
    pieces = _reduce_pieces(grads, bases)
    n_small = len(_small_sum_scratch(small.shape))
    seeds = seeds or [None] * n_w
    seeded = [(k, a) for k, seed in enumerate(seeds) if seed is not None for a in range(4)]
    n_in = 4 * n_w + 1

    def body(*refs):
        refs = list(refs)
        del refs[n_in:n_in + len(seeded)]
        small_in = refs.pop(4 * n_w)
        small_out = refs.pop(8 * n_w)
        small_scratch = [refs.pop() for _ in range(n_small)][::-1]
        sends = _small_sum_start(small_in, *small_scratch)
        ticks, drain = _reduce_ticks(pieces, n_w, refs)
        for tick in ticks:
            tick()
        drain()
        _small_sum_finish(sends, small_scratch[0], small_out)

    hbm = pl.BlockSpec(memory_space=pltpu.HBM)
    vmem = pl.BlockSpec(memory_space=pltpu.VMEM)
    outs = pl.pallas_call(
        body, name="reduce_adam_all",
        in_specs=[hbm] * (4 * n_w) + [vmem] + [hbm] * len(seeded), out_specs=[hbm] * (4 * n_w) + [vmem],
        out_shape=[jax.ShapeDtypeStruct(w.shape, F32) for _ in range(4) for w in ws]
        + [jax.ShapeDtypeStruct(small.shape, F32)],
        input_output_aliases={n_in + i: a * n_w + k for i, (k, a) in enumerate(seeded)},
        scratch_shapes=_reduce_scratch() + _small_sum_scratch(small.shape),
        compiler_params=_params(None, 48),
    )(*grads, *ws, *ms, *vs, small, *[seeds[k][a] for k, a in seeded])
    return [outs[i * n_w:(i + 1) * n_w] for i in range(4)], outs[4 * n_w]


def _reduce_pieces(grads, bases=None):
    pieces = []
    for k, g in enumerate(grads):
        hr, cols = g.shape[1] // 2, g.shape[2]
        pr, pc = min(hr, RS_PIECE_ROWS), min(cols, RS_PIECE_COLS)
        base = bases[k] if bases else 0
        pieces += [(k, ro, hr, co, pr, pc, base) for ro in range(0, hr, pr) for co in range(0, cols, pc)]
    return pieces


def _reduce_scratch():
    P, C = RS_PIECE_ROWS, RS_PIECE_COLS
    return [
        pltpu.VMEM((3, N_CHIPS, P, C), F32), pltpu.VMEM((3, N_CHIPS, P, C), F32),
        pltpu.VMEM((2, N_CHIPS, P, C), BF16), pltpu.VMEM((2, N_CHIPS, P, C), BF16),
        pltpu.VMEM((2, N_CHIPS, P, C), F32),
        pltpu.VMEM((2, 3, P, C), BF16), pltpu.VMEM((2, 3, P, C), BF16),
        pltpu.VMEM((2, 2, P, C), F32),
        pltpu.VMEM((2, 3, 2, P, C), F32), pltpu.VMEM((2, 4, 2, P, C), F32),
        pltpu.SemaphoreType.DMA((3, 2)), pltpu.SemaphoreType.DMA((2, 3, 2)),
        pltpu.SemaphoreType.DMA((2,)), pltpu.SemaphoreType.DMA((2,)),
        pltpu.SemaphoreType.DMA((2, 3)), pltpu.SemaphoreType.DMA((2, 3)),
        pltpu.SemaphoreType.DMA((2,)), pltpu.SemaphoreType.DMA((2,)),
        pltpu.SemaphoreType.DMA((2, 4, 2))]


def _reduce_ticks(pieces, n_w, refs):
    n = len(pieces)

    def build(*refs):
        g_in, w_in, m_in, v_in = (refs[i * n_w:(i + 1) * n_w] for i in range(4))
        g_out, d_out, m_out, v_out = (refs[(4 + i) * n_w:(5 + i) * n_w] for i in range(4))
        (gm, go, sb1, rb1, part, sb2, rb2, fin, wmv, outs,
         ld_sem, wmv_sem, s1_send, s1_recv, s2_send, s2_recv, s3_send, s3_recv, out_sem) = refs[8 * n_w:]
        x, y, c, chips = _mesh_place()
        me = 2 * x + y
        sibling = (x, y, 1 - c)

        def at_hbm(i, which, in_shard):
            _, ro, hr, co, pr, pc, base = pieces[i]
            half = c if which == 0 else 1 - c
            return pl.ds(pl.multiple_of((base if in_shard else 0) + half * hr + ro, 64), pr), pl.ds(co, pc)

        def win(i):
            return pl.ds(0, pieces[i][4]), pl.ds(0, pieces[i][5])

        every = slice(None)

        def loads(i):
            k, s = pieces[i][0], i % 3
            return [pltpu.make_async_copy(g_in[k].at[(every,) + at_hbm(i, h, False)], buf.at[(s, every) + win(i)],
                                          ld_sem.at[s, h])
                    for h, buf in enumerate((gm, go))]

        def wmv_loads(i):
            k, s = pieces[i][0], i % 2
            return [pltpu.make_async_copy(src[k].at[at_hbm(i, h, True)], wmv.at[(s, a, h) + win(i)], wmv_sem.at[s, a, h])
                    for a, src in enumerate((w_in, m_in, v_in)) for h in range(2)]

        def stores(i):
            k, s = pieces[i][0], i % 2
            return [pltpu.make_async_copy(outs.at[(s, a, h) + win(i)], dst[k].at[at_hbm(i, h, True)], out_sem.at[s, a, h])
                    for a, dst in enumerate((g_out, d_out, m_out, v_out)) for h in range(2)]

        def swap1(i):
            s = i % 2
            return pltpu.make_async_remote_copy(
                src_ref=sb1.at[(s, every) + win(i)], dst_ref=rb1.at[(s, every) + win(i)],
                send_sem=s1_send.at[s], recv_sem=s1_recv.at[s], device_id=sibling, device_id_type=MESH)

        def far2(i, j):
            s = i % 2
            px, py = chips[j]
            return pltpu.make_async_remote_copy(
                src_ref=sb2.at[(s, j) + win(i)], dst_ref=rb2.at[(s, j) + win(i)],
                send_sem=s2_send.at[s, j], recv_sem=s2_recv.at[s, j], device_id=(px, py, c), device_id_type=MESH)

        def swap3(i):
            s = i % 2
            return pltpu.make_async_remote_copy(
                src_ref=fin.at[(s, 0) + win(i)], dst_ref=fin.at[(s, 1) + win(i)],
                send_sem=s3_send.at[s], recv_sem=s3_recv.at[s], device_id=sibling, device_id_type=MESH)

        def stage0(i):
            for cp in loads(i):
                cp.start()

        def stage1(i):
            s, s3 = i % 2, i % 3
            for cp in loads(i):
                cp.wait()
            sb1[(s, every) + win(i)] = go[(s3, every) + win(i)].astype(BF16)
            swap1(i).start()

        def stage2(i):
            s, s3 = i % 2, i % 3
            swap1(i).wait()
            part[(s, every) + win(i)] = gm[(s3, every) + win(i)] + rb1[(s, every) + win(i)].astype(F32)
            for j, (px, py) in enumerate(chips):
                sb2[(s, j) + win(i)] = part[(s, 2 * px + py) + win(i)].astype(BF16)
                far2(i, j).start()

        def stage3(i):
            s = i % 2
            total = part[(s, me) + win(i)]
            for j in range(3):
                far2(i, j).wait()
                total = total + rb2[(s, j) + win(i)].astype(F32)
            fin[(s, 0) + win(i)] = total
            swap3(i).start()
            for cp in wmv_loads(i):
                cp.start()

        def stage4(i):
            s = i % 2
            if i >= 2:
                for cp in stores(i - 2):
                    cp.wait()
            swap3(i).wait()
            for cp in wmv_loads(i):
                cp.wait()
            both = (every,) + win(i)
            g = fin[(s,) + both]
            delta, m_new, v_new = _adamw(wmv[(s, 0) + both], g, wmv[(s, 1) + both], wmv[(s, 2) + both])
            outs[(s, 0) + both] = g
            outs[(s, 1) + both] = delta
            outs[(s, 2) + both] = m_new
            outs[(s, 3) + both] = v_new
            for cp in stores(i):
                cp.start()

        stages = (stage0, stage1, stage2, stage3, stage4)

        def tick(t):
            for age in reversed(range(len(stages))):
                if 0 <= t - age < n:
                    stages[age](t - age)

        def drain():
            for i in range(max(0, n - 2), n):
                for cp in stores(i):
                    cp.wait()

        return [functools.partial(tick, t) for t in range(n + len(stages) - 1)], drain

    return build(*refs)


N_DEVICES = 8


def _small_sum_scratch(shape):
    return [pltpu.VMEM((N_DEVICES,) + shape, F32),
            pltpu.SemaphoreType.DMA((N_DEVICES - 1,)), pltpu.SemaphoreType.DMA((N_DEVICES - 1,))]


def _small_sum_start(part_ref, buf, send_sem, recv_sem):
    x, y, c, _ = _mesh_place()
    me = 4 * x + 2 * y + c
    buf[me] = part_ref[...]
    sends = []
    for k in range(1, N_DEVICES):
        peer = ((1 - x) if k & 4 else x, (1 - y) if k & 2 else y, (1 - c) if k & 1 else c)
        cp = pltpu.make_async_remote_copy(src_ref=part_ref, dst_ref=buf.at[me], send_sem=send_sem.at[k - 1],
                                          recv_sem=recv_sem.at[k - 1], device_id=peer, device_id_type=MESH)
        cp.start()
        sends.append(cp)
    return sends


def _small_sum_finish(sends, buf, out_ref):
    for cp in sends:
        cp.wait_recv()
    total = buf[0]
    for s in range(1, N_DEVICES):
        total = total + buf[s]
    out_ref[...] = total
    for cp in sends:
        cp.wait_send()


def _adam_small(w, g, m, v):
    def body(w_ref, g_ref, m_ref, v_ref, d_ref, mo_ref, vo_ref):
        delta, m_new, v_new = _adamw(w_ref[...], g_ref[...], m_ref[...], v_ref[...])
        d_ref[...] = delta
        mo_ref[...] = m_new
        vo_ref[...] = v_new

    vmem = pl.BlockSpec(memory_space=pltpu.VMEM)
    return pl.pallas_call(
        body, name="adam_small", in_specs=[vmem] * 4, out_specs=[vmem] * 3,
        out_shape=[jax.ShapeDtypeStruct(w.shape, F32)] * 3,
    )(w, g, m, v)


BIG = ("a_w_in", "a_w_group", "a_w_out", "w_kv", "b_w_in", "b_w_out", "ple_w", "ple_gate_w")
SMALL = ("a_norm", "a_scale", "kv_norm", "b_norm", "k_norm", "b_q_norm")
SMALL_SHARDED = ("a_norm", "a_scale")
WEIGHTS = ("a_norm", "a_w_in", "a_w_group", "a_scale", "a_w_out", "kv_norm", "w_kv", "k_norm", "b_norm", "b_w_in",
           "b_q_norm", "b_w_out", "ple_w", "ple_gate_w")


def _as_matrix(a):
    return a.reshape(-1, a.shape[-1])


def _pack_small(arrs):
    rows = [jnp.pad(a.reshape(-1), (0, D_MODEL - a.size)) for a in arrs]
    rows += [jnp.zeros((D_MODEL,), F32)] * (8 - len(rows))
    return jnp.stack(rows)


def kernel(x, p, a_norm, a_w_in, a_w_group, a_scale, a_w_out, kv_norm, w_kv, k_norm, b_norm, b_w_in, b_q_norm, b_w_out, ple_w, ple_gate_w, loss_target, m_a_norm, m_a_w_in, m_a_w_group, m_a_scale, m_a_w_out, m_kv_norm, m_w_kv, m_k_norm, m_b_norm, m_b_w_in, m_b_q_norm, m_b_w_out, m_ple_w, m_ple_gate_w, v_a_norm, v_a_w_in, v_a_w_group, v_a_scale, v_a_w_out, v_kv_norm, v_w_kv, v_k_norm, v_b_norm, v_b_w_in, v_b_q_norm, v_b_w_out, v_ple_w, v_ple_gate_w):
    wts = dict(a_norm=a_norm, a_w_in=a_w_in, a_w_group=a_w_group, a_scale=a_scale, a_w_out=a_w_out, kv_norm=kv_norm,
               w_kv=w_kv, k_norm=k_norm, b_norm=b_norm, b_w_in=b_w_in, b_q_norm=b_q_norm, b_w_out=b_w_out,
               ple_w=ple_w, ple_gate_w=ple_gate_w)
    mom = dict(a_norm=m_a_norm, a_w_in=m_a_w_in, a_w_group=m_a_w_group, a_scale=m_a_scale, a_w_out=m_a_w_out,
               kv_norm=m_kv_norm, w_kv=m_w_kv, k_norm=m_k_norm, b_norm=m_b_norm, b_w_in=m_b_w_in,
               b_q_norm=m_b_q_norm, b_w_out=m_b_w_out, ple_w=m_ple_w, ple_gate_w=m_ple_gate_w)
    var = dict(a_norm=v_a_norm, a_w_in=v_a_w_in, a_w_group=v_a_w_group, a_scale=v_a_scale, a_w_out=v_a_w_out,
               kv_norm=v_kv_norm, w_kv=v_w_kv, k_norm=v_k_norm, b_norm=v_b_norm, b_w_in=v_b_w_in,
               b_q_norm=v_b_q_norm, b_w_out=v_b_w_out, ple_w=v_ple_w, ple_gate_w=v_ple_gate_w)
    S = x.shape[1]
    chip = 2 * lax.axis_index("x") + lax.axis_index("y")

    sharded_small = jnp.concatenate([a_norm.reshape(1, 256), a_scale.reshape(1, 256), jnp.zeros((6, 256), F32)], axis=0)
    later = ("a_w_group", "a_w_out", "w_kv", "b_w_in", "b_w_out", "ple_w", "ple_gate_w")
    (a_w_in_full,), small_full, copies = _allgather_weights(
        [_as_matrix(a_w_in)], sharded_small,
        [(_as_matrix(wts[n]), [(0, 256), (256, 512)] if n.startswith("ple") else [(0, _as_matrix(wts[n]).shape[0])])
         for n in later])
    local = dict(zip(("a_w_group", "a_w_out", "w_kv", "b_w_in", "b_w_out", "ple_w0", "ple_w1", "ple_gate_w0",
                      "ple_gate_w1"), copies))
    full = dict(a_w_in=a_w_in_full,
                a_norm=small_full[:, 0, :].reshape(1, D_MODEL), a_scale=small_full[:, 1, :].reshape(1, D_MODEL),
                kv_norm=kv_norm.reshape(1, D_MODEL), b_norm=b_norm.reshape(1, D_MODEL), k_norm=k_norm, b_q_norm=b_q_norm)

    def shards(t):
        out = {}
        for n in BIG:
            for entry in ((n + "0", n + "1") if n.startswith("ple") else (n,)):
                out[entry] = _as_matrix(t[n])
        return out

    base = {n: 256 if n.startswith("ple") and n.endswith("1") else 0 for n in shards(wts)}
    state = (shards(wts), shards(mom), shards(var), base)
    grad_x, grads, updates, small_part = _local_step(x.reshape(S, D_MODEL), p, loss_target.reshape(S, D_MODEL),
                                                     full, local, state)

    names = sorted(grads)
    reduced, small_sum = _reduce_adam_all(
        [grads[n] for n in names], *[[t[n] for n in names] for t in state[:3]], small_part,
        bases=[base[n] for n in names], seeds=[updates.get(n[:-1] + "1") if n.startswith("ple") else None for n in names])
    for i, n in enumerate(names):
        updates[n] = tuple(group[i] for group in reduced)
    out_g, out_d, out_m, out_v = {}, {}, {}, {}
    for n in BIG:
        for i, out in enumerate((out_g, out_d, out_m, out_v)):
            out[n] = updates[n + "0" if n.startswith("ple") else n][i].reshape(wts[n].shape)

    loss = small_sum[len(SMALL), 0]
    small_rows = []
    for i, n in enumerate(SMALL):
        row = small_sum[i]
        if n in SMALL_SHARDED:
            row = lax.dynamic_slice(row, (chip * 256,), (256,))
        else:
            row = row[:wts[n].size]
        small_rows.append(row)
    g_small = _pack_small(small_rows)
    d_small, m_small, v_small = _adam_small(_pack_small([wts[n] for n in SMALL]), g_small,
                                            _pack_small([mom[n] for n in SMALL]), _pack_small([var[n] for n in SMALL]))
    for i, n in enumerate(SMALL):
        shape, size = wts[n].shape, wts[n].size
        out_g[n], out_d[n], out_m[n], out_v[n] = (t[i, :size].reshape(shape) for t in (g_small, d_small, m_small, v_small))

    return (loss, grad_x.reshape(1, S, D_MODEL), *[out_g[n] for n in WEIGHTS], *[out_d[n] for n in WEIGHTS],
            *[out_m[n] for n in WEIGHTS], *[out_v[n] for n in WEIGHTS])
```

```python
import functools

import jax
import jax.numpy as jnp
from jax import lax
from jax.experimental import pallas as pl
from jax.experimental.pallas import tpu as pltpu

F32 = jnp.float32
BF16 = jnp.bfloat16
MESH = pl.DeviceIdType.MESH

D_MODEL = 1024
N_HEADS = 16
HEAD_DIM = 64
PLE_DIM = 256
N_GROUPS = 4
GROUP_DIM = 256
POOL_WINDOWS = (2, 4, 8, 16)
N_CHIPS = 4
EPS = 1e-6
SB_SCALE = HEAD_DIM ** -0.5

ADAM_LR = 0.001
ADAM_B1 = 0.9
ADAM_B2 = 0.999
ADAM_EPS = 1e-08
ADAM_WD = 0.01
ADAM_STEP = 10

ROW_TILE = 256
WIDE_ROW_TILE = 512
EXP_UNDERFLOW = -104.0
ATT_Q_TILE = 512
ATT_K_TILE = 256
WGRAD_SEQ_TILE = 1024
WGRAD_ACC_BYTES = 4 * 1024 * 1024
MIB = 1024 * 1024


def _params(semantics=None, vmem_mib=48):
    return pltpu.CompilerParams(dimension_semantics=semantics, vmem_limit_bytes=vmem_mib * MIB)


def _dot(a, b):
    return jnp.dot(a, b, preferred_element_type=F32)


def _dot_nt(a, b):
    return lax.dot_general(a, b, (((1,), (1,)), ((), ())), preferred_element_type=F32)


def _dot_tn(a, b):
    return lax.dot_general(a, b, (((0,), (0,)), ((), ())), preferred_element_type=F32)


def _hilo(x):
    hi = x.astype(BF16)
    lo = (x - hi.astype(F32)).astype(BF16)
    return hi, lo


def _dot_hilo(x, w):
    hi, lo = _hilo(x)
    return _dot(hi, w) + _dot(lo, w)


def _sigmoid(z):
    return jax.nn.sigmoid(z)


def _dsilu(z, sg):
    return sg * (1.0 + z * (1.0 - sg))


def _mask_bf16(cond):
    return jnp.where(cond, 1.0, 0.0).astype(BF16)


def _head_mean_matrix():
    r = lax.broadcasted_iota(jnp.int32, (256, 256), 0) // HEAD_DIM
    c = lax.broadcasted_iota(jnp.int32, (256, 256), 1) // HEAD_DIM
    return _mask_bf16(r == c)


def _head_mean(x, bd):
    parts = []
    for s in range(x.shape[1] // 256):
        parts.append(_dot_hilo(x[:, s * 256:(s + 1) * 256], bd))
    out = parts[0] if len(parts) == 1 else jnp.concatenate(parts, axis=1)
    return out * (1.0 / HEAD_DIM)


def _a_in(x, gain, w_sh, gather=()):
    S = x.shape[0]
    tm = 512
    nsh, _, wn = w_sh.shape

    def body(x_ref, g_ref, w_ref, uz_ref, h_ref):
        @pl.when(pl.program_id(1) == 0)
        def _():
            xv = x_ref[...]
            r = lax.rsqrt(jnp.mean(xv * xv, axis=-1, keepdims=True) + EPS)
            h_ref[...] = (xv * r * g_ref[...]).astype(BF16)

        uz_ref[...] = _dot(h_ref[...], w_ref[0])

    return _call_with_gather(
        body, name="a_in", grid=(S // tm, nsh),
        in_specs=[pl.BlockSpec((tm, D_MODEL), lambda i, j: (i, 0)),
                  pl.BlockSpec((1, D_MODEL), lambda i, j: (0, 0)),
                  pl.BlockSpec((1, D_MODEL, wn), lambda i, j: (j, 0, 0))],
        out_specs=[pl.BlockSpec((tm, wn), lambda i, j: (i, j)),
                   pl.BlockSpec((tm, D_MODEL), lambda i, j: (i, 0))],
        out_shape=[jax.ShapeDtypeStruct((S, nsh * wn), F32),
                   jax.ShapeDtypeStruct((S, D_MODEL), BF16)],
        args=(x, gain, w_sh), gather=gather)


def _inv_count(first_row, rows, w):
    t1 = first_row + 1 + lax.broadcasted_iota(jnp.int32, (rows, 1), 0)
    return 1.0 / jnp.minimum(t1, w).astype(F32)


def _group_weight(wg_ref, g):
    return jnp.concatenate([wg_ref[sh, g] for sh in range(N_CHIPS)], axis=0)


def _a_mix(uz, wg, scale, gather=()):
    S = uz.shape[0]
    tm = ROW_TILE

    def body(u_ref, up_ref, z_ref, wg_ref, sc_ref, ga_ref, p_ref):
        i = pl.program_id(0)
        row = lax.broadcasted_iota(jnp.int32, (tm, tm), 0)
        col = lax.broadcasted_iota(jnp.int32, (tm, tm), 1)
        d = row - col
        for g, w in enumerate(POOL_WINDOWS):
            cols = slice(g * GROUP_DIM, (g + 1) * GROUP_DIM)
            t_main = _mask_bf16((d >= 0) & (d < w))
            t_halo = _mask_bf16(d + tm < w)
            u = u_ref[:, cols]
            up = jnp.where(i > 0, up_ref[:, cols], 0.0)
            hi, lo = _hilo(u)
            hip, lop = _hilo(up)
            wsum = _dot(t_main, hi) + _dot(t_main, lo) + _dot(t_halo, hip) + _dot(t_halo, lop)
            pooled = (wsum * _inv_count(i * tm, tm, w) - u).astype(BF16)
            p_ref[:, cols] = pooled
            mraw = _dot(pooled, _group_weight(wg_ref, g))
            z = z_ref[:, cols]
            ga_ref[:, cols] = (mraw * sc_ref[:, cols] * (z * _sigmoid(z))).astype(BF16)

    return _call_with_gather(
        body, name="a_mix", grid=(S // tm,),
        in_specs=[pl.BlockSpec((tm, D_MODEL), lambda i: (i, 0)),
                  pl.BlockSpec((tm, D_MODEL), lambda i: (jnp.maximum(i - 1, 0), 0)),
                  pl.BlockSpec((tm, D_MODEL), lambda i: (i, 1)),
                  pl.BlockSpec((N_CHIPS, N_GROUPS, 64, GROUP_DIM), lambda i: (0, 0, 0, 0)),
                  pl.BlockSpec((1, D_MODEL), lambda i: (0, 0))],
        out_specs=[pl.BlockSpec((tm, D_MODEL), lambda i: (i, 0)),
                   pl.BlockSpec((tm, D_MODEL), lambda i: (i, 0))],
        out_shape=[jax.ShapeDtypeStruct((S, D_MODEL), BF16),
                   jax.ShapeDtypeStruct((S, D_MODEL), BF16)],
        args=(uz, uz, uz, wg, scale), gather=gather)


def _out_ple(name, gated, x_in, w_out, p, layer, ple_w, ple_g, target=None, gather=()):
    S = x_in.shape[0]
    tm = WIDE_ROW_TILE
    with_loss = target is not None

    def body(*refs):
        if with_loss:
            g_ref, x_ref, wo_ref, p_ref, pw_ref, pg_ref, t_ref, xm_ref, dx_ref, e_ref, gt_ref, loss_ref = refs
        else:
            g_ref, x_ref, wo_ref, p_ref, pw_ref, pg_ref, xm_ref, xo_ref, e_ref, gt_ref = refs
        xm = x_ref[...] + _dot(g_ref[...], wo_ref[...])
        xm_ref[...] = xm
        pb = p_ref[...].astype(BF16)
        e = jnp.concatenate([_dot(pb, pw_ref[sh]) for sh in range(N_CHIPS)], axis=1)
        pg = jnp.concatenate([pg_ref[sh] for sh in range(N_CHIPS)], axis=0)
        gate = _sigmoid(_dot(xm.astype(BF16), pg))
        e_ref[...] = e.astype(BF16)
        gt_ref[...] = gate.astype(BF16)
        xo = xm + e * gate
        if with_loss:
            diff = xo - t_ref[...]
            dx_ref[...] = diff * (1.0 / D_MODEL)

            @pl.when(pl.program_id(0) == 0)
            def _():
                loss_ref[...] = jnp.zeros_like(loss_ref)

            loss_ref[...] += jnp.sum(diff * diff) * (0.5 / D_MODEL)
        else:
            xo_ref[...] = xo

    row = pl.BlockSpec((tm, D_MODEL), lambda i: (i, 0))
    in_specs = [row, row,
                pl.BlockSpec((D_MODEL, D_MODEL), lambda i: (0, 0)),
                pl.BlockSpec((None, None, tm, PLE_DIM), lambda i: (layer, 0, i, 0)),
                pl.BlockSpec((N_CHIPS, PLE_DIM, 256), lambda i: (0, 0, 0)),
                pl.BlockSpec((N_CHIPS, 256, D_MODEL), lambda i: (0, 0, 0))]
    args = [gated, x_in, w_out, p, ple_w, ple_g]
    out_specs = [row, row, row, row]
    out_shape = [jax.ShapeDtypeStruct((S, D_MODEL), F32), jax.ShapeDtypeStruct((S, D_MODEL), F32),
                 jax.ShapeDtypeStruct((S, D_MODEL), BF16), jax.ShapeDtypeStruct((S, D_MODEL), BF16)]
    if with_loss:
        in_specs.append(row)
        args.append(target)
        out_specs.append(pl.BlockSpec((8, 128), lambda i: (0, 0)))
        out_shape.append(jax.ShapeDtypeStruct((8, 128), F32))
    return _call_with_gather(body, name=name, grid=(S // tm,), in_specs=in_specs, out_specs=out_specs,
                             out_shape=out_shape, args=args, gather=gather)


def _b_in(x, kv_gain, b_gain, k_gain_t, q_gain_t, w_kv, w_in, gather=()):
    S = x.shape[0]
    tm = ROW_TILE

    def body(x_ref, kvg_ref, bg_ref, kg_ref, qg_ref, wkv_ref, win_ref,
             hkv_ref, hb_ref, kraw_ref, qraw_ref, k_ref, q_ref, v_ref, z_ref):
        xv = x_ref[...]
        y = xv * lax.rsqrt(jnp.mean(xv * xv, axis=-1, keepdims=True) + EPS)
        hkv = (y * kvg_ref[...]).astype(BF16)
        hb = (y * bg_ref[...]).astype(BF16)
        hkv_ref[...] = hkv
        hb_ref[...] = hb
        bd = _head_mean_matrix()

        def head_norm(raw, gain):
            rr = lax.rsqrt(_head_mean(raw * raw, bd) + EPS)
            return raw * rr * gain

        for sh in range(N_CHIPS):
            kvc = _dot(hkv, wkv_ref[sh])
            qzc = _dot(hb, win_ref[sh])
            cols = slice((sh % 2) * 512, (sh % 2) * 512 + 512)
            if sh < 2:
                kraw_ref[:, cols] = kvc.astype(BF16)
                qraw_ref[:, cols] = qzc.astype(BF16)
                k_ref[:, cols] = head_norm(kvc, kg_ref[:, cols]).astype(BF16)
                q_ref[:, cols] = (head_norm(qzc, qg_ref[:, cols]) * SB_SCALE).astype(BF16)
            else:
                v_ref[:, cols] = kvc.astype(BF16)
                z_ref[:, cols] = qzc.astype(BF16)

    row = pl.BlockSpec((tm, D_MODEL), lambda i: (i, 0))
    vec = pl.BlockSpec((1, D_MODEL), lambda i: (0, 0))
    wsp = pl.BlockSpec((N_CHIPS, D_MODEL, 512), lambda i: (0, 0, 0))
    return _call_with_gather(
        body, name="b_in", grid=(S // tm,),
        in_specs=[row, vec, vec, vec, vec, wsp, wsp],
        out_specs=[row] * 8,
        out_shape=[jax.ShapeDtypeStruct((S, D_MODEL), BF16)] * 8,
        args=(x, kv_gain, b_gain, k_gain_t, q_gain_t, w_kv, w_in), gather=gather, vmem_mib=56)


def _softplus_parts(z):
    e = jnp.exp(-jnp.abs(z))
    return -(jnp.maximum(z, 0.0) + jnp.log(1.0 + e)), e


def _add_rows(total, rows, update):
    lo, hi = rows
    parts = ([total[:lo]] if lo else []) + [total[lo:hi] + update] + ([total[hi:]] if hi < total.shape[0] else [])
    return parts[0] if len(parts) == 1 else jnp.concatenate(parts, axis=0)


def _attn_fwd(q, k, v, zgate, gather=()):
    S = q.shape[0]
    tq, tk = ATT_Q_TILE, ATT_K_TILE
    kpq = tq // tk
    assert kpq == 2

    def body(q_ref, k_ref, v_ref, z_ref, o_ref, g_ref, lt_ref, steps_ref):
        qi = pl.program_id(1)
        lane = lax.broadcasted_iota(jnp.int32, (1, 128), 1)
        ri = lax.broadcasted_iota(jnp.int32, (tk, tk), 0)
        ci = lax.broadcasted_iota(jnp.int32, (tk, tk), 1)
        later_mat = _mask_bf16(ri > ci)
        causal = ci < ri
        qv = q_ref[...]
        first = lane < HEAD_DIM
        q_heads = (jnp.where(first, qv, jnp.zeros_like(qv)), jnp.where(first, jnp.zeros_like(qv), qv))

        def step(blocks, carry):
            chains = [(b, h) for b in range(len(blocks)) for h in range(2)]
            rows = [r for _, r, _ in blocks]
            s0 = [pl.multiple_of(kj * tk, tk) for kj, _, _ in blocks]
            kb = [k_ref[pl.ds(s, tk), :] for s in s0]
            vb = [v_ref[pl.ds(s, tk), :] for s in s0]
            visible = [causal if masked else None for _, _, masked in blocks]
            z = {c: _dot_nt(q_heads[c[1]][rows[c[0]][0]:rows[c[0]][1]], kb[c[0]]) for c in chains}
            run = [carry[0], carry[2]]
            log_own, later, run_at = {}, {}, {}
            for c in chains:
                b, h = c
                lk = _softplus_parts(z[c])[0]
                if visible[b] is not None:
                    lk = jnp.where(visible[b], lk, 0.0)
                log_own[c] = z[c] + lk
                later[c] = _dot(lk.astype(BF16), later_mat)
                run_at[c] = run[h][rows[b][0]:rows[b][1]]
                run[h] = _add_rows(run[h], rows[b], jnp.sum(lk, axis=-1, keepdims=True))
            acc = [carry[1], carry[3]]
            for c in chains:
                b, h = c
                a = jnp.exp(log_own[c] + later[c] + run_at[c])
                if visible[b] is not None:
                    a = jnp.where(visible[b], a, 0.0)
                acc[h] = _add_rows(acc[h], rows[b], _dot(a.astype(BF16), vb[b]))
            return run[0], acc[0], run[1], acc[1]

        zero1, zero128 = jnp.zeros((tq, 1), F32), jnp.zeros((tq, 128), F32)
        carry = step([(qi * kpq + 1, (tk, tq), True), (qi * kpq, (tk, tq), False), (qi * kpq, (0, tk), True)],
                     (zero1, zero128, zero1, zero128))

        def low(run):
            return jnp.max(run)

        def pair_more(c):
            return (c[0] < qi) & (jnp.maximum(low(c[1][tk:]), low(c[3][tk:])) > EXP_UNDERFLOW)

        def pair_step(c):
            last = (qi - c[0]) * kpq - 1
            return (c[0] + 1, *step([(last, (0, tq), False), (last - 1, (0, tq), False)], c[1:]))

        pairs, *carry = lax.while_loop(pair_more, pair_step, (jnp.int32(0), *carry))
        left = (qi - pairs) * kpq

        def single_more(c):
            return (c[0] < left) & (jnp.maximum(low(c[1][:tk]), low(c[3][:tk])) > EXP_UNDERFLOW)

        def single_step(c):
            return (c[0] + 1, *step([(left - 1 - c[0], (0, tk), False)], c[1:]))

        singles, *carry = lax.while_loop(single_more, single_step, (jnp.int32(0), *carry))
        steps_ref[...] = jnp.concatenate([jnp.full((4, 128), pairs, F32), jnp.full((4, 128), singles, F32)], axis=0)
        o_tot = jnp.where(first, carry[1], carry[3])
        l_tot = jnp.where(first, carry[0], carry[2])
        o_ref[...] = o_tot.astype(BF16)
        lt_ref[...] = l_tot
        zz = z_ref[...].astype(F32)
        g_ref[...] = (o_tot * (zz * _sigmoid(zz))).astype(BF16)

    blk = pl.BlockSpec((tq, 128), lambda hp, qi: (qi, hp))
    seq = pl.BlockSpec((S, 128), lambda hp, qi: (0, hp))
    return _call_with_gather(
        body, name="attn_fwd", grid=(D_MODEL // 128, S // tq),
        in_specs=[blk, seq, seq, blk],
        out_specs=[blk, blk, blk, pl.BlockSpec((None, None, 8, 128), lambda hp, qi: (hp, qi, 0, 0))],
        out_shape=[jax.ShapeDtypeStruct((S, D_MODEL), BF16)] * 2 + [jax.ShapeDtypeStruct((S, D_MODEL), F32)]
        + [jax.ShapeDtypeStruct((D_MODEL // 128, S // tq, 8, 128), F32)],
        args=(q, k, v, zgate), gather=gather)


def _ple_out_bwd(name, dx_out, e, gate, ple_g, w_out):
    S = dx_out.shape[0]
    tm = WIDE_ROW_TILE

    def body(dx_ref, e_ref, gt_ref, pg_ref, wo_ref, de_ref, dgp_ref, dxm_ref, dg_ref):
        dxo = dx_ref[...]
        ev = e_ref[...].astype(F32)
        gv = gt_ref[...].astype(F32)
        de_ref[...] = (dxo * gv).astype(BF16)
        dgp = (dxo * ev * gv * (1.0 - gv)).astype(BF16)
        dgp_ref[...] = dgp
        pg = jnp.concatenate([pg_ref[sh] for sh in range(N_CHIPS)], axis=0)
        dxm = dxo + _dot_nt(dgp, pg)
        dxm_ref[...] = dxm
        dg_ref[...] = _dot_nt(dxm.astype(BF16), wo_ref[...]).astype(BF16)

    row = pl.BlockSpec((tm, D_MODEL), lambda i: (i, 0))
    return pl.pallas_call(
        body, name=name, grid=(S // tm,),
        in_specs=[row, row, row,
                  pl.BlockSpec((N_CHIPS, 256, D_MODEL), lambda i: (0, 0, 0)),
                  pl.BlockSpec((D_MODEL, D_MODEL), lambda i: (0, 0))],
        out_specs=[row, row, row, row],
        out_shape=[jax.ShapeDtypeStruct((S, D_MODEL), BF16), jax.ShapeDtypeStruct((S, D_MODEL), BF16),
                   jax.ShapeDtypeStruct((S, D_MODEL), F32), jax.ShapeDtypeStruct((S, D_MODEL), BF16)],
        compiler_params=_params(("arbitrary",)),
    )(dx_out, e, gate, ple_g, w_out)


def _attn_bwd(q, k, v, ltot, steps, dgated, o, zgate, reduce=None):
    S = q.shape[0]
    tq, tk = ATT_Q_TILE, ATT_K_TILE
    kpq = tq // tk
    nq = S // tq

    def body(q_ref, k_ref, v_ref, lt_ref, steps_ref, dg_ref, o_ref, z_ref, dq_ref, dk_ref, dv_ref, dz_ref,
             dk_acc, dv_acc):
        qi = pl.program_id(1)

        @pl.when(qi == 0)
        def _():
            dk_acc[...] = jnp.zeros_like(dk_acc)
            dv_acc[...] = jnp.zeros_like(dv_acc)

        lane = lax.broadcasted_iota(jnp.int32, (1, 128), 1)
        ri = lax.broadcasted_iota(jnp.int32, (tk, tk), 0)
        ci = lax.broadcasted_iota(jnp.int32, (tk, tk), 1)
        later_mat = _mask_bf16(ri > ci)
        before_mat = _mask_bf16(ri < ci)
        causal = ci < ri
        zz = z_ref[...].astype(F32)
        sg = _sigmoid(zz)
        dgv = dg_ref[...].astype(F32)
        dz_ref[...] = (dgv * o_ref[...].astype(F32) * _dsilu(zz, sg)).astype(BF16)
        dob = (dgv * (zz * sg)).astype(BF16)
        ltv = lt_ref[...]
        qv = q_ref[...]
        first = lane < HEAD_DIM
        masks = (first, jnp.logical_not(first))
        q_heads = [jnp.where(hm, qv, jnp.zeros_like(qv)) for hm in masks]
        do_heads = [jnp.where(hm, dob, jnp.zeros_like(dob)) for hm in masks]
        totals = [jnp.max(jnp.where(hm, ltv, -jnp.inf), axis=-1, keepdims=True) for hm in masks]

        def step(blocks, carry):
            chains = [(b, h) for b in range(len(blocks)) for h in range(2)]
            rows = [r for _, r, _ in blocks]
            cut = lambda t, b: t[rows[b][0]:rows[b][1]]
            s0 = [pl.multiple_of(kj * tk, tk) for kj, _, _ in blocks]
            kb = [k_ref[pl.ds(s, tk), :] for s in s0]
            vb = [v_ref[pl.ds(s, tk), :] for s in s0]
            visible = [causal if masked else None for _, _, masked in blocks]
            z = {c: _dot_nt(cut(q_heads[c[1]], c[0]), kb[c[0]]) for c in chains}
            da = {c: _dot_nt(cut(do_heads[c[1]], c[0]), vb[c[0]]) for c in chains}
            run = [carry[0], carry[3]]
            log_own, beta, later, base = {}, {}, {}, {}
            for c in chains:
                b, h = c
                lk = _softplus_parts(z[c])[0]
                if visible[b] is not None:
                    lk = jnp.where(visible[b], lk, 0.0)
                log_own[c] = z[c] + lk
                beta[c] = jnp.exp(log_own[c]).astype(BF16)
                later[c] = _dot(lk.astype(BF16), later_mat)
                run[h] = _add_rows(run[h], rows[b], jnp.sum(lk, axis=-1, keepdims=True))
                base[c] = cut(totals[h] - run[h], b)
            grun = [carry[1], carry[4]]
            a_bf, g_bf, gbefore, grun_at = {}, {}, {}, {}
            for c in chains:
                b, h = c
                a = jnp.exp(log_own[c] + later[c] + base[c])
                if visible[b] is not None:
                    a = jnp.where(visible[b], a, 0.0)
                a_bf[c] = a.astype(BF16)
                g = da[c] * a
                g_bf[c] = g.astype(BF16)
                gbefore[c] = _dot(g_bf[c], before_mat)
                grun_at[c] = cut(grun[h], b)
                grun[h] = _add_rows(grun[h], rows[b], jnp.sum(g, axis=-1, keepdims=True))
            dq = [carry[2], carry[5]]
            dk_blk = [jnp.zeros((tk, 128), F32) for _ in blocks]
            dv_blk = [jnp.zeros((tk, 128), F32) for _ in blocks]
            for c in chains:
                b, h = c
                g = g_bf[c].astype(F32)
                dz = g - beta[c].astype(F32) * (g + gbefore[c] + grun_at[c])
                if visible[b] is not None:
                    dz = jnp.where(visible[b], dz, 0.0)
                dzb = dz.astype(BF16)
                dq[h] = _add_rows(dq[h], rows[b], _dot(dzb, kb[b]))
                dk_blk[b] = dk_blk[b] + _dot_tn(dzb, cut(q_heads[h], b))
                dv_blk[b] = dv_blk[b] + _dot_tn(a_bf[c], cut(do_heads[h], b))
            for b in range(len(blocks)):
                dk_acc[pl.ds(s0[b], tk), :] += dk_blk[b]
                dv_acc[pl.ds(s0[b], tk), :] += dv_blk[b]
            return run[0], grun[0], dq[0], run[1], grun[1], dq[1]

        pairs = jnp.clip(jnp.max(steps_ref[0:4, :]).astype(jnp.int32), 0, qi)
        left = (qi - pairs) * kpq
        singles = jnp.clip(jnp.max(steps_ref[4:8, :]).astype(jnp.int32), 0, left)
        zero1, zero128 = jnp.zeros((tq, 1), F32), jnp.zeros((tq, 128), F32)
        carry = lax.fori_loop(left - singles, left, lambda kj, c: step([(kj, (0, tk), False)], c),
                              (zero1, zero1, zero128, zero1, zero1, zero128))
        carry = lax.fori_loop(qi - pairs, qi,
                              lambda n, c: step([(n * kpq, (0, tq), False), (n * kpq + 1, (0, tq), False)], c), carry)
        carry = step([(qi * kpq, (0, tk), True), (qi * kpq, (tk, tq), False), (qi * kpq + 1, (tk, tq), True)], carry)
        dq_ref[...] = jnp.where(first, carry[2], carry[5]).astype(BF16)

        @pl.when(qi == nq - 1)
        def _():
            dk_ref[...] = dk_acc[...].astype(BF16)
            dv_ref[...] = dv_acc[...].astype(BF16)

    blk = pl.BlockSpec((tq, 128), lambda hp, qi: (qi, hp))
    seq = pl.BlockSpec((S, 128), lambda hp, qi: (0, hp))
    return _call_with_gather(
        body, name="attn_bwd", grid=(D_MODEL // 128, nq),
        in_specs=[blk, seq, seq, blk, pl.BlockSpec((None, None, 8, 128), lambda hp, qi: (hp, qi, 0, 0)),
                  blk, blk, blk],
        out_specs=[blk, seq, seq, blk],
        out_shape=[jax.ShapeDtypeStruct((S, D_MODEL), BF16)] * 4,
        scratch_shapes=[pltpu.VMEM((S, 128), F32), pltpu.VMEM((S, 128), F32)],
        args=(q, k, v, ltot, steps, dgated, o, zgate), reduce=reduce, vmem_mib=56)


def _rms_bwd(xv, dh_gain_sum):
    r = lax.rsqrt(jnp.mean(xv * xv, axis=-1, keepdims=True) + EPS)
    xhat = xv * r
    dx = r * (dh_gain_sum - xhat * jnp.mean(dh_gain_sum * xhat, axis=-1, keepdims=True))
    return dx, xhat


def _b_in_bwd(dq, dk, dv, dz, q_raw, k_raw, x, dx_mid, q_gain_t, k_gain_t, b_gain, kv_gain, w_in, w_kv):
    S = x.shape[0]
    tm = ROW_TILE

    def body(dq_ref, dk_ref, dv_ref, dz_ref, qr_ref, kr_ref, x_ref, dxm_ref, qg_ref, kg_ref, bg_ref, kvg_ref,
             win_ref, wkv_ref, dqz_ref, dkv_ref, dx_ref, small_ref):
        @pl.when(pl.program_id(0) == 0)
        def _():
            small_ref[...] = jnp.zeros_like(small_ref)

        bd = _head_mean_matrix()

        def head_norm_bwd(dy_ref, raw_ref, gain, scale):
            raw = raw_ref[...].astype(F32)
            rr = lax.rsqrt(_head_mean(raw * raw, bd) + EPS)
            xhat = raw * rr
            dy = dy_ref[...].astype(F32) * scale
            gdy = dy * gain
            draw = rr * (gdy - xhat * _head_mean(gdy * xhat, bd))
            return draw.astype(BF16), jnp.sum(dy * xhat, axis=0, keepdims=True)

        dqr, dqg = head_norm_bwd(dq_ref, qr_ref, qg_ref[...], SB_SCALE)
        dkr, dkg = head_norm_bwd(dk_ref, kr_ref, kg_ref[...], 1.0)
        dqz_ref[:, :D_MODEL] = dqr
        dqz_ref[:, D_MODEL:] = dz_ref[...]
        dkv_ref[:, :D_MODEL] = dkr
        dkv_ref[:, D_MODEL:] = dv_ref[...]
        dhb = jnp.zeros((tm, D_MODEL), F32)
        dhkv = jnp.zeros((tm, D_MODEL), F32)
        for sh in range(N_CHIPS):
            cols = slice(sh * 512, (sh + 1) * 512)
            dhb = dhb + _dot_nt(dqz_ref[:, cols], win_ref[sh])
            dhkv = dhkv + _dot_nt(dkv_ref[:, cols], wkv_ref[sh])
        dx, xhat = _rms_bwd(x_ref[...], dhb * bg_ref[...] + dhkv * kvg_ref[...])
        dx_ref[...] = dxm_ref[...] + dx
        small_ref[0:1, :] += dqg
        small_ref[1:2, :] += dkg
        small_ref[2:3, :] += jnp.sum(dhb * xhat, axis=0, keepdims=True)
        small_ref[3:4, :] += jnp.sum(dhkv * xhat, axis=0, keepdims=True)

    row = pl.BlockSpec((tm, D_MODEL), lambda i: (i, 0))
    wide = pl.BlockSpec((tm, 2 * D_MODEL), lambda i: (i, 0))
    vec = pl.BlockSpec((1, D_MODEL), lambda i: (0, 0))
    wsp = pl.BlockSpec((N_CHIPS, D_MODEL, 512), lambda i: (0, 0, 0))
    return pl.pallas_call(
        body, name="b_in_bwd", grid=(S // tm,),
        in_specs=[row] * 8 + [vec] * 4 + [wsp, wsp],
        out_specs=[wide, wide, row, pl.BlockSpec((8, D_MODEL), lambda i: (0, 0))],
        out_shape=[jax.ShapeDtypeStruct((S, 2 * D_MODEL), BF16), jax.ShapeDtypeStruct((S, 2 * D_MODEL), BF16),
                   jax.ShapeDtypeStruct((S, D_MODEL), F32), jax.ShapeDtypeStruct((8, D_MODEL), F32)],
        compiler_params=_params(("arbitrary",), 56),
    )(dq, dk, dv, dz, q_raw, k_raw, x, dx_mid, q_gain_t, k_gain_t, b_gain, kv_gain, w_in, w_kv)


def _a_mix_bwd(dgated, uz, pooled, wg, scale, w_in, x, dx_mid, gain, reduce=None):
    S = x.shape[0]
    tm = ROW_TILE
    n = S // tm

    def body(dg_ref, z_ref, p_ref, wg_ref, sc_ref, win_ref, x_ref, dxm_ref, gn_ref,
             duz_ref, dmr_ref, dx_ref, small_ref, halo_hi, halo_lo):
        i = pl.program_id(0)

        @pl.when(i == 0)
        def _():
            small_ref[...] = jnp.zeros_like(small_ref)
            halo_hi[...] = jnp.zeros_like(halo_hi)
            halo_lo[...] = jnp.zeros_like(halo_lo)

        first_row = (n - 1 - i) * tm
        row = lax.broadcasted_iota(jnp.int32, (tm, tm), 0)
        col = lax.broadcasted_iota(jnp.int32, (tm, tm), 1)
        d = col - row
        for g, w in enumerate(POOL_WINDOWS):
            cols = slice(g * GROUP_DIM, (g + 1) * GROUP_DIM)
            wgg = _group_weight(wg_ref, g)
            sc = sc_ref[:, cols]
            mraw = _dot(p_ref[:, cols], wgg)
            z = z_ref[:, cols]
            sg = _sigmoid(z)
            dga = dg_ref[:, cols].astype(F32)
            dm = dga * (z * sg)
            duz_ref[:, D_MODEL + g * GROUP_DIM:D_MODEL + (g + 1) * GROUP_DIM] = (
                dga * (mraw * sc) * _dsilu(z, sg)).astype(BF16)
            small_ref[0:1, cols] += jnp.sum(dm * mraw, axis=0, keepdims=True)
            dmr = (dm * sc).astype(BF16)
            dmr_ref[:, cols] = dmr
            dp = _dot_nt(dmr, wgg)
            hi, lo = _hilo(dp * _inv_count(first_row, tm, w))
            t_main = _mask_bf16((d >= 0) & (d < w))
            t_halo = _mask_bf16(d + tm < w)
            du = (_dot(t_main, hi) + _dot(t_main, lo) + _dot(t_halo, halo_hi[:, cols]) + _dot(t_halo, halo_lo[:, cols])
                  - dp)
            halo_hi[:, cols] = hi
            halo_lo[:, cols] = lo
            duz_ref[:, cols] = du.astype(BF16)
        dh = jnp.zeros((tm, D_MODEL), F32)
        for sh in range(N_CHIPS):
            dh = dh + _dot_nt(duz_ref[:, sh * 512:(sh + 1) * 512], win_ref[sh])
        dx, xhat = _rms_bwd(x_ref[...], dh * gn_ref[...])
        dx_ref[...] = dxm_ref[...] + dx
        small_ref[1:2, :] += jnp.sum(dh * xhat, axis=0, keepdims=True)

    rev = lambda i: (n - 1 - i, 0)
    row = pl.BlockSpec((tm, D_MODEL), rev)
    vec = pl.BlockSpec((1, D_MODEL), lambda i: (0, 0))
    return _call_with_gather(
        body, name="a_mix_bwd", grid=(n,),
        in_specs=[row,
                  pl.BlockSpec((tm, D_MODEL), lambda i: (n - 1 - i, 1)),
                  row,
                  pl.BlockSpec((N_CHIPS, N_GROUPS, 64, GROUP_DIM), lambda i: (0, 0, 0, 0)),
                  vec,
                  pl.BlockSpec((N_CHIPS, D_MODEL, 512), lambda i: (0, 0, 0)),
                  row, row, vec],
        out_specs=[pl.BlockSpec((tm, 2 * D_MODEL), rev), row, row,
                   pl.BlockSpec((8, D_MODEL), lambda i: (0, 0))],
        out_shape=[jax.ShapeDtypeStruct((S, 2 * D_MODEL), BF16), jax.ShapeDtypeStruct((S, D_MODEL), BF16),
                   jax.ShapeDtypeStruct((S, D_MODEL), F32), jax.ShapeDtypeStruct((8, D_MODEL), F32)],
        scratch_shapes=[pltpu.VMEM((tm, D_MODEL), BF16), pltpu.VMEM((tm, D_MODEL), BF16)],
        args=(dgated, uz, pooled, wg, scale, w_in, x, dx_mid, gain), reduce=reduce, vmem_mib=56)


def _wgrad(name, a, dy, n_shards, a_spec=None, k_dim=None):
    S, n_cols = dy.shape
    ts = WGRAD_SEQ_TILE
    k_dim = a.shape[-1] if k_dim is None else k_dim
    wn = n_cols // n_shards
    tk = min(k_dim, WGRAD_ACC_BYTES // (4 * n_cols))
    nst = S // ts

    def body(a_ref, dy_ref, out_ref, acc):
        st = pl.program_id(1)

        @pl.when(st == 0)
        def _():
            acc[...] = jnp.zeros_like(acc)

        acc[...] += _dot_tn(a_ref[...].astype(BF16), dy_ref[...].astype(BF16))

        @pl.when(st == nst - 1)
        def _():
            for sh in range(n_shards):
                out_ref[sh] = acc[:, sh * wn:(sh + 1) * wn]

    if a_spec is None:
        a_spec = pl.BlockSpec((ts, tk), lambda kt, st: (st, kt))
    return pl.pallas_call(
        body, name=name, grid=(k_dim // tk, nst),
        in_specs=[a_spec, pl.BlockSpec((ts, n_cols), lambda kt, st: (st, 0))],
        out_specs=pl.BlockSpec((n_shards, tk, wn), lambda kt, st: (0, kt, 0)),
        out_shape=jax.ShapeDtypeStruct((n_shards, k_dim, wn), F32),
        scratch_shapes=[pltpu.VMEM((tk, n_cols), F32)],
        compiler_params=_params(("parallel", "arbitrary")),
    )(a, dy)


def _wgrad_ple(name, p, layer, de):
    ts = WGRAD_SEQ_TILE
    spec = pl.BlockSpec((None, None, ts, PLE_DIM), lambda kt, st: (layer, 0, st, 0))
    return _wgrad(name, p, de, N_CHIPS, a_spec=spec, k_dim=PLE_DIM)


def _wgrad_group(pooled, dmr):
    S = pooled.shape[0]
    ts = WGRAD_SEQ_TILE
    nst = S // ts

    def body(p_ref, d_ref, out_ref, acc):
        st = pl.program_id(1)

        @pl.when(st == 0)
        def _():
            acc[...] = jnp.zeros_like(acc)

        acc[...] += _dot_tn(p_ref[...], d_ref[...])

        @pl.when(st == nst - 1)
        def _():
            for sh in range(N_CHIPS):
                out_ref[sh] = acc[sh * 64:(sh + 1) * 64, :]

    blk = pl.BlockSpec((ts, GROUP_DIM), lambda g, st: (st, g))
    return pl.pallas_call(
        body, name="wgrad_group", grid=(N_GROUPS, nst),
        in_specs=[blk, blk],
        out_specs=pl.BlockSpec((N_CHIPS, None, 64, GROUP_DIM), lambda g, st: (0, g, 0, 0)),
        out_shape=jax.ShapeDtypeStruct((N_CHIPS, N_GROUPS, 64, GROUP_DIM), F32),
        scratch_shapes=[pltpu.VMEM((GROUP_DIM, GROUP_DIM), F32)],
        compiler_params=_params(("parallel", "arbitrary")),
    )(pooled, dmr)


GATHER_AT = {
    "a_in": ("a_w_group", "a_w_out", "ple_w0", "ple_gate_w0"),
    "a_mix": ("w_kv",),
    "a_out_ple": ("b_w_in",),
    "attn_fwd": ("b_w_out", "ple_w1", "ple_gate_w1"),
}


REDUCE_AT = {
    "attn_bwd": ("b_w_out", "ple_w1", "ple_gate_w1"),
    "a_mix_bwd": ("a_w_out", "ple_w0", "ple_gate_w0"),
}


def _local_step(x, p, target, w, local=None, state=None):
    w = dict(w)

    def run(fn, host, n_out, *args, **kwargs):
        names = GATHER_AT[host] if local is not None else ()
        res = fn(*args, gather=[local[n] for n in names], **kwargs)
        w.update(zip(names, res[n_out:]))
        return res[:n_out]

    k_gain_t = jnp.tile(w["k_norm"].reshape(1, HEAD_DIM), (1, N_HEADS))
    q_gain_t = jnp.tile(w["b_q_norm"].reshape(1, HEAD_DIM), (1, N_HEADS))

    uz, h_a = run(_a_in, "a_in", 2, x, w["a_norm"], w["a_w_in"])
    wg4 = w["a_w_group"].reshape(N_CHIPS, N_GROUPS, 64, GROUP_DIM)
    wa_out = w["a_w_out"].reshape(D_MODEL, D_MODEL)
    gated_a, pooled = run(_a_mix, "a_mix", 2, uz, wg4, w["a_scale"])
    x1, x2, e_a, gate_a = run(_out_ple, "a_out_ple", 4, "a_out_ple", gated_a, x, wa_out, p, 0,
                              w["ple_w0"], w["ple_gate_w0"])
    h_kv, h_b, k_raw, q_raw, k, q, v, z_b = _b_in(
        x2, w["kv_norm"], w["b_norm"], k_gain_t, q_gain_t, w["w_kv"], w["b_w_in"])
    o, gated_b, ltot, att_steps = run(_attn_fwd, "attn_fwd", 4, q, k, v, z_b)
    wb_out = w["b_w_out"].reshape(D_MODEL, D_MODEL)
    x3, dx4, e_b, gate_b, loss_blk = _out_ple("b_out_ple", gated_b, x2, wb_out, p, 1, w["ple_w1"], w["ple_gate_w1"],
                                              target=target)

    grads, updates = {}, {}

    def hosted(fn, host, n_out, *args):
        if state is None:
            return fn(*args)
        names = REDUCE_AT[host]
        seeds = [updates.get(n[:-1] + "1") if n.startswith("ple") and n.endswith("0") else None for n in names]
        res = fn(*args, reduce=([grads.pop(n) for n in names], *[[t[n] for n in names] for t in state[:3]],
                                [state[3][n] for n in names], seeds))
        for i, n in enumerate(names):
            updates[n] = tuple(group[i] for group in res[n_out:])
        return res[:n_out]

    de_b, dgp_b, dx3, dgated_b = _ple_out_bwd("b_ple_out_bwd", dx4, e_b, gate_b, w["ple_gate_w1"], wb_out)
    grads["b_w_out"] = _wgrad("wgrad_b_out", gated_b, dx3, 1).reshape(N_CHIPS, 256, D_MODEL)
    grads["ple_w1"] = _wgrad_ple("wgrad_ple1", p, 1, de_b)
    grads["ple_gate_w1"] = _wgrad("wgrad_gate1", x3, dgp_b, 1).reshape(N_CHIPS, 256, D_MODEL)
    dq, dk, dv, dz_b = hosted(_attn_bwd, "attn_bwd", 4, q, k, v, ltot, att_steps, dgated_b, o, z_b)
    dqz, dkv, dx2, small_b = _b_in_bwd(dq, dk, dv, dz_b, q_raw, k_raw, x2, dx3, q_gain_t, k_gain_t,
                                       w["b_norm"], w["kv_norm"], w["b_w_in"], w["w_kv"])
    grads["w_kv"] = _wgrad("wgrad_kv", h_kv, dkv, N_CHIPS)
    grads["b_w_in"] = _wgrad("wgrad_b_in", h_b, dqz, N_CHIPS)
    de_a, dgp_a, dx1, dgated_a = _ple_out_bwd("a_ple_out_bwd", dx2, e_a, gate_a, w["ple_gate_w0"], wa_out)
    grads["a_w_out"] = _wgrad("wgrad_a_out", gated_a, dx1, 1).reshape(N_CHIPS, 256, D_MODEL)
    grads["ple_w0"] = _wgrad_ple("wgrad_ple0", p, 0, de_a)
    grads["ple_gate_w0"] = _wgrad("wgrad_gate0", x1, dgp_a, 1).reshape(N_CHIPS, 256, D_MODEL)
    duz, dmr, grad_x, small_a = hosted(_a_mix_bwd, "a_mix_bwd", 4, dgated_a, uz, pooled, wg4, w["a_scale"],
                                       w["a_w_in"], x, dx1, w["a_norm"])
    grads["a_w_in"] = _wgrad("wgrad_a_in", h_a, duz, N_CHIPS)
    grads["a_w_group"] = _wgrad_group(pooled, dmr).reshape(N_CHIPS, N_GROUPS * 64, GROUP_DIM)

    fold = lambda row: jnp.pad(row.reshape(N_HEADS, HEAD_DIM).sum(axis=0), (0, D_MODEL - HEAD_DIM))
    small = jnp.stack([small_a[1], small_a[0], small_b[3], small_b[2], fold(small_b[1]), fold(small_b[0]),
                       jnp.pad(loss_blk[0], (0, D_MODEL - loss_blk.shape[1])), jnp.zeros((D_MODEL,), F32)])
    return grad_x, grads, updates, small


def _mesh_place():
    x, y, c = lax.axis_index("x"), lax.axis_index("y"), lax.axis_index("c")
    other_chips = [(1 - x, y), (x, 1 - y), (1 - x, 1 - y)]
    return x, y, c, other_chips


def _gather_sems(n):
    return [pltpu.SemaphoreType.DMA((3 * n,)), pltpu.SemaphoreType.DMA((3 * n,)),
            pltpu.SemaphoreType.DMA((3 * n,)), pltpu.SemaphoreType.DMA((3 * n,)), pltpu.SemaphoreType.DMA((n,))]


def _gather_copies(srcs, outs, sems):
    send_far, recv_far, send_sib, recv_sib, local_sem = sems
    n = len(srcs)
    x, y, c, chips = _mesh_place()
    me = 2 * x + y
    sibling = (x, y, 1 - c)

    def half(k, which):
        rows = srcs[k].shape[0] // 2
        return pl.ds(pl.multiple_of(which * rows, 16), rows)

    local = [pltpu.make_async_copy(srcs[k], outs[k].at[me], local_sem.at[k]) for k in range(n)]
    far = [pltpu.make_async_remote_copy(
        src_ref=srcs[k].at[half(k, c)], dst_ref=outs[k].at[me, half(k, c)],
        send_sem=send_far.at[j * n + k], recv_sem=recv_far.at[j * n + k], device_id=(px, py, c), device_id_type=MESH)
        for j, (px, py) in enumerate(chips) for k in range(n)]

    def landed(j, k, which, from_far):
        px, py = chips[j]
        piece = outs[k].at[2 * px + py, half(k, which)]
        send, recv = (send_far, recv_far) if from_far else (send_sib, recv_sib)
        return pltpu.make_async_remote_copy(src_ref=piece, dst_ref=piece, send_sem=send.at[j * n + k],
                                            recv_sem=recv.at[j * n + k], device_id=sibling, device_id_type=MESH)

    return local, far, landed, c


def _gather_start(srcs, outs, sems):
    local, far, _, _ = _gather_copies(srcs, outs, sems)
    for cp in local + far:
        cp.start()


def _gather_pass_on(srcs, outs, sems):
    _, _, landed, c = _gather_copies(srcs, outs, sems)
    for j in range(3):
        for k in range(len(srcs)):
            landed(j, k, c, True).wait_recv()
            landed(j, k, c, False).start()


def _gather_finish(srcs, outs, sems):
    local, far, landed, c = _gather_copies(srcs, outs, sems)
    pairs = [(j, k) for j in range(3) for k in range(len(srcs))]
    for j, k in pairs:
        landed(j, k, 1 - c, False).wait_recv()
    for cp in far + [landed(j, k, c, False) for j, k in pairs]:
        cp.wait_send()
    for cp in local:
        cp.wait()


def _call_with_gather(body, *, name, grid, in_specs, out_specs, out_shape, args, gather=(), reduce=None,
                      scratch_shapes=(), vmem_mib=48):
    n_in, n_out, n_scr, n_g = len(args), len(out_shape), len(scratch_shapes), len(gather)
    n_r = len(reduce[0]) if reduce else 0
    pieces = _reduce_pieces(reduce[0], reduce[4]) if reduce else []
    reduce_args = [a for group in reduce[:4] for a in group] if reduce else []
    seeds = reduce[5] if reduce else []
    seeded = [(k, a) for k, seed in enumerate(seeds) if seed is not None for a in range(4)]
    gather_sems = _gather_sems(n_g) if n_g else []
    n_steps = 1
    for g in grid:
        n_steps *= g

    def wrapped(*refs):
        refs = list(refs)
        take = lambda count: [refs.pop(0) for _ in range(count)]
        ins, g_in, r_in = take(n_in), take(n_g), take(4 * n_r)
        take(len(seeded))
        outs, g_out, r_out = take(n_out), take(n_g), take(4 * n_r)
        scratch, sems, r_scratch = take(n_scr), take(len(gather_sems)), refs
        step = 0
        for axis, g in enumerate(grid):
            step = step * g + pl.program_id(axis)
        if n_g:
            @pl.when(step == 0)
            def _():
                _gather_start(g_in, g_out, sems)

        if n_r:
            ticks, drain = _reduce_ticks(pieces, n_r, (*r_in, *r_out, *r_scratch))
            for t, tick in enumerate(ticks[:n_steps]):
                pl.when(step == t)(tick)

        body(*ins, *outs, *scratch)
        if n_r:
            for tick in ticks[n_steps:]:
                pl.when(step == n_steps - 1)(tick)
            pl.when(step == n_steps - 1)(drain)
        if n_g:
            @pl.when(step == max(n_steps - 2, 0))
            def _():
                _gather_pass_on(g_in, g_out, sems)

            @pl.when(step == n_steps - 1)
            def _():
                _gather_finish(g_in, g_out, sems)

    hbm = pl.BlockSpec(memory_space=pltpu.HBM)
    res = pl.pallas_call(
        wrapped, name=name, grid=grid,
        in_specs=list(in_specs) + [hbm] * (n_g + 4 * n_r + len(seeded)),
        out_specs=list(out_specs) + [hbm] * (n_g + 4 * n_r),
        out_shape=list(out_shape) + [jax.ShapeDtypeStruct((N_CHIPS,) + g.shape, BF16) for g in gather]
        + ([jax.ShapeDtypeStruct(w.shape, F32) for _ in range(4) for w in reduce[1]] if reduce else []),
        input_output_aliases={n_in + n_g + 4 * n_r + i: n_out + n_g + a * n_r + k for i, (k, a) in enumerate(seeded)},
        scratch_shapes=list(scratch_shapes) + gather_sems + (_reduce_scratch() if reduce else []),
        compiler_params=_params(("arbitrary",) * len(grid), vmem_mib),
    )(*args, *gather, *reduce_args, *[seeds[k][a] for k, a in seeded])
    if not reduce:
        return res
    plain = list(res[:n_out + n_g])
    return plain + [res[n_out + n_g + i * n_r:n_out + n_g + (i + 1) * n_r] for i in range(4)]


def _allgather_weights(shards, small, casts):
    n = len(shards)
    cast_out = [(k, r0, r1) for k, (_, ranges) in enumerate(casts) for r0, r1 in ranges]
    n_c, n_co = len(casts), len(cast_out)

    def body(*refs):
        ins, small_in, cast_in = refs[:n], refs[n], refs[n + 1:n + 1 + n_c]
        refs = refs[n + 1 + n_c:]
        outs, small_out, cast_dst = refs[:n], refs[n], refs[n + 1:n + 1 + n_co]
        refs = refs[n + 1 + n_co:]
        cast, cast_buf = refs[:n], refs[n:n + n_co]
        send_far, recv_far, send_sib, recv_sib, send_small, recv_small, local_sem, cast_sem = refs[n + n_co:]
        x, y, c, chips = _mesh_place()
        me = 2 * x + y
        sibling = (x, y, 1 - c)

        def half(k, which):
            rows = ins[k].shape[0] // 2
            return pl.ds(pl.multiple_of(which * rows, 16), rows)

        local = []
        for k in range(n):
            cast[k][...] = ins[k][...].astype(BF16)
            local.append(pltpu.make_async_copy(cast[k], outs[k].at[me], local_sem.at[k]))
            local[-1].start()
        local.append(pltpu.make_async_copy(small_in, small_out.at[me], local_sem.at[n]))
        local[-1].start()

        sends = []
        for j, (px, py) in enumerate(chips):
            for k in range(n):
                cp = pltpu.make_async_remote_copy(
                    src_ref=cast[k].at[half(k, c)], dst_ref=outs[k].at[me, half(k, c)],
                    send_sem=send_far.at[j * n + k], recv_sem=recv_far.at[j * n + k],
                    device_id=(px, py, c), device_id_type=MESH)
                cp.start()
                sends.append(cp)
            cp = pltpu.make_async_remote_copy(
                src_ref=small_in, dst_ref=small_out.at[me], send_sem=send_small.at[j], recv_sem=recv_small.at[j],
                device_id=(px, py, c), device_id_type=MESH)
            cp.start()
            sends.append(cp)

        for i, (k, r0, r1) in enumerate(cast_out):
            cast_buf[i][...] = cast_in[k][r0:r1, :].astype(BF16)
            local.append(pltpu.make_async_copy(cast_buf[i], cast_dst[i], cast_sem.at[i]))
            local[-1].start()

        def landed(j, k, which, sems_s, sems_r, device):
            px, py = chips[j]
            piece = outs[k].at[2 * px + py, half(k, which)]
            return pltpu.make_async_remote_copy(
                src_ref=piece, dst_ref=piece, send_sem=sems_s.at[j * n + k], recv_sem=sems_r.at[j * n + k],
                device_id=device, device_id_type=MESH)

        for j in range(len(chips)):
            for k in range(n):
                landed(j, k, c, send_far, recv_far, sibling).wait_recv()
                cp = landed(j, k, c, send_sib, recv_sib, sibling)
                cp.start()
                sends.append(cp)
        for j, (px, py) in enumerate(chips):
            for k in range(n):
                landed(j, k, 1 - c, send_sib, recv_sib, sibling).wait_recv()
            pltpu.make_async_remote_copy(
                src_ref=small_in, dst_ref=small_out.at[2 * px + py], send_sem=send_small.at[j],
                recv_sem=recv_small.at[j], device_id=(px, py, c), device_id_type=MESH).wait_recv()
        for cp in sends:
            cp.wait_send()
        for cp in local:
            cp.wait()

    vmem = pl.BlockSpec(memory_space=pltpu.VMEM)
    hbm = pl.BlockSpec(memory_space=pltpu.HBM)
    cast_shapes = [(r1 - r0, casts[k][0].shape[1]) for k, r0, r1 in cast_out]
    res = pl.pallas_call(
        body, name="allgather_weights",
        in_specs=[vmem] * (n + 1 + n_c), out_specs=[hbm] * (n + 1 + n_co),
        out_shape=[jax.ShapeDtypeStruct((N_CHIPS,) + s.shape, BF16) for s in shards]
        + [jax.ShapeDtypeStruct((N_CHIPS,) + small.shape, F32)]
        + [jax.ShapeDtypeStruct(s, BF16) for s in cast_shapes],
        scratch_shapes=[pltpu.VMEM(s.shape, BF16) for s in shards] + [pltpu.VMEM(s, BF16) for s in cast_shapes]
        + [pltpu.SemaphoreType.DMA((3 * n,)), pltpu.SemaphoreType.DMA((3 * n,)),
           pltpu.SemaphoreType.DMA((3 * n,)), pltpu.SemaphoreType.DMA((3 * n,)),
           pltpu.SemaphoreType.DMA((3,)), pltpu.SemaphoreType.DMA((3,)),
           pltpu.SemaphoreType.DMA((n + 1,)), pltpu.SemaphoreType.DMA((n_co,))],
        compiler_params=_params(None, 40),
    )(*shards, small, *[a for a, _ in casts])
    return res[:n], res[n], res[n + 1:]


def _adamw(w, g, m, v):
    m = ADAM_B1 * m + (1.0 - ADAM_B1) * g
    v = ADAM_B2 * v + (1.0 - ADAM_B2) * (g * g)
    m_hat = m / (1.0 - ADAM_B1 ** ADAM_STEP)
    v_hat = v / (1.0 - ADAM_B2 ** ADAM_STEP)
    delta = -ADAM_LR * (m_hat / (jnp.sqrt(v_hat) + ADAM_EPS) + ADAM_WD * w)
    return delta, m, v


RS_PIECE_ROWS = 128
RS_PIECE_COLS = 512
RS_PIECE_ROWS_ALONE = 256


def _reduce_adam_all(grads, ws, ms, vs, small, bases=None, seeds=None):
    n_w = len(grads)
    pieces = _reduce_pieces(grads, bases, RS_PIECE_ROWS_ALONE)
    n_small = len(_small_sum_scratch(small.shape))
    seeds = seeds or [None] * n_w
    seeded = [(k, a) for k, seed in enumerate(seeds) if seed is not None for a in range(4)]
    n_in = 4 * n_w + 1

    def body(*refs):
        refs = list(refs)
        del refs[n_in:n_in + len(seeded)]
        small_in = refs.pop(4 * n_w)
        small_out = refs.pop(8 * n_w)
        small_scratch = [refs.pop() for _ in range(n_small)][::-1]
        sends = _small_sum_start(small_in, *small_scratch)
        ticks, drain = _reduce_ticks(pieces, n_w, refs)
        for tick in ticks:
            tick()
        drain()
        _small_sum_finish(sends, small_scratch[0], small_out)

    hbm = pl.BlockSpec(memory_space=pltpu.HBM)
    vmem = pl.BlockSpec(memory_space=pltpu.VMEM)
    outs = pl.pallas_call(
        body, name="reduce_adam_all",
        in_specs=[hbm] * (4 * n_w) + [vmem] + [hbm] * len(seeded), out_specs=[hbm] * (4 * n_w) + [vmem],
        out_shape=[jax.ShapeDtypeStruct(w.shape, F32) for _ in range(4) for w in ws]
        + [jax.ShapeDtypeStruct(small.shape, F32)],
        input_output_aliases={n_in + i: a * n_w + k for i, (k, a) in enumerate(seeded)},
        scratch_shapes=_reduce_scratch(RS_PIECE_ROWS_ALONE) + _small_sum_scratch(small.shape),
        compiler_params=_params(None, 56),
    )(*grads, *ws, *ms, *vs, small, *[seeds[k][a] for k, a in seeded])
    return [outs[i * n_w:(i + 1) * n_w] for i in range(4)], outs[4 * n_w]


def _reduce_pieces(grads, bases=None, piece_rows=RS_PIECE_ROWS):
    pieces = []
    for k, g in enumerate(grads):
        hr, cols = g.shape[1] // 2, g.shape[2]
        pr, pc = min(hr, piece_rows), min(cols, RS_PIECE_COLS)
        base = bases[k] if bases else 0
        pieces += [(k, ro, hr, co, pr, pc, base) for ro in range(0, hr, pr) for co in range(0, cols, pc)]
    return pieces


def _reduce_scratch(piece_rows=RS_PIECE_ROWS):
    P, C = piece_rows, RS_PIECE_COLS
    return [
        pltpu.VMEM((3, N_CHIPS, P, C), F32), pltpu.VMEM((3, N_CHIPS, P, C), F32),
        pltpu.VMEM((2, N_CHIPS, P, C), BF16), pltpu.VMEM((2, N_CHIPS, P, C), BF16),
        pltpu.VMEM((2, N_CHIPS, P, C), F32),
        pltpu.VMEM((2, 3, P, C), BF16), pltpu.VMEM((2, 3, P, C), BF16),
        pltpu.VMEM((2, 2, P, C), F32),
        pltpu.VMEM((2, 3, 2, P, C), F32), pltpu.VMEM((2, 4, 2, P, C), F32),
        pltpu.SemaphoreType.DMA((3, 2)), pltpu.SemaphoreType.DMA((2, 3, 2)),
        pltpu.SemaphoreType.DMA((2,)), pltpu.SemaphoreType.DMA((2,)),
        pltpu.SemaphoreType.DMA((2, 3)), pltpu.SemaphoreType.DMA((2, 3)),
        pltpu.SemaphoreType.DMA((2,)), pltpu.SemaphoreType.DMA((2,)),
        pltpu.SemaphoreType.DMA((2, 4, 2))]


def _reduce_ticks(pieces, n_w, refs):
    n = len(pieces)

    def build(*refs):
        g_in, w_in, m_in, v_in = (refs[i * n_w:(i + 1) * n_w] for i in range(4))
        g_out, d_out, m_out, v_out = (refs[(4 + i) * n_w:(5 + i) * n_w] for i in range(4))
        (gm, go, sb1, rb1, part, sb2, rb2, fin, wmv, outs,
         ld_sem, wmv_sem, s1_send, s1_recv, s2_send, s2_recv, s3_send, s3_recv, out_sem) = refs[8 * n_w:]
        x, y, c, chips = _mesh_place()
        me = 2 * x + y
        sibling = (x, y, 1 - c)

        def at_hbm(i, which, in_shard):
            _, ro, hr, co, pr, pc, base = pieces[i]
            half = c if which == 0 else 1 - c
            return pl.ds(pl.multiple_of((base if in_shard else 0) + half * hr + ro, 64), pr), pl.ds(co, pc)

        def win(i):
            return pl.ds(0, pieces[i][4]), pl.ds(0, pieces[i][5])

        every = slice(None)

        def loads(i):
            k, s = pieces[i][0], i % 3
            return [pltpu.make_async_copy(g_in[k].at[(every,) + at_hbm(i, h, False)], buf.at[(s, every) + win(i)],
                                          ld_sem.at[s, h])
                    for h, buf in enumerate((gm, go))]

        def wmv_loads(i):
            k, s = pieces[i][0], i % 2
            return [pltpu.make_async_copy(src[k].at[at_hbm(i, h, True)], wmv.at[(s, a, h) + win(i)], wmv_sem.at[s, a, h])
                    for a, src in enumerate((w_in, m_in, v_in)) for h in range(2)]

        def stores(i):
            k, s = pieces[i][0], i % 2
            return [pltpu.make_async_copy(outs.at[(s, a, h) + win(i)], dst[k].at[at_hbm(i, h, True)], out_sem.at[s, a, h])
                    for a, dst in enumerate((g_out, d_out, m_out, v_out)) for h in range(2)]

        def swap1(i):
            s = i % 2
            return pltpu.make_async_remote_copy(
                src_ref=sb1.at[(s, every) + win(i)], dst_ref=rb1.at[(s, every) + win(i)],
                send_sem=s1_send.at[s], recv_sem=s1_recv.at[s], device_id=sibling, device_id_type=MESH)

        def far2(i, j):
            s = i % 2
            px, py = chips[j]
            return pltpu.make_async_remote_copy(
                src_ref=sb2.at[(s, j) + win(i)], dst_ref=rb2.at[(s, j) + win(i)],
                send_sem=s2_send.at[s, j], recv_sem=s2_recv.at[s, j], device_id=(px, py, c), device_id_type=MESH)

        def swap3(i):
            s = i % 2
            return pltpu.make_async_remote_copy(
                src_ref=fin.at[(s, 0) + win(i)], dst_ref=fin.at[(s, 1) + win(i)],
                send_sem=s3_send.at[s], recv_sem=s3_recv.at[s], device_id=sibling, device_id_type=MESH)

        def stage0(i):
            for cp in loads(i):
                cp.start()

        def stage1(i):
            s, s3 = i % 2, i % 3
            for cp in loads(i):
                cp.wait()
            sb1[(s, every) + win(i)] = go[(s3, every) + win(i)].astype(BF16)
            swap1(i).start()

        def stage2(i):
            s, s3 = i % 2, i % 3
            swap1(i).wait()
            part[(s, every) + win(i)] = gm[(s3, every) + win(i)] + rb1[(s, every) + win(i)].astype(F32)
            for j, (px, py) in enumerate(chips):
                sb2[(s, j) + win(i)] = part[(s, 2 * px + py) + win(i)].astype(BF16)
                far2(i, j).start()

        def stage3(i):
            s = i % 2
            total = part[(s, me) + win(i)]
            for j in range(3):
                far2(i, j).wait()
                total = total + rb2[(s, j) + win(i)].astype(F32)
            fin[(s, 0) + win(i)] = total
            swap3(i).start()
            for cp in wmv_loads(i):
                cp.start()

        def stage4(i):
            s = i % 2
            if i >= 2:
                for cp in stores(i - 2):
                    cp.wait()
            swap3(i).wait()
            for cp in wmv_loads(i):
                cp.wait()
            both = (every,) + win(i)
            g = fin[(s,) + both]
            delta, m_new, v_new = _adamw(wmv[(s, 0) + both], g, wmv[(s, 1) + both], wmv[(s, 2) + both])
            outs[(s, 0) + both] = g
            outs[(s, 1) + both] = delta
            outs[(s, 2) + both] = m_new
            outs[(s, 3) + both] = v_new
            for cp in stores(i):
                cp.start()

        stages = (stage0, stage1, stage2, stage3, stage4)

        def tick(t):
            for age in reversed(range(len(stages))):
                if 0 <= t - age < n:
                    stages[age](t - age)

        def drain():
            for i in range(max(0, n - 2), n):
                for cp in stores(i):
                    cp.wait()

        return [functools.partial(tick, t) for t in range(n + len(stages) - 1)], drain

    return build(*refs)


N_DEVICES = 8


def _small_sum_scratch(shape):
    return [pltpu.VMEM((N_DEVICES,) + shape, F32),
            pltpu.SemaphoreType.DMA((N_DEVICES - 1,)), pltpu.SemaphoreType.DMA((N_DEVICES - 1,))]


def _small_sum_start(part_ref, buf, send_sem, recv_sem):
    x, y, c, _ = _mesh_place()
    me = 4 * x + 2 * y + c
    buf[me] = part_ref[...]
    sends = []
    for k in range(1, N_DEVICES):
        peer = ((1 - x) if k & 4 else x, (1 - y) if k & 2 else y, (1 - c) if k & 1 else c)
        cp = pltpu.make_async_remote_copy(src_ref=part_ref, dst_ref=buf.at[me], send_sem=send_sem.at[k - 1],
                                          recv_sem=recv_sem.at[k - 1], device_id=peer, device_id_type=MESH)
        cp.start()
        sends.append(cp)
    return sends


def _small_sum_finish(sends, buf, out_ref):
    for cp in sends:
        cp.wait_recv()
    total = buf[0]
    for s in range(1, N_DEVICES):
        total = total + buf[s]
    out_ref[...] = total
    for cp in sends:
        cp.wait_send()


def _adam_small(w, g, m, v):
    def body(w_ref, g_ref, m_ref, v_ref, d_ref, mo_ref, vo_ref):
        delta, m_new, v_new = _adamw(w_ref[...], g_ref[...], m_ref[...], v_ref[...])
        d_ref[...] = delta
        mo_ref[...] = m_new
        vo_ref[...] = v_new

    vmem = pl.BlockSpec(memory_space=pltpu.VMEM)
    return pl.pallas_call(
        body, name="adam_small", in_specs=[vmem] * 4, out_specs=[vmem] * 3,
        out_shape=[jax.ShapeDtypeStruct(w.shape, F32)] * 3,
    )(w, g, m, v)


BIG = ("a_w_in", "a_w_group", "a_w_out", "w_kv", "b_w_in", "b_w_out", "ple_w", "ple_gate_w")
SMALL = ("a_norm", "a_scale", "kv_norm", "b_norm", "k_norm", "b_q_norm")
SMALL_SHARDED = ("a_norm", "a_scale")
WEIGHTS = ("a_norm", "a_w_in", "a_w_group", "a_scale", "a_w_out", "kv_norm", "w_kv", "k_norm", "b_norm", "b_w_in",
           "b_q_norm", "b_w_out", "ple_w", "ple_gate_w")


def _as_matrix(a):
    return a.reshape(-1, a.shape[-1])


def _pack_small(arrs):
    rows = [jnp.pad(a.reshape(-1), (0, D_MODEL - a.size)) for a in arrs]
    rows += [jnp.zeros((D_MODEL,), F32)] * (8 - len(rows))
    return jnp.stack(rows)


def kernel(x, p, a_norm, a_w_in, a_w_group, a_scale, a_w_out, kv_norm, w_kv, k_norm, b_norm, b_w_in, b_q_norm, b_w_out, ple_w, ple_gate_w, loss_target, m_a_norm, m_a_w_in, m_a_w_group, m_a_scale, m_a_w_out, m_kv_norm, m_w_kv, m_k_norm, m_b_norm, m_b_w_in, m_b_q_norm, m_b_w_out, m_ple_w, m_ple_gate_w, v_a_norm, v_a_w_in, v_a_w_group, v_a_scale, v_a_w_out, v_kv_norm, v_w_kv, v_k_norm, v_b_norm, v_b_w_in, v_b_q_norm, v_b_w_out, v_ple_w, v_ple_gate_w):
    wts = dict(a_norm=a_norm, a_w_in=a_w_in, a_w_group=a_w_group, a_scale=a_scale, a_w_out=a_w_out, kv_norm=kv_norm,
               w_kv=w_kv, k_norm=k_norm, b_norm=b_norm, b_w_in=b_w_in, b_q_norm=b_q_norm, b_w_out=b_w_out,
               ple_w=ple_w, ple_gate_w=ple_gate_w)
    mom = dict(a_norm=m_a_norm, a_w_in=m_a_w_in, a_w_group=m_a_w_group, a_scale=m_a_scale, a_w_out=m_a_w_out,
               kv_norm=m_kv_norm, w_kv=m_w_kv, k_norm=m_k_norm, b_norm=m_b_norm, b_w_in=m_b_w_in,
               b_q_norm=m_b_q_norm, b_w_out=m_b_w_out, ple_w=m_ple_w, ple_gate_w=m_ple_gate_w)
    var = dict(a_norm=v_a_norm, a_w_in=v_a_w_in, a_w_group=v_a_w_group, a_scale=v_a_scale, a_w_out=v_a_w_out,
               kv_norm=v_kv_norm, w_kv=v_w_kv, k_norm=v_k_norm, b_norm=v_b_norm, b_w_in=v_b_w_in,
               b_q_norm=v_b_q_norm, b_w_out=v_b_w_out, ple_w=v_ple_w, ple_gate_w=v_ple_gate_w)
    S = x.shape[1]
    chip = 2 * lax.axis_index("x") + lax.axis_index("y")

    sharded_small = jnp.concatenate([a_norm.reshape(1, 256), a_scale.reshape(1, 256), jnp.zeros((6, 256), F32)], axis=0)
    later = ("a_w_group", "a_w_out", "w_kv", "b_w_in", "b_w_out", "ple_w", "ple_gate_w")
    (a_w_in_full,), small_full, copies = _allgather_weights(
        [_as_matrix(a_w_in)], sharded_small,
        [(_as_matrix(wts[n]), [(0, 256), (256, 512)] if n.startswith("ple") else [(0, _as_matrix(wts[n]).shape[0])])
         for n in later])
    local = dict(zip(("a_w_group", "a_w_out", "w_kv", "b_w_in", "b_w_out", "ple_w0", "ple_w1", "ple_gate_w0",
                      "ple_gate_w1"), copies))
    full = dict(a_w_in=a_w_in_full,
                a_norm=small_full[:, 0, :].reshape(1, D_MODEL), a_scale=small_full[:, 1, :].reshape(1, D_MODEL),
                kv_norm=kv_norm.reshape(1, D_MODEL), b_norm=b_norm.reshape(1, D_MODEL), k_norm=k_norm, b_q_norm=b_q_norm)

    def shards(t):
        out = {}
        for n in BIG:
            for entry in ((n + "0", n + "1") if n.startswith("ple") else (n,)):
                out[entry] = _as_matrix(t[n])
        return out

    base = {n: 256 if n.startswith("ple") and n.endswith("1") else 0 for n in shards(wts)}
    state = (shards(wts), shards(mom), shards(var), base)
    grad_x, grads, updates, small_part = _local_step(x.reshape(S, D_MODEL), p, loss_target.reshape(S, D_MODEL),
                                                     full, local, state)

    names = sorted(grads)
    reduced, small_sum = _reduce_adam_all(
        [grads[n] for n in names], *[[t[n] for n in names] for t in state[:3]], small_part,
        bases=[base[n] for n in names], seeds=[updates.get(n[:-1] + "1") if n.startswith("ple") else None for n in names])
    for i, n in enumerate(names):
        updates[n] = tuple(group[i] for group in reduced)
    out_g, out_d, out_m, out_v = {}, {}, {}, {}
    for n in BIG:
        for i, out in enumerate((out_g, out_d, out_m, out_v)):
            out[n] = updates[n + "0" if n.startswith("ple") else n][i].reshape(wts[n].shape)

    loss = small_sum[len(SMALL), 0]
    small_rows = []
    for i, n in enumerate(SMALL):
        row = small_sum[i]
        if n in SMALL_SHARDED:
            row = lax.dynamic_slice(row, (chip * 256,), (256,))
        else:
            row = row[:wts[n].size]
        small_rows.append(row)
    g_small = _pack_small(small_rows)
    d_small, m_small, v_small = _adam_small(_pack_small([wts[n] for n in SMALL]), g_small,
                                            _pack_small([mom[n] for n in SMALL]), _pack_small([var[n] for n in SMALL]))
    for i, n in enumerate(SMALL):
        shape, size = wts[n].shape, wts[n].size
        out_g[n], out_d[n], out_m[n], out_v[n] = (t[i, :size].reshape(shape) for t in (g_small, d_small, m_small, v_small))

    return (loss, grad_x.reshape(1, S, D_MODEL), *[out_g[n] for n in WEIGHTS], *[out_d[n] for n in WEIGHTS],
            *[out_m[n] for n in WEIGHTS], *[out_v[n] for n in WEIGHTS])
```

```python
import functools

import jax
import jax.numpy as jnp
from jax import lax
from jax.experimental import pallas as pl
from jax.experimental.pallas import tpu as pltpu

F32 = jnp.float32
BF16 = jnp.bfloat16
MESH = pl.DeviceIdType.MESH

D_MODEL = 1024
N_HEADS = 16
HEAD_DIM = 64
PLE_DIM = 256
N_GROUPS = 4
GROUP_DIM = 256
POOL_WINDOWS = (2, 4, 8, 16)
N_CHIPS = 4
EPS = 1e-6
SB_SCALE = HEAD_DIM ** -0.5

ADAM_LR = 0.001
ADAM_B1 = 0.9
ADAM_B2 = 0.999
ADAM_EPS = 1e-08
ADAM_WD = 0.01
ADAM_STEP = 10

ROW_TILE = 256
WIDE_ROW_TILE = 512
EXP_UNDERFLOW = -104.0
ATT_Q_TILE = 512
ATT_K_TILE = 256
WGRAD_SEQ_TILE = 1024
WGRAD_ACC_BYTES = 4 * 1024 * 1024
MIB = 1024 * 1024


def _params(semantics=None, vmem_mib=48):
    return pltpu.CompilerParams(dimension_semantics=semantics, vmem_limit_bytes=vmem_mib * MIB)


def _dot(a, b):
    return jnp.dot(a, b, preferred_element_type=F32)


def _dot_nt(a, b):
    return lax.dot_general(a, b, (((1,), (1,)), ((), ())), preferred_element_type=F32)


def _dot_tn(a, b):
    return lax.dot_general(a, b, (((0,), (0,)), ((), ())), preferred_element_type=F32)


def _hilo(x):
    hi = x.astype(BF16)
    lo = (x - hi.astype(F32)).astype(BF16)
    return hi, lo


def _dot_hilo(x, w):
    hi, lo = _hilo(x)
    return _dot(hi, w) + _dot(lo, w)


def _sigmoid(z):
    return jax.nn.sigmoid(z)


def _dsilu(z, sg):
    return sg * (1.0 + z * (1.0 - sg))


def _mask_bf16(cond):
    return jnp.where(cond, 1.0, 0.0).astype(BF16)


def _head_mean_matrix():
    r = lax.broadcasted_iota(jnp.int32, (256, 256), 0) // HEAD_DIM
    c = lax.broadcasted_iota(jnp.int32, (256, 256), 1) // HEAD_DIM
    return _mask_bf16(r == c)


def _head_mean(x, bd):
    parts = []
    for s in range(x.shape[1] // 256):
        parts.append(_dot_hilo(x[:, s * 256:(s + 1) * 256], bd))
    out = parts[0] if len(parts) == 1 else jnp.concatenate(parts, axis=1)
    return out * (1.0 / HEAD_DIM)


def _a_in(x, gain, w_sh, gather=()):
    S = x.shape[0]
    tm = 512
    nsh, _, wn = w_sh.shape

    def body(x_ref, g_ref, w_ref, uz_ref, h_ref):
        @pl.when(pl.program_id(1) == 0)
        def _():
            xv = x_ref[...]
            r = lax.rsqrt(jnp.mean(xv * xv, axis=-1, keepdims=True) + EPS)
            h_ref[...] = (xv * r * g_ref[...]).astype(BF16)

        uz_ref[...] = _dot(h_ref[...], w_ref[0])

    return _call_with_gather(
        body, name="a_in", grid=(S // tm, nsh),
        in_specs=[pl.BlockSpec((tm, D_MODEL), lambda i, j: (i, 0)),
                  pl.BlockSpec((1, D_MODEL), lambda i, j: (0, 0)),
                  pl.BlockSpec((1, D_MODEL, wn), lambda i, j: (j, 0, 0))],
        out_specs=[pl.BlockSpec((tm, wn), lambda i, j: (i, j)),
                   pl.BlockSpec((tm, D_MODEL), lambda i, j: (i, 0))],
        out_shape=[jax.ShapeDtypeStruct((S, nsh * wn), F32),
                   jax.ShapeDtypeStruct((S, D_MODEL), BF16)],
        args=(x, gain, w_sh), gather=gather)


def _inv_count(first_row, rows, w):
    t1 = first_row + 1 + lax.broadcasted_iota(jnp.int32, (rows, 1), 0)
    return 1.0 / jnp.minimum(t1, w).astype(F32)


def _group_weight(wg_ref, g):
    return jnp.concatenate([wg_ref[sh, g] for sh in range(N_CHIPS)], axis=0)


def _a_mix(uz, wg, scale, gather=()):
    S = uz.shape[0]
    tm = ROW_TILE

    def body(u_ref, up_ref, z_ref, wg_ref, sc_ref, ga_ref, p_ref):
        i = pl.program_id(0)
        row = lax.broadcasted_iota(jnp.int32, (tm, tm), 0)
        col = lax.broadcasted_iota(jnp.int32, (tm, tm), 1)
        d = row - col
        for g, w in enumerate(POOL_WINDOWS):
            cols = slice(g * GROUP_DIM, (g + 1) * GROUP_DIM)
            t_main = _mask_bf16((d >= 0) & (d < w))
            t_halo = _mask_bf16(d + tm < w)
            u = u_ref[:, cols]
            up = jnp.where(i > 0, up_ref[:, cols], 0.0)
            hi, lo = _hilo(u)
            hip, lop = _hilo(up)
            wsum = _dot(t_main, hi) + _dot(t_main, lo) + _dot(t_halo, hip) + _dot(t_halo, lop)
            pooled = (wsum * _inv_count(i * tm, tm, w) - u).astype(BF16)
            p_ref[:, cols] = pooled
            mraw = _dot(pooled, _group_weight(wg_ref, g))
            z = z_ref[:, cols]
            ga_ref[:, cols] = (mraw * sc_ref[:, cols] * (z * _sigmoid(z))).astype(BF16)

    return _call_with_gather(
        body, name="a_mix", grid=(S // tm,),
        in_specs=[pl.BlockSpec((tm, D_MODEL), lambda i: (i, 0)),
                  pl.BlockSpec((tm, D_MODEL), lambda i: (jnp.maximum(i - 1, 0), 0)),
                  pl.BlockSpec((tm, D_MODEL), lambda i: (i, 1)),
                  pl.BlockSpec((N_CHIPS, N_GROUPS, 64, GROUP_DIM), lambda i: (0, 0, 0, 0)),
                  pl.BlockSpec((1, D_MODEL), lambda i: (0, 0))],
        out_specs=[pl.BlockSpec((tm, D_MODEL), lambda i: (i, 0)),
                   pl.BlockSpec((tm, D_MODEL), lambda i: (i, 0))],
        out_shape=[jax.ShapeDtypeStruct((S, D_MODEL), BF16),
                   jax.ShapeDtypeStruct((S, D_MODEL), BF16)],
        args=(uz, uz, uz, wg, scale), gather=gather)


def _out_ple(name, gated, x_in, w_out, p, layer, ple_w, ple_g, target=None, gather=()):
    S = x_in.shape[0]
    tm = WIDE_ROW_TILE
    with_loss = target is not None

    def body(*refs):
        if with_loss:
            g_ref, x_ref, wo_ref, p_ref, pw_ref, pg_ref, t_ref, xm_ref, dx_ref, e_ref, gt_ref, loss_ref = refs
        else:
            g_ref, x_ref, wo_ref, p_ref, pw_ref, pg_ref, xm_ref, xo_ref, e_ref, gt_ref = refs
        xm = x_ref[...] + _dot(g_ref[...], wo_ref[...])
        xm_ref[...] = xm
        pb = p_ref[...].astype(BF16)
        e = jnp.concatenate([_dot(pb, pw_ref[sh]) for sh in range(N_CHIPS)], axis=1)
        pg = jnp.concatenate([pg_ref[sh] for sh in range(N_CHIPS)], axis=0)
        gate = _sigmoid(_dot(xm.astype(BF16), pg))
        e_ref[...] = e.astype(BF16)
        gt_ref[...] = gate.astype(BF16)
        xo = xm + e * gate
        if with_loss:
            diff = xo - t_ref[...]
            dx_ref[...] = diff * (1.0 / D_MODEL)

            @pl.when(pl.program_id(0) == 0)
            def _():
                loss_ref[...] = jnp.zeros_like(loss_ref)

            loss_ref[...] += jnp.sum(diff * diff) * (0.5 / D_MODEL)
        else:
            xo_ref[...] = xo

    row = pl.BlockSpec((tm, D_MODEL), lambda i: (i, 0))
    in_specs = [row, row,
                pl.BlockSpec((D_MODEL, D_MODEL), lambda i: (0, 0)),
                pl.BlockSpec((None, None, tm, PLE_DIM), lambda i: (layer, 0, i, 0)),
                pl.BlockSpec((N_CHIPS, PLE_DIM, 256), lambda i: (0, 0, 0)),
                pl.BlockSpec((N_CHIPS, 256, D_MODEL), lambda i: (0, 0, 0))]
    args = [gated, x_in, w_out, p, ple_w, ple_g]
    out_specs = [row, row, row, row]
    out_shape = [jax.ShapeDtypeStruct((S, D_MODEL), F32), jax.ShapeDtypeStruct((S, D_MODEL), F32),
                 jax.ShapeDtypeStruct((S, D_MODEL), BF16), jax.ShapeDtypeStruct((S, D_MODEL), BF16)]
    if with_loss:
        in_specs.append(row)
        args.append(target)
        out_specs.append(pl.BlockSpec((8, 128), lambda i: (0, 0)))
        out_shape.append(jax.ShapeDtypeStruct((8, 128), F32))
    return _call_with_gather(body, name=name, grid=(S // tm,), in_specs=in_specs, out_specs=out_specs,
                             out_shape=out_shape, args=args, gather=gather)


def _b_in(x, kv_gain, b_gain, k_gain_t, q_gain_t, w_kv, w_in, gather=()):
    S = x.shape[0]
    tm = ROW_TILE

    def body(x_ref, kvg_ref, bg_ref, kg_ref, qg_ref, wkv_ref, win_ref,
             hkv_ref, hb_ref, kraw_ref, qraw_ref, k_ref, q_ref, v_ref, z_ref):
        xv = x_ref[...]
        y = xv * lax.rsqrt(jnp.mean(xv * xv, axis=-1, keepdims=True) + EPS)
        hkv = (y * kvg_ref[...]).astype(BF16)
        hb = (y * bg_ref[...]).astype(BF16)
        hkv_ref[...] = hkv
        hb_ref[...] = hb
        bd = _head_mean_matrix()

        def head_norm(raw, gain):
            rr = lax.rsqrt(_head_mean(raw * raw, bd) + EPS)
            return raw * rr * gain

        for sh in range(N_CHIPS):
            kvc = _dot(hkv, wkv_ref[sh])
            qzc = _dot(hb, win_ref[sh])
            cols = slice((sh % 2) * 512, (sh % 2) * 512 + 512)
            if sh < 2:
                kraw_ref[:, cols] = kvc.astype(BF16)
                qraw_ref[:, cols] = qzc.astype(BF16)
                k_ref[:, cols] = head_norm(kvc, kg_ref[:, cols]).astype(BF16)
                q_ref[:, cols] = (head_norm(qzc, qg_ref[:, cols]) * SB_SCALE).astype(BF16)
            else:
                v_ref[:, cols] = kvc.astype(BF16)
                z_ref[:, cols] = qzc.astype(BF16)

    row = pl.BlockSpec((tm, D_MODEL), lambda i: (i, 0))
    vec = pl.BlockSpec((1, D_MODEL), lambda i: (0, 0))
    wsp = pl.BlockSpec((N_CHIPS, D_MODEL, 512), lambda i: (0, 0, 0))
    return _call_with_gather(
        body, name="b_in", grid=(S // tm,),
        in_specs=[row, vec, vec, vec, vec, wsp, wsp],
        out_specs=[row] * 8,
        out_shape=[jax.ShapeDtypeStruct((S, D_MODEL), BF16)] * 8,
        args=(x, kv_gain, b_gain, k_gain_t, q_gain_t, w_kv, w_in), gather=gather, vmem_mib=56)


def _softplus_parts(z):
    e = jnp.exp(-jnp.abs(z))
    return -(jnp.maximum(z, 0.0) + jnp.log(1.0 + e)), e


def _add_rows(total, rows, update):
    lo, hi = rows
    parts = ([total[:lo]] if lo else []) + [total[lo:hi] + update] + ([total[hi:]] if hi < total.shape[0] else [])
    return parts[0] if len(parts) == 1 else jnp.concatenate(parts, axis=0)


def _attn_fwd(q, k, v, zgate, gather=()):
    S = q.shape[0]
    tq, tk = ATT_Q_TILE, ATT_K_TILE
    kpq = tq // tk
    assert kpq == 2

    def body(q_ref, k_ref, v_ref, z_ref, o_ref, g_ref, lt_ref, steps_ref):
        qi = pl.program_id(1)
        lane = lax.broadcasted_iota(jnp.int32, (1, 128), 1)
        ri = lax.broadcasted_iota(jnp.int32, (tk, tk), 0)
        ci = lax.broadcasted_iota(jnp.int32, (tk, tk), 1)
        later_mat = _mask_bf16(ri > ci)
        causal = ci < ri
        qv = q_ref[...]
        first = lane < HEAD_DIM
        q_heads = (jnp.where(first, qv, jnp.zeros_like(qv)), jnp.where(first, jnp.zeros_like(qv), qv))

        def step(blocks, carry):
            chains = [(b, h) for b in range(len(blocks)) for h in range(2)]
            rows = [r for _, r, _ in blocks]
            s0 = [pl.multiple_of(kj * tk, tk) for kj, _, _ in blocks]
            kb = [k_ref[pl.ds(s, tk), :] for s in s0]
            vb = [v_ref[pl.ds(s, tk), :] for s in s0]
            visible = [causal if masked else None for _, _, masked in blocks]
            z = {c: _dot_nt(q_heads[c[1]][rows[c[0]][0]:rows[c[0]][1]], kb[c[0]]) for c in chains}
            run = [carry[0], carry[2]]
            log_own, later, run_at = {}, {}, {}
            for c in chains:
                b, h = c
                lk = _softplus_parts(z[c])[0]
                if visible[b] is not None:
                    lk = jnp.where(visible[b], lk, 0.0)
                log_own[c] = z[c] + lk
                later[c] = _dot(lk.astype(BF16), later_mat)
                run_at[c] = run[h][rows[b][0]:rows[b][1]]
                run[h] = _add_rows(run[h], rows[b], jnp.sum(lk, axis=-1, keepdims=True))
            acc = [carry[1], carry[3]]
            for c in chains:
                b, h = c
                a = jnp.exp(log_own[c] + later[c] + run_at[c])
                if visible[b] is not None:
                    a = jnp.where(visible[b], a, 0.0)
                acc[h] = _add_rows(acc[h], rows[b], _dot(a.astype(BF16), vb[b]))
            return run[0], acc[0], run[1], acc[1]

        zero1, zero128 = jnp.zeros((tq, 1), F32), jnp.zeros((tq, 128), F32)
        carry = step([(qi * kpq + 1, (tk, tq), True), (qi * kpq, (tk, tq), False), (qi * kpq, (0, tk), True)],
                     (zero1, zero128, zero1, zero128))

        def low(run):
            return jnp.max(run)

        def pair_more(c):
            return (c[0] < qi) & (jnp.maximum(low(c[1][tk:]), low(c[3][tk:])) > EXP_UNDERFLOW)

        def pair_step(c):
            last = (qi - c[0]) * kpq - 1
            return (c[0] + 1, *step([(last, (0, tq), False), (last - 1, (0, tq), False)], c[1:]))

        pairs, *carry = lax.while_loop(pair_more, pair_step, (jnp.int32(0), *carry))
        left = (qi - pairs) * kpq

        def single_more(c):
            return (c[0] < left) & (jnp.maximum(low(c[1][:tk]), low(c[3][:tk])) > EXP_UNDERFLOW)

        def single_step(c):
            return (c[0] + 1, *step([(left - 1 - c[0], (0, tk), False)], c[1:]))

        singles, *carry = lax.while_loop(single_more, single_step, (jnp.int32(0), *carry))
        steps_ref[...] = jnp.concatenate([jnp.full((4, 128), pairs, F32), jnp.full((4, 128), singles, F32)], axis=0)
        o_tot = jnp.where(first, carry[1], carry[3])
        l_tot = jnp.where(first, carry[0], carry[2])
        o_ref[...] = o_tot.astype(BF16)
        lt_ref[...] = l_tot
        zz = z_ref[...].astype(F32)
        g_ref[...] = (o_tot * (zz * _sigmoid(zz))).astype(BF16)

    blk = pl.BlockSpec((tq, 128), lambda hp, qi: (qi, hp))
    seq = pl.BlockSpec((S, 128), lambda hp, qi: (0, hp))
    return _call_with_gather(
        body, name="attn_fwd", grid=(D_MODEL // 128, S // tq),
        in_specs=[blk, seq, seq, blk],
        out_specs=[blk, blk, blk, pl.BlockSpec((None, None, 8, 128), lambda hp, qi: (hp, qi, 0, 0))],
        out_shape=[jax.ShapeDtypeStruct((S, D_MODEL), BF16)] * 2 + [jax.ShapeDtypeStruct((S, D_MODEL), F32)]
        + [jax.ShapeDtypeStruct((D_MODEL // 128, S // tq, 8, 128), F32)],
        args=(q, k, v, zgate), gather=gather)


def _ple_out_bwd(name, dx_out, e, gate, ple_g, w_out):
    S = dx_out.shape[0]
    tm = WIDE_ROW_TILE

    def body(dx_ref, e_ref, gt_ref, pg_ref, wo_ref, de_ref, dgp_ref, dxm_ref, dg_ref):
        dxo = dx_ref[...]
        ev = e_ref[...].astype(F32)
        gv = gt_ref[...].astype(F32)
        de_ref[...] = (dxo * gv).astype(BF16)
        dgp = (dxo * ev * gv * (1.0 - gv)).astype(BF16)
        dgp_ref[...] = dgp
        pg = jnp.concatenate([pg_ref[sh] for sh in range(N_CHIPS)], axis=0)
        dxm = dxo + _dot_nt(dgp, pg)
        dxm_ref[...] = dxm
        dg_ref[...] = _dot_nt(dxm.astype(BF16), wo_ref[...]).astype(BF16)

    row = pl.BlockSpec((tm, D_MODEL), lambda i: (i, 0))
    return pl.pallas_call(
        body, name=name, grid=(S // tm,),
        in_specs=[row, row, row,
                  pl.BlockSpec((N_CHIPS, 256, D_MODEL), lambda i: (0, 0, 0)),
                  pl.BlockSpec((D_MODEL, D_MODEL), lambda i: (0, 0))],
        out_specs=[row, row, row, row],
        out_shape=[jax.ShapeDtypeStruct((S, D_MODEL), BF16), jax.ShapeDtypeStruct((S, D_MODEL), BF16),
                   jax.ShapeDtypeStruct((S, D_MODEL), F32), jax.ShapeDtypeStruct((S, D_MODEL), BF16)],
        compiler_params=_params(("arbitrary",)),
    )(dx_out, e, gate, ple_g, w_out)


def _attn_bwd(q, k, v, ltot, steps, dgated, o, zgate, reduce=None):
    S = q.shape[0]
    tq, tk = ATT_Q_TILE, ATT_K_TILE
    kpq = tq // tk
    nq = S // tq

    def body(q_ref, k_ref, v_ref, lt_ref, steps_ref, dg_ref, o_ref, z_ref, dq_ref, dk_ref, dv_ref, dz_ref,
             dk_acc, dv_acc):
        qi = pl.program_id(1)

        @pl.when(qi == 0)
        def _():
            dk_acc[...] = jnp.zeros_like(dk_acc)
            dv_acc[...] = jnp.zeros_like(dv_acc)

        lane = lax.broadcasted_iota(jnp.int32, (1, 128), 1)
        ri = lax.broadcasted_iota(jnp.int32, (tk, tk), 0)
        ci = lax.broadcasted_iota(jnp.int32, (tk, tk), 1)
        later_mat = _mask_bf16(ri > ci)
        before_mat = _mask_bf16(ri < ci)
        causal = ci < ri
        zz = z_ref[...].astype(F32)
        sg = _sigmoid(zz)
        dgv = dg_ref[...].astype(F32)
        dz_ref[...] = (dgv * o_ref[...].astype(F32) * _dsilu(zz, sg)).astype(BF16)
        dob = (dgv * (zz * sg)).astype(BF16)
        ltv = lt_ref[...]
        qv = q_ref[...]
        first = lane < HEAD_DIM
        masks = (first, jnp.logical_not(first))
        q_heads = [jnp.where(hm, qv, jnp.zeros_like(qv)) for hm in masks]
        do_heads = [jnp.where(hm, dob, jnp.zeros_like(dob)) for hm in masks]
        totals = [jnp.max(jnp.where(hm, ltv, -jnp.inf), axis=-1, keepdims=True) for hm in masks]

        def step(blocks, carry):
            chains = [(b, h) for b in range(len(blocks)) for h in range(2)]
            rows = [r for _, r, _ in blocks]
            cut = lambda t, b: t[rows[b][0]:rows[b][1]]
            s0 = [pl.multiple_of(kj * tk, tk) for kj, _, _ in blocks]
            kb = [k_ref[pl.ds(s, tk), :] for s in s0]
            vb = [v_ref[pl.ds(s, tk), :] for s in s0]
            visible = [causal if masked else None for _, _, masked in blocks]
            z = {c: _dot_nt(cut(q_heads[c[1]], c[0]), kb[c[0]]) for c in chains}
            da = {c: _dot_nt(cut(do_heads[c[1]], c[0]), vb[c[0]]) for c in chains}
            run = [carry[0], carry[3]]
            log_own, beta, later, base = {}, {}, {}, {}
            for c in chains:
                b, h = c
                lk = _softplus_parts(z[c])[0]
                if visible[b] is not None:
                    lk = jnp.where(visible[b], lk, 0.0)
                log_own[c] = z[c] + lk
                beta[c] = jnp.exp(log_own[c]).astype(BF16)
                later[c] = _dot(lk.astype(BF16), later_mat)
                run[h] = _add_rows(run[h], rows[b], jnp.sum(lk, axis=-1, keepdims=True))
                base[c] = cut(totals[h] - run[h], b)
            grun = [carry[1], carry[4]]
            a_bf, g_bf, gbefore, grun_at = {}, {}, {}, {}
            for c in chains:
                b, h = c
                a = jnp.exp(log_own[c] + later[c] + base[c])
                if visible[b] is not None:
                    a = jnp.where(visible[b], a, 0.0)
                a_bf[c] = a.astype(BF16)
                g = da[c] * a
                g_bf[c] = g.astype(BF16)
                gbefore[c] = _dot(g_bf[c], before_mat)
                grun_at[c] = cut(grun[h], b)
                grun[h] = _add_rows(grun[h], rows[b], jnp.sum(g, axis=-1, keepdims=True))
            dq = [carry[2], carry[5]]
            dk_blk = [jnp.zeros((tk, 128), F32) for _ in blocks]
            dv_blk = [jnp.zeros((tk, 128), F32) for _ in blocks]
            for c in chains:
                b, h = c
                g = g_bf[c].astype(F32)
                dz = g - beta[c].astype(F32) * (g + gbefore[c] + grun_at[c])
                if visible[b] is not None:
                    dz = jnp.where(visible[b], dz, 0.0)
                dzb = dz.astype(BF16)
                dq[h] = _add_rows(dq[h], rows[b], _dot(dzb, kb[b]))
                dk_blk[b] = dk_blk[b] + _dot_tn(dzb, cut(q_heads[h], b))
                dv_blk[b] = dv_blk[b] + _dot_tn(a_bf[c], cut(do_heads[h], b))
            for b in range(len(blocks)):
                dk_acc[pl.ds(s0[b], tk), :] += dk_blk[b]
                dv_acc[pl.ds(s0[b], tk), :] += dv_blk[b]
            return run[0], grun[0], dq[0], run[1], grun[1], dq[1]

        pairs = jnp.clip(jnp.max(steps_ref[0:4, :]).astype(jnp.int32), 0, qi)
        left = (qi - pairs) * kpq
        singles = jnp.clip(jnp.max(steps_ref[4:8, :]).astype(jnp.int32), 0, left)
        zero1, zero128 = jnp.zeros((tq, 1), F32), jnp.zeros((tq, 128), F32)
        carry = lax.fori_loop(left - singles, left, lambda kj, c: step([(kj, (0, tk), False)], c),
                              (zero1, zero1, zero128, zero1, zero1, zero128))
        carry = lax.fori_loop(qi - pairs, qi,
                              lambda n, c: step([(n * kpq, (0, tq), False), (n * kpq + 1, (0, tq), False)], c), carry)
        carry = step([(qi * kpq, (0, tk), True), (qi * kpq, (tk, tq), False), (qi * kpq + 1, (tk, tq), True)], carry)
        dq_ref[...] = jnp.where(first, carry[2], carry[5]).astype(BF16)

        @pl.when(qi == nq - 1)
        def _():
            dk_ref[...] = dk_acc[...].astype(BF16)
            dv_ref[...] = dv_acc[...].astype(BF16)

    blk = pl.BlockSpec((tq, 128), lambda hp, qi: (qi, hp))
    seq = pl.BlockSpec((S, 128), lambda hp, qi: (0, hp))
    return _call_with_gather(
        body, name="attn_bwd", grid=(D_MODEL // 128, nq),
        in_specs=[blk, seq, seq, blk, pl.BlockSpec((None, None, 8, 128), lambda hp, qi: (hp, qi, 0, 0)),
                  blk, blk, blk],
        out_specs=[blk, seq, seq, blk],
        out_shape=[jax.ShapeDtypeStruct((S, D_MODEL), BF16)] * 4,
        scratch_shapes=[pltpu.VMEM((S, 128), F32), pltpu.VMEM((S, 128), F32)],
        args=(q, k, v, ltot, steps, dgated, o, zgate), reduce=reduce, vmem_mib=56)


def _rms_bwd(xv, dh_gain_sum):
    r = lax.rsqrt(jnp.mean(xv * xv, axis=-1, keepdims=True) + EPS)
    xhat = xv * r
    dx = r * (dh_gain_sum - xhat * jnp.mean(dh_gain_sum * xhat, axis=-1, keepdims=True))
    return dx, xhat


def _b_in_bwd(dq, dk, dv, dz, q_raw, k_raw, x, dx_mid, q_gain_t, k_gain_t, b_gain, kv_gain, w_in, w_kv):
    S = x.shape[0]
    tm = ROW_TILE

    def body(dq_ref, dk_ref, dv_ref, dz_ref, qr_ref, kr_ref, x_ref, dxm_ref, qg_ref, kg_ref, bg_ref, kvg_ref,
             win_ref, wkv_ref, dqz_ref, dkv_ref, dx_ref, small_ref):
        @pl.when(pl.program_id(0) == 0)
        def _():
            small_ref[...] = jnp.zeros_like(small_ref)

        bd = _head_mean_matrix()

        def head_norm_bwd(dy_ref, raw_ref, gain, scale):
            raw = raw_ref[...].astype(F32)
            rr = lax.rsqrt(_head_mean(raw * raw, bd) + EPS)
            xhat = raw * rr
            dy = dy_ref[...].astype(F32) * scale
            gdy = dy * gain
            draw = rr * (gdy - xhat * _head_mean(gdy * xhat, bd))
            return draw.astype(BF16), jnp.sum(dy * xhat, axis=0, keepdims=True)

        dqr, dqg = head_norm_bwd(dq_ref, qr_ref, qg_ref[...], SB_SCALE)
        dkr, dkg = head_norm_bwd(dk_ref, kr_ref, kg_ref[...], 1.0)
        dqz_ref[:, :D_MODEL] = dqr
        dqz_ref[:, D_MODEL:] = dz_ref[...]
        dkv_ref[:, :D_MODEL] = dkr
        dkv_ref[:, D_MODEL:] = dv_ref[...]
        dhb = jnp.zeros((tm, D_MODEL), F32)
        dhkv = jnp.zeros((tm, D_MODEL), F32)
        for sh in range(N_CHIPS):
            cols = slice(sh * 512, (sh + 1) * 512)
            dhb = dhb + _dot_nt(dqz_ref[:, cols], win_ref[sh])
            dhkv = dhkv + _dot_nt(dkv_ref[:, cols], wkv_ref[sh])
        dx, xhat = _rms_bwd(x_ref[...], dhb * bg_ref[...] + dhkv * kvg_ref[...])
        dx_ref[...] = dxm_ref[...] + dx
        small_ref[0:1, :] += dqg
        small_ref[1:2, :] += dkg
        small_ref[2:3, :] += jnp.sum(dhb * xhat, axis=0, keepdims=True)
        small_ref[3:4, :] += jnp.sum(dhkv * xhat, axis=0, keepdims=True)

    row = pl.BlockSpec((tm, D_MODEL), lambda i: (i, 0))
    wide = pl.BlockSpec((tm, 2 * D_MODEL), lambda i: (i, 0))
    vec = pl.BlockSpec((1, D_MODEL), lambda i: (0, 0))
    wsp = pl.BlockSpec((N_CHIPS, D_MODEL, 512), lambda i: (0, 0, 0))
    return pl.pallas_call(
        body, name="b_in_bwd", grid=(S // tm,),
        in_specs=[row] * 8 + [vec] * 4 + [wsp, wsp],
        out_specs=[wide, wide, row, pl.BlockSpec((8, D_MODEL), lambda i: (0, 0))],
        out_shape=[jax.ShapeDtypeStruct((S, 2 * D_MODEL), BF16), jax.ShapeDtypeStruct((S, 2 * D_MODEL), BF16),
                   jax.ShapeDtypeStruct((S, D_MODEL), F32), jax.ShapeDtypeStruct((8, D_MODEL), F32)],
        compiler_params=_params(("arbitrary",), 56),
    )(dq, dk, dv, dz, q_raw, k_raw, x, dx_mid, q_gain_t, k_gain_t, b_gain, kv_gain, w_in, w_kv)


def _a_mix_bwd(dgated, uz, pooled, wg, scale, w_in, x, dx_mid, gain, reduce=None):
    S = x.shape[0]
    tm = ROW_TILE
    n = S // tm

    def body(dg_ref, z_ref, p_ref, wg_ref, sc_ref, win_ref, x_ref, dxm_ref, gn_ref,
             duz_ref, dmr_ref, dx_ref, small_ref, halo_hi, halo_lo):
        i = pl.program_id(0)

        @pl.when(i == 0)
        def _():
            small_ref[...] = jnp.zeros_like(small_ref)
            halo_hi[...] = jnp.zeros_like(halo_hi)
            halo_lo[...] = jnp.zeros_like(halo_lo)

        first_row = (n - 1 - i) * tm
        row = lax.broadcasted_iota(jnp.int32, (tm, tm), 0)
        col = lax.broadcasted_iota(jnp.int32, (tm, tm), 1)
        d = col - row
        for g, w in enumerate(POOL_WINDOWS):
            cols = slice(g * GROUP_DIM, (g + 1) * GROUP_DIM)
            wgg = _group_weight(wg_ref, g)
            sc = sc_ref[:, cols]
            mraw = _dot(p_ref[:, cols], wgg)
            z = z_ref[:, cols]
            sg = _sigmoid(z)
            dga = dg_ref[:, cols].astype(F32)
            dm = dga * (z * sg)
            duz_ref[:, D_MODEL + g * GROUP_DIM:D_MODEL + (g + 1) * GROUP_DIM] = (
                dga * (mraw * sc) * _dsilu(z, sg)).astype(BF16)
            small_ref[0:1, cols] += jnp.sum(dm * mraw, axis=0, keepdims=True)
            dmr = (dm * sc).astype(BF16)
            dmr_ref[:, cols] = dmr
            dp = _dot_nt(dmr, wgg)
            hi, lo = _hilo(dp * _inv_count(first_row, tm, w))
            t_main = _mask_bf16((d >= 0) & (d < w))
            t_halo = _mask_bf16(d + tm < w)
            du = (_dot(t_main, hi) + _dot(t_main, lo) + _dot(t_halo, halo_hi[:, cols]) + _dot(t_halo, halo_lo[:, cols])
                  - dp)
            halo_hi[:, cols] = hi
            halo_lo[:, cols] = lo
            duz_ref[:, cols] = du.astype(BF16)
        dh = jnp.zeros((tm, D_MODEL), F32)
        for sh in range(N_CHIPS):
            dh = dh + _dot_nt(duz_ref[:, sh * 512:(sh + 1) * 512], win_ref[sh])
        dx, xhat = _rms_bwd(x_ref[...], dh * gn_ref[...])
        dx_ref[...] = dxm_ref[...] + dx
        small_ref[1:2, :] += jnp.sum(dh * xhat, axis=0, keepdims=True)

    rev = lambda i: (n - 1 - i, 0)
    row = pl.BlockSpec((tm, D_MODEL), rev)
    vec = pl.BlockSpec((1, D_MODEL), lambda i: (0, 0))
    return _call_with_gather(
        body, name="a_mix_bwd", grid=(n,),
        in_specs=[row,
                  pl.BlockSpec((tm, D_MODEL), lambda i: (n - 1 - i, 1)),
                  row,
                  pl.BlockSpec((N_CHIPS, N_GROUPS, 64, GROUP_DIM), lambda i: (0, 0, 0, 0)),
                  vec,
                  pl.BlockSpec((N_CHIPS, D_MODEL, 512), lambda i: (0, 0, 0)),
                  row, row, vec],
        out_specs=[pl.BlockSpec((tm, 2 * D_MODEL), rev), row, row,
                   pl.BlockSpec((8, D_MODEL), lambda i: (0, 0))],
        out_shape=[jax.ShapeDtypeStruct((S, 2 * D_MODEL), BF16), jax.ShapeDtypeStruct((S, D_MODEL), BF16),
                   jax.ShapeDtypeStruct((S, D_MODEL), F32), jax.ShapeDtypeStruct((8, D_MODEL), F32)],
        scratch_shapes=[pltpu.VMEM((tm, D_MODEL), BF16), pltpu.VMEM((tm, D_MODEL), BF16)],
        args=(dgated, uz, pooled, wg, scale, w_in, x, dx_mid, gain), reduce=reduce, vmem_mib=56)


def _wgrad(name, a, dy, n_shards, a_spec=None, k_dim=None):
    S, n_cols = dy.shape
    ts = WGRAD_SEQ_TILE
    k_dim = a.shape[-1] if k_dim is None else k_dim
    wn = n_cols // n_shards
    tk = min(k_dim, WGRAD_ACC_BYTES // (4 * n_cols))
    nst = S // ts

    def body(a_ref, dy_ref, out_ref, acc):
        st = pl.program_id(1)

        @pl.when(st == 0)
        def _():
            acc[...] = jnp.zeros_like(acc)

        acc[...] += _dot_tn(a_ref[...].astype(BF16), dy_ref[...].astype(BF16))

        @pl.when(st == nst - 1)
        def _():
            for sh in range(n_shards):
                out_ref[sh] = acc[:, sh * wn:(sh + 1) * wn]

    if a_spec is None:
        a_spec = pl.BlockSpec((ts, tk), lambda kt, st: (st, kt))
    return pl.pallas_call(
        body, name=name, grid=(k_dim // tk, nst),
        in_specs=[a_spec, pl.BlockSpec((ts, n_cols), lambda kt, st: (st, 0))],
        out_specs=pl.BlockSpec((n_shards, tk, wn), lambda kt, st: (0, kt, 0)),
        out_shape=jax.ShapeDtypeStruct((n_shards, k_dim, wn), F32),
        scratch_shapes=[pltpu.VMEM((tk, n_cols), F32)],
        compiler_params=_params(("parallel", "arbitrary")),
    )(a, dy)


def _wgrad_ple(name, p, layer, de):
    ts = WGRAD_SEQ_TILE
    spec = pl.BlockSpec((None, None, ts, PLE_DIM), lambda kt, st: (layer, 0, st, 0))
    return _wgrad(name, p, de, N_CHIPS, a_spec=spec, k_dim=PLE_DIM)


def _wgrad_group(pooled, dmr):
    S = pooled.shape[0]
    ts = WGRAD_SEQ_TILE
    nst = S // ts

    def body(p_ref, d_ref, out_ref, acc):
        st = pl.program_id(1)

        @pl.when(st == 0)
        def _():
            acc[...] = jnp.zeros_like(acc)

        acc[...] += _dot_tn(p_ref[...], d_ref[...])

        @pl.when(st == nst - 1)
        def _():
            for sh in range(N_CHIPS):
                out_ref[sh] = acc[sh * 64:(sh + 1) * 64, :]

    blk = pl.BlockSpec((ts, GROUP_DIM), lambda g, st: (st, g))
    return pl.pallas_call(
        body, name="wgrad_group", grid=(N_GROUPS, nst),
        in_specs=[blk, blk],
        out_specs=pl.BlockSpec((N_CHIPS, None, 64, GROUP_DIM), lambda g, st: (0, g, 0, 0)),
        out_shape=jax.ShapeDtypeStruct((N_CHIPS, N_GROUPS, 64, GROUP_DIM), F32),
        scratch_shapes=[pltpu.VMEM((GROUP_DIM, GROUP_DIM), F32)],
        compiler_params=_params(("parallel", "arbitrary")),
    )(pooled, dmr)


GATHER_AT = {
    "a_in": ("a_w_group", "a_w_out", "ple_w0", "ple_gate_w0"),
    "a_mix": ("w_kv",),
    "a_out_ple": ("b_w_in",),
    "attn_fwd": ("b_w_out", "ple_w1", "ple_gate_w1"),
}


REDUCE_AT = {
    "attn_bwd": ("b_w_out", "ple_w1", "ple_gate_w1"),
    "a_mix_bwd": ("a_w_out", "ple_w0", "ple_gate_w0"),
}


def _local_step(x, p, target, w, local=None, state=None):
    w = dict(w)

    def run(fn, host, n_out, *args, **kwargs):
        names = GATHER_AT[host] if local is not None else ()
        res = fn(*args, gather=[local[n] for n in names], **kwargs)
        w.update(zip(names, res[n_out:]))
        return res[:n_out]

    k_gain_t = jnp.tile(w["k_norm"].reshape(1, HEAD_DIM), (1, N_HEADS))
    q_gain_t = jnp.tile(w["b_q_norm"].reshape(1, HEAD_DIM), (1, N_HEADS))

    uz, h_a = run(_a_in, "a_in", 2, x, w["a_norm"], w["a_w_in"])
    wg4 = w["a_w_group"].reshape(N_CHIPS, N_GROUPS, 64, GROUP_DIM)
    wa_out = w["a_w_out"].reshape(D_MODEL, D_MODEL)
    gated_a, pooled = run(_a_mix, "a_mix", 2, uz, wg4, w["a_scale"])
    x1, x2, e_a, gate_a = run(_out_ple, "a_out_ple", 4, "a_out_ple", gated_a, x, wa_out, p, 0,
                              w["ple_w0"], w["ple_gate_w0"])
    h_kv, h_b, k_raw, q_raw, k, q, v, z_b = _b_in(
        x2, w["kv_norm"], w["b_norm"], k_gain_t, q_gain_t, w["w_kv"], w["b_w_in"])
    o, gated_b, ltot, att_steps = run(_attn_fwd, "attn_fwd", 4, q, k, v, z_b)
    wb_out = w["b_w_out"].reshape(D_MODEL, D_MODEL)
    x3, dx4, e_b, gate_b, loss_blk = _out_ple("b_out_ple", gated_b, x2, wb_out, p, 1, w["ple_w1"], w["ple_gate_w1"],
                                              target=target)

    grads, updates = {}, {}

    def hosted(fn, host, n_out, *args):
        if state is None:
            return fn(*args)
        names = REDUCE_AT[host]
        seeds = [updates.get(n[:-1] + "1") if n.startswith("ple") and n.endswith("0") else None for n in names]
        res = fn(*args, reduce=([grads.pop(n) for n in names], *[[t[n] for n in names] for t in state[:3]],
                                [state[3][n] for n in names], seeds))
        for i, n in enumerate(names):
            updates[n] = tuple(group[i] for group in res[n_out:])
        return res[:n_out]

    de_b, dgp_b, dx3, dgated_b = _ple_out_bwd("b_ple_out_bwd", dx4, e_b, gate_b, w["ple_gate_w1"], wb_out)
    grads["b_w_out"] = _wgrad("wgrad_b_out", gated_b, dx3, 1).reshape(N_CHIPS, 256, D_MODEL)
    grads["ple_w1"] = _wgrad_ple("wgrad_ple1", p, 1, de_b)
    grads["ple_gate_w1"] = _wgrad("wgrad_gate1", x3, dgp_b, 1).reshape(N_CHIPS, 256, D_MODEL)
    dq, dk, dv, dz_b = hosted(_attn_bwd, "attn_bwd", 4, q, k, v, ltot, att_steps, dgated_b, o, z_b)
    dqz, dkv, dx2, small_b = _b_in_bwd(dq, dk, dv, dz_b, q_raw, k_raw, x2, dx3, q_gain_t, k_gain_t,
                                       w["b_norm"], w["kv_norm"], w["b_w_in"], w["w_kv"])
    grads["w_kv"] = _wgrad("wgrad_kv", h_kv, dkv, N_CHIPS)
    grads["b_w_in"] = _wgrad("wgrad_b_in", h_b, dqz, N_CHIPS)
    de_a, dgp_a, dx1, dgated_a = _ple_out_bwd("a_ple_out_bwd", dx2, e_a, gate_a, w["ple_gate_w0"], wa_out)
    grads["a_w_out"] = _wgrad("wgrad_a_out", gated_a, dx1, 1).reshape(N_CHIPS, 256, D_MODEL)
    grads["ple_w0"] = _wgrad_ple("wgrad_ple0", p, 0, de_a)
    grads["ple_gate_w0"] = _wgrad("wgrad_gate0", x1, dgp_a, 1).reshape(N_CHIPS, 256, D_MODEL)
    duz, dmr, grad_x, small_a = hosted(_a_mix_bwd, "a_mix_bwd", 4, dgated_a, uz, pooled, wg4, w["a_scale"],
                                       w["a_w_in"], x, dx1, w["a_norm"])
    grads["a_w_in"] = _wgrad("wgrad_a_in", h_a, duz, N_CHIPS)
    grads["a_w_group"] = _wgrad_group(pooled, dmr).reshape(N_CHIPS, N_GROUPS * 64, GROUP_DIM)

    fold = lambda row: jnp.pad(row.reshape(N_HEADS, HEAD_DIM).sum(axis=0), (0, D_MODEL - HEAD_DIM))
    small = jnp.stack([small_a[1], small_a[0], small_b[3], small_b[2], fold(small_b[1]), fold(small_b[0]),
                       jnp.pad(loss_blk[0], (0, D_MODEL - loss_blk.shape[1])), jnp.zeros((D_MODEL,), F32)])
    return grad_x, grads, updates, small


def _mesh_place():
    x, y, c = lax.axis_index("x"), lax.axis_index("y"), lax.axis_index("c")
    other_chips = [(1 - x, y), (x, 1 - y), (1 - x, 1 - y)]
    return x, y, c, other_chips


def _gather_sems(n):
    return [pltpu.SemaphoreType.DMA((3 * n,)), pltpu.SemaphoreType.DMA((3 * n,)),
            pltpu.SemaphoreType.DMA((3 * n,)), pltpu.SemaphoreType.DMA((3 * n,)), pltpu.SemaphoreType.DMA((n,))]


def _gather_copies(srcs, outs, sems):
    send_far, recv_far, send_sib, recv_sib, local_sem = sems
    n = len(srcs)
    x, y, c, chips = _mesh_place()
    me = 2 * x + y
    sibling = (x, y, 1 - c)

    def half(k, which):
        rows = srcs[k].shape[0] // 2
        return pl.ds(pl.multiple_of(which * rows, 16), rows)

    local = [pltpu.make_async_copy(srcs[k], outs[k].at[me], local_sem.at[k]) for k in range(n)]
    far = [pltpu.make_async_remote_copy(
        src_ref=srcs[k].at[half(k, c)], dst_ref=outs[k].at[me, half(k, c)],
        send_sem=send_far.at[j * n + k], recv_sem=recv_far.at[j * n + k], device_id=(px, py, c), device_id_type=MESH)
        for j, (px, py) in enumerate(chips) for k in range(n)]

    def landed(j, k, which, from_far):
        px, py = chips[j]
        piece = outs[k].at[2 * px + py, half(k, which)]
        send, recv = (send_far, recv_far) if from_far else (send_sib, recv_sib)
        return pltpu.make_async_remote_copy(src_ref=piece, dst_ref=piece, send_sem=send.at[j * n + k],
                                            recv_sem=recv.at[j * n + k], device_id=sibling, device_id_type=MESH)

    return local, far, landed, c


def _gather_start(srcs, outs, sems):
    local, far, _, _ = _gather_copies(srcs, outs, sems)
    for cp in local + far:
        cp.start()


def _gather_pass_on(srcs, outs, sems):
    _, _, landed, c = _gather_copies(srcs, outs, sems)
    for j in range(3):
        for k in range(len(srcs)):
            landed(j, k, c, True).wait_recv()
            landed(j, k, c, False).start()


def _gather_finish(srcs, outs, sems):
    local, far, landed, c = _gather_copies(srcs, outs, sems)
    pairs = [(j, k) for j in range(3) for k in range(len(srcs))]
    for j, k in pairs:
        landed(j, k, 1 - c, False).wait_recv()
    for cp in far + [landed(j, k, c, False) for j, k in pairs]:
        cp.wait_send()
    for cp in local:
        cp.wait()


def _call_with_gather(body, *, name, grid, in_specs, out_specs, out_shape, args, gather=(), reduce=None,
                      scratch_shapes=(), vmem_mib=48):
    n_in, n_out, n_scr, n_g = len(args), len(out_shape), len(scratch_shapes), len(gather)
    n_r = len(reduce[0]) if reduce else 0
    pieces = _reduce_pieces(reduce[0], reduce[4]) if reduce else []
    reduce_args = [a for group in reduce[:4] for a in group] if reduce else []
    seeds = reduce[5] if reduce else []
    seeded = [(k, a) for k, seed in enumerate(seeds) if seed is not None for a in range(4)]
    gather_sems = _gather_sems(n_g) if n_g else []
    n_steps = 1
    for g in grid:
        n_steps *= g

    def wrapped(*refs):
        refs = list(refs)
        take = lambda count: [refs.pop(0) for _ in range(count)]
        ins, g_in, r_in = take(n_in), take(n_g), take(4 * n_r)
        take(len(seeded))
        outs, g_out, r_out = take(n_out), take(n_g), take(4 * n_r)
        scratch, sems, r_scratch = take(n_scr), take(len(gather_sems)), refs
        step = 0
        for axis, g in enumerate(grid):
            step = step * g + pl.program_id(axis)
        if n_g:
            @pl.when(step == 0)
            def _():
                _gather_start(g_in, g_out, sems)

        if n_r:
            ticks, drain = _reduce_ticks(pieces, n_r, (*r_in, *r_out, *r_scratch))
            for t, tick in enumerate(ticks[:n_steps]):
                pl.when(step == t)(tick)

        body(*ins, *outs, *scratch)
        if n_r:
            for tick in ticks[n_steps:]:
                pl.when(step == n_steps - 1)(tick)
            pl.when(step == n_steps - 1)(drain)
        if n_g:
            @pl.when(step == max(n_steps - 2, 0))
            def _():
                _gather_pass_on(g_in, g_out, sems)

            @pl.when(step == n_steps - 1)
            def _():
                _gather_finish(g_in, g_out, sems)

    hbm = pl.BlockSpec(memory_space=pltpu.HBM)
    res = pl.pallas_call(
        wrapped, name=name, grid=grid,
        in_specs=list(in_specs) + [hbm] * (n_g + 4 * n_r + len(seeded)),
        out_specs=list(out_specs) + [hbm] * (n_g + 4 * n_r),
        out_shape=list(out_shape) + [jax.ShapeDtypeStruct((N_CHIPS,) + g.shape, BF16) for g in gather]
        + ([jax.ShapeDtypeStruct(w.shape, F32) for _ in range(4) for w in reduce[1]] if reduce else []),
        input_output_aliases={n_in + n_g + 4 * n_r + i: n_out + n_g + a * n_r + k for i, (k, a) in enumerate(seeded)},
        scratch_shapes=list(scratch_shapes) + gather_sems + (_reduce_scratch() if reduce else []),
        compiler_params=_params(("arbitrary",) * len(grid), vmem_mib),
    )(*args, *gather, *reduce_args, *[seeds[k][a] for k, a in seeded])
    if not reduce:
        return res
    plain = list(res[:n_out + n_g])
    return plain + [res[n_out + n_g + i * n_r:n_out + n_g + (i + 1) * n_r] for i in range(4)]


def _allgather_weights(shards, small, casts):
    n = len(shards)
    cast_out = [(k, r0, r1) for k, (_, ranges) in enumerate(casts) for r0, r1 in ranges]
    n_c, n_co = len(casts), len(cast_out)

    def body(*refs):
        ins, small_in, cast_in = refs[:n], refs[n], refs[n + 1:n + 1 + n_c]
        refs = refs[n + 1 + n_c:]
        outs, small_out, cast_dst = refs[:n], refs[n], refs[n + 1:n + 1 + n_co]
        refs = refs[n + 1 + n_co:]
        cast, cast_buf = refs[:n], refs[n:n + n_co]
        send_far, recv_far, send_sib, recv_sib, send_small, recv_small, local_sem, cast_sem = refs[n + n_co:]
        x, y, c, chips = _mesh_place()
        me = 2 * x + y
        sibling = (x, y, 1 - c)

        def half(k, which):
            rows = ins[k].shape[0] // 2
            return pl.ds(pl.multiple_of(which * rows, 16), rows)

        local = []
        for k in range(n):
            cast[k][...] = ins[k][...].astype(BF16)
            local.append(pltpu.make_async_copy(cast[k], outs[k].at[me], local_sem.at[k]))
            local[-1].start()
        local.append(pltpu.make_async_copy(small_in, small_out.at[me], local_sem.at[n]))
        local[-1].start()

        sends = []
        for j, (px, py) in enumerate(chips):
            for k in range(n):
                cp = pltpu.make_async_remote_copy(
                    src_ref=cast[k].at[half(k, c)], dst_ref=outs[k].at[me, half(k, c)],
                    send_sem=send_far.at[j * n + k], recv_sem=recv_far.at[j * n + k],
                    device_id=(px, py, c), device_id_type=MESH)
                cp.start()
                sends.append(cp)
            cp = pltpu.make_async_remote_copy(
                src_ref=small_in, dst_ref=small_out.at[me], send_sem=send_small.at[j], recv_sem=recv_small.at[j],
                device_id=(px, py, c), device_id_type=MESH)
            cp.start()
            sends.append(cp)

        for i, (k, r0, r1) in enumerate(cast_out):
            cast_buf[i][...] = cast_in[k][r0:r1, :].astype(BF16)
            local.append(pltpu.make_async_copy(cast_buf[i], cast_dst[i], cast_sem.at[i]))
            local[-1].start()

        def landed(j, k, which, sems_s, sems_r, device):
            px, py = chips[j]
            piece = outs[k].at[2 * px + py, half(k, which)]
            return pltpu.make_async_remote_copy(
                src_ref=piece, dst_ref=piece, send_sem=sems_s.at[j * n + k], recv_sem=sems_r.at[j * n + k],
                device_id=device, device_id_type=MESH)

        for j in range(len(chips)):
            for k in range(n):
                landed(j, k, c, send_far, recv_far, sibling).wait_recv()
                cp = landed(j, k, c, send_sib, recv_sib, sibling)
                cp.start()
                sends.append(cp)
        for j, (px, py) in enumerate(chips):
            for k in range(n):
                landed(j, k, 1 - c, send_sib, recv_sib, sibling).wait_recv()
            pltpu.make_async_remote_copy(
                src_ref=small_in, dst_ref=small_out.at[2 * px + py], send_sem=send_small.at[j],
                recv_sem=recv_small.at[j], device_id=(px, py, c), device_id_type=MESH).wait_recv()
        for cp in sends:
            cp.wait_send()
        for cp in local:
            cp.wait()

    vmem = pl.BlockSpec(memory_space=pltpu.VMEM)
    hbm = pl.BlockSpec(memory_space=pltpu.HBM)
    cast_shapes = [(r1 - r0, casts[k][0].shape[1]) for k, r0, r1 in cast_out]
    res = pl.pallas_call(
        body, name="allgather_weights",
        in_specs=[vmem] * (n + 1 + n_c), out_specs=[hbm] * (n + 1 + n_co),
        out_shape=[jax.ShapeDtypeStruct((N_CHIPS,) + s.shape, BF16) for s in shards]
        + [jax.ShapeDtypeStruct((N_CHIPS,) + small.shape, F32)]
        + [jax.ShapeDtypeStruct(s, BF16) for s in cast_shapes],
        scratch_shapes=[pltpu.VMEM(s.shape, BF16) for s in shards] + [pltpu.VMEM(s, BF16) for s in cast_shapes]
        + [pltpu.SemaphoreType.DMA((3 * n,)), pltpu.SemaphoreType.DMA((3 * n,)),
           pltpu.SemaphoreType.DMA((3 * n,)), pltpu.SemaphoreType.DMA((3 * n,)),
           pltpu.SemaphoreType.DMA((3,)), pltpu.SemaphoreType.DMA((3,)),
           pltpu.SemaphoreType.DMA((n + 1,)), pltpu.SemaphoreType.DMA((n_co,))],
        compiler_params=_params(None, 40),
    )(*shards, small, *[a for a, _ in casts])
    return res[:n], res[n], res[n + 1:]


def _adamw(w, g, m, v):
    m = ADAM_B1 * m + (1.0 - ADAM_B1) * g
    v = ADAM_B2 * v + (1.0 - ADAM_B2) * (g * g)
    m_hat = m / (1.0 - ADAM_B1 ** ADAM_STEP)
    v_hat = v / (1.0 - ADAM_B2 ** ADAM_STEP)
    delta = -ADAM_LR * (m_hat / (jnp.sqrt(v_hat) + ADAM_EPS) + ADAM_WD * w)
    return delta, m, v


RS_PIECE_ROWS = 128
RS_PIECE_COLS = 512


def _reduce_adam_all(grads, ws, ms, vs, small, bases=None, seeds=None):
    n_w = len(grads)
    pieces = _reduce_pieces(grads, bases)
    lanes = [pieces[0::2], pieces[1::2]]
    n_lane = len(_reduce_scratch())
    n_small = len(_small_sum_scratch(small.shape))
    seeds = seeds or [None] * n_w
    seeded = [(k, a) for k, seed in enumerate(seeds) if seed is not None for a in range(4)]
    n_in = 4 * n_w + 1

    def body(*refs):
        refs = list(refs)
        del refs[n_in:n_in + len(seeded)]
        small_in = refs.pop(4 * n_w)
        small_out = refs.pop(8 * n_w)
        small_scratch = [refs.pop() for _ in range(n_small)][::-1]
        sends = _small_sum_start(small_in, *small_scratch)
        arrays, scratch = refs[:8 * n_w], refs[8 * n_w:]
        runs = [_reduce_ticks(lane, n_w, arrays + scratch[i * n_lane:(i + 1) * n_lane])
                for i, lane in enumerate(lanes) if lane]
        for t in range(max(len(ticks) for ticks, _ in runs)):
            for ticks, _ in runs:
                if t < len(ticks):
                    ticks[t]()
        for _, drain in runs:
            drain()
        _small_sum_finish(sends, small_scratch[0], small_out)

    hbm = pl.BlockSpec(memory_space=pltpu.HBM)
    vmem = pl.BlockSpec(memory_space=pltpu.VMEM)
    outs = pl.pallas_call(
        body, name="reduce_adam_all",
        in_specs=[hbm] * (4 * n_w) + [vmem] + [hbm] * len(seeded), out_specs=[hbm] * (4 * n_w) + [vmem],
        out_shape=[jax.ShapeDtypeStruct(w.shape, F32) for _ in range(4) for w in ws]
        + [jax.ShapeDtypeStruct(small.shape, F32)],
        input_output_aliases={n_in + i: a * n_w + k for i, (k, a) in enumerate(seeded)},
        scratch_shapes=_reduce_scratch() * len(lanes) + _small_sum_scratch(small.shape),
        compiler_params=_params(None, 56),
    )(*grads, *ws, *ms, *vs, small, *[seeds[k][a] for k, a in seeded])
    return [outs[i * n_w:(i + 1) * n_w] for i in range(4)], outs[4 * n_w]


def _reduce_pieces(grads, bases=None):
    pieces = []
    for k, g in enumerate(grads):
        hr, cols = g.shape[1] // 2, g.shape[2]
        pr, pc = min(hr, RS_PIECE_ROWS), min(cols, RS_PIECE_COLS)
        base = bases[k] if bases else 0
        pieces += [(k, ro, hr, co, pr, pc, base) for ro in range(0, hr, pr) for co in range(0, cols, pc)]
    return pieces


def _reduce_scratch():
    P, C = RS_PIECE_ROWS, RS_PIECE_COLS
    return [
        pltpu.VMEM((3, N_CHIPS, P, C), F32), pltpu.VMEM((3, N_CHIPS, P, C), F32),
        pltpu.VMEM((2, N_CHIPS, P, C), BF16), pltpu.VMEM((2, N_CHIPS, P, C), BF16),
        pltpu.VMEM((2, N_CHIPS, P, C), F32),
        pltpu.VMEM((2, 3, P, C), BF16), pltpu.VMEM((2, 3, P, C), BF16),
        pltpu.VMEM((2, 2, P, C), F32),
        pltpu.VMEM((2, 3, 2, P, C), F32), pltpu.VMEM((2, 4, 2, P, C), F32),
        pltpu.SemaphoreType.DMA((3, 2)), pltpu.SemaphoreType.DMA((2, 3, 2)),
        pltpu.SemaphoreType.DMA((2,)), pltpu.SemaphoreType.DMA((2,)),
        pltpu.SemaphoreType.DMA((2, 3)), pltpu.SemaphoreType.DMA((2, 3)),
        pltpu.SemaphoreType.DMA((2,)), pltpu.SemaphoreType.DMA((2,)),
        pltpu.SemaphoreType.DMA((2, 4, 2))]


def _reduce_ticks(pieces, n_w, refs):
    n = len(pieces)

    def build(*refs):
        g_in, w_in, m_in, v_in = (refs[i * n_w:(i + 1) * n_w] for i in range(4))
        g_out, d_out, m_out, v_out = (refs[(4 + i) * n_w:(5 + i) * n_w] for i in range(4))
        (gm, go, sb1, rb1, part, sb2, rb2, fin, wmv, outs,
         ld_sem, wmv_sem, s1_send, s1_recv, s2_send, s2_recv, s3_send, s3_recv, out_sem) = refs[8 * n_w:]
        x, y, c, chips = _mesh_place()
        me = 2 * x + y
        sibling = (x, y, 1 - c)

        def at_hbm(i, which, in_shard):
            _, ro, hr, co, pr, pc, base = pieces[i]
            half = c if which == 0 else 1 - c
            return pl.ds(pl.multiple_of((base if in_shard else 0) + half * hr + ro, 64), pr), pl.ds(co, pc)

        def win(i):
            return pl.ds(0, pieces[i][4]), pl.ds(0, pieces[i][5])

        every = slice(None)

        def loads(i):
            k, s = pieces[i][0], i % 3
            return [pltpu.make_async_copy(g_in[k].at[(every,) + at_hbm(i, h, False)], buf.at[(s, every) + win(i)],
                                          ld_sem.at[s, h])
                    for h, buf in enumerate((gm, go))]

        def wmv_loads(i):
            k, s = pieces[i][0], i % 2
            return [pltpu.make_async_copy(src[k].at[at_hbm(i, h, True)], wmv.at[(s, a, h) + win(i)], wmv_sem.at[s, a, h])
                    for a, src in enumerate((w_in, m_in, v_in)) for h in range(2)]

        def stores(i):
            k, s = pieces[i][0], i % 2
            return [pltpu.make_async_copy(outs.at[(s, a, h) + win(i)], dst[k].at[at_hbm(i, h, True)], out_sem.at[s, a, h])
                    for a, dst in enumerate((g_out, d_out, m_out, v_out)) for h in range(2)]

        def swap1(i):
            s = i % 2
            return pltpu.make_async_remote_copy(
                src_ref=sb1.at[(s, every) + win(i)], dst_ref=rb1.at[(s, every) + win(i)],
                send_sem=s1_send.at[s], recv_sem=s1_recv.at[s], device_id=sibling, device_id_type=MESH)

        def far2(i, j):
            s = i % 2
            px, py = chips[j]
            return pltpu.make_async_remote_copy(
                src_ref=sb2.at[(s, j) + win(i)], dst_ref=rb2.at[(s, j) + win(i)],
                send_sem=s2_send.at[s, j], recv_sem=s2_recv.at[s, j], device_id=(px, py, c), device_id_type=MESH)

        def swap3(i):
            s = i % 2
            return pltpu.make_async_remote_copy(
                src_ref=fin.at[(s, 0) + win(i)], dst_ref=fin.at[(s, 1) + win(i)],
                send_sem=s3_send.at[s], recv_sem=s3_recv.at[s], device_id=sibling, device_id_type=MESH)

        def stage0(i):
            for cp in loads(i):
                cp.start()

        def stage1(i):
            s, s3 = i % 2, i % 3
            for cp in loads(i):
                cp.wait()
            sb1[(s, every) + win(i)] = go[(s3, every) + win(i)].astype(BF16)
            swap1(i).start()

        def stage2(i):
            s, s3 = i % 2, i % 3
            swap1(i).wait()
            part[(s, every) + win(i)] = gm[(s3, every) + win(i)] + rb1[(s, every) + win(i)].astype(F32)
            for j, (px, py) in enumerate(chips):
                sb2[(s, j) + win(i)] = part[(s, 2 * px + py) + win(i)].astype(BF16)
                far2(i, j).start()

        def stage3(i):
            s = i % 2
            total = part[(s, me) + win(i)]
            for j in range(3):
                far2(i, j).wait()
                total = total + rb2[(s, j) + win(i)].astype(F32)
            fin[(s, 0) + win(i)] = total
            swap3(i).start()
            for cp in wmv_loads(i):
                cp.start()

        def stage4(i):
            s = i % 2
            if i >= 2:
                for cp in stores(i - 2):
                    cp.wait()
            swap3(i).wait()
            for cp in wmv_loads(i):
                cp.wait()
            both = (every,) + win(i)
            g = fin[(s,) + both]
            delta, m_new, v_new = _adamw(wmv[(s, 0) + both], g, wmv[(s, 1) + both], wmv[(s, 2) + both])
            outs[(s, 0) + both] = g
            outs[(s, 1) + both] = delta
            outs[(s, 2) + both] = m_new
            outs[(s, 3) + both] = v_new
            for cp in stores(i):
                cp.start()

        stages = (stage0, stage1, stage2, stage3, stage4)

        def tick(t):
            for age in reversed(range(len(stages))):
                if 0 <= t - age < n:
                    stages[age](t - age)

        def drain():
            for i in range(max(0, n - 2), n):
                for cp in stores(i):
                    cp.wait()

        return [functools.partial(tick, t) for t in range(n + len(stages) - 1)], drain

    return build(*refs)


N_DEVICES = 8


def _small_sum_scratch(shape):
    return [pltpu.VMEM((N_DEVICES,) + shape, F32),
            pltpu.SemaphoreType.DMA((N_DEVICES - 1,)), pltpu.SemaphoreType.DMA((N_DEVICES - 1,))]


def _small_sum_start(part_ref, buf, send_sem, recv_sem):
    x, y, c, _ = _mesh_place()
    me = 4 * x + 2 * y + c
    buf[me] = part_ref[...]
    sends = []
    for k in range(1, N_DEVICES):
        peer = ((1 - x) if k & 4 else x, (1 - y) if k & 2 else y, (1 - c) if k & 1 else c)
        cp = pltpu.make_async_remote_copy(src_ref=part_ref, dst_ref=buf.at[me], send_sem=send_sem.at[k - 1],
                                          recv_sem=recv_sem.at[k - 1], device_id=peer, device_id_type=MESH)
        cp.start()
        sends.append(cp)
    return sends


def _small_sum_finish(sends, buf, out_ref):
    for cp in sends:
        cp.wait_recv()
    total = buf[0]
    for s in range(1, N_DEVICES):
        total = total + buf[s]
    out_ref[...] = total
    for cp in sends:
        cp.wait_send()


def _adam_small(w, g, m, v):
    def body(w_ref, g_ref, m_ref, v_ref, d_ref, mo_ref, vo_ref):
        delta, m_new, v_new = _adamw(w_ref[...], g_ref[...], m_ref[...], v_ref[...])
        d_ref[...] = delta
        mo_ref[...] = m_new
        vo_ref[...] = v_new

    vmem = pl.BlockSpec(memory_space=pltpu.VMEM)
    return pl.pallas_call(
        body, name="adam_small", in_specs=[vmem] * 4, out_specs=[vmem] * 3,
        out_shape=[jax.ShapeDtypeStruct(w.shape, F32)] * 3,
    )(w, g, m, v)


BIG = ("a_w_in", "a_w_group", "a_w_out", "w_kv", "b_w_in", "b_w_out", "ple_w", "ple_gate_w")
SMALL = ("a_norm", "a_scale", "kv_norm", "b_norm", "k_norm", "b_q_norm")
SMALL_SHARDED = ("a_norm", "a_scale")
WEIGHTS = ("a_norm", "a_w_in", "a_w_group", "a_scale", "a_w_out", "kv_norm", "w_kv", "k_norm", "b_norm", "b_w_in",
           "b_q_norm", "b_w_out", "ple_w", "ple_gate_w")


def _as_matrix(a):
    return a.reshape(-1, a.shape[-1])


def _pack_small(arrs):
    rows = [jnp.pad(a.reshape(-1), (0, D_MODEL - a.size)) for a in arrs]
    rows += [jnp.zeros((D_MODEL,), F32)] * (8 - len(rows))
    return jnp.stack(rows)


def kernel(x, p, a_norm, a_w_in, a_w_group, a_scale, a_w_out, kv_norm, w_kv, k_norm, b_norm, b_w_in, b_q_norm, b_w_out, ple_w, ple_gate_w, loss_target, m_a_norm, m_a_w_in, m_a_w_group, m_a_scale, m_a_w_out, m_kv_norm, m_w_kv, m_k_norm, m_b_norm, m_b_w_in, m_b_q_norm, m_b_w_out, m_ple_w, m_ple_gate_w, v_a_norm, v_a_w_in, v_a_w_group, v_a_scale, v_a_w_out, v_kv_norm, v_w_kv, v_k_norm, v_b_norm, v_b_w_in, v_b_q_norm, v_b_w_out, v_ple_w, v_ple_gate_w):
    wts = dict(a_norm=a_norm, a_w_in=a_w_in, a_w_group=a_w_group, a_scale=a_scale, a_w_out=a_w_out, kv_norm=kv_norm,
               w_kv=w_kv, k_norm=k_norm, b_norm=b_norm, b_w_in=b_w_in, b_q_norm=b_q_norm, b_w_out=b_w_out,
               ple_w=ple_w, ple_gate_w=ple_gate_w)
    mom = dict(a_norm=m_a_norm, a_w_in=m_a_w_in, a_w_group=m_a_w_group, a_scale=m_a_scale, a_w_out=m_a_w_out,
               kv_norm=m_kv_norm, w_kv=m_w_kv, k_norm=m_k_norm, b_norm=m_b_norm, b_w_in=m_b_w_in,
               b_q_norm=m_b_q_norm, b_w_out=m_b_w_out, ple_w=m_ple_w, ple_gate_w=m_ple_gate_w)
    var = dict(a_norm=v_a_norm, a_w_in=v_a_w_in, a_w_group=v_a_w_group, a_scale=v_a_scale, a_w_out=v_a_w_out,
               kv_norm=v_kv_norm, w_kv=v_w_kv, k_norm=v_k_norm, b_norm=v_b_norm, b_w_in=v_b_w_in,
               b_q_norm=v_b_q_norm, b_w_out=v_b_w_out, ple_w=v_ple_w, ple_gate_w=v_ple_gate_w)
    S = x.shape[1]
    chip = 2 * lax.axis_index("x") + lax.axis_index("y")

    sharded_small = jnp.concatenate([a_norm.reshape(1, 256), a_scale.reshape(1, 256), jnp.zeros((6, 256), F32)], axis=0)
    later = ("a_w_group", "a_w_out", "w_kv", "b_w_in", "b_w_out", "ple_w", "ple_gate_w")
    (a_w_in_full,), small_full, copies = _allgather_weights(
        [_as_matrix(a_w_in)], sharded_small,
        [(_as_matrix(wts[n]), [(0, 256), (256, 512)] if n.startswith("ple") else [(0, _as_matrix(wts[n]).shape[0])])
         for n in later])
    local = dict(zip(("a_w_group", "a_w_out", "w_kv", "b_w_in", "b_w_out", "ple_w0", "ple_w1", "ple_gate_w0",
                      "ple_gate_w1"), copies))
    full = dict(a_w_in=a_w_in_full,
                a_norm=small_full[:, 0, :].reshape(1, D_MODEL), a_scale=small_full[:, 1, :].reshape(1, D_MODEL),
                kv_norm=kv_norm.reshape(1, D_MODEL), b_norm=b_norm.reshape(1, D_MODEL), k_norm=k_norm, b_q_norm=b_q_norm)

    def shards(t):
        out = {}
        for n in BIG:
            for entry in ((n + "0", n + "1") if n.startswith("ple") else (n,)):
                out[entry] = _as_matrix(t[n])
        return out

    base = {n: 256 if n.startswith("ple") and n.endswith("1") else 0 for n in shards(wts)}
    state = (shards(wts), shards(mom), shards(var), base)
    grad_x, grads, updates, small_part = _local_step(x.reshape(S, D_MODEL), p, loss_target.reshape(S, D_MODEL),
                                                     full, local, state)

    names = sorted(grads)
    reduced, small_sum = _reduce_adam_all(
        [grads[n] for n in names], *[[t[n] for n in names] for t in state[:3]], small_part,
        bases=[base[n] for n in names], seeds=[updates.get(n[:-1] + "1") if n.startswith("ple") else None for n in names])
    for i, n in enumerate(names):
        updates[n] = tuple(group[i] for group in reduced)
    out_g, out_d, out_m, out_v = {}, {}, {}, {}
    for n in BIG:
        for i, out in enumerate((out_g, out_d, out_m, out_v)):
            out[n] = updates[n + "0" if n.startswith("ple") else n][i].reshape(wts[n].shape)

    loss = small_sum[len(SMALL), 0]
    small_rows = []
    for i, n in enumerate(SMALL):
        row = small_sum[i]
        if n in SMALL_SHARDED:
            row = lax.dynamic_slice(row, (chip * 256,), (256,))
        else:
            row = row[:wts[n].size]
        small_rows.append(row)
    g_small = _pack_small(small_rows)
    d_small, m_small, v_small = _adam_small(_pack_small([wts[n] for n in SMALL]), g_small,
                                            _pack_small([mom[n] for n in SMALL]), _pack_small([var[n] for n in SMALL]))
    for i, n in enumerate(SMALL):
        shape, size = wts[n].shape, wts[n].size
        out_g[n], out_d[n], out_m[n], out_v[n] = (t[i, :size].reshape(shape) for t in (g_small, d_small, m_small, v_small))

    return (loss, grad_x.reshape(1, S, D_MODEL), *[out_g[n] for n in WEIGHTS], *[out_d[n] for n in WEIGHTS],
            *[out_m[n] for n in WEIGHTS], *[out_v[n] for n in WEIGHTS])
```

```python
import functools

import jax
import jax.numpy as jnp
from jax import lax
from jax.experimental import pallas as pl
from jax.experimental.pallas import tpu as pltpu

F32 = jnp.float32
BF16 = jnp.bfloat16
MESH = pl.DeviceIdType.MESH

D_MODEL = 1024
N_HEADS = 16
HEAD_DIM = 64
PLE_DIM = 256
N_GROUPS = 4
GROUP_DIM = 256
POOL_WINDOWS = (2, 4, 8, 16)
N_CHIPS = 4
EPS = 1e-6
SB_SCALE = HEAD_DIM ** -0.5

ADAM_LR = 0.001
ADAM_B1 = 0.9
ADAM_B2 = 0.999
ADAM_EPS = 1e-08
ADAM_WD = 0.01
ADAM_STEP = 10

ROW_TILE = 256
WIDE_ROW_TILE = 512
EXP_UNDERFLOW = -104.0
ATT_Q_TILE = 512
ATT_K_TILE = 256
WGRAD_SEQ_TILE = 1024
WGRAD_ACC_BYTES = 4 * 1024 * 1024
MIB = 1024 * 1024


def _params(semantics=None, vmem_mib=48):
    return pltpu.CompilerParams(dimension_semantics=semantics, vmem_limit_bytes=vmem_mib * MIB)


def _dot(a, b):
    return jnp.dot(a, b, preferred_element_type=F32)


def _dot_nt(a, b):
    return lax.dot_general(a, b, (((1,), (1,)), ((), ())), preferred_element_type=F32)


def _dot_tn(a, b):
    return lax.dot_general(a, b, (((0,), (0,)), ((), ())), preferred_element_type=F32)


def _hilo(x):
    hi = x.astype(BF16)
    lo = (x - hi.astype(F32)).astype(BF16)
    return hi, lo


def _dot_hilo(x, w):
    hi, lo = _hilo(x)
    return _dot(hi, w) + _dot(lo, w)


def _sigmoid(z):
    return jax.nn.sigmoid(z)


def _dsilu(z, sg):
    return sg * (1.0 + z * (1.0 - sg))


def _mask_bf16(cond):
    return jnp.where(cond, 1.0, 0.0).astype(BF16)


def _head_mean_matrix():
    r = lax.broadcasted_iota(jnp.int32, (256, 256), 0) // HEAD_DIM
    c = lax.broadcasted_iota(jnp.int32, (256, 256), 1) // HEAD_DIM
    return _mask_bf16(r == c)


def _head_mean(x, bd):
    parts = []
    for s in range(x.shape[1] // 256):
        parts.append(_dot_hilo(x[:, s * 256:(s + 1) * 256], bd))
    out = parts[0] if len(parts) == 1 else jnp.concatenate(parts, axis=1)
    return out * (1.0 / HEAD_DIM)


def _a_in(x, gain, w_sh, gather=()):
    S = x.shape[0]
    tm = 512
    nsh, _, wn = w_sh.shape

    def body(x_ref, g_ref, w_ref, uz_ref, h_ref):
        @pl.when(pl.program_id(1) == 0)
        def _():
            xv = x_ref[...]
            r = lax.rsqrt(jnp.mean(xv * xv, axis=-1, keepdims=True) + EPS)
            h_ref[...] = (xv * r * g_ref[...]).astype(BF16)

        uz_ref[...] = _dot(h_ref[...], w_ref[0])

    return _call_with_gather(
        body, name="a_in", grid=(S // tm, nsh),
        in_specs=[pl.BlockSpec((tm, D_MODEL), lambda i, j: (i, 0)),
                  pl.BlockSpec((1, D_MODEL), lambda i, j: (0, 0)),
                  pl.BlockSpec((1, D_MODEL, wn), lambda i, j: (j, 0, 0))],
        out_specs=[pl.BlockSpec((tm, wn), lambda i, j: (i, j)),
                   pl.BlockSpec((tm, D_MODEL), lambda i, j: (i, 0))],
        out_shape=[jax.ShapeDtypeStruct((S, nsh * wn), F32),
                   jax.ShapeDtypeStruct((S, D_MODEL), BF16)],
        args=(x, gain, w_sh), gather=gather)


def _inv_count(first_row, rows, w):
    t1 = first_row + 1 + lax.broadcasted_iota(jnp.int32, (rows, 1), 0)
    return 1.0 / jnp.minimum(t1, w).astype(F32)


def _group_weight(wg_ref, g):
    return jnp.concatenate([wg_ref[sh, g] for sh in range(N_CHIPS)], axis=0)


def _a_mix(uz, wg, scale, gather=()):
    S = uz.shape[0]
    tm = ROW_TILE

    def body(u_ref, up_ref, z_ref, wg_ref, sc_ref, ga_ref, p_ref):
        i = pl.program_id(0)
        row = lax.broadcasted_iota(jnp.int32, (tm, tm), 0)
        col = lax.broadcasted_iota(jnp.int32, (tm, tm), 1)
        d = row - col
        for g, w in enumerate(POOL_WINDOWS):
            cols = slice(g * GROUP_DIM, (g + 1) * GROUP_DIM)
            t_main = _mask_bf16((d >= 0) & (d < w))
            t_halo = _mask_bf16(d + tm < w)
            u = u_ref[:, cols]
            up = jnp.where(i > 0, up_ref[:, cols], 0.0)
            hi, lo = _hilo(u)
            hip, lop = _hilo(up)
            wsum = _dot(t_main, hi) + _dot(t_main, lo) + _dot(t_halo, hip) + _dot(t_halo, lop)
            pooled = (wsum * _inv_count(i * tm, tm, w) - u).astype(BF16)
            p_ref[:, cols] = pooled
            mraw = _dot(pooled, _group_weight(wg_ref, g))
            z = z_ref[:, cols]
            ga_ref[:, cols] = (mraw * sc_ref[:, cols] * (z * _sigmoid(z))).astype(BF16)

    return _call_with_gather(
        body, name="a_mix", grid=(S // tm,),
        in_specs=[pl.BlockSpec((tm, D_MODEL), lambda i: (i, 0)),
                  pl.BlockSpec((tm, D_MODEL), lambda i: (jnp.maximum(i - 1, 0), 0)),
                  pl.BlockSpec((tm, D_MODEL), lambda i: (i, 1)),
                  pl.BlockSpec((N_CHIPS, N_GROUPS, 64, GROUP_DIM), lambda i: (0, 0, 0, 0)),
                  pl.BlockSpec((1, D_MODEL), lambda i: (0, 0))],
        out_specs=[pl.BlockSpec((tm, D_MODEL), lambda i: (i, 0)),
                   pl.BlockSpec((tm, D_MODEL), lambda i: (i, 0))],
        out_shape=[jax.ShapeDtypeStruct((S, D_MODEL), BF16),
                   jax.ShapeDtypeStruct((S, D_MODEL), BF16)],
        args=(uz, uz, uz, wg, scale), gather=gather)


def _out_ple(name, gated, x_in, w_out, p, layer, ple_w, ple_g, target=None, gather=()):
    S = x_in.shape[0]
    tm = WIDE_ROW_TILE
    with_loss = target is not None

    def body(*refs):
        if with_loss:
            g_ref, x_ref, wo_ref, p_ref, pw_ref, pg_ref, t_ref, xm_ref, dx_ref, e_ref, gt_ref, loss_ref = refs
        else:
            g_ref, x_ref, wo_ref, p_ref, pw_ref, pg_ref, xm_ref, xo_ref, e_ref, gt_ref = refs
        xm = x_ref[...] + _dot(g_ref[...], wo_ref[...])
        xm_ref[...] = xm
        pb = p_ref[...].astype(BF16)
        e = jnp.concatenate([_dot(pb, pw_ref[sh]) for sh in range(N_CHIPS)], axis=1)
        pg = jnp.concatenate([pg_ref[sh] for sh in range(N_CHIPS)], axis=0)
        gate = _sigmoid(_dot(xm.astype(BF16), pg))
        e_ref[...] = e.astype(BF16)
        gt_ref[...] = gate.astype(BF16)
        xo = xm + e * gate
        if with_loss:
            diff = xo - t_ref[...]
            dx_ref[...] = diff * (1.0 / D_MODEL)

            @pl.when(pl.program_id(0) == 0)
            def _():
                loss_ref[...] = jnp.zeros_like(loss_ref)

            loss_ref[...] += jnp.sum(diff * diff) * (0.5 / D_MODEL)
        else:
            xo_ref[...] = xo

    row = pl.BlockSpec((tm, D_MODEL), lambda i: (i, 0))
    in_specs = [row, row,
                pl.BlockSpec((D_MODEL, D_MODEL), lambda i: (0, 0)),
                pl.BlockSpec((None, None, tm, PLE_DIM), lambda i: (layer, 0, i, 0)),
                pl.BlockSpec((N_CHIPS, PLE_DIM, 256), lambda i: (0, 0, 0)),
                pl.BlockSpec((N_CHIPS, 256, D_MODEL), lambda i: (0, 0, 0))]
    args = [gated, x_in, w_out, p, ple_w, ple_g]
    out_specs = [row, row, row, row]
    out_shape = [jax.ShapeDtypeStruct((S, D_MODEL), F32), jax.ShapeDtypeStruct((S, D_MODEL), F32),
                 jax.ShapeDtypeStruct((S, D_MODEL), BF16), jax.ShapeDtypeStruct((S, D_MODEL), BF16)]
    if with_loss:
        in_specs.append(row)
        args.append(target)
        out_specs.append(pl.BlockSpec((8, 128), lambda i: (0, 0)))
        out_shape.append(jax.ShapeDtypeStruct((8, 128), F32))
    return _call_with_gather(body, name=name, grid=(S // tm,), in_specs=in_specs, out_specs=out_specs,
                             out_shape=out_shape, args=args, gather=gather)


def _b_in(x, kv_gain, b_gain, k_gain_t, q_gain_t, w_kv, w_in, gather=()):
    S = x.shape[0]
    tm = ROW_TILE

    def body(x_ref, kvg_ref, bg_ref, kg_ref, qg_ref, wkv_ref, win_ref,
             hkv_ref, hb_ref, kraw_ref, qraw_ref, k_ref, q_ref, v_ref, z_ref):
        xv = x_ref[...]
        y = xv * lax.rsqrt(jnp.mean(xv * xv, axis=-1, keepdims=True) + EPS)
        hkv = (y * kvg_ref[...]).astype(BF16)
        hb = (y * bg_ref[...]).astype(BF16)
        hkv_ref[...] = hkv
        hb_ref[...] = hb
        bd = _head_mean_matrix()

        def head_norm(raw, gain):
            rr = lax.rsqrt(_head_mean(raw * raw, bd) + EPS)
            return raw * rr * gain

        for sh in range(N_CHIPS):
            kvc = _dot(hkv, wkv_ref[sh])
            qzc = _dot(hb, win_ref[sh])
            cols = slice((sh % 2) * 512, (sh % 2) * 512 + 512)
            if sh < 2:
                kraw_ref[:, cols] = kvc.astype(BF16)
                qraw_ref[:, cols] = qzc.astype(BF16)
                k_ref[:, cols] = head_norm(kvc, kg_ref[:, cols]).astype(BF16)
                q_ref[:, cols] = (head_norm(qzc, qg_ref[:, cols]) * SB_SCALE).astype(BF16)
            else:
                v_ref[:, cols] = kvc.astype(BF16)
                z_ref[:, cols] = qzc.astype(BF16)

    row = pl.BlockSpec((tm, D_MODEL), lambda i: (i, 0))
    vec = pl.BlockSpec((1, D_MODEL), lambda i: (0, 0))
    wsp = pl.BlockSpec((N_CHIPS, D_MODEL, 512), lambda i: (0, 0, 0))
    return _call_with_gather(
        body, name="b_in", grid=(S // tm,),
        in_specs=[row, vec, vec, vec, vec, wsp, wsp],
        out_specs=[row] * 8,
        out_shape=[jax.ShapeDtypeStruct((S, D_MODEL), BF16)] * 8,
        args=(x, kv_gain, b_gain, k_gain_t, q_gain_t, w_kv, w_in), gather=gather, vmem_mib=56)


def _softplus_parts(z):
    e = jnp.exp(-jnp.abs(z))
    return -(jnp.maximum(z, 0.0) + jnp.log(1.0 + e)), e


def _add_rows(total, rows, update):
    lo, hi = rows
    parts = ([total[:lo]] if lo else []) + [total[lo:hi] + update] + ([total[hi:]] if hi < total.shape[0] else [])
    return parts[0] if len(parts) == 1 else jnp.concatenate(parts, axis=0)


def _attn_fwd(q, k, v, zgate, gather=()):
    S = q.shape[0]
    tq, tk = ATT_Q_TILE, ATT_K_TILE
    kpq = tq // tk
    assert kpq == 2

    def body(q_ref, k_ref, v_ref, z_ref, o_ref, g_ref, lt_ref, steps_ref):
        qi = pl.program_id(1)
        lane = lax.broadcasted_iota(jnp.int32, (1, 128), 1)
        ri = lax.broadcasted_iota(jnp.int32, (tk, tk), 0)
        ci = lax.broadcasted_iota(jnp.int32, (tk, tk), 1)
        later_mat = _mask_bf16(ri > ci)
        causal = ci < ri
        qv = q_ref[...]
        first = lane < HEAD_DIM
        q_heads = (jnp.where(first, qv, jnp.zeros_like(qv)), jnp.where(first, jnp.zeros_like(qv), qv))

        def step(blocks, carry):
            chains = [(b, h) for b in range(len(blocks)) for h in range(2)]
            rows = [r for _, r, _ in blocks]
            s0 = [pl.multiple_of(kj * tk, tk) for kj, _, _ in blocks]
            kb = [k_ref[pl.ds(s, tk), :] for s in s0]
            vb = [v_ref[pl.ds(s, tk), :] for s in s0]
            visible = [causal if masked else None for _, _, masked in blocks]
            z = {c: _dot_nt(q_heads[c[1]][rows[c[0]][0]:rows[c[0]][1]], kb[c[0]]) for c in chains}
            run = [carry[0], carry[2]]
            log_own, later, run_at = {}, {}, {}
            for c in chains:
                b, h = c
                lk = _softplus_parts(z[c])[0]
                if visible[b] is not None:
                    lk = jnp.where(visible[b], lk, 0.0)
                log_own[c] = z[c] + lk
                later[c] = _dot(lk.astype(BF16), later_mat)
                run_at[c] = run[h][rows[b][0]:rows[b][1]]
                run[h] = _add_rows(run[h], rows[b], jnp.sum(lk, axis=-1, keepdims=True))
            acc = [carry[1], carry[3]]
            for c in chains:
                b, h = c
                a = jnp.exp(log_own[c] + later[c] + run_at[c])
                if visible[b] is not None:
                    a = jnp.where(visible[b], a, 0.0)
                acc[h] = _add_rows(acc[h], rows[b], _dot(a.astype(BF16), vb[b]))
            return run[0], acc[0], run[1], acc[1]

        zero1, zero128 = jnp.zeros((tq, 1), F32), jnp.zeros((tq, 128), F32)
        carry = step([(qi * kpq + 1, (tk, tq), True), (qi * kpq, (tk, tq), False), (qi * kpq, (0, tk), True)],
                     (zero1, zero128, zero1, zero128))

        def low(run):
            return jnp.max(run)

        def pair_more(c):
            return (c[0] < qi) & (jnp.maximum(low(c[1][tk:]), low(c[3][tk:])) > EXP_UNDERFLOW)

        def pair_step(c):
            last = (qi - c[0]) * kpq - 1
            return (c[0] + 1, *step([(last, (0, tq), False), (last - 1, (0, tq), False)], c[1:]))

        pairs, *carry = lax.while_loop(pair_more, pair_step, (jnp.int32(0), *carry))
        left = (qi - pairs) * kpq

        def single_more(c):
            return (c[0] < left) & (jnp.maximum(low(c[1][:tk]), low(c[3][:tk])) > EXP_UNDERFLOW)

        def single_step(c):
            return (c[0] + 1, *step([(left - 1 - c[0], (0, tk), False)], c[1:]))

        singles, *carry = lax.while_loop(single_more, single_step, (jnp.int32(0), *carry))
        steps_ref[...] = jnp.concatenate([jnp.full((4, 128), pairs, F32), jnp.full((4, 128), singles, F32)], axis=0)
        o_tot = jnp.where(first, carry[1], carry[3])
        l_tot = jnp.where(first, carry[0], carry[2])
        o_ref[...] = o_tot.astype(BF16)
        lt_ref[...] = l_tot
        zz = z_ref[...].astype(F32)
        g_ref[...] = (o_tot * (zz * _sigmoid(zz))).astype(BF16)

    blk = pl.BlockSpec((tq, 128), lambda hp, qi: (qi, hp))
    seq = pl.BlockSpec((S, 128), lambda hp, qi: (0, hp))
    return _call_with_gather(
        body, name="attn_fwd", grid=(D_MODEL // 128, S // tq),
        in_specs=[blk, seq, seq, blk],
        out_specs=[blk, blk, blk, pl.BlockSpec((None, None, 8, 128), lambda hp, qi: (hp, qi, 0, 0))],
        out_shape=[jax.ShapeDtypeStruct((S, D_MODEL), BF16)] * 2 + [jax.ShapeDtypeStruct((S, D_MODEL), F32)]
        + [jax.ShapeDtypeStruct((D_MODEL // 128, S // tq, 8, 128), F32)],
        args=(q, k, v, zgate), gather=gather)


def _ple_out_bwd(name, dx_out, e, gate, ple_g, w_out):
    S = dx_out.shape[0]
    tm = WIDE_ROW_TILE

    def body(dx_ref, e_ref, gt_ref, pg_ref, wo_ref, de_ref, dgp_ref, dxm_ref, dg_ref):
        dxo = dx_ref[...]
        ev = e_ref[...].astype(F32)
        gv = gt_ref[...].astype(F32)
        de_ref[...] = (dxo * gv).astype(BF16)
        dgp = (dxo * ev * gv * (1.0 - gv)).astype(BF16)
        dgp_ref[...] = dgp
        pg = jnp.concatenate([pg_ref[sh] for sh in range(N_CHIPS)], axis=0)
        dxm = dxo + _dot_nt(dgp, pg)
        dxm_ref[...] = dxm
        dg_ref[...] = _dot_nt(dxm.astype(BF16), wo_ref[...]).astype(BF16)

    row = pl.BlockSpec((tm, D_MODEL), lambda i: (i, 0))
    return pl.pallas_call(
        body, name=name, grid=(S // tm,),
        in_specs=[row, row, row,
                  pl.BlockSpec((N_CHIPS, 256, D_MODEL), lambda i: (0, 0, 0)),
                  pl.BlockSpec((D_MODEL, D_MODEL), lambda i: (0, 0))],
        out_specs=[row, row, row, row],
        out_shape=[jax.ShapeDtypeStruct((S, D_MODEL), BF16), jax.ShapeDtypeStruct((S, D_MODEL), BF16),
                   jax.ShapeDtypeStruct((S, D_MODEL), F32), jax.ShapeDtypeStruct((S, D_MODEL), BF16)],
        compiler_params=_params(("arbitrary",)),
    )(dx_out, e, gate, ple_g, w_out)


def _attn_bwd(q, k, v, ltot, steps, dgated, o, zgate, reduce=None):
    S = q.shape[0]
    tq, tk = ATT_Q_TILE, ATT_K_TILE
    kpq = tq // tk
    nq = S // tq

    def body(q_ref, k_ref, v_ref, lt_ref, steps_ref, dg_ref, o_ref, z_ref, dq_ref, dk_ref, dv_ref, dz_ref,
             dk_acc, dv_acc):
        qi = pl.program_id(1)

        @pl.when(qi == 0)
        def _():
            dk_acc[...] = jnp.zeros_like(dk_acc)
            dv_acc[...] = jnp.zeros_like(dv_acc)

        lane = lax.broadcasted_iota(jnp.int32, (1, 128), 1)
        ri = lax.broadcasted_iota(jnp.int32, (tk, tk), 0)
        ci = lax.broadcasted_iota(jnp.int32, (tk, tk), 1)
        later_mat = _mask_bf16(ri > ci)
        before_mat = _mask_bf16(ri < ci)
        causal = ci < ri
        zz = z_ref[...].astype(F32)
        sg = _sigmoid(zz)
        dgv = dg_ref[...].astype(F32)
        dz_ref[...] = (dgv * o_ref[...].astype(F32) * _dsilu(zz, sg)).astype(BF16)
        dob = (dgv * (zz * sg)).astype(BF16)
        ltv = lt_ref[...]
        qv = q_ref[...]
        first = lane < HEAD_DIM
        masks = (first, jnp.logical_not(first))
        q_heads = [jnp.where(hm, qv, jnp.zeros_like(qv)) for hm in masks]
        do_heads = [jnp.where(hm, dob, jnp.zeros_like(dob)) for hm in masks]
        totals = [jnp.max(jnp.where(hm, ltv, -jnp.inf), axis=-1, keepdims=True) for hm in masks]

        def step(blocks, carry):
            chains = [(b, h) for b in range(len(blocks)) for h in range(2)]
            rows = [r for _, r, _ in blocks]
            cut = lambda t, b: t[rows[b][0]:rows[b][1]]
            s0 = [pl.multiple_of(kj * tk, tk) for kj, _, _ in blocks]
            kb = [k_ref[pl.ds(s, tk), :] for s in s0]
            vb = [v_ref[pl.ds(s, tk), :] for s in s0]
            visible = [causal if masked else None for _, _, masked in blocks]
            z = {c: _dot_nt(cut(q_heads[c[1]], c[0]), kb[c[0]]) for c in chains}
            da = {c: _dot_nt(cut(do_heads[c[1]], c[0]), vb[c[0]]) for c in chains}
            run = [carry[0], carry[3]]
            log_own, beta, later, base = {}, {}, {}, {}
            for c in chains:
                b, h = c
                lk = _softplus_parts(z[c])[0]
                if visible[b] is not None:
                    lk = jnp.where(visible[b], lk, 0.0)
                log_own[c] = z[c] + lk
                beta[c] = jnp.exp(log_own[c]).astype(BF16)
                later[c] = _dot(lk.astype(BF16), later_mat)
                run[h] = _add_rows(run[h], rows[b], jnp.sum(lk, axis=-1, keepdims=True))
                base[c] = cut(totals[h] - run[h], b)
            grun = [carry[1], carry[4]]
            a_bf, g_bf, gbefore, grun_at = {}, {}, {}, {}
            for c in chains:
                b, h = c
                a = jnp.exp(log_own[c] + later[c] + base[c])
                if visible[b] is not None:
                    a = jnp.where(visible[b], a, 0.0)
                a_bf[c] = a.astype(BF16)
                g = da[c] * a
                g_bf[c] = g.astype(BF16)
                gbefore[c] = _dot(g_bf[c], before_mat)
                grun_at[c] = cut(grun[h], b)
                grun[h] = _add_rows(grun[h], rows[b], jnp.sum(g, axis=-1, keepdims=True))
            dq = [carry[2], carry[5]]
            dk_blk = [jnp.zeros((tk, 128), F32) for _ in blocks]
            dv_blk = [jnp.zeros((tk, 128), F32) for _ in blocks]
            for c in chains:
                b, h = c
                g = g_bf[c].astype(F32)
                dz = g - beta[c].astype(F32) * (g + gbefore[c] + grun_at[c])
                if visible[b] is not None:
                    dz = jnp.where(visible[b], dz, 0.0)
                dzb = dz.astype(BF16)
                dq[h] = _add_rows(dq[h], rows[b], _dot(dzb, kb[b]))
                dk_blk[b] = dk_blk[b] + _dot_tn(dzb, cut(q_heads[h], b))
                dv_blk[b] = dv_blk[b] + _dot_tn(a_bf[c], cut(do_heads[h], b))
            for b in range(len(blocks)):
                dk_acc[pl.ds(s0[b], tk), :] += dk_blk[b]
                dv_acc[pl.ds(s0[b], tk), :] += dv_blk[b]
            return run[0], grun[0], dq[0], run[1], grun[1], dq[1]

        pairs = jnp.clip(jnp.max(steps_ref[0:4, :]).astype(jnp.int32), 0, qi)
        left = (qi - pairs) * kpq
        singles = jnp.clip(jnp.max(steps_ref[4:8, :]).astype(jnp.int32), 0, left)
        zero1, zero128 = jnp.zeros((tq, 1), F32), jnp.zeros((tq, 128), F32)
        carry = lax.fori_loop(left - singles, left, lambda kj, c: step([(kj, (0, tk), False)], c),
                              (zero1, zero1, zero128, zero1, zero1, zero128))
        carry = lax.fori_loop(qi - pairs, qi,
                              lambda n, c: step([(n * kpq, (0, tq), False), (n * kpq + 1, (0, tq), False)], c), carry)
        carry = step([(qi * kpq, (0, tk), True), (qi * kpq, (tk, tq), False), (qi * kpq + 1, (tk, tq), True)], carry)
        dq_ref[...] = jnp.where(first, carry[2], carry[5]).astype(BF16)

        @pl.when(qi == nq - 1)
        def _():
            dk_ref[...] = dk_acc[...].astype(BF16)
            dv_ref[...] = dv_acc[...].astype(BF16)

    blk = pl.BlockSpec((tq, 128), lambda hp, qi: (qi, hp))
    seq = pl.BlockSpec((S, 128), lambda hp, qi: (0, hp))
    return _call_with_gather(
        body, name="attn_bwd", grid=(D_MODEL // 128, nq),
        in_specs=[blk, seq, seq, blk, pl.BlockSpec((None, None, 8, 128), lambda hp, qi: (hp, qi, 0, 0)),
                  blk, blk, blk],
        out_specs=[blk, seq, seq, blk],
        out_shape=[jax.ShapeDtypeStruct((S, D_MODEL), BF16)] * 4,
        scratch_shapes=[pltpu.VMEM((S, 128), F32), pltpu.VMEM((S, 128), F32)],
        args=(q, k, v, ltot, steps, dgated, o, zgate), reduce=reduce, vmem_mib=56)


def _rms_bwd(xv, dh_gain_sum):
    r = lax.rsqrt(jnp.mean(xv * xv, axis=-1, keepdims=True) + EPS)
    xhat = xv * r
    dx = r * (dh_gain_sum - xhat * jnp.mean(dh_gain_sum * xhat, axis=-1, keepdims=True))
    return dx, xhat


def _b_in_bwd(dq, dk, dv, dz, q_raw, k_raw, x, dx_mid, q_gain_t, k_gain_t, b_gain, kv_gain, w_in, w_kv):
    S = x.shape[0]
    tm = ROW_TILE

    def body(dq_ref, dk_ref, dv_ref, dz_ref, qr_ref, kr_ref, x_ref, dxm_ref, qg_ref, kg_ref, bg_ref, kvg_ref,
             win_ref, wkv_ref, dqz_ref, dkv_ref, dx_ref, small_ref):
        @pl.when(pl.program_id(0) == 0)
        def _():
            small_ref[...] = jnp.zeros_like(small_ref)

        bd = _head_mean_matrix()

        def head_norm_bwd(dy_ref, raw_ref, gain, scale):
            raw = raw_ref[...].astype(F32)
            rr = lax.rsqrt(_head_mean(raw * raw, bd) + EPS)
            xhat = raw * rr
            dy = dy_ref[...].astype(F32) * scale
            gdy = dy * gain
            draw = rr * (gdy - xhat * _head_mean(gdy * xhat, bd))
            return draw.astype(BF16), jnp.sum(dy * xhat, axis=0, keepdims=True)

        dqr, dqg = head_norm_bwd(dq_ref, qr_ref, qg_ref[...], SB_SCALE)
        dkr, dkg = head_norm_bwd(dk_ref, kr_ref, kg_ref[...], 1.0)
        dqz_ref[:, :D_MODEL] = dqr
        dqz_ref[:, D_MODEL:] = dz_ref[...]
        dkv_ref[:, :D_MODEL] = dkr
        dkv_ref[:, D_MODEL:] = dv_ref[...]
        dhb = jnp.zeros((tm, D_MODEL), F32)
        dhkv = jnp.zeros((tm, D_MODEL), F32)
        for sh in range(N_CHIPS):
            cols = slice(sh * 512, (sh + 1) * 512)
            dhb = dhb + _dot_nt(dqz_ref[:, cols], win_ref[sh])
            dhkv = dhkv + _dot_nt(dkv_ref[:, cols], wkv_ref[sh])
        dx, xhat = _rms_bwd(x_ref[...], dhb * bg_ref[...] + dhkv * kvg_ref[...])
        dx_ref[...] = dxm_ref[...] + dx
        small_ref[0:1, :] += dqg
        small_ref[1:2, :] += dkg
        small_ref[2:3, :] += jnp.sum(dhb * xhat, axis=0, keepdims=True)
        small_ref[3:4, :] += jnp.sum(dhkv * xhat, axis=0, keepdims=True)

    row = pl.BlockSpec((tm, D_MODEL), lambda i: (i, 0))
    wide = pl.BlockSpec((tm, 2 * D_MODEL), lambda i: (i, 0))
    vec = pl.BlockSpec((1, D_MODEL), lambda i: (0, 0))
    wsp = pl.BlockSpec((N_CHIPS, D_MODEL, 512), lambda i: (0, 0, 0))
    return pl.pallas_call(
        body, name="b_in_bwd", grid=(S // tm,),
        in_specs=[row] * 8 + [vec] * 4 + [wsp, wsp],
        out_specs=[wide, wide, row, pl.BlockSpec((8, D_MODEL), lambda i: (0, 0))],
        out_shape=[jax.ShapeDtypeStruct((S, 2 * D_MODEL), BF16), jax.ShapeDtypeStruct((S, 2 * D_MODEL), BF16),
                   jax.ShapeDtypeStruct((S, D_MODEL), F32), jax.ShapeDtypeStruct((8, D_MODEL), F32)],
        compiler_params=_params(("arbitrary",), 56),
    )(dq, dk, dv, dz, q_raw, k_raw, x, dx_mid, q_gain_t, k_gain_t, b_gain, kv_gain, w_in, w_kv)


def _a_mix_bwd(dgated, uz, pooled, wg, scale, w_in, x, dx_mid, gain, reduce=None):
    S = x.shape[0]
    tm = ROW_TILE
    n = S // tm

    def body(dg_ref, z_ref, p_ref, wg_ref, sc_ref, win_ref, x_ref, dxm_ref, gn_ref,
             duz_ref, dmr_ref, dx_ref, small_ref, halo_hi, halo_lo):
        i = pl.program_id(0)

        @pl.when(i == 0)
        def _():
            small_ref[...] = jnp.zeros_like(small_ref)
            halo_hi[...] = jnp.zeros_like(halo_hi)
            halo_lo[...] = jnp.zeros_like(halo_lo)

        first_row = (n - 1 - i) * tm
        row = lax.broadcasted_iota(jnp.int32, (tm, tm), 0)
        col = lax.broadcasted_iota(jnp.int32, (tm, tm), 1)
        d = col - row
        for g, w in enumerate(POOL_WINDOWS):
            cols = slice(g * GROUP_DIM, (g + 1) * GROUP_DIM)
            wgg = _group_weight(wg_ref, g)
            sc = sc_ref[:, cols]
            mraw = _dot(p_ref[:, cols], wgg)
            z = z_ref[:, cols]
            sg = _sigmoid(z)
            dga = dg_ref[:, cols].astype(F32)
            dm = dga * (z * sg)
            duz_ref[:, D_MODEL + g * GROUP_DIM:D_MODEL + (g + 1) * GROUP_DIM] = (
                dga * (mraw * sc) * _dsilu(z, sg)).astype(BF16)
            small_ref[0:1, cols] += jnp.sum(dm * mraw, axis=0, keepdims=True)
            dmr = (dm * sc).astype(BF16)
            dmr_ref[:, cols] = dmr
            dp = _dot_nt(dmr, wgg)
            hi, lo = _hilo(dp * _inv_count(first_row, tm, w))
            t_main = _mask_bf16((d >= 0) & (d < w))
            t_halo = _mask_bf16(d + tm < w)
            du = (_dot(t_main, hi) + _dot(t_main, lo) + _dot(t_halo, halo_hi[:, cols]) + _dot(t_halo, halo_lo[:, cols])
                  - dp)
            halo_hi[:, cols] = hi
            halo_lo[:, cols] = lo
            duz_ref[:, cols] = du.astype(BF16)
        dh = jnp.zeros((tm, D_MODEL), F32)
        for sh in range(N_CHIPS):
            dh = dh + _dot_nt(duz_ref[:, sh * 512:(sh + 1) * 512], win_ref[sh])
        dx, xhat = _rms_bwd(x_ref[...], dh * gn_ref[...])
        dx_ref[...] = dxm_ref[...] + dx
        small_ref[1:2, :] += jnp.sum(dh * xhat, axis=0, keepdims=True)

    rev = lambda i: (n - 1 - i, 0)
    row = pl.BlockSpec((tm, D_MODEL), rev)
    vec = pl.BlockSpec((1, D_MODEL), lambda i: (0, 0))
    return _call_with_gather(
        body, name="a_mix_bwd", grid=(n,),
        in_specs=[row,
                  pl.BlockSpec((tm, D_MODEL), lambda i: (n - 1 - i, 1)),
                  row,
                  pl.BlockSpec((N_CHIPS, N_GROUPS, 64, GROUP_DIM), lambda i: (0, 0, 0, 0)),
                  vec,
                  pl.BlockSpec((N_CHIPS, D_MODEL, 512), lambda i: (0, 0, 0)),
                  row, row, vec],
        out_specs=[pl.BlockSpec((tm, 2 * D_MODEL), rev), row, row,
                   pl.BlockSpec((8, D_MODEL), lambda i: (0, 0))],
        out_shape=[jax.ShapeDtypeStruct((S, 2 * D_MODEL), BF16), jax.ShapeDtypeStruct((S, D_MODEL), BF16),
                   jax.ShapeDtypeStruct((S, D_MODEL), F32), jax.ShapeDtypeStruct((8, D_MODEL), F32)],
        scratch_shapes=[pltpu.VMEM((tm, D_MODEL), BF16), pltpu.VMEM((tm, D_MODEL), BF16)],
        args=(dgated, uz, pooled, wg, scale, w_in, x, dx_mid, gain), reduce=reduce, vmem_mib=56)


def _wgrad(name, a, dy, n_shards, a_spec=None, k_dim=None):
    S, n_cols = dy.shape
    ts = WGRAD_SEQ_TILE
    k_dim = a.shape[-1] if k_dim is None else k_dim
    wn = n_cols // n_shards
    tk = min(k_dim, WGRAD_ACC_BYTES // (4 * n_cols))
    nst = S // ts

    def body(a_ref, dy_ref, out_ref, acc):
        st = pl.program_id(1)

        @pl.when(st == 0)
        def _():
            acc[...] = jnp.zeros_like(acc)

        acc[...] += _dot_tn(a_ref[...].astype(BF16), dy_ref[...].astype(BF16))

        @pl.when(st == nst - 1)
        def _():
            for sh in range(n_shards):
                out_ref[sh] = acc[:, sh * wn:(sh + 1) * wn]

    if a_spec is None:
        a_spec = pl.BlockSpec((ts, tk), lambda kt, st: (st, kt))
    return pl.pallas_call(
        body, name=name, grid=(k_dim // tk, nst),
        in_specs=[a_spec, pl.BlockSpec((ts, n_cols), lambda kt, st: (st, 0))],
        out_specs=pl.BlockSpec((n_shards, tk, wn), lambda kt, st: (0, kt, 0)),
        out_shape=jax.ShapeDtypeStruct((n_shards, k_dim, wn), F32),
        scratch_shapes=[pltpu.VMEM((tk, n_cols), F32)],
        compiler_params=_params(("parallel", "arbitrary")),
    )(a, dy)


def _wgrad_ple(name, p, layer, de):
    ts = WGRAD_SEQ_TILE
    spec = pl.BlockSpec((None, None, ts, PLE_DIM), lambda kt, st: (layer, 0, st, 0))
    return _wgrad(name, p, de, N_CHIPS, a_spec=spec, k_dim=PLE_DIM)


def _wgrad_group(pooled, dmr):
    S = pooled.shape[0]
    ts = WGRAD_SEQ_TILE
    nst = S // ts

    def body(p_ref, d_ref, out_ref, acc):
        st = pl.program_id(1)

        @pl.when(st == 0)
        def _():
            acc[...] = jnp.zeros_like(acc)

        acc[...] += _dot_tn(p_ref[...], d_ref[...])

        @pl.when(st == nst - 1)
        def _():
            for sh in range(N_CHIPS):
                out_ref[sh] = acc[sh * 64:(sh + 1) * 64, :]

    blk = pl.BlockSpec((ts, GROUP_DIM), lambda g, st: (st, g))
    return pl.pallas_call(
        body, name="wgrad_group", grid=(N_GROUPS, nst),
        in_specs=[blk, blk],
        out_specs=pl.BlockSpec((N_CHIPS, None, 64, GROUP_DIM), lambda g, st: (0, g, 0, 0)),
        out_shape=jax.ShapeDtypeStruct((N_CHIPS, N_GROUPS, 64, GROUP_DIM), F32),
        scratch_shapes=[pltpu.VMEM((GROUP_DIM, GROUP_DIM), F32)],
        compiler_params=_params(("parallel", "arbitrary")),
    )(pooled, dmr)


GATHER_AT = {
    "a_in": ("a_w_group", "a_w_out", "ple_w0", "ple_gate_w0"),
    "a_mix": ("w_kv",),
    "a_out_ple": ("b_w_in",),
    "attn_fwd": ("b_w_out", "ple_w1", "ple_gate_w1"),
}


REDUCE_AT = {
    "attn_bwd": ("b_w_out", "ple_w1", "ple_gate_w1"),
    "a_mix_bwd": ("a_w_out", "ple_w0", "ple_gate_w0"),
}


def _local_step(x, p, target, w, local=None, state=None):
    w = dict(w)

    def run(fn, host, n_out, *args, **kwargs):
        names = GATHER_AT[host] if local is not None else ()
        res = fn(*args, gather=[local[n] for n in names], **kwargs)
        w.update(zip(names, res[n_out:]))
        return res[:n_out]

    k_gain_t = jnp.tile(w["k_norm"].reshape(1, HEAD_DIM), (1, N_HEADS))
    q_gain_t = jnp.tile(w["b_q_norm"].reshape(1, HEAD_DIM), (1, N_HEADS))

    uz, h_a = run(_a_in, "a_in", 2, x, w["a_norm"], w["a_w_in"])
    wg4 = w["a_w_group"].reshape(N_CHIPS, N_GROUPS, 64, GROUP_DIM)
    wa_out = w["a_w_out"].reshape(D_MODEL, D_MODEL)
    gated_a, pooled = run(_a_mix, "a_mix", 2, uz, wg4, w["a_scale"])
    x1, x2, e_a, gate_a = run(_out_ple, "a_out_ple", 4, "a_out_ple", gated_a, x, wa_out, p, 0,
                              w["ple_w0"], w["ple_gate_w0"])
    h_kv, h_b, k_raw, q_raw, k, q, v, z_b = _b_in(
        x2, w["kv_norm"], w["b_norm"], k_gain_t, q_gain_t, w["w_kv"], w["b_w_in"])
    o, gated_b, ltot, att_steps = run(_attn_fwd, "attn_fwd", 4, q, k, v, z_b)
    wb_out = w["b_w_out"].reshape(D_MODEL, D_MODEL)
    x3, dx4, e_b, gate_b, loss_blk = _out_ple("b_out_ple", gated_b, x2, wb_out, p, 1, w["ple_w1"], w["ple_gate_w1"],
                                              target=target)

    grads, updates = {}, {}

    def hosted(fn, host, n_out, *args):
        if state is None:
            return fn(*args)
        names = REDUCE_AT[host]
        seeds = [updates.get(n[:-1] + "1") if n.startswith("ple") and n.endswith("0") else None for n in names]
        res = fn(*args, reduce=([grads.pop(n) for n in names], *[[t[n] for n in names] for t in state[:3]],
                                [state[3][n] for n in names], seeds))
        for i, n in enumerate(names):
            updates[n] = tuple(group[i] for group in res[n_out:])
        return res[:n_out]

    de_b, dgp_b, dx3, dgated_b = _ple_out_bwd("b_ple_out_bwd", dx4, e_b, gate_b, w["ple_gate_w1"], wb_out)
    grads["b_w_out"] = _wgrad("wgrad_b_out", gated_b, dx3, 1).reshape(N_CHIPS, 256, D_MODEL)
    grads["ple_w1"] = _wgrad_ple("wgrad_ple1", p, 1, de_b)
    grads["ple_gate_w1"] = _wgrad("wgrad_gate1", x3, dgp_b, 1).reshape(N_CHIPS, 256, D_MODEL)
    dq, dk, dv, dz_b = hosted(_attn_bwd, "attn_bwd", 4, q, k, v, ltot, att_steps, dgated_b, o, z_b)
    dqz, dkv, dx2, small_b = _b_in_bwd(dq, dk, dv, dz_b, q_raw, k_raw, x2, dx3, q_gain_t, k_gain_t,
                                       w["b_norm"], w["kv_norm"], w["b_w_in"], w["w_kv"])
    grads["w_kv"] = _wgrad("wgrad_kv", h_kv, dkv, N_CHIPS)
    grads["b_w_in"] = _wgrad("wgrad_b_in", h_b, dqz, N_CHIPS)
    de_a, dgp_a, dx1, dgated_a = _ple_out_bwd("a_ple_out_bwd", dx2, e_a, gate_a, w["ple_gate_w0"], wa_out)
    grads["a_w_out"] = _wgrad("wgrad_a_out", gated_a, dx1, 1).reshape(N_CHIPS, 256, D_MODEL)
    grads["ple_w0"] = _wgrad_ple("wgrad_ple0", p, 0, de_a)
    grads["ple_gate_w0"] = _wgrad("wgrad_gate0", x1, dgp_a, 1).reshape(N_CHIPS, 256, D_MODEL)
    duz, dmr, grad_x, small_a = hosted(_a_mix_bwd, "a_mix_bwd", 4, dgated_a, uz, pooled, wg4, w["a_scale"],
                                       w["a_w_in"], x, dx1, w["a_norm"])
    grads["a_w_in"] = _wgrad("wgrad_a_in", h_a, duz, N_CHIPS)
    grads["a_w_group"] = _wgrad_group(pooled, dmr).reshape(N_CHIPS, N_GROUPS * 64, GROUP_DIM)

    fold = lambda row: jnp.pad(row.reshape(N_HEADS, HEAD_DIM).sum(axis=0), (0, D_MODEL - HEAD_DIM))
    small = jnp.stack([small_a[1], small_a[0], small_b[3], small_b[2], fold(small_b[1]), fold(small_b[0]),
                       jnp.pad(loss_blk[0], (0, D_MODEL - loss_blk.shape[1])), jnp.zeros((D_MODEL,), F32)])
    return grad_x, grads, updates, small


def _mesh_place():
    x, y, c = lax.axis_index("x"), lax.axis_index("y"), lax.axis_index("c")
    other_chips = [(1 - x, y), (x, 1 - y), (1 - x, 1 - y)]
    return x, y, c, other_chips


def _gather_sems(n):
    return [pltpu.SemaphoreType.DMA((3 * n,)), pltpu.SemaphoreType.DMA((3 * n,)),
            pltpu.SemaphoreType.DMA((3 * n,)), pltpu.SemaphoreType.DMA((3 * n,)), pltpu.SemaphoreType.DMA((n,))]


def _gather_copies(srcs, outs, sems):
    send_far, recv_far, send_sib, recv_sib, local_sem = sems
    n = len(srcs)
    x, y, c, chips = _mesh_place()
    me = 2 * x + y
    sibling = (x, y, 1 - c)

    def half(k, which):
        rows = srcs[k].shape[0] // 2
        return pl.ds(pl.multiple_of(which * rows, 16), rows)

    local = [pltpu.make_async_copy(srcs[k], outs[k].at[me], local_sem.at[k]) for k in range(n)]
    far = [pltpu.make_async_remote_copy(
        src_ref=srcs[k].at[half(k, c)], dst_ref=outs[k].at[me, half(k, c)],
        send_sem=send_far.at[j * n + k], recv_sem=recv_far.at[j * n + k], device_id=(px, py, c), device_id_type=MESH)
        for j, (px, py) in enumerate(chips) for k in range(n)]

    def landed(j, k, which, from_far):
        px, py = chips[j]
        piece = outs[k].at[2 * px + py, half(k, which)]
        send, recv = (send_far, recv_far) if from_far else (send_sib, recv_sib)
        return pltpu.make_async_remote_copy(src_ref=piece, dst_ref=piece, send_sem=send.at[j * n + k],
                                            recv_sem=recv.at[j * n + k], device_id=sibling, device_id_type=MESH)

    return local, far, landed, c


def _gather_start(srcs, outs, sems):
    local, far, _, _ = _gather_copies(srcs, outs, sems)
    for cp in local + far:
        cp.start()


def _gather_pass_on(srcs, outs, sems):
    _, _, landed, c = _gather_copies(srcs, outs, sems)
    for j in range(3):
        for k in range(len(srcs)):
            landed(j, k, c, True).wait_recv()
            landed(j, k, c, False).start()


def _gather_finish(srcs, outs, sems):
    local, far, landed, c = _gather_copies(srcs, outs, sems)
    pairs = [(j, k) for j in range(3) for k in range(len(srcs))]
    for j, k in pairs:
        landed(j, k, 1 - c, False).wait_recv()
    for cp in far + [landed(j, k, c, False) for j, k in pairs]:
        cp.wait_send()
    for cp in local:
        cp.wait()


def _call_with_gather(body, *, name, grid, in_specs, out_specs, out_shape, args, gather=(), reduce=None,
                      scratch_shapes=(), vmem_mib=48):
    n_in, n_out, n_scr, n_g = len(args), len(out_shape), len(scratch_shapes), len(gather)
    n_r = len(reduce[0]) if reduce else 0
    pieces = _reduce_pieces(reduce[0], reduce[4]) if reduce else []
    reduce_args = [a for group in reduce[:4] for a in group] if reduce else []
    seeds = reduce[5] if reduce else []
    seeded = [(k, a) for k, seed in enumerate(seeds) if seed is not None for a in range(4)]
    gather_sems = _gather_sems(n_g) if n_g else []
    n_steps = 1
    for g in grid:
        n_steps *= g

    def wrapped(*refs):
        refs = list(refs)
        take = lambda count: [refs.pop(0) for _ in range(count)]
        ins, g_in, r_in = take(n_in), take(n_g), take(4 * n_r)
        take(len(seeded))
        outs, g_out, r_out = take(n_out), take(n_g), take(4 * n_r)
        scratch, sems, r_scratch = take(n_scr), take(len(gather_sems)), refs
        step = 0
        for axis, g in enumerate(grid):
            step = step * g + pl.program_id(axis)
        if n_g:
            @pl.when(step == 0)
            def _():
                _gather_start(g_in, g_out, sems)

        if n_r:
            ticks, drain = _reduce_ticks(pieces, n_r, (*r_in, *r_out, *r_scratch))
            for t, tick in enumerate(ticks[:n_steps]):
                pl.when(step == t)(tick)

        body(*ins, *outs, *scratch)
        if n_r:
            for tick in ticks[n_steps:]:
                pl.when(step == n_steps - 1)(tick)
            pl.when(step == n_steps - 1)(drain)
        if n_g:
            @pl.when(step == max(n_steps - 2, 0))
            def _():
                _gather_pass_on(g_in, g_out, sems)

            @pl.when(step == n_steps - 1)
            def _():
                _gather_finish(g_in, g_out, sems)

    hbm = pl.BlockSpec(memory_space=pltpu.HBM)
    res = pl.pallas_call(
        wrapped, name=name, grid=grid,
        in_specs=list(in_specs) + [hbm] * (n_g + 4 * n_r + len(seeded)),
        out_specs=list(out_specs) + [hbm] * (n_g + 4 * n_r),
        out_shape=list(out_shape) + [jax.ShapeDtypeStruct((N_CHIPS,) + g.shape, BF16) for g in gather]
        + ([jax.ShapeDtypeStruct(w.shape, F32) for _ in range(4) for w in reduce[1]] if reduce else []),
        input_output_aliases={n_in + n_g + 4 * n_r + i: n_out + n_g + a * n_r + k for i, (k, a) in enumerate(seeded)},
        scratch_shapes=list(scratch_shapes) + gather_sems + (_reduce_scratch() if reduce else []),
        compiler_params=_params(("arbitrary",) * len(grid), vmem_mib),
    )(*args, *gather, *reduce_args, *[seeds[k][a] for k, a in seeded])
    if not reduce:
        return res
    plain = list(res[:n_out + n_g])
    return plain + [res[n_out + n_g + i * n_r:n_out + n_g + (i + 1) * n_r] for i in range(4)]


def _allgather_weights(shards, small, casts):
    n = len(shards)
    cast_out = [(k, r0, r1) for k, (_, ranges) in enumerate(casts) for r0, r1 in ranges]
    n_c, n_co = len(casts), len(cast_out)

    def body(*refs):
        ins, small_in, cast_in = refs[:n], refs[n], refs[n + 1:n + 1 + n_c]
        refs = refs[n + 1 + n_c:]
        outs, small_out, cast_dst = refs[:n], refs[n], refs[n + 1:n + 1 + n_co]
        refs = refs[n + 1 + n_co:]
        cast, cast_buf = refs[:n], refs[n:n + n_co]
        send_far, recv_far, send_sib, recv_sib, send_small, recv_small, local_sem, cast_sem = refs[n + n_co:]
        x, y, c, chips = _mesh_place()
        me = 2 * x + y
        sibling = (x, y, 1 - c)

        def half(k, which):
            rows = ins[k].shape[0] // 2
            return pl.ds(pl.multiple_of(which * rows, 16), rows)

        local = []
        for k in range(n):
            cast[k][...] = ins[k][...].astype(BF16)
            local.append(pltpu.make_async_copy(cast[k], outs[k].at[me], local_sem.at[k]))
            local[-1].start()
        local.append(pltpu.make_async_copy(small_in, small_out.at[me], local_sem.at[n]))
        local[-1].start()

        sends = []
        for j, (px, py) in enumerate(chips):
            for k in range(n):
                cp = pltpu.make_async_remote_copy(
                    src_ref=cast[k].at[half(k, c)], dst_ref=outs[k].at[me, half(k, c)],
                    send_sem=send_far.at[j * n + k], recv_sem=recv_far.at[j * n + k],
                    device_id=(px, py, c), device_id_type=MESH)
                cp.start()
                sends.append(cp)
            cp = pltpu.make_async_remote_copy(
                src_ref=small_in, dst_ref=small_out.at[me], send_sem=send_small.at[j], recv_sem=recv_small.at[j],
                device_id=(px, py, c), device_id_type=MESH)
            cp.start()
            sends.append(cp)

        for i, (k, r0, r1) in enumerate(cast_out):
            cast_buf[i][...] = cast_in[k][r0:r1, :].astype(BF16)
            local.append(pltpu.make_async_copy(cast_buf[i], cast_dst[i], cast_sem.at[i]))
            local[-1].start()

        def landed(j, k, which, sems_s, sems_r, device):
            px, py = chips[j]
            piece = outs[k].at[2 * px + py, half(k, which)]
            return pltpu.make_async_remote_copy(
                src_ref=piece, dst_ref=piece, send_sem=sems_s.at[j * n + k], recv_sem=sems_r.at[j * n + k],
                device_id=device, device_id_type=MESH)

        for j in range(len(chips)):
            for k in range(n):
                landed(j, k, c, send_far, recv_far, sibling).wait_recv()
                cp = landed(j, k, c, send_sib, recv_sib, sibling)
                cp.start()
                sends.append(cp)
        for j, (px, py) in enumerate(chips):
            for k in range(n):
                landed(j, k, 1 - c, send_sib, recv_sib, sibling).wait_recv()
            pltpu.make_async_remote_copy(
                src_ref=small_in, dst_ref=small_out.at[2 * px + py], send_sem=send_small.at[j],
                recv_sem=recv_small.at[j], device_id=(px, py, c), device_id_type=MESH).wait_recv()
        for cp in sends:
            cp.wait_send()
        for cp in local:
            cp.wait()

    vmem = pl.BlockSpec(memory_space=pltpu.VMEM)
    hbm = pl.BlockSpec(memory_space=pltpu.HBM)
    cast_shapes = [(r1 - r0, casts[k][0].shape[1]) for k, r0, r1 in cast_out]
    res = pl.pallas_call(
        body, name="allgather_weights",
        in_specs=[vmem] * (n + 1 + n_c), out_specs=[hbm] * (n + 1 + n_co),
        out_shape=[jax.ShapeDtypeStruct((N_CHIPS,) + s.shape, BF16) for s in shards]
        + [jax.ShapeDtypeStruct((N_CHIPS,) + small.shape, F32)]
        + [jax.ShapeDtypeStruct(s, BF16) for s in cast_shapes],
        scratch_shapes=[pltpu.VMEM(s.shape, BF16) for s in shards] + [pltpu.VMEM(s, BF16) for s in cast_shapes]
        + [pltpu.SemaphoreType.DMA((3 * n,)), pltpu.SemaphoreType.DMA((3 * n,)),
           pltpu.SemaphoreType.DMA((3 * n,)), pltpu.SemaphoreType.DMA((3 * n,)),
           pltpu.SemaphoreType.DMA((3,)), pltpu.SemaphoreType.DMA((3,)),
           pltpu.SemaphoreType.DMA((n + 1,)), pltpu.SemaphoreType.DMA((n_co,))],
        compiler_params=_params(None, 40),
    )(*shards, small, *[a for a, _ in casts])
    return res[:n], res[n], res[n + 1:]


def _adamw(w, g, m, v):
    m = ADAM_B1 * m + (1.0 - ADAM_B1) * g
    v = ADAM_B2 * v + (1.0 - ADAM_B2) * (g * g)
    m_hat = m / (1.0 - ADAM_B1 ** ADAM_STEP)
    v_hat = v / (1.0 - ADAM_B2 ** ADAM_STEP)
    delta = -ADAM_LR * (m_hat / (jnp.sqrt(v_hat) + ADAM_EPS) + ADAM_WD * w)
    return delta, m, v


RS_PIECE_ROWS = 128
RS_PIECE_COLS = 512


def _reduce_adam_all(grads, ws, ms, vs, small, bases=None, seeds=None):
    n_w = len(grads)
    pieces = _reduce_pieces(grads, bases)
    lanes = [pieces[0::2], pieces[1::2]]
    n_lane = len(_reduce_scratch())
    n_small = len(_small_sum_scratch(small.shape))
    seeds = seeds or [None] * n_w
    seeded = [(k, a) for k, seed in enumerate(seeds) if seed is not None for a in range(4)]
    n_in = 4 * n_w + 1

    def body(*refs):
        refs = list(refs)
        del refs[n_in:n_in + len(seeded)]
        small_in = refs.pop(4 * n_w)
        small_out = refs.pop(8 * n_w)
        small_scratch = [refs.pop() for _ in range(n_small)][::-1]
        sends = _small_sum_start(small_in, *small_scratch)
        arrays, scratch = refs[:8 * n_w], refs[8 * n_w:]
        runs = [_reduce_ticks(lane, n_w, arrays + scratch[i * n_lane:(i + 1) * n_lane])
                for i, lane in enumerate(lanes) if lane]
        for t in range(max(len(ticks) for ticks, _ in runs)):
            for ticks, _ in runs:
                if t < len(ticks):
                    ticks[t]()
        for _, drain in runs:
            drain()
        _small_sum_finish(sends, small_scratch[0], small_out)

    hbm = pl.BlockSpec(memory_space=pltpu.HBM)
    vmem = pl.BlockSpec(memory_space=pltpu.VMEM)
    outs = pl.pallas_call(
        body, name="reduce_adam_all",
        in_specs=[hbm] * (4 * n_w) + [vmem] + [hbm] * len(seeded), out_specs=[hbm] * (4 * n_w) + [vmem],
        out_shape=[jax.ShapeDtypeStruct(w.shape, F32) for _ in range(4) for w in ws]
        + [jax.ShapeDtypeStruct(small.shape, F32)],
        input_output_aliases={n_in + i: a * n_w + k for i, (k, a) in enumerate(seeded)},
        scratch_shapes=_reduce_scratch() * len(lanes) + _small_sum_scratch(small.shape),
        compiler_params=_params(None, 48),
    )(*grads, *ws, *ms, *vs, small, *[seeds[k][a] for k, a in seeded])
    return [outs[i * n_w:(i + 1) * n_w] for i in range(4)], outs[4 * n_w]


def _reduce_pieces(grads, bases=None):
    pieces = []
    for k, g in enumerate(grads):
        hr, cols = g.shape[1] // 2, g.shape[2]
        pr, pc = min(hr, RS_PIECE_ROWS), min(cols, RS_PIECE_COLS)
        base = bases[k] if bases else 0
        pieces += [(k, ro, hr, co, pr, pc, base) for ro in range(0, hr, pr) for co in range(0, cols, pc)]
    return pieces


def _reduce_scratch():
    P, C = RS_PIECE_ROWS, RS_PIECE_COLS
    return [
        pltpu.VMEM((3, N_CHIPS, P, C), F32), pltpu.VMEM((3, N_CHIPS, P, C), F32),
        pltpu.VMEM((2, N_CHIPS, P, C), BF16), pltpu.VMEM((2, N_CHIPS, P, C), BF16),
        pltpu.VMEM((2, N_CHIPS, P, C), F32),
        pltpu.VMEM((2, 3, P, C), BF16), pltpu.VMEM((2, 3, P, C), BF16),
        pltpu.VMEM((2, 2, P, C), F32),
        pltpu.VMEM((2, 3, 2, P, C), F32), pltpu.VMEM((2, 4, 2, P, C), F32),
        pltpu.SemaphoreType.DMA((3, 2)), pltpu.SemaphoreType.DMA((2, 3, 2)),
        pltpu.SemaphoreType.DMA((2,)), pltpu.SemaphoreType.DMA((2,)),
        pltpu.SemaphoreType.DMA((2, 3)), pltpu.SemaphoreType.DMA((2, 3)),
        pltpu.SemaphoreType.DMA((2,)), pltpu.SemaphoreType.DMA((2,)),
        pltpu.SemaphoreType.DMA((2, 4, 2))]


def _reduce_ticks(pieces, n_w, refs):
    n = len(pieces)

    def build(*refs):
        g_in, w_in, m_in, v_in = (refs[i * n_w:(i + 1) * n_w] for i in range(4))
        g_out, d_out, m_out, v_out = (refs[(4 + i) * n_w:(5 + i) * n_w] for i in range(4))
        (gm, go, sb1, rb1, part, sb2, rb2, fin, wmv, outs,
         ld_sem, wmv_sem, s1_send, s1_recv, s2_send, s2_recv, s3_send, s3_recv, out_sem) = refs[8 * n_w:]
        x, y, c, chips = _mesh_place()
        me = 2 * x + y
        sibling = (x, y, 1 - c)

        def at_hbm(i, which, in_shard):
            _, ro, hr, co, pr, pc, base = pieces[i]
            half = c if which == 0 else 1 - c
            return pl.ds(pl.multiple_of((base if in_shard else 0) + half * hr + ro, 64), pr), pl.ds(co, pc)

        def win(i):
            return pl.ds(0, pieces[i][4]), pl.ds(0, pieces[i][5])

        every = slice(None)

        def loads(i):
            k, s = pieces[i][0], i % 3
            return [pltpu.make_async_copy(g_in[k].at[(every,) + at_hbm(i, h, False)], buf.at[(s, every) + win(i)],
                                          ld_sem.at[s, h])
                    for h, buf in enumerate((gm, go))]

        def wmv_loads(i):
            k, s = pieces[i][0], i % 2
            return [pltpu.make_async_copy(src[k].at[at_hbm(i, h, True)], wmv.at[(s, a, h) + win(i)], wmv_sem.at[s, a, h])
                    for a, src in enumerate((w_in, m_in, v_in)) for h in range(2)]

        def stores(i):
            k, s = pieces[i][0], i % 2
            return [pltpu.make_async_copy(outs.at[(s, a, h) + win(i)], dst[k].at[at_hbm(i, h, True)], out_sem.at[s, a, h])
                    for a, dst in enumerate((g_out, d_out, m_out, v_out)) for h in range(2)]

        def swap1(i):
            s = i % 2
            return pltpu.make_async_remote_copy(
                src_ref=sb1.at[(s, every) + win(i)], dst_ref=rb1.at[(s, every) + win(i)],
                send_sem=s1_send.at[s], recv_sem=s1_recv.at[s], device_id=sibling, device_id_type=MESH)

        def far2(i, j):
            s = i % 2
            px, py = chips[j]
            return pltpu.make_async_remote_copy(
                src_ref=sb2.at[(s, j) + win(i)], dst_ref=rb2.at[(s, j) + win(i)],
                send_sem=s2_send.at[s, j], recv_sem=s2_recv.at[s, j], device_id=(px, py, c), device_id_type=MESH)

        def swap3(i):
            s = i % 2
            return pltpu.make_async_remote_copy(
                src_ref=fin.at[(s, 0) + win(i)], dst_ref=fin.at[(s, 1) + win(i)],
                send_sem=s3_send.at[s], recv_sem=s3_recv.at[s], device_id=sibling, device_id_type=MESH)

        def stage0(i):
            for cp in loads(i):
                cp.start()

        def stage1(i):
            s, s3 = i % 2, i % 3
            for cp in loads(i):
                cp.wait()
            sb1[(s, every) + win(i)] = go[(s3, every) + win(i)].astype(BF16)
            swap1(i).start()

        def stage2(i):
            s, s3 = i % 2, i % 3
            swap1(i).wait()
            part[(s, every) + win(i)] = gm[(s3, every) + win(i)] + rb1[(s, every) + win(i)].astype(F32)
            for j, (px, py) in enumerate(chips):
                sb2[(s, j) + win(i)] = part[(s, 2 * px + py) + win(i)].astype(BF16)
                far2(i, j).start()

        def stage3(i):
            s = i % 2
            total = part[(s, me) + win(i)]
            for j in range(3):
                far2(i, j).wait()
                total = total + rb2[(s, j) + win(i)].astype(F32)
            fin[(s, 0) + win(i)] = total
            swap3(i).start()
            for cp in wmv_loads(i):
                cp.start()

        def stage4(i):
            s = i % 2
            if i >= 2:
                for cp in stores(i - 2):
                    cp.wait()
            swap3(i).wait()
            for cp in wmv_loads(i):
                cp.wait()
            both = (every,) + win(i)
            g = fin[(s,) + both]
            delta, m_new, v_new = _adamw(wmv[(s, 0) + both], g, wmv[(s, 1) + both], wmv[(s, 2) + both])
            outs[(s, 0) + both] = g
            outs[(s, 1) + both] = delta
            outs[(s, 2) + both] = m_new
            outs[(s, 3) + both] = v_new
            for cp in stores(i):
                cp.start()

        stages = (stage0, stage1, stage2, stage3, stage4)

        def tick(t):
            for age in reversed(range(len(stages))):
                if 0 <= t - age < n:
                    stages[age](t - age)

        def drain():
            for i in range(max(0, n - 2), n):
                for cp in stores(i):
                    cp.wait()

        return [functools.partial(tick, t) for t in range(n + len(stages) - 1)], drain

    return build(*refs)


N_DEVICES = 8


def _small_sum_scratch(shape):
    return [pltpu.VMEM((N_DEVICES,) + shape, F32),
            pltpu.SemaphoreType.DMA((N_DEVICES - 1,)), pltpu.SemaphoreType.DMA((N_DEVICES - 1,))]


def _small_sum_start(part_ref, buf, send_sem, recv_sem):
    x, y, c, _ = _mesh_place()
    me = 4 * x + 2 * y + c
    buf[me] = part_ref[...]
    sends = []
    for k in range(1, N_DEVICES):
        peer = ((1 - x) if k & 4 else x, (1 - y) if k & 2 else y, (1 - c) if k & 1 else c)
        cp = pltpu.make_async_remote_copy(src_ref=part_ref, dst_ref=buf.at[me], send_sem=send_sem.at[k - 1],
                                          recv_sem=recv_sem.at[k - 1], device_id=peer, device_id_type=MESH)
        cp.start()
        sends.append(cp)
    return sends


def _small_sum_finish(sends, buf, out_ref):
    for cp in sends:
        cp.wait_recv()
    total = buf[0]
    for s in range(1, N_DEVICES):
        total = total + buf[s]
    out_ref[...] = total
    for cp in sends:
        cp.wait_send()


def _adam_small(w, g, m, v):
    def body(w_ref, g_ref, m_ref, v_ref, d_ref, mo_ref, vo_ref):
        delta, m_new, v_new = _adamw(w_ref[...], g_ref[...], m_ref[...], v_ref[...])
        d_ref[...] = delta
        mo_ref[...] = m_new
        vo_ref[...] = v_new

    vmem = pl.BlockSpec(memory_space=pltpu.VMEM)
    return pl.pallas_call(
        body, name="adam_small", in_specs=[vmem] * 4, out_specs=[vmem] * 3,
        out_shape=[jax.ShapeDtypeStruct(w.shape, F32)] * 3,
    )(w, g, m, v)


BIG = ("a_w_in", "a_w_group", "a_w_out", "w_kv", "b_w_in", "b_w_out", "ple_w", "ple_gate_w")
SMALL = ("a_norm", "a_scale", "kv_norm", "b_norm", "k_norm", "b_q_norm")
SMALL_SHARDED = ("a_norm", "a_scale")
WEIGHTS = ("a_norm", "a_w_in", "a_w_group", "a_scale", "a_w_out", "kv_norm", "w_kv", "k_norm", "b_norm", "b_w_in",
           "b_q_norm", "b_w_out", "ple_w", "ple_gate_w")


def _as_matrix(a):
    return a.reshape(-1, a.shape[-1])


def _pack_small(arrs):
    rows = [jnp.pad(a.reshape(-1), (0, D_MODEL - a.size)) for a in arrs]
    rows += [jnp.zeros((D_MODEL,), F32)] * (8 - len(rows))
    return jnp.stack(rows)


def kernel(x, p, a_norm, a_w_in, a_w_group, a_scale, a_w_out, kv_norm, w_kv, k_norm, b_norm, b_w_in, b_q_norm, b_w_out, ple_w, ple_gate_w, loss_target, m_a_norm, m_a_w_in, m_a_w_group, m_a_scale, m_a_w_out, m_kv_norm, m_w_kv, m_k_norm, m_b_norm, m_b_w_in, m_b_q_norm, m_b_w_out, m_ple_w, m_ple_gate_w, v_a_norm, v_a_w_in, v_a_w_group, v_a_scale, v_a_w_out, v_kv_norm, v_w_kv, v_k_norm, v_b_norm, v_b_w_in, v_b_q_norm, v_b_w_out, v_ple_w, v_ple_gate_w):
    wts = dict(a_norm=a_norm, a_w_in=a_w_in, a_w_group=a_w_group, a_scale=a_scale, a_w_out=a_w_out, kv_norm=kv_norm,
               w_kv=w_kv, k_norm=k_norm, b_norm=b_norm, b_w_in=b_w_in, b_q_norm=b_q_norm, b_w_out=b_w_out,
               ple_w=ple_w, ple_gate_w=ple_gate_w)
    mom = dict(a_norm=m_a_norm, a_w_in=m_a_w_in, a_w_group=m_a_w_group, a_scale=m_a_scale, a_w_out=m_a_w_out,
               kv_norm=m_kv_norm, w_kv=m_w_kv, k_norm=m_k_norm, b_norm=m_b_norm, b_w_in=m_b_w_in,
               b_q_norm=m_b_q_norm, b_w_out=m_b_w_out, ple_w=m_ple_w, ple_gate_w=m_ple_gate_w)
    var = dict(a_norm=v_a_norm, a_w_in=v_a_w_in, a_w_group=v_a_w_group, a_scale=v_a_scale, a_w_out=v_a_w_out,
               kv_norm=v_kv_norm, w_kv=v_w_kv, k_norm=v_k_norm, b_norm=v_b_norm, b_w_in=v_b_w_in,
               b_q_norm=v_b_q_norm, b_w_out=v_b_w_out, ple_w=v_ple_w, ple_gate_w=v_ple_gate_w)
    S = x.shape[1]
    chip = 2 * lax.axis_index("x") + lax.axis_index("y")

    sharded_small = jnp.concatenate([a_norm.reshape(1, 256), a_scale.reshape(1, 256), jnp.zeros((6, 256), F32)], axis=0)
    later = ("a_w_group", "a_w_out", "w_kv", "b_w_in", "b_w_out", "ple_w", "ple_gate_w")
    (a_w_in_full,), small_full, copies = _allgather_weights(
        [_as_matrix(a_w_in)], sharded_small,
        [(_as_matrix(wts[n]), [(0, 256), (256, 512)] if n.startswith("ple") else [(0, _as_matrix(wts[n]).shape[0])])
         for n in later])
    local = dict(zip(("a_w_group", "a_w_out", "w_kv", "b_w_in", "b_w_out", "ple_w0", "ple_w1", "ple_gate_w0",
                      "ple_gate_w1"), copies))
    full = dict(a_w_in=a_w_in_full,
                a_norm=small_full[:, 0, :].reshape(1, D_MODEL), a_scale=small_full[:, 1, :].reshape(1, D_MODEL),
                kv_norm=kv_norm.reshape(1, D_MODEL), b_norm=b_norm.reshape(1, D_MODEL), k_norm=k_norm, b_q_norm=b_q_norm)

    def shards(t):
        out = {}
        for n in BIG:
            for entry in ((n + "0", n + "1") if n.startswith("ple") else (n,)):
                out[entry] = _as_matrix(t[n])
        return out

    base = {n: 256 if n.startswith("ple") and n.endswith("1") else 0 for n in shards(wts)}
    state = (shards(wts), shards(mom), shards(var), base)
    grad_x, grads, updates, small_part = _local_step(x.reshape(S, D_MODEL), p, loss_target.reshape(S, D_MODEL),
                                                     full, local, state)

    names = sorted(grads)
    reduced, small_sum = _reduce_adam_all(
        [grads[n] for n in names], *[[t[n] for n in names] for t in state[:3]], small_part,
        bases=[base[n] for n in names], seeds=[updates.get(n[:-1] + "1") if n.startswith("ple") else None for n in names])
    for i, n in enumerate(names):
        updates[n] = tuple(group[i] for group in reduced)
    out_g, out_d, out_m, out_v = {}, {}, {}, {}
    for n in BIG:
        for i, out in enumerate((out_g, out_d, out_m, out_v)):
            out[n] = updates[n + "0" if n.startswith("ple") else n][i].reshape(wts[n].shape)

    loss = small_sum[len(SMALL), 0]
    small_rows = []
    for i, n in enumerate(SMALL):
        row = small_sum[i]
        if n in SMALL_SHARDED:
            row = lax.dynamic_slice(row, (chip * 256,), (256,))
        else:
            row = row[:wts[n].size]
        small_rows.append(row)
    g_small = _pack_small(small_rows)
    d_small, m_small, v_small = _adam_small(_pack_small([wts[n] for n in SMALL]), g_small,
                                            _pack_small([mom[n] for n in SMALL]), _pack_small([var[n] for n in SMALL]))
    for i, n in enumerate(SMALL):
        shape, size = wts[n].shape, wts[n].size
        out_g[n], out_d[n], out_m[n], out_v[n] = (t[i, :size].reshape(shape) for t in (g_small, d_small, m_small, v_small))

    return (loss, grad_x.reshape(1, S, D_MODEL), *[out_g[n] for n in WEIGHTS], *[out_d[n] for n in WEIGHTS],
            *[out_m[n] for n in WEIGHTS], *[out_v[n] for n in WEIGHTS])
```

```python
import functools

import jax
import jax.numpy as jnp
from jax import lax
from jax.experimental import pallas as pl
from jax.experimental.pallas import tpu as pltpu

F32 = jnp.float32
BF16 = jnp.bfloat16
MESH = pl.DeviceIdType.MESH

D_MODEL = 1024
N_HEADS = 16
HEAD_DIM = 64
PLE_DIM = 256
N_GROUPS = 4
GROUP_DIM = 256
POOL_WINDOWS = (2, 4, 8, 16)
N_CHIPS = 4
EPS = 1e-6
SB_SCALE = HEAD_DIM ** -0.5

ADAM_LR = 0.001
ADAM_B1 = 0.9
ADAM_B2 = 0.999
ADAM_EPS = 1e-08
ADAM_WD = 0.01
ADAM_STEP = 10

ROW_TILE = 256
WIDE_ROW_TILE = 512
EXP_UNDERFLOW = -104.0
ATT_Q_TILE = 512
ATT_K_TILE = 256
WGRAD_SEQ_TILE = 1024
WGRAD_ACC_BYTES = 4 * 1024 * 1024
MIB = 1024 * 1024


def _params(semantics=None, vmem_mib=32):
    return pltpu.CompilerParams(dimension_semantics=semantics, vmem_limit_bytes=vmem_mib * MIB)


def _dot(a, b):
    return jnp.dot(a, b, preferred_element_type=F32)


def _dot_nt(a, b):
    return lax.dot_general(a, b, (((1,), (1,)), ((), ())), preferred_element_type=F32)


def _dot_tn(a, b):
    return lax.dot_general(a, b, (((0,), (0,)), ((), ())), preferred_element_type=F32)


def _hilo(x):
    hi = x.astype(BF16)
    lo = (x - hi.astype(F32)).astype(BF16)
    return hi, lo


def _dot_hilo(x, w):
    hi, lo = _hilo(x)
    return _dot(hi, w) + _dot(lo, w)


def _sigmoid(z):
    return jax.nn.sigmoid(z)


def _dsilu(z, sg):
    return sg * (1.0 + z * (1.0 - sg))


def _mask_bf16(cond):
    return jnp.where(cond, 1.0, 0.0).astype(BF16)


def _head_mean_matrix():
    r = lax.broadcasted_iota(jnp.int32, (256, 256), 0) // HEAD_DIM
    c = lax.broadcasted_iota(jnp.int32, (256, 256), 1) // HEAD_DIM
    return _mask_bf16(r == c)


def _head_mean(x, bd):
    parts = []
    for s in range(x.shape[1] // 256):
        parts.append(_dot_hilo(x[:, s * 256:(s + 1) * 256], bd))
    out = parts[0] if len(parts) == 1 else jnp.concatenate(parts, axis=1)
    return out * (1.0 / HEAD_DIM)


def _a_in(x, gain, w_sh, gather=()):
    S = x.shape[0]
    tm = 512
    nsh, _, wn = w_sh.shape

    def body(x_ref, g_ref, w_ref, uz_ref, h_ref):
        @pl.when(pl.program_id(1) == 0)
        def _():
            xv = x_ref[...]
            r = lax.rsqrt(jnp.mean(xv * xv, axis=-1, keepdims=True) + EPS)
            h_ref[...] = (xv * r * g_ref[...]).astype(BF16)

        uz_ref[...] = _dot(h_ref[...], w_ref[0])

    return _call_with_gather(
        body, name="a_in", grid=(S // tm, nsh),
        in_specs=[pl.BlockSpec((tm, D_MODEL), lambda i, j: (i, 0)),
                  pl.BlockSpec((1, D_MODEL), lambda i, j: (0, 0)),
                  pl.BlockSpec((1, D_MODEL, wn), lambda i, j: (j, 0, 0))],
        out_specs=[pl.BlockSpec((tm, wn), lambda i, j: (i, j)),
                   pl.BlockSpec((tm, D_MODEL), lambda i, j: (i, 0))],
        out_shape=[jax.ShapeDtypeStruct((S, nsh * wn), F32),
                   jax.ShapeDtypeStruct((S, D_MODEL), BF16)],
        args=(x, gain, w_sh), gather=gather)


def _inv_count(first_row, rows, w):
    t1 = first_row + 1 + lax.broadcasted_iota(jnp.int32, (rows, 1), 0)
    return 1.0 / jnp.minimum(t1, w).astype(F32)


def _group_weight(wg_ref, g):
    return jnp.concatenate([wg_ref[sh, g] for sh in range(N_CHIPS)], axis=0)


def _a_mix(uz, wg, scale, gather=()):
    S = uz.shape[0]
    tm = ROW_TILE

    def body(u_ref, up_ref, z_ref, wg_ref, sc_ref, ga_ref, p_ref):
        i = pl.program_id(0)
        row = lax.broadcasted_iota(jnp.int32, (tm, tm), 0)
        col = lax.broadcasted_iota(jnp.int32, (tm, tm), 1)
        d = row - col
        for g, w in enumerate(POOL_WINDOWS):
            cols = slice(g * GROUP_DIM, (g + 1) * GROUP_DIM)
            t_main = _mask_bf16((d >= 0) & (d < w))
            t_halo = _mask_bf16(d + tm < w)
            u = u_ref[:, cols]
            up = jnp.where(i > 0, up_ref[:, cols], 0.0)
            hi, lo = _hilo(u)
            hip, lop = _hilo(up)
            wsum = _dot(t_main, hi) + _dot(t_main, lo) + _dot(t_halo, hip) + _dot(t_halo, lop)
            pooled = (wsum * _inv_count(i * tm, tm, w) - u).astype(BF16)
            p_ref[:, cols] = pooled
            mraw = _dot(pooled, _group_weight(wg_ref, g))
            z = z_ref[:, cols]
            ga_ref[:, cols] = (mraw * sc_ref[:, cols] * (z * _sigmoid(z))).astype(BF16)

    return _call_with_gather(
        body, name="a_mix", grid=(S // tm,),
        in_specs=[pl.BlockSpec((tm, D_MODEL), lambda i: (i, 0)),
                  pl.BlockSpec((tm, D_MODEL), lambda i: (jnp.maximum(i - 1, 0), 0)),
                  pl.BlockSpec((tm, D_MODEL), lambda i: (i, 1)),
                  pl.BlockSpec((N_CHIPS, N_GROUPS, 64, GROUP_DIM), lambda i: (0, 0, 0, 0)),
                  pl.BlockSpec((1, D_MODEL), lambda i: (0, 0))],
        out_specs=[pl.BlockSpec((tm, D_MODEL), lambda i: (i, 0)),
                   pl.BlockSpec((tm, D_MODEL), lambda i: (i, 0))],
        out_shape=[jax.ShapeDtypeStruct((S, D_MODEL), BF16),
                   jax.ShapeDtypeStruct((S, D_MODEL), BF16)],
        args=(uz, uz, uz, wg, scale), gather=gather)


def _out_ple(name, gated, x_in, w_out, p, layer, ple_w, ple_g, target=None, gather=()):
    S = x_in.shape[0]
    tm = WIDE_ROW_TILE
    with_loss = target is not None

    def body(*refs):
        if with_loss:
            g_ref, x_ref, wo_ref, p_ref, pw_ref, pg_ref, t_ref, xm_ref, dx_ref, e_ref, gt_ref, loss_ref = refs
        else:
            g_ref, x_ref, wo_ref, p_ref, pw_ref, pg_ref, xm_ref, xo_ref, e_ref, gt_ref = refs
        xm = x_ref[...] + _dot(g_ref[...], wo_ref[...])
        xm_ref[...] = xm
        pb = p_ref[...].astype(BF16)
        e = jnp.concatenate([_dot(pb, pw_ref[sh]) for sh in range(N_CHIPS)], axis=1)
        pg = jnp.concatenate([pg_ref[sh] for sh in range(N_CHIPS)], axis=0)
        gate = _sigmoid(_dot(xm.astype(BF16), pg))
        e_ref[...] = e.astype(BF16)
        gt_ref[...] = gate.astype(BF16)
        xo = xm + e * gate
        if with_loss:
            diff = xo - t_ref[...]
            dx_ref[...] = diff * (1.0 / D_MODEL)

            @pl.when(pl.program_id(0) == 0)
            def _():
                loss_ref[...] = jnp.zeros_like(loss_ref)

            loss_ref[...] += jnp.sum(diff * diff) * (0.5 / D_MODEL)
        else:
            xo_ref[...] = xo

    row = pl.BlockSpec((tm, D_MODEL), lambda i: (i, 0))
    in_specs = [row, row,
                pl.BlockSpec((D_MODEL, D_MODEL), lambda i: (0, 0)),
                pl.BlockSpec((None, None, tm, PLE_DIM), lambda i: (layer, 0, i, 0)),
                pl.BlockSpec((N_CHIPS, PLE_DIM, 256), lambda i: (0, 0, 0)),
                pl.BlockSpec((N_CHIPS, 256, D_MODEL), lambda i: (0, 0, 0))]
    args = [gated, x_in, w_out, p, ple_w, ple_g]
    out_specs = [row, row, row, row]
    out_shape = [jax.ShapeDtypeStruct((S, D_MODEL), F32), jax.ShapeDtypeStruct((S, D_MODEL), F32),
                 jax.ShapeDtypeStruct((S, D_MODEL), BF16), jax.ShapeDtypeStruct((S, D_MODEL), BF16)]
    if with_loss:
        in_specs.append(row)
        args.append(target)
        out_specs.append(pl.BlockSpec((8, 128), lambda i: (0, 0)))
        out_shape.append(jax.ShapeDtypeStruct((8, 128), F32))
    return _call_with_gather(body, name=name, grid=(S // tm,), in_specs=in_specs, out_specs=out_specs,
                             out_shape=out_shape, args=args, gather=gather)


def _b_in(x, kv_gain, b_gain, k_gain_t, q_gain_t, w_kv, w_in, gather=()):
    S = x.shape[0]
    tm = ROW_TILE

    def body(x_ref, kvg_ref, bg_ref, kg_ref, qg_ref, wkv_ref, win_ref,
             hkv_ref, hb_ref, kraw_ref, qraw_ref, k_ref, q_ref, v_ref, z_ref):
        xv = x_ref[...]
        y = xv * lax.rsqrt(jnp.mean(xv * xv, axis=-1, keepdims=True) + EPS)
        hkv = (y * kvg_ref[...]).astype(BF16)
        hb = (y * bg_ref[...]).astype(BF16)
        hkv_ref[...] = hkv
        hb_ref[...] = hb
        bd = _head_mean_matrix()

        def head_norm(raw, gain):
            rr = lax.rsqrt(_head_mean(raw * raw, bd) + EPS)
            return raw * rr * gain

        for sh in range(N_CHIPS):
            kvc = _dot(hkv, wkv_ref[sh])
            qzc = _dot(hb, win_ref[sh])
            cols = slice((sh % 2) * 512, (sh % 2) * 512 + 512)
            if sh < 2:
                kraw_ref[:, cols] = kvc.astype(BF16)
                qraw_ref[:, cols] = qzc.astype(BF16)
                k_ref[:, cols] = head_norm(kvc, kg_ref[:, cols]).astype(BF16)
                q_ref[:, cols] = (head_norm(qzc, qg_ref[:, cols]) * SB_SCALE).astype(BF16)
            else:
                v_ref[:, cols] = kvc.astype(BF16)
                z_ref[:, cols] = qzc.astype(BF16)

    row = pl.BlockSpec((tm, D_MODEL), lambda i: (i, 0))
    vec = pl.BlockSpec((1, D_MODEL), lambda i: (0, 0))
    wsp = pl.BlockSpec((N_CHIPS, D_MODEL, 512), lambda i: (0, 0, 0))
    return _call_with_gather(
        body, name="b_in", grid=(S // tm,),
        in_specs=[row, vec, vec, vec, vec, wsp, wsp],
        out_specs=[row] * 8,
        out_shape=[jax.ShapeDtypeStruct((S, D_MODEL), BF16)] * 8,
        args=(x, kv_gain, b_gain, k_gain_t, q_gain_t, w_kv, w_in), gather=gather)


def _softplus_parts(z):
    e = jnp.exp(-jnp.abs(z))
    return -(jnp.maximum(z, 0.0) + jnp.log(1.0 + e)), e


def _add_rows(total, rows, update):
    lo, hi = rows
    parts = ([total[:lo]] if lo else []) + [total[lo:hi] + update] + ([total[hi:]] if hi < total.shape[0] else [])
    return parts[0] if len(parts) == 1 else jnp.concatenate(parts, axis=0)


def _attn_fwd(q, k, v, zgate, gather=()):
    S = q.shape[0]
    tq, tk = ATT_Q_TILE, ATT_K_TILE
    kpq = tq // tk
    assert kpq == 2

    def body(q_ref, k_ref, v_ref, z_ref, o_ref, g_ref, lt_ref, steps_ref):
        qi = pl.program_id(1)
        lane = lax.broadcasted_iota(jnp.int32, (1, 128), 1)
        ri = lax.broadcasted_iota(jnp.int32, (tk, tk), 0)
        ci = lax.broadcasted_iota(jnp.int32, (tk, tk), 1)
        later_mat = _mask_bf16(ri > ci)
        causal = ci < ri
        qv = q_ref[...]
        first = lane < HEAD_DIM
        q_heads = (jnp.where(first, qv, jnp.zeros_like(qv)), jnp.where(first, jnp.zeros_like(qv), qv))

        def step(blocks, carry):
            chains = [(b, h) for b in range(len(blocks)) for h in range(2)]
            rows = [r for _, r, _ in blocks]
            s0 = [pl.multiple_of(kj * tk, tk) for kj, _, _ in blocks]
            kb = [k_ref[pl.ds(s, tk), :] for s in s0]
            vb = [v_ref[pl.ds(s, tk), :] for s in s0]
            visible = [causal if masked else None for _, _, masked in blocks]
            z = {c: _dot_nt(q_heads[c[1]][rows[c[0]][0]:rows[c[0]][1]], kb[c[0]]) for c in chains}
            run = [carry[0], carry[2]]
            log_own, later, run_at = {}, {}, {}
            for c in chains:
                b, h = c
                lk = _softplus_parts(z[c])[0]
                if visible[b] is not None:
                    lk = jnp.where(visible[b], lk, 0.0)
                log_own[c] = z[c] + lk
                later[c] = _dot(lk.astype(BF16), later_mat)
                run_at[c] = run[h][rows[b][0]:rows[b][1]]
                run[h] = _add_rows(run[h], rows[b], jnp.sum(lk, axis=-1, keepdims=True))
            acc = [carry[1], carry[3]]
            for c in chains:
                b, h = c
                a = jnp.exp(log_own[c] + later[c] + run_at[c])
                if visible[b] is not None:
                    a = jnp.where(visible[b], a, 0.0)
                acc[h] = _add_rows(acc[h], rows[b], _dot(a.astype(BF16), vb[b]))
            return run[0], acc[0], run[1], acc[1]

        zero1, zero128 = jnp.zeros((tq, 1), F32), jnp.zeros((tq, 128), F32)
        carry = step([(qi * kpq + 1, (tk, tq), True), (qi * kpq, (tk, tq), False), (qi * kpq, (0, tk), True)],
                     (zero1, zero128, zero1, zero128))

        def low(run):
            return jnp.max(run)

        def pair_more(c):
            return (c[0] < qi) & (jnp.maximum(low(c[1][tk:]), low(c[3][tk:])) > EXP_UNDERFLOW)

        def pair_step(c):
            last = (qi - c[0]) * kpq - 1
            return (c[0] + 1, *step([(last, (0, tq), False), (last - 1, (0, tq), False)], c[1:]))

        pairs, *carry = lax.while_loop(pair_more, pair_step, (jnp.int32(0), *carry))
        left = (qi - pairs) * kpq

        def single_more(c):
            return (c[0] < left) & (jnp.maximum(low(c[1][:tk]), low(c[3][:tk])) > EXP_UNDERFLOW)

        def single_step(c):
            return (c[0] + 1, *step([(left - 1 - c[0], (0, tk), False)], c[1:]))

        singles, *carry = lax.while_loop(single_more, single_step, (jnp.int32(0), *carry))
        steps_ref[...] = jnp.concatenate([jnp.full((4, 128), pairs, F32), jnp.full((4, 128), singles, F32)], axis=0)
        o_tot = jnp.where(first, carry[1], carry[3])
        l_tot = jnp.where(first, carry[0], carry[2])
        o_ref[...] = o_tot.astype(BF16)
        lt_ref[...] = l_tot
        zz = z_ref[...].astype(F32)
        g_ref[...] = (o_tot * (zz * _sigmoid(zz))).astype(BF16)

    blk = pl.BlockSpec((tq, 128), lambda hp, qi: (qi, hp))
    seq = pl.BlockSpec((S, 128), lambda hp, qi: (0, hp))
    return _call_with_gather(
        body, name="attn_fwd", grid=(D_MODEL // 128, S // tq),
        in_specs=[blk, seq, seq, blk],
        out_specs=[blk, blk, blk, pl.BlockSpec((None, None, 8, 128), lambda hp, qi: (hp, qi, 0, 0))],
        out_shape=[jax.ShapeDtypeStruct((S, D_MODEL), BF16)] * 2 + [jax.ShapeDtypeStruct((S, D_MODEL), F32)]
        + [jax.ShapeDtypeStruct((D_MODEL // 128, S // tq, 8, 128), F32)],
        args=(q, k, v, zgate), gather=gather)


def _ple_out_bwd(name, dx_out, e, gate, ple_g, w_out):
    S = dx_out.shape[0]
    tm = WIDE_ROW_TILE

    def body(dx_ref, e_ref, gt_ref, pg_ref, wo_ref, de_ref, dgp_ref, dxm_ref, dg_ref):
        dxo = dx_ref[...]
        ev = e_ref[...].astype(F32)
        gv = gt_ref[...].astype(F32)
        de_ref[...] = (dxo * gv).astype(BF16)
        dgp = (dxo * ev * gv * (1.0 - gv)).astype(BF16)
        dgp_ref[...] = dgp
        pg = jnp.concatenate([pg_ref[sh] for sh in range(N_CHIPS)], axis=0)
        dxm = dxo + _dot_nt(dgp, pg)
        dxm_ref[...] = dxm
        dg_ref[...] = _dot_nt(dxm.astype(BF16), wo_ref[...]).astype(BF16)

    row = pl.BlockSpec((tm, D_MODEL), lambda i: (i, 0))
    return pl.pallas_call(
        body, name=name, grid=(S // tm,),
        in_specs=[row, row, row,
                  pl.BlockSpec((N_CHIPS, 256, D_MODEL), lambda i: (0, 0, 0)),
                  pl.BlockSpec((D_MODEL, D_MODEL), lambda i: (0, 0))],
        out_specs=[row, row, row, row],
        out_shape=[jax.ShapeDtypeStruct((S, D_MODEL), BF16), jax.ShapeDtypeStruct((S, D_MODEL), BF16),
                   jax.ShapeDtypeStruct((S, D_MODEL), F32), jax.ShapeDtypeStruct((S, D_MODEL), BF16)],
        compiler_params=_params(("arbitrary",)),
    )(dx_out, e, gate, ple_g, w_out)


def _attn_bwd(q, k, v, ltot, steps, dgated, o, zgate, reduce=None):
    S = q.shape[0]
    tq, tk = ATT_Q_TILE, ATT_K_TILE
    kpq = tq // tk
    nq = S // tq

    def body(q_ref, k_ref, v_ref, lt_ref, steps_ref, dg_ref, o_ref, z_ref, dq_ref, dk_ref, dv_ref, dz_ref,
             dk_acc, dv_acc):
        qi = pl.program_id(1)

        @pl.when(qi == 0)
        def _():
            dk_acc[...] = jnp.zeros_like(dk_acc)
            dv_acc[...] = jnp.zeros_like(dv_acc)

        lane = lax.broadcasted_iota(jnp.int32, (1, 128), 1)
        ri = lax.broadcasted_iota(jnp.int32, (tk, tk), 0)
        ci = lax.broadcasted_iota(jnp.int32, (tk, tk), 1)
        later_mat = _mask_bf16(ri > ci)
        before_mat = _mask_bf16(ri < ci)
        causal = ci < ri
        zz = z_ref[...].astype(F32)
        sg = _sigmoid(zz)
        dgv = dg_ref[...].astype(F32)
        dz_ref[...] = (dgv * o_ref[...].astype(F32) * _dsilu(zz, sg)).astype(BF16)
        dob = (dgv * (zz * sg)).astype(BF16)
        ltv = lt_ref[...]
        qv = q_ref[...]
        first = lane < HEAD_DIM
        masks = (first, jnp.logical_not(first))
        q_heads = [jnp.where(hm, qv, jnp.zeros_like(qv)) for hm in masks]
        do_heads = [jnp.where(hm, dob, jnp.zeros_like(dob)) for hm in masks]
        totals = [jnp.max(jnp.where(hm, ltv, -jnp.inf), axis=-1, keepdims=True) for hm in masks]

        def step(blocks, carry):
            chains = [(b, h) for b in range(len(blocks)) for h in range(2)]
            rows = [r for _, r, _ in blocks]
            cut = lambda t, b: t[rows[b][0]:rows[b][1]]
            s0 = [pl.multiple_of(kj * tk, tk) for kj, _, _ in blocks]
            kb = [k_ref[pl.ds(s, tk), :] for s in s0]
            vb = [v_ref[pl.ds(s, tk), :] for s in s0]
            visible = [causal if masked else None for _, _, masked in blocks]
            z = {c: _dot_nt(cut(q_heads[c[1]], c[0]), kb[c[0]]) for c in chains}
            da = {c: _dot_nt(cut(do_heads[c[1]], c[0]), vb[c[0]]) for c in chains}
            run = [carry[0], carry[3]]
            log_own, beta, later, base = {}, {}, {}, {}
            for c in chains:
                b, h = c
                lk = _softplus_parts(z[c])[0]
                if visible[b] is not None:
                    lk = jnp.where(visible[b], lk, 0.0)
                log_own[c] = z[c] + lk
                beta[c] = jnp.exp(log_own[c]).astype(BF16)
                later[c] = _dot(lk.astype(BF16), later_mat)
                run[h] = _add_rows(run[h], rows[b], jnp.sum(lk, axis=-1, keepdims=True))
                base[c] = cut(totals[h] - run[h], b)
            grun = [carry[1], carry[4]]
            a_bf, g_bf, gbefore, grun_at = {}, {}, {}, {}
            for c in chains:
                b, h = c
                a = jnp.exp(log_own[c] + later[c] + base[c])
                if visible[b] is not None:
                    a = jnp.where(visible[b], a, 0.0)
                a_bf[c] = a.astype(BF16)
                g = da[c] * a
                g_bf[c] = g.astype(BF16)
                gbefore[c] = _dot(g_bf[c], before_mat)
                grun_at[c] = cut(grun[h], b)
                grun[h] = _add_rows(grun[h], rows[b], jnp.sum(g, axis=-1, keepdims=True))
            dq = [carry[2], carry[5]]
            dk_blk = [jnp.zeros((tk, 128), F32) for _ in blocks]
            dv_blk = [jnp.zeros((tk, 128), F32) for _ in blocks]
            for c in chains:
                b, h = c
                g = g_bf[c].astype(F32)
                dz = g - beta[c].astype(F32) * (g + gbefore[c] + grun_at[c])
                if visible[b] is not None:
                    dz = jnp.where(visible[b], dz, 0.0)
                dzb = dz.astype(BF16)
                dq[h] = _add_rows(dq[h], rows[b], _dot(dzb, kb[b]))
                dk_blk[b] = dk_blk[b] + _dot_tn(dzb, cut(q_heads[h], b))
                dv_blk[b] = dv_blk[b] + _dot_tn(a_bf[c], cut(do_heads[h], b))
            for b in range(len(blocks)):
                dk_acc[pl.ds(s0[b], tk), :] += dk_blk[b]
                dv_acc[pl.ds(s0[b], tk), :] += dv_blk[b]
            return run[0], grun[0], dq[0], run[1], grun[1], dq[1]

        pairs = jnp.clip(jnp.max(steps_ref[0:4, :]).astype(jnp.int32), 0, qi)
        left = (qi - pairs) * kpq
        singles = jnp.clip(jnp.max(steps_ref[4:8, :]).astype(jnp.int32), 0, left)
        zero1, zero128 = jnp.zeros((tq, 1), F32), jnp.zeros((tq, 128), F32)
        carry = lax.fori_loop(left - singles, left, lambda kj, c: step([(kj, (0, tk), False)], c),
                              (zero1, zero1, zero128, zero1, zero1, zero128))
        carry = lax.fori_loop(qi - pairs, qi,
                              lambda n, c: step([(n * kpq, (0, tq), False), (n * kpq + 1, (0, tq), False)], c), carry)
        carry = step([(qi * kpq, (0, tk), True), (qi * kpq, (tk, tq), False), (qi * kpq + 1, (tk, tq), True)], carry)
        dq_ref[...] = jnp.where(first, carry[2], carry[5]).astype(BF16)

        @pl.when(qi == nq - 1)
        def _():
            dk_ref[...] = dk_acc[...].astype(BF16)
            dv_ref[...] = dv_acc[...].astype(BF16)

    blk = pl.BlockSpec((tq, 128), lambda hp, qi: (qi, hp))
    seq = pl.BlockSpec((S, 128), lambda hp, qi: (0, hp))
    return _call_with_gather(
        body, name="attn_bwd", grid=(D_MODEL // 128, nq),
        in_specs=[blk, seq, seq, blk, pl.BlockSpec((None, None, 8, 128), lambda hp, qi: (hp, qi, 0, 0)),
                  blk, blk, blk],
        out_specs=[blk, seq, seq, blk],
        out_shape=[jax.ShapeDtypeStruct((S, D_MODEL), BF16)] * 4,
        scratch_shapes=[pltpu.VMEM((S, 128), F32), pltpu.VMEM((S, 128), F32)],
        args=(q, k, v, ltot, steps, dgated, o, zgate), reduce=reduce, vmem_mib=48)


def _rms_bwd(xv, dh_gain_sum):
    r = lax.rsqrt(jnp.mean(xv * xv, axis=-1, keepdims=True) + EPS)
    xhat = xv * r
    dx = r * (dh_gain_sum - xhat * jnp.mean(dh_gain_sum * xhat, axis=-1, keepdims=True))
    return dx, xhat


def _b_in_bwd(dq, dk, dv, dz, q_raw, k_raw, x, dx_mid, q_gain_t, k_gain_t, b_gain, kv_gain, w_in, w_kv):
    S = x.shape[0]
    tm = ROW_TILE

    def body(dq_ref, dk_ref, dv_ref, dz_ref, qr_ref, kr_ref, x_ref, dxm_ref, qg_ref, kg_ref, bg_ref, kvg_ref,
             win_ref, wkv_ref, dqz_ref, dkv_ref, dx_ref, small_ref):
        @pl.when(pl.program_id(0) == 0)
        def _():
            small_ref[...] = jnp.zeros_like(small_ref)

        bd = _head_mean_matrix()

        def head_norm_bwd(dy_ref, raw_ref, gain, scale):
            raw = raw_ref[...].astype(F32)
            rr = lax.rsqrt(_head_mean(raw * raw, bd) + EPS)
            xhat = raw * rr
            dy = dy_ref[...].astype(F32) * scale
            gdy = dy * gain
            draw = rr * (gdy - xhat * _head_mean(gdy * xhat, bd))
            return draw.astype(BF16), jnp.sum(dy * xhat, axis=0, keepdims=True)

        dqr, dqg = head_norm_bwd(dq_ref, qr_ref, qg_ref[...], SB_SCALE)
        dkr, dkg = head_norm_bwd(dk_ref, kr_ref, kg_ref[...], 1.0)
        dqz_ref[:, :D_MODEL] = dqr
        dqz_ref[:, D_MODEL:] = dz_ref[...]
        dkv_ref[:, :D_MODEL] = dkr
        dkv_ref[:, D_MODEL:] = dv_ref[...]
        dhb = jnp.zeros((tm, D_MODEL), F32)
        dhkv = jnp.zeros((tm, D_MODEL), F32)
        for sh in range(N_CHIPS):
            cols = slice(sh * 512, (sh + 1) * 512)
            dhb = dhb + _dot_nt(dqz_ref[:, cols], win_ref[sh])
            dhkv = dhkv + _dot_nt(dkv_ref[:, cols], wkv_ref[sh])
        dx, xhat = _rms_bwd(x_ref[...], dhb * bg_ref[...] + dhkv * kvg_ref[...])
        dx_ref[...] = dxm_ref[...] + dx
        small_ref[0:1, :] += dqg
        small_ref[1:2, :] += dkg
        small_ref[2:3, :] += jnp.sum(dhb * xhat, axis=0, keepdims=True)
        small_ref[3:4, :] += jnp.sum(dhkv * xhat, axis=0, keepdims=True)

    row = pl.BlockSpec((tm, D_MODEL), lambda i: (i, 0))
    wide = pl.BlockSpec((tm, 2 * D_MODEL), lambda i: (i, 0))
    vec = pl.BlockSpec((1, D_MODEL), lambda i: (0, 0))
    wsp = pl.BlockSpec((N_CHIPS, D_MODEL, 512), lambda i: (0, 0, 0))
    return pl.pallas_call(
        body, name="b_in_bwd", grid=(S // tm,),
        in_specs=[row] * 8 + [vec] * 4 + [wsp, wsp],
        out_specs=[wide, wide, row, pl.BlockSpec((8, D_MODEL), lambda i: (0, 0))],
        out_shape=[jax.ShapeDtypeStruct((S, 2 * D_MODEL), BF16), jax.ShapeDtypeStruct((S, 2 * D_MODEL), BF16),
                   jax.ShapeDtypeStruct((S, D_MODEL), F32), jax.ShapeDtypeStruct((8, D_MODEL), F32)],
        compiler_params=_params(("arbitrary",)),
    )(dq, dk, dv, dz, q_raw, k_raw, x, dx_mid, q_gain_t, k_gain_t, b_gain, kv_gain, w_in, w_kv)


def _a_mix_bwd(dgated, uz, pooled, wg, scale, w_in, x, dx_mid, gain, reduce=None):
    S = x.shape[0]
    tm = ROW_TILE
    n = S // tm

    def body(dg_ref, z_ref, p_ref, wg_ref, sc_ref, win_ref, x_ref, dxm_ref, gn_ref,
             duz_ref, dmr_ref, dx_ref, small_ref, halo_hi, halo_lo):
        i = pl.program_id(0)

        @pl.when(i == 0)
        def _():
            small_ref[...] = jnp.zeros_like(small_ref)
            halo_hi[...] = jnp.zeros_like(halo_hi)
            halo_lo[...] = jnp.zeros_like(halo_lo)

        first_row = (n - 1 - i) * tm
        row = lax.broadcasted_iota(jnp.int32, (tm, tm), 0)
        col = lax.broadcasted_iota(jnp.int32, (tm, tm), 1)
        d = col - row
        for g, w in enumerate(POOL_WINDOWS):
            cols = slice(g * GROUP_DIM, (g + 1) * GROUP_DIM)
            wgg = _group_weight(wg_ref, g)
            sc = sc_ref[:, cols]
            mraw = _dot(p_ref[:, cols], wgg)
            z = z_ref[:, cols]
            sg = _sigmoid(z)
            dga = dg_ref[:, cols].astype(F32)
            dm = dga * (z * sg)
            duz_ref[:, D_MODEL + g * GROUP_DIM:D_MODEL + (g + 1) * GROUP_DIM] = (
                dga * (mraw * sc) * _dsilu(z, sg)).astype(BF16)
            small_ref[0:1, cols] += jnp.sum(dm * mraw, axis=0, keepdims=True)
            dmr = (dm * sc).astype(BF16)
            dmr_ref[:, cols] = dmr
            dp = _dot_nt(dmr, wgg)
            hi, lo = _hilo(dp * _inv_count(first_row, tm, w))
            t_main = _mask_bf16((d >= 0) & (d < w))
            t_halo = _mask_bf16(d + tm < w)
            du = (_dot(t_main, hi) + _dot(t_main, lo) + _dot(t_halo, halo_hi[:, cols]) + _dot(t_halo, halo_lo[:, cols])
                  - dp)
            halo_hi[:, cols] = hi
            halo_lo[:, cols] = lo
            duz_ref[:, cols] = du.astype(BF16)
        dh = jnp.zeros((tm, D_MODEL), F32)
        for sh in range(N_CHIPS):
            dh = dh + _dot_nt(duz_ref[:, sh * 512:(sh + 1) * 512], win_ref[sh])
        dx, xhat = _rms_bwd(x_ref[...], dh * gn_ref[...])
        dx_ref[...] = dxm_ref[...] + dx
        small_ref[1:2, :] += jnp.sum(dh * xhat, axis=0, keepdims=True)

    rev = lambda i: (n - 1 - i, 0)
    row = pl.BlockSpec((tm, D_MODEL), rev)
    vec = pl.BlockSpec((1, D_MODEL), lambda i: (0, 0))
    return _call_with_gather(
        body, name="a_mix_bwd", grid=(n,),
        in_specs=[row,
                  pl.BlockSpec((tm, D_MODEL), lambda i: (n - 1 - i, 1)),
                  row,
                  pl.BlockSpec((N_CHIPS, N_GROUPS, 64, GROUP_DIM), lambda i: (0, 0, 0, 0)),
                  vec,
                  pl.BlockSpec((N_CHIPS, D_MODEL, 512), lambda i: (0, 0, 0)),
                  row, row, vec],
        out_specs=[pl.BlockSpec((tm, 2 * D_MODEL), rev), row, row,
                   pl.BlockSpec((8, D_MODEL), lambda i: (0, 0))],
        out_shape=[jax.ShapeDtypeStruct((S, 2 * D_MODEL), BF16), jax.ShapeDtypeStruct((S, D_MODEL), BF16),
                   jax.ShapeDtypeStruct((S, D_MODEL), F32), jax.ShapeDtypeStruct((8, D_MODEL), F32)],
        scratch_shapes=[pltpu.VMEM((tm, D_MODEL), BF16), pltpu.VMEM((tm, D_MODEL), BF16)],
        args=(dgated, uz, pooled, wg, scale, w_in, x, dx_mid, gain), reduce=reduce, vmem_mib=48)


def _wgrad(name, a, dy, n_shards, a_spec=None, k_dim=None):
    S, n_cols = dy.shape
    ts = WGRAD_SEQ_TILE
    k_dim = a.shape[-1] if k_dim is None else k_dim
    wn = n_cols // n_shards
    tk = min(k_dim, WGRAD_ACC_BYTES // (4 * n_cols))
    nst = S // ts

    def body(a_ref, dy_ref, out_ref, acc):
        st = pl.program_id(1)

        @pl.when(st == 0)
        def _():
            acc[...] = jnp.zeros_like(acc)

        acc[...] += _dot_tn(a_ref[...].astype(BF16), dy_ref[...].astype(BF16))

        @pl.when(st == nst - 1)
        def _():
            for sh in range(n_shards):
                out_ref[sh] = acc[:, sh * wn:(sh + 1) * wn]

    if a_spec is None:
        a_spec = pl.BlockSpec((ts, tk), lambda kt, st: (st, kt))
    return pl.pallas_call(
        body, name=name, grid=(k_dim // tk, nst),
        in_specs=[a_spec, pl.BlockSpec((ts, n_cols), lambda kt, st: (st, 0))],
        out_specs=pl.BlockSpec((n_shards, tk, wn), lambda kt, st: (0, kt, 0)),
        out_shape=jax.ShapeDtypeStruct((n_shards, k_dim, wn), F32),
        scratch_shapes=[pltpu.VMEM((tk, n_cols), F32)],
        compiler_params=_params(("parallel", "arbitrary")),
    )(a, dy)


def _wgrad_ple(name, p, layer, de):
    ts = WGRAD_SEQ_TILE
    spec = pl.BlockSpec((None, None, ts, PLE_DIM), lambda kt, st: (layer, 0, st, 0))
    return _wgrad(name, p, de, N_CHIPS, a_spec=spec, k_dim=PLE_DIM)


def _wgrad_group(pooled, dmr):
    S = pooled.shape[0]
    ts = WGRAD_SEQ_TILE
    nst = S // ts

    def body(p_ref, d_ref, out_ref, acc):
        st = pl.program_id(1)

        @pl.when(st == 0)
        def _():
            acc[...] = jnp.zeros_like(acc)

        acc[...] += _dot_tn(p_ref[...], d_ref[...])

        @pl.when(st == nst - 1)
        def _():
            for sh in range(N_CHIPS):
                out_ref[sh] = acc[sh * 64:(sh + 1) * 64, :]

    blk = pl.BlockSpec((ts, GROUP_DIM), lambda g, st: (st, g))
    return pl.pallas_call(
        body, name="wgrad_group", grid=(N_GROUPS, nst),
        in_specs=[blk, blk],
        out_specs=pl.BlockSpec((N_CHIPS, None, 64, GROUP_DIM), lambda g, st: (0, g, 0, 0)),
        out_shape=jax.ShapeDtypeStruct((N_CHIPS, N_GROUPS, 64, GROUP_DIM), F32),
        scratch_shapes=[pltpu.VMEM((GROUP_DIM, GROUP_DIM), F32)],
        compiler_params=_params(("parallel", "arbitrary")),
    )(pooled, dmr)


GATHER_AT = {
    "a_in": ("a_w_group", "a_w_out", "ple_w0", "ple_gate_w0"),
    "a_mix": ("w_kv",),
    "a_out_ple": ("b_w_in",),
    "attn_fwd": ("b_w_out", "ple_w1", "ple_gate_w1"),
}


REDUCE_AT = {
    "attn_bwd": ("b_w_out", "ple_w1", "ple_gate_w1"),
    "a_mix_bwd": ("a_w_out", "ple_w0", "ple_gate_w0"),
}


def _local_step(x, p, target, w, local=None, state=None):
    w = dict(w)

    def run(fn, host, n_out, *args, **kwargs):
        names = GATHER_AT[host] if local is not None else ()
        res = fn(*args, gather=[local[n] for n in names], **kwargs)
        w.update(zip(names, res[n_out:]))
        return res[:n_out]

    k_gain_t = jnp.tile(w["k_norm"].reshape(1, HEAD_DIM), (1, N_HEADS))
    q_gain_t = jnp.tile(w["b_q_norm"].reshape(1, HEAD_DIM), (1, N_HEADS))

    uz, h_a = run(_a_in, "a_in", 2, x, w["a_norm"], w["a_w_in"])
    wg4 = w["a_w_group"].reshape(N_CHIPS, N_GROUPS, 64, GROUP_DIM)
    wa_out = w["a_w_out"].reshape(D_MODEL, D_MODEL)
    gated_a, pooled = run(_a_mix, "a_mix", 2, uz, wg4, w["a_scale"])
    x1, x2, e_a, gate_a = run(_out_ple, "a_out_ple", 4, "a_out_ple", gated_a, x, wa_out, p, 0,
                              w["ple_w0"], w["ple_gate_w0"])
    h_kv, h_b, k_raw, q_raw, k, q, v, z_b = _b_in(
        x2, w["kv_norm"], w["b_norm"], k_gain_t, q_gain_t, w["w_kv"], w["b_w_in"])
    o, gated_b, ltot, att_steps = run(_attn_fwd, "attn_fwd", 4, q, k, v, z_b)
    wb_out = w["b_w_out"].reshape(D_MODEL, D_MODEL)
    x3, dx4, e_b, gate_b, loss_blk = _out_ple("b_out_ple", gated_b, x2, wb_out, p, 1, w["ple_w1"], w["ple_gate_w1"],
                                              target=target)

    grads, updates = {}, {}

    def hosted(fn, host, n_out, *args):
        if state is None:
            return fn(*args)
        names = REDUCE_AT[host]
        seeds = [updates.get(n[:-1] + "1") if n.startswith("ple") and n.endswith("0") else None for n in names]
        res = fn(*args, reduce=([grads.pop(n) for n in names], *[[t[n] for n in names] for t in state[:3]],
                                [state[3][n] for n in names], seeds))
        for i, n in enumerate(names):
            updates[n] = tuple(group[i] for group in res[n_out:])
        return res[:n_out]

    de_b, dgp_b, dx3, dgated_b = _ple_out_bwd("b_ple_out_bwd", dx4, e_b, gate_b, w["ple_gate_w1"], wb_out)
    grads["b_w_out"] = _wgrad("wgrad_b_out", gated_b, dx3, 1).reshape(N_CHIPS, 256, D_MODEL)
    grads["ple_w1"] = _wgrad_ple("wgrad_ple1", p, 1, de_b)
    grads["ple_gate_w1"] = _wgrad("wgrad_gate1", x3, dgp_b, 1).reshape(N_CHIPS, 256, D_MODEL)
    dq, dk, dv, dz_b = hosted(_attn_bwd, "attn_bwd", 4, q, k, v, ltot, att_steps, dgated_b, o, z_b)
    dqz, dkv, dx2, small_b = _b_in_bwd(dq, dk, dv, dz_b, q_raw, k_raw, x2, dx3, q_gain_t, k_gain_t,
                                       w["b_norm"], w["kv_norm"], w["b_w_in"], w["w_kv"])
    grads["w_kv"] = _wgrad("wgrad_kv", h_kv, dkv, N_CHIPS)
    grads["b_w_in"] = _wgrad("wgrad_b_in", h_b, dqz, N_CHIPS)
    de_a, dgp_a, dx1, dgated_a = _ple_out_bwd("a_ple_out_bwd", dx2, e_a, gate_a, w["ple_gate_w0"], wa_out)
    grads["a_w_out"] = _wgrad("wgrad_a_out", gated_a, dx1, 1).reshape(N_CHIPS, 256, D_MODEL)
    grads["ple_w0"] = _wgrad_ple("wgrad_ple0", p, 0, de_a)
    grads["ple_gate_w0"] = _wgrad("wgrad_gate0", x1, dgp_a, 1).reshape(N_CHIPS, 256, D_MODEL)
    duz, dmr, grad_x, small_a = hosted(_a_mix_bwd, "a_mix_bwd", 4, dgated_a, uz, pooled, wg4, w["a_scale"],
                                       w["a_w_in"], x, dx1, w["a_norm"])
    grads["a_w_in"] = _wgrad("wgrad_a_in", h_a, duz, N_CHIPS)
    grads["a_w_group"] = _wgrad_group(pooled, dmr).reshape(N_CHIPS, N_GROUPS * 64, GROUP_DIM)

    fold = lambda row: jnp.pad(row.reshape(N_HEADS, HEAD_DIM).sum(axis=0), (0, D_MODEL - HEAD_DIM))
    small = jnp.stack([small_a[1], small_a[0], small_b[3], small_b[2], fold(small_b[1]), fold(small_b[0]),
                       jnp.pad(loss_blk[0], (0, D_MODEL - loss_blk.shape[1])), jnp.zeros((D_MODEL,), F32)])
    return grad_x, grads, updates, small


def _mesh_place():
    x, y, c = lax.axis_index("x"), lax.axis_index("y"), lax.axis_index("c")
    other_chips = [(1 - x, y), (x, 1 - y), (1 - x, 1 - y)]
    return x, y, c, other_chips


def _gather_sems(n):
    return [pltpu.SemaphoreType.DMA((3 * n,)), pltpu.SemaphoreType.DMA((3 * n,)),
            pltpu.SemaphoreType.DMA((3 * n,)), pltpu.SemaphoreType.DMA((3 * n,)), pltpu.SemaphoreType.DMA((n,))]


def _gather_copies(srcs, outs, sems):
    send_far, recv_far, send_sib, recv_sib, local_sem = sems
    n = len(srcs)
    x, y, c, chips = _mesh_place()
    me = 2 * x + y
    sibling = (x, y, 1 - c)

    def half(k, which):
        rows = srcs[k].shape[0] // 2
        return pl.ds(pl.multiple_of(which * rows, 16), rows)

    local = [pltpu.make_async_copy(srcs[k], outs[k].at[me], local_sem.at[k]) for k in range(n)]
    far = [pltpu.make_async_remote_copy(
        src_ref=srcs[k].at[half(k, c)], dst_ref=outs[k].at[me, half(k, c)],
        send_sem=send_far.at[j * n + k], recv_sem=recv_far.at[j * n + k], device_id=(px, py, c), device_id_type=MESH)
        for j, (px, py) in enumerate(chips) for k in range(n)]

    def landed(j, k, which, from_far):
        px, py = chips[j]
        piece = outs[k].at[2 * px + py, half(k, which)]
        send, recv = (send_far, recv_far) if from_far else (send_sib, recv_sib)
        return pltpu.make_async_remote_copy(src_ref=piece, dst_ref=piece, send_sem=send.at[j * n + k],
                                            recv_sem=recv.at[j * n + k], device_id=sibling, device_id_type=MESH)

    return local, far, landed, c


def _gather_start(srcs, outs, sems):
    local, far, _, _ = _gather_copies(srcs, outs, sems)
    for cp in local + far:
        cp.start()


def _gather_pass_on(srcs, outs, sems):
    _, _, landed, c = _gather_copies(srcs, outs, sems)
    for j in range(3):
        for k in range(len(srcs)):
            landed(j, k, c, True).wait_recv()
            landed(j, k, c, False).start()


def _gather_finish(srcs, outs, sems):
    local, far, landed, c = _gather_copies(srcs, outs, sems)
    pairs = [(j, k) for j in range(3) for k in range(len(srcs))]
    for j, k in pairs:
        landed(j, k, 1 - c, False).wait_recv()
    for cp in far + [landed(j, k, c, False) for j, k in pairs]:
        cp.wait_send()
    for cp in local:
        cp.wait()


def _call_with_gather(body, *, name, grid, in_specs, out_specs, out_shape, args, gather=(), reduce=None,
                      scratch_shapes=(), vmem_mib=32):
    n_in, n_out, n_scr, n_g = len(args), len(out_shape), len(scratch_shapes), len(gather)
    n_r = len(reduce[0]) if reduce else 0
    pieces = _reduce_pieces(reduce[0], reduce[4]) if reduce else []
    reduce_args = [a for group in reduce[:4] for a in group] if reduce else []
    seeds = reduce[5] if reduce else []
    seeded = [(k, a) for k, seed in enumerate(seeds) if seed is not None for a in range(4)]
    gather_sems = _gather_sems(n_g) if n_g else []
    n_steps = 1
    for g in grid:
        n_steps *= g

    def wrapped(*refs):
        refs = list(refs)
        take = lambda count: [refs.pop(0) for _ in range(count)]
        ins, g_in, r_in = take(n_in), take(n_g), take(4 * n_r)
        take(len(seeded))
        outs, g_out, r_out = take(n_out), take(n_g), take(4 * n_r)
        scratch, sems, r_scratch = take(n_scr), take(len(gather_sems)), refs
        step = 0
        for axis, g in enumerate(grid):
            step = step * g + pl.program_id(axis)
        if n_g:
            @pl.when(step == 0)
            def _():
                _gather_start(g_in, g_out, sems)

        if n_r:
            ticks, drain = _reduce_ticks(pieces, n_r, (*r_in, *r_out, *r_scratch))
            for t, tick in enumerate(ticks[:n_steps]):
                pl.when(step == t)(tick)

        body(*ins, *outs, *scratch)
        if n_r:
            for tick in ticks[n_steps:]:
                pl.when(step == n_steps - 1)(tick)
            pl.when(step == n_steps - 1)(drain)
        if n_g:
            @pl.when(step == max(n_steps - 2, 0))
            def _():
                _gather_pass_on(g_in, g_out, sems)

            @pl.when(step == n_steps - 1)
            def _():
                _gather_finish(g_in, g_out, sems)

    hbm = pl.BlockSpec(memory_space=pltpu.HBM)
    res = pl.pallas_call(
        wrapped, name=name, grid=grid,
        in_specs=list(in_specs) + [hbm] * (n_g + 4 * n_r + len(seeded)),
        out_specs=list(out_specs) + [hbm] * (n_g + 4 * n_r),
        out_shape=list(out_shape) + [jax.ShapeDtypeStruct((N_CHIPS,) + g.shape, BF16) for g in gather]
        + ([jax.ShapeDtypeStruct(w.shape, F32) for _ in range(4) for w in reduce[1]] if reduce else []),
        input_output_aliases={n_in + n_g + 4 * n_r + i: n_out + n_g + a * n_r + k for i, (k, a) in enumerate(seeded)},
        scratch_shapes=list(scratch_shapes) + gather_sems + (_reduce_scratch() if reduce else []),
        compiler_params=_params(("arbitrary",) * len(grid), vmem_mib),
    )(*args, *gather, *reduce_args, *[seeds[k][a] for k, a in seeded])
    if not reduce:
        return res
    plain = list(res[:n_out + n_g])
    return plain + [res[n_out + n_g + i * n_r:n_out + n_g + (i + 1) * n_r] for i in range(4)]


def _allgather_weights(shards, small, casts):
    n = len(shards)
    cast_out = [(k, r0, r1) for k, (_, ranges) in enumerate(casts) for r0, r1 in ranges]
    n_c, n_co = len(casts), len(cast_out)

    def body(*refs):
        ins, small_in, cast_in = refs[:n], refs[n], refs[n + 1:n + 1 + n_c]
        refs = refs[n + 1 + n_c:]
        outs, small_out, cast_dst = refs[:n], refs[n], refs[n + 1:n + 1 + n_co]
        refs = refs[n + 1 + n_co:]
        cast, cast_buf = refs[:n], refs[n:n + n_co]
        send_far, recv_far, send_sib, recv_sib, send_small, recv_small, local_sem, cast_sem = refs[n + n_co:]
        x, y, c, chips = _mesh_place()
        me = 2 * x + y
        sibling = (x, y, 1 - c)

        def half(k, which):
            rows = ins[k].shape[0] // 2
            return pl.ds(pl.multiple_of(which * rows, 16), rows)

        local = []
        for k in range(n):
            cast[k][...] = ins[k][...].astype(BF16)
            local.append(pltpu.make_async_copy(cast[k], outs[k].at[me], local_sem.at[k]))
            local[-1].start()
        local.append(pltpu.make_async_copy(small_in, small_out.at[me], local_sem.at[n]))
        local[-1].start()

        sends = []
        for j, (px, py) in enumerate(chips):
            for k in range(n):
                cp = pltpu.make_async_remote_copy(
                    src_ref=cast[k].at[half(k, c)], dst_ref=outs[k].at[me, half(k, c)],
                    send_sem=send_far.at[j * n + k], recv_sem=recv_far.at[j * n + k],
                    device_id=(px, py, c), device_id_type=MESH)
                cp.start()
                sends.append(cp)
            cp = pltpu.make_async_remote_copy(
                src_ref=small_in, dst_ref=small_out.at[me], send_sem=send_small.at[j], recv_sem=recv_small.at[j],
                device_id=(px, py, c), device_id_type=MESH)
            cp.start()
            sends.append(cp)

        for i, (k, r0, r1) in enumerate(cast_out):
            cast_buf[i][...] = cast_in[k][r0:r1, :].astype(BF16)
            local.append(pltpu.make_async_copy(cast_buf[i], cast_dst[i], cast_sem.at[i]))
            local[-1].start()

        def landed(j, k, which, sems_s, sems_r, device):
            px, py = chips[j]
            piece = outs[k].at[2 * px + py, half(k, which)]
            return pltpu.make_async_remote_copy(
                src_ref=piece, dst_ref=piece, send_sem=sems_s.at[j * n + k], recv_sem=sems_r.at[j * n + k],
                device_id=device, device_id_type=MESH)

        for j in range(len(chips)):
            for k in range(n):
                landed(j, k, c, send_far, recv_far, sibling).wait_recv()
                cp = landed(j, k, c, send_sib, recv_sib, sibling)
                cp.start()
                sends.append(cp)
        for j, (px, py) in enumerate(chips):
            for k in range(n):
                landed(j, k, 1 - c, send_sib, recv_sib, sibling).wait_recv()
            pltpu.make_async_remote_copy(
                src_ref=small_in, dst_ref=small_out.at[2 * px + py], send_sem=send_small.at[j],
                recv_sem=recv_small.at[j], device_id=(px, py, c), device_id_type=MESH).wait_recv()
        for cp in sends:
            cp.wait_send()
        for cp in local:
            cp.wait()

    vmem = pl.BlockSpec(memory_space=pltpu.VMEM)
    hbm = pl.BlockSpec(memory_space=pltpu.HBM)
    cast_shapes = [(r1 - r0, casts[k][0].shape[1]) for k, r0, r1 in cast_out]
    res = pl.pallas_call(
        body, name="allgather_weights",
        in_specs=[vmem] * (n + 1 + n_c), out_specs=[hbm] * (n + 1 + n_co),
        out_shape=[jax.ShapeDtypeStruct((N_CHIPS,) + s.shape, BF16) for s in shards]
        + [jax.ShapeDtypeStruct((N_CHIPS,) + small.shape, F32)]
        + [jax.ShapeDtypeStruct(s, BF16) for s in cast_shapes],
        scratch_shapes=[pltpu.VMEM(s.shape, BF16) for s in shards] + [pltpu.VMEM(s, BF16) for s in cast_shapes]
        + [pltpu.SemaphoreType.DMA((3 * n,)), pltpu.SemaphoreType.DMA((3 * n,)),
           pltpu.SemaphoreType.DMA((3 * n,)), pltpu.SemaphoreType.DMA((3 * n,)),
           pltpu.SemaphoreType.DMA((3,)), pltpu.SemaphoreType.DMA((3,)),
           pltpu.SemaphoreType.DMA((n + 1,)), pltpu.SemaphoreType.DMA((n_co,))],
        compiler_params=_params(None, 40),
    )(*shards, small, *[a for a, _ in casts])
    return res[:n], res[n], res[n + 1:]


def _adamw(w, g, m, v):
    m = ADAM_B1 * m + (1.0 - ADAM_B1) * g
    v = ADAM_B2 * v + (1.0 - ADAM_B2) * (g * g)
    m_hat = m / (1.0 - ADAM_B1 ** ADAM_STEP)
    v_hat = v / (1.0 - ADAM_B2 ** ADAM_STEP)
    delta = -ADAM_LR * (m_hat / (jnp.sqrt(v_hat) + ADAM_EPS) + ADAM_WD * w)
    return delta, m, v


RS_PIECE_ROWS = 128
RS_PIECE_COLS = 512


def _reduce_adam_all(grads, ws, ms, vs, small, bases=None, seeds=None):
    n_w = len(grads)
    pieces = _reduce_pieces(grads, bases)
    lanes = [pieces[0::2], pieces[1::2]]
    n_lane = len(_reduce_scratch())
    n_small = len(_small_sum_scratch(small.shape))
    seeds = seeds or [None] * n_w
    seeded = [(k, a) for k, seed in enumerate(seeds) if seed is not None for a in range(4)]
    n_in = 4 * n_w + 1

    def body(*refs):
        refs = list(refs)
        del refs[n_in:n_in + len(seeded)]
        small_in = refs.pop(4 * n_w)
        small_out = refs.pop(8 * n_w)
        small_scratch = [refs.pop() for _ in range(n_small)][::-1]
        sends = _small_sum_start(small_in, *small_scratch)
        arrays, scratch = refs[:8 * n_w], refs[8 * n_w:]
        runs = [_reduce_ticks(lane, n_w, arrays + scratch[i * n_lane:(i + 1) * n_lane])
                for i, lane in enumerate(lanes) if lane]
        for t in range(max(len(ticks) for ticks, _ in runs)):
            for ticks, _ in runs:
                if t < len(ticks):
                    ticks[t]()
        for _, drain in runs:
            drain()
        _small_sum_finish(sends, small_scratch[0], small_out)

    hbm = pl.BlockSpec(memory_space=pltpu.HBM)
    vmem = pl.BlockSpec(memory_space=pltpu.VMEM)
    outs = pl.pallas_call(
        body, name="reduce_adam_all",
        in_specs=[hbm] * (4 * n_w) + [vmem] + [hbm] * len(seeded), out_specs=[hbm] * (4 * n_w) + [vmem],
        out_shape=[jax.ShapeDtypeStruct(w.shape, F32) for _ in range(4) for w in ws]
        + [jax.ShapeDtypeStruct(small.shape, F32)],
        input_output_aliases={n_in + i: a * n_w + k for i, (k, a) in enumerate(seeded)},
        scratch_shapes=_reduce_scratch() * len(lanes) + _small_sum_scratch(small.shape),
        compiler_params=_params(None, 48),
    )(*grads, *ws, *ms, *vs, small, *[seeds[k][a] for k, a in seeded])
    return [outs[i * n_w:(i + 1) * n_w] for i in range(4)], outs[4 * n_w]


def _reduce_pieces(grads, bases=None):
    pieces = []
    for k, g in enumerate(grads):
        hr, cols = g.shape[1] // 2, g.shape[2]
        pr, pc = min(hr, RS_PIECE_ROWS), min(cols, RS_PIECE_COLS)
        base = bases[k] if bases else 0
        pieces += [(k, ro, hr, co, pr, pc, base) for ro in range(0, hr, pr) for co in range(0, cols, pc)]
    return pieces


def _reduce_scratch():
    P, C = RS_PIECE_ROWS, RS_PIECE_COLS
    return [
        pltpu.VMEM((3, N_CHIPS, P, C), F32), pltpu.VMEM((3, N_CHIPS, P, C), F32),
        pltpu.VMEM((2, N_CHIPS, P, C), BF16), pltpu.VMEM((2, N_CHIPS, P, C), BF16),
        pltpu.VMEM((2, N_CHIPS, P, C), F32),
        pltpu.VMEM((2, 3, P, C), BF16), pltpu.VMEM((2, 3, P, C), BF16),
        pltpu.VMEM((2, 2, P, C), F32),
        pltpu.VMEM((2, 3, 2, P, C), F32), pltpu.VMEM((2, 4, 2, P, C), F32),
        pltpu.SemaphoreType.DMA((3, 2)), pltpu.SemaphoreType.DMA((2, 3, 2)),
        pltpu.SemaphoreType.DMA((2,)), pltpu.SemaphoreType.DMA((2,)),
        pltpu.SemaphoreType.DMA((2, 3)), pltpu.SemaphoreType.DMA((2, 3)),
        pltpu.SemaphoreType.DMA((2,)), pltpu.SemaphoreType.DMA((2,)),
        pltpu.SemaphoreType.DMA((2, 4, 2))]


def _reduce_ticks(pieces, n_w, refs):
    n = len(pieces)

    def build(*refs):
        g_in, w_in, m_in, v_in = (refs[i * n_w:(i + 1) * n_w] for i in range(4))
        g_out, d_out, m_out, v_out = (refs[(4 + i) * n_w:(5 + i) * n_w] for i in range(4))
        (gm, go, sb1, rb1, part, sb2, rb2, fin, wmv, outs,
         ld_sem, wmv_sem, s1_send, s1_recv, s2_send, s2_recv, s3_send, s3_recv, out_sem) = refs[8 * n_w:]
        x, y, c, chips = _mesh_place()
        me = 2 * x + y
        sibling = (x, y, 1 - c)

        def at_hbm(i, which, in_shard):
            _, ro, hr, co, pr, pc, base = pieces[i]
            half = c if which == 0 else 1 - c
            return pl.ds(pl.multiple_of((base if in_shard else 0) + half * hr + ro, 64), pr), pl.ds(co, pc)

        def win(i):
            return pl.ds(0, pieces[i][4]), pl.ds(0, pieces[i][5])

        every = slice(None)

        def loads(i):
            k, s = pieces[i][0], i % 3
            return [pltpu.make_async_copy(g_in[k].at[(every,) + at_hbm(i, h, False)], buf.at[(s, every) + win(i)],
                                          ld_sem.at[s, h])
                    for h, buf in enumerate((gm, go))]

        def wmv_loads(i):
            k, s = pieces[i][0], i % 2
            return [pltpu.make_async_copy(src[k].at[at_hbm(i, h, True)], wmv.at[(s, a, h) + win(i)], wmv_sem.at[s, a, h])
                    for a, src in enumerate((w_in, m_in, v_in)) for h in range(2)]

        def stores(i):
            k, s = pieces[i][0], i % 2
            return [pltpu.make_async_copy(outs.at[(s, a, h) + win(i)], dst[k].at[at_hbm(i, h, True)], out_sem.at[s, a, h])
                    for a, dst in enumerate((g_out, d_out, m_out, v_out)) for h in range(2)]

        def swap1(i):
            s = i % 2
            return pltpu.make_async_remote_copy(
                src_ref=sb1.at[(s, every) + win(i)], dst_ref=rb1.at[(s, every) + win(i)],
                send_sem=s1_send.at[s], recv_sem=s1_recv.at[s], device_id=sibling, device_id_type=MESH)

        def far2(i, j):
            s = i % 2
            px, py = chips[j]
            return pltpu.make_async_remote_copy(
                src_ref=sb2.at[(s, j) + win(i)], dst_ref=rb2.at[(s, j) + win(i)],
                send_sem=s2_send.at[s, j], recv_sem=s2_recv.at[s, j], device_id=(px, py, c), device_id_type=MESH)

        def swap3(i):
            s = i % 2
            return pltpu.make_async_remote_copy(
                src_ref=fin.at[(s, 0) + win(i)], dst_ref=fin.at[(s, 1) + win(i)],
                send_sem=s3_send.at[s], recv_sem=s3_recv.at[s], device_id=sibling, device_id_type=MESH)

        def stage0(i):
            for cp in loads(i):
                cp.start()

        def stage1(i):
            s, s3 = i % 2, i % 3
            for cp in loads(i):
                cp.wait()
            sb1[(s, every) + win(i)] = go[(s3, every) + win(i)].astype(BF16)
            swap1(i).start()

        def stage2(i):
            s, s3 = i % 2, i % 3
            swap1(i).wait()
            part[(s, every) + win(i)] = gm[(s3, every) + win(i)] + rb1[(s, every) + win(i)].astype(F32)
            for j, (px, py) in enumerate(chips):
                sb2[(s, j) + win(i)] = part[(s, 2 * px + py) + win(i)].astype(BF16)
                far2(i, j).start()

        def stage3(i):
            s = i % 2
            total = part[(s, me) + win(i)]
            for j in range(3):
                far2(i, j).wait()
                total = total + rb2[(s, j) + win(i)].astype(F32)
            fin[(s, 0) + win(i)] = total
            swap3(i).start()
            for cp in wmv_loads(i):
                cp.start()

        def stage4(i):
            s = i % 2
            if i >= 2:
                for cp in stores(i - 2):
                    cp.wait()
            swap3(i).wait()
            for cp in wmv_loads(i):
                cp.wait()
            both = (every,) + win(i)
            g = fin[(s,) + both]
            delta, m_new, v_new = _adamw(wmv[(s, 0) + both], g, wmv[(s, 1) + both], wmv[(s, 2) + both])
            outs[(s, 0) + both] = g
            outs[(s, 1) + both] = delta
            outs[(s, 2) + both] = m_new
            outs[(s, 3) + both] = v_new
            for cp in stores(i):
                cp.start()

        stages = (stage0, stage1, stage2, stage3, stage4)

        def tick(t):
            for age in reversed(range(len(stages))):
                if 0 <= t - age < n:
                    stages[age](t - age)

        def drain():
            for i in range(max(0, n - 2), n):
                for cp in stores(i):
                    cp.wait()

        return [functools.partial(tick, t) for t in range(n + len(stages) - 1)], drain

    return build(*refs)


N_DEVICES = 8


def _small_sum_scratch(shape):
    return [pltpu.VMEM((N_DEVICES,) + shape, F32),
            pltpu.SemaphoreType.DMA((N_DEVICES - 1,)), pltpu.SemaphoreType.DMA((N_DEVICES - 1,))]


def _small_sum_start(part_ref, buf, send_sem, recv_sem):
    x, y, c, _ = _mesh_place()
    me = 4 * x + 2 * y + c
    buf[me] = part_ref[...]
    sends = []
    for k in range(1, N_DEVICES):
        peer = ((1 - x) if k & 4 else x, (1 - y) if k & 2 else y, (1 - c) if k & 1 else c)
        cp = pltpu.make_async_remote_copy(src_ref=part_ref, dst_ref=buf.at[me], send_sem=send_sem.at[k - 1],
                                          recv_sem=recv_sem.at[k - 1], device_id=peer, device_id_type=MESH)
        cp.start()
        sends.append(cp)
    return sends


def _small_sum_finish(sends, buf, out_ref):
    for cp in sends:
        cp.wait_recv()
    total = buf[0]
    for s in range(1, N_DEVICES):
        total = total + buf[s]
    out_ref[...] = total
    for cp in sends:
        cp.wait_send()


def _adam_small(w, g, m, v):
    def body(w_ref, g_ref, m_ref, v_ref, d_ref, mo_ref, vo_ref):
        delta, m_new, v_new = _adamw(w_ref[...], g_ref[...], m_ref[...], v_ref[...])
        d_ref[...] = delta
        mo_ref[...] = m_new
        vo_ref[...] = v_new

    vmem = pl.BlockSpec(memory_space=pltpu.VMEM)
    return pl.pallas_call(
        body, name="adam_small", in_specs=[vmem] * 4, out_specs=[vmem] * 3,
        out_shape=[jax.ShapeDtypeStruct(w.shape, F32)] * 3,
    )(w, g, m, v)


BIG = ("a_w_in", "a_w_group", "a_w_out", "w_kv", "b_w_in", "b_w_out", "ple_w", "ple_gate_w")
SMALL = ("a_norm", "a_scale", "kv_norm", "b_norm", "k_norm", "b_q_norm")
SMALL_SHARDED = ("a_norm", "a_scale")
WEIGHTS = ("a_norm", "a_w_in", "a_w_group", "a_scale", "a_w_out", "kv_norm", "w_kv", "k_norm", "b_norm", "b_w_in",
           "b_q_norm", "b_w_out", "ple_w", "ple_gate_w")


def _as_matrix(a):
    return a.reshape(-1, a.shape[-1])


def _pack_small(arrs):
    rows = [jnp.pad(a.reshape(-1), (0, D_MODEL - a.size)) for a in arrs]
    rows += [jnp.zeros((D_MODEL,), F32)] * (8 - len(rows))
    return jnp.stack(rows)


def kernel(x, p, a_norm, a_w_in, a_w_group, a_scale, a_w_out, kv_norm, w_kv, k_norm, b_norm, b_w_in, b_q_norm, b_w_out, ple_w, ple_gate_w, loss_target, m_a_norm, m_a_w_in, m_a_w_group, m_a_scale, m_a_w_out, m_kv_norm, m_w_kv, m_k_norm, m_b_norm, m_b_w_in, m_b_q_norm, m_b_w_out, m_ple_w, m_ple_gate_w, v_a_norm, v_a_w_in, v_a_w_group, v_a_scale, v_a_w_out, v_kv_norm, v_w_kv, v_k_norm, v_b_norm, v_b_w_in, v_b_q_norm, v_b_w_out, v_ple_w, v_ple_gate_w):
    wts = dict(a_norm=a_norm, a_w_in=a_w_in, a_w_group=a_w_group, a_scale=a_scale, a_w_out=a_w_out, kv_norm=kv_norm,
               w_kv=w_kv, k_norm=k_norm, b_norm=b_norm, b_w_in=b_w_in, b_q_norm=b_q_norm, b_w_out=b_w_out,
               ple_w=ple_w, ple_gate_w=ple_gate_w)
    mom = dict(a_norm=m_a_norm, a_w_in=m_a_w_in, a_w_group=m_a_w_group, a_scale=m_a_scale, a_w_out=m_a_w_out,
               kv_norm=m_kv_norm, w_kv=m_w_kv, k_norm=m_k_norm, b_norm=m_b_norm, b_w_in=m_b_w_in,
               b_q_norm=m_b_q_norm, b_w_out=m_b_w_out, ple_w=m_ple_w, ple_gate_w=m_ple_gate_w)
    var = dict(a_norm=v_a_norm, a_w_in=v_a_w_in, a_w_group=v_a_w_group, a_scale=v_a_scale, a_w_out=v_a_w_out,
               kv_norm=v_kv_norm, w_kv=v_w_kv, k_norm=v_k_norm, b_norm=v_b_norm, b_w_in=v_b_w_in,
               b_q_norm=v_b_q_norm, b_w_out=v_b_w_out, ple_w=v_ple_w, ple_gate_w=v_ple_gate_w)
    S = x.shape[1]
    chip = 2 * lax.axis_index("x") + lax.axis_index("y")

    sharded_small = jnp.concatenate([a_norm.reshape(1, 256), a_scale.reshape(1, 256), jnp.zeros((6, 256), F32)], axis=0)
    later = ("a_w_group", "a_w_out", "w_kv", "b_w_in", "b_w_out", "ple_w", "ple_gate_w")
    (a_w_in_full,), small_full, copies = _allgather_weights(
        [_as_matrix(a_w_in)], sharded_small,
        [(_as_matrix(wts[n]), [(0, 256), (256, 512)] if n.startswith("ple") else [(0, _as_matrix(wts[n]).shape[0])])
         for n in later])
    local = dict(zip(("a_w_group", "a_w_out", "w_kv", "b_w_in", "b_w_out", "ple_w0", "ple_w1", "ple_gate_w0",
                      "ple_gate_w1"), copies))
    full = dict(a_w_in=a_w_in_full,
                a_norm=small_full[:, 0, :].reshape(1, D_MODEL), a_scale=small_full[:, 1, :].reshape(1, D_MODEL),
                kv_norm=kv_norm.reshape(1, D_MODEL), b_norm=b_norm.reshape(1, D_MODEL), k_norm=k_norm, b_q_norm=b_q_norm)

    def shards(t):
        out = {}
        for n in BIG:
            for entry in ((n + "0", n + "1") if n.startswith("ple") else (n,)):
                out[entry] = _as_matrix(t[n])
        return out

    base = {n: 256 if n.startswith("ple") and n.endswith("1") else 0 for n in shards(wts)}
    state = (shards(wts), shards(mom), shards(var), base)
    grad_x, grads, updates, small_part = _local_step(x.reshape(S, D_MODEL), p, loss_target.reshape(S, D_MODEL),
                                                     full, local, state)

    names = sorted(grads)
    reduced, small_sum = _reduce_adam_all(
        [grads[n] for n in names], *[[t[n] for n in names] for t in state[:3]], small_part,
        bases=[base[n] for n in names], seeds=[updates.get(n[:-1] + "1") if n.startswith("ple") else None for n in names])
    for i, n in enumerate(names):
        updates[n] = tuple(group[i] for group in reduced)
    out_g, out_d, out_m, out_v = {}, {}, {}, {}
    for n in BIG:
        for i, out in enumerate((out_g, out_d, out_m, out_v)):
            out[n] = updates[n + "0" if n.startswith("ple") else n][i].reshape(wts[n].shape)

    loss = small_sum[len(SMALL), 0]
    small_rows = []
    for i, n in enumerate(SMALL):
        row = small_sum[i]
        if n in SMALL_SHARDED:
            row = lax.dynamic_slice(row, (chip * 256,), (256,))
        else:
            row = row[:wts[n].size]
        small_rows.append(row)
    g_small = _pack_small(small_rows)
    d_small, m_small, v_small = _adam_small(_pack_small([wts[n] for n in SMALL]), g_small,
                                            _pack_small([mom[n] for n in SMALL]), _pack_small([var[n] for n in SMALL]))
    for i, n in enumerate(SMALL):
        shape, size = wts[n].shape, wts[n].size
        out_g[n], out_d[n], out_m[n], out_v[n] = (t[i, :size].reshape(shape) for t in (g_small, d_small, m_small, v_small))

    return (loss, grad_x.reshape(1, S, D_MODEL), *[out_g[n] for n in WEIGHTS], *[out_d[n] for n in WEIGHTS],
            *[out_m[n] for n in WEIGHTS], *[out_v[n] for n in WEIGHTS])
```

```python
import functools

import jax
import jax.numpy as jnp
from jax import lax
from jax.experimental import pallas as pl
from jax.experimental.pallas import tpu as pltpu

F32 = jnp.float32
BF16 = jnp.bfloat16
MESH = pl.DeviceIdType.MESH

D_MODEL = 1024
N_HEADS = 16
HEAD_DIM = 64
PLE_DIM = 256
N_GROUPS = 4
GROUP_DIM = 256
POOL_WINDOWS = (2, 4, 8, 16)
N_CHIPS = 4
EPS = 1e-6
SB_SCALE = HEAD_DIM ** -0.5

ADAM_LR = 0.001
ADAM_B1 = 0.9
ADAM_B2 = 0.999
ADAM_EPS = 1e-08
ADAM_WD = 0.01
ADAM_STEP = 10

ROW_TILE = 256
WIDE_ROW_TILE = 512
EXP_UNDERFLOW = -104.0
ATT_Q_TILE = 512
ATT_K_TILE = 256
WGRAD_SEQ_TILE = 1024
WGRAD_ACC_BYTES = 4 * 1024 * 1024
MIB = 1024 * 1024


def _params(semantics=None, vmem_mib=48):
    return pltpu.CompilerParams(dimension_semantics=semantics, vmem_limit_bytes=vmem_mib * MIB)


def _dot(a, b):
    return jnp.dot(a, b, preferred_element_type=F32)


def _dot_nt(a, b):
    return lax.dot_general(a, b, (((1,), (1,)), ((), ())), preferred_element_type=F32)


def _dot_tn(a, b):
    return lax.dot_general(a, b, (((0,), (0,)), ((), ())), preferred_element_type=F32)


def _hilo(x):
    hi = x.astype(BF16)
    lo = (x - hi.astype(F32)).astype(BF16)
    return hi, lo


def _dot_hilo(x, w):
    hi, lo = _hilo(x)
    return _dot(hi, w) + _dot(lo, w)


def _sigmoid(z):
    return jax.nn.sigmoid(z)


def _dsilu(z, sg):
    return sg * (1.0 + z * (1.0 - sg))


def _mask_bf16(cond):
    return jnp.where(cond, 1.0, 0.0).astype(BF16)


def _head_mean_matrix():
    r = lax.broadcasted_iota(jnp.int32, (256, 256), 0) // HEAD_DIM
    c = lax.broadcasted_iota(jnp.int32, (256, 256), 1) // HEAD_DIM
    return _mask_bf16(r == c)


def _head_mean(x, bd):
    parts = []
    for s in range(x.shape[1] // 256):
        parts.append(_dot_hilo(x[:, s * 256:(s + 1) * 256], bd))
    out = parts[0] if len(parts) == 1 else jnp.concatenate(parts, axis=1)
    return out * (1.0 / HEAD_DIM)


def _a_in(x, gain, w_sh, gather=()):
    S = x.shape[0]
    tm = 512
    nsh, _, wn = w_sh.shape

    def body(x_ref, g_ref, w_ref, uz_ref, h_ref):
        @pl.when(pl.program_id(1) == 0)
        def _():
            xv = x_ref[...]
            r = lax.rsqrt(jnp.mean(xv * xv, axis=-1, keepdims=True) + EPS)
            h_ref[...] = (xv * r * g_ref[...]).astype(BF16)

        uz_ref[...] = _dot(h_ref[...], w_ref[0])

    return _call_with_gather(
        body, name="a_in", grid=(S // tm, nsh),
        in_specs=[pl.BlockSpec((tm, D_MODEL), lambda i, j: (i, 0)),
                  pl.BlockSpec((1, D_MODEL), lambda i, j: (0, 0)),
                  pl.BlockSpec((1, D_MODEL, wn), lambda i, j: (j, 0, 0))],
        out_specs=[pl.BlockSpec((tm, wn), lambda i, j: (i, j)),
                   pl.BlockSpec((tm, D_MODEL), lambda i, j: (i, 0))],
        out_shape=[jax.ShapeDtypeStruct((S, nsh * wn), F32),
                   jax.ShapeDtypeStruct((S, D_MODEL), BF16)],
        args=(x, gain, w_sh), gather=gather)


def _inv_count(first_row, rows, w):
    t1 = first_row + 1 + lax.broadcasted_iota(jnp.int32, (rows, 1), 0)
    return 1.0 / jnp.minimum(t1, w).astype(F32)


def _group_weight(wg_ref, g):
    return jnp.concatenate([wg_ref[sh, g] for sh in range(N_CHIPS)], axis=0)


def _a_mix(uz, wg, scale, gather=()):
    S = uz.shape[0]
    tm = ROW_TILE

    def body(u_ref, up_ref, z_ref, wg_ref, sc_ref, ga_ref, p_ref):
        i = pl.program_id(0)
        row = lax.broadcasted_iota(jnp.int32, (tm, tm), 0)
        col = lax.broadcasted_iota(jnp.int32, (tm, tm), 1)
        d = row - col
        for g, w in enumerate(POOL_WINDOWS):
            cols = slice(g * GROUP_DIM, (g + 1) * GROUP_DIM)
            t_main = _mask_bf16((d >= 0) & (d < w))
            t_halo = _mask_bf16(d + tm < w)
            u = u_ref[:, cols]
            up = jnp.where(i > 0, up_ref[:, cols], 0.0)
            hi, lo = _hilo(u)
            hip, lop = _hilo(up)
            wsum = _dot(t_main, hi) + _dot(t_main, lo) + _dot(t_halo, hip) + _dot(t_halo, lop)
            pooled = (wsum * _inv_count(i * tm, tm, w) - u).astype(BF16)
            p_ref[:, cols] = pooled
            mraw = _dot(pooled, _group_weight(wg_ref, g))
            z = z_ref[:, cols]
            ga_ref[:, cols] = (mraw * sc_ref[:, cols] * (z * _sigmoid(z))).astype(BF16)

    return _call_with_gather(
        body, name="a_mix", grid=(S // tm,),
        in_specs=[pl.BlockSpec((tm, D_MODEL), lambda i: (i, 0)),
                  pl.BlockSpec((tm, D_MODEL), lambda i: (jnp.maximum(i - 1, 0), 0)),
                  pl.BlockSpec((tm, D_MODEL), lambda i: (i, 1)),
                  pl.BlockSpec((N_CHIPS, N_GROUPS, 64, GROUP_DIM), lambda i: (0, 0, 0, 0)),
                  pl.BlockSpec((1, D_MODEL), lambda i: (0, 0))],
        out_specs=[pl.BlockSpec((tm, D_MODEL), lambda i: (i, 0)),
                   pl.BlockSpec((tm, D_MODEL), lambda i: (i, 0))],
        out_shape=[jax.ShapeDtypeStruct((S, D_MODEL), BF16),
                   jax.ShapeDtypeStruct((S, D_MODEL), BF16)],
        args=(uz, uz, uz, wg, scale), gather=gather)


def _out_ple(name, gated, x_in, w_out, p, layer, ple_w, ple_g, target=None, gather=()):
    S = x_in.shape[0]
    tm = WIDE_ROW_TILE
    with_loss = target is not None

    def body(*refs):
        if with_loss:
            g_ref, x_ref, wo_ref, p_ref, pw_ref, pg_ref, t_ref, xm_ref, dx_ref, e_ref, gt_ref, loss_ref = refs
        else:
            g_ref, x_ref, wo_ref, p_ref, pw_ref, pg_ref, xm_ref, xo_ref, e_ref, gt_ref = refs
        xm = x_ref[...] + _dot(g_ref[...], wo_ref[...])
        xm_ref[...] = xm
        pb = p_ref[...].astype(BF16)
        e = jnp.concatenate([_dot(pb, pw_ref[sh]) for sh in range(N_CHIPS)], axis=1)
        pg = jnp.concatenate([pg_ref[sh] for sh in range(N_CHIPS)], axis=0)
        gate = _sigmoid(_dot(xm.astype(BF16), pg))
        e_ref[...] = e.astype(BF16)
        gt_ref[...] = gate.astype(BF16)
        xo = xm + e * gate
        if with_loss:
            diff = xo - t_ref[...]
            dx_ref[...] = diff * (1.0 / D_MODEL)

            @pl.when(pl.program_id(0) == 0)
            def _():
                loss_ref[...] = jnp.zeros_like(loss_ref)

            loss_ref[...] += jnp.sum(diff * diff) * (0.5 / D_MODEL)
        else:
            xo_ref[...] = xo

    row = pl.BlockSpec((tm, D_MODEL), lambda i: (i, 0))
    in_specs = [row, row,
                pl.BlockSpec((D_MODEL, D_MODEL), lambda i: (0, 0)),
                pl.BlockSpec((None, None, tm, PLE_DIM), lambda i: (layer, 0, i, 0)),
                pl.BlockSpec((N_CHIPS, PLE_DIM, 256), lambda i: (0, 0, 0)),
                pl.BlockSpec((N_CHIPS, 256, D_MODEL), lambda i: (0, 0, 0))]
    args = [gated, x_in, w_out, p, ple_w, ple_g]
    out_specs = [row, row, row, row]
    out_shape = [jax.ShapeDtypeStruct((S, D_MODEL), F32), jax.ShapeDtypeStruct((S, D_MODEL), F32),
                 jax.ShapeDtypeStruct((S, D_MODEL), BF16), jax.ShapeDtypeStruct((S, D_MODEL), BF16)]
    if with_loss:
        in_specs.append(row)
        args.append(target)
        out_specs.append(pl.BlockSpec((8, 128), lambda i: (0, 0)))
        out_shape.append(jax.ShapeDtypeStruct((8, 128), F32))
    return _call_with_gather(body, name=name, grid=(S // tm,), in_specs=in_specs, out_specs=out_specs,
                             out_shape=out_shape, args=args, gather=gather)


def _b_in(x, kv_gain, b_gain, k_gain_t, q_gain_t, w_kv, w_in, gather=()):
    S = x.shape[0]
    tm = ROW_TILE

    def body(x_ref, kvg_ref, bg_ref, kg_ref, qg_ref, wkv_ref, win_ref,
             hkv_ref, hb_ref, kraw_ref, qraw_ref, k_ref, q_ref, v_ref, z_ref):
        xv = x_ref[...]
        y = xv * lax.rsqrt(jnp.mean(xv * xv, axis=-1, keepdims=True) + EPS)
        hkv = (y * kvg_ref[...]).astype(BF16)
        hb = (y * bg_ref[...]).astype(BF16)
        hkv_ref[...] = hkv
        hb_ref[...] = hb
        bd = _head_mean_matrix()

        def head_norm(raw, gain):
            rr = lax.rsqrt(_head_mean(raw * raw, bd) + EPS)
            return raw * rr * gain

        for sh in range(N_CHIPS):
            kvc = _dot(hkv, wkv_ref[sh])
            qzc = _dot(hb, win_ref[sh])
            cols = slice((sh % 2) * 512, (sh % 2) * 512 + 512)
            if sh < 2:
                kraw_ref[:, cols] = kvc.astype(BF16)
                qraw_ref[:, cols] = qzc.astype(BF16)
                k_ref[:, cols] = head_norm(kvc, kg_ref[:, cols]).astype(BF16)
                q_ref[:, cols] = (head_norm(qzc, qg_ref[:, cols]) * SB_SCALE).astype(BF16)
            else:
                v_ref[:, cols] = kvc.astype(BF16)
                z_ref[:, cols] = qzc.astype(BF16)

    row = pl.BlockSpec((tm, D_MODEL), lambda i: (i, 0))
    vec = pl.BlockSpec((1, D_MODEL), lambda i: (0, 0))
    wsp = pl.BlockSpec((N_CHIPS, D_MODEL, 512), lambda i: (0, 0, 0))
    return _call_with_gather(
        body, name="b_in", grid=(S // tm,),
        in_specs=[row, vec, vec, vec, vec, wsp, wsp],
        out_specs=[row] * 8,
        out_shape=[jax.ShapeDtypeStruct((S, D_MODEL), BF16)] * 8,
        args=(x, kv_gain, b_gain, k_gain_t, q_gain_t, w_kv, w_in), gather=gather)


def _softplus_parts(z):
    e = jnp.exp(-jnp.abs(z))
    return -(jnp.maximum(z, 0.0) + jnp.log(1.0 + e)), e


def _add_rows(total, rows, update):
    lo, hi = rows
    parts = ([total[:lo]] if lo else []) + [total[lo:hi] + update] + ([total[hi:]] if hi < total.shape[0] else [])
    return parts[0] if len(parts) == 1 else jnp.concatenate(parts, axis=0)


def _attn_fwd(q, k, v, zgate, gather=()):
    S = q.shape[0]
    tq, tk = ATT_Q_TILE, ATT_K_TILE
    kpq = tq // tk
    assert kpq == 2

    def body(q_ref, k_ref, v_ref, z_ref, o_ref, g_ref, lt_ref, steps_ref):
        qi = pl.program_id(1)
        lane = lax.broadcasted_iota(jnp.int32, (1, 128), 1)
        ri = lax.broadcasted_iota(jnp.int32, (tk, tk), 0)
        ci = lax.broadcasted_iota(jnp.int32, (tk, tk), 1)
        later_mat = _mask_bf16(ri > ci)
        causal = ci < ri
        qv = q_ref[...]
        first = lane < HEAD_DIM
        q_heads = (jnp.where(first, qv, jnp.zeros_like(qv)), jnp.where(first, jnp.zeros_like(qv), qv))

        def step(blocks, carry):
            chains = [(b, h) for b in range(len(blocks)) for h in range(2)]
            rows = [r for _, r, _ in blocks]
            s0 = [pl.multiple_of(kj * tk, tk) for kj, _, _ in blocks]
            kb = [k_ref[pl.ds(s, tk), :] for s in s0]
            vb = [v_ref[pl.ds(s, tk), :] for s in s0]
            visible = [causal if masked else None for _, _, masked in blocks]
            z = {c: _dot_nt(q_heads[c[1]][rows[c[0]][0]:rows[c[0]][1]], kb[c[0]]) for c in chains}
            run = [carry[0], carry[2]]
            log_own, later, run_at = {}, {}, {}
            for c in chains:
                b, h = c
                lk = _softplus_parts(z[c])[0]
                if visible[b] is not None:
                    lk = jnp.where(visible[b], lk, 0.0)
                log_own[c] = z[c] + lk
                later[c] = _dot(lk.astype(BF16), later_mat)
                run_at[c] = run[h][rows[b][0]:rows[b][1]]
                run[h] = _add_rows(run[h], rows[b], jnp.sum(lk, axis=-1, keepdims=True))
            acc = [carry[1], carry[3]]
            for c in chains:
                b, h = c
                a = jnp.exp(log_own[c] + later[c] + run_at[c])
                if visible[b] is not None:
                    a = jnp.where(visible[b], a, 0.0)
                acc[h] = _add_rows(acc[h], rows[b], _dot(a.astype(BF16), vb[b]))
            return run[0], acc[0], run[1], acc[1]

        zero1, zero128 = jnp.zeros((tq, 1), F32), jnp.zeros((tq, 128), F32)
        carry = step([(qi * kpq + 1, (tk, tq), True), (qi * kpq, (tk, tq), False), (qi * kpq, (0, tk), True)],
                     (zero1, zero128, zero1, zero128))

        def low(run):
            return jnp.max(run)

        def pair_more(c):
            return (c[0] < qi) & (jnp.maximum(low(c[1][tk:]), low(c[3][tk:])) > EXP_UNDERFLOW)

        def pair_step(c):
            last = (qi - c[0]) * kpq - 1
            return (c[0] + 1, *step([(last, (0, tq), False), (last - 1, (0, tq), False)], c[1:]))

        pairs, *carry = lax.while_loop(pair_more, pair_step, (jnp.int32(0), *carry))
        left = (qi - pairs) * kpq

        def single_more(c):
            return (c[0] < left) & (jnp.maximum(low(c[1][:tk]), low(c[3][:tk])) > EXP_UNDERFLOW)

        def single_step(c):
            return (c[0] + 1, *step([(left - 1 - c[0], (0, tk), False)], c[1:]))

        singles, *carry = lax.while_loop(single_more, single_step, (jnp.int32(0), *carry))
        steps_ref[...] = jnp.concatenate([jnp.full((4, 128), pairs, F32), jnp.full((4, 128), singles, F32)], axis=0)
        o_tot = jnp.where(first, carry[1], carry[3])
        l_tot = jnp.where(first, carry[0], carry[2])
        o_ref[...] = o_tot.astype(BF16)
        lt_ref[...] = l_tot
        zz = z_ref[...].astype(F32)
        g_ref[...] = (o_tot * (zz * _sigmoid(zz))).astype(BF16)

    blk = pl.BlockSpec((tq, 128), lambda hp, qi: (qi, hp))
    seq = pl.BlockSpec((S, 128), lambda hp, qi: (0, hp))
    return _call_with_gather(
        body, name="attn_fwd", grid=(D_MODEL // 128, S // tq),
        in_specs=[blk, seq, seq, blk],
        out_specs=[blk, blk, blk, pl.BlockSpec((None, None, 8, 128), lambda hp, qi: (hp, qi, 0, 0))],
        out_shape=[jax.ShapeDtypeStruct((S, D_MODEL), BF16)] * 2 + [jax.ShapeDtypeStruct((S, D_MODEL), F32)]
        + [jax.ShapeDtypeStruct((D_MODEL // 128, S // tq, 8, 128), F32)],
        args=(q, k, v, zgate), gather=gather)


def _ple_out_bwd(name, dx_out, e, gate, ple_g, w_out):
    S = dx_out.shape[0]
    tm = WIDE_ROW_TILE

    def body(dx_ref, e_ref, gt_ref, pg_ref, wo_ref, de_ref, dgp_ref, dxm_ref, dg_ref):
        dxo = dx_ref[...]
        ev = e_ref[...].astype(F32)
        gv = gt_ref[...].astype(F32)
        de_ref[...] = (dxo * gv).astype(BF16)
        dgp = (dxo * ev * gv * (1.0 - gv)).astype(BF16)
        dgp_ref[...] = dgp
        pg = jnp.concatenate([pg_ref[sh] for sh in range(N_CHIPS)], axis=0)
        dxm = dxo + _dot_nt(dgp, pg)
        dxm_ref[...] = dxm
        dg_ref[...] = _dot_nt(dxm.astype(BF16), wo_ref[...]).astype(BF16)

    row = pl.BlockSpec((tm, D_MODEL), lambda i: (i, 0))
    return pl.pallas_call(
        body, name=name, grid=(S // tm,),
        in_specs=[row, row, row,
                  pl.BlockSpec((N_CHIPS, 256, D_MODEL), lambda i: (0, 0, 0)),
                  pl.BlockSpec((D_MODEL, D_MODEL), lambda i: (0, 0))],
        out_specs=[row, row, row, row],
        out_shape=[jax.ShapeDtypeStruct((S, D_MODEL), BF16), jax.ShapeDtypeStruct((S, D_MODEL), BF16),
                   jax.ShapeDtypeStruct((S, D_MODEL), F32), jax.ShapeDtypeStruct((S, D_MODEL), BF16)],
        compiler_params=_params(("arbitrary",)),
    )(dx_out, e, gate, ple_g, w_out)


def _attn_bwd(q, k, v, ltot, steps, dgated, o, zgate, reduce=None):
    S = q.shape[0]
    tq, tk = ATT_Q_TILE, ATT_K_TILE
    kpq = tq // tk
    nq = S // tq

    def body(q_ref, k_ref, v_ref, lt_ref, steps_ref, dg_ref, o_ref, z_ref, dq_ref, dk_ref, dv_ref, dz_ref,
             dk_acc, dv_acc):
        qi = pl.program_id(1)

        @pl.when(qi == 0)
        def _():
            dk_acc[...] = jnp.zeros_like(dk_acc)
            dv_acc[...] = jnp.zeros_like(dv_acc)

        lane = lax.broadcasted_iota(jnp.int32, (1, 128), 1)
        ri = lax.broadcasted_iota(jnp.int32, (tk, tk), 0)
        ci = lax.broadcasted_iota(jnp.int32, (tk, tk), 1)
        later_mat = _mask_bf16(ri > ci)
        before_mat = _mask_bf16(ri < ci)
        causal = ci < ri
        zz = z_ref[...].astype(F32)
        sg = _sigmoid(zz)
        dgv = dg_ref[...].astype(F32)
        dz_ref[...] = (dgv * o_ref[...].astype(F32) * _dsilu(zz, sg)).astype(BF16)
        dob = (dgv * (zz * sg)).astype(BF16)
        ltv = lt_ref[...]
        qv = q_ref[...]
        first = lane < HEAD_DIM
        masks = (first, jnp.logical_not(first))
        q_heads = [jnp.where(hm, qv, jnp.zeros_like(qv)) for hm in masks]
        do_heads = [jnp.where(hm, dob, jnp.zeros_like(dob)) for hm in masks]
        totals = [jnp.max(jnp.where(hm, ltv, -jnp.inf), axis=-1, keepdims=True) for hm in masks]

        def step(blocks, carry):
            chains = [(b, h) for b in range(len(blocks)) for h in range(2)]
            rows = [r for _, r, _ in blocks]
            cut = lambda t, b: t[rows[b][0]:rows[b][1]]
            s0 = [pl.multiple_of(kj * tk, tk) for kj, _, _ in blocks]
            kb = [k_ref[pl.ds(s, tk), :] for s in s0]
            vb = [v_ref[pl.ds(s, tk), :] for s in s0]
            visible = [causal if masked else None for _, _, masked in blocks]
            z = {c: _dot_nt(cut(q_heads[c[1]], c[0]), kb[c[0]]) for c in chains}
            da = {c: _dot_nt(cut(do_heads[c[1]], c[0]), vb[c[0]]) for c in chains}
            run = [carry[0], carry[3]]
            log_own, beta, later, base = {}, {}, {}, {}
            for c in chains:
                b, h = c
                lk = _softplus_parts(z[c])[0]
                if visible[b] is not None:
                    lk = jnp.where(visible[b], lk, 0.0)
                log_own[c] = z[c] + lk
                beta[c] = jnp.exp(log_own[c]).astype(BF16)
                later[c] = _dot(lk.astype(BF16), later_mat)
                run[h] = _add_rows(run[h], rows[b], jnp.sum(lk, axis=-1, keepdims=True))
                base[c] = cut(totals[h] - run[h], b)
            grun = [carry[1], carry[4]]
            a_bf, g_bf, gbefore, grun_at = {}, {}, {}, {}
            for c in chains:
                b, h = c
                a = jnp.exp(log_own[c] + later[c] + base[c])
                if visible[b] is not None:
                    a = jnp.where(visible[b], a, 0.0)
                a_bf[c] = a.astype(BF16)
                g = da[c] * a
                g_bf[c] = g.astype(BF16)
                gbefore[c] = _dot(g_bf[c], before_mat)
                grun_at[c] = cut(grun[h], b)
                grun[h] = _add_rows(grun[h], rows[b], jnp.sum(g, axis=-1, keepdims=True))
            dq = [carry[2], carry[5]]
            dk_blk = [jnp.zeros((tk, 128), F32) for _ in blocks]
            dv_blk = [jnp.zeros((tk, 128), F32) for _ in blocks]
            for c in chains:
                b, h = c
                g = g_bf[c].astype(F32)
                dz = g - beta[c].astype(F32) * (g + gbefore[c] + grun_at[c])
                if visible[b] is not None:
                    dz = jnp.where(visible[b], dz, 0.0)
                dzb = dz.astype(BF16)
                dq[h] = _add_rows(dq[h], rows[b], _dot(dzb, kb[b]))
                dk_blk[b] = dk_blk[b] + _dot_tn(dzb, cut(q_heads[h], b))
                dv_blk[b] = dv_blk[b] + _dot_tn(a_bf[c], cut(do_heads[h], b))
            for b in range(len(blocks)):
                dk_acc[pl.ds(s0[b], tk), :] += dk_blk[b]
                dv_acc[pl.ds(s0[b], tk), :] += dv_blk[b]
            return run[0], grun[0], dq[0], run[1], grun[1], dq[1]

        pairs = jnp.clip(jnp.max(steps_ref[0:4, :]).astype(jnp.int32), 0, qi)
        left = (qi - pairs) * kpq
        singles = jnp.clip(jnp.max(steps_ref[4:8, :]).astype(jnp.int32), 0, left)
        zero1, zero128 = jnp.zeros((tq, 1), F32), jnp.zeros((tq, 128), F32)
        carry = lax.fori_loop(left - singles, left, lambda kj, c: step([(kj, (0, tk), False)], c),
                              (zero1, zero1, zero128, zero1, zero1, zero128))
        carry = lax.fori_loop(qi - pairs, qi,
                              lambda n, c: step([(n * kpq, (0, tq), False), (n * kpq + 1, (0, tq), False)], c), carry)
        carry = step([(qi * kpq, (0, tk), True), (qi * kpq, (tk, tq), False), (qi * kpq + 1, (tk, tq), True)], carry)
        dq_ref[...] = jnp.where(first, carry[2], carry[5]).astype(BF16)

        @pl.when(qi == nq - 1)
        def _():
            dk_ref[...] = dk_acc[...].astype(BF16)
            dv_ref[...] = dv_acc[...].astype(BF16)

    blk = pl.BlockSpec((tq, 128), lambda hp, qi: (qi, hp))
    seq = pl.BlockSpec((S, 128), lambda hp, qi: (0, hp))
    return _call_with_gather(
        body, name="attn_bwd", grid=(D_MODEL // 128, nq),
        in_specs=[blk, seq, seq, blk, pl.BlockSpec((None, None, 8, 128), lambda hp, qi: (hp, qi, 0, 0)),
                  blk, blk, blk],
        out_specs=[blk, seq, seq, blk],
        out_shape=[jax.ShapeDtypeStruct((S, D_MODEL), BF16)] * 4,
        scratch_shapes=[pltpu.VMEM((S, 128), F32), pltpu.VMEM((S, 128), F32)],
        args=(q, k, v, ltot, steps, dgated, o, zgate), reduce=reduce)


def _rms_bwd(xv, dh_gain_sum):
    r = lax.rsqrt(jnp.mean(xv * xv, axis=-1, keepdims=True) + EPS)
    xhat = xv * r
    dx = r * (dh_gain_sum - xhat * jnp.mean(dh_gain_sum * xhat, axis=-1, keepdims=True))
    return dx, xhat


def _b_in_bwd(dq, dk, dv, dz, q_raw, k_raw, x, dx_mid, q_gain_t, k_gain_t, b_gain, kv_gain, w_in, w_kv):
    S = x.shape[0]
    tm = ROW_TILE

    def body(dq_ref, dk_ref, dv_ref, dz_ref, qr_ref, kr_ref, x_ref, dxm_ref, qg_ref, kg_ref, bg_ref, kvg_ref,
             win_ref, wkv_ref, dqz_ref, dkv_ref, dx_ref, small_ref):
        @pl.when(pl.program_id(0) == 0)
        def _():
            small_ref[...] = jnp.zeros_like(small_ref)

        bd = _head_mean_matrix()

        def head_norm_bwd(dy_ref, raw_ref, gain, scale):
            raw = raw_ref[...].astype(F32)
            rr = lax.rsqrt(_head_mean(raw * raw, bd) + EPS)
            xhat = raw * rr
            dy = dy_ref[...].astype(F32) * scale
            gdy = dy * gain
            draw = rr * (gdy - xhat * _head_mean(gdy * xhat, bd))
            return draw.astype(BF16), jnp.sum(dy * xhat, axis=0, keepdims=True)

        dqr, dqg = head_norm_bwd(dq_ref, qr_ref, qg_ref[...], SB_SCALE)
        dkr, dkg = head_norm_bwd(dk_ref, kr_ref, kg_ref[...], 1.0)
        dqz_ref[:, :D_MODEL] = dqr
        dqz_ref[:, D_MODEL:] = dz_ref[...]
        dkv_ref[:, :D_MODEL] = dkr
        dkv_ref[:, D_MODEL:] = dv_ref[...]
        dhb = jnp.zeros((tm, D_MODEL), F32)
        dhkv = jnp.zeros((tm, D_MODEL), F32)
        for sh in range(N_CHIPS):
            cols = slice(sh * 512, (sh + 1) * 512)
            dhb = dhb + _dot_nt(dqz_ref[:, cols], win_ref[sh])
            dhkv = dhkv + _dot_nt(dkv_ref[:, cols], wkv_ref[sh])
        dx, xhat = _rms_bwd(x_ref[...], dhb * bg_ref[...] + dhkv * kvg_ref[...])
        dx_ref[...] = dxm_ref[...] + dx
        small_ref[0:1, :] += dqg
        small_ref[1:2, :] += dkg
        small_ref[2:3, :] += jnp.sum(dhb * xhat, axis=0, keepdims=True)
        small_ref[3:4, :] += jnp.sum(dhkv * xhat, axis=0, keepdims=True)

    row = pl.BlockSpec((tm, D_MODEL), lambda i: (i, 0))
    wide = pl.BlockSpec((tm, 2 * D_MODEL), lambda i: (i, 0))
    vec = pl.BlockSpec((1, D_MODEL), lambda i: (0, 0))
    wsp = pl.BlockSpec((N_CHIPS, D_MODEL, 512), lambda i: (0, 0, 0))
    return pl.pallas_call(
        body, name="b_in_bwd", grid=(S // tm,),
        in_specs=[row] * 8 + [vec] * 4 + [wsp, wsp],
        out_specs=[wide, wide, row, pl.BlockSpec((8, D_MODEL), lambda i: (0, 0))],
        out_shape=[jax.ShapeDtypeStruct((S, 2 * D_MODEL), BF16), jax.ShapeDtypeStruct((S, 2 * D_MODEL), BF16),
                   jax.ShapeDtypeStruct((S, D_MODEL), F32), jax.ShapeDtypeStruct((8, D_MODEL), F32)],
        compiler_params=_params(("arbitrary",)),
    )(dq, dk, dv, dz, q_raw, k_raw, x, dx_mid, q_gain_t, k_gain_t, b_gain, kv_gain, w_in, w_kv)


def _a_mix_bwd(dgated, uz, pooled, wg, scale, w_in, x, dx_mid, gain, reduce=None):
    S = x.shape[0]
    tm = ROW_TILE
    n = S // tm

    def body(dg_ref, z_ref, p_ref, wg_ref, sc_ref, win_ref, x_ref, dxm_ref, gn_ref,
             duz_ref, dmr_ref, dx_ref, small_ref, halo_hi, halo_lo):
        i = pl.program_id(0)

        @pl.when(i == 0)
        def _():
            small_ref[...] = jnp.zeros_like(small_ref)
            halo_hi[...] = jnp.zeros_like(halo_hi)
            halo_lo[...] = jnp.zeros_like(halo_lo)

        first_row = (n - 1 - i) * tm
        row = lax.broadcasted_iota(jnp.int32, (tm, tm), 0)
        col = lax.broadcasted_iota(jnp.int32, (tm, tm), 1)
        d = col - row
        for g, w in enumerate(POOL_WINDOWS):
            cols = slice(g * GROUP_DIM, (g + 1) * GROUP_DIM)
            wgg = _group_weight(wg_ref, g)
            sc = sc_ref[:, cols]
            mraw = _dot(p_ref[:, cols], wgg)
            z = z_ref[:, cols]
            sg = _sigmoid(z)
            dga = dg_ref[:, cols].astype(F32)
            dm = dga * (z * sg)
            duz_ref[:, D_MODEL + g * GROUP_DIM:D_MODEL + (g + 1) * GROUP_DIM] = (
                dga * (mraw * sc) * _dsilu(z, sg)).astype(BF16)
            small_ref[0:1, cols] += jnp.sum(dm * mraw, axis=0, keepdims=True)
            dmr = (dm * sc).astype(BF16)
            dmr_ref[:, cols] = dmr
            dp = _dot_nt(dmr, wgg)
            hi, lo = _hilo(dp * _inv_count(first_row, tm, w))
            t_main = _mask_bf16((d >= 0) & (d < w))
            t_halo = _mask_bf16(d + tm < w)
            du = (_dot(t_main, hi) + _dot(t_main, lo) + _dot(t_halo, halo_hi[:, cols]) + _dot(t_halo, halo_lo[:, cols])
                  - dp)
            halo_hi[:, cols] = hi
            halo_lo[:, cols] = lo
            duz_ref[:, cols] = du.astype(BF16)
        dh = jnp.zeros((tm, D_MODEL), F32)
        for sh in range(N_CHIPS):
            dh = dh + _dot_nt(duz_ref[:, sh * 512:(sh + 1) * 512], win_ref[sh])
        dx, xhat = _rms_bwd(x_ref[...], dh * gn_ref[...])
        dx_ref[...] = dxm_ref[...] + dx
        small_ref[1:2, :] += jnp.sum(dh * xhat, axis=0, keepdims=True)

    rev = lambda i: (n - 1 - i, 0)
    row = pl.BlockSpec((tm, D_MODEL), rev)
    vec = pl.BlockSpec((1, D_MODEL), lambda i: (0, 0))
    return _call_with_gather(
        body, name="a_mix_bwd", grid=(n,),
        in_specs=[row,
                  pl.BlockSpec((tm, D_MODEL), lambda i: (n - 1 - i, 1)),
                  row,
                  pl.BlockSpec((N_CHIPS, N_GROUPS, 64, GROUP_DIM), lambda i: (0, 0, 0, 0)),
                  vec,
                  pl.BlockSpec((N_CHIPS, D_MODEL, 512), lambda i: (0, 0, 0)),
                  row, row, vec],
        out_specs=[pl.BlockSpec((tm, 2 * D_MODEL), rev), row, row,
                   pl.BlockSpec((8, D_MODEL), lambda i: (0, 0))],
        out_shape=[jax.ShapeDtypeStruct((S, 2 * D_MODEL), BF16), jax.ShapeDtypeStruct((S, D_MODEL), BF16),
                   jax.ShapeDtypeStruct((S, D_MODEL), F32), jax.ShapeDtypeStruct((8, D_MODEL), F32)],
        scratch_shapes=[pltpu.VMEM((tm, D_MODEL), BF16), pltpu.VMEM((tm, D_MODEL), BF16)],
        args=(dgated, uz, pooled, wg, scale, w_in, x, dx_mid, gain), reduce=reduce)


def _wgrad(name, a, dy, n_shards, a_spec=None, k_dim=None):
    S, n_cols = dy.shape
    ts = WGRAD_SEQ_TILE
    k_dim = a.shape[-1] if k_dim is None else k_dim
    wn = n_cols // n_shards
    tk = min(k_dim, WGRAD_ACC_BYTES // (4 * n_cols))
    nst = S // ts

    def body(a_ref, dy_ref, out_ref, acc):
        st = pl.program_id(1)

        @pl.when(st == 0)
        def _():
            acc[...] = jnp.zeros_like(acc)

        acc[...] += _dot_tn(a_ref[...].astype(BF16), dy_ref[...].astype(BF16))

        @pl.when(st == nst - 1)
        def _():
            for sh in range(n_shards):
                out_ref[sh] = acc[:, sh * wn:(sh + 1) * wn]

    if a_spec is None:
        a_spec = pl.BlockSpec((ts, tk), lambda kt, st: (st, kt))
    return pl.pallas_call(
        body, name=name, grid=(k_dim // tk, nst),
        in_specs=[a_spec, pl.BlockSpec((ts, n_cols), lambda kt, st: (st, 0))],
        out_specs=pl.BlockSpec((n_shards, tk, wn), lambda kt, st: (0, kt, 0)),
        out_shape=jax.ShapeDtypeStruct((n_shards, k_dim, wn), F32),
        scratch_shapes=[pltpu.VMEM((tk, n_cols), F32)],
        compiler_params=_params(("parallel", "arbitrary")),
    )(a, dy)


def _wgrad_ple(name, p, layer, de):
    ts = WGRAD_SEQ_TILE
    spec = pl.BlockSpec((None, None, ts, PLE_DIM), lambda kt, st: (layer, 0, st, 0))
    return _wgrad(name, p, de, N_CHIPS, a_spec=spec, k_dim=PLE_DIM)


def _wgrad_group(pooled, dmr):
    S = pooled.shape[0]
    ts = WGRAD_SEQ_TILE
    nst = S // ts

    def body(p_ref, d_ref, out_ref, acc):
        st = pl.program_id(1)

        @pl.when(st == 0)
        def _():
            acc[...] = jnp.zeros_like(acc)

        acc[...] += _dot_tn(p_ref[...], d_ref[...])

        @pl.when(st == nst - 1)
        def _():
            for sh in range(N_CHIPS):
                out_ref[sh] = acc[sh * 64:(sh + 1) * 64, :]

    blk = pl.BlockSpec((ts, GROUP_DIM), lambda g, st: (st, g))
    return pl.pallas_call(
        body, name="wgrad_group", grid=(N_GROUPS, nst),
        in_specs=[blk, blk],
        out_specs=pl.BlockSpec((N_CHIPS, None, 64, GROUP_DIM), lambda g, st: (0, g, 0, 0)),
        out_shape=jax.ShapeDtypeStruct((N_CHIPS, N_GROUPS, 64, GROUP_DIM), F32),
        scratch_shapes=[pltpu.VMEM((GROUP_DIM, GROUP_DIM), F32)],
        compiler_params=_params(("parallel", "arbitrary")),
    )(pooled, dmr)


GATHER_AT = {
    "a_in": ("a_w_group", "a_w_out", "ple_w0", "ple_gate_w0"),
    "a_mix": ("w_kv",),
    "a_out_ple": ("b_w_in",),
    "attn_fwd": ("b_w_out", "ple_w1", "ple_gate_w1"),
}


REDUCE_AT = {
    "attn_bwd": ("b_w_out", "ple_w1", "ple_gate_w1"),
    "a_mix_bwd": ("a_w_out", "ple_w0", "ple_gate_w0"),
}


def _local_step(x, p, target, w, local=None, state=None):
    w = dict(w)

    def run(fn, host, n_out, *args, **kwargs):
        names = GATHER_AT[host] if local is not None else ()
        res = fn(*args, gather=[local[n] for n in names], **kwargs)
        w.update(zip(names, res[n_out:]))
        return res[:n_out]

    k_gain_t = jnp.tile(w["k_norm"].reshape(1, HEAD_DIM), (1, N_HEADS))
    q_gain_t = jnp.tile(w["b_q_norm"].reshape(1, HEAD_DIM), (1, N_HEADS))

    uz, h_a = run(_a_in, "a_in", 2, x, w["a_norm"], w["a_w_in"])
    wg4 = w["a_w_group"].reshape(N_CHIPS, N_GROUPS, 64, GROUP_DIM)
    wa_out = w["a_w_out"].reshape(D_MODEL, D_MODEL)
    gated_a, pooled = run(_a_mix, "a_mix", 2, uz, wg4, w["a_scale"])
    x1, x2, e_a, gate_a = run(_out_ple, "a_out_ple", 4, "a_out_ple", gated_a, x, wa_out, p, 0,
                              w["ple_w0"], w["ple_gate_w0"])
    h_kv, h_b, k_raw, q_raw, k, q, v, z_b = _b_in(
        x2, w["kv_norm"], w["b_norm"], k_gain_t, q_gain_t, w["w_kv"], w["b_w_in"])
    o, gated_b, ltot, att_steps = run(_attn_fwd, "attn_fwd", 4, q, k, v, z_b)
    wb_out = w["b_w_out"].reshape(D_MODEL, D_MODEL)
    x3, dx4, e_b, gate_b, loss_blk = _out_ple("b_out_ple", gated_b, x2, wb_out, p, 1, w["ple_w1"], w["ple_gate_w1"],
                                              target=target)

    grads, updates = {}, {}

    def hosted(fn, host, n_out, *args):
        if state is None:
            return fn(*args)
        names = REDUCE_AT[host]
        seeds = [updates.get(n[:-1] + "1") if n.startswith("ple") and n.endswith("0") else None for n in names]
        res = fn(*args, reduce=([grads.pop(n) for n in names], *[[t[n] for n in names] for t in state[:3]],
                                [state[3][n] for n in names], seeds))
        for i, n in enumerate(names):
            updates[n] = tuple(group[i] for group in res[n_out:])
        return res[:n_out]

    de_b, dgp_b, dx3, dgated_b = _ple_out_bwd("b_ple_out_bwd", dx4, e_b, gate_b, w["ple_gate_w1"], wb_out)
    grads["b_w_out"] = _wgrad("wgrad_b_out", gated_b, dx3, 1).reshape(N_CHIPS, 256, D_MODEL)
    grads["ple_w1"] = _wgrad_ple("wgrad_ple1", p, 1, de_b)
    grads["ple_gate_w1"] = _wgrad("wgrad_gate1", x3, dgp_b, 1).reshape(N_CHIPS, 256, D_MODEL)
    dq, dk, dv, dz_b = hosted(_attn_bwd, "attn_bwd", 4, q, k, v, ltot, att_steps, dgated_b, o, z_b)
    dqz, dkv, dx2, small_b = _b_in_bwd(dq, dk, dv, dz_b, q_raw, k_raw, x2, dx3, q_gain_t, k_gain_t,
                                       w["b_norm"], w["kv_norm"], w["b_w_in"], w["w_kv"])
    grads["w_kv"] = _wgrad("wgrad_kv", h_kv, dkv, N_CHIPS)
    grads["b_w_in"] = _wgrad("wgrad_b_in", h_b, dqz, N_CHIPS)
    de_a, dgp_a, dx1, dgated_a = _ple_out_bwd("a_ple_out_bwd", dx2, e_a, gate_a, w["ple_gate_w0"], wa_out)
    grads["a_w_out"] = _wgrad("wgrad_a_out", gated_a, dx1, 1).reshape(N_CHIPS, 256, D_MODEL)
    grads["ple_w0"] = _wgrad_ple("wgrad_ple0", p, 0, de_a)
    grads["ple_gate_w0"] = _wgrad("wgrad_gate0", x1, dgp_a, 1).reshape(N_CHIPS, 256, D_MODEL)
    duz, dmr, grad_x, small_a = hosted(_a_mix_bwd, "a_mix_bwd", 4, dgated_a, uz, pooled, wg4, w["a_scale"],
                                       w["a_w_in"], x, dx1, w["a_norm"])
    grads["a_w_in"] = _wgrad("wgrad_a_in", h_a, duz, N_CHIPS)
    grads["a_w_group"] = _wgrad_group(pooled, dmr).reshape(N_CHIPS, N_GROUPS * 64, GROUP_DIM)

    fold = lambda row: jnp.pad(row.reshape(N_HEADS, HEAD_DIM).sum(axis=0), (0, D_MODEL - HEAD_DIM))
    small = jnp.stack([small_a[1], small_a[0], small_b[3], small_b[2], fold(small_b[1]), fold(small_b[0]),
                       jnp.pad(loss_blk[0], (0, D_MODEL - loss_blk.shape[1])), jnp.zeros((D_MODEL,), F32)])
    return grad_x, grads, updates, small


def _mesh_place():
    x, y, c = lax.axis_index("x"), lax.axis_index("y"), lax.axis_index("c")
    other_chips = [(1 - x, y), (x, 1 - y), (1 - x, 1 - y)]
    return x, y, c, other_chips


def _gather_sems(n):
    return [pltpu.SemaphoreType.DMA((3 * n,)), pltpu.SemaphoreType.DMA((3 * n,)),
            pltpu.SemaphoreType.DMA((3 * n,)), pltpu.SemaphoreType.DMA((3 * n,)), pltpu.SemaphoreType.DMA((n,))]


def _gather_copies(srcs, outs, sems):
    send_far, recv_far, send_sib, recv_sib, local_sem = sems
    n = len(srcs)
    x, y, c, chips = _mesh_place()
    me = 2 * x + y
    sibling = (x, y, 1 - c)

    def half(k, which):
        rows = srcs[k].shape[0] // 2
        return pl.ds(pl.multiple_of(which * rows, 16), rows)

    local = [pltpu.make_async_copy(srcs[k], outs[k].at[me], local_sem.at[k]) for k in range(n)]
    far = [pltpu.make_async_remote_copy(
        src_ref=srcs[k].at[half(k, c)], dst_ref=outs[k].at[me, half(k, c)],
        send_sem=send_far.at[j * n + k], recv_sem=recv_far.at[j * n + k], device_id=(px, py, c), device_id_type=MESH)
        for j, (px, py) in enumerate(chips) for k in range(n)]

    def landed(j, k, which, from_far):
        px, py = chips[j]
        piece = outs[k].at[2 * px + py, half(k, which)]
        send, recv = (send_far, recv_far) if from_far else (send_sib, recv_sib)
        return pltpu.make_async_remote_copy(src_ref=piece, dst_ref=piece, send_sem=send.at[j * n + k],
                                            recv_sem=recv.at[j * n + k], device_id=sibling, device_id_type=MESH)

    return local, far, landed, c


def _gather_start(srcs, outs, sems):
    local, far, _, _ = _gather_copies(srcs, outs, sems)
    for cp in local + far:
        cp.start()


def _gather_pass_on(srcs, outs, sems):
    _, _, landed, c = _gather_copies(srcs, outs, sems)
    for j in range(3):
        for k in range(len(srcs)):
            landed(j, k, c, True).wait_recv()
            landed(j, k, c, False).start()


def _gather_finish(srcs, outs, sems):
    local, far, landed, c = _gather_copies(srcs, outs, sems)
    pairs = [(j, k) for j in range(3) for k in range(len(srcs))]
    for j, k in pairs:
        landed(j, k, 1 - c, False).wait_recv()
    for cp in far + [landed(j, k, c, False) for j, k in pairs]:
        cp.wait_send()
    for cp in local:
        cp.wait()


def _call_with_gather(body, *, name, grid, in_specs, out_specs, out_shape, args, gather=(), reduce=None,
                      scratch_shapes=(), vmem_mib=48):
    n_in, n_out, n_scr, n_g = len(args), len(out_shape), len(scratch_shapes), len(gather)
    n_r = len(reduce[0]) if reduce else 0
    pieces = _reduce_pieces(reduce[0], reduce[4]) if reduce else []
    reduce_args = [a for group in reduce[:4] for a in group] if reduce else []
    seeds = reduce[5] if reduce else []
    seeded = [(k, a) for k, seed in enumerate(seeds) if seed is not None for a in range(4)]
    gather_sems = _gather_sems(n_g) if n_g else []
    n_steps = 1
    for g in grid:
        n_steps *= g

    def wrapped(*refs):
        refs = list(refs)
        take = lambda count: [refs.pop(0) for _ in range(count)]
        ins, g_in, r_in = take(n_in), take(n_g), take(4 * n_r)
        take(len(seeded))
        outs, g_out, r_out = take(n_out), take(n_g), take(4 * n_r)
        scratch, sems, r_scratch = take(n_scr), take(len(gather_sems)), refs
        step = 0
        for axis, g in enumerate(grid):
            step = step * g + pl.program_id(axis)
        if n_g:
            @pl.when(step == 0)
            def _():
                _gather_start(g_in, g_out, sems)

        if n_r:
            ticks, drain = _reduce_ticks(pieces, n_r, (*r_in, *r_out, *r_scratch))
            for t, tick in enumerate(ticks[:n_steps]):
                pl.when(step == t)(tick)

        body(*ins, *outs, *scratch)
        if n_r:
            for tick in ticks[n_steps:]:
                pl.when(step == n_steps - 1)(tick)
            pl.when(step == n_steps - 1)(drain)
        if n_g:
            @pl.when(step == max(n_steps - 2, 0))
            def _():
                _gather_pass_on(g_in, g_out, sems)

            @pl.when(step == n_steps - 1)
            def _():
                _gather_finish(g_in, g_out, sems)

    hbm = pl.BlockSpec(memory_space=pltpu.HBM)
    res = pl.pallas_call(
        wrapped, name=name, grid=grid,
        in_specs=list(in_specs) + [hbm] * (n_g + 4 * n_r + len(seeded)),
        out_specs=list(out_specs) + [hbm] * (n_g + 4 * n_r),
        out_shape=list(out_shape) + [jax.ShapeDtypeStruct((N_CHIPS,) + g.shape, BF16) for g in gather]
        + ([jax.ShapeDtypeStruct(w.shape, F32) for _ in range(4) for w in reduce[1]] if reduce else []),
        input_output_aliases={n_in + n_g + 4 * n_r + i: n_out + n_g + a * n_r + k for i, (k, a) in enumerate(seeded)},
        scratch_shapes=list(scratch_shapes) + gather_sems + (_reduce_scratch() if reduce else []),
        compiler_params=_params(("arbitrary",) * len(grid), vmem_mib),
    )(*args, *gather, *reduce_args, *[seeds[k][a] for k, a in seeded])
    if not reduce:
        return res
    plain = list(res[:n_out + n_g])
    return plain + [res[n_out + n_g + i * n_r:n_out + n_g + (i + 1) * n_r] for i in range(4)]


def _allgather_weights(shards, small, casts):
    n = len(shards)
    cast_out = [(k, r0, r1) for k, (_, ranges) in enumerate(casts) for r0, r1 in ranges]
    n_c, n_co = len(casts), len(cast_out)

    def body(*refs):
        ins, small_in, cast_in = refs[:n], refs[n], refs[n + 1:n + 1 + n_c]
        refs = refs[n + 1 + n_c:]
        outs, small_out, cast_dst = refs[:n], refs[n], refs[n + 1:n + 1 + n_co]
        refs = refs[n + 1 + n_co:]
        cast, cast_buf = refs[:n], refs[n:n + n_co]
        send_far, recv_far, send_sib, recv_sib, send_small, recv_small, local_sem, cast_sem = refs[n + n_co:]
        x, y, c, chips = _mesh_place()
        me = 2 * x + y
        sibling = (x, y, 1 - c)

        def half(k, which):
            rows = ins[k].shape[0] // 2
            return pl.ds(pl.multiple_of(which * rows, 16), rows)

        local = []
        for k in range(n):
            cast[k][...] = ins[k][...].astype(BF16)
            local.append(pltpu.make_async_copy(cast[k], outs[k].at[me], local_sem.at[k]))
            local[-1].start()
        local.append(pltpu.make_async_copy(small_in, small_out.at[me], local_sem.at[n]))
        local[-1].start()

        sends = []
        for j, (px, py) in enumerate(chips):
            for k in range(n):
                cp = pltpu.make_async_remote_copy(
                    src_ref=cast[k].at[half(k, c)], dst_ref=outs[k].at[me, half(k, c)],
                    send_sem=send_far.at[j * n + k], recv_sem=recv_far.at[j * n + k],
                    device_id=(px, py, c), device_id_type=MESH)
                cp.start()
                sends.append(cp)
            cp = pltpu.make_async_remote_copy(
                src_ref=small_in, dst_ref=small_out.at[me], send_sem=send_small.at[j], recv_sem=recv_small.at[j],
                device_id=(px, py, c), device_id_type=MESH)
            cp.start()
            sends.append(cp)

        for i, (k, r0, r1) in enumerate(cast_out):
            cast_buf[i][...] = cast_in[k][r0:r1, :].astype(BF16)
            local.append(pltpu.make_async_copy(cast_buf[i], cast_dst[i], cast_sem.at[i]))
            local[-1].start()

        def landed(j, k, which, sems_s, sems_r, device):
            px, py = chips[j]
            piece = outs[k].at[2 * px + py, half(k, which)]
            return pltpu.make_async_remote_copy(
                src_ref=piece, dst_ref=piece, send_sem=sems_s.at[j * n + k], recv_sem=sems_r.at[j * n + k],
                device_id=device, device_id_type=MESH)

        for j in range(len(chips)):
            for k in range(n):
                landed(j, k, c, send_far, recv_far, sibling).wait_recv()
                cp = landed(j, k, c, send_sib, recv_sib, sibling)
                cp.start()
                sends.append(cp)
        for j, (px, py) in enumerate(chips):
            for k in range(n):
                landed(j, k, 1 - c, send_sib, recv_sib, sibling).wait_recv()
            pltpu.make_async_remote_copy(
                src_ref=small_in, dst_ref=small_out.at[2 * px + py], send_sem=send_small.at[j],
                recv_sem=recv_small.at[j], device_id=(px, py, c), device_id_type=MESH).wait_recv()
        for cp in sends:
            cp.wait_send()
        for cp in local:
            cp.wait()

    vmem = pl.BlockSpec(memory_space=pltpu.VMEM)
    hbm = pl.BlockSpec(memory_space=pltpu.HBM)
    cast_shapes = [(r1 - r0, casts[k][0].shape[1]) for k, r0, r1 in cast_out]
    res = pl.pallas_call(
        body, name="allgather_weights",
        in_specs=[vmem] * (n + 1 + n_c), out_specs=[hbm] * (n + 1 + n_co),
        out_shape=[jax.ShapeDtypeStruct((N_CHIPS,) + s.shape, BF16) for s in shards]
        + [jax.ShapeDtypeStruct((N_CHIPS,) + small.shape, F32)]
        + [jax.ShapeDtypeStruct(s, BF16) for s in cast_shapes],
        scratch_shapes=[pltpu.VMEM(s.shape, BF16) for s in shards] + [pltpu.VMEM(s, BF16) for s in cast_shapes]
        + [pltpu.SemaphoreType.DMA((3 * n,)), pltpu.SemaphoreType.DMA((3 * n,)),
           pltpu.SemaphoreType.DMA((3 * n,)), pltpu.SemaphoreType.DMA((3 * n,)),
           pltpu.SemaphoreType.DMA((3,)), pltpu.SemaphoreType.DMA((3,)),
           pltpu.SemaphoreType.DMA((n + 1,)), pltpu.SemaphoreType.DMA((n_co,))],
        compiler_params=_params(None, 40),
    )(*shards, small, *[a for a, _ in casts])
    return res[:n], res[n], res[n + 1:]


def _adamw(w, g, m, v):
    m = ADAM_B1 * m + (1.0 - ADAM_B1) * g
    v = ADAM_B2 * v + (1.0 - ADAM_B2) * (g * g)
    m_hat = m / (1.0 - ADAM_B1 ** ADAM_STEP)
    v_hat = v / (1.0 - ADAM_B2 ** ADAM_STEP)
    delta = -ADAM_LR * (m_hat / (jnp.sqrt(v_hat) + ADAM_EPS) + ADAM_WD * w)
    return delta, m, v


RS_PIECE_ROWS = 128
RS_PIECE_COLS = 512


def _reduce_adam_all(grads, ws, ms, vs, small, bases=None, seeds=None):
    n_w = len(grads)
    pieces = _reduce_pieces(grads, bases)
    lanes = [pieces[0::2], pieces[1::2]]
    n_lane = len(_reduce_scratch())
    n_small = len(_small_sum_scratch(small.shape))
    seeds = seeds or [None] * n_w
    seeded = [(k, a) for k, seed in enumerate(seeds) if seed is not None for a in range(4)]
    n_in = 4 * n_w + 1

    def body(*refs):
        refs = list(refs)
        del refs[n_in:n_in + len(seeded)]
        small_in = refs.pop(4 * n_w)
        small_out = refs.pop(8 * n_w)
        small_scratch = [refs.pop() for _ in range(n_small)][::-1]
        sends = _small_sum_start(small_in, *small_scratch)
        arrays, scratch = refs[:8 * n_w], refs[8 * n_w:]
        runs = [_reduce_ticks(lane, n_w, arrays + scratch[i * n_lane:(i + 1) * n_lane])
                for i, lane in enumerate(lanes) if lane]
        for t in range(max(len(ticks) for ticks, _ in runs)):
            for ticks, _ in runs:
                if t < len(ticks):
                    ticks[t]()
        for _, drain in runs:
            drain()
        _small_sum_finish(sends, small_scratch[0], small_out)

    hbm = pl.BlockSpec(memory_space=pltpu.HBM)
    vmem = pl.BlockSpec(memory_space=pltpu.VMEM)
    outs = pl.pallas_call(
        body, name="reduce_adam_all",
        in_specs=[hbm] * (4 * n_w) + [vmem] + [hbm] * len(seeded), out_specs=[hbm] * (4 * n_w) + [vmem],
        out_shape=[jax.ShapeDtypeStruct(w.shape, F32) for _ in range(4) for w in ws]
        + [jax.ShapeDtypeStruct(small.shape, F32)],
        input_output_aliases={n_in + i: a * n_w + k for i, (k, a) in enumerate(seeded)},
        scratch_shapes=_reduce_scratch() * len(lanes) + _small_sum_scratch(small.shape),
        compiler_params=_params(None, 48),
    )(*grads, *ws, *ms, *vs, small, *[seeds[k][a] for k, a in seeded])
    return [outs[i * n_w:(i + 1) * n_w] for i in range(4)], outs[4 * n_w]


def _reduce_pieces(grads, bases=None):
    pieces = []
    for k, g in enumerate(grads):
        hr, cols = g.shape[1] // 2, g.shape[2]
        pr, pc = min(hr, RS_PIECE_ROWS), min(cols, RS_PIECE_COLS)
        base = bases[k] if bases else 0
        pieces += [(k, ro, hr, co, pr, pc, base) for ro in range(0, hr, pr) for co in range(0, cols, pc)]
    return pieces


def _reduce_scratch():
    P, C = RS_PIECE_ROWS, RS_PIECE_COLS
    return [
        pltpu.VMEM((3, N_CHIPS, P, C), F32), pltpu.VMEM((3, N_CHIPS, P, C), F32),
        pltpu.VMEM((2, N_CHIPS, P, C), BF16), pltpu.VMEM((2, N_CHIPS, P, C), BF16),
        pltpu.VMEM((2, N_CHIPS, P, C), F32),
        pltpu.VMEM((2, 3, P, C), BF16), pltpu.VMEM((2, 3, P, C), BF16),
        pltpu.VMEM((2, 2, P, C), F32),
        pltpu.VMEM((2, 3, 2, P, C), F32), pltpu.VMEM((2, 4, 2, P, C), F32),
        pltpu.SemaphoreType.DMA((3, 2)), pltpu.SemaphoreType.DMA((2, 3, 2)),
        pltpu.SemaphoreType.DMA((2,)), pltpu.SemaphoreType.DMA((2,)),
        pltpu.SemaphoreType.DMA((2, 3)), pltpu.SemaphoreType.DMA((2, 3)),
        pltpu.SemaphoreType.DMA((2,)), pltpu.SemaphoreType.DMA((2,)),
        pltpu.SemaphoreType.DMA((2, 4, 2))]


def _reduce_ticks(pieces, n_w, refs):
    n = len(pieces)

    def build(*refs):
        g_in, w_in, m_in, v_in = (refs[i * n_w:(i + 1) * n_w] for i in range(4))
        g_out, d_out, m_out, v_out = (refs[(4 + i) * n_w:(5 + i) * n_w] for i in range(4))
        (gm, go, sb1, rb1, part, sb2, rb2, fin, wmv, outs,
         ld_sem, wmv_sem, s1_send, s1_recv, s2_send, s2_recv, s3_send, s3_recv, out_sem) = refs[8 * n_w:]
        x, y, c, chips = _mesh_place()
        me = 2 * x + y
        sibling = (x, y, 1 - c)

        def at_hbm(i, which, in_shard):
            _, ro, hr, co, pr, pc, base = pieces[i]
            half = c if which == 0 else 1 - c
            return pl.ds(pl.multiple_of((base if in_shard else 0) + half * hr + ro, 64), pr), pl.ds(co, pc)

        def win(i):
            return pl.ds(0, pieces[i][4]), pl.ds(0, pieces[i][5])

        every = slice(None)

        def loads(i):
            k, s = pieces[i][0], i % 3
            return [pltpu.make_async_copy(g_in[k].at[(every,) + at_hbm(i, h, False)], buf.at[(s, every) + win(i)],
                                          ld_sem.at[s, h])
                    for h, buf in enumerate((gm, go))]

        def wmv_loads(i):
            k, s = pieces[i][0], i % 2
            return [pltpu.make_async_copy(src[k].at[at_hbm(i, h, True)], wmv.at[(s, a, h) + win(i)], wmv_sem.at[s, a, h])
                    for a, src in enumerate((w_in, m_in, v_in)) for h in range(2)]

        def stores(i):
            k, s = pieces[i][0], i % 2
            return [pltpu.make_async_copy(outs.at[(s, a, h) + win(i)], dst[k].at[at_hbm(i, h, True)], out_sem.at[s, a, h])
                    for a, dst in enumerate((g_out, d_out, m_out, v_out)) for h in range(2)]

        def swap1(i):
            s = i % 2
            return pltpu.make_async_remote_copy(
                src_ref=sb1.at[(s, every) + win(i)], dst_ref=rb1.at[(s, every) + win(i)],
                send_sem=s1_send.at[s], recv_sem=s1_recv.at[s], device_id=sibling, device_id_type=MESH)

        def far2(i, j):
            s = i % 2
            px, py = chips[j]
            return pltpu.make_async_remote_copy(
                src_ref=sb2.at[(s, j) + win(i)], dst_ref=rb2.at[(s, j) + win(i)],
                send_sem=s2_send.at[s, j], recv_sem=s2_recv.at[s, j], device_id=(px, py, c), device_id_type=MESH)

        def swap3(i):
            s = i % 2
            return pltpu.make_async_remote_copy(
                src_ref=fin.at[(s, 0) + win(i)], dst_ref=fin.at[(s, 1) + win(i)],
                send_sem=s3_send.at[s], recv_sem=s3_recv.at[s], device_id=sibling, device_id_type=MESH)

        def stage0(i):
            for cp in loads(i):
                cp.start()

        def stage1(i):
            s, s3 = i % 2, i % 3
            for cp in loads(i):
                cp.wait()
            sb1[(s, every) + win(i)] = go[(s3, every) + win(i)].astype(BF16)
            swap1(i).start()

        def stage2(i):
            s, s3 = i % 2, i % 3
            swap1(i).wait()
            part[(s, every) + win(i)] = gm[(s3, every) + win(i)] + rb1[(s, every) + win(i)].astype(F32)
            for j, (px, py) in enumerate(chips):
                sb2[(s, j) + win(i)] = part[(s, 2 * px + py) + win(i)].astype(BF16)
                far2(i, j).start()

        def stage3(i):
            s = i % 2
            total = part[(s, me) + win(i)]
            for j in range(3):
                far2(i, j).wait()
                total = total + rb2[(s, j) + win(i)].astype(F32)
            fin[(s, 0) + win(i)] = total
            swap3(i).start()
            for cp in wmv_loads(i):
                cp.start()

        def stage4(i):
            s = i % 2
            if i >= 2:
                for cp in stores(i - 2):
                    cp.wait()
            swap3(i).wait()
            for cp in wmv_loads(i):
                cp.wait()
            both = (every,) + win(i)
            g = fin[(s,) + both]
            delta, m_new, v_new = _adamw(wmv[(s, 0) + both], g, wmv[(s, 1) + both], wmv[(s, 2) + both])
            outs[(s, 0) + both] = g
            outs[(s, 1) + both] = delta
            outs[(s, 2) + both] = m_new
            outs[(s, 3) + both] = v_new
            for cp in stores(i):
                cp.start()

        stages = (stage0, stage1, stage2, stage3, stage4)

        def tick(t):
            for age in reversed(range(len(stages))):
                if 0 <= t - age < n:
                    stages[age](t - age)

        def drain():
            for i in range(max(0, n - 2), n):
                for cp in stores(i):
                    cp.wait()

        return [functools.partial(tick, t) for t in range(n + len(stages) - 1)], drain

    return build(*refs)


N_DEVICES = 8


def _small_sum_scratch(shape):
    return [pltpu.VMEM((N_DEVICES,) + shape, F32),
            pltpu.SemaphoreType.DMA((N_DEVICES - 1,)), pltpu.SemaphoreType.DMA((N_DEVICES - 1,))]


def _small_sum_start(part_ref, buf, send_sem, recv_sem):
    x, y, c, _ = _mesh_place()
    me = 4 * x + 2 * y + c
    buf[me] = part_ref[...]
    sends = []
    for k in range(1, N_DEVICES):
        peer = ((1 - x) if k & 4 else x, (1 - y) if k & 2 else y, (1 - c) if k & 1 else c)
        cp = pltpu.make_async_remote_copy(src_ref=part_ref, dst_ref=buf.at[me], send_sem=send_sem.at[k - 1],
                                          recv_sem=recv_sem.at[k - 1], device_id=peer, device_id_type=MESH)
        cp.start()
        sends.append(cp)
    return sends


def _small_sum_finish(sends, buf, out_ref):
    for cp in sends:
        cp.wait_recv()
    total = buf[0]
    for s in range(1, N_DEVICES):
        total = total + buf[s]
    out_ref[...] = total
    for cp in sends:
        cp.wait_send()


def _adam_small(w, g, m, v):
    def body(w_ref, g_ref, m_ref, v_ref, d_ref, mo_ref, vo_ref):
        delta, m_new, v_new = _adamw(w_ref[...], g_ref[...], m_ref[...], v_ref[...])
        d_ref[...] = delta
        mo_ref[...] = m_new
        vo_ref[...] = v_new

    vmem = pl.BlockSpec(memory_space=pltpu.VMEM)
    return pl.pallas_call(
        body, name="adam_small", in_specs=[vmem] * 4, out_specs=[vmem] * 3,
        out_shape=[jax.ShapeDtypeStruct(w.shape, F32)] * 3,
    )(w, g, m, v)


BIG = ("a_w_in", "a_w_group", "a_w_out", "w_kv", "b_w_in", "b_w_out", "ple_w", "ple_gate_w")
SMALL = ("a_norm", "a_scale", "kv_norm", "b_norm", "k_norm", "b_q_norm")
SMALL_SHARDED = ("a_norm", "a_scale")
WEIGHTS = ("a_norm", "a_w_in", "a_w_group", "a_scale", "a_w_out", "kv_norm", "w_kv", "k_norm", "b_norm", "b_w_in",
           "b_q_norm", "b_w_out", "ple_w", "ple_gate_w")


def _as_matrix(a):
    return a.reshape(-1, a.shape[-1])


def _pack_small(arrs):
    rows = [jnp.pad(a.reshape(-1), (0, D_MODEL - a.size)) for a in arrs]
    rows += [jnp.zeros((D_MODEL,), F32)] * (8 - len(rows))
    return jnp.stack(rows)


def kernel(x, p, a_norm, a_w_in, a_w_group, a_scale, a_w_out, kv_norm, w_kv, k_norm, b_norm, b_w_in, b_q_norm, b_w_out, ple_w, ple_gate_w, loss_target, m_a_norm, m_a_w_in, m_a_w_group, m_a_scale, m_a_w_out, m_kv_norm, m_w_kv, m_k_norm, m_b_norm, m_b_w_in, m_b_q_norm, m_b_w_out, m_ple_w, m_ple_gate_w, v_a_norm, v_a_w_in, v_a_w_group, v_a_scale, v_a_w_out, v_kv_norm, v_w_kv, v_k_norm, v_b_norm, v_b_w_in, v_b_q_norm, v_b_w_out, v_ple_w, v_ple_gate_w):
    wts = dict(a_norm=a_norm, a_w_in=a_w_in, a_w_group=a_w_group, a_scale=a_scale, a_w_out=a_w_out, kv_norm=kv_norm,
               w_kv=w_kv, k_norm=k_norm, b_norm=b_norm, b_w_in=b_w_in, b_q_norm=b_q_norm, b_w_out=b_w_out,
               ple_w=ple_w, ple_gate_w=ple_gate_w)
    mom = dict(a_norm=m_a_norm, a_w_in=m_a_w_in, a_w_group=m_a_w_group, a_scale=m_a_scale, a_w_out=m_a_w_out,
               kv_norm=m_kv_norm, w_kv=m_w_kv, k_norm=m_k_norm, b_norm=m_b_norm, b_w_in=m_b_w_in,
               b_q_norm=m_b_q_norm, b_w_out=m_b_w_out, ple_w=m_ple_w, ple_gate_w=m_ple_gate_w)
    var = dict(a_norm=v_a_norm, a_w_in=v_a_w_in, a_w_group=v_a_w_group, a_scale=v_a_scale, a_w_out=v_a_w_out,
               kv_norm=v_kv_norm, w_kv=v_w_kv, k_norm=v_k_norm, b_norm=v_b_norm, b_w_in=v_b_w_in,
               b_q_norm=v_b_q_norm, b_w_out=v_b_w_out, ple_w=v_ple_w, ple_gate_w=v_ple_gate_w)
    S = x.shape[1]
    chip = 2 * lax.axis_index("x") + lax.axis_index("y")

    sharded_small = jnp.concatenate([a_norm.reshape(1, 256), a_scale.reshape(1, 256), jnp.zeros((6, 256), F32)], axis=0)
    later = ("a_w_group", "a_w_out", "w_kv", "b_w_in", "b_w_out", "ple_w", "ple_gate_w")
    (a_w_in_full,), small_full, copies = _allgather_weights(
        [_as_matrix(a_w_in)], sharded_small,
        [(_as_matrix(wts[n]), [(0, 256), (256, 512)] if n.startswith("ple") else [(0, _as_matrix(wts[n]).shape[0])])
         for n in later])
    local = dict(zip(("a_w_group", "a_w_out", "w_kv", "b_w_in", "b_w_out", "ple_w0", "ple_w1", "ple_gate_w0",
                      "ple_gate_w1"), copies))
    full = dict(a_w_in=a_w_in_full,
                a_norm=small_full[:, 0, :].reshape(1, D_MODEL), a_scale=small_full[:, 1, :].reshape(1, D_MODEL),
                kv_norm=kv_norm.reshape(1, D_MODEL), b_norm=b_norm.reshape(1, D_MODEL), k_norm=k_norm, b_q_norm=b_q_norm)

    def shards(t):
        out = {}
        for n in BIG:
            for entry in ((n + "0", n + "1") if n.startswith("ple") else (n,)):
                out[entry] = _as_matrix(t[n])
        return out

    base = {n: 256 if n.startswith("ple") and n.endswith("1") else 0 for n in shards(wts)}
    state = (shards(wts), shards(mom), shards(var), base)
    grad_x, grads, updates, small_part = _local_step(x.reshape(S, D_MODEL), p, loss_target.reshape(S, D_MODEL),
                                                     full, local, state)

    names = sorted(grads)
    reduced, small_sum = _reduce_adam_all(
        [grads[n] for n in names], *[[t[n] for n in names] for t in state[:3]], small_part,
        bases=[base[n] for n in names], seeds=[updates.get(n[:-1] + "1") if n.startswith("ple") else None for n in names])
    for i, n in enumerate(names):
        updates[n] = tuple(group[i] for group in reduced)
    out_g, out_d, out_m, out_v = {}, {}, {}, {}
    for n in BIG:
        for i, out in enumerate((out_g, out_d, out_m, out_v)):
            out[n] = updates[n + "0" if n.startswith("ple") else n][i].reshape(wts[n].shape)

    loss = small_sum[len(SMALL), 0]
    small_rows = []
    for i, n in enumerate(SMALL):
        row = small_sum[i]
        if n in SMALL_SHARDED:
            row = lax.dynamic_slice(row, (chip * 256,), (256,))
        else:
            row = row[:wts[n].size]
        small_rows.append(row)
    g_small = _pack_small(small_rows)
    d_small, m_small, v_small = _adam_small(_pack_small([wts[n] for n in SMALL]), g_small,
                                            _pack_small([mom[n] for n in SMALL]), _pack_small([var[n] for n in SMALL]))
    for i, n in enumerate(SMALL):
        shape, size = wts[n].shape, wts[n].size
        out_g[n], out_d[n], out_m[n], out_v[n] = (t[i, :size].reshape(shape) for t in (g_small, d_small, m_small, v_small))

    return (loss, grad_x.reshape(1, S, D_MODEL), *[out_g[n] for n in WEIGHTS], *[out_d[n] for n in WEIGHTS],
            *[out_m[n] for n in WEIGHTS], *[out_v[n] for n in WEIGHTS])
```

```python
import functools

import jax
import jax.numpy as jnp
from jax import lax
from jax.experimental import pallas as pl
from jax.experimental.pallas import tpu as pltpu

F32 = jnp.float32
BF16 = jnp.bfloat16
MESH = pl.DeviceIdType.MESH

D_MODEL = 1024
N_HEADS = 16
HEAD_DIM = 64
PLE_DIM = 256
N_GROUPS = 4
GROUP_DIM = 256
POOL_WINDOWS = (2, 4, 8, 16)
N_CHIPS = 4
EPS = 1e-6
SB_SCALE = HEAD_DIM ** -0.5

ADAM_LR = 0.001
ADAM_B1 = 0.9
ADAM_B2 = 0.999
ADAM_EPS = 1e-08
ADAM_WD = 0.01
ADAM_STEP = 10

ROW_TILE = 256
WIDE_ROW_TILE = 512
EXP_UNDERFLOW = -104.0
ATT_Q_TILE = 512
ATT_K_TILE = 256
WGRAD_SEQ_TILE = 1024
WGRAD_ACC_BYTES = 4 * 1024 * 1024
MIB = 1024 * 1024


def _params(semantics=None, vmem_mib=48):
    return pltpu.CompilerParams(dimension_semantics=semantics, vmem_limit_bytes=vmem_mib * MIB)


def _dot(a, b):
    return jnp.dot(a, b, preferred_element_type=F32)


def _dot_nt(a, b):
    return lax.dot_general(a, b, (((1,), (1,)), ((), ())), preferred_element_type=F32)


def _dot_tn(a, b):
    return lax.dot_general(a, b, (((0,), (0,)), ((), ())), preferred_element_type=F32)


def _hilo(x):
    hi = x.astype(BF16)
    lo = (x - hi.astype(F32)).astype(BF16)
    return hi, lo


def _dot_hilo(x, w):
    hi, lo = _hilo(x)
    return _dot(hi, w) + _dot(lo, w)


def _sigmoid(z):
    return jax.nn.sigmoid(z)


def _dsilu(z, sg):
    return sg * (1.0 + z * (1.0 - sg))


def _mask_bf16(cond):
    return jnp.where(cond, 1.0, 0.0).astype(BF16)


def _head_mean_matrix():
    r = lax.broadcasted_iota(jnp.int32, (256, 256), 0) // HEAD_DIM
    c = lax.broadcasted_iota(jnp.int32, (256, 256), 1) // HEAD_DIM
    return _mask_bf16(r == c)


def _head_mean(x, bd):
    parts = []
    for s in range(x.shape[1] // 256):
        parts.append(_dot_hilo(x[:, s * 256:(s + 1) * 256], bd))
    out = parts[0] if len(parts) == 1 else jnp.concatenate(parts, axis=1)
    return out * (1.0 / HEAD_DIM)


def _a_in(x, gain, w_sh, gather=()):
    S = x.shape[0]
    tm = 512
    nsh, _, wn = w_sh.shape

    def body(x_ref, g_ref, w_ref, uz_ref, h_ref):
        @pl.when(pl.program_id(1) == 0)
        def _():
            xv = x_ref[...]
            r = lax.rsqrt(jnp.mean(xv * xv, axis=-1, keepdims=True) + EPS)
            h_ref[...] = (xv * r * g_ref[...]).astype(BF16)

        uz_ref[...] = _dot(h_ref[...], w_ref[0])

    return _call_with_gather(
        body, name="a_in", grid=(S // tm, nsh),
        in_specs=[pl.BlockSpec((tm, D_MODEL), lambda i, j: (i, 0)),
                  pl.BlockSpec((1, D_MODEL), lambda i, j: (0, 0)),
                  pl.BlockSpec((1, D_MODEL, wn), lambda i, j: (j, 0, 0))],
        out_specs=[pl.BlockSpec((tm, wn), lambda i, j: (i, j)),
                   pl.BlockSpec((tm, D_MODEL), lambda i, j: (i, 0))],
        out_shape=[jax.ShapeDtypeStruct((S, nsh * wn), F32),
                   jax.ShapeDtypeStruct((S, D_MODEL), BF16)],
        args=(x, gain, w_sh), gather=gather)


def _inv_count(first_row, rows, w):
    t1 = first_row + 1 + lax.broadcasted_iota(jnp.int32, (rows, 1), 0)
    return 1.0 / jnp.minimum(t1, w).astype(F32)


def _group_weight(wg_ref, g):
    return jnp.concatenate([wg_ref[sh, g] for sh in range(N_CHIPS)], axis=0)


def _a_mix(uz, wg, scale, gather=()):
    S = uz.shape[0]
    tm = ROW_TILE

    def body(u_ref, up_ref, z_ref, wg_ref, sc_ref, ga_ref, p_ref):
        i = pl.program_id(0)
        row = lax.broadcasted_iota(jnp.int32, (tm, tm), 0)
        col = lax.broadcasted_iota(jnp.int32, (tm, tm), 1)
        d = row - col
        for g, w in enumerate(POOL_WINDOWS):
            cols = slice(g * GROUP_DIM, (g + 1) * GROUP_DIM)
            t_main = _mask_bf16((d >= 0) & (d < w))
            t_halo = _mask_bf16(d + tm < w)
            u = u_ref[:, cols]
            up = jnp.where(i > 0, up_ref[:, cols], 0.0)
            hi, lo = _hilo(u)
            hip, lop = _hilo(up)
            wsum = _dot(t_main, hi) + _dot(t_main, lo) + _dot(t_halo, hip) + _dot(t_halo, lop)
            pooled = (wsum * _inv_count(i * tm, tm, w) - u).astype(BF16)
            p_ref[:, cols] = pooled
            mraw = _dot(pooled, _group_weight(wg_ref, g))
            z = z_ref[:, cols]
            ga_ref[:, cols] = (mraw * sc_ref[:, cols] * (z * _sigmoid(z))).astype(BF16)

    return _call_with_gather(
        body, name="a_mix", grid=(S // tm,),
        in_specs=[pl.BlockSpec((tm, D_MODEL), lambda i: (i, 0)),
                  pl.BlockSpec((tm, D_MODEL), lambda i: (jnp.maximum(i - 1, 0), 0)),
                  pl.BlockSpec((tm, D_MODEL), lambda i: (i, 1)),
                  pl.BlockSpec((N_CHIPS, N_GROUPS, 64, GROUP_DIM), lambda i: (0, 0, 0, 0)),
                  pl.BlockSpec((1, D_MODEL), lambda i: (0, 0))],
        out_specs=[pl.BlockSpec((tm, D_MODEL), lambda i: (i, 0)),
                   pl.BlockSpec((tm, D_MODEL), lambda i: (i, 0))],
        out_shape=[jax.ShapeDtypeStruct((S, D_MODEL), BF16),
                   jax.ShapeDtypeStruct((S, D_MODEL), BF16)],
        args=(uz, uz, uz, wg, scale), gather=gather)


def _out_ple(name, gated, x_in, w_out, p, layer, ple_w, ple_g, target=None, gather=()):
    S = x_in.shape[0]
    tm = WIDE_ROW_TILE
    with_loss = target is not None

    def body(*refs):
        if with_loss:
            g_ref, x_ref, wo_ref, p_ref, pw_ref, pg_ref, t_ref, xm_ref, dx_ref, e_ref, gt_ref, loss_ref = refs
        else:
            g_ref, x_ref, wo_ref, p_ref, pw_ref, pg_ref, xm_ref, xo_ref, e_ref, gt_ref = refs
        xm = x_ref[...] + _dot(g_ref[...], wo_ref[...])
        xm_ref[...] = xm
        pb = p_ref[...].astype(BF16)
        e = jnp.concatenate([_dot(pb, pw_ref[sh]) for sh in range(N_CHIPS)], axis=1)
        pg = jnp.concatenate([pg_ref[sh] for sh in range(N_CHIPS)], axis=0)
        gate = _sigmoid(_dot(xm.astype(BF16), pg))
        e_ref[...] = e.astype(BF16)
        gt_ref[...] = gate.astype(BF16)
        xo = xm + e * gate
        if with_loss:
            diff = xo - t_ref[...]
            dx_ref[...] = diff * (1.0 / D_MODEL)

            @pl.when(pl.program_id(0) == 0)
            def _():
                loss_ref[...] = jnp.zeros_like(loss_ref)

            loss_ref[...] += jnp.sum(diff * diff) * (0.5 / D_MODEL)
        else:
            xo_ref[...] = xo

    row = pl.BlockSpec((tm, D_MODEL), lambda i: (i, 0))
    in_specs = [row, row,
                pl.BlockSpec((D_MODEL, D_MODEL), lambda i: (0, 0)),
                pl.BlockSpec((None, None, tm, PLE_DIM), lambda i: (layer, 0, i, 0)),
                pl.BlockSpec((N_CHIPS, PLE_DIM, 256), lambda i: (0, 0, 0)),
                pl.BlockSpec((N_CHIPS, 256, D_MODEL), lambda i: (0, 0, 0))]
    args = [gated, x_in, w_out, p, ple_w, ple_g]
    out_specs = [row, row, row, row]
    out_shape = [jax.ShapeDtypeStruct((S, D_MODEL), F32), jax.ShapeDtypeStruct((S, D_MODEL), F32),
                 jax.ShapeDtypeStruct((S, D_MODEL), BF16), jax.ShapeDtypeStruct((S, D_MODEL), BF16)]
    if with_loss:
        in_specs.append(row)
        args.append(target)
        out_specs.append(pl.BlockSpec((8, 128), lambda i: (0, 0)))
        out_shape.append(jax.ShapeDtypeStruct((8, 128), F32))
    return _call_with_gather(body, name=name, grid=(S // tm,), in_specs=in_specs, out_specs=out_specs,
                             out_shape=out_shape, args=args, gather=gather)


def _b_in(x, kv_gain, b_gain, k_gain_t, q_gain_t, w_kv, w_in, gather=()):
    S = x.shape[0]
    tm = ROW_TILE

    def body(x_ref, kvg_ref, bg_ref, kg_ref, qg_ref, wkv_ref, win_ref,
             hkv_ref, hb_ref, kraw_ref, qraw_ref, k_ref, q_ref, v_ref, z_ref):
        xv = x_ref[...]
        y = xv * lax.rsqrt(jnp.mean(xv * xv, axis=-1, keepdims=True) + EPS)
        hkv = (y * kvg_ref[...]).astype(BF16)
        hb = (y * bg_ref[...]).astype(BF16)
        hkv_ref[...] = hkv
        hb_ref[...] = hb
        bd = _head_mean_matrix()

        def head_norm(raw, gain):
            rr = lax.rsqrt(_head_mean(raw * raw, bd) + EPS)
            return raw * rr * gain

        for sh in range(N_CHIPS):
            kvc = _dot(hkv, wkv_ref[sh])
            qzc = _dot(hb, win_ref[sh])
            cols = slice((sh % 2) * 512, (sh % 2) * 512 + 512)
            if sh < 2:
                kraw_ref[:, cols] = kvc.astype(BF16)
                qraw_ref[:, cols] = qzc.astype(BF16)
                k_ref[:, cols] = head_norm(kvc, kg_ref[:, cols]).astype(BF16)
                q_ref[:, cols] = (head_norm(qzc, qg_ref[:, cols]) * SB_SCALE).astype(BF16)
            else:
                v_ref[:, cols] = kvc.astype(BF16)
                z_ref[:, cols] = qzc.astype(BF16)

    row = pl.BlockSpec((tm, D_MODEL), lambda i: (i, 0))
    vec = pl.BlockSpec((1, D_MODEL), lambda i: (0, 0))
    wsp = pl.BlockSpec((N_CHIPS, D_MODEL, 512), lambda i: (0, 0, 0))
    return _call_with_gather(
        body, name="b_in", grid=(S // tm,),
        in_specs=[row, vec, vec, vec, vec, wsp, wsp],
        out_specs=[row] * 8,
        out_shape=[jax.ShapeDtypeStruct((S, D_MODEL), BF16)] * 8,
        args=(x, kv_gain, b_gain, k_gain_t, q_gain_t, w_kv, w_in), gather=gather)


def _softplus_parts(z):
    e = jnp.exp(-jnp.abs(z))
    return -(jnp.maximum(z, 0.0) + jnp.log(1.0 + e)), e


def _add_rows(total, rows, update):
    lo, hi = rows
    parts = ([total[:lo]] if lo else []) + [total[lo:hi] + update] + ([total[hi:]] if hi < total.shape[0] else [])
    return parts[0] if len(parts) == 1 else jnp.concatenate(parts, axis=0)


def _attn_fwd(q, k, v, zgate, gather=()):
    S = q.shape[0]
    tq, tk = ATT_Q_TILE, ATT_K_TILE
    kpq = tq // tk
    assert kpq == 2

    def body(q_ref, k_ref, v_ref, z_ref, o_ref, g_ref, lt_ref, steps_ref):
        qi = pl.program_id(1)
        lane = lax.broadcasted_iota(jnp.int32, (1, 128), 1)
        ri = lax.broadcasted_iota(jnp.int32, (tk, tk), 0)
        ci = lax.broadcasted_iota(jnp.int32, (tk, tk), 1)
        later_mat = _mask_bf16(ri > ci)
        causal = ci < ri
        qv = q_ref[...]
        first = lane < HEAD_DIM
        q_heads = (jnp.where(first, qv, jnp.zeros_like(qv)), jnp.where(first, jnp.zeros_like(qv), qv))

        def step(blocks, carry):
            chains = [(b, h) for b in range(len(blocks)) for h in range(2)]
            rows = [r for _, r, _ in blocks]
            s0 = [pl.multiple_of(kj * tk, tk) for kj, _, _ in blocks]
            kb = [k_ref[pl.ds(s, tk), :] for s in s0]
            vb = [v_ref[pl.ds(s, tk), :] for s in s0]
            visible = [causal if masked else None for _, _, masked in blocks]
            z = {c: _dot_nt(q_heads[c[1]][rows[c[0]][0]:rows[c[0]][1]], kb[c[0]]) for c in chains}
            run = [carry[0], carry[2]]
            log_own, later, run_at = {}, {}, {}
            for c in chains:
                b, h = c
                lk = _softplus_parts(z[c])[0]
                if visible[b] is not None:
                    lk = jnp.where(visible[b], lk, 0.0)
                log_own[c] = z[c] + lk
                later[c] = _dot(lk.astype(BF16), later_mat)
                run_at[c] = run[h][rows[b][0]:rows[b][1]]
                run[h] = _add_rows(run[h], rows[b], jnp.sum(lk, axis=-1, keepdims=True))
            acc = [carry[1], carry[3]]
            for c in chains:
                b, h = c
                a = jnp.exp(log_own[c] + later[c] + run_at[c])
                if visible[b] is not None:
                    a = jnp.where(visible[b], a, 0.0)
                acc[h] = _add_rows(acc[h], rows[b], _dot(a.astype(BF16), vb[b]))
            return run[0], acc[0], run[1], acc[1]

        zero1, zero128 = jnp.zeros((tq, 1), F32), jnp.zeros((tq, 128), F32)
        carry = step([(qi * kpq + 1, (tk, tq), True), (qi * kpq, (tk, tq), False), (qi * kpq, (0, tk), True)],
                     (zero1, zero128, zero1, zero128))

        def low(run):
            return jnp.max(run)

        def pair_more(c):
            return (c[0] < qi) & (jnp.maximum(low(c[1][tk:]), low(c[3][tk:])) > EXP_UNDERFLOW)

        def pair_step(c):
            last = (qi - c[0]) * kpq - 1
            return (c[0] + 1, *step([(last, (0, tq), False), (last - 1, (0, tq), False)], c[1:]))

        pairs, *carry = lax.while_loop(pair_more, pair_step, (jnp.int32(0), *carry))
        left = (qi - pairs) * kpq

        def single_more(c):
            return (c[0] < left) & (jnp.maximum(low(c[1][:tk]), low(c[3][:tk])) > EXP_UNDERFLOW)

        def single_step(c):
            return (c[0] + 1, *step([(left - 1 - c[0], (0, tk), False)], c[1:]))

        singles, *carry = lax.while_loop(single_more, single_step, (jnp.int32(0), *carry))
        steps_ref[...] = jnp.concatenate([jnp.full((4, 128), pairs, F32), jnp.full((4, 128), singles, F32)], axis=0)
        o_tot = jnp.where(first, carry[1], carry[3])
        l_tot = jnp.where(first, carry[0], carry[2])
        o_ref[...] = o_tot.astype(BF16)
        lt_ref[...] = l_tot
        zz = z_ref[...].astype(F32)
        g_ref[...] = (o_tot * (zz * _sigmoid(zz))).astype(BF16)

    blk = pl.BlockSpec((tq, 128), lambda hp, qi: (qi, hp))
    seq = pl.BlockSpec((S, 128), lambda hp, qi: (0, hp))
    return _call_with_gather(
        body, name="attn_fwd", grid=(D_MODEL // 128, S // tq),
        in_specs=[blk, seq, seq, blk],
        out_specs=[blk, blk, blk, pl.BlockSpec((None, None, 8, 128), lambda hp, qi: (hp, qi, 0, 0))],
        out_shape=[jax.ShapeDtypeStruct((S, D_MODEL), BF16)] * 2 + [jax.ShapeDtypeStruct((S, D_MODEL), F32)]
        + [jax.ShapeDtypeStruct((D_MODEL // 128, S // tq, 8, 128), F32)],
        args=(q, k, v, zgate), gather=gather)


def _ple_out_bwd(name, dx_out, e, gate, ple_g, w_out):
    S = dx_out.shape[0]
    tm = WIDE_ROW_TILE

    def body(dx_ref, e_ref, gt_ref, pg_ref, wo_ref, de_ref, dgp_ref, dxm_ref, dg_ref):
        dxo = dx_ref[...]
        ev = e_ref[...].astype(F32)
        gv = gt_ref[...].astype(F32)
        de_ref[...] = (dxo * gv).astype(BF16)
        dgp = (dxo * ev * gv * (1.0 - gv)).astype(BF16)
        dgp_ref[...] = dgp
        pg = jnp.concatenate([pg_ref[sh] for sh in range(N_CHIPS)], axis=0)
        dxm = dxo + _dot_nt(dgp, pg)
        dxm_ref[...] = dxm
        dg_ref[...] = _dot_nt(dxm.astype(BF16), wo_ref[...]).astype(BF16)

    row = pl.BlockSpec((tm, D_MODEL), lambda i: (i, 0))
    return pl.pallas_call(
        body, name=name, grid=(S // tm,),
        in_specs=[row, row, row,
                  pl.BlockSpec((N_CHIPS, 256, D_MODEL), lambda i: (0, 0, 0)),
                  pl.BlockSpec((D_MODEL, D_MODEL), lambda i: (0, 0))],
        out_specs=[row, row, row, row],
        out_shape=[jax.ShapeDtypeStruct((S, D_MODEL), BF16), jax.ShapeDtypeStruct((S, D_MODEL), BF16),
                   jax.ShapeDtypeStruct((S, D_MODEL), F32), jax.ShapeDtypeStruct((S, D_MODEL), BF16)],
        compiler_params=_params(("arbitrary",)),
    )(dx_out, e, gate, ple_g, w_out)


def _attn_bwd(q, k, v, ltot, steps, dgated, o, zgate, reduce=None):
    S = q.shape[0]
    tq, tk = ATT_Q_TILE, ATT_K_TILE
    kpq = tq // tk
    nq = S // tq

    def body(q_ref, k_ref, v_ref, lt_ref, steps_ref, dg_ref, o_ref, z_ref, dq_ref, dk_ref, dv_ref, dz_ref,
             dk_acc, dv_acc):
        qi = pl.program_id(1)

        @pl.when(qi == 0)
        def _():
            dk_acc[...] = jnp.zeros_like(dk_acc)
            dv_acc[...] = jnp.zeros_like(dv_acc)

        lane = lax.broadcasted_iota(jnp.int32, (1, 128), 1)
        ri = lax.broadcasted_iota(jnp.int32, (tk, tk), 0)
        ci = lax.broadcasted_iota(jnp.int32, (tk, tk), 1)
        later_mat = _mask_bf16(ri > ci)
        before_mat = _mask_bf16(ri < ci)
        causal = ci < ri
        zz = z_ref[...].astype(F32)
        sg = _sigmoid(zz)
        dgv = dg_ref[...].astype(F32)
        dz_ref[...] = (dgv * o_ref[...].astype(F32) * _dsilu(zz, sg)).astype(BF16)
        dob = (dgv * (zz * sg)).astype(BF16)
        ltv = lt_ref[...]
        qv = q_ref[...]
        first = lane < HEAD_DIM
        masks = (first, jnp.logical_not(first))
        q_heads = [jnp.where(hm, qv, jnp.zeros_like(qv)) for hm in masks]
        do_heads = [jnp.where(hm, dob, jnp.zeros_like(dob)) for hm in masks]
        totals = [jnp.max(jnp.where(hm, ltv, -jnp.inf), axis=-1, keepdims=True) for hm in masks]

        def step(blocks, carry):
            chains = [(b, h) for b in range(len(blocks)) for h in range(2)]
            rows = [r for _, r, _ in blocks]
            cut = lambda t, b: t[rows[b][0]:rows[b][1]]
            s0 = [pl.multiple_of(kj * tk, tk) for kj, _, _ in blocks]
            kb = [k_ref[pl.ds(s, tk), :] for s in s0]
            vb = [v_ref[pl.ds(s, tk), :] for s in s0]
            visible = [causal if masked else None for _, _, masked in blocks]
            z = {c: _dot_nt(cut(q_heads[c[1]], c[0]), kb[c[0]]) for c in chains}
            da = {c: _dot_nt(cut(do_heads[c[1]], c[0]), vb[c[0]]) for c in chains}
            run = [carry[0], carry[3]]
            log_own, beta, later, base = {}, {}, {}, {}
            for c in chains:
                b, h = c
                lk = _softplus_parts(z[c])[0]
                if visible[b] is not None:
                    lk = jnp.where(visible[b], lk, 0.0)
                log_own[c] = z[c] + lk
                beta[c] = jnp.exp(log_own[c]).astype(BF16)
                later[c] = _dot(lk.astype(BF16), later_mat)
                run[h] = _add_rows(run[h], rows[b], jnp.sum(lk, axis=-1, keepdims=True))
                base[c] = cut(totals[h] - run[h], b)
            grun = [carry[1], carry[4]]
            a_bf, g_bf, gbefore, grun_at = {}, {}, {}, {}
            for c in chains:
                b, h = c
                a = jnp.exp(log_own[c] + later[c] + base[c])
                if visible[b] is not None:
                    a = jnp.where(visible[b], a, 0.0)
                a_bf[c] = a.astype(BF16)
                g = da[c] * a
                g_bf[c] = g.astype(BF16)
                gbefore[c] = _dot(g_bf[c], before_mat)
                grun_at[c] = cut(grun[h], b)
                grun[h] = _add_rows(grun[h], rows[b], jnp.sum(g, axis=-1, keepdims=True))
            dq = [carry[2], carry[5]]
            dk_blk = [jnp.zeros((tk, 128), F32) for _ in blocks]
            dv_blk = [jnp.zeros((tk, 128), F32) for _ in blocks]
            for c in chains:
                b, h = c
                g = g_bf[c].astype(F32)
                dz = g - beta[c].astype(F32) * (g + gbefore[c] + grun_at[c])
                if visible[b] is not None:
                    dz = jnp.where(visible[b], dz, 0.0)
                dzb = dz.astype(BF16)
                dq[h] = _add_rows(dq[h], rows[b], _dot(dzb, kb[b]))
                dk_blk[b] = dk_blk[b] + _dot_tn(dzb, cut(q_heads[h], b))
                dv_blk[b] = dv_blk[b] + _dot_tn(a_bf[c], cut(do_heads[h], b))
            for b in range(len(blocks)):
                dk_acc[pl.ds(s0[b], tk), :] += dk_blk[b]
                dv_acc[pl.ds(s0[b], tk), :] += dv_blk[b]
            return run[0], grun[0], dq[0], run[1], grun[1], dq[1]

        pairs = jnp.clip(jnp.max(steps_ref[0:4, :]).astype(jnp.int32), 0, qi)
        left = (qi - pairs) * kpq
        singles = jnp.clip(jnp.max(steps_ref[4:8, :]).astype(jnp.int32), 0, left)
        zero1, zero128 = jnp.zeros((tq, 1), F32), jnp.zeros((tq, 128), F32)
        carry = lax.fori_loop(left - singles, left, lambda kj, c: step([(kj, (0, tk), False)], c),
                              (zero1, zero1, zero128, zero1, zero1, zero128))
        carry = lax.fori_loop(qi - pairs, qi,
                              lambda n, c: step([(n * kpq, (0, tq), False), (n * kpq + 1, (0, tq), False)], c), carry)
        carry = step([(qi * kpq, (0, tk), True), (qi * kpq, (tk, tq), False), (qi * kpq + 1, (tk, tq), True)], carry)
        dq_ref[...] = jnp.where(first, carry[2], carry[5]).astype(BF16)

        @pl.when(qi == nq - 1)
        def _():
            dk_ref[...] = dk_acc[...].astype(BF16)
            dv_ref[...] = dv_acc[...].astype(BF16)

    blk = pl.BlockSpec((tq, 128), lambda hp, qi: (qi, hp))
    seq = pl.BlockSpec((S, 128), lambda hp, qi: (0, hp))
    return _call_with_gather(
        body, name="attn_bwd", grid=(D_MODEL // 128, nq),
        in_specs=[blk, seq, seq, blk, pl.BlockSpec((None, None, 8, 128), lambda hp, qi: (hp, qi, 0, 0)),
                  blk, blk, blk],
        out_specs=[blk, seq, seq, blk],
        out_shape=[jax.ShapeDtypeStruct((S, D_MODEL), BF16)] * 4,
        scratch_shapes=[pltpu.VMEM((S, 128), F32), pltpu.VMEM((S, 128), F32)],
        args=(q, k, v, ltot, steps, dgated, o, zgate), reduce=reduce, vmem_mib=56)


def _rms_bwd(xv, dh_gain_sum):
    r = lax.rsqrt(jnp.mean(xv * xv, axis=-1, keepdims=True) + EPS)
    xhat = xv * r
    dx = r * (dh_gain_sum - xhat * jnp.mean(dh_gain_sum * xhat, axis=-1, keepdims=True))
    return dx, xhat


def _b_in_bwd(dq, dk, dv, dz, q_raw, k_raw, x, dx_mid, q_gain_t, k_gain_t, b_gain, kv_gain, w_in, w_kv):
    S = x.shape[0]
    tm = ROW_TILE

    def body(dq_ref, dk_ref, dv_ref, dz_ref, qr_ref, kr_ref, x_ref, dxm_ref, qg_ref, kg_ref, bg_ref, kvg_ref,
             win_ref, wkv_ref, dqz_ref, dkv_ref, dx_ref, small_ref):
        @pl.when(pl.program_id(0) == 0)
        def _():
            small_ref[...] = jnp.zeros_like(small_ref)

        bd = _head_mean_matrix()

        def head_norm_bwd(dy_ref, raw_ref, gain, scale):
            raw = raw_ref[...].astype(F32)
            rr = lax.rsqrt(_head_mean(raw * raw, bd) + EPS)
            xhat = raw * rr
            dy = dy_ref[...].astype(F32) * scale
            gdy = dy * gain
            draw = rr * (gdy - xhat * _head_mean(gdy * xhat, bd))
            return draw.astype(BF16), jnp.sum(dy * xhat, axis=0, keepdims=True)

        dqr, dqg = head_norm_bwd(dq_ref, qr_ref, qg_ref[...], SB_SCALE)
        dkr, dkg = head_norm_bwd(dk_ref, kr_ref, kg_ref[...], 1.0)
        dqz_ref[:, :D_MODEL] = dqr
        dqz_ref[:, D_MODEL:] = dz_ref[...]
        dkv_ref[:, :D_MODEL] = dkr
        dkv_ref[:, D_MODEL:] = dv_ref[...]
        dhb = jnp.zeros((tm, D_MODEL), F32)
        dhkv = jnp.zeros((tm, D_MODEL), F32)
        for sh in range(N_CHIPS):
            cols = slice(sh * 512, (sh + 1) * 512)
            dhb = dhb + _dot_nt(dqz_ref[:, cols], win_ref[sh])
            dhkv = dhkv + _dot_nt(dkv_ref[:, cols], wkv_ref[sh])
        dx, xhat = _rms_bwd(x_ref[...], dhb * bg_ref[...] + dhkv * kvg_ref[...])
        dx_ref[...] = dxm_ref[...] + dx
        small_ref[0:1, :] += dqg
        small_ref[1:2, :] += dkg
        small_ref[2:3, :] += jnp.sum(dhb * xhat, axis=0, keepdims=True)
        small_ref[3:4, :] += jnp.sum(dhkv * xhat, axis=0, keepdims=True)

    row = pl.BlockSpec((tm, D_MODEL), lambda i: (i, 0))
    wide = pl.BlockSpec((tm, 2 * D_MODEL), lambda i: (i, 0))
    vec = pl.BlockSpec((1, D_MODEL), lambda i: (0, 0))
    wsp = pl.BlockSpec((N_CHIPS, D_MODEL, 512), lambda i: (0, 0, 0))
    return pl.pallas_call(
        body, name="b_in_bwd", grid=(S // tm,),
        in_specs=[row] * 8 + [vec] * 4 + [wsp, wsp],
        out_specs=[wide, wide, row, pl.BlockSpec((8, D_MODEL), lambda i: (0, 0))],
        out_shape=[jax.ShapeDtypeStruct((S, 2 * D_MODEL), BF16), jax.ShapeDtypeStruct((S, 2 * D_MODEL), BF16),
                   jax.ShapeDtypeStruct((S, D_MODEL), F32), jax.ShapeDtypeStruct((8, D_MODEL), F32)],
        compiler_params=_params(("arbitrary",), 56),
    )(dq, dk, dv, dz, q_raw, k_raw, x, dx_mid, q_gain_t, k_gain_t, b_gain, kv_gain, w_in, w_kv)


def _a_mix_bwd(dgated, uz, pooled, wg, scale, w_in, x, dx_mid, gain, reduce=None):
    S = x.shape[0]
    tm = ROW_TILE
    n = S // tm

    def body(dg_ref, z_ref, p_ref, wg_ref, sc_ref, win_ref, x_ref, dxm_ref, gn_ref,
             duz_ref, dmr_ref, dx_ref, small_ref, halo_hi, halo_lo):
        i = pl.program_id(0)

        @pl.when(i == 0)
        def _():
            small_ref[...] = jnp.zeros_like(small_ref)
            halo_hi[...] = jnp.zeros_like(halo_hi)
            halo_lo[...] = jnp.zeros_like(halo_lo)

        first_row = (n - 1 - i) * tm
        row = lax.broadcasted_iota(jnp.int32, (tm, tm), 0)
        col = lax.broadcasted_iota(jnp.int32, (tm, tm), 1)
        d = col - row
        for g, w in enumerate(POOL_WINDOWS):
            cols = slice(g * GROUP_DIM, (g + 1) * GROUP_DIM)
            wgg = _group_weight(wg_ref, g)
            sc = sc_ref[:, cols]
            mraw = _dot(p_ref[:, cols], wgg)
            z = z_ref[:, cols]
            sg = _sigmoid(z)
            dga = dg_ref[:, cols].astype(F32)
            dm = dga * (z * sg)
            duz_ref[:, D_MODEL + g * GROUP_DIM:D_MODEL + (g + 1) * GROUP_DIM] = (
                dga * (mraw * sc) * _dsilu(z, sg)).astype(BF16)
            small_ref[0:1, cols] += jnp.sum(dm * mraw, axis=0, keepdims=True)
            dmr = (dm * sc).astype(BF16)
            dmr_ref[:, cols] = dmr
            dp = _dot_nt(dmr, wgg)
            hi, lo = _hilo(dp * _inv_count(first_row, tm, w))
            t_main = _mask_bf16((d >= 0) & (d < w))
            t_halo = _mask_bf16(d + tm < w)
            du = (_dot(t_main, hi) + _dot(t_main, lo) + _dot(t_halo, halo_hi[:, cols]) + _dot(t_halo, halo_lo[:, cols])
                  - dp)
            halo_hi[:, cols] = hi
            halo_lo[:, cols] = lo
            duz_ref[:, cols] = du.astype(BF16)
        dh = jnp.zeros((tm, D_MODEL), F32)
        for sh in range(N_CHIPS):
            dh = dh + _dot_nt(duz_ref[:, sh * 512:(sh + 1) * 512], win_ref[sh])
        dx, xhat = _rms_bwd(x_ref[...], dh * gn_ref[...])
        dx_ref[...] = dxm_ref[...] + dx
        small_ref[1:2, :] += jnp.sum(dh * xhat, axis=0, keepdims=True)

    rev = lambda i: (n - 1 - i, 0)
    row = pl.BlockSpec((tm, D_MODEL), rev)
    vec = pl.BlockSpec((1, D_MODEL), lambda i: (0, 0))
    return _call_with_gather(
        body, name="a_mix_bwd", grid=(n,),
        in_specs=[row,
                  pl.BlockSpec((tm, D_MODEL), lambda i: (n - 1 - i, 1)),
                  row,
                  pl.BlockSpec((N_CHIPS, N_GROUPS, 64, GROUP_DIM), lambda i: (0, 0, 0, 0)),
                  vec,
                  pl.BlockSpec((N_CHIPS, D_MODEL, 512), lambda i: (0, 0, 0)),
                  row, row, vec],
        out_specs=[pl.BlockSpec((tm, 2 * D_MODEL), rev), row, row,
                   pl.BlockSpec((8, D_MODEL), lambda i: (0, 0))],
        out_shape=[jax.ShapeDtypeStruct((S, 2 * D_MODEL), BF16), jax.ShapeDtypeStruct((S, D_MODEL), BF16),
                   jax.ShapeDtypeStruct((S, D_MODEL), F32), jax.ShapeDtypeStruct((8, D_MODEL), F32)],
        scratch_shapes=[pltpu.VMEM((tm, D_MODEL), BF16), pltpu.VMEM((tm, D_MODEL), BF16)],
        args=(dgated, uz, pooled, wg, scale, w_in, x, dx_mid, gain), reduce=reduce, vmem_mib=56)


def _wgrad(name, a, dy, n_shards, a_spec=None, k_dim=None):
    S, n_cols = dy.shape
    ts = WGRAD_SEQ_TILE
    k_dim = a.shape[-1] if k_dim is None else k_dim
    wn = n_cols // n_shards
    tk = min(k_dim, WGRAD_ACC_BYTES // (4 * n_cols))
    nst = S // ts

    def body(a_ref, dy_ref, out_ref, acc):
        st = pl.program_id(1)

        @pl.when(st == 0)
        def _():
            acc[...] = jnp.zeros_like(acc)

        acc[...] += _dot_tn(a_ref[...].astype(BF16), dy_ref[...].astype(BF16))

        @pl.when(st == nst - 1)
        def _():
            for sh in range(n_shards):
                out_ref[sh] = acc[:, sh * wn:(sh + 1) * wn]

    if a_spec is None:
        a_spec = pl.BlockSpec((ts, tk), lambda kt, st: (st, kt))
    return pl.pallas_call(
        body, name=name, grid=(k_dim // tk, nst),
        in_specs=[a_spec, pl.BlockSpec((ts, n_cols), lambda kt, st: (st, 0))],
        out_specs=pl.BlockSpec((n_shards, tk, wn), lambda kt, st: (0, kt, 0)),
        out_shape=jax.ShapeDtypeStruct((n_shards, k_dim, wn), F32),
        scratch_shapes=[pltpu.VMEM((tk, n_cols), F32)],
        compiler_params=_params(("parallel", "arbitrary")),
    )(a, dy)


def _wgrad_ple(name, p, layer, de):
    ts = WGRAD_SEQ_TILE
    spec = pl.BlockSpec((None, None, ts, PLE_DIM), lambda kt, st: (layer, 0, st, 0))
    return _wgrad(name, p, de, N_CHIPS, a_spec=spec, k_dim=PLE_DIM)


def _wgrad_group(pooled, dmr):
    S = pooled.shape[0]
    ts = WGRAD_SEQ_TILE
    nst = S // ts

    def body(p_ref, d_ref, out_ref, acc):
        st = pl.program_id(1)

        @pl.when(st == 0)
        def _():
            acc[...] = jnp.zeros_like(acc)

        acc[...] += _dot_tn(p_ref[...], d_ref[...])

        @pl.when(st == nst - 1)
        def _():
            for sh in range(N_CHIPS):
                out_ref[sh] = acc[sh * 64:(sh + 1) * 64, :]

    blk = pl.BlockSpec((ts, GROUP_DIM), lambda g, st: (st, g))
    return pl.pallas_call(
        body, name="wgrad_group", grid=(N_GROUPS, nst),
        in_specs=[blk, blk],
        out_specs=pl.BlockSpec((N_CHIPS, None, 64, GROUP_DIM), lambda g, st: (0, g, 0, 0)),
        out_shape=jax.ShapeDtypeStruct((N_CHIPS, N_GROUPS, 64, GROUP_DIM), F32),
        scratch_shapes=[pltpu.VMEM((GROUP_DIM, GROUP_DIM), F32)],
        compiler_params=_params(("parallel", "arbitrary")),
    )(pooled, dmr)


GATHER_AT = {
    "a_in": ("a_w_group", "a_w_out", "ple_w0", "ple_gate_w0"),
    "a_mix": ("w_kv",),
    "a_out_ple": ("b_w_in",),
    "attn_fwd": ("b_w_out", "ple_w1", "ple_gate_w1"),
}


REDUCE_AT = {
    "attn_bwd": ("b_w_out", "ple_w1", "ple_gate_w1"),
    "a_mix_bwd": ("a_w_out", "ple_w0", "ple_gate_w0"),
}


def _local_step(x, p, target, w, local=None, state=None):
    w = dict(w)

    def run(fn, host, n_out, *args, **kwargs):
        names = GATHER_AT[host] if local is not None else ()
        res = fn(*args, gather=[local[n] for n in names], **kwargs)
        w.update(zip(names, res[n_out:]))
        return res[:n_out]

    k_gain_t = jnp.tile(w["k_norm"].reshape(1, HEAD_DIM), (1, N_HEADS))
    q_gain_t = jnp.tile(w["b_q_norm"].reshape(1, HEAD_DIM), (1, N_HEADS))

    uz, h_a = run(_a_in, "a_in", 2, x, w["a_norm"], w["a_w_in"])
    wg4 = w["a_w_group"].reshape(N_CHIPS, N_GROUPS, 64, GROUP_DIM)
    wa_out = w["a_w_out"].reshape(D_MODEL, D_MODEL)
    gated_a, pooled = run(_a_mix, "a_mix", 2, uz, wg4, w["a_scale"])
    x1, x2, e_a, gate_a = run(_out_ple, "a_out_ple", 4, "a_out_ple", gated_a, x, wa_out, p, 0,
                              w["ple_w0"], w["ple_gate_w0"])
    h_kv, h_b, k_raw, q_raw, k, q, v, z_b = _b_in(
        x2, w["kv_norm"], w["b_norm"], k_gain_t, q_gain_t, w["w_kv"], w["b_w_in"])
    o, gated_b, ltot, att_steps = run(_attn_fwd, "attn_fwd", 4, q, k, v, z_b)
    wb_out = w["b_w_out"].reshape(D_MODEL, D_MODEL)
    x3, dx4, e_b, gate_b, loss_blk = _out_ple("b_out_ple", gated_b, x2, wb_out, p, 1, w["ple_w1"], w["ple_gate_w1"],
                                              target=target)

    grads, updates = {}, {}

    def hosted(fn, host, n_out, *args):
        if state is None:
            return fn(*args)
        names = REDUCE_AT[host]
        seeds = [updates.get(n[:-1] + "1") if n.startswith("ple") and n.endswith("0") else None for n in names]
        res = fn(*args, reduce=([grads.pop(n) for n in names], *[[t[n] for n in names] for t in state[:3]],
                                [state[3][n] for n in names], seeds))
        for i, n in enumerate(names):
            updates[n] = tuple(group[i] for group in res[n_out:])
        return res[:n_out]

    de_b, dgp_b, dx3, dgated_b = _ple_out_bwd("b_ple_out_bwd", dx4, e_b, gate_b, w["ple_gate_w1"], wb_out)
    grads["b_w_out"] = _wgrad("wgrad_b_out", gated_b, dx3, 1).reshape(N_CHIPS, 256, D_MODEL)
    grads["ple_w1"] = _wgrad_ple("wgrad_ple1", p, 1, de_b)
    grads["ple_gate_w1"] = _wgrad("wgrad_gate1", x3, dgp_b, 1).reshape(N_CHIPS, 256, D_MODEL)
    dq, dk, dv, dz_b = hosted(_attn_bwd, "attn_bwd", 4, q, k, v, ltot, att_steps, dgated_b, o, z_b)
    dqz, dkv, dx2, small_b = _b_in_bwd(dq, dk, dv, dz_b, q_raw, k_raw, x2, dx3, q_gain_t, k_gain_t,
                                       w["b_norm"], w["kv_norm"], w["b_w_in"], w["w_kv"])
    grads["w_kv"] = _wgrad("wgrad_kv", h_kv, dkv, N_CHIPS)
    grads["b_w_in"] = _wgrad("wgrad_b_in", h_b, dqz, N_CHIPS)
    de_a, dgp_a, dx1, dgated_a = _ple_out_bwd("a_ple_out_bwd", dx2, e_a, gate_a, w["ple_gate_w0"], wa_out)
    grads["a_w_out"] = _wgrad("wgrad_a_out", gated_a, dx1, 1).reshape(N_CHIPS, 256, D_MODEL)
    grads["ple_w0"] = _wgrad_ple("wgrad_ple0", p, 0, de_a)
    grads["ple_gate_w0"] = _wgrad("wgrad_gate0", x1, dgp_a, 1).reshape(N_CHIPS, 256, D_MODEL)
    duz, dmr, grad_x, small_a = hosted(_a_mix_bwd, "a_mix_bwd", 4, dgated_a, uz, pooled, wg4, w["a_scale"],
                                       w["a_w_in"], x, dx1, w["a_norm"])
    grads["a_w_in"] = _wgrad("wgrad_a_in", h_a, duz, N_CHIPS)
    grads["a_w_group"] = _wgrad_group(pooled, dmr).reshape(N_CHIPS, N_GROUPS * 64, GROUP_DIM)

    fold = lambda row: jnp.pad(row.reshape(N_HEADS, HEAD_DIM).sum(axis=0), (0, D_MODEL - HEAD_DIM))
    small = jnp.stack([small_a[1], small_a[0], small_b[3], small_b[2], fold(small_b[1]), fold(small_b[0]),
                       jnp.pad(loss_blk[0], (0, D_MODEL - loss_blk.shape[1])), jnp.zeros((D_MODEL,), F32)])
    return grad_x, grads, updates, small


def _mesh_place():
    x, y, c = lax.axis_index("x"), lax.axis_index("y"), lax.axis_index("c")
    other_chips = [(1 - x, y), (x, 1 - y), (1 - x, 1 - y)]
    return x, y, c, other_chips


def _gather_sems(n):
    return [pltpu.SemaphoreType.DMA((3 * n,)), pltpu.SemaphoreType.DMA((3 * n,)),
            pltpu.SemaphoreType.DMA((3 * n,)), pltpu.SemaphoreType.DMA((3 * n,)), pltpu.SemaphoreType.DMA((n,))]


def _gather_copies(srcs, outs, sems):
    send_far, recv_far, send_sib, recv_sib, local_sem = sems
    n = len(srcs)
    x, y, c, chips = _mesh_place()
    me = 2 * x + y
    sibling = (x, y, 1 - c)

    def half(k, which):
        rows = srcs[k].shape[0] // 2
        return pl.ds(pl.multiple_of(which * rows, 16), rows)

    local = [pltpu.make_async_copy(srcs[k], outs[k].at[me], local_sem.at[k]) for k in range(n)]
    far = [pltpu.make_async_remote_copy(
        src_ref=srcs[k].at[half(k, c)], dst_ref=outs[k].at[me, half(k, c)],
        send_sem=send_far.at[j * n + k], recv_sem=recv_far.at[j * n + k], device_id=(px, py, c), device_id_type=MESH)
        for j, (px, py) in enumerate(chips) for k in range(n)]

    def landed(j, k, which, from_far):
        px, py = chips[j]
        piece = outs[k].at[2 * px + py, half(k, which)]
        send, recv = (send_far, recv_far) if from_far else (send_sib, recv_sib)
        return pltpu.make_async_remote_copy(src_ref=piece, dst_ref=piece, send_sem=send.at[j * n + k],
                                            recv_sem=recv.at[j * n + k], device_id=sibling, device_id_type=MESH)

    return local, far, landed, c


def _gather_start(srcs, outs, sems):
    local, far, _, _ = _gather_copies(srcs, outs, sems)
    for cp in local + far:
        cp.start()


def _gather_pass_on(srcs, outs, sems):
    _, _, landed, c = _gather_copies(srcs, outs, sems)
    for j in range(3):
        for k in range(len(srcs)):
            landed(j, k, c, True).wait_recv()
            landed(j, k, c, False).start()


def _gather_finish(srcs, outs, sems):
    local, far, landed, c = _gather_copies(srcs, outs, sems)
    pairs = [(j, k) for j in range(3) for k in range(len(srcs))]
    for j, k in pairs:
        landed(j, k, 1 - c, False).wait_recv()
    for cp in far + [landed(j, k, c, False) for j, k in pairs]:
        cp.wait_send()
    for cp in local:
        cp.wait()


def _call_with_gather(body, *, name, grid, in_specs, out_specs, out_shape, args, gather=(), reduce=None,
                      scratch_shapes=(), vmem_mib=48):
    n_in, n_out, n_scr, n_g = len(args), len(out_shape), len(scratch_shapes), len(gather)
    n_r = len(reduce[0]) if reduce else 0
    pieces = _reduce_pieces(reduce[0], reduce[4]) if reduce else []
    reduce_args = [a for group in reduce[:4] for a in group] if reduce else []
    seeds = reduce[5] if reduce else []
    seeded = [(k, a) for k, seed in enumerate(seeds) if seed is not None for a in range(4)]
    gather_sems = _gather_sems(n_g) if n_g else []
    n_steps = 1
    for g in grid:
        n_steps *= g

    def wrapped(*refs):
        refs = list(refs)
        take = lambda count: [refs.pop(0) for _ in range(count)]
        ins, g_in, r_in = take(n_in), take(n_g), take(4 * n_r)
        take(len(seeded))
        outs, g_out, r_out = take(n_out), take(n_g), take(4 * n_r)
        scratch, sems, r_scratch = take(n_scr), take(len(gather_sems)), refs
        step = 0
        for axis, g in enumerate(grid):
            step = step * g + pl.program_id(axis)
        if n_g:
            @pl.when(step == 0)
            def _():
                _gather_start(g_in, g_out, sems)

        if n_r:
            ticks, drain = _reduce_ticks(pieces, n_r, (*r_in, *r_out, *r_scratch))
            for t, tick in enumerate(ticks[:n_steps]):
                pl.when(step == t)(tick)

        body(*ins, *outs, *scratch)
        if n_r:
            for tick in ticks[n_steps:]:
                pl.when(step == n_steps - 1)(tick)
            pl.when(step == n_steps - 1)(drain)
        if n_g:
            @pl.when(step == max(n_steps - 2, 0))
            def _():
                _gather_pass_on(g_in, g_out, sems)

            @pl.when(step == n_steps - 1)
            def _():
                _gather_finish(g_in, g_out, sems)

    hbm = pl.BlockSpec(memory_space=pltpu.HBM)
    res = pl.pallas_call(
        wrapped, name=name, grid=grid,
        in_specs=list(in_specs) + [hbm] * (n_g + 4 * n_r + len(seeded)),
        out_specs=list(out_specs) + [hbm] * (n_g + 4 * n_r),
        out_shape=list(out_shape) + [jax.ShapeDtypeStruct((N_CHIPS,) + g.shape, BF16) for g in gather]
        + ([jax.ShapeDtypeStruct(w.shape, F32) for _ in range(4) for w in reduce[1]] if reduce else []),
        input_output_aliases={n_in + n_g + 4 * n_r + i: n_out + n_g + a * n_r + k for i, (k, a) in enumerate(seeded)},
        scratch_shapes=list(scratch_shapes) + gather_sems + (_reduce_scratch() if reduce else []),
        compiler_params=_params(("arbitrary",) * len(grid), vmem_mib),
    )(*args, *gather, *reduce_args, *[seeds[k][a] for k, a in seeded])
    if not reduce:
        return res
    plain = list(res[:n_out + n_g])
    return plain + [res[n_out + n_g + i * n_r:n_out + n_g + (i + 1) * n_r] for i in range(4)]


def _allgather_weights(shards, small, casts):
    n = len(shards)
    cast_out = [(k, r0, r1) for k, (_, ranges) in enumerate(casts) for r0, r1 in ranges]
    n_c, n_co = len(casts), len(cast_out)

    def body(*refs):
        ins, small_in, cast_in = refs[:n], refs[n], refs[n + 1:n + 1 + n_c]
        refs = refs[n + 1 + n_c:]
        outs, small_out, cast_dst = refs[:n], refs[n], refs[n + 1:n + 1 + n_co]
        refs = refs[n + 1 + n_co:]
        cast, cast_buf = refs[:n], refs[n:n + n_co]
        send_far, recv_far, send_sib, recv_sib, send_small, recv_small, local_sem, cast_sem = refs[n + n_co:]
        x, y, c, chips = _mesh_place()
        me = 2 * x + y
        sibling = (x, y, 1 - c)

        def half(k, which):
            rows = ins[k].shape[0] // 2
            return pl.ds(pl.multiple_of(which * rows, 16), rows)

        local = []
        for k in range(n):
            cast[k][...] = ins[k][...].astype(BF16)
            local.append(pltpu.make_async_copy(cast[k], outs[k].at[me], local_sem.at[k]))
            local[-1].start()
        local.append(pltpu.make_async_copy(small_in, small_out.at[me], local_sem.at[n]))
        local[-1].start()

        sends = []
        for j, (px, py) in enumerate(chips):
            for k in range(n):
                cp = pltpu.make_async_remote_copy(
                    src_ref=cast[k].at[half(k, c)], dst_ref=outs[k].at[me, half(k, c)],
                    send_sem=send_far.at[j * n + k], recv_sem=recv_far.at[j * n + k],
                    device_id=(px, py, c), device_id_type=MESH)
                cp.start()
                sends.append(cp)
            cp = pltpu.make_async_remote_copy(
                src_ref=small_in, dst_ref=small_out.at[me], send_sem=send_small.at[j], recv_sem=recv_small.at[j],
                device_id=(px, py, c), device_id_type=MESH)
            cp.start()
            sends.append(cp)

        for i, (k, r0, r1) in enumerate(cast_out):
            cast_buf[i][...] = cast_in[k][r0:r1, :].astype(BF16)
            local.append(pltpu.make_async_copy(cast_buf[i], cast_dst[i], cast_sem.at[i]))
            local[-1].start()

        def landed(j, k, which, sems_s, sems_r, device):
            px, py = chips[j]
            piece = outs[k].at[2 * px + py, half(k, which)]
            return pltpu.make_async_remote_copy(
                src_ref=piece, dst_ref=piece, send_sem=sems_s.at[j * n + k], recv_sem=sems_r.at[j * n + k],
                device_id=device, device_id_type=MESH)

        for j in range(len(chips)):
            for k in range(n):
                landed(j, k, c, send_far, recv_far, sibling).wait_recv()
                cp = landed(j, k, c, send_sib, recv_sib, sibling)
                cp.start()
                sends.append(cp)
        for j, (px, py) in enumerate(chips):
            for k in range(n):
                landed(j, k, 1 - c, send_sib, recv_sib, sibling).wait_recv()
            pltpu.make_async_remote_copy(
                src_ref=small_in, dst_ref=small_out.at[2 * px + py], send_sem=send_small.at[j],
                recv_sem=recv_small.at[j], device_id=(px, py, c), device_id_type=MESH).wait_recv()
        for cp in sends:
            cp.wait_send()
        for cp in local:
            cp.wait()

    vmem = pl.BlockSpec(memory_space=pltpu.VMEM)
    hbm = pl.BlockSpec(memory_space=pltpu.HBM)
    cast_shapes = [(r1 - r0, casts[k][0].shape[1]) for k, r0, r1 in cast_out]
    res = pl.pallas_call(
        body, name="allgather_weights",
        in_specs=[vmem] * (n + 1 + n_c), out_specs=[hbm] * (n + 1 + n_co),
        out_shape=[jax.ShapeDtypeStruct((N_CHIPS,) + s.shape, BF16) for s in shards]
        + [jax.ShapeDtypeStruct((N_CHIPS,) + small.shape, F32)]
        + [jax.ShapeDtypeStruct(s, BF16) for s in cast_shapes],
        scratch_shapes=[pltpu.VMEM(s.shape, BF16) for s in shards] + [pltpu.VMEM(s, BF16) for s in cast_shapes]
        + [pltpu.SemaphoreType.DMA((3 * n,)), pltpu.SemaphoreType.DMA((3 * n,)),
           pltpu.SemaphoreType.DMA((3 * n,)), pltpu.SemaphoreType.DMA((3 * n,)),
           pltpu.SemaphoreType.DMA((3,)), pltpu.SemaphoreType.DMA((3,)),
           pltpu.SemaphoreType.DMA((n + 1,)), pltpu.SemaphoreType.DMA((n_co,))],
        compiler_params=_params(None, 40),
    )(*shards, small, *[a for a, _ in casts])
    return res[:n], res[n], res[n + 1:]


def _adamw(w, g, m, v):
    m = ADAM_B1 * m + (1.0 - ADAM_B1) * g
    v = ADAM_B2 * v + (1.0 - ADAM_B2) * (g * g)
    m_hat = m / (1.0 - ADAM_B1 ** ADAM_STEP)
    v_hat = v / (1.0 - ADAM_B2 ** ADAM_STEP)
    delta = -ADAM_LR * (m_hat / (jnp.sqrt(v_hat) + ADAM_EPS) + ADAM_WD * w)
    return delta, m, v


RS_PIECE_ROWS = 128
RS_PIECE_COLS = 512


def _reduce_adam_all(grads, ws, ms, vs, small, bases=None, seeds=None):
    n_w = len(grads)
    pieces = _reduce_pieces(grads, bases)
    lanes = [pieces[0::2], pieces[1::2]]
    n_lane = len(_reduce_scratch())
    n_small = len(_small_sum_scratch(small.shape))
    seeds = seeds or [None] * n_w
    seeded = [(k, a) for k, seed in enumerate(seeds) if seed is not None for a in range(4)]
    n_in = 4 * n_w + 1

    def body(*refs):
        refs = list(refs)
        del refs[n_in:n_in + len(seeded)]
        small_in = refs.pop(4 * n_w)
        small_out = refs.pop(8 * n_w)
        small_scratch = [refs.pop() for _ in range(n_small)][::-1]
        sends = _small_sum_start(small_in, *small_scratch)
        arrays, scratch = refs[:8 * n_w], refs[8 * n_w:]
        runs = [_reduce_ticks(lane, n_w, arrays + scratch[i * n_lane:(i + 1) * n_lane])
                for i, lane in enumerate(lanes) if lane]
        for t in range(max(len(ticks) for ticks, _ in runs)):
            for ticks, _ in runs:
                if t < len(ticks):
                    ticks[t]()
        for _, drain in runs:
            drain()
        _small_sum_finish(sends, small_scratch[0], small_out)

    hbm = pl.BlockSpec(memory_space=pltpu.HBM)
    vmem = pl.BlockSpec(memory_space=pltpu.VMEM)
    outs = pl.pallas_call(
        body, name="reduce_adam_all",
        in_specs=[hbm] * (4 * n_w) + [vmem] + [hbm] * len(seeded), out_specs=[hbm] * (4 * n_w) + [vmem],
        out_shape=[jax.ShapeDtypeStruct(w.shape, F32) for _ in range(4) for w in ws]
        + [jax.ShapeDtypeStruct(small.shape, F32)],
        input_output_aliases={n_in + i: a * n_w + k for i, (k, a) in enumerate(seeded)},
        scratch_shapes=_reduce_scratch() * len(lanes) + _small_sum_scratch(small.shape),
        compiler_params=_params(None, 48),
    )(*grads, *ws, *ms, *vs, small, *[seeds[k][a] for k, a in seeded])
    return [outs[i * n_w:(i + 1) * n_w] for i in range(4)], outs[4 * n_w]


def _reduce_pieces(grads, bases=None):
    pieces = []
    for k, g in enumerate(grads):
        hr, cols = g.shape[1] // 2, g.shape[2]
        pr, pc = min(hr, RS_PIECE_ROWS), min(cols, RS_PIECE_COLS)
        base = bases[k] if bases else 0
        pieces += [(k, ro, hr, co, pr, pc, base) for ro in range(0, hr, pr) for co in range(0, cols, pc)]
    return pieces


def _reduce_scratch():
    P, C = RS_PIECE_ROWS, RS_PIECE_COLS
    return [
        pltpu.VMEM((3, N_CHIPS, P, C), F32), pltpu.VMEM((3, N_CHIPS, P, C), F32),
        pltpu.VMEM((2, N_CHIPS, P, C), BF16), pltpu.VMEM((2, N_CHIPS, P, C), BF16),
        pltpu.VMEM((2, N_CHIPS, P, C), F32),
        pltpu.VMEM((2, 3, P, C), BF16), pltpu.VMEM((2, 3, P, C), BF16),
        pltpu.VMEM((2, 2, P, C), F32),
        pltpu.VMEM((2, 3, 2, P, C), F32), pltpu.VMEM((2, 4, 2, P, C), F32),
        pltpu.SemaphoreType.DMA((3, 2)), pltpu.SemaphoreType.DMA((2, 3, 2)),
        pltpu.SemaphoreType.DMA((2,)), pltpu.SemaphoreType.DMA((2,)),
        pltpu.SemaphoreType.DMA((2, 3)), pltpu.SemaphoreType.DMA((2, 3)),
        pltpu.SemaphoreType.DMA((2,)), pltpu.SemaphoreType.DMA((2,)),
        pltpu.SemaphoreType.DMA((2, 4, 2))]


def _reduce_ticks(pieces, n_w, refs):
    n = len(pieces)

    def build(*refs):
        g_in, w_in, m_in, v_in = (refs[i * n_w:(i + 1) * n_w] for i in range(4))
        g_out, d_out, m_out, v_out = (refs[(4 + i) * n_w:(5 + i) * n_w] for i in range(4))
        (gm, go, sb1, rb1, part, sb2, rb2, fin, wmv, outs,
         ld_sem, wmv_sem, s1_send, s1_recv, s2_send, s2_recv, s3_send, s3_recv, out_sem) = refs[8 * n_w:]
        x, y, c, chips = _mesh_place()
        me = 2 * x + y
        sibling = (x, y, 1 - c)

        def at_hbm(i, which, in_shard):
            _, ro, hr, co, pr, pc, base = pieces[i]
            half = c if which == 0 else 1 - c
            return pl.ds(pl.multiple_of((base if in_shard else 0) + half * hr + ro, 64), pr), pl.ds(co, pc)

        def win(i):
            return pl.ds(0, pieces[i][4]), pl.ds(0, pieces[i][5])

        every = slice(None)

        def loads(i):
            k, s = pieces[i][0], i % 3
            return [pltpu.make_async_copy(g_in[k].at[(every,) + at_hbm(i, h, False)], buf.at[(s, every) + win(i)],
                                          ld_sem.at[s, h])
                    for h, buf in enumerate((gm, go))]

        def wmv_loads(i):
            k, s = pieces[i][0], i % 2
            return [pltpu.make_async_copy(src[k].at[at_hbm(i, h, True)], wmv.at[(s, a, h) + win(i)], wmv_sem.at[s, a, h])
                    for a, src in enumerate((w_in, m_in, v_in)) for h in range(2)]

        def stores(i):
            k, s = pieces[i][0], i % 2
            return [pltpu.make_async_copy(outs.at[(s, a, h) + win(i)], dst[k].at[at_hbm(i, h, True)], out_sem.at[s, a, h])
                    for a, dst in enumerate((g_out, d_out, m_out, v_out)) for h in range(2)]

        def swap1(i):
            s = i % 2
            return pltpu.make_async_remote_copy(
                src_ref=sb1.at[(s, every) + win(i)], dst_ref=rb1.at[(s, every) + win(i)],
                send_sem=s1_send.at[s], recv_sem=s1_recv.at[s], device_id=sibling, device_id_type=MESH)

        def far2(i, j):
            s = i % 2
            px, py = chips[j]
            return pltpu.make_async_remote_copy(
                src_ref=sb2.at[(s, j) + win(i)], dst_ref=rb2.at[(s, j) + win(i)],
                send_sem=s2_send.at[s, j], recv_sem=s2_recv.at[s, j], device_id=(px, py, c), device_id_type=MESH)

        def swap3(i):
            s = i % 2
            return pltpu.make_async_remote_copy(
                src_ref=fin.at[(s, 0) + win(i)], dst_ref=fin.at[(s, 1) + win(i)],
                send_sem=s3_send.at[s], recv_sem=s3_recv.at[s], device_id=sibling, device_id_type=MESH)

        def stage0(i):
            for cp in loads(i):
                cp.start()

        def stage1(i):
            s, s3 = i % 2, i % 3
            for cp in loads(i):
                cp.wait()
            sb1[(s, every) + win(i)] = go[(s3, every) + win(i)].astype(BF16)
            swap1(i).start()

        def stage2(i):
            s, s3 = i % 2, i % 3
            swap1(i).wait()
            part[(s, every) + win(i)] = gm[(s3, every) + win(i)] + rb1[(s, every) + win(i)].astype(F32)
            for j, (px, py) in enumerate(chips):
                sb2[(s, j) + win(i)] = part[(s, 2 * px + py) + win(i)].astype(BF16)
                far2(i, j).start()

        def stage3(i):
            s = i % 2
            total = part[(s, me) + win(i)]
            for j in range(3):
                far2(i, j).wait()
                total = total + rb2[(s, j) + win(i)].astype(F32)
            fin[(s, 0) + win(i)] = total
            swap3(i).start()
            for cp in wmv_loads(i):
                cp.start()

        def stage4(i):
            s = i % 2
            if i >= 2:
                for cp in stores(i - 2):
                    cp.wait()
            swap3(i).wait()
            for cp in wmv_loads(i):
                cp.wait()
            both = (every,) + win(i)
            g = fin[(s,) + both]
            delta, m_new, v_new = _adamw(wmv[(s, 0) + both], g, wmv[(s, 1) + both], wmv[(s, 2) + both])
            outs[(s, 0) + both] = g
            outs[(s, 1) + both] = delta
            outs[(s, 2) + both] = m_new
            outs[(s, 3) + both] = v_new
            for cp in stores(i):
                cp.start()

        stages = (stage0, stage1, stage2, stage3, stage4)

        def tick(t):
            for age in reversed(range(len(stages))):
                if 0 <= t - age < n:
                    stages[age](t - age)

        def drain():
            for i in range(max(0, n - 2), n):
                for cp in stores(i):
                    cp.wait()

        return [functools.partial(tick, t) for t in range(n + len(stages) - 1)], drain

    return build(*refs)


N_DEVICES = 8


def _small_sum_scratch(shape):
    return [pltpu.VMEM((N_DEVICES,) + shape, F32),
            pltpu.SemaphoreType.DMA((N_DEVICES - 1,)), pltpu.SemaphoreType.DMA((N_DEVICES - 1,))]


def _small_sum_start(part_ref, buf, send_sem, recv_sem):
    x, y, c, _ = _mesh_place()
    me = 4 * x + 2 * y + c
    buf[me] = part_ref[...]
    sends = []
    for k in range(1, N_DEVICES):
        peer = ((1 - x) if k & 4 else x, (1 - y) if k & 2 else y, (1 - c) if k & 1 else c)
        cp = pltpu.make_async_remote_copy(src_ref=part_ref, dst_ref=buf.at[me], send_sem=send_sem.at[k - 1],
                                          recv_sem=recv_sem.at[k - 1], device_id=peer, device_id_type=MESH)
        cp.start()
        sends.append(cp)
    return sends


def _small_sum_finish(sends, buf, out_ref):
    for cp in sends:
        cp.wait_recv()
    total = buf[0]
    for s in range(1, N_DEVICES):
        total = total + buf[s]
    out_ref[...] = total
    for cp in sends:
        cp.wait_send()


def _adam_small(w, g, m, v):
    def body(w_ref, g_ref, m_ref, v_ref, d_ref, mo_ref, vo_ref):
        delta, m_new, v_new = _adamw(w_ref[...], g_ref[...], m_ref[...], v_ref[...])
        d_ref[...] = delta
        mo_ref[...] = m_new
        vo_ref[...] = v_new

    vmem = pl.BlockSpec(memory_space=pltpu.VMEM)
    return pl.pallas_call(
        body, name="adam_small", in_specs=[vmem] * 4, out_specs=[vmem] * 3,
        out_shape=[jax.ShapeDtypeStruct(w.shape, F32)] * 3,
    )(w, g, m, v)


BIG = ("a_w_in", "a_w_group", "a_w_out", "w_kv", "b_w_in", "b_w_out", "ple_w", "ple_gate_w")
SMALL = ("a_norm", "a_scale", "kv_norm", "b_norm", "k_norm", "b_q_norm")
SMALL_SHARDED = ("a_norm", "a_scale")
WEIGHTS = ("a_norm", "a_w_in", "a_w_group", "a_scale", "a_w_out", "kv_norm", "w_kv", "k_norm", "b_norm", "b_w_in",
           "b_q_norm", "b_w_out", "ple_w", "ple_gate_w")


def _as_matrix(a):
    return a.reshape(-1, a.shape[-1])


def _pack_small(arrs):
    rows = [jnp.pad(a.reshape(-1), (0, D_MODEL - a.size)) for a in arrs]
    rows += [jnp.zeros((D_MODEL,), F32)] * (8 - len(rows))
    return jnp.stack(rows)


def kernel(x, p, a_norm, a_w_in, a_w_group, a_scale, a_w_out, kv_norm, w_kv, k_norm, b_norm, b_w_in, b_q_norm, b_w_out, ple_w, ple_gate_w, loss_target, m_a_norm, m_a_w_in, m_a_w_group, m_a_scale, m_a_w_out, m_kv_norm, m_w_kv, m_k_norm, m_b_norm, m_b_w_in, m_b_q_norm, m_b_w_out, m_ple_w, m_ple_gate_w, v_a_norm, v_a_w_in, v_a_w_group, v_a_scale, v_a_w_out, v_kv_norm, v_w_kv, v_k_norm, v_b_norm, v_b_w_in, v_b_q_norm, v_b_w_out, v_ple_w, v_ple_gate_w):
    wts = dict(a_norm=a_norm, a_w_in=a_w_in, a_w_group=a_w_group, a_scale=a_scale, a_w_out=a_w_out, kv_norm=kv_norm,
               w_kv=w_kv, k_norm=k_norm, b_norm=b_norm, b_w_in=b_w_in, b_q_norm=b_q_norm, b_w_out=b_w_out,
               ple_w=ple_w, ple_gate_w=ple_gate_w)
    mom = dict(a_norm=m_a_norm, a_w_in=m_a_w_in, a_w_group=m_a_w_group, a_scale=m_a_scale, a_w_out=m_a_w_out,
               kv_norm=m_kv_norm, w_kv=m_w_kv, k_norm=m_k_norm, b_norm=m_b_norm, b_w_in=m_b_w_in,
               b_q_norm=m_b_q_norm, b_w_out=m_b_w_out, ple_w=m_ple_w, ple_gate_w=m_ple_gate_w)
    var = dict(a_norm=v_a_norm, a_w_in=v_a_w_in, a_w_group=v_a_w_group, a_scale=v_a_scale, a_w_out=v_a_w_out,
               kv_norm=v_kv_norm, w_kv=v_w_kv, k_norm=v_k_norm, b_norm=v_b_norm, b_w_in=v_b_w_in,
               b_q_norm=v_b_q_norm, b_w_out=v_b_w_out, ple_w=v_ple_w, ple_gate_w=v_ple_gate_w)
    S = x.shape[1]
    chip = 2 * lax.axis_index("x") + lax.axis_index("y")

    sharded_small = jnp.concatenate([a_norm.reshape(1, 256), a_scale.reshape(1, 256), jnp.zeros((6, 256), F32)], axis=0)
    later = ("a_w_group", "a_w_out", "w_kv", "b_w_in", "b_w_out", "ple_w", "ple_gate_w")
    (a_w_in_full,), small_full, copies = _allgather_weights(
        [_as_matrix(a_w_in)], sharded_small,
        [(_as_matrix(wts[n]), [(0, 256), (256, 512)] if n.startswith("ple") else [(0, _as_matrix(wts[n]).shape[0])])
         for n in later])
    local = dict(zip(("a_w_group", "a_w_out", "w_kv", "b_w_in", "b_w_out", "ple_w0", "ple_w1", "ple_gate_w0",
                      "ple_gate_w1"), copies))
    full = dict(a_w_in=a_w_in_full,
                a_norm=small_full[:, 0, :].reshape(1, D_MODEL), a_scale=small_full[:, 1, :].reshape(1, D_MODEL),
                kv_norm=kv_norm.reshape(1, D_MODEL), b_norm=b_norm.reshape(1, D_MODEL), k_norm=k_norm, b_q_norm=b_q_norm)

    def shards(t):
        out = {}
        for n in BIG:
            for entry in ((n + "0", n + "1") if n.startswith("ple") else (n,)):
                out[entry] = _as_matrix(t[n])
        return out

    base = {n: 256 if n.startswith("ple") and n.endswith("1") else 0 for n in shards(wts)}
    state = (shards(wts), shards(mom), shards(var), base)
    grad_x, grads, updates, small_part = _local_step(x.reshape(S, D_MODEL), p, loss_target.reshape(S, D_MODEL),
                                                     full, local, state)

    names = sorted(grads)
    reduced, small_sum = _reduce_adam_all(
        [grads[n] for n in names], *[[t[n] for n in names] for t in state[:3]], small_part,
        bases=[base[n] for n in names], seeds=[updates.get(n[:-1] + "1") if n.startswith("ple") else None for n in names])
    for i, n in enumerate(names):
        updates[n] = tuple(group[i] for group in reduced)
    out_g, out_d, out_m, out_v = {}, {}, {}, {}
    for n in BIG:
        for i, out in enumerate((out_g, out_d, out_m, out_v)):
            out[n] = updates[n + "0" if n.startswith("ple") else n][i].reshape(wts[n].shape)

    loss = small_sum[len(SMALL), 0]
    small_rows = []
    for i, n in enumerate(SMALL):
        row = small_sum[i]
        if n in SMALL_SHARDED:
            row = lax.dynamic_slice(row, (chip * 256,), (256,))
        else:
            row = row[:wts[n].size]
        small_rows.append(row)
    g_small = _pack_small(small_rows)
    d_small, m_small, v_small = _adam_small(_pack_small([wts[n] for n in SMALL]), g_small,
                                            _pack_small([mom[n] for n in SMALL]), _pack_small([var[n] for n in SMALL]))
    for i, n in enumerate(SMALL):
        shape, size = wts[n].shape, wts[n].size
        out_g[n], out_d[n], out_m[n], out_v[n] = (t[i, :size].reshape(shape) for t in (g_small, d_small, m_small, v_small))

    return (loss, grad_x.reshape(1, S, D_MODEL), *[out_g[n] for n in WEIGHTS], *[out_d[n] for n in WEIGHTS],
            *[out_m[n] for n in WEIGHTS], *[out_v[n] for n in WEIGHTS])
```

```python
import functools

import jax
import jax.numpy as jnp
from jax import lax
from jax.experimental import pallas as pl
from jax.experimental.pallas import tpu as pltpu

F32 = jnp.float32
BF16 = jnp.bfloat16
MESH = pl.DeviceIdType.MESH

D_MODEL = 1024
N_HEADS = 16
HEAD_DIM = 64
PLE_DIM = 256
N_GROUPS = 4
GROUP_DIM = 256
POOL_WINDOWS = (2, 4, 8, 16)
N_CHIPS = 4
EPS = 1e-6
SB_SCALE = HEAD_DIM ** -0.5

ADAM_LR = 0.001
ADAM_B1 = 0.9
ADAM_B2 = 0.999
ADAM_EPS = 1e-08
ADAM_WD = 0.01
ADAM_STEP = 10

ROW_TILE = 256
WIDE_ROW_TILE = 512
EXP_UNDERFLOW = -104.0
ATT_Q_TILE = 512
ATT_K_TILE = 256
WGRAD_SEQ_TILE = 1024
WGRAD_ACC_BYTES = 4 * 1024 * 1024
MIB = 1024 * 1024


def _params(semantics=None, vmem_mib=48):
    return pltpu.CompilerParams(dimension_semantics=semantics, vmem_limit_bytes=vmem_mib * MIB)


def _dot(a, b):
    return jnp.dot(a, b, preferred_element_type=F32)


def _dot_nt(a, b):
    return lax.dot_general(a, b, (((1,), (1,)), ((), ())), preferred_element_type=F32)


def _dot_tn(a, b):
    return lax.dot_general(a, b, (((0,), (0,)), ((), ())), preferred_element_type=F32)


def _hilo(x):
    hi = x.astype(BF16)
    lo = (x - hi.astype(F32)).astype(BF16)
    return hi, lo


def _dot_hilo(x, w):
    hi, lo = _hilo(x)
    return _dot(hi, w) + _dot(lo, w)


def _sigmoid(z):
    return jax.nn.sigmoid(z)


def _dsilu(z, sg):
    return sg * (1.0 + z * (1.0 - sg))


def _mask_bf16(cond):
    return jnp.where(cond, 1.0, 0.0).astype(BF16)


def _head_mean_matrix():
    r = lax.broadcasted_iota(jnp.int32, (256, 256), 0) // HEAD_DIM
    c = lax.broadcasted_iota(jnp.int32, (256, 256), 1) // HEAD_DIM
    return _mask_bf16(r == c)


def _head_mean(x, bd):
    parts = []
    for s in range(x.shape[1] // 256):
        parts.append(_dot_hilo(x[:, s * 256:(s + 1) * 256], bd))
    out = parts[0] if len(parts) == 1 else jnp.concatenate(parts, axis=1)
    return out * (1.0 / HEAD_DIM)


def _a_in(x, gain, w_sh, gather=()):
    S = x.shape[0]
    tm = 512
    nsh, _, wn = w_sh.shape

    def body(x_ref, g_ref, w_ref, uz_ref, h_ref):
        @pl.when(pl.program_id(1) == 0)
        def _():
            xv = x_ref[...]
            r = lax.rsqrt(jnp.mean(xv * xv, axis=-1, keepdims=True) + EPS)
            h_ref[...] = (xv * r * g_ref[...]).astype(BF16)

        uz_ref[...] = _dot(h_ref[...], w_ref[0])

    return _call_with_gather(
        body, name="a_in", grid=(S // tm, nsh),
        in_specs=[pl.BlockSpec((tm, D_MODEL), lambda i, j: (i, 0)),
                  pl.BlockSpec((1, D_MODEL), lambda i, j: (0, 0)),
                  pl.BlockSpec((1, D_MODEL, wn), lambda i, j: (j, 0, 0))],
        out_specs=[pl.BlockSpec((tm, wn), lambda i, j: (i, j)),
                   pl.BlockSpec((tm, D_MODEL), lambda i, j: (i, 0))],
        out_shape=[jax.ShapeDtypeStruct((S, nsh * wn), F32),
                   jax.ShapeDtypeStruct((S, D_MODEL), BF16)],
        args=(x, gain, w_sh), gather=gather)


def _inv_count(first_row, rows, w):
    t1 = first_row + 1 + lax.broadcasted_iota(jnp.int32, (rows, 1), 0)
    return 1.0 / jnp.minimum(t1, w).astype(F32)


def _group_weight(wg_ref, g):
    return jnp.concatenate([wg_ref[sh, g] for sh in range(N_CHIPS)], axis=0)


def _a_mix(uz, wg, scale, gather=()):
    S = uz.shape[0]
    tm = ROW_TILE

    def body(u_ref, up_ref, z_ref, wg_ref, sc_ref, ga_ref, p_ref):
        i = pl.program_id(0)
        row = lax.broadcasted_iota(jnp.int32, (tm, tm), 0)
        col = lax.broadcasted_iota(jnp.int32, (tm, tm), 1)
        d = row - col
        for g, w in enumerate(POOL_WINDOWS):
            cols = slice(g * GROUP_DIM, (g + 1) * GROUP_DIM)
            t_main = _mask_bf16((d >= 0) & (d < w))
            t_halo = _mask_bf16(d + tm < w)
            u = u_ref[:, cols]
            up = jnp.where(i > 0, up_ref[:, cols], 0.0)
            hi, lo = _hilo(u)
            hip, lop = _hilo(up)
            wsum = _dot(t_main, hi) + _dot(t_main, lo) + _dot(t_halo, hip) + _dot(t_halo, lop)
            pooled = (wsum * _inv_count(i * tm, tm, w) - u).astype(BF16)
            p_ref[:, cols] = pooled
            mraw = _dot(pooled, _group_weight(wg_ref, g))
            z = z_ref[:, cols]
            ga_ref[:, cols] = (mraw * sc_ref[:, cols] * (z * _sigmoid(z))).astype(BF16)

    return _call_with_gather(
        body, name="a_mix", grid=(S // tm,),
        in_specs=[pl.BlockSpec((tm, D_MODEL), lambda i: (i, 0)),
                  pl.BlockSpec((tm, D_MODEL), lambda i: (jnp.maximum(i - 1, 0), 0)),
                  pl.BlockSpec((tm, D_MODEL), lambda i: (i, 1)),
                  pl.BlockSpec((N_CHIPS, N_GROUPS, 64, GROUP_DIM), lambda i: (0, 0, 0, 0)),
                  pl.BlockSpec((1, D_MODEL), lambda i: (0, 0))],
        out_specs=[pl.BlockSpec((tm, D_MODEL), lambda i: (i, 0)),
                   pl.BlockSpec((tm, D_MODEL), lambda i: (i, 0))],
        out_shape=[jax.ShapeDtypeStruct((S, D_MODEL), BF16),
                   jax.ShapeDtypeStruct((S, D_MODEL), BF16)],
        args=(uz, uz, uz, wg, scale), gather=gather)


def _out_ple(name, gated, x_in, w_out, p, layer, ple_w, ple_g, target=None, gather=()):
    S = x_in.shape[0]
    tm = WIDE_ROW_TILE
    with_loss = target is not None

    def body(*refs):
        if with_loss:
            g_ref, x_ref, wo_ref, p_ref, pw_ref, pg_ref, t_ref, xm_ref, dx_ref, e_ref, gt_ref, loss_ref = refs
        else:
            g_ref, x_ref, wo_ref, p_ref, pw_ref, pg_ref, xm_ref, xo_ref, e_ref, gt_ref = refs
        xm = x_ref[...] + _dot(g_ref[...], wo_ref[...])
        xm_ref[...] = xm
        pb = p_ref[...].astype(BF16)
        e = jnp.concatenate([_dot(pb, pw_ref[sh]) for sh in range(N_CHIPS)], axis=1)
        pg = jnp.concatenate([pg_ref[sh] for sh in range(N_CHIPS)], axis=0)
        gate = _sigmoid(_dot(xm.astype(BF16), pg))
        e_ref[...] = e.astype(BF16)
        gt_ref[...] = gate.astype(BF16)
        xo = xm + e * gate
        if with_loss:
            diff = xo - t_ref[...]
            dx_ref[...] = diff * (1.0 / D_MODEL)

            @pl.when(pl.program_id(0) == 0)
            def _():
                loss_ref[...] = jnp.zeros_like(loss_ref)

            loss_ref[...] += jnp.sum(diff * diff) * (0.5 / D_MODEL)
        else:
            xo_ref[...] = xo

    row = pl.BlockSpec((tm, D_MODEL), lambda i: (i, 0))
    in_specs = [row, row,
                pl.BlockSpec((D_MODEL, D_MODEL), lambda i: (0, 0)),
                pl.BlockSpec((None, None, tm, PLE_DIM), lambda i: (layer, 0, i, 0)),
                pl.BlockSpec((N_CHIPS, PLE_DIM, 256), lambda i: (0, 0, 0)),
                pl.BlockSpec((N_CHIPS, 256, D_MODEL), lambda i: (0, 0, 0))]
    args = [gated, x_in, w_out, p, ple_w, ple_g]
    out_specs = [row, row, row, row]
    out_shape = [jax.ShapeDtypeStruct((S, D_MODEL), F32), jax.ShapeDtypeStruct((S, D_MODEL), F32),
                 jax.ShapeDtypeStruct((S, D_MODEL), BF16), jax.ShapeDtypeStruct((S, D_MODEL), BF16)]
    if with_loss:
        in_specs.append(row)
        args.append(target)
        out_specs.append(pl.BlockSpec((8, 128), lambda i: (0, 0)))
        out_shape.append(jax.ShapeDtypeStruct((8, 128), F32))
    return _call_with_gather(body, name=name, grid=(S // tm,), in_specs=in_specs, out_specs=out_specs,
                             out_shape=out_shape, args=args, gather=gather)


def _b_in(x, kv_gain, b_gain, k_gain_t, q_gain_t, w_kv, w_in, gather=()):
    S = x.shape[0]
    tm = ROW_TILE

    def body(x_ref, kvg_ref, bg_ref, kg_ref, qg_ref, wkv_ref, win_ref,
             hkv_ref, hb_ref, kraw_ref, qraw_ref, k_ref, q_ref, v_ref, z_ref):
        xv = x_ref[...]
        y = xv * lax.rsqrt(jnp.mean(xv * xv, axis=-1, keepdims=True) + EPS)
        hkv = (y * kvg_ref[...]).astype(BF16)
        hb = (y * bg_ref[...]).astype(BF16)
        hkv_ref[...] = hkv
        hb_ref[...] = hb
        bd = _head_mean_matrix()

        def head_norm(raw, gain):
            rr = lax.rsqrt(_head_mean(raw * raw, bd) + EPS)
            return raw * rr * gain

        for sh in range(N_CHIPS):
            kvc = _dot(hkv, wkv_ref[sh])
            qzc = _dot(hb, win_ref[sh])
            cols = slice((sh % 2) * 512, (sh % 2) * 512 + 512)
            if sh < 2:
                kraw_ref[:, cols] = kvc.astype(BF16)
                qraw_ref[:, cols] = qzc.astype(BF16)
                k_ref[:, cols] = head_norm(kvc, kg_ref[:, cols]).astype(BF16)
                q_ref[:, cols] = (head_norm(qzc, qg_ref[:, cols]) * SB_SCALE).astype(BF16)
            else:
                v_ref[:, cols] = kvc.astype(BF16)
                z_ref[:, cols] = qzc.astype(BF16)

    row = pl.BlockSpec((tm, D_MODEL), lambda i: (i, 0))
    vec = pl.BlockSpec((1, D_MODEL), lambda i: (0, 0))
    wsp = pl.BlockSpec((N_CHIPS, D_MODEL, 512), lambda i: (0, 0, 0))
    return _call_with_gather(
        body, name="b_in", grid=(S // tm,),
        in_specs=[row, vec, vec, vec, vec, wsp, wsp],
        out_specs=[row] * 8,
        out_shape=[jax.ShapeDtypeStruct((S, D_MODEL), BF16)] * 8,
        args=(x, kv_gain, b_gain, k_gain_t, q_gain_t, w_kv, w_in), gather=gather, vmem_mib=56)


def _softplus_parts(z):
    e = jnp.exp(-jnp.abs(z))
    return -(jnp.maximum(z, 0.0) + jnp.log(1.0 + e)), e


def _add_rows(total, rows, update):
    lo, hi = rows
    parts = ([total[:lo]] if lo else []) + [total[lo:hi] + update] + ([total[hi:]] if hi < total.shape[0] else [])
    return parts[0] if len(parts) == 1 else jnp.concatenate(parts, axis=0)


def _attn_fwd(q, k, v, zgate, gather=()):
    S = q.shape[0]
    tq, tk = ATT_Q_TILE, ATT_K_TILE
    kpq = tq // tk
    assert kpq == 2

    def body(q_ref, k_ref, v_ref, z_ref, o_ref, g_ref, lt_ref, steps_ref):
        qi = pl.program_id(1)
        lane = lax.broadcasted_iota(jnp.int32, (1, 128), 1)
        ri = lax.broadcasted_iota(jnp.int32, (tk, tk), 0)
        ci = lax.broadcasted_iota(jnp.int32, (tk, tk), 1)
        later_mat = _mask_bf16(ri > ci)
        causal = ci < ri
        qv = q_ref[...]
        first = lane < HEAD_DIM
        q_heads = (jnp.where(first, qv, jnp.zeros_like(qv)), jnp.where(first, jnp.zeros_like(qv), qv))

        def step(blocks, carry):
            chains = [(b, h) for b in range(len(blocks)) for h in range(2)]
            rows = [r for _, r, _ in blocks]
            s0 = [pl.multiple_of(kj * tk, tk) for kj, _, _ in blocks]
            kb = [k_ref[pl.ds(s, tk), :] for s in s0]
            vb = [v_ref[pl.ds(s, tk), :] for s in s0]
            visible = [causal if masked else None for _, _, masked in blocks]
            z = {c: _dot_nt(q_heads[c[1]][rows[c[0]][0]:rows[c[0]][1]], kb[c[0]]) for c in chains}
            run = [carry[0], carry[2]]
            log_own, later, run_at = {}, {}, {}
            for c in chains:
                b, h = c
                lk = _softplus_parts(z[c])[0]
                if visible[b] is not None:
                    lk = jnp.where(visible[b], lk, 0.0)
                log_own[c] = z[c] + lk
                later[c] = _dot(lk.astype(BF16), later_mat)
                run_at[c] = run[h][rows[b][0]:rows[b][1]]
                run[h] = _add_rows(run[h], rows[b], jnp.sum(lk, axis=-1, keepdims=True))
            acc = [carry[1], carry[3]]
            for c in chains:
                b, h = c
                a = jnp.exp(log_own[c] + later[c] + run_at[c])
                if visible[b] is not None:
                    a = jnp.where(visible[b], a, 0.0)
                acc[h] = _add_rows(acc[h], rows[b], _dot(a.astype(BF16), vb[b]))
            return run[0], acc[0], run[1], acc[1]

        zero1, zero128 = jnp.zeros((tq, 1), F32), jnp.zeros((tq, 128), F32)
        carry = step([(qi * kpq + 1, (tk, tq), True), (qi * kpq, (tk, tq), False), (qi * kpq, (0, tk), True)],
                     (zero1, zero128, zero1, zero128))

        def low(run):
            return jnp.max(run)

        def pair_more(c):
            return (c[0] < qi) & (jnp.maximum(low(c[1][tk:]), low(c[3][tk:])) > EXP_UNDERFLOW)

        def pair_step(c):
            last = (qi - c[0]) * kpq - 1
            return (c[0] + 1, *step([(last, (0, tq), False), (last - 1, (0, tq), False)], c[1:]))

        pairs, *carry = lax.while_loop(pair_more, pair_step, (jnp.int32(0), *carry))
        left = (qi - pairs) * kpq

        def single_more(c):
            return (c[0] < left) & (jnp.maximum(low(c[1][:tk]), low(c[3][:tk])) > EXP_UNDERFLOW)

        def single_step(c):
            return (c[0] + 1, *step([(left - 1 - c[0], (0, tk), False)], c[1:]))

        singles, *carry = lax.while_loop(single_more, single_step, (jnp.int32(0), *carry))
        steps_ref[...] = jnp.concatenate([jnp.full((4, 128), pairs, F32), jnp.full((4, 128), singles, F32)], axis=0)
        o_tot = jnp.where(first, carry[1], carry[3])
        l_tot = jnp.where(first, carry[0], carry[2])
        o_ref[...] = o_tot.astype(BF16)
        lt_ref[...] = l_tot
        zz = z_ref[...].astype(F32)
        g_ref[...] = (o_tot * (zz * _sigmoid(zz))).astype(BF16)

    blk = pl.BlockSpec((tq, 128), lambda hp, qi: (qi, hp))
    seq = pl.BlockSpec((S, 128), lambda hp, qi: (0, hp))
    return _call_with_gather(
        body, name="attn_fwd", grid=(D_MODEL // 128, S // tq),
        in_specs=[blk, seq, seq, blk],
        out_specs=[blk, blk, blk, pl.BlockSpec((None, None, 8, 128), lambda hp, qi: (hp, qi, 0, 0))],
        out_shape=[jax.ShapeDtypeStruct((S, D_MODEL), BF16)] * 2 + [jax.ShapeDtypeStruct((S, D_MODEL), F32)]
        + [jax.ShapeDtypeStruct((D_MODEL // 128, S // tq, 8, 128), F32)],
        args=(q, k, v, zgate), gather=gather)


def _ple_out_bwd(name, dx_out, e, gate, ple_g, w_out):
    S = dx_out.shape[0]
    tm = WIDE_ROW_TILE

    def body(dx_ref, e_ref, gt_ref, pg_ref, wo_ref, de_ref, dgp_ref, dxm_ref, dg_ref):
        dxo = dx_ref[...]
        ev = e_ref[...].astype(F32)
        gv = gt_ref[...].astype(F32)
        de_ref[...] = (dxo * gv).astype(BF16)
        dgp = (dxo * ev * gv * (1.0 - gv)).astype(BF16)
        dgp_ref[...] = dgp
        pg = jnp.concatenate([pg_ref[sh] for sh in range(N_CHIPS)], axis=0)
        dxm = dxo + _dot_nt(dgp, pg)
        dxm_ref[...] = dxm
        dg_ref[...] = _dot_nt(dxm.astype(BF16), wo_ref[...]).astype(BF16)

    row = pl.BlockSpec((tm, D_MODEL), lambda i: (i, 0))
    return pl.pallas_call(
        body, name=name, grid=(S // tm,),
        in_specs=[row, row, row,
                  pl.BlockSpec((N_CHIPS, 256, D_MODEL), lambda i: (0, 0, 0)),
                  pl.BlockSpec((D_MODEL, D_MODEL), lambda i: (0, 0))],
        out_specs=[row, row, row, row],
        out_shape=[jax.ShapeDtypeStruct((S, D_MODEL), BF16), jax.ShapeDtypeStruct((S, D_MODEL), BF16),
                   jax.ShapeDtypeStruct((S, D_MODEL), F32), jax.ShapeDtypeStruct((S, D_MODEL), BF16)],
        compiler_params=_params(("arbitrary",)),
    )(dx_out, e, gate, ple_g, w_out)


def _attn_bwd(q, k, v, ltot, steps, dgated, o, zgate, reduce=None):
    S = q.shape[0]
    tq, tk = ATT_Q_TILE, ATT_K_TILE
    kpq = tq // tk
    nq = S // tq

    def body(q_ref, k_ref, v_ref, lt_ref, steps_ref, dg_ref, o_ref, z_ref, dq_ref, dk_ref, dv_ref, dz_ref,
             dk_acc, dv_acc):
        qi = pl.program_id(1)

        @pl.when(qi == 0)
        def _():
            dk_acc[...] = jnp.zeros_like(dk_acc)
            dv_acc[...] = jnp.zeros_like(dv_acc)

        lane = lax.broadcasted_iota(jnp.int32, (1, 128), 1)
        ri = lax.broadcasted_iota(jnp.int32, (tk, tk), 0)
        ci = lax.broadcasted_iota(jnp.int32, (tk, tk), 1)
        later_mat = _mask_bf16(ri > ci)
        before_mat = _mask_bf16(ri < ci)
        causal = ci < ri
        zz = z_ref[...].astype(F32)
        sg = _sigmoid(zz)
        dgv = dg_ref[...].astype(F32)
        dz_ref[...] = (dgv * o_ref[...].astype(F32) * _dsilu(zz, sg)).astype(BF16)
        dob = (dgv * (zz * sg)).astype(BF16)
        ltv = lt_ref[...]
        qv = q_ref[...]
        first = lane < HEAD_DIM
        masks = (first, jnp.logical_not(first))
        q_heads = [jnp.where(hm, qv, jnp.zeros_like(qv)) for hm in masks]
        do_heads = [jnp.where(hm, dob, jnp.zeros_like(dob)) for hm in masks]
        totals = [jnp.max(jnp.where(hm, ltv, -jnp.inf), axis=-1, keepdims=True) for hm in masks]

        def step(blocks, carry):
            chains = [(b, h) for b in range(len(blocks)) for h in range(2)]
            rows = [r for _, r, _ in blocks]
            cut = lambda t, b: t[rows[b][0]:rows[b][1]]
            s0 = [pl.multiple_of(kj * tk, tk) for kj, _, _ in blocks]
            kb = [k_ref[pl.ds(s, tk), :] for s in s0]
            vb = [v_ref[pl.ds(s, tk), :] for s in s0]
            visible = [causal if masked else None for _, _, masked in blocks]
            z = {c: _dot_nt(cut(q_heads[c[1]], c[0]), kb[c[0]]) for c in chains}
            da = {c: _dot_nt(cut(do_heads[c[1]], c[0]), vb[c[0]]) for c in chains}
            run = [carry[0], carry[3]]
            log_own, beta, later, base = {}, {}, {}, {}
            for c in chains:
                b, h = c
                lk = _softplus_parts(z[c])[0]
                if visible[b] is not None:
                    lk = jnp.where(visible[b], lk, 0.0)
                log_own[c] = z[c] + lk
                beta[c] = jnp.exp(log_own[c]).astype(BF16)
                later[c] = _dot(lk.astype(BF16), later_mat)
                run[h] = _add_rows(run[h], rows[b], jnp.sum(lk, axis=-1, keepdims=True))
                base[c] = cut(totals[h] - run[h], b)
            grun = [carry[1], carry[4]]
            a_bf, g_bf, gbefore, grun_at = {}, {}, {}, {}
            for c in chains:
                b, h = c
                a = jnp.exp(log_own[c] + later[c] + base[c])
                if visible[b] is not None:
                    a = jnp.where(visible[b], a, 0.0)
                a_bf[c] = a.astype(BF16)
                g = da[c] * a
                g_bf[c] = g.astype(BF16)
                gbefore[c] = _dot(g_bf[c], before_mat)
                grun_at[c] = cut(grun[h], b)
                grun[h] = _add_rows(grun[h], rows[b], jnp.sum(g, axis=-1, keepdims=True))
            dq = [carry[2], carry[5]]
            dk_blk = [jnp.zeros((tk, 128), F32) for _ in blocks]
            dv_blk = [jnp.zeros((tk, 128), F32) for _ in blocks]
            for c in chains:
                b, h = c
                g = g_bf[c].astype(F32)
                dz = g - beta[c].astype(F32) * (g + gbefore[c] + grun_at[c])
                if visible[b] is not None:
                    dz = jnp.where(visible[b], dz, 0.0)
                dzb = dz.astype(BF16)
                dq[h] = _add_rows(dq[h], rows[b], _dot(dzb, kb[b]))
                dk_blk[b] = dk_blk[b] + _dot_tn(dzb, cut(q_heads[h], b))
                dv_blk[b] = dv_blk[b] + _dot_tn(a_bf[c], cut(do_heads[h], b))
            for b in range(len(blocks)):
                dk_acc[pl.ds(s0[b], tk), :] += dk_blk[b]
                dv_acc[pl.ds(s0[b], tk), :] += dv_blk[b]
            return run[0], grun[0], dq[0], run[1], grun[1], dq[1]

        pairs = jnp.clip(jnp.max(steps_ref[0:4, :]).astype(jnp.int32), 0, qi)
        left = (qi - pairs) * kpq
        singles = jnp.clip(jnp.max(steps_ref[4:8, :]).astype(jnp.int32), 0, left)
        zero1, zero128 = jnp.zeros((tq, 1), F32), jnp.zeros((tq, 128), F32)
        carry = lax.fori_loop(left - singles, left, lambda kj, c: step([(kj, (0, tk), False)], c),
                              (zero1, zero1, zero128, zero1, zero1, zero128))
        carry = lax.fori_loop(qi - pairs, qi,
                              lambda n, c: step([(n * kpq, (0, tq), False), (n * kpq + 1, (0, tq), False)], c), carry)
        carry = step([(qi * kpq, (0, tk), True), (qi * kpq, (tk, tq), False), (qi * kpq + 1, (tk, tq), True)], carry)
        dq_ref[...] = jnp.where(first, carry[2], carry[5]).astype(BF16)

        @pl.when(qi == nq - 1)
        def _():
            dk_ref[...] = dk_acc[...].astype(BF16)
            dv_ref[...] = dv_acc[...].astype(BF16)

    blk = pl.BlockSpec((tq, 128), lambda hp, qi: (qi, hp))
    seq = pl.BlockSpec((S, 128), lambda hp, qi: (0, hp))
    return _call_with_gather(
        body, name="attn_bwd", grid=(D_MODEL // 128, nq),
        in_specs=[blk, seq, seq, blk, pl.BlockSpec((None, None, 8, 128), lambda hp, qi: (hp, qi, 0, 0)),
                  blk, blk, blk],
        out_specs=[blk, seq, seq, blk],
        out_shape=[jax.ShapeDtypeStruct((S, D_MODEL), BF16)] * 4,
        scratch_shapes=[pltpu.VMEM((S, 128), F32), pltpu.VMEM((S, 128), F32)],
        args=(q, k, v, ltot, steps, dgated, o, zgate), reduce=reduce, vmem_mib=56)


def _rms_bwd(xv, dh_gain_sum):
    r = lax.rsqrt(jnp.mean(xv * xv, axis=-1, keepdims=True) + EPS)
    xhat = xv * r
    dx = r * (dh_gain_sum - xhat * jnp.mean(dh_gain_sum * xhat, axis=-1, keepdims=True))
    return dx, xhat


def _b_in_bwd(dq, dk, dv, dz, q_raw, k_raw, x, dx_mid, q_gain_t, k_gain_t, b_gain, kv_gain, w_in, w_kv):
    S = x.shape[0]
    tm = ROW_TILE

    def body(dq_ref, dk_ref, dv_ref, dz_ref, qr_ref, kr_ref, x_ref, dxm_ref, qg_ref, kg_ref, bg_ref, kvg_ref,
             win_ref, wkv_ref, dqz_ref, dkv_ref, dx_ref, small_ref):
        @pl.when(pl.program_id(0) == 0)
        def _():
            small_ref[...] = jnp.zeros_like(small_ref)

        bd = _head_mean_matrix()

        def head_norm_bwd(dy_ref, raw_ref, gain, scale):
            raw = raw_ref[...].astype(F32)
            rr = lax.rsqrt(_head_mean(raw * raw, bd) + EPS)
            xhat = raw * rr
            dy = dy_ref[...].astype(F32) * scale
            gdy = dy * gain
            draw = rr * (gdy - xhat * _head_mean(gdy * xhat, bd))
            return draw.astype(BF16), jnp.sum(dy * xhat, axis=0, keepdims=True)

        dqr, dqg = head_norm_bwd(dq_ref, qr_ref, qg_ref[...], SB_SCALE)
        dkr, dkg = head_norm_bwd(dk_ref, kr_ref, kg_ref[...], 1.0)
        dqz_ref[:, :D_MODEL] = dqr
        dqz_ref[:, D_MODEL:] = dz_ref[...]
        dkv_ref[:, :D_MODEL] = dkr
        dkv_ref[:, D_MODEL:] = dv_ref[...]
        dhb = jnp.zeros((tm, D_MODEL), F32)
        dhkv = jnp.zeros((tm, D_MODEL), F32)
        for sh in range(N_CHIPS):
            cols = slice(sh * 512, (sh + 1) * 512)
            dhb = dhb + _dot_nt(dqz_ref[:, cols], win_ref[sh])
            dhkv = dhkv + _dot_nt(dkv_ref[:, cols], wkv_ref[sh])
        dx, xhat = _rms_bwd(x_ref[...], dhb * bg_ref[...] + dhkv * kvg_ref[...])
        dx_ref[...] = dxm_ref[...] + dx
        small_ref[0:1, :] += dqg
        small_ref[1:2, :] += dkg
        small_ref[2:3, :] += jnp.sum(dhb * xhat, axis=0, keepdims=True)
        small_ref[3:4, :] += jnp.sum(dhkv * xhat, axis=0, keepdims=True)

    row = pl.BlockSpec((tm, D_MODEL), lambda i: (i, 0))
    wide = pl.BlockSpec((tm, 2 * D_MODEL), lambda i: (i, 0))
    vec = pl.BlockSpec((1, D_MODEL), lambda i: (0, 0))
    wsp = pl.BlockSpec((N_CHIPS, D_MODEL, 512), lambda i: (0, 0, 0))
    return pl.pallas_call(
        body, name="b_in_bwd", grid=(S // tm,),
        in_specs=[row] * 8 + [vec] * 4 + [wsp, wsp],
        out_specs=[wide, wide, row, pl.BlockSpec((8, D_MODEL), lambda i: (0, 0))],
        out_shape=[jax.ShapeDtypeStruct((S, 2 * D_MODEL), BF16), jax.ShapeDtypeStruct((S, 2 * D_MODEL), BF16),
                   jax.ShapeDtypeStruct((S, D_MODEL), F32), jax.ShapeDtypeStruct((8, D_MODEL), F32)],
        compiler_params=_params(("arbitrary",), 56),
    )(dq, dk, dv, dz, q_raw, k_raw, x, dx_mid, q_gain_t, k_gain_t, b_gain, kv_gain, w_in, w_kv)


def _a_mix_bwd(dgated, uz, pooled, wg, scale, w_in, x, dx_mid, gain, reduce=None):
    S = x.shape[0]
    tm = ROW_TILE
    n = S // tm

    def body(dg_ref, z_ref, p_ref, wg_ref, sc_ref, win_ref, x_ref, dxm_ref, gn_ref,
             duz_ref, dmr_ref, dx_ref, small_ref, halo_hi, halo_lo):
        i = pl.program_id(0)

        @pl.when(i == 0)
        def _():
            small_ref[...] = jnp.zeros_like(small_ref)
            halo_hi[...] = jnp.zeros_like(halo_hi)
            halo_lo[...] = jnp.zeros_like(halo_lo)

        first_row = (n - 1 - i) * tm
        row = lax.broadcasted_iota(jnp.int32, (tm, tm), 0)
        col = lax.broadcasted_iota(jnp.int32, (tm, tm), 1)
        d = col - row
        for g, w in enumerate(POOL_WINDOWS):
            cols = slice(g * GROUP_DIM, (g + 1) * GROUP_DIM)
            wgg = _group_weight(wg_ref, g)
            sc = sc_ref[:, cols]
            mraw = _dot(p_ref[:, cols], wgg)
            z = z_ref[:, cols]
            sg = _sigmoid(z)
            dga = dg_ref[:, cols].astype(F32)
            dm = dga * (z * sg)
            duz_ref[:, D_MODEL + g * GROUP_DIM:D_MODEL + (g + 1) * GROUP_DIM] = (
                dga * (mraw * sc) * _dsilu(z, sg)).astype(BF16)
            small_ref[0:1, cols] += jnp.sum(dm * mraw, axis=0, keepdims=True)
            dmr = (dm * sc).astype(BF16)
            dmr_ref[:, cols] = dmr
            dp = _dot_nt(dmr, wgg)
            hi, lo = _hilo(dp * _inv_count(first_row, tm, w))
            t_main = _mask_bf16((d >= 0) & (d < w))
            t_halo = _mask_bf16(d + tm < w)
            du = (_dot(t_main, hi) + _dot(t_main, lo) + _dot(t_halo, halo_hi[:, cols]) + _dot(t_halo, halo_lo[:, cols])
                  - dp)
            halo_hi[:, cols] = hi
            halo_lo[:, cols] = lo
            duz_ref[:, cols] = du.astype(BF16)
        dh = jnp.zeros((tm, D_MODEL), F32)
        for sh in range(N_CHIPS):
            dh = dh + _dot_nt(duz_ref[:, sh * 512:(sh + 1) * 512], win_ref[sh])
        dx, xhat = _rms_bwd(x_ref[...], dh * gn_ref[...])
        dx_ref[...] = dxm_ref[...] + dx
        small_ref[1:2, :] += jnp.sum(dh * xhat, axis=0, keepdims=True)

    rev = lambda i: (n - 1 - i, 0)
    row = pl.BlockSpec((tm, D_MODEL), rev)
    vec = pl.BlockSpec((1, D_MODEL), lambda i: (0, 0))
    return _call_with_gather(
        body, name="a_mix_bwd", grid=(n,),
        in_specs=[row,
                  pl.BlockSpec((tm, D_MODEL), lambda i: (n - 1 - i, 1)),
                  row,
                  pl.BlockSpec((N_CHIPS, N_GROUPS, 64, GROUP_DIM), lambda i: (0, 0, 0, 0)),
                  vec,
                  pl.BlockSpec((N_CHIPS, D_MODEL, 512), lambda i: (0, 0, 0)),
                  row, row, vec],
        out_specs=[pl.BlockSpec((tm, 2 * D_MODEL), rev), row, row,
                   pl.BlockSpec((8, D_MODEL), lambda i: (0, 0))],
        out_shape=[jax.ShapeDtypeStruct((S, 2 * D_MODEL), BF16), jax.ShapeDtypeStruct((S, D_MODEL), BF16),
                   jax.ShapeDtypeStruct((S, D_MODEL), F32), jax.ShapeDtypeStruct((8, D_MODEL), F32)],
        scratch_shapes=[pltpu.VMEM((tm, D_MODEL), BF16), pltpu.VMEM((tm, D_MODEL), BF16)],
        args=(dgated, uz, pooled, wg, scale, w_in, x, dx_mid, gain), reduce=reduce, vmem_mib=56)


def _wgrad(name, a, dy, n_shards, a_spec=None, k_dim=None):
    S, n_cols = dy.shape
    ts = WGRAD_SEQ_TILE
    k_dim = a.shape[-1] if k_dim is None else k_dim
    wn = n_cols // n_shards
    tk = min(k_dim, WGRAD_ACC_BYTES // (4 * n_cols))
    nst = S // ts

    def body(a_ref, dy_ref, out_ref, acc):
        st = pl.program_id(1)

        @pl.when(st == 0)
        def _():
            acc[...] = jnp.zeros_like(acc)

        acc[...] += _dot_tn(a_ref[...].astype(BF16), dy_ref[...].astype(BF16))

        @pl.when(st == nst - 1)
        def _():
            for sh in range(n_shards):
                out_ref[sh] = acc[:, sh * wn:(sh + 1) * wn]

    if a_spec is None:
        a_spec = pl.BlockSpec((ts, tk), lambda kt, st: (st, kt))
    return pl.pallas_call(
        body, name=name, grid=(k_dim // tk, nst),
        in_specs=[a_spec, pl.BlockSpec((ts, n_cols), lambda kt, st: (st, 0))],
        out_specs=pl.BlockSpec((n_shards, tk, wn), lambda kt, st: (0, kt, 0)),
        out_shape=jax.ShapeDtypeStruct((n_shards, k_dim, wn), F32),
        scratch_shapes=[pltpu.VMEM((tk, n_cols), F32)],
        compiler_params=_params(("parallel", "arbitrary")),
    )(a, dy)


def _wgrad_ple(name, p, layer, de):
    ts = WGRAD_SEQ_TILE
    spec = pl.BlockSpec((None, None, ts, PLE_DIM), lambda kt, st: (layer, 0, st, 0))
    return _wgrad(name, p, de, N_CHIPS, a_spec=spec, k_dim=PLE_DIM)


def _wgrad_group(pooled, dmr):
    S = pooled.shape[0]
    ts = WGRAD_SEQ_TILE
    nst = S // ts

    def body(p_ref, d_ref, out_ref, acc):
        st = pl.program_id(1)

        @pl.when(st == 0)
        def _():
            acc[...] = jnp.zeros_like(acc)

        acc[...] += _dot_tn(p_ref[...], d_ref[...])

        @pl.when(st == nst - 1)
        def _():
            for sh in range(N_CHIPS):
                out_ref[sh] = acc[sh * 64:(sh + 1) * 64, :]

    blk = pl.BlockSpec((ts, GROUP_DIM), lambda g, st: (st, g))
    return pl.pallas_call(
        body, name="wgrad_group", grid=(N_GROUPS, nst),
        in_specs=[blk, blk],
        out_specs=pl.BlockSpec((N_CHIPS, None, 64, GROUP_DIM), lambda g, st: (0, g, 0, 0)),
        out_shape=jax.ShapeDtypeStruct((N_CHIPS, N_GROUPS, 64, GROUP_DIM), F32),
        scratch_shapes=[pltpu.VMEM((GROUP_DIM, GROUP_DIM), F32)],
        compiler_params=_params(("parallel", "arbitrary")),
    )(pooled, dmr)


GATHER_AT = {
    "a_in": ("a_w_group", "a_w_out", "ple_w0", "ple_gate_w0"),
    "a_mix": ("w_kv",),
    "a_out_ple": ("b_w_in",),
    "attn_fwd": ("b_w_out", "ple_w1", "ple_gate_w1"),
}


REDUCE_AT = {
    "attn_bwd": ("b_w_out", "ple_w1", "ple_gate_w1"),
    "a_mix_bwd": ("a_w_out", "ple_w0", "ple_gate_w0"),
}


def _local_step(x, p, target, w, local=None, state=None):
    w = dict(w)

    def run(fn, host, n_out, *args, **kwargs):
        names = GATHER_AT[host] if local is not None else ()
        res = fn(*args, gather=[local[n] for n in names], **kwargs)
        w.update(zip(names, res[n_out:]))
        return res[:n_out]

    k_gain_t = jnp.tile(w["k_norm"].reshape(1, HEAD_DIM), (1, N_HEADS))
    q_gain_t = jnp.tile(w["b_q_norm"].reshape(1, HEAD_DIM), (1, N_HEADS))

    uz, h_a = run(_a_in, "a_in", 2, x, w["a_norm"], w["a_w_in"])
    wg4 = w["a_w_group"].reshape(N_CHIPS, N_GROUPS, 64, GROUP_DIM)
    wa_out = w["a_w_out"].reshape(D_MODEL, D_MODEL)
    gated_a, pooled = run(_a_mix, "a_mix", 2, uz, wg4, w["a_scale"])
    x1, x2, e_a, gate_a = run(_out_ple, "a_out_ple", 4, "a_out_ple", gated_a, x, wa_out, p, 0,
                              w["ple_w0"], w["ple_gate_w0"])
    h_kv, h_b, k_raw, q_raw, k, q, v, z_b = _b_in(
        x2, w["kv_norm"], w["b_norm"], k_gain_t, q_gain_t, w["w_kv"], w["b_w_in"])
    o, gated_b, ltot, att_steps = run(_attn_fwd, "attn_fwd", 4, q, k, v, z_b)
    wb_out = w["b_w_out"].reshape(D_MODEL, D_MODEL)
    x3, dx4, e_b, gate_b, loss_blk = _out_ple("b_out_ple", gated_b, x2, wb_out, p, 1, w["ple_w1"], w["ple_gate_w1"],
                                              target=target)

    grads, updates = {}, {}

    def hosted(fn, host, n_out, *args):
        if state is None:
            return fn(*args)
        names = REDUCE_AT[host]
        seeds = [updates.get(n[:-1] + "1") if n.startswith("ple") and n.endswith("0") else None for n in names]
        res = fn(*args, reduce=([grads.pop(n) for n in names], *[[t[n] for n in names] for t in state[:3]],
                                [state[3][n] for n in names], seeds))
        for i, n in enumerate(names):
            updates[n] = tuple(group[i] for group in res[n_out:])
        return res[:n_out]

    de_b, dgp_b, dx3, dgated_b = _ple_out_bwd("b_ple_out_bwd", dx4, e_b, gate_b, w["ple_gate_w1"], wb_out)
    grads["b_w_out"] = _wgrad("wgrad_b_out", gated_b, dx3, 1).reshape(N_CHIPS, 256, D_MODEL)
    grads["ple_w1"] = _wgrad_ple("wgrad_ple1", p, 1, de_b)
    grads["ple_gate_w1"] = _wgrad("wgrad_gate1", x3, dgp_b, 1).reshape(N_CHIPS, 256, D_MODEL)
    dq, dk, dv, dz_b = hosted(_attn_bwd, "attn_bwd", 4, q, k, v, ltot, att_steps, dgated_b, o, z_b)
    dqz, dkv, dx2, small_b = _b_in_bwd(dq, dk, dv, dz_b, q_raw, k_raw, x2, dx3, q_gain_t, k_gain_t,
                                       w["b_norm"], w["kv_norm"], w["b_w_in"], w["w_kv"])
    grads["w_kv"] = _wgrad("wgrad_kv", h_kv, dkv, N_CHIPS)
    grads["b_w_in"] = _wgrad("wgrad_b_in", h_b, dqz, N_CHIPS)
    de_a, dgp_a, dx1, dgated_a = _ple_out_bwd("a_ple_out_bwd", dx2, e_a, gate_a, w["ple_gate_w0"], wa_out)
    grads["a_w_out"] = _wgrad("wgrad_a_out", gated_a, dx1, 1).reshape(N_CHIPS, 256, D_MODEL)
    grads["ple_w0"] = _wgrad_ple("wgrad_ple0", p, 0, de_a)
    grads["ple_gate_w0"] = _wgrad("wgrad_gate0", x1, dgp_a, 1).reshape(N_CHIPS, 256, D_MODEL)
    duz, dmr, grad_x, small_a = hosted(_a_mix_bwd, "a_mix_bwd", 4, dgated_a, uz, pooled, wg4, w["a_scale"],
                                       w["a_w_in"], x, dx1, w["a_norm"])
    grads["a_w_in"] = _wgrad("wgrad_a_in", h_a, duz, N_CHIPS)
    grads["a_w_group"] = _wgrad_group(pooled, dmr).reshape(N_CHIPS, N_GROUPS * 64, GROUP_DIM)

    fold = lambda row: jnp.pad(row.reshape(N_HEADS, HEAD_DIM).sum(axis=0), (0, D_MODEL - HEAD_DIM))
    small = jnp.stack([small_a[1], small_a[0], small_b[3], small_b[2], fold(small_b[1]), fold(small_b[0]),
                       jnp.pad(loss_blk[0], (0, D_MODEL - loss_blk.shape[1])), jnp.zeros((D_MODEL,), F32)])
    return grad_x, grads, updates, small


def _mesh_place():
    x, y, c = lax.axis_index("x"), lax.axis_index("y"), lax.axis_index("c")
    other_chips = [(1 - x, y), (x, 1 - y), (1 - x, 1 - y)]
    return x, y, c, other_chips


def _gather_sems(n):
    return [pltpu.SemaphoreType.DMA((3 * n,)), pltpu.SemaphoreType.DMA((3 * n,)),
            pltpu.SemaphoreType.DMA((3 * n,)), pltpu.SemaphoreType.DMA((3 * n,)), pltpu.SemaphoreType.DMA((n,))]


def _gather_copies(srcs, outs, sems):
    send_far, recv_far, send_sib, recv_sib, local_sem = sems
    n = len(srcs)
    x, y, c, chips = _mesh_place()
    me = 2 * x + y
    sibling = (x, y, 1 - c)

    def half(k, which):
        rows = srcs[k].shape[0] // 2
        return pl.ds(pl.multiple_of(which * rows, 16), rows)

    local = [pltpu.make_async_copy(srcs[k], outs[k].at[me], local_sem.at[k]) for k in range(n)]
    far = [pltpu.make_async_remote_copy(
        src_ref=srcs[k].at[half(k, c)], dst_ref=outs[k].at[me, half(k, c)],
        send_sem=send_far.at[j * n + k], recv_sem=recv_far.at[j * n + k], device_id=(px, py, c), device_id_type=MESH)
        for j, (px, py) in enumerate(chips) for k in range(n)]

    def landed(j, k, which, from_far):
        px, py = chips[j]
        piece = outs[k].at[2 * px + py, half(k, which)]
        send, recv = (send_far, recv_far) if from_far else (send_sib, recv_sib)
        return pltpu.make_async_remote_copy(src_ref=piece, dst_ref=piece, send_sem=send.at[j * n + k],
                                            recv_sem=recv.at[j * n + k], device_id=sibling, device_id_type=MESH)

    return local, far, landed, c


def _gather_start(srcs, outs, sems):
    local, far, _, _ = _gather_copies(srcs, outs, sems)
    for cp in local + far:
        cp.start()


def _gather_pass_on(srcs, outs, sems):
    _, _, landed, c = _gather_copies(srcs, outs, sems)
    for j in range(3):
        for k in range(len(srcs)):
            landed(j, k, c, True).wait_recv()
            landed(j, k, c, False).start()


def _gather_finish(srcs, outs, sems):
    local, far, landed, c = _gather_copies(srcs, outs, sems)
    pairs = [(j, k) for j in range(3) for k in range(len(srcs))]
    for j, k in pairs:
        landed(j, k, 1 - c, False).wait_recv()
    for cp in far + [landed(j, k, c, False) for j, k in pairs]:
        cp.wait_send()
    for cp in local:
        cp.wait()


def _call_with_gather(body, *, name, grid, in_specs, out_specs, out_shape, args, gather=(), reduce=None,
                      scratch_shapes=(), vmem_mib=48):
    n_in, n_out, n_scr, n_g = len(args), len(out_shape), len(scratch_shapes), len(gather)
    n_r = len(reduce[0]) if reduce else 0
    pieces = _reduce_pieces(reduce[0], reduce[4]) if reduce else []
    reduce_args = [a for group in reduce[:4] for a in group] if reduce else []
    seeds = reduce[5] if reduce else []
    seeded = [(k, a) for k, seed in enumerate(seeds) if seed is not None for a in range(4)]
    gather_sems = _gather_sems(n_g) if n_g else []
    n_steps = 1
    for g in grid:
        n_steps *= g

    def wrapped(*refs):
        refs = list(refs)
        take = lambda count: [refs.pop(0) for _ in range(count)]
        ins, g_in, r_in = take(n_in), take(n_g), take(4 * n_r)
        take(len(seeded))
        outs, g_out, r_out = take(n_out), take(n_g), take(4 * n_r)
        scratch, sems, r_scratch = take(n_scr), take(len(gather_sems)), refs
        step = 0
        for axis, g in enumerate(grid):
            step = step * g + pl.program_id(axis)
        if n_g:
            @pl.when(step == 0)
            def _():
                _gather_start(g_in, g_out, sems)

        if n_r:
            ticks, drain = _reduce_ticks(pieces, n_r, (*r_in, *r_out, *r_scratch))
            for t, tick in enumerate(ticks[:n_steps]):
                pl.when(step == t)(tick)

        body(*ins, *outs, *scratch)
        if n_r:
            for tick in ticks[n_steps:]:
                pl.when(step == n_steps - 1)(tick)
            pl.when(step == n_steps - 1)(drain)
        if n_g:
            @pl.when(step == max(n_steps - 2, 0))
            def _():
                _gather_pass_on(g_in, g_out, sems)

            @pl.when(step == n_steps - 1)
            def _():
                _gather_finish(g_in, g_out, sems)

    hbm = pl.BlockSpec(memory_space=pltpu.HBM)
    res = pl.pallas_call(
        wrapped, name=name, grid=grid,
        in_specs=list(in_specs) + [hbm] * (n_g + 4 * n_r + len(seeded)),
        out_specs=list(out_specs) + [hbm] * (n_g + 4 * n_r),
        out_shape=list(out_shape) + [jax.ShapeDtypeStruct((N_CHIPS,) + g.shape, BF16) for g in gather]
        + ([jax.ShapeDtypeStruct(w.shape, F32) for _ in range(4) for w in reduce[1]] if reduce else []),
        input_output_aliases={n_in + n_g + 4 * n_r + i: n_out + n_g + a * n_r + k for i, (k, a) in enumerate(seeded)},
        scratch_shapes=list(scratch_shapes) + gather_sems + (_reduce_scratch() if reduce else []),
        compiler_params=_params(("arbitrary",) * len(grid), vmem_mib),
    )(*args, *gather, *reduce_args, *[seeds[k][a] for k, a in seeded])
    if not reduce:
        return res
    plain = list(res[:n_out + n_g])
    return plain + [res[n_out + n_g + i * n_r:n_out + n_g + (i + 1) * n_r] for i in range(4)]


def _allgather_weights(shards, small, casts):
    n = len(shards)
    cast_out = [(k, r0, r1) for k, (_, ranges) in enumerate(casts) for r0, r1 in ranges]
    n_c, n_co = len(casts), len(cast_out)

    def body(*refs):
        ins, small_in, cast_in = refs[:n], refs[n], refs[n + 1:n + 1 + n_c]
        refs = refs[n + 1 + n_c:]
        outs, small_out, cast_dst = refs[:n], refs[n], refs[n + 1:n + 1 + n_co]
        refs = refs[n + 1 + n_co:]
        cast, cast_buf = refs[:n], refs[n:n + n_co]
        send_far, recv_far, send_sib, recv_sib, send_small, recv_small, local_sem, cast_sem = refs[n + n_co:]
        x, y, c, chips = _mesh_place()
        me = 2 * x + y
        sibling = (x, y, 1 - c)

        def half(k, which):
            rows = ins[k].shape[0] // 2
            return pl.ds(pl.multiple_of(which * rows, 16), rows)

        local = []
        for k in range(n):
            cast[k][...] = ins[k][...].astype(BF16)
            local.append(pltpu.make_async_copy(cast[k], outs[k].at[me], local_sem.at[k]))
            local[-1].start()
        local.append(pltpu.make_async_copy(small_in, small_out.at[me], local_sem.at[n]))
        local[-1].start()

        sends = []
        for j, (px, py) in enumerate(chips):
            for k in range(n):
                cp = pltpu.make_async_remote_copy(
                    src_ref=cast[k].at[half(k, c)], dst_ref=outs[k].at[me, half(k, c)],
                    send_sem=send_far.at[j * n + k], recv_sem=recv_far.at[j * n + k],
                    device_id=(px, py, c), device_id_type=MESH)
                cp.start()
                sends.append(cp)
            cp = pltpu.make_async_remote_copy(
                src_ref=small_in, dst_ref=small_out.at[me], send_sem=send_small.at[j], recv_sem=recv_small.at[j],
                device_id=(px, py, c), device_id_type=MESH)
            cp.start()
            sends.append(cp)

        for i, (k, r0, r1) in enumerate(cast_out):
            cast_buf[i][...] = cast_in[k][r0:r1, :].astype(BF16)
            local.append(pltpu.make_async_copy(cast_buf[i], cast_dst[i], cast_sem.at[i]))
            local[-1].start()

        def landed(j, k, which, sems_s, sems_r, device):
            px, py = chips[j]
            piece = outs[k].at[2 * px + py, half(k, which)]
            return pltpu.make_async_remote_copy(
                src_ref=piece, dst_ref=piece, send_sem=sems_s.at[j * n + k], recv_sem=sems_r.at[j * n + k],
                device_id=device, device_id_type=MESH)

        for j in range(len(chips)):
            for k in range(n):
                landed(j, k, c, send_far, recv_far, sibling).wait_recv()
                cp = landed(j, k, c, send_sib, recv_sib, sibling)
                cp.start()
                sends.append(cp)
        for j, (px, py) in enumerate(chips):
            for k in range(n):
                landed(j, k, 1 - c, send_sib, recv_sib, sibling).wait_recv()
            pltpu.make_async_remote_copy(
                src_ref=small_in, dst_ref=small_out.at[2 * px + py], send_sem=send_small.at[j],
                recv_sem=recv_small.at[j], device_id=(px, py, c), device_id_type=MESH).wait_recv()
        for cp in sends:
            cp.wait_send()
        for cp in local:
            cp.wait()

    vmem = pl.BlockSpec(memory_space=pltpu.VMEM)
    hbm = pl.BlockSpec(memory_space=pltpu.HBM)
    cast_shapes = [(r1 - r0, casts[k][0].shape[1]) for k, r0, r1 in cast_out]
    res = pl.pallas_call(
        body, name="allgather_weights",
        in_specs=[vmem] * (n + 1 + n_c), out_specs=[hbm] * (n + 1 + n_co),
        out_shape=[jax.ShapeDtypeStruct((N_CHIPS,) + s.shape, BF16) for s in shards]
        + [jax.ShapeDtypeStruct((N_CHIPS,) + small.shape, F32)]
        + [jax.ShapeDtypeStruct(s, BF16) for s in cast_shapes],
        scratch_shapes=[pltpu.VMEM(s.shape, BF16) for s in shards] + [pltpu.VMEM(s, BF16) for s in cast_shapes]
        + [pltpu.SemaphoreType.DMA((3 * n,)), pltpu.SemaphoreType.DMA((3 * n,)),
           pltpu.SemaphoreType.DMA((3 * n,)), pltpu.SemaphoreType.DMA((3 * n,)),
           pltpu.SemaphoreType.DMA((3,)), pltpu.SemaphoreType.DMA((3,)),
           pltpu.SemaphoreType.DMA((n + 1,)), pltpu.SemaphoreType.DMA((n_co,))],
        compiler_params=_params(None, 40),
    )(*shards, small, *[a for a, _ in casts])
    return res[:n], res[n], res[n + 1:]


def _adamw(w, g, m, v):
    m = ADAM_B1 * m + (1.0 - ADAM_B1) * g
    v = ADAM_B2 * v + (1.0 - ADAM_B2) * (g * g)
    m_hat = m / (1.0 - ADAM_B1 ** ADAM_STEP)
    v_hat = v / (1.0 - ADAM_B2 ** ADAM_STEP)
    delta = -ADAM_LR * (m_hat / (jnp.sqrt(v_hat) + ADAM_EPS) + ADAM_WD * w)
    return delta, m, v


RS_PIECE_ROWS = 128
RS_PIECE_COLS = 512


def _reduce_adam_all(grads, ws, ms, vs, small, bases=None, seeds=None):
    n_w = len(grads)
    pieces = _reduce_pieces(grads, bases)
    lanes = [pieces[0::2], pieces[1::2]]
    n_lane = len(_reduce_scratch())
    n_small = len(_small_sum_scratch(small.shape))
    seeds = seeds or [None] * n_w
    seeded = [(k, a) for k, seed in enumerate(seeds) if seed is not None for a in range(4)]
    n_in = 4 * n_w + 1

    def body(*refs):
        refs = list(refs)
        del refs[n_in:n_in + len(seeded)]
        small_in = refs.pop(4 * n_w)
        small_out = refs.pop(8 * n_w)
        small_scratch = [refs.pop() for _ in range(n_small)][::-1]
        sends = _small_sum_start(small_in, *small_scratch)
        arrays, scratch = refs[:8 * n_w], refs[8 * n_w:]
        runs = [_reduce_ticks(lane, n_w, arrays + scratch[i * n_lane:(i + 1) * n_lane])
                for i, lane in enumerate(lanes) if lane]
        for t in range(max(len(ticks) for ticks, _ in runs)):
            for ticks, _ in runs:
                if t < len(ticks):
                    ticks[t]()
        for _, drain in runs:
            drain()
        _small_sum_finish(sends, small_scratch[0], small_out)

    hbm = pl.BlockSpec(memory_space=pltpu.HBM)
    vmem = pl.BlockSpec(memory_space=pltpu.VMEM)
    outs = pl.pallas_call(
        body, name="reduce_adam_all",
        in_specs=[hbm] * (4 * n_w) + [vmem] + [hbm] * len(seeded), out_specs=[hbm] * (4 * n_w) + [vmem],
        out_shape=[jax.ShapeDtypeStruct(w.shape, F32) for _ in range(4) for w in ws]
        + [jax.ShapeDtypeStruct(small.shape, F32)],
        input_output_aliases={n_in + i: a * n_w + k for i, (k, a) in enumerate(seeded)},
        scratch_shapes=_reduce_scratch() * len(lanes) + _small_sum_scratch(small.shape),
        compiler_params=_params(None, 48),
    )(*grads, *ws, *ms, *vs, small, *[seeds[k][a] for k, a in seeded])
    return [outs[i * n_w:(i + 1) * n_w] for i in range(4)], outs[4 * n_w]


def _reduce_pieces(grads, bases=None):
    pieces = []
    for k, g in enumerate(grads):
        hr, cols = g.shape[1] // 2, g.shape[2]
        pr, pc = min(hr, RS_PIECE_ROWS), min(cols, RS_PIECE_COLS)
        base = bases[k] if bases else 0
        pieces += [(k, ro, hr, co, pr, pc, base) for ro in range(0, hr, pr) for co in range(0, cols, pc)]
    return pieces


def _reduce_scratch():
    P, C = RS_PIECE_ROWS, RS_PIECE_COLS
    return [
        pltpu.VMEM((3, N_CHIPS, P, C), F32), pltpu.VMEM((3, N_CHIPS, P, C), F32),
        pltpu.VMEM((2, N_CHIPS, P, C), BF16), pltpu.VMEM((2, N_CHIPS, P, C), BF16),
        pltpu.VMEM((2, N_CHIPS, P, C), F32),
        pltpu.VMEM((2, 3, P, C), BF16), pltpu.VMEM((2, 3, P, C), BF16),
        pltpu.VMEM((2, 2, P, C), F32),
        pltpu.VMEM((2, 3, 2, P, C), F32), pltpu.VMEM((2, 4, 2, P, C), F32),
        pltpu.SemaphoreType.DMA((3, 2)), pltpu.SemaphoreType.DMA((2, 3, 2)),
        pltpu.SemaphoreType.DMA((2,)), pltpu.SemaphoreType.DMA((2,)),
        pltpu.SemaphoreType.DMA((2, 3)), pltpu.SemaphoreType.DMA((2, 3)),
        pltpu.SemaphoreType.DMA((2,)), pltpu.SemaphoreType.DMA((2,)),
        pltpu.SemaphoreType.DMA((2, 4, 2))]


def _reduce_ticks(pieces, n_w, refs):
    n = len(pieces)

    def build(*refs):
        g_in, w_in, m_in, v_in = (refs[i * n_w:(i + 1) * n_w] for i in range(4))
        g_out, d_out, m_out, v_out = (refs[(4 + i) * n_w:(5 + i) * n_w] for i in range(4))
        (gm, go, sb1, rb1, part, sb2, rb2, fin, wmv, outs,
         ld_sem, wmv_sem, s1_send, s1_recv, s2_send, s2_recv, s3_send, s3_recv, out_sem) = refs[8 * n_w:]
        x, y, c, chips = _mesh_place()
        me = 2 * x + y
        sibling = (x, y, 1 - c)

        def at_hbm(i, which, in_shard):
            _, ro, hr, co, pr, pc, base = pieces[i]
            half = c if which == 0 else 1 - c
            return pl.ds(pl.multiple_of((base if in_shard else 0) + half * hr + ro, 64), pr), pl.ds(co, pc)

        def win(i):
            return pl.ds(0, pieces[i][4]), pl.ds(0, pieces[i][5])

        every = slice(None)

        def loads(i):
            k, s = pieces[i][0], i % 3
            return [pltpu.make_async_copy(g_in[k].at[(every,) + at_hbm(i, h, False)], buf.at[(s, every) + win(i)],
                                          ld_sem.at[s, h])
                    for h, buf in enumerate((gm, go))]

        def wmv_loads(i):
            k, s = pieces[i][0], i % 2
            return [pltpu.make_async_copy(src[k].at[at_hbm(i, h, True)], wmv.at[(s, a, h) + win(i)], wmv_sem.at[s, a, h])
                    for a, src in enumerate((w_in, m_in, v_in)) for h in range(2)]

        def stores(i):
            k, s = pieces[i][0], i % 2
            return [pltpu.make_async_copy(outs.at[(s, a, h) + win(i)], dst[k].at[at_hbm(i, h, True)], out_sem.at[s, a, h])
                    for a, dst in enumerate((g_out, d_out, m_out, v_out)) for h in range(2)]

        def swap1(i):
            s = i % 2
            return pltpu.make_async_remote_copy(
                src_ref=sb1.at[(s, every) + win(i)], dst_ref=rb1.at[(s, every) + win(i)],
                send_sem=s1_send.at[s], recv_sem=s1_recv.at[s], device_id=sibling, device_id_type=MESH)

        def far2(i, j):
            s = i % 2
            px, py = chips[j]
            return pltpu.make_async_remote_copy(
                src_ref=sb2.at[(s, j) + win(i)], dst_ref=rb2.at[(s, j) + win(i)],
                send_sem=s2_send.at[s, j], recv_sem=s2_recv.at[s, j], device_id=(px, py, c), device_id_type=MESH)

        def swap3(i):
            s = i % 2
            return pltpu.make_async_remote_copy(
                src_ref=fin.at[(s, 0) + win(i)], dst_ref=fin.at[(s, 1) + win(i)],
                send_sem=s3_send.at[s], recv_sem=s3_recv.at[s], device_id=sibling, device_id_type=MESH)

        def stage0(i):
            for cp in loads(i):
                cp.start()

        def stage1(i):
            s, s3 = i % 2, i % 3
            for cp in loads(i):
                cp.wait()
            sb1[(s, every) + win(i)] = go[(s3, every) + win(i)].astype(BF16)
            swap1(i).start()

        def stage2(i):
            s, s3 = i % 2, i % 3
            swap1(i).wait()
            for j, (px, py) in enumerate(chips):
                far = 2 * px + py
                sb2[(s, j) + win(i)] = (gm[(s3, far) + win(i)] + rb1[(s, far) + win(i)].astype(F32)).astype(BF16)
                far2(i, j).start()
            part[(s, me) + win(i)] = gm[(s3, me) + win(i)] + rb1[(s, me) + win(i)].astype(F32)

        def stage3(i):
            s = i % 2
            total = part[(s, me) + win(i)]
            for j in range(3):
                far2(i, j).wait()
                total = total + rb2[(s, j) + win(i)].astype(F32)
            fin[(s, 0) + win(i)] = total
            swap3(i).start()
            for cp in wmv_loads(i):
                cp.start()

        def stage4(i):
            s = i % 2
            if i >= 2:
                for cp in stores(i - 2):
                    cp.wait()
            swap3(i).wait()
            for cp in wmv_loads(i):
                cp.wait()
            both = (every,) + win(i)
            g = fin[(s,) + both]
            delta, m_new, v_new = _adamw(wmv[(s, 0) + both], g, wmv[(s, 1) + both], wmv[(s, 2) + both])
            outs[(s, 0) + both] = g
            outs[(s, 1) + both] = delta
            outs[(s, 2) + both] = m_new
            outs[(s, 3) + both] = v_new
            for cp in stores(i):
                cp.start()

        stages = (stage0, stage1, stage2, stage3, stage4)

        def tick(t):
            for age in reversed(range(len(stages))):
                if 0 <= t - age < n:
                    stages[age](t - age)

        def drain():
            for i in range(max(0, n - 2), n):
                for cp in stores(i):
                    cp.wait()

        return [functools.partial(tick, t) for t in range(n + len(stages) - 1)], drain

    return build(*refs)


N_DEVICES = 8


def _small_sum_scratch(shape):
    return [pltpu.VMEM((N_DEVICES,) + shape, F32),
            pltpu.SemaphoreType.DMA((N_DEVICES - 1,)), pltpu.SemaphoreType.DMA((N_DEVICES - 1,))]


def _small_sum_start(part_ref, buf, send_sem, recv_sem):
    x, y, c, _ = _mesh_place()
    me = 4 * x + 2 * y + c
    buf[me] = part_ref[...]
    sends = []
    for k in range(1, N_DEVICES):
        peer = ((1 - x) if k & 4 else x, (1 - y) if k & 2 else y, (1 - c) if k & 1 else c)
        cp = pltpu.make_async_remote_copy(src_ref=part_ref, dst_ref=buf.at[me], send_sem=send_sem.at[k - 1],
                                          recv_sem=recv_sem.at[k - 1], device_id=peer, device_id_type=MESH)
        cp.start()
        sends.append(cp)
    return sends


def _small_sum_finish(sends, buf, out_ref):
    for cp in sends:
        cp.wait_recv()
    total = buf[0]
    for s in range(1, N_DEVICES):
        total = total + buf[s]
    out_ref[...] = total
    for cp in sends:
        cp.wait_send()


def _adam_small(w, g, m, v):
    def body(w_ref, g_ref, m_ref, v_ref, d_ref, mo_ref, vo_ref):
        delta, m_new, v_new = _adamw(w_ref[...], g_ref[...], m_ref[...], v_ref[...])
        d_ref[...] = delta
        mo_ref[...] = m_new
        vo_ref[...] = v_new

    vmem = pl.BlockSpec(memory_space=pltpu.VMEM)
    return pl.pallas_call(
        body, name="adam_small", in_specs=[vmem] * 4, out_specs=[vmem] * 3,
        out_shape=[jax.ShapeDtypeStruct(w.shape, F32)] * 3,
    )(w, g, m, v)


BIG = ("a_w_in", "a_w_group", "a_w_out", "w_kv", "b_w_in", "b_w_out", "ple_w", "ple_gate_w")
SMALL = ("a_norm", "a_scale", "kv_norm", "b_norm", "k_norm", "b_q_norm")
SMALL_SHARDED = ("a_norm", "a_scale")
WEIGHTS = ("a_norm", "a_w_in", "a_w_group", "a_scale", "a_w_out", "kv_norm", "w_kv", "k_norm", "b_norm", "b_w_in",
           "b_q_norm", "b_w_out", "ple_w", "ple_gate_w")


def _as_matrix(a):
    return a.reshape(-1, a.shape[-1])


def _pack_small(arrs):
    rows = [jnp.pad(a.reshape(-1), (0, D_MODEL - a.size)) for a in arrs]
    rows += [jnp.zeros((D_MODEL,), F32)] * (8 - len(rows))
    return jnp.stack(rows)


def kernel(x, p, a_norm, a_w_in, a_w_group, a_scale, a_w_out, kv_norm, w_kv, k_norm, b_norm, b_w_in, b_q_norm, b_w_out, ple_w, ple_gate_w, loss_target, m_a_norm, m_a_w_in, m_a_w_group, m_a_scale, m_a_w_out, m_kv_norm, m_w_kv, m_k_norm, m_b_norm, m_b_w_in, m_b_q_norm, m_b_w_out, m_ple_w, m_ple_gate_w, v_a_norm, v_a_w_in, v_a_w_group, v_a_scale, v_a_w_out, v_kv_norm, v_w_kv, v_k_norm, v_b_norm, v_b_w_in, v_b_q_norm, v_b_w_out, v_ple_w, v_ple_gate_w):
    wts = dict(a_norm=a_norm, a_w_in=a_w_in, a_w_group=a_w_group, a_scale=a_scale, a_w_out=a_w_out, kv_norm=kv_norm,
               w_kv=w_kv, k_norm=k_norm, b_norm=b_norm, b_w_in=b_w_in, b_q_norm=b_q_norm, b_w_out=b_w_out,
               ple_w=ple_w, ple_gate_w=ple_gate_w)
    mom = dict(a_norm=m_a_norm, a_w_in=m_a_w_in, a_w_group=m_a_w_group, a_scale=m_a_scale, a_w_out=m_a_w_out,
               kv_norm=m_kv_norm, w_kv=m_w_kv, k_norm=m_k_norm, b_norm=m_b_norm, b_w_in=m_b_w_in,
               b_q_norm=m_b_q_norm, b_w_out=m_b_w_out, ple_w=m_ple_w, ple_gate_w=m_ple_gate_w)
    var = dict(a_norm=v_a_norm, a_w_in=v_a_w_in, a_w_group=v_a_w_group, a_scale=v_a_scale, a_w_out=v_a_w_out,
               kv_norm=v_kv_norm, w_kv=v_w_kv, k_norm=v_k_norm, b_norm=v_b_norm, b_w_in=v_b_w_in,
               b_q_norm=v_b_q_norm, b_w_out=v_b_w_out, ple_w=v_ple_w, ple_gate_w=v_ple_gate_w)
    S = x.shape[1]
    chip = 2 * lax.axis_index("x") + lax.axis_index("y")

    sharded_small = jnp.concatenate([a_norm.reshape(1, 256), a_scale.reshape(1, 256), jnp.zeros((6, 256), F32)], axis=0)
    later = ("a_w_group", "a_w_out", "w_kv", "b_w_in", "b_w_out", "ple_w", "ple_gate_w")
    (a_w_in_full,), small_full, copies = _allgather_weights(
        [_as_matrix(a_w_in)], sharded_small,
        [(_as_matrix(wts[n]), [(0, 256), (256, 512)] if n.startswith("ple") else [(0, _as_matrix(wts[n]).shape[0])])
         for n in later])
    local = dict(zip(("a_w_group", "a_w_out", "w_kv", "b_w_in", "b_w_out", "ple_w0", "ple_w1", "ple_gate_w0",
                      "ple_gate_w1"), copies))
    full = dict(a_w_in=a_w_in_full,
                a_norm=small_full[:, 0, :].reshape(1, D_MODEL), a_scale=small_full[:, 1, :].reshape(1, D_MODEL),
                kv_norm=kv_norm.reshape(1, D_MODEL), b_norm=b_norm.reshape(1, D_MODEL), k_norm=k_norm, b_q_norm=b_q_norm)

    def shards(t):
        out = {}
        for n in BIG:
            for entry in ((n + "0", n + "1") if n.startswith("ple") else (n,)):
                out[entry] = _as_matrix(t[n])
        return out

    base = {n: 256 if n.startswith("ple") and n.endswith("1") else 0 for n in shards(wts)}
    state = (shards(wts), shards(mom), shards(var), base)
    grad_x, grads, updates, small_part = _local_step(x.reshape(S, D_MODEL), p, loss_target.reshape(S, D_MODEL),
                                                     full, local, state)

    names = sorted(grads)
    reduced, small_sum = _reduce_adam_all(
        [grads[n] for n in names], *[[t[n] for n in names] for t in state[:3]], small_part,
        bases=[base[n] for n in names], seeds=[updates.get(n[:-1] + "1") if n.startswith("ple") else None for n in names])
    for i, n in enumerate(names):
        updates[n] = tuple(group[i] for group in reduced)
    out_g, out_d, out_m, out_v = {}, {}, {}, {}
    for n in BIG:
        for i, out in enumerate((out_g, out_d, out_m, out_v)):
            out[n] = updates[n + "0" if n.startswith("ple") else n][i].reshape(wts[n].shape)

    loss = small_sum[len(SMALL), 0]
    small_rows = []
    for i, n in enumerate(SMALL):
        row = small_sum[i]
        if n in SMALL_SHARDED:
            row = lax.dynamic_slice(row, (chip * 256,), (256,))
        else:
            row = row[:wts[n].size]
        small_rows.append(row)
    g_small = _pack_small(small_rows)
    d_small, m_small, v_small = _adam_small(_pack_small([wts[n] for n in SMALL]), g_small,
                                            _pack_small([mom[n] for n in SMALL]), _pack_small([var[n] for n in SMALL]))
    for i, n in enumerate(SMALL):
        shape, size = wts[n].shape, wts[n].size
        out_g[n], out_d[n], out_m[n], out_v[n] = (t[i, :size].reshape(shape) for t in (g_small, d_small, m_small, v_small))

    return (loss, grad_x.reshape(1, S, D_MODEL), *[out_g[n] for n in WEIGHTS], *[out_d[n] for n in WEIGHTS],
            *[out_m[n] for n in WEIGHTS], *[out_v[n] for n in WEIGHTS])
```

```python
import functools

import jax
import jax.numpy as jnp
from jax import lax
from jax.experimental import pallas as pl
from jax.experimental.pallas import tpu as pltpu

F32 = jnp.float32
BF16 = jnp.bfloat16
MESH = pl.DeviceIdType.MESH

D_MODEL = 1024
N_HEADS = 16
HEAD_DIM = 64
PLE_DIM = 256
N_GROUPS = 4
GROUP_DIM = 256
POOL_WINDOWS = (2, 4, 8, 16)
N_CHIPS = 4
EPS = 1e-6
SB_SCALE = HEAD_DIM ** -0.5

ADAM_LR = 0.001
ADAM_B1 = 0.9
ADAM_B2 = 0.999
ADAM_EPS = 1e-08
ADAM_WD = 0.01
ADAM_STEP = 10

ROW_TILE = 256
WIDE_ROW_TILE = 512
EXP_UNDERFLOW = -104.0
ATT_Q_TILE = 512
ATT_K_TILE = 256
WGRAD_SEQ_TILE = 1024
WGRAD_ACC_BYTES = 4 * 1024 * 1024
MIB = 1024 * 1024


def _params(semantics=None, vmem_mib=48):
    return pltpu.CompilerParams(dimension_semantics=semantics, vmem_limit_bytes=vmem_mib * MIB)


def _dot(a, b):
    return jnp.dot(a, b, preferred_element_type=F32)


def _dot_nt(a, b):
    return lax.dot_general(a, b, (((1,), (1,)), ((), ())), preferred_element_type=F32)


def _dot_tn(a, b):
    return lax.dot_general(a, b, (((0,), (0,)), ((), ())), preferred_element_type=F32)


def _hilo(x):
    hi = x.astype(BF16)
    lo = (x - hi.astype(F32)).astype(BF16)
    return hi, lo


def _dot_hilo(x, w):
    hi, lo = _hilo(x)
    return _dot(hi, w) + _dot(lo, w)


def _sigmoid(z):
    return jax.nn.sigmoid(z)


def _dsilu(z, sg):
    return sg * (1.0 + z * (1.0 - sg))


def _mask_bf16(cond):
    return jnp.where(cond, 1.0, 0.0).astype(BF16)


def _head_mean_matrix():
    r = lax.broadcasted_iota(jnp.int32, (256, 256), 0) // HEAD_DIM
    c = lax.broadcasted_iota(jnp.int32, (256, 256), 1) // HEAD_DIM
    return _mask_bf16(r == c)


def _head_mean(x, bd):
    parts = []
    for s in range(x.shape[1] // 256):
        parts.append(_dot_hilo(x[:, s * 256:(s + 1) * 256], bd))
    out = parts[0] if len(parts) == 1 else jnp.concatenate(parts, axis=1)
    return out * (1.0 / HEAD_DIM)


def _a_in(x, gain, w_sh, gather=()):
    S = x.shape[0]
    tm = 512
    nsh, _, wn = w_sh.shape

    def body(x_ref, g_ref, w_ref, uz_ref, h_ref):
        @pl.when(pl.program_id(1) == 0)
        def _():
            xv = x_ref[...]
            r = lax.rsqrt(jnp.mean(xv * xv, axis=-1, keepdims=True) + EPS)
            h_ref[...] = (xv * r * g_ref[...]).astype(BF16)

        uz_ref[...] = _dot(h_ref[...], w_ref[0])

    return _call_with_gather(
        body, name="a_in", grid=(S // tm, nsh),
        in_specs=[pl.BlockSpec((tm, D_MODEL), lambda i, j: (i, 0)),
                  pl.BlockSpec((1, D_MODEL), lambda i, j: (0, 0)),
                  pl.BlockSpec((1, D_MODEL, wn), lambda i, j: (j, 0, 0))],
        out_specs=[pl.BlockSpec((tm, wn), lambda i, j: (i, j)),
                   pl.BlockSpec((tm, D_MODEL), lambda i, j: (i, 0))],
        out_shape=[jax.ShapeDtypeStruct((S, nsh * wn), F32),
                   jax.ShapeDtypeStruct((S, D_MODEL), BF16)],
        args=(x, gain, w_sh), gather=gather)


def _inv_count(first_row, rows, w):
    t1 = first_row + 1 + lax.broadcasted_iota(jnp.int32, (rows, 1), 0)
    return 1.0 / jnp.minimum(t1, w).astype(F32)


def _group_weight(wg_ref, g):
    return jnp.concatenate([wg_ref[sh, g] for sh in range(N_CHIPS)], axis=0)


def _a_mix(uz, wg, scale, gather=()):
    S = uz.shape[0]
    tm = ROW_TILE

    def body(u_ref, up_ref, z_ref, wg_ref, sc_ref, ga_ref, p_ref):
        i = pl.program_id(0)
        row = lax.broadcasted_iota(jnp.int32, (tm, tm), 0)
        col = lax.broadcasted_iota(jnp.int32, (tm, tm), 1)
        d = row - col
        for g, w in enumerate(POOL_WINDOWS):
            cols = slice(g * GROUP_DIM, (g + 1) * GROUP_DIM)
            t_main = _mask_bf16((d >= 0) & (d < w))
            t_halo = _mask_bf16(d + tm < w)
            u = u_ref[:, cols]
            up = jnp.where(i > 0, up_ref[:, cols], 0.0)
            hi, lo = _hilo(u)
            hip, lop = _hilo(up)
            wsum = _dot(t_main, hi) + _dot(t_main, lo) + _dot(t_halo, hip) + _dot(t_halo, lop)
            pooled = (wsum * _inv_count(i * tm, tm, w) - u).astype(BF16)
            p_ref[:, cols] = pooled
            mraw = _dot(pooled, _group_weight(wg_ref, g))
            z = z_ref[:, cols]
            ga_ref[:, cols] = (mraw * sc_ref[:, cols] * (z * _sigmoid(z))).astype(BF16)

    return _call_with_gather(
        body, name="a_mix", grid=(S // tm,),
        in_specs=[pl.BlockSpec((tm, D_MODEL), lambda i: (i, 0)),
                  pl.BlockSpec((tm, D_MODEL), lambda i: (jnp.maximum(i - 1, 0), 0)),
                  pl.BlockSpec((tm, D_MODEL), lambda i: (i, 1)),
                  pl.BlockSpec((N_CHIPS, N_GROUPS, 64, GROUP_DIM), lambda i: (0, 0, 0, 0)),
                  pl.BlockSpec((1, D_MODEL), lambda i: (0, 0))],
        out_specs=[pl.BlockSpec((tm, D_MODEL), lambda i: (i, 0)),
                   pl.BlockSpec((tm, D_MODEL), lambda i: (i, 0))],
        out_shape=[jax.ShapeDtypeStruct((S, D_MODEL), BF16),
                   jax.ShapeDtypeStruct((S, D_MODEL), BF16)],
        args=(uz, uz, uz, wg, scale), gather=gather)


def _out_ple(name, gated, x_in, w_out, p, layer, ple_w, ple_g, target=None, gather=()):
    S = x_in.shape[0]
    tm = WIDE_ROW_TILE
    with_loss = target is not None

    def body(*refs):
        if with_loss:
            g_ref, x_ref, wo_ref, p_ref, pw_ref, pg_ref, t_ref, xm_ref, dx_ref, e_ref, gt_ref, loss_ref = refs
        else:
            g_ref, x_ref, wo_ref, p_ref, pw_ref, pg_ref, xm_ref, xo_ref, e_ref, gt_ref = refs
        xm = x_ref[...] + _dot(g_ref[...], wo_ref[...])
        xm_ref[...] = xm
        pb = p_ref[...].astype(BF16)
        e = jnp.concatenate([_dot(pb, pw_ref[sh]) for sh in range(N_CHIPS)], axis=1)
        pg = jnp.concatenate([pg_ref[sh] for sh in range(N_CHIPS)], axis=0)
        gate = _sigmoid(_dot(xm.astype(BF16), pg))
        e_ref[...] = e.astype(BF16)
        gt_ref[...] = gate.astype(BF16)
        xo = xm + e * gate
        if with_loss:
            diff = xo - t_ref[...]
            dx_ref[...] = diff * (1.0 / D_MODEL)

            @pl.when(pl.program_id(0) == 0)
            def _():
                loss_ref[...] = jnp.zeros_like(loss_ref)

            loss_ref[...] += jnp.sum(diff * diff) * (0.5 / D_MODEL)
        else:
            xo_ref[...] = xo

    row = pl.BlockSpec((tm, D_MODEL), lambda i: (i, 0))
    in_specs = [row, row,
                pl.BlockSpec((D_MODEL, D_MODEL), lambda i: (0, 0)),
                pl.BlockSpec((None, None, tm, PLE_DIM), lambda i: (layer, 0, i, 0)),
                pl.BlockSpec((N_CHIPS, PLE_DIM, 256), lambda i: (0, 0, 0)),
                pl.BlockSpec((N_CHIPS, 256, D_MODEL), lambda i: (0, 0, 0))]
    args = [gated, x_in, w_out, p, ple_w, ple_g]
    out_specs = [row, row, row, row]
    out_shape = [jax.ShapeDtypeStruct((S, D_MODEL), F32), jax.ShapeDtypeStruct((S, D_MODEL), F32),
                 jax.ShapeDtypeStruct((S, D_MODEL), BF16), jax.ShapeDtypeStruct((S, D_MODEL), BF16)]
    if with_loss:
        in_specs.append(row)
        args.append(target)
        out_specs.append(pl.BlockSpec((8, 128), lambda i: (0, 0)))
        out_shape.append(jax.ShapeDtypeStruct((8, 128), F32))
    return _call_with_gather(body, name=name, grid=(S // tm,), in_specs=in_specs, out_specs=out_specs,
                             out_shape=out_shape, args=args, gather=gather)


def _b_in(x, kv_gain, b_gain, k_gain_t, q_gain_t, w_kv, w_in, gather=()):
    S = x.shape[0]
    tm = ROW_TILE

    def body(x_ref, kvg_ref, bg_ref, kg_ref, qg_ref, wkv_ref, win_ref,
             hkv_ref, hb_ref, kraw_ref, qraw_ref, k_ref, q_ref, v_ref, z_ref):
        xv = x_ref[...]
        y = xv * lax.rsqrt(jnp.mean(xv * xv, axis=-1, keepdims=True) + EPS)
        hkv = (y * kvg_ref[...]).astype(BF16)
        hb = (y * bg_ref[...]).astype(BF16)
        hkv_ref[...] = hkv
        hb_ref[...] = hb
        bd = _head_mean_matrix()

        def head_norm(raw, gain):
            rr = lax.rsqrt(_head_mean(raw * raw, bd) + EPS)
            return raw * rr * gain

        for sh in range(N_CHIPS):
            kvc = _dot(hkv, wkv_ref[sh])
            qzc = _dot(hb, win_ref[sh])
            cols = slice((sh % 2) * 512, (sh % 2) * 512 + 512)
            if sh < 2:
                kraw_ref[:, cols] = kvc.astype(BF16)
                qraw_ref[:, cols] = qzc.astype(BF16)
                k_ref[:, cols] = head_norm(kvc, kg_ref[:, cols]).astype(BF16)
                q_ref[:, cols] = (head_norm(qzc, qg_ref[:, cols]) * SB_SCALE).astype(BF16)
            else:
                v_ref[:, cols] = kvc.astype(BF16)
                z_ref[:, cols] = qzc.astype(BF16)

    row = pl.BlockSpec((tm, D_MODEL), lambda i: (i, 0))
    vec = pl.BlockSpec((1, D_MODEL), lambda i: (0, 0))
    wsp = pl.BlockSpec((N_CHIPS, D_MODEL, 512), lambda i: (0, 0, 0))
    return _call_with_gather(
        body, name="b_in", grid=(S // tm,),
        in_specs=[row, vec, vec, vec, vec, wsp, wsp],
        out_specs=[row] * 8,
        out_shape=[jax.ShapeDtypeStruct((S, D_MODEL), BF16)] * 8,
        args=(x, kv_gain, b_gain, k_gain_t, q_gain_t, w_kv, w_in), gather=gather, vmem_mib=56)


def _softplus_parts(z):
    e = jnp.exp(-jnp.abs(z))
    return -(jnp.maximum(z, 0.0) + jnp.log(1.0 + e)), e


def _add_rows(total, rows, update):
    lo, hi = rows
    parts = ([total[:lo]] if lo else []) + [total[lo:hi] + update] + ([total[hi:]] if hi < total.shape[0] else [])
    return parts[0] if len(parts) == 1 else jnp.concatenate(parts, axis=0)


def _attn_fwd(q, k, v, zgate, gather=()):
    S = q.shape[0]
    tq, tk = ATT_Q_TILE, ATT_K_TILE
    kpq = tq // tk
    assert kpq == 2

    def body(q_ref, k_ref, v_ref, z_ref, o_ref, g_ref, lt_ref, steps_ref):
        qi = pl.program_id(1)
        lane = lax.broadcasted_iota(jnp.int32, (1, 128), 1)
        ri = lax.broadcasted_iota(jnp.int32, (tk, tk), 0)
        ci = lax.broadcasted_iota(jnp.int32, (tk, tk), 1)
        later_mat = _mask_bf16(ri > ci)
        causal = ci < ri
        qv = q_ref[...]
        first = lane < HEAD_DIM
        q_heads = (jnp.where(first, qv, jnp.zeros_like(qv)), jnp.where(first, jnp.zeros_like(qv), qv))

        def step(blocks, carry):
            chains = [(b, h) for b in range(len(blocks)) for h in range(2)]
            rows = [r for _, r, _ in blocks]
            s0 = [pl.multiple_of(kj * tk, tk) for kj, _, _ in blocks]
            kb = [k_ref[pl.ds(s, tk), :] for s in s0]
            vb = [v_ref[pl.ds(s, tk), :] for s in s0]
            visible = [causal if masked else None for _, _, masked in blocks]
            z = {c: _dot_nt(q_heads[c[1]][rows[c[0]][0]:rows[c[0]][1]], kb[c[0]]) for c in chains}
            run = [carry[0], carry[2]]
            log_own, later, run_at = {}, {}, {}
            for c in chains:
                b, h = c
                lk = _softplus_parts(z[c])[0]
                if visible[b] is not None:
                    lk = jnp.where(visible[b], lk, 0.0)
                log_own[c] = z[c] + lk
                later[c] = _dot(lk.astype(BF16), later_mat)
                run_at[c] = run[h][rows[b][0]:rows[b][1]]
                run[h] = _add_rows(run[h], rows[b], jnp.sum(lk, axis=-1, keepdims=True))
            acc = [carry[1], carry[3]]
            for c in chains:
                b, h = c
                a = jnp.exp(log_own[c] + later[c] + run_at[c])
                if visible[b] is not None:
                    a = jnp.where(visible[b], a, 0.0)
                acc[h] = _add_rows(acc[h], rows[b], _dot(a.astype(BF16), vb[b]))
            return run[0], acc[0], run[1], acc[1]

        zero1, zero128 = jnp.zeros((tq, 1), F32), jnp.zeros((tq, 128), F32)
        carry = step([(qi * kpq + 1, (tk, tq), True), (qi * kpq, (tk, tq), False), (qi * kpq, (0, tk), True)],
                     (zero1, zero128, zero1, zero128))

        def low(run):
            return jnp.max(run)

        def pair_more(c):
            return (c[0] < qi) & (jnp.maximum(low(c[1][tk:]), low(c[3][tk:])) > EXP_UNDERFLOW)

        def pair_step(c):
            last = (qi - c[0]) * kpq - 1
            return (c[0] + 1, *step([(last, (0, tq), False), (last - 1, (0, tq), False)], c[1:]))

        pairs, *carry = lax.while_loop(pair_more, pair_step, (jnp.int32(0), *carry))
        left = (qi - pairs) * kpq

        def single_more(c):
            return (c[0] < left) & (jnp.maximum(low(c[1][:tk]), low(c[3][:tk])) > EXP_UNDERFLOW)

        def single_step(c):
            return (c[0] + 1, *step([(left - 1 - c[0], (0, tk), False)], c[1:]))

        singles, *carry = lax.while_loop(single_more, single_step, (jnp.int32(0), *carry))
        steps_ref[...] = jnp.concatenate([jnp.full((4, 128), pairs, F32), jnp.full((4, 128), singles, F32)], axis=0)
        o_tot = jnp.where(first, carry[1], carry[3])
        l_tot = jnp.where(first, carry[0], carry[2])
        o_ref[...] = o_tot.astype(BF16)
        lt_ref[...] = l_tot
        zz = z_ref[...].astype(F32)
        g_ref[...] = (o_tot * (zz * _sigmoid(zz))).astype(BF16)

    blk = pl.BlockSpec((tq, 128), lambda hp, qi: (qi, hp))
    seq = pl.BlockSpec((S, 128), lambda hp, qi: (0, hp))
    return _call_with_gather(
        body, name="attn_fwd", grid=(D_MODEL // 128, S // tq),
        in_specs=[blk, seq, seq, blk],
        out_specs=[blk, blk, blk, pl.BlockSpec((None, None, 8, 128), lambda hp, qi: (hp, qi, 0, 0))],
        out_shape=[jax.ShapeDtypeStruct((S, D_MODEL), BF16)] * 2 + [jax.ShapeDtypeStruct((S, D_MODEL), F32)]
        + [jax.ShapeDtypeStruct((D_MODEL // 128, S // tq, 8, 128), F32)],
        args=(q, k, v, zgate), gather=gather)


def _ple_out_bwd(name, dx_out, e, gate, ple_g, w_out):
    S = dx_out.shape[0]
    tm = WIDE_ROW_TILE

    def body(dx_ref, e_ref, gt_ref, pg_ref, wo_ref, de_ref, dgp_ref, dxm_ref, dg_ref):
        dxo = dx_ref[...]
        ev = e_ref[...].astype(F32)
        gv = gt_ref[...].astype(F32)
        de_ref[...] = (dxo * gv).astype(BF16)
        dgp = (dxo * ev * gv * (1.0 - gv)).astype(BF16)
        dgp_ref[...] = dgp
        pg = jnp.concatenate([pg_ref[sh] for sh in range(N_CHIPS)], axis=0)
        dxm = dxo + _dot_nt(dgp, pg)
        dxm_ref[...] = dxm
        dg_ref[...] = _dot_nt(dxm.astype(BF16), wo_ref[...]).astype(BF16)

    row = pl.BlockSpec((tm, D_MODEL), lambda i: (i, 0))
    return pl.pallas_call(
        body, name=name, grid=(S // tm,),
        in_specs=[row, row, row,
                  pl.BlockSpec((N_CHIPS, 256, D_MODEL), lambda i: (0, 0, 0)),
                  pl.BlockSpec((D_MODEL, D_MODEL), lambda i: (0, 0))],
        out_specs=[row, row, row, row],
        out_shape=[jax.ShapeDtypeStruct((S, D_MODEL), BF16), jax.ShapeDtypeStruct((S, D_MODEL), BF16),
                   jax.ShapeDtypeStruct((S, D_MODEL), F32), jax.ShapeDtypeStruct((S, D_MODEL), BF16)],
        compiler_params=_params(("arbitrary",)),
    )(dx_out, e, gate, ple_g, w_out)


def _attn_bwd(q, k, v, ltot, steps, dgated, o, zgate, reduce=None):
    S = q.shape[0]
    tq, tk = ATT_Q_TILE, ATT_K_TILE
    kpq = tq // tk
    nq = S // tq

    def body(q_ref, k_ref, v_ref, lt_ref, steps_ref, dg_ref, o_ref, z_ref, dq_ref, dk_ref, dv_ref, dz_ref,
             dk_acc, dv_acc):
        qi = pl.program_id(1)

        @pl.when(qi == 0)
        def _():
            dk_acc[...] = jnp.zeros_like(dk_acc)
            dv_acc[...] = jnp.zeros_like(dv_acc)

        lane = lax.broadcasted_iota(jnp.int32, (1, 128), 1)
        ri = lax.broadcasted_iota(jnp.int32, (tk, tk), 0)
        ci = lax.broadcasted_iota(jnp.int32, (tk, tk), 1)
        later_mat = _mask_bf16(ri > ci)
        before_mat = _mask_bf16(ri < ci)
        causal = ci < ri
        zz = z_ref[...].astype(F32)
        sg = _sigmoid(zz)
        dgv = dg_ref[...].astype(F32)
        dz_ref[...] = (dgv * o_ref[...].astype(F32) * _dsilu(zz, sg)).astype(BF16)
        dob = (dgv * (zz * sg)).astype(BF16)
        ltv = lt_ref[...]
        qv = q_ref[...]
        first = lane < HEAD_DIM
        masks = (first, jnp.logical_not(first))
        q_heads = [jnp.where(hm, qv, jnp.zeros_like(qv)) for hm in masks]
        do_heads = [jnp.where(hm, dob, jnp.zeros_like(dob)) for hm in masks]
        totals = [jnp.max(jnp.where(hm, ltv, -jnp.inf), axis=-1, keepdims=True) for hm in masks]

        def step(blocks, carry):
            chains = [(b, h) for b in range(len(blocks)) for h in range(2)]
            rows = [r for _, r, _ in blocks]
            cut = lambda t, b: t[rows[b][0]:rows[b][1]]
            s0 = [pl.multiple_of(kj * tk, tk) for kj, _, _ in blocks]
            kb = [k_ref[pl.ds(s, tk), :] for s in s0]
            vb = [v_ref[pl.ds(s, tk), :] for s in s0]
            visible = [causal if masked else None for _, _, masked in blocks]
            z = {c: _dot_nt(cut(q_heads[c[1]], c[0]), kb[c[0]]) for c in chains}
            da = {c: _dot_nt(cut(do_heads[c[1]], c[0]), vb[c[0]]) for c in chains}
            run = [carry[0], carry[3]]
            log_own, beta, later, base = {}, {}, {}, {}
            for c in chains:
                b, h = c
                lk = _softplus_parts(z[c])[0]
                if visible[b] is not None:
                    lk = jnp.where(visible[b], lk, 0.0)
                log_own[c] = z[c] + lk
                beta[c] = jnp.exp(log_own[c]).astype(BF16)
                later[c] = _dot(lk.astype(BF16), later_mat)
                run[h] = _add_rows(run[h], rows[b], jnp.sum(lk, axis=-1, keepdims=True))
                base[c] = cut(totals[h] - run[h], b)
            grun = [carry[1], carry[4]]
            a_bf, g_bf, gbefore, grun_at = {}, {}, {}, {}
            for c in chains:
                b, h = c
                a = jnp.exp(log_own[c] + later[c] + base[c])
                if visible[b] is not None:
                    a = jnp.where(visible[b], a, 0.0)
                a_bf[c] = a.astype(BF16)
                g = da[c] * a
                g_bf[c] = g.astype(BF16)
                gbefore[c] = _dot(g_bf[c], before_mat)
                grun_at[c] = cut(grun[h], b)
                grun[h] = _add_rows(grun[h], rows[b], jnp.sum(g, axis=-1, keepdims=True))
            dq = [carry[2], carry[5]]
            dk_blk = [jnp.zeros((tk, 128), F32) for _ in blocks]
            dv_blk = [jnp.zeros((tk, 128), F32) for _ in blocks]
            for c in chains:
                b, h = c
                g = g_bf[c].astype(F32)
                dz = g - beta[c].astype(F32) * (g + gbefore[c] + grun_at[c])
                if visible[b] is not None:
                    dz = jnp.where(visible[b], dz, 0.0)
                dzb = dz.astype(BF16)
                dq[h] = _add_rows(dq[h], rows[b], _dot(dzb, kb[b]))
                dk_blk[b] = dk_blk[b] + _dot_tn(dzb, cut(q_heads[h], b))
                dv_blk[b] = dv_blk[b] + _dot_tn(a_bf[c], cut(do_heads[h], b))
            for b in range(len(blocks)):
                dk_acc[pl.ds(s0[b], tk), :] += dk_blk[b]
                dv_acc[pl.ds(s0[b], tk), :] += dv_blk[b]
            return run[0], grun[0], dq[0], run[1], grun[1], dq[1]

        pairs = jnp.clip(jnp.max(steps_ref[0:4, :]).astype(jnp.int32), 0, qi)
        left = (qi - pairs) * kpq
        singles = jnp.clip(jnp.max(steps_ref[4:8, :]).astype(jnp.int32), 0, left)
        zero1, zero128 = jnp.zeros((tq, 1), F32), jnp.zeros((tq, 128), F32)
        carry = lax.fori_loop(left - singles, left, lambda kj, c: step([(kj, (0, tk), False)], c),
                              (zero1, zero1, zero128, zero1, zero1, zero128))
        carry = lax.fori_loop(qi - pairs, qi,
                              lambda n, c: step([(n * kpq, (0, tq), False), (n * kpq + 1, (0, tq), False)], c), carry)
        carry = step([(qi * kpq, (0, tk), True), (qi * kpq, (tk, tq), False), (qi * kpq + 1, (tk, tq), True)], carry)
        dq_ref[...] = jnp.where(first, carry[2], carry[5]).astype(BF16)

        @pl.when(qi == nq - 1)
        def _():
            dk_ref[...] = dk_acc[...].astype(BF16)
            dv_ref[...] = dv_acc[...].astype(BF16)

    blk = pl.BlockSpec((tq, 128), lambda hp, qi: (qi, hp))
    seq = pl.BlockSpec((S, 128), lambda hp, qi: (0, hp))
    return _call_with_gather(
        body, name="attn_bwd", grid=(D_MODEL // 128, nq),
        in_specs=[blk, seq, seq, blk, pl.BlockSpec((None, None, 8, 128), lambda hp, qi: (hp, qi, 0, 0)),
                  blk, blk, blk],
        out_specs=[blk, seq, seq, blk],
        out_shape=[jax.ShapeDtypeStruct((S, D_MODEL), BF16)] * 4,
        scratch_shapes=[pltpu.VMEM((S, 128), F32), pltpu.VMEM((S, 128), F32)],
        args=(q, k, v, ltot, steps, dgated, o, zgate), reduce=reduce, vmem_mib=56)


def _rms_bwd(xv, dh_gain_sum):
    r = lax.rsqrt(jnp.mean(xv * xv, axis=-1, keepdims=True) + EPS)
    xhat = xv * r
    dx = r * (dh_gain_sum - xhat * jnp.mean(dh_gain_sum * xhat, axis=-1, keepdims=True))
    return dx, xhat


def _b_in_bwd(dq, dk, dv, dz, q_raw, k_raw, x, dx_mid, q_gain_t, k_gain_t, b_gain, kv_gain, w_in, w_kv):
    S = x.shape[0]
    tm = ROW_TILE

    def body(dq_ref, dk_ref, dv_ref, dz_ref, qr_ref, kr_ref, x_ref, dxm_ref, qg_ref, kg_ref, bg_ref, kvg_ref,
             win_ref, wkv_ref, dqz_ref, dkv_ref, dx_ref, small_ref):
        @pl.when(pl.program_id(0) == 0)
        def _():
            small_ref[...] = jnp.zeros_like(small_ref)

        bd = _head_mean_matrix()

        def head_norm_bwd(dy_ref, raw_ref, gain, scale):
            raw = raw_ref[...].astype(F32)
            rr = lax.rsqrt(_head_mean(raw * raw, bd) + EPS)
            xhat = raw * rr
            dy = dy_ref[...].astype(F32) * scale
            gdy = dy * gain
            draw = rr * (gdy - xhat * _head_mean(gdy * xhat, bd))
            return draw.astype(BF16), jnp.sum(dy * xhat, axis=0, keepdims=True)

        dqr, dqg = head_norm_bwd(dq_ref, qr_ref, qg_ref[...], SB_SCALE)
        dkr, dkg = head_norm_bwd(dk_ref, kr_ref, kg_ref[...], 1.0)
        dqz_ref[:, :D_MODEL] = dqr
        dqz_ref[:, D_MODEL:] = dz_ref[...]
        dkv_ref[:, :D_MODEL] = dkr
        dkv_ref[:, D_MODEL:] = dv_ref[...]
        dhb = jnp.zeros((tm, D_MODEL), F32)
        dhkv = jnp.zeros((tm, D_MODEL), F32)
        for sh in range(N_CHIPS):
            cols = slice(sh * 512, (sh + 1) * 512)
            dhb = dhb + _dot_nt(dqz_ref[:, cols], win_ref[sh])
            dhkv = dhkv + _dot_nt(dkv_ref[:, cols], wkv_ref[sh])
        dx, xhat = _rms_bwd(x_ref[...], dhb * bg_ref[...] + dhkv * kvg_ref[...])
        dx_ref[...] = dxm_ref[...] + dx
        small_ref[0:1, :] += dqg
        small_ref[1:2, :] += dkg
        small_ref[2:3, :] += jnp.sum(dhb * xhat, axis=0, keepdims=True)
        small_ref[3:4, :] += jnp.sum(dhkv * xhat, axis=0, keepdims=True)

    row = pl.BlockSpec((tm, D_MODEL), lambda i: (i, 0))
    wide = pl.BlockSpec((tm, 2 * D_MODEL), lambda i: (i, 0))
    vec = pl.BlockSpec((1, D_MODEL), lambda i: (0, 0))
    wsp = pl.BlockSpec((N_CHIPS, D_MODEL, 512), lambda i: (0, 0, 0))
    return pl.pallas_call(
        body, name="b_in_bwd", grid=(S // tm,),
        in_specs=[row] * 8 + [vec] * 4 + [wsp, wsp],
        out_specs=[wide, wide, row, pl.BlockSpec((8, D_MODEL), lambda i: (0, 0))],
        out_shape=[jax.ShapeDtypeStruct((S, 2 * D_MODEL), BF16), jax.ShapeDtypeStruct((S, 2 * D_MODEL), BF16),
                   jax.ShapeDtypeStruct((S, D_MODEL), F32), jax.ShapeDtypeStruct((8, D_MODEL), F32)],
        compiler_params=_params(("arbitrary",), 56),
    )(dq, dk, dv, dz, q_raw, k_raw, x, dx_mid, q_gain_t, k_gain_t, b_gain, kv_gain, w_in, w_kv)


def _a_mix_bwd(dgated, uz, pooled, wg, scale, w_in, x, dx_mid, gain, reduce=None):
    S = x.shape[0]
    tm = ROW_TILE
    n = S // tm

    def body(dg_ref, z_ref, p_ref, wg_ref, sc_ref, win_ref, x_ref, dxm_ref, gn_ref,
             duz_ref, dmr_ref, dx_ref, small_ref, halo_hi, halo_lo):
        i = pl.program_id(0)

        @pl.when(i == 0)
        def _():
            small_ref[...] = jnp.zeros_like(small_ref)
            halo_hi[...] = jnp.zeros_like(halo_hi)
            halo_lo[...] = jnp.zeros_like(halo_lo)

        first_row = (n - 1 - i) * tm
        row = lax.broadcasted_iota(jnp.int32, (tm, tm), 0)
        col = lax.broadcasted_iota(jnp.int32, (tm, tm), 1)
        d = col - row
        for g, w in enumerate(POOL_WINDOWS):
            cols = slice(g * GROUP_DIM, (g + 1) * GROUP_DIM)
            wgg = _group_weight(wg_ref, g)
            sc = sc_ref[:, cols]
            mraw = _dot(p_ref[:, cols], wgg)
            z = z_ref[:, cols]
            sg = _sigmoid(z)
            dga = dg_ref[:, cols].astype(F32)
            dm = dga * (z * sg)
            duz_ref[:, D_MODEL + g * GROUP_DIM:D_MODEL + (g + 1) * GROUP_DIM] = (
                dga * (mraw * sc) * _dsilu(z, sg)).astype(BF16)
            small_ref[0:1, cols] += jnp.sum(dm * mraw, axis=0, keepdims=True)
            dmr = (dm * sc).astype(BF16)
            dmr_ref[:, cols] = dmr
            dp = _dot_nt(dmr, wgg)
            hi, lo = _hilo(dp * _inv_count(first_row, tm, w))
            t_main = _mask_bf16((d >= 0) & (d < w))
            t_halo = _mask_bf16(d + tm < w)
            du = (_dot(t_main, hi) + _dot(t_main, lo) + _dot(t_halo, halo_hi[:, cols]) + _dot(t_halo, halo_lo[:, cols])
                  - dp)
            halo_hi[:, cols] = hi
            halo_lo[:, cols] = lo
            duz_ref[:, cols] = du.astype(BF16)
        dh = jnp.zeros((tm, D_MODEL), F32)
        for sh in range(N_CHIPS):
            dh = dh + _dot_nt(duz_ref[:, sh * 512:(sh + 1) * 512], win_ref[sh])
        dx, xhat = _rms_bwd(x_ref[...], dh * gn_ref[...])
        dx_ref[...] = dxm_ref[...] + dx
        small_ref[1:2, :] += jnp.sum(dh * xhat, axis=0, keepdims=True)

    rev = lambda i: (n - 1 - i, 0)
    row = pl.BlockSpec((tm, D_MODEL), rev)
    vec = pl.BlockSpec((1, D_MODEL), lambda i: (0, 0))
    return _call_with_gather(
        body, name="a_mix_bwd", grid=(n,),
        in_specs=[row,
                  pl.BlockSpec((tm, D_MODEL), lambda i: (n - 1 - i, 1)),
                  row,
                  pl.BlockSpec((N_CHIPS, N_GROUPS, 64, GROUP_DIM), lambda i: (0, 0, 0, 0)),
                  vec,
                  pl.BlockSpec((N_CHIPS, D_MODEL, 512), lambda i: (0, 0, 0)),
                  row, row, vec],
        out_specs=[pl.BlockSpec((tm, 2 * D_MODEL), rev), row, row,
                   pl.BlockSpec((8, D_MODEL), lambda i: (0, 0))],
        out_shape=[jax.ShapeDtypeStruct((S, 2 * D_MODEL), BF16), jax.ShapeDtypeStruct((S, D_MODEL), BF16),
                   jax.ShapeDtypeStruct((S, D_MODEL), F32), jax.ShapeDtypeStruct((8, D_MODEL), F32)],
        scratch_shapes=[pltpu.VMEM((tm, D_MODEL), BF16), pltpu.VMEM((tm, D_MODEL), BF16)],
        args=(dgated, uz, pooled, wg, scale, w_in, x, dx_mid, gain), reduce=reduce, vmem_mib=56)


def _wgrad(name, a, dy, n_shards, a_spec=None, k_dim=None):
    S, n_cols = dy.shape
    ts = WGRAD_SEQ_TILE
    k_dim = a.shape[-1] if k_dim is None else k_dim
    wn = n_cols // n_shards
    tk = min(k_dim, WGRAD_ACC_BYTES // (4 * n_cols))
    nst = S // ts

    def body(a_ref, dy_ref, out_ref, acc):
        st = pl.program_id(1)

        @pl.when(st == 0)
        def _():
            acc[...] = jnp.zeros_like(acc)

        acc[...] += _dot_tn(a_ref[...].astype(BF16), dy_ref[...].astype(BF16))

        @pl.when(st == nst - 1)
        def _():
            for sh in range(n_shards):
                out_ref[sh] = acc[:, sh * wn:(sh + 1) * wn]

    if a_spec is None:
        a_spec = pl.BlockSpec((ts, tk), lambda kt, st: (st, kt))
    return pl.pallas_call(
        body, name=name, grid=(k_dim // tk, nst),
        in_specs=[a_spec, pl.BlockSpec((ts, n_cols), lambda kt, st: (st, 0))],
        out_specs=pl.BlockSpec((n_shards, tk, wn), lambda kt, st: (0, kt, 0)),
        out_shape=jax.ShapeDtypeStruct((n_shards, k_dim, wn), F32),
        scratch_shapes=[pltpu.VMEM((tk, n_cols), F32)],
        compiler_params=_params(("parallel", "arbitrary")),
    )(a, dy)


def _wgrad_ple(name, p, layer, de):
    ts = WGRAD_SEQ_TILE
    spec = pl.BlockSpec((None, None, ts, PLE_DIM), lambda kt, st: (layer, 0, st, 0))
    return _wgrad(name, p, de, N_CHIPS, a_spec=spec, k_dim=PLE_DIM)


def _wgrad_group(pooled, dmr):
    S = pooled.shape[0]
    ts = WGRAD_SEQ_TILE
    nst = S // ts

    def body(p_ref, d_ref, out_ref, acc):
        st = pl.program_id(1)

        @pl.when(st == 0)
        def _():
            acc[...] = jnp.zeros_like(acc)

        acc[...] += _dot_tn(p_ref[...], d_ref[...])

        @pl.when(st == nst - 1)
        def _():
            for sh in range(N_CHIPS):
                out_ref[sh] = acc[sh * 64:(sh + 1) * 64, :]

    blk = pl.BlockSpec((ts, GROUP_DIM), lambda g, st: (st, g))
    return pl.pallas_call(
        body, name="wgrad_group", grid=(N_GROUPS, nst),
        in_specs=[blk, blk],
        out_specs=pl.BlockSpec((N_CHIPS, None, 64, GROUP_DIM), lambda g, st: (0, g, 0, 0)),
        out_shape=jax.ShapeDtypeStruct((N_CHIPS, N_GROUPS, 64, GROUP_DIM), F32),
        scratch_shapes=[pltpu.VMEM((GROUP_DIM, GROUP_DIM), F32)],
        compiler_params=_params(("parallel", "arbitrary")),
    )(pooled, dmr)


GATHER_AT = {
    "a_in": ("a_w_group", "a_w_out", "ple_w0", "ple_gate_w0"),
    "a_mix": ("w_kv",),
    "a_out_ple": ("b_w_in",),
    "attn_fwd": ("b_w_out", "ple_w1", "ple_gate_w1"),
}


REDUCE_AT = {
    "attn_bwd": ("b_w_out", "ple_w1", "ple_gate_w1"),
    "a_mix_bwd": ("a_w_out", "ple_w0", "ple_gate_w0"),
}


def _local_step(x, p, target, w, local=None, state=None):
    w = dict(w)

    def run(fn, host, n_out, *args, **kwargs):
        names = GATHER_AT[host] if local is not None else ()
        res = fn(*args, gather=[local[n] for n in names], **kwargs)
        w.update(zip(names, res[n_out:]))
        return res[:n_out]

    k_gain_t = jnp.tile(w["k_norm"].reshape(1, HEAD_DIM), (1, N_HEADS))
    q_gain_t = jnp.tile(w["b_q_norm"].reshape(1, HEAD_DIM), (1, N_HEADS))

    uz, h_a = run(_a_in, "a_in", 2, x, w["a_norm"], w["a_w_in"])
    wg4 = w["a_w_group"].reshape(N_CHIPS, N_GROUPS, 64, GROUP_DIM)
    wa_out = w["a_w_out"].reshape(D_MODEL, D_MODEL)
    gated_a, pooled = run(_a_mix, "a_mix", 2, uz, wg4, w["a_scale"])
    x1, x2, e_a, gate_a = run(_out_ple, "a_out_ple", 4, "a_out_ple", gated_a, x, wa_out, p, 0,
                              w["ple_w0"], w["ple_gate_w0"])
    h_kv, h_b, k_raw, q_raw, k, q, v, z_b = _b_in(
        x2, w["kv_norm"], w["b_norm"], k_gain_t, q_gain_t, w["w_kv"], w["b_w_in"])
    o, gated_b, ltot, att_steps = run(_attn_fwd, "attn_fwd", 4, q, k, v, z_b)
    wb_out = w["b_w_out"].reshape(D_MODEL, D_MODEL)
    x3, dx4, e_b, gate_b, loss_blk = _out_ple("b_out_ple", gated_b, x2, wb_out, p, 1, w["ple_w1"], w["ple_gate_w1"],
                                              target=target)

    grads, updates = {}, {}

    def hosted(fn, host, n_out, *args):
        if state is None:
            return fn(*args)
        names = REDUCE_AT[host]
        seeds = [updates.get(n[:-1] + "1") if n.startswith("ple") and n.endswith("0") else None for n in names]
        res = fn(*args, reduce=([grads.pop(n) for n in names], *[[t[n] for n in names] for t in state[:3]],
                                [state[3][n] for n in names], seeds))
        for i, n in enumerate(names):
            updates[n] = tuple(group[i] for group in res[n_out:])
        return res[:n_out]

    de_b, dgp_b, dx3, dgated_b = _ple_out_bwd("b_ple_out_bwd", dx4, e_b, gate_b, w["ple_gate_w1"], wb_out)
    grads["b_w_out"] = _wgrad("wgrad_b_out", gated_b, dx3, 1).reshape(N_CHIPS, 256, D_MODEL)
    grads["ple_w1"] = _wgrad_ple("wgrad_ple1", p, 1, de_b)
    grads["ple_gate_w1"] = _wgrad("wgrad_gate1", x3, dgp_b, 1).reshape(N_CHIPS, 256, D_MODEL)
    dq, dk, dv, dz_b = hosted(_attn_bwd, "attn_bwd", 4, q, k, v, ltot, att_steps, dgated_b, o, z_b)
    dqz, dkv, dx2, small_b = _b_in_bwd(dq, dk, dv, dz_b, q_raw, k_raw, x2, dx3, q_gain_t, k_gain_t,
                                       w["b_norm"], w["kv_norm"], w["b_w_in"], w["w_kv"])
    grads["w_kv"] = _wgrad("wgrad_kv", h_kv, dkv, N_CHIPS)
    grads["b_w_in"] = _wgrad("wgrad_b_in", h_b, dqz, N_CHIPS)
    de_a, dgp_a, dx1, dgated_a = _ple_out_bwd("a_ple_out_bwd", dx2, e_a, gate_a, w["ple_gate_w0"], wa_out)
    grads["a_w_out"] = _wgrad("wgrad_a_out", gated_a, dx1, 1).reshape(N_CHIPS, 256, D_MODEL)
    grads["ple_w0"] = _wgrad_ple("wgrad_ple0", p, 0, de_a)
    grads["ple_gate_w0"] = _wgrad("wgrad_gate0", x1, dgp_a, 1).reshape(N_CHIPS, 256, D_MODEL)
    duz, dmr, grad_x, small_a = hosted(_a_mix_bwd, "a_mix_bwd", 4, dgated_a, uz, pooled, wg4, w["a_scale"],
                                       w["a_w_in"], x, dx1, w["a_norm"])
    grads["a_w_in"] = _wgrad("wgrad_a_in", h_a, duz, N_CHIPS)
    grads["a_w_group"] = _wgrad_group(pooled, dmr).reshape(N_CHIPS, N_GROUPS * 64, GROUP_DIM)

    fold = lambda row: jnp.pad(row.reshape(N_HEADS, HEAD_DIM).sum(axis=0), (0, D_MODEL - HEAD_DIM))
    small = jnp.stack([small_a[1], small_a[0], small_b[3], small_b[2], fold(small_b[1]), fold(small_b[0]),
                       jnp.pad(loss_blk[0], (0, D_MODEL - loss_blk.shape[1])), jnp.zeros((D_MODEL,), F32)])
    return grad_x, grads, updates, small


def _mesh_place():
    x, y, c = lax.axis_index("x"), lax.axis_index("y"), lax.axis_index("c")
    other_chips = [(1 - x, y), (x, 1 - y), (1 - x, 1 - y)]
    return x, y, c, other_chips


def _gather_sems(n):
    return [pltpu.SemaphoreType.DMA((3 * n,)), pltpu.SemaphoreType.DMA((3 * n,)),
            pltpu.SemaphoreType.DMA((3 * n,)), pltpu.SemaphoreType.DMA((3 * n,)), pltpu.SemaphoreType.DMA((n,))]


def _gather_copies(srcs, outs, sems):
    send_far, recv_far, send_sib, recv_sib, local_sem = sems
    n = len(srcs)
    x, y, c, chips = _mesh_place()
    me = 2 * x + y
    sibling = (x, y, 1 - c)

    def half(k, which):
        rows = srcs[k].shape[0] // 2
        return pl.ds(pl.multiple_of(which * rows, 16), rows)

    local = [pltpu.make_async_copy(srcs[k], outs[k].at[me], local_sem.at[k]) for k in range(n)]
    far = [pltpu.make_async_remote_copy(
        src_ref=srcs[k].at[half(k, c)], dst_ref=outs[k].at[me, half(k, c)],
        send_sem=send_far.at[j * n + k], recv_sem=recv_far.at[j * n + k], device_id=(px, py, c), device_id_type=MESH)
        for j, (px, py) in enumerate(chips) for k in range(n)]

    def landed(j, k, which, from_far):
        px, py = chips[j]
        piece = outs[k].at[2 * px + py, half(k, which)]
        send, recv = (send_far, recv_far) if from_far else (send_sib, recv_sib)
        return pltpu.make_async_remote_copy(src_ref=piece, dst_ref=piece, send_sem=send.at[j * n + k],
                                            recv_sem=recv.at[j * n + k], device_id=sibling, device_id_type=MESH)

    return local, far, landed, c


def _gather_start(srcs, outs, sems):
    local, far, _, _ = _gather_copies(srcs, outs, sems)
    for cp in local + far:
        cp.start()


def _gather_pass_on(srcs, outs, sems):
    _, _, landed, c = _gather_copies(srcs, outs, sems)
    for j in range(3):
        for k in range(len(srcs)):
            landed(j, k, c, True).wait_recv()
            landed(j, k, c, False).start()


def _gather_finish(srcs, outs, sems):
    local, far, landed, c = _gather_copies(srcs, outs, sems)
    pairs = [(j, k) for j in range(3) for k in range(len(srcs))]
    for j, k in pairs:
        landed(j, k, 1 - c, False).wait_recv()
    for cp in far + [landed(j, k, c, False) for j, k in pairs]:
        cp.wait_send()
    for cp in local:
        cp.wait()


def _call_with_gather(body, *, name, grid, in_specs, out_specs, out_shape, args, gather=(), reduce=None,
                      scratch_shapes=(), vmem_mib=48):
    n_in, n_out, n_scr, n_g = len(args), len(out_shape), len(scratch_shapes), len(gather)
    n_r = len(reduce[0]) if reduce else 0
    pieces = _reduce_pieces(reduce[0], reduce[4]) if reduce else []
    reduce_args = [a for group in reduce[:4] for a in group] if reduce else []
    seeds = reduce[5] if reduce else []
    seeded = [(k, a) for k, seed in enumerate(seeds) if seed is not None for a in range(4)]
    gather_sems = _gather_sems(n_g) if n_g else []
    n_steps = 1
    for g in grid:
        n_steps *= g

    def wrapped(*refs):
        refs = list(refs)
        take = lambda count: [refs.pop(0) for _ in range(count)]
        ins, g_in, r_in = take(n_in), take(n_g), take(4 * n_r)
        take(len(seeded))
        outs, g_out, r_out = take(n_out), take(n_g), take(4 * n_r)
        scratch, sems, r_scratch = take(n_scr), take(len(gather_sems)), refs
        step = 0
        for axis, g in enumerate(grid):
            step = step * g + pl.program_id(axis)
        if n_g:
            @pl.when(step == 0)
            def _():
                _gather_start(g_in, g_out, sems)

        if n_r:
            ticks, drain = _reduce_ticks(pieces, n_r, (*r_in, *r_out, *r_scratch))
            for t, tick in enumerate(ticks[:n_steps]):
                pl.when(step == t)(tick)

        body(*ins, *outs, *scratch)
        if n_r:
            for tick in ticks[n_steps:]:
                pl.when(step == n_steps - 1)(tick)
            pl.when(step == n_steps - 1)(drain)
        if n_g:
            @pl.when(step == max((n_steps - 1) // 2, 0))
            def _():
                _gather_pass_on(g_in, g_out, sems)

            @pl.when(step == n_steps - 1)
            def _():
                _gather_finish(g_in, g_out, sems)

    hbm = pl.BlockSpec(memory_space=pltpu.HBM)
    res = pl.pallas_call(
        wrapped, name=name, grid=grid,
        in_specs=list(in_specs) + [hbm] * (n_g + 4 * n_r + len(seeded)),
        out_specs=list(out_specs) + [hbm] * (n_g + 4 * n_r),
        out_shape=list(out_shape) + [jax.ShapeDtypeStruct((N_CHIPS,) + g.shape, BF16) for g in gather]
        + ([jax.ShapeDtypeStruct(w.shape, F32) for _ in range(4) for w in reduce[1]] if reduce else []),
        input_output_aliases={n_in + n_g + 4 * n_r + i: n_out + n_g + a * n_r + k for i, (k, a) in enumerate(seeded)},
        scratch_shapes=list(scratch_shapes) + gather_sems + (_reduce_scratch() if reduce else []),
        compiler_params=_params(("arbitrary",) * len(grid), vmem_mib),
    )(*args, *gather, *reduce_args, *[seeds[k][a] for k, a in seeded])
    if not reduce:
        return res
    plain = list(res[:n_out + n_g])
    return plain + [res[n_out + n_g + i * n_r:n_out + n_g + (i + 1) * n_r] for i in range(4)]


def _allgather_weights(shards, small, casts):
    n = len(shards)
    cast_out = [(k, r0, r1) for k, (_, ranges) in enumerate(casts) for r0, r1 in ranges]
    n_c, n_co = len(casts), len(cast_out)

    def body(*refs):
        ins, small_in, cast_in = refs[:n], refs[n], refs[n + 1:n + 1 + n_c]
        refs = refs[n + 1 + n_c:]
        outs, small_out, cast_dst = refs[:n], refs[n], refs[n + 1:n + 1 + n_co]
        refs = refs[n + 1 + n_co:]
        cast, cast_buf = refs[:n], refs[n:n + n_co]
        send_far, recv_far, send_sib, recv_sib, send_small, recv_small, local_sem, cast_sem = refs[n + n_co:]
        x, y, c, chips = _mesh_place()
        me = 2 * x + y
        sibling = (x, y, 1 - c)

        def half(k, which):
            rows = ins[k].shape[0] // 2
            return pl.ds(pl.multiple_of(which * rows, 16), rows)

        local = []
        for k in range(n):
            cast[k][...] = ins[k][...].astype(BF16)
            local.append(pltpu.make_async_copy(cast[k], outs[k].at[me], local_sem.at[k]))
            local[-1].start()
        local.append(pltpu.make_async_copy(small_in, small_out.at[me], local_sem.at[n]))
        local[-1].start()

        sends = []
        for j, (px, py) in enumerate(chips):
            for k in range(n):
                cp = pltpu.make_async_remote_copy(
                    src_ref=cast[k].at[half(k, c)], dst_ref=outs[k].at[me, half(k, c)],
                    send_sem=send_far.at[j * n + k], recv_sem=recv_far.at[j * n + k],
                    device_id=(px, py, c), device_id_type=MESH)
                cp.start()
                sends.append(cp)
            cp = pltpu.make_async_remote_copy(
                src_ref=small_in, dst_ref=small_out.at[me], send_sem=send_small.at[j], recv_sem=recv_small.at[j],
                device_id=(px, py, c), device_id_type=MESH)
            cp.start()
            sends.append(cp)

        for i, (k, r0, r1) in enumerate(cast_out):
            cast_buf[i][...] = cast_in[k][r0:r1, :].astype(BF16)
            local.append(pltpu.make_async_copy(cast_buf[i], cast_dst[i], cast_sem.at[i]))
            local[-1].start()

        def landed(j, k, which, sems_s, sems_r, device):
            px, py = chips[j]
            piece = outs[k].at[2 * px + py, half(k, which)]
            return pltpu.make_async_remote_copy(
                src_ref=piece, dst_ref=piece, send_sem=sems_s.at[j * n + k], recv_sem=sems_r.at[j * n + k],
                device_id=device, device_id_type=MESH)

        for j in range(len(chips)):
            for k in range(n):
                landed(j, k, c, send_far, recv_far, sibling).wait_recv()
                cp = landed(j, k, c, send_sib, recv_sib, sibling)
                cp.start()
                sends.append(cp)
        for j, (px, py) in enumerate(chips):
            for k in range(n):
                landed(j, k, 1 - c, send_sib, recv_sib, sibling).wait_recv()
            pltpu.make_async_remote_copy(
                src_ref=small_in, dst_ref=small_out.at[2 * px + py], send_sem=send_small.at[j],
                recv_sem=recv_small.at[j], device_id=(px, py, c), device_id_type=MESH).wait_recv()
        for cp in sends:
            cp.wait_send()
        for cp in local:
            cp.wait()

    vmem = pl.BlockSpec(memory_space=pltpu.VMEM)
    hbm = pl.BlockSpec(memory_space=pltpu.HBM)
    cast_shapes = [(r1 - r0, casts[k][0].shape[1]) for k, r0, r1 in cast_out]
    res = pl.pallas_call(
        body, name="allgather_weights",
        in_specs=[vmem] * (n + 1 + n_c), out_specs=[hbm] * (n + 1 + n_co),
        out_shape=[jax.ShapeDtypeStruct((N_CHIPS,) + s.shape, BF16) for s in shards]
        + [jax.ShapeDtypeStruct((N_CHIPS,) + small.shape, F32)]
        + [jax.ShapeDtypeStruct(s, BF16) for s in cast_shapes],
        scratch_shapes=[pltpu.VMEM(s.shape, BF16) for s in shards] + [pltpu.VMEM(s, BF16) for s in cast_shapes]
        + [pltpu.SemaphoreType.DMA((3 * n,)), pltpu.SemaphoreType.DMA((3 * n,)),
           pltpu.SemaphoreType.DMA((3 * n,)), pltpu.SemaphoreType.DMA((3 * n,)),
           pltpu.SemaphoreType.DMA((3,)), pltpu.SemaphoreType.DMA((3,)),
           pltpu.SemaphoreType.DMA((n + 1,)), pltpu.SemaphoreType.DMA((n_co,))],
        compiler_params=_params(None, 40),
    )(*shards, small, *[a for a, _ in casts])
    return res[:n], res[n], res[n + 1:]


def _adamw(w, g, m, v):
    m = ADAM_B1 * m + (1.0 - ADAM_B1) * g
    v = ADAM_B2 * v + (1.0 - ADAM_B2) * (g * g)
    m_hat = m / (1.0 - ADAM_B1 ** ADAM_STEP)
    v_hat = v / (1.0 - ADAM_B2 ** ADAM_STEP)
    delta = -ADAM_LR * (m_hat / (jnp.sqrt(v_hat) + ADAM_EPS) + ADAM_WD * w)
    return delta, m, v


RS_PIECE_ROWS = 128
RS_PIECE_COLS = 512


def _reduce_adam_all(grads, ws, ms, vs, small, bases=None, seeds=None):
    n_w = len(grads)
    pieces = _reduce_pieces(grads, bases)
    lanes = [pieces[0::2], pieces[1::2]]
    n_lane = len(_reduce_scratch())
    n_small = len(_small_sum_scratch(small.shape))
    seeds = seeds or [None] * n_w
    seeded = [(k, a) for k, seed in enumerate(seeds) if seed is not None for a in range(4)]
    n_in = 4 * n_w + 1

    def body(*refs):
        refs = list(refs)
        del refs[n_in:n_in + len(seeded)]
        small_in = refs.pop(4 * n_w)
        small_out = refs.pop(8 * n_w)
        small_scratch = [refs.pop() for _ in range(n_small)][::-1]
        sends = _small_sum_start(small_in, *small_scratch)
        arrays, scratch = refs[:8 * n_w], refs[8 * n_w:]
        runs = [_reduce_ticks(lane, n_w, arrays + scratch[i * n_lane:(i + 1) * n_lane])
                for i, lane in enumerate(lanes) if lane]
        for t in range(max(len(ticks) for ticks, _ in runs)):
            for ticks, _ in runs:
                if t < len(ticks):
                    ticks[t]()
        for _, drain in runs:
            drain()
        _small_sum_finish(sends, small_scratch[0], small_out)

    hbm = pl.BlockSpec(memory_space=pltpu.HBM)
    vmem = pl.BlockSpec(memory_space=pltpu.VMEM)
    outs = pl.pallas_call(
        body, name="reduce_adam_all",
        in_specs=[hbm] * (4 * n_w) + [vmem] + [hbm] * len(seeded), out_specs=[hbm] * (4 * n_w) + [vmem],
        out_shape=[jax.ShapeDtypeStruct(w.shape, F32) for _ in range(4) for w in ws]
        + [jax.ShapeDtypeStruct(small.shape, F32)],
        input_output_aliases={n_in + i: a * n_w + k for i, (k, a) in enumerate(seeded)},
        scratch_shapes=_reduce_scratch() * len(lanes) + _small_sum_scratch(small.shape),
        compiler_params=_params(None, 48),
    )(*grads, *ws, *ms, *vs, small, *[seeds[k][a] for k, a in seeded])
    return [outs[i * n_w:(i + 1) * n_w] for i in range(4)], outs[4 * n_w]


def _reduce_pieces(grads, bases=None):
    pieces = []
    for k, g in enumerate(grads):
        hr, cols = g.shape[1] // 2, g.shape[2]
        pr, pc = min(hr, RS_PIECE_ROWS), min(cols, RS_PIECE_COLS)
        base = bases[k] if bases else 0
        pieces += [(k, ro, hr, co, pr, pc, base) for ro in range(0, hr, pr) for co in range(0, cols, pc)]
    return pieces


def _reduce_scratch():
    P, C = RS_PIECE_ROWS, RS_PIECE_COLS
    return [
        pltpu.VMEM((3, N_CHIPS, P, C), F32), pltpu.VMEM((3, N_CHIPS, P, C), F32),
        pltpu.VMEM((2, N_CHIPS, P, C), BF16), pltpu.VMEM((2, N_CHIPS, P, C), BF16),
        pltpu.VMEM((2, N_CHIPS, P, C), F32),
        pltpu.VMEM((2, 3, P, C), BF16), pltpu.VMEM((2, 3, P, C), BF16),
        pltpu.VMEM((2, 2, P, C), F32),
        pltpu.VMEM((2, 3, 2, P, C), F32), pltpu.VMEM((2, 4, 2, P, C), F32),
        pltpu.SemaphoreType.DMA((3, 2)), pltpu.SemaphoreType.DMA((2, 3, 2)),
        pltpu.SemaphoreType.DMA((2,)), pltpu.SemaphoreType.DMA((2,)),
        pltpu.SemaphoreType.DMA((2, 3)), pltpu.SemaphoreType.DMA((2, 3)),
        pltpu.SemaphoreType.DMA((2,)), pltpu.SemaphoreType.DMA((2,)),
        pltpu.SemaphoreType.DMA((2, 4, 2))]


def _reduce_ticks(pieces, n_w, refs):
    n = len(pieces)

    def build(*refs):
        g_in, w_in, m_in, v_in = (refs[i * n_w:(i + 1) * n_w] for i in range(4))
        g_out, d_out, m_out, v_out = (refs[(4 + i) * n_w:(5 + i) * n_w] for i in range(4))
        (gm, go, sb1, rb1, part, sb2, rb2, fin, wmv, outs,
         ld_sem, wmv_sem, s1_send, s1_recv, s2_send, s2_recv, s3_send, s3_recv, out_sem) = refs[8 * n_w:]
        x, y, c, chips = _mesh_place()
        me = 2 * x + y
        sibling = (x, y, 1 - c)

        def at_hbm(i, which, in_shard):
            _, ro, hr, co, pr, pc, base = pieces[i]
            half = c if which == 0 else 1 - c
            return pl.ds(pl.multiple_of((base if in_shard else 0) + half * hr + ro, 64), pr), pl.ds(co, pc)

        def win(i):
            return pl.ds(0, pieces[i][4]), pl.ds(0, pieces[i][5])

        every = slice(None)

        def loads(i):
            k, s = pieces[i][0], i % 3
            return [pltpu.make_async_copy(g_in[k].at[(every,) + at_hbm(i, h, False)], buf.at[(s, every) + win(i)],
                                          ld_sem.at[s, h])
                    for h, buf in enumerate((gm, go))]

        def wmv_loads(i):
            k, s = pieces[i][0], i % 2
            return [pltpu.make_async_copy(src[k].at[at_hbm(i, h, True)], wmv.at[(s, a, h) + win(i)], wmv_sem.at[s, a, h])
                    for a, src in enumerate((w_in, m_in, v_in)) for h in range(2)]

        def stores(i):
            k, s = pieces[i][0], i % 2
            return [pltpu.make_async_copy(outs.at[(s, a, h) + win(i)], dst[k].at[at_hbm(i, h, True)], out_sem.at[s, a, h])
                    for a, dst in enumerate((g_out, d_out, m_out, v_out)) for h in range(2)]

        def swap1(i):
            s = i % 2
            return pltpu.make_async_remote_copy(
                src_ref=sb1.at[(s, every) + win(i)], dst_ref=rb1.at[(s, every) + win(i)],
                send_sem=s1_send.at[s], recv_sem=s1_recv.at[s], device_id=sibling, device_id_type=MESH)

        def far2(i, j):
            s = i % 2
            px, py = chips[j]
            return pltpu.make_async_remote_copy(
                src_ref=sb2.at[(s, j) + win(i)], dst_ref=rb2.at[(s, j) + win(i)],
                send_sem=s2_send.at[s, j], recv_sem=s2_recv.at[s, j], device_id=(px, py, c), device_id_type=MESH)

        def swap3(i):
            s = i % 2
            return pltpu.make_async_remote_copy(
                src_ref=fin.at[(s, 0) + win(i)], dst_ref=fin.at[(s, 1) + win(i)],
                send_sem=s3_send.at[s], recv_sem=s3_recv.at[s], device_id=sibling, device_id_type=MESH)

        def stage0(i):
            for cp in loads(i):
                cp.start()

        def stage1(i):
            s, s3 = i % 2, i % 3
            for cp in loads(i):
                cp.wait()
            sb1[(s, every) + win(i)] = go[(s3, every) + win(i)].astype(BF16)
            swap1(i).start()

        def stage2(i):
            s, s3 = i % 2, i % 3
            swap1(i).wait()
            part[(s, every) + win(i)] = gm[(s3, every) + win(i)] + rb1[(s, every) + win(i)].astype(F32)
            for j, (px, py) in enumerate(chips):
                sb2[(s, j) + win(i)] = part[(s, 2 * px + py) + win(i)].astype(BF16)
                far2(i, j).start()

        def stage3(i):
            s = i % 2
            total = part[(s, me) + win(i)]
            for j in range(3):
                far2(i, j).wait()
                total = total + rb2[(s, j) + win(i)].astype(F32)
            fin[(s, 0) + win(i)] = total
            swap3(i).start()
            for cp in wmv_loads(i):
                cp.start()

        def stage4(i):
            s = i % 2
            if i >= 2:
                for cp in stores(i - 2):
                    cp.wait()
            swap3(i).wait()
            for cp in wmv_loads(i):
                cp.wait()
            both = (every,) + win(i)
            g = fin[(s,) + both]
            delta, m_new, v_new = _adamw(wmv[(s, 0) + both], g, wmv[(s, 1) + both], wmv[(s, 2) + both])
            outs[(s, 0) + both] = g
            outs[(s, 1) + both] = delta
            outs[(s, 2) + both] = m_new
            outs[(s, 3) + both] = v_new
            for cp in stores(i):
                cp.start()

        stages = (stage0, stage1, stage2, stage3, stage4)

        def tick(t):
            for age in reversed(range(len(stages))):
                if 0 <= t - age < n:
                    stages[age](t - age)

        def drain():
            for i in range(max(0, n - 2), n):
                for cp in stores(i):
                    cp.wait()

        return [functools.partial(tick, t) for t in range(n + len(stages) - 1)], drain

    return build(*refs)


N_DEVICES = 8


def _small_sum_scratch(shape):
    return [pltpu.VMEM((N_DEVICES,) + shape, F32),
            pltpu.SemaphoreType.DMA((N_DEVICES - 1,)), pltpu.SemaphoreType.DMA((N_DEVICES - 1,))]


def _small_sum_start(part_ref, buf, send_sem, recv_sem):
    x, y, c, _ = _mesh_place()
    me = 4 * x + 2 * y + c
    buf[me] = part_ref[...]
    sends = []
    for k in range(1, N_DEVICES):
        peer = ((1 - x) if k & 4 else x, (1 - y) if k & 2 else y, (1 - c) if k & 1 else c)
        cp = pltpu.make_async_remote_copy(src_ref=part_ref, dst_ref=buf.at[me], send_sem=send_sem.at[k - 1],
                                          recv_sem=recv_sem.at[k - 1], device_id=peer, device_id_type=MESH)
        cp.start()
        sends.append(cp)
    return sends


def _small_sum_finish(sends, buf, out_ref):
    for cp in sends:
        cp.wait_recv()
    total = buf[0]
    for s in range(1, N_DEVICES):
        total = total + buf[s]
    out_ref[...] = total
    for cp in sends:
        cp.wait_send()


def _adam_small(w, g, m, v):
    def body(w_ref, g_ref, m_ref, v_ref, d_ref, mo_ref, vo_ref):
        delta, m_new, v_new = _adamw(w_ref[...], g_ref[...], m_ref[...], v_ref[...])
        d_ref[...] = delta
        mo_ref[...] = m_new
        vo_ref[...] = v_new

    vmem = pl.BlockSpec(memory_space=pltpu.VMEM)
    return pl.pallas_call(
        body, name="adam_small", in_specs=[vmem] * 4, out_specs=[vmem] * 3,
        out_shape=[jax.ShapeDtypeStruct(w.shape, F32)] * 3,
    )(w, g, m, v)


BIG = ("a_w_in", "a_w_group", "a_w_out", "w_kv", "b_w_in", "b_w_out", "ple_w", "ple_gate_w")
SMALL = ("a_norm", "a_scale", "kv_norm", "b_norm", "k_norm", "b_q_norm")
SMALL_SHARDED = ("a_norm", "a_scale")
WEIGHTS = ("a_norm", "a_w_in", "a_w_group", "a_scale", "a_w_out", "kv_norm", "w_kv", "k_norm", "b_norm", "b_w_in",
           "b_q_norm", "b_w_out", "ple_w", "ple_gate_w")


def _as_matrix(a):
    return a.reshape(-1, a.shape[-1])


def _pack_small(arrs):
    rows = [jnp.pad(a.reshape(-1), (0, D_MODEL - a.size)) for a in arrs]
    rows += [jnp.zeros((D_MODEL,), F32)] * (8 - len(rows))
    return jnp.stack(rows)


def kernel(x, p, a_norm, a_w_in, a_w_group, a_scale, a_w_out, kv_norm, w_kv, k_norm, b_norm, b_w_in, b_q_norm, b_w_out, ple_w, ple_gate_w, loss_target, m_a_norm, m_a_w_in, m_a_w_group, m_a_scale, m_a_w_out, m_kv_norm, m_w_kv, m_k_norm, m_b_norm, m_b_w_in, m_b_q_norm, m_b_w_out, m_ple_w, m_ple_gate_w, v_a_norm, v_a_w_in, v_a_w_group, v_a_scale, v_a_w_out, v_kv_norm, v_w_kv, v_k_norm, v_b_norm, v_b_w_in, v_b_q_norm, v_b_w_out, v_ple_w, v_ple_gate_w):
    wts = dict(a_norm=a_norm, a_w_in=a_w_in, a_w_group=a_w_group, a_scale=a_scale, a_w_out=a_w_out, kv_norm=kv_norm,
               w_kv=w_kv, k_norm=k_norm, b_norm=b_norm, b_w_in=b_w_in, b_q_norm=b_q_norm, b_w_out=b_w_out,
               ple_w=ple_w, ple_gate_w=ple_gate_w)
    mom = dict(a_norm=m_a_norm, a_w_in=m_a_w_in, a_w_group=m_a_w_group, a_scale=m_a_scale, a_w_out=m_a_w_out,
               kv_norm=m_kv_norm, w_kv=m_w_kv, k_norm=m_k_norm, b_norm=m_b_norm, b_w_in=m_b_w_in,
               b_q_norm=m_b_q_norm, b_w_out=m_b_w_out, ple_w=m_ple_w, ple_gate_w=m_ple_gate_w)
    var = dict(a_norm=v_a_norm, a_w_in=v_a_w_in, a_w_group=v_a_w_group, a_scale=v_a_scale, a_w_out=v_a_w_out,
               kv_norm=v_kv_norm, w_kv=v_w_kv, k_norm=v_k_norm, b_norm=v_b_norm, b_w_in=v_b_w_in,
               b_q_norm=v_b_q_norm, b_w_out=v_b_w_out, ple_w=v_ple_w, ple_gate_w=v_ple_gate_w)
    S = x.shape[1]
    chip = 2 * lax.axis_index("x") + lax.axis_index("y")

    sharded_small = jnp.concatenate([a_norm.reshape(1, 256), a_scale.reshape(1, 256), jnp.zeros((6, 256), F32)], axis=0)
    later = ("a_w_group", "a_w_out", "w_kv", "b_w_in", "b_w_out", "ple_w", "ple_gate_w")
    (a_w_in_full,), small_full, copies = _allgather_weights(
        [_as_matrix(a_w_in)], sharded_small,
        [(_as_matrix(wts[n]), [(0, 256), (256, 512)] if n.startswith("ple") else [(0, _as_matrix(wts[n]).shape[0])])
         for n in later])
    local = dict(zip(("a_w_group", "a_w_out", "w_kv", "b_w_in", "b_w_out", "ple_w0", "ple_w1", "ple_gate_w0",
                      "ple_gate_w1"), copies))
    full = dict(a_w_in=a_w_in_full,
                a_norm=small_full[:, 0, :].reshape(1, D_MODEL), a_scale=small_full[:, 1, :].reshape(1, D_MODEL),
                kv_norm=kv_norm.reshape(1, D_MODEL), b_norm=b_norm.reshape(1, D_MODEL), k_norm=k_norm, b_q_norm=b_q_norm)

    def shards(t):
        out = {}
        for n in BIG:
            for entry in ((n + "0", n + "1") if n.startswith("ple") else (n,)):
                out[entry] = _as_matrix(t[n])
        return out

    base = {n: 256 if n.startswith("ple") and n.endswith("1") else 0 for n in shards(wts)}
    state = (shards(wts), shards(mom), shards(var), base)
    grad_x, grads, updates, small_part = _local_step(x.reshape(S, D_MODEL), p, loss_target.reshape(S, D_MODEL),
                                                     full, local, state)

    names = sorted(grads)
    reduced, small_sum = _reduce_adam_all(
        [grads[n] for n in names], *[[t[n] for n in names] for t in state[:3]], small_part,
        bases=[base[n] for n in names], seeds=[updates.get(n[:-1] + "1") if n.startswith("ple") else None for n in names])
    for i, n in enumerate(names):
        updates[n] = tuple(group[i] for group in reduced)
    out_g, out_d, out_m, out_v = {}, {}, {}, {}
    for n in BIG:
        for i, out in enumerate((out_g, out_d, out_m, out_v)):
            out[n] = updates[n + "0" if n.startswith("ple") else n][i].reshape(wts[n].shape)

    loss = small_sum[len(SMALL), 0]
    small_rows = []
    for i, n in enumerate(SMALL):
        row = small_sum[i]
        if n in SMALL_SHARDED:
            row = lax.dynamic_slice(row, (chip * 256,), (256,))
        else:
            row = row[:wts[n].size]
        small_rows.append(row)
    g_small = _pack_small(small_rows)
    d_small, m_small, v_small = _adam_small(_pack_small([wts[n] for n in SMALL]), g_small,
                                            _pack_small([mom[n] for n in SMALL]), _pack_small([var[n] for n in SMALL]))
    for i, n in enumerate(SMALL):
        shape, size = wts[n].shape, wts[n].size
        out_g[n], out_d[n], out_m[n], out_v[n] = (t[i, :size].reshape(shape) for t in (g_small, d_small, m_small, v_small))

    return (loss, grad_x.reshape(1, S, D_MODEL), *[out_g[n] for n in WEIGHTS], *[out_d[n] for n in WEIGHTS],
            *[out_m[n] for n in WEIGHTS], *[out_v[n] for n in WEIGHTS])
```
